```python
import math
import jax
import jax.numpy as jnp
from jax import lax
import numpy as np

D_MODEL = 2048
BATCH = 4
SEQ = 4096
DEPTH = 2

GRID_W = 64
CTX_LEN = 256
N_MOD = 6
N_BRANCH = 3
BRANCH_W = D_MODEL // 2
ML_HEADS = 4
ML_DV = BRANCH_W // ML_HEADS
ML_DQK = ML_DV // 2
ML_CHUNK = 64
MLA_HEADS = 8
MLA_Q_LORA = 512
MLA_KV_LORA = 512
MLA_NOPE = 128
MLA_ROPE = 64
MLA_DV = BRANCH_W // MLA_HEADS
MLA_DQK = MLA_NOPE + MLA_ROPE
ATTN_BLOCK = 128
ROPE_THETA = 10000.0
S5_WIDTH = BRANCH_W
S5_GROUP = 16
S5_GROUPS = S5_WIDTH // S5_GROUP
S5_STATE = 64
D_FF = 4 * D_MODEL
EPS = 1e-6
NEG_BIG = -1e30
IN_SIZES = (ML_HEADS * ML_DQK, ML_HEADS * ML_DQK, ML_HEADS * ML_DV, ML_HEADS * ML_DV, 4 * ML_HEADS, MLA_Q_LORA, MLA_KV_LORA, MLA_ROPE, S5_WIDTH, N_BRANCH * D_MODEL)
D_IN = sum(IN_SIZES)

kernel_name = 'hybrid_mlstm_mla_s5_dit_trunk'


def rms_norm(x, g):
    xf = x.astype(jnp.float32)
    y = xf * lax.rsqrt(jnp.mean(xf * xf, axis=-1, keepdims=True) + EPS)
    return (y * g.astype(jnp.float32)).astype(x.dtype)


def modulate(h, shift, scale):
    return h * (1.0 + scale) + shift


def split_cols(z):
    bounds = np.cumsum(IN_SIZES)[:-1].tolist()
    return jnp.split(z, bounds, axis=-1)


def flip_time(a, direction, axis):
    return jnp.flip(a, axis=axis) if direction == 1 else a


def axial_rope_tables(n_tokens):
    rows = n_tokens // GRID_W
    row = jnp.repeat(jnp.arange(rows, dtype=jnp.float32), GRID_W)
    col = jnp.tile(jnp.arange(GRID_W, dtype=jnp.float32), rows)
    n_freq = MLA_ROPE // 4
    inv_freq = ROPE_THETA ** (-jnp.arange(n_freq, dtype=jnp.float32) / n_freq)
    ang_r = row[:, None] * inv_freq
    ang_c = col[:, None] * inv_freq
    return (jnp.cos(ang_r), jnp.sin(ang_r), jnp.cos(ang_c), jnp.sin(ang_c))


def rotate_half_pairs(x, cos, sin):
    m = x.shape[-1] // 2
    x1, x2 = x[..., :m], x[..., m:]
    c, s = cos[:, None, :], sin[:, None, :]
    return jnp.concatenate([x1 * c - x2 * s, x2 * c + x1 * s], axis=-1)


def axial_rope(x, rope):
    cos_r, sin_r, cos_c, sin_c = rope
    xf = x.astype(jnp.float32)
    half = MLA_ROPE // 2
    out = jnp.concatenate([rotate_half_pairs(xf[..., :half], cos_r, sin_r), rotate_half_pairs(xf[..., half:], cos_c, sin_c)], axis=-1)
    return out.astype(x.dtype)


def mlstm_prep(q, k, v, gates, gate_b):
    b, t, _ = q.shape

    def heads(a, dh):
        return a.reshape(b, t, ML_HEADS, dh).transpose(0, 2, 1, 3).astype(jnp.float32)

    qh = heads(q, ML_DQK) * (ML_DQK ** -0.5)
    kh = heads(k, ML_DQK)
    vh = heads(v, ML_DV)
    g = (gates.reshape(b, t, 4, ML_HEADS).astype(jnp.float32) + gate_b.astype(jnp.float32)).transpose(2, 0, 3, 1)
    log_gates = ((g[0], jax.nn.log_sigmoid(g[1])), (g[2], jax.nn.log_sigmoid(g[3])))
    return qh, kh, vh, log_gates


def mlstm_chunkwise(q, k, v, log_i, log_f, state):
    b, h, t, _ = q.shape
    nc = t // ML_CHUNK

    def chunks(a):
        return jnp.moveaxis(a.reshape(a.shape[:2] + (nc, ML_CHUNK) + a.shape[3:]), 2, 0)

    causal = jnp.tril(jnp.ones((ML_CHUNK, ML_CHUNK), dtype=bool))

    def step(carry, inp):
        c_mat, n_vec, m = carry
        qc, kc, vc, lic, lfc = inp
        cum_f = jnp.cumsum(lfc, axis=-1)
        log_w = jnp.where(causal, cum_f[..., :, None] - cum_f[..., None, :] + lic[..., None, :], NEG_BIG)
        log_inter = cum_f + m[..., None]
        m_t = jnp.maximum(log_inter, jnp.max(log_w, axis=-1))
        w_inter = jnp.exp(log_inter - m_t)
        s = jnp.einsum('bhtd,bhsd->bhts', qc, kc) * jnp.exp(log_w - m_t[..., None])
        num = w_inter[..., None] * jnp.einsum('bhtd,bhdv->bhtv', qc, c_mat) + jnp.einsum('bhts,bhsv->bhtv', s, vc)
        den = w_inter * jnp.einsum('bhtd,bhd->bht', qc, n_vec) + jnp.sum(s, axis=-1)
        h_out = num / jnp.maximum(jnp.abs(den), jnp.exp(-m_t))[..., None]
        log_to_end = cum_f[..., -1:] - cum_f + lic
        m_new = jnp.maximum(cum_f[..., -1] + m, jnp.max(log_to_end, axis=-1))
        decay = jnp.exp(cum_f[..., -1] + m - m_new)
        w_end = jnp.exp(log_to_end - m_new[..., None])
        c_mat = decay[..., None, None] * c_mat + jnp.einsum('bhs,bhsd,bhsv->bhdv', w_end, kc, vc)
        n_vec = decay[..., None] * n_vec + jnp.einsum('bhs,bhsd->bhd', w_end, kc)
        return (c_mat, n_vec, m_new), h_out

    state, hs = lax.scan(step, state, (chunks(q), chunks(k), chunks(v), chunks(log_i), chunks(log_f)))
    hs = jnp.moveaxis(hs, 0, 2).reshape(b, h, t, v.shape[-1])
    return hs, state


def mlstm_bidir(ctx_in, lat_in):
    qc, kc, vc, gc = ctx_in
    qx, kx, vx, gx = lat_in
    b = qx.shape[0]
    h_ctx = jnp.zeros(qc.shape[:3] + (ML_DV,), jnp.float32)
    h_lat = jnp.zeros(qx.shape[:3] + (ML_DV,), jnp.float32)
    for d in range(2):
        state0 = (jnp.zeros((b, ML_HEADS, ML_DQK, ML_DV), jnp.float32), jnp.zeros((b, ML_HEADS, ML_DQK), jnp.float32), jnp.zeros((b, ML_HEADS), jnp.float32))
        hc, state_ctx = mlstm_chunkwise(flip_time(qc, d, 2), flip_time(kc, d, 2), flip_time(vc, d, 2), flip_time(gc[d][0], d, 2), flip_time(gc[d][1], d, 2), state0)
        hx, _ = mlstm_chunkwise(flip_time(qx, d, 2), flip_time(kx, d, 2), flip_time(vx, d, 2), flip_time(gx[d][0], d, 2), flip_time(gx[d][1], d, 2), state_ctx)
        h_ctx = h_ctx + flip_time(hc, d, 2)
        h_lat = h_lat + flip_time(hx, d, 2)
    return h_ctx, h_lat


def mlstm_out(h, o, norm_g):
    b, _, t, _ = h.shape
    hn = rms_norm(h.transpose(0, 2, 1, 3), norm_g).reshape(b, t, ML_HEADS * ML_DV)
    return (hn * jax.nn.sigmoid(o.astype(jnp.float32))).astype(o.dtype)


def mla_qkv(q_a, kv_a, k_pe, qa_g, kva_g, w_uq, w_ukv, qn_g, kn_g, rope):
    b, t, _ = q_a.shape
    q = (rms_norm(q_a, qa_g) @ w_uq).reshape(b, t, MLA_HEADS, MLA_DQK)
    kv = (rms_norm(kv_a, kva_g) @ w_ukv).reshape(b, t, MLA_HEADS, MLA_NOPE + MLA_DV)
    k_pe = jnp.broadcast_to(k_pe[:, :, None, :], (b, t, MLA_HEADS, MLA_ROPE))
    k = jnp.concatenate([kv[..., :MLA_NOPE], k_pe], axis=-1)
    v = kv[..., MLA_NOPE:]
    q = rms_norm(q, qn_g)
    k = rms_norm(k, kn_g)
    if rope is not None:
        q = jnp.concatenate([q[..., :MLA_NOPE], axial_rope(q[..., MLA_NOPE:], rope)], axis=-1)
        k = jnp.concatenate([k[..., :MLA_NOPE], axial_rope(k[..., MLA_NOPE:], rope)], axis=-1)
    return q, k, v


def attend(q, k, v):
    s = jnp.einsum('bqhd,bkhd->bhqk', q, k).astype(jnp.float32) * (MLA_DQK ** -0.5)
    p = jax.nn.softmax(s, axis=-1).astype(v.dtype)
    return jnp.einsum('bhqk,bkhd->bqhd', p, v)


def latent_attention(q, k_ctx, v_ctx, k_lat, v_lat):
    k = jnp.concatenate([k_ctx, k_lat], axis=1)
    v = jnp.concatenate([v_ctx, v_lat], axis=1)
    b, t, h, d = q.shape
    nb = t // ATTN_BLOCK
    qb = jnp.moveaxis(q.reshape(b, nb, ATTN_BLOCK, h, d), 1, 0)
    out = lax.map(lambda qi: attend(qi, k, v), qb)
    return jnp.moveaxis(out, 0, 1).reshape(b, t, MLA_HEADS * MLA_DV)


def s5_discretize(a_re, a_im, log_dt, b_re, b_im):
    lam = lax.complex(jnp.minimum(a_re.astype(jnp.float32), -1e-4), a_im.astype(jnp.float32))
    dt = jnp.exp(log_dt.astype(jnp.float32))[:, None]
    lam_bar = jnp.exp(lam * dt)
    b_bar = ((lam_bar - 1.0) / lam)[..., None] * lax.complex(b_re.astype(jnp.float32), b_im.astype(jnp.float32))
    return lam_bar, b_bar


def ssm_combine(e1, e2):
    a1, b1 = e1
    a2, b2 = e2
    return a2 * a1, a2 * b1 + b2


def s5_states(u, lam_bar, b_bar, x0):
    bu = jnp.einsum('gnc,btgc->btgn', b_bar, u)
    bu = bu.at[:, 0].add(lam_bar * x0)
    a = jnp.broadcast_to(lam_bar, bu.shape)
    _, xs = lax.associative_scan(ssm_combine, (a, bu), axis=1)
    return xs


def s5_readout(xs, c_mat):
    return jnp.einsum('gcn,btgn->btgc', c_mat, xs).real


def s5_mixer(u_c, u_x, a_re, a_im, log_dt, b_re, b_im, c_re, c_im, d_skip, w_glu, b_glu, with_ctx_out):
    def grouped(u):
        return u.astype(jnp.float32).reshape(u.shape[0], u.shape[1], S5_GROUPS, S5_GROUP)

    uc, ux = grouped(u_c), grouped(u_x)
    dg = d_skip.astype(jnp.float32).reshape(S5_GROUPS, S5_GROUP)
    yx = dg * ux
    yc = dg * uc if with_ctx_out else None
    for d in range(2):
        lam_bar, b_bar = s5_discretize(a_re[d], a_im[d], log_dt[d], b_re[d], b_im[d])
        c_mat = lax.complex(c_re[d].astype(jnp.float32), c_im[d].astype(jnp.float32))
        x0 = jnp.zeros((uc.shape[0], S5_GROUPS, S5_STATE), jnp.complex64)
        xs_c = s5_states(flip_time(uc, d, 1), lam_bar, b_bar, x0)
        xs_x = s5_states(flip_time(ux, d, 1), lam_bar, b_bar, xs_c[:, -1])
        yx = yx + flip_time(s5_readout(xs_x, c_mat), d, 1)
        if with_ctx_out:
            yc = yc + flip_time(s5_readout(xs_c, c_mat), d, 1)

    def glu(y):
        g = jax.nn.gelu(y.reshape(y.shape[0], y.shape[1], S5_WIDTH)).astype(u_x.dtype)
        return g * jax.nn.sigmoid(g @ w_glu + b_glu)

    return glu(yx), (glu(yc) if with_ctx_out else None)


def merge_branches(branches, gate_pre, w_branch, w_out):
    stacked = jnp.stack(branches, axis=-2)
    b, t = stacked.shape[0], stacked.shape[1]
    gates = jax.nn.sigmoid(gate_pre.reshape(b, t, N_BRANCH, D_MODEL).astype(jnp.float32)).astype(stacked.dtype)
    proj = jnp.einsum('btrc,rcd->btrd', stacked, w_branch)
    return jnp.sum(gates * proj, axis=-2) @ w_out


def sq_relu_mlp(h, w1, w2):
    return jnp.square(jax.nn.relu(h @ w1)) @ w2


def token_mixers(hx, hc, w_in, b_in, ml_gate_b, ml_norm_g, mla_qa_g, mla_kva_g, mla_w_uq, mla_w_ukv, mla_qn_g, mla_kn_g, s5_a_re, s5_a_im, s5_log_dt, s5_b_re, s5_b_im, s5_c_re, s5_c_im, s5_d, s5_w_glu, s5_b_glu, w_branch, w_out, rope, with_ctx_out):
    zx = split_cols(hx @ w_in + b_in)
    zc = split_cols(hc @ w_in + b_in)
    ml_c = mlstm_prep(zc[0], zc[1], zc[2], zc[4], ml_gate_b)
    ml_x = mlstm_prep(zx[0], zx[1], zx[2], zx[4], ml_gate_b)
    h_a_c, h_a_x = mlstm_bidir(ml_c, ml_x)
    a_x = mlstm_out(h_a_x, zx[3], ml_norm_g)
    q_c, k_c, v_c = mla_qkv(zc[5], zc[6], zc[7], mla_qa_g, mla_kva_g, mla_w_uq, mla_w_ukv, mla_qn_g, mla_kn_g, None)
    q_x, k_x, v_x = mla_qkv(zx[5], zx[6], zx[7], mla_qa_g, mla_kva_g, mla_w_uq, mla_w_ukv, mla_qn_g, mla_kn_g, rope)
    b_x = latent_attention(q_x, k_c, v_c, k_x, v_x)
    c_x, c_c = s5_mixer(zc[8], zx[8], s5_a_re, s5_a_im, s5_log_dt, s5_b_re, s5_b_im, s5_c_re, s5_c_im, s5_d, s5_w_glu, s5_b_glu, with_ctx_out)
    out_x = merge_branches((a_x, b_x, c_x), zx[9], w_branch, w_out)
    if not with_ctx_out:
        return out_x, None
    a_c = mlstm_out(h_a_c, zc[3], ml_norm_g)
    b_c = attend(q_c, k_c, v_c).reshape(q_c.shape[0], q_c.shape[1], MLA_HEADS * MLA_DV)
    out_c = merge_branches((a_c, b_c, c_c), zc[9], w_branch, w_out)
    return out_x, out_c


def setup_inputs(seed: int = 0) -> dict:
    key = jax.random.key(seed)
    ks = jax.random.split(key, 32)
    f32 = jnp.float32

    def nrm(k, shape, scale):
        return scale * jax.random.normal(k, shape, f32)

    L, D, H = DEPTH, D_MODEL, ML_HEADS
    G, N, GC = S5_GROUPS, S5_STATE, S5_GROUP
    f_bias = jnp.linspace(3.0, 6.0, H, dtype=f32)
    gate_base = jnp.stack([jnp.zeros((H,), f32), f_bias, jnp.zeros((H,), f32), f_bias])
    return {
        'x': nrm(ks[0], (BATCH, SEQ, D), 1.0),
        'c': nrm(ks[1], (BATCH, D), 1.0),
        'ctx': nrm(ks[2], (BATCH, CTX_LEN, D), 1.0),
        'c_ctx': nrm(ks[3], (D,), 1.0),
        'w_mod': nrm(ks[4], (L, D, N_MOD * D), 0.5 * D ** -0.5),
        'b_mod': nrm(ks[5], (L, N_MOD * D), 0.01),
        'norm_g': 1.0 + nrm(ks[6], (L, 2, D), 0.02),
        'w_in': nrm(ks[7], (L, D, D_IN), D ** -0.5),
        'b_in': nrm(ks[8], (L, D_IN), 0.01),
        'ml_gate_b': gate_base[None] + nrm(ks[9], (L, 4, H), 0.1),
        'ml_norm_g': 1.0 + nrm(ks[10], (L, H, ML_DV), 0.02),
        'mla_qa_g': 1.0 + nrm(ks[11], (L, MLA_Q_LORA), 0.02),
        'mla_kva_g': 1.0 + nrm(ks[12], (L, MLA_KV_LORA), 0.02),
        'mla_w_uq': nrm(ks[13], (L, MLA_Q_LORA, MLA_HEADS * MLA_DQK), MLA_Q_LORA ** -0.5),
        'mla_w_ukv': nrm(ks[14], (L, MLA_KV_LORA, MLA_HEADS * (MLA_NOPE + MLA_DV)), MLA_KV_LORA ** -0.5),
        'mla_qn_g': 1.0 + nrm(ks[15], (L, MLA_DQK), 0.02),
        'mla_kn_g': 1.0 + nrm(ks[16], (L, MLA_DQK), 0.02),
        's5_a_re': -0.5 + nrm(ks[17], (L, 2, G, N), 0.01),
        's5_a_im': math.pi * jnp.arange(N, dtype=f32) + nrm(ks[18], (L, 2, G, N), 0.01),
        's5_log_dt': jax.random.uniform(ks[19], (L, 2, G), f32, math.log(1e-3), math.log(1e-1)),
        's5_b_re': nrm(ks[20], (L, 2, G, N, GC), (2.0 * GC) ** -0.5),
        's5_b_im': nrm(ks[21], (L, 2, G, N, GC), (2.0 * GC) ** -0.5),
        's5_c_re': nrm(ks[22], (L, 2, G, GC, N), N ** -0.5),
        's5_c_im': nrm(ks[23], (L, 2, G, GC, N), N ** -0.5),
        's5_d': nrm(ks[24], (L, S5_WIDTH), 1.0),
        's5_w_glu': nrm(ks[25], (L, S5_WIDTH, S5_WIDTH), S5_WIDTH ** -0.5),
        's5_b_glu': nrm(ks[26], (L, S5_WIDTH), 0.01),
        'w_branch': nrm(ks[27], (L, N_BRANCH, BRANCH_W, D), BRANCH_W ** -0.5),
        'w_out': nrm(ks[28], (L, D, D), D ** -0.5),
        'w_ff1': nrm(ks[29], (L, D, D_FF), D ** -0.5),
        'w_ff2': nrm(ks[30], (L, D_FF, D), D_FF ** -0.5),
    }


def reference(x, c, ctx, c_ctx, w_mod, b_mod, norm_g, w_in, b_in, ml_gate_b, ml_norm_g, mla_qa_g, mla_kva_g, mla_w_uq, mla_w_ukv, mla_qn_g, mla_kn_g, s5_a_re, s5_a_im, s5_log_dt, s5_b_re, s5_b_im, s5_c_re, s5_c_im, s5_d, s5_w_glu, s5_b_glu, w_branch, w_out, w_ff1, w_ff2):
    batch = x.shape[0]
    rope = axial_rope_tables(x.shape[1])
    sc = jax.nn.silu(c)
    scc = jax.nn.silu(c_ctx)
    for l in range(DEPTH):
        with_ctx_out = l < DEPTH - 1
        mx = (sc @ w_mod[l] + b_mod[l]).reshape(batch, N_MOD, D_MODEL).transpose(1, 0, 2)[:, :, None, :]
        mc = (scc @ w_mod[l] + b_mod[l]).reshape(N_MOD, 1, 1, D_MODEL)
        hx = modulate(rms_norm(x, norm_g[l, 0]), mx[0], mx[1])
        hc = modulate(rms_norm(ctx, norm_g[l, 0]), mc[0], mc[1])
        out_x, out_c = token_mixers(hx, hc, w_in[l], b_in[l], ml_gate_b[l], ml_norm_g[l], mla_qa_g[l], mla_kva_g[l], mla_w_uq[l], mla_w_ukv[l], mla_qn_g[l], mla_kn_g[l], s5_a_re[l], s5_a_im[l], s5_log_dt[l], s5_b_re[l], s5_b_im[l], s5_c_re[l], s5_c_im[l], s5_d[l], s5_w_glu[l], s5_b_glu[l], w_branch[l], w_out[l], rope, with_ctx_out)
        x = x + mx[2] * out_x
        hx = modulate(rms_norm(x, norm_g[l, 1]), mx[3], mx[4])
        x = x + mx[5] * sq_relu_mlp(hx, w_ff1[l], w_ff2[l])
        if with_ctx_out:
            ctx = ctx + mc[2] * out_c
            hc = modulate(rms_norm(ctx, norm_g[l, 1]), mc[3], mc[4])
            ctx = ctx + mc[5] * sq_relu_mlp(hc, w_ff1[l], w_ff2[l])
    return x
```

```python
import functools
import math

import jax
import jax.numpy as jnp
import numpy as np
from jax import lax
from jax.experimental import pallas as pl
from jax.experimental.pallas import tpu as pltpu

F32 = jnp.float32
BF16 = jnp.bfloat16
HIGHEST = lax.Precision.HIGHEST

N_MOD = 6
N_BRANCH = 3
ML_HEADS = 4
MLA_HEADS = 8
MLA_NOPE = 128
MLA_ROPE = 64
MLA_DQK = MLA_NOPE + MLA_ROPE
MLA_SLAB = 256
GRID_W = 64
ROPE_THETA = 10000.0
S5_GROUP = 16
S5_STATE = 64
S5_BLOCK_GROUPS = 8
EPS = 1e-6
NEG_BIG = -1e30

LANES = 128
SUBLANES = 8
VMEM_LIMIT = 56 * 1024 * 1024

ML_CHUNK = 256
S5_CHUNK = 256
ROW_CHUNK = 64


def _cparams(sem):
    return pltpu.CompilerParams(dimension_semantics=sem, vmem_limit_bytes=VMEM_LIMIT)


def _row_tile(n_x_rows_per_batch, n_ctx_rows):
    tm = 1024
    while n_x_rows_per_batch % tm or n_ctx_rows % tm:
        tm //= 2
    return tm


def _mod_kernel(c_ref, w_ref, b_ref, o_ref):
    s = c_ref[...]
    s = s * jax.nn.sigmoid(s)
    o_ref[0] = jnp.dot(s.astype(BF16), w_ref[0].astype(BF16), preferred_element_type=F32) + b_ref[0]


def _modulation(cc, w_mod, b_mod):
    n_layers, d, n = w_mod.shape
    tn = 1024
    return pl.pallas_call(
        _mod_kernel,
        out_shape=jax.ShapeDtypeStruct((n_layers, SUBLANES, n), F32),
        grid=(n_layers, n // tn),
        in_specs=[
            pl.BlockSpec((SUBLANES, d), lambda l, j: (0, 0)),
            pl.BlockSpec((1, d, tn), lambda l, j: (l, 0, j)),
            pl.BlockSpec((1, 1, tn), lambda l, j: (l, 0, j)),
        ],
        out_specs=pl.BlockSpec((1, SUBLANES, tn), lambda l, j: (l, 0, j)),
        compiler_params=_cparams(("parallel", "parallel")),
        name="adaln_mod",
    )(cc, w_mod, b_mod.reshape(n_layers, 1, n))


def _norm_mod_rows(x_ref, g_ref, sh_ref, sc_ref, xn_ref):
    tm = x_ref.shape[0]
    g = g_ref[...]
    sc = 1.0 + sc_ref[0]
    sh = sh_ref[0]

    def body(r, carry):
        rows = pl.ds(pl.multiple_of(r * ROW_CHUNK, ROW_CHUNK), ROW_CHUNK)
        x = x_ref[rows, :]
        ms = jnp.mean(x * x, axis=-1, keepdims=True)
        y = x * lax.rsqrt(ms + EPS) * g
        xn_ref[rows, :] = (y * sc + sh).astype(BF16)
        return carry

    lax.fori_loop(0, tm // ROW_CHUNK, body, 0)


def _mod_row_map(n_x_tiles, tiles_per_batch, ctx_row, k):
    def index_map(i, j):
        r = jnp.where(i < n_x_tiles, i // tiles_per_batch, ctx_row)
        return (r * N_MOD + k, 0, 0)
    return index_map


def _in_kernel(x_ref, g_ref, sh_ref, sc_ref, w_ref, b_ref, wg_ref, bg_ref, z_ref, gz_ref, xn_ref):
    @pl.when(pl.program_id(1) == 0)
    def _():
        _norm_mod_rows(x_ref, g_ref, sh_ref, sc_ref, xn_ref)
        gz_ref[...] = jnp.dot(xn_ref[...], wg_ref[...], preferred_element_type=F32) + bg_ref[...]

    z_ref[...] = (jnp.dot(xn_ref[...], w_ref[...], preferred_element_type=F32) + b_ref[...]).astype(BF16)


def _in_proj(xs, g, mod3, w, b, wg, bg, tm, tn, n_x_tiles, tiles_per_batch, ctx_row):
    r, d = xs.shape
    nz = w.shape[1]
    return pl.pallas_call(
        _in_kernel,
        out_shape=(jax.ShapeDtypeStruct((r, nz), BF16), jax.ShapeDtypeStruct((r, LANES), F32)),
        grid=(r // tm, nz // tn),
        in_specs=[
            pl.BlockSpec((tm, d), lambda i, j: (i, 0)),
            pl.BlockSpec((1, d), lambda i, j: (0, 0)),
            pl.BlockSpec((1, 1, d), _mod_row_map(n_x_tiles, tiles_per_batch, ctx_row, 0)),
            pl.BlockSpec((1, 1, d), _mod_row_map(n_x_tiles, tiles_per_batch, ctx_row, 1)),
            pl.BlockSpec((d, tn), lambda i, j: (0, j)),
            pl.BlockSpec((1, tn), lambda i, j: (0, j)),
            pl.BlockSpec((d, LANES), lambda i, j: (0, 0)),
            pl.BlockSpec((1, LANES), lambda i, j: (0, 0)),
        ],
        out_specs=(
            pl.BlockSpec((tm, tn), lambda i, j: (i, j)),
            pl.BlockSpec((tm, LANES), lambda i, j: (i, 0)),
        ),
        scratch_shapes=[pltpu.VMEM((tm, d), BF16)],
        compiler_params=_cparams(("parallel", "arbitrary")),
        name="in_proj",
    )(xs, g, mod3, mod3, w, b, wg, bg)


def _log_sigmoid(x):
    return jnp.minimum(x, 0.0) - jnp.log1p(jnp.exp(-jnp.abs(x)))


def _ml_chunk(q, k, v, lg_c, lg_r, tri, tri_t, carry, reverse, scale):
    c_mat, n_vec, m = carry
    length = q.shape[0]
    gi = 2 if reverse else 0
    cum_all_c = jnp.dot(tri, lg_c, precision=HIGHEST, preferred_element_type=F32)
    cum_all_r = jnp.dot(lg_r, tri_t, precision=HIGHEST, preferred_element_type=F32)
    li_c = lg_c[:, gi:gi + 1]
    cum_c = cum_all_c[:, gi + 1:gi + 2]
    li_r = lg_r[gi:gi + 1, :]
    cum_r = cum_all_r[gi + 1:gi + 2, :]
    total = jnp.sum(lg_r[gi + 1:gi + 2, :], axis=-1, keepdims=True)

    t_idx = lax.broadcasted_iota(jnp.int32, (length, length), 0)
    s_idx = lax.broadcasted_iota(jnp.int32, (length, length), 1)
    keep = (s_idx >= t_idx) if reverse else (s_idx <= t_idx)
    log_w = jnp.where(keep, cum_c - cum_r + li_r, NEG_BIG)
    log_inter = cum_c + m
    m_t = jnp.maximum(log_inter, jnp.max(log_w, axis=-1, keepdims=True))
    w_inter = jnp.exp(log_inter - m_t) * scale
    qk = lax.dot_general(q, k, (((1,), (1,)), ((), ())), preferred_element_type=F32)
    s = qk * (jnp.exp(log_w - m_t) * scale)
    num = w_inter * jnp.dot(q, c_mat.astype(BF16), preferred_element_type=F32)
    num = num + jnp.dot(s.astype(BF16), v, preferred_element_type=F32)
    qn = jnp.sum(q.astype(F32) * n_vec, axis=-1, keepdims=True)
    den = w_inter * qn + jnp.sum(s, axis=-1, keepdims=True)
    h = num / jnp.maximum(jnp.abs(den), jnp.exp(-m_t))

    log_end_r = total - cum_r + li_r
    m_new = jnp.maximum(total + m, jnp.max(log_end_r, axis=-1, keepdims=True))
    decay = jnp.exp(total + m - m_new)
    w_end_c = jnp.exp(total - cum_c + li_c - m_new)
    kw = k.astype(F32) * w_end_c
    c_new = decay * c_mat + lax.dot_general(kw.astype(BF16), v, (((0,), (0,)), ((), ())),
                                            preferred_element_type=F32)
    n_new = decay * n_vec + jnp.sum(kw, axis=0, keepdims=True)
    return h, (c_new, n_new, m_new)


def _mlstm_kernel(with_ctx_out, qx, kx, vx, ox, qc, kc, vc, oc, gx, gc, tri_ref, ng_ref, *rest):
    if with_ctx_out:
        ax_ref, ac_ref, lgx_c, lgx_r, lgc_c, lgc_r, hx, hc = rest
    else:
        ax_ref, lgx_c, lgx_r, lgc_c, lgc_r, hx, hc = rest
        ac_ref = None
    head = pl.program_id(1)
    dk = qx.shape[1]
    scale = dk ** -0.5
    n_x_chunks = qx.shape[0] // ML_CHUNK
    n_c_chunks = qc.shape[0] // ML_CHUNK

    r_idx = lax.broadcasted_iota(jnp.int32, (LANES, LANES), 0)
    c_idx = lax.broadcasted_iota(jnp.int32, (LANES, LANES), 1)
    sel = jnp.where((r_idx == c_idx * ML_HEADS + head) & (c_idx < 4), 1.0, 0.0).astype(F32)
    r8 = lax.broadcasted_iota(jnp.int32, (SUBLANES, LANES), 0)
    c8 = lax.broadcasted_iota(jnp.int32, (SUBLANES, LANES), 1)
    sel_t = jnp.where((c8 == r8 * ML_HEADS + head) & (r8 < 4), 1.0, 0.0).astype(F32)

    def prep(g_ref, lg_c_ref, lg_r_ref):
        def body(i, carry):
            rows = pl.ds(pl.multiple_of(i * ML_CHUNK, ML_CHUNK), ML_CHUNK)
            g = g_ref[rows, :]
            gs = jnp.dot(g, sel, precision=HIGHEST, preferred_element_type=F32)
            col = lax.broadcasted_iota(jnp.int32, gs.shape, 1)
            lg_c_ref[rows, :] = jnp.where(col % 2 == 1, _log_sigmoid(gs), gs)
            gt = lax.dot_general(sel_t, g, (((1,), (1,)), ((), ())), precision=HIGHEST,
                                 preferred_element_type=F32)
            row = lax.broadcasted_iota(jnp.int32, gt.shape, 0)
            lg_r_ref[i] = jnp.where(row % 2 == 1, _log_sigmoid(gt), gt)
            return carry
        lax.fori_loop(0, g_ref.shape[0] // ML_CHUNK, body, 0)

    prep(gx, lgx_c, lgx_r)
    prep(gc, lgc_c, lgc_r)

    for d in range(2):
        reverse = d == 1
        tri = tri_ref[d]
        tri_t = tri_ref[2 + d]

        def step(q_ref, k_ref, v_ref, lg_c_ref, lg_r_ref, h_ref, ci, carry):
            rows = pl.ds(pl.multiple_of(ci * ML_CHUNK, ML_CHUNK), ML_CHUNK)
            h, carry = _ml_chunk(q_ref[rows, :], k_ref[rows, :], v_ref[rows, :], lg_c_ref[rows, :],
                                 lg_r_ref[ci], tri, tri_t, carry, reverse, scale)
            if reverse:
                h_ref[rows, :] += h
            else:
                h_ref[rows, :] = h
            return carry

        carry = (jnp.zeros((dk, vx.shape[1]), F32), jnp.zeros((1, dk), F32), jnp.zeros((1, 1), F32))

        def ctx_body(i, carry):
            ci = (n_c_chunks - 1 - i) if reverse else i
            return step(qc, kc, vc, lgc_c, lgc_r, hc, ci, carry)

        def x_body(i, carry):
            ci = (n_x_chunks - 1 - i) if reverse else i
            return step(qx, kx, vx, lgx_c, lgx_r, hx, ci, carry)

        carry = lax.fori_loop(0, n_c_chunks, ctx_body, carry)
        lax.fori_loop(0, n_x_chunks, x_body, carry)

    ng = ng_ref[0]

    def finish(h_ref, o_ref, a_ref):
        def body(i, carry):
            rows = pl.ds(pl.multiple_of(i * ML_CHUNK, ML_CHUNK), ML_CHUNK)
            h = h_ref[rows, :]
            hn = h * lax.rsqrt(jnp.mean(h * h, axis=-1, keepdims=True) + EPS) * ng
            a_ref[rows, :] = (hn * jax.nn.sigmoid(o_ref[rows, :].astype(F32))).astype(BF16)
            return carry
        lax.fori_loop(0, h_ref.shape[0] // ML_CHUNK, body, 0)

    finish(hx, ox, ax_ref)
    if with_ctx_out:
        finish(hc, oc, ac_ref)


def _mlstm(z, gz, tri, ml_norm_g, b, s, tc, dk, dv, cols, with_ctx_out):
    n_x = b * s
    cq, ck, cv, co = cols
    ctx0 = n_x // tc

    def xspec(width, col0):
        return pl.BlockSpec((s, width), lambda i, h: (i, col0 // width + h))

    def cspec(width, col0):
        return pl.BlockSpec((tc, width), lambda i, h: (ctx0 + i, col0 // width + h))

    out_shape = [jax.ShapeDtypeStruct((n_x, ML_HEADS * dv), BF16)]
    out_specs = [pl.BlockSpec((s, dv), lambda i, h: (i, h))]
    if with_ctx_out:
        out_shape.append(jax.ShapeDtypeStruct((b * tc, ML_HEADS * dv), BF16))
        out_specs.append(pl.BlockSpec((tc, dv), lambda i, h: (i, h)))
    res = pl.pallas_call(
        functools.partial(_mlstm_kernel, with_ctx_out),
        out_shape=tuple(out_shape),
        grid=(b, ML_HEADS),
        in_specs=[
            xspec(dk, cq), xspec(dk, ck), xspec(dv, cv), xspec(dv, co),
            cspec(dk, cq), cspec(dk, ck), cspec(dv, cv), cspec(dv, co),
            pl.BlockSpec((s, LANES), lambda i, h: (i, 0)),
            pl.BlockSpec((tc, LANES), lambda i, h: (ctx0 + i, 0)),
            pl.BlockSpec((4, ML_CHUNK, ML_CHUNK), lambda i, h: (0, 0, 0)),
            pl.BlockSpec((1, 1, dv), lambda i, h: (h, 0, 0)),
        ],
        out_specs=tuple(out_specs),
        scratch_shapes=[
            pltpu.VMEM((s, LANES), F32), pltpu.VMEM((s // ML_CHUNK, SUBLANES, ML_CHUNK), F32),
            pltpu.VMEM((tc, LANES), F32), pltpu.VMEM((tc // ML_CHUNK, SUBLANES, ML_CHUNK), F32),
            pltpu.VMEM((s, dv), F32), pltpu.VMEM((tc, dv), F32),
        ],
        compiler_params=_cparams(("parallel", "parallel")),
        name="mlstm",
    )(z, z, z, z, z, z, z, z, gz, gz, tri, ml_norm_g.reshape(ML_HEADS, 1, dv))
    return res if with_ctx_out else (res[0], None)


def _mla_proj_kernel(qa_ref, kva_ref, kpe_ref, tab_ref, qag_ref, kvag_ref, wq_ref, wkv_ref,
                     gq_ref, gk_ref, q_ref, k_ref, v_ref):
    def normed(a_ref, g_ref):
        a = a_ref[...].astype(F32)
        return (a * lax.rsqrt(jnp.mean(a * a, axis=-1, keepdims=True) + EPS) * g_ref[...]).astype(BF16)

    q_all = jnp.dot(normed(qa_ref, qag_ref), wq_ref[...], preferred_element_type=F32)
    kv_all = jnp.dot(normed(kva_ref, kvag_ref), wkv_ref[...], preferred_element_type=F32)
    tab = tab_ref[...]
    lane = lax.broadcasted_iota(jnp.int32, tab.shape, 1)
    first_half = lane < MLA_ROPE
    gq = gq_ref[...]
    gk = gk_ref[...]
    inv_dqk = 1.0 / MLA_DQK

    kpe = kpe_ref[...].astype(F32)
    ss_kpe = jnp.sum(jnp.where(first_half, kpe * kpe, 0.0), axis=-1, keepdims=True)
    kpe_t = kpe * (tab * gk[:, LANES:])
    kpe_rot = jnp.where(first_half, kpe_t + pltpu.roll(kpe_t, MLA_ROPE, axis=1), 0.0)

    for h in range(MLA_HEADS):
        qn = q_all[:, h * MLA_SLAB:h * MLA_SLAB + LANES]
        qp = q_all[:, h * MLA_SLAB + LANES:(h + 1) * MLA_SLAB]
        ss = jnp.sum(qn * qn, axis=-1, keepdims=True) + jnp.sum(jnp.where(first_half, qp * qp, 0.0), axis=-1,
                                                                keepdims=True)
        r = lax.rsqrt(ss * inv_dqk + EPS)
        qp_t = qp * (tab * gq[:, LANES:])
        qp_rot = qp_t + pltpu.roll(qp_t, MLA_ROPE, axis=1)
        q_ref[:, h * MLA_SLAB:h * MLA_SLAB + LANES] = (qn * r * gq[:, :LANES]).astype(BF16)
        q_ref[:, h * MLA_SLAB + LANES:(h + 1) * MLA_SLAB] = (qp_rot * r).astype(BF16)

        kn = kv_all[:, h * LANES:(h + 1) * LANES]
        rk = lax.rsqrt((jnp.sum(kn * kn, axis=-1, keepdims=True) + ss_kpe) * inv_dqk + EPS)
        k_ref[:, h * MLA_SLAB:h * MLA_SLAB + LANES] = (kn * rk * gk[:, :LANES]).astype(BF16)
        k_ref[:, h * MLA_SLAB + LANES:(h + 1) * MLA_SLAB] = (kpe_rot * rk).astype(BF16)

    v_ref[...] = kv_all[:, MLA_HEADS * LANES:].astype(BF16)


def _mla_proj(z, tab, qag, kvag, wq, wkv, gq, gk, tm, cols, n_x_tiles, tab_tiles):
    r = z.shape[0]
    cqa, ckva, ckpe = cols
    lora = qag.shape[1]
    hs = MLA_HEADS * MLA_SLAB
    hv = MLA_HEADS * LANES
    return pl.pallas_call(
        _mla_proj_kernel,
        out_shape=(jax.ShapeDtypeStruct((r, hs), BF16), jax.ShapeDtypeStruct((r, hs), BF16),
                   jax.ShapeDtypeStruct((r, hv), BF16)),
        grid=(r // tm,),
        in_specs=[
            pl.BlockSpec((tm, lora), lambda i: (i, cqa // lora)),
            pl.BlockSpec((tm, lora), lambda i: (i, ckva // lora)),
            pl.BlockSpec((tm, LANES), lambda i: (i, ckpe // LANES)),
            pl.BlockSpec((tm, LANES), lambda i: (jnp.where(i < n_x_tiles, i % tab_tiles, tab_tiles), 0)),
            pl.BlockSpec((1, lora), lambda i: (0, 0)),
            pl.BlockSpec((1, lora), lambda i: (0, 0)),
            pl.BlockSpec((lora, hs), lambda i: (0, 0)),
            pl.BlockSpec((lora, 2 * hv), lambda i: (0, 0)),
            pl.BlockSpec((1, MLA_SLAB), lambda i: (0, 0)),
            pl.BlockSpec((1, MLA_SLAB), lambda i: (0, 0)),
        ],
        out_specs=(pl.BlockSpec((tm, hs), lambda i: (i, 0)), pl.BlockSpec((tm, hs), lambda i: (i, 0)),
                   pl.BlockSpec((tm, hv), lambda i: (i, 0))),
        compiler_params=_cparams(("parallel",)),
        name="mla_qkv",
    )(z, z, z, tab, qag, kvag, wq, wkv, gq, gk)


def _attn_kernel(n_kv, q_ref, *refs):
    k_refs = refs[:n_kv]
    v_refs = refs[n_kv:2 * n_kv]
    o_ref = refs[2 * n_kv]
    q = q_ref[...]
    scale = MLA_DQK ** -0.5
    scores = [lax.dot_general(q, k_ref[...], (((1,), (1,)), ((), ())), preferred_element_type=F32) * scale
              for k_ref in k_refs]
    m = functools.reduce(jnp.maximum, [jnp.max(s, axis=-1, keepdims=True) for s in scores])
    acc = None
    l_sum = None
    for s, v_ref in zip(scores, v_refs):
        p = jnp.exp(s - m)
        pv = jnp.dot(p.astype(BF16), v_ref[...], preferred_element_type=F32)
        ps = jnp.sum(p, axis=-1, keepdims=True)
        acc = pv if acc is None else acc + pv
        l_sum = ps if l_sum is None else l_sum + ps
    o_ref[...] = (acc / l_sum).astype(BF16)


def _attention_x(qo, ko, vo, b, s, tc, tq):
    n_x = b * s
    ctx0 = n_x // tc
    qt = s // tq
    return pl.pallas_call(
        functools.partial(_attn_kernel, 2),
        out_shape=jax.ShapeDtypeStruct((n_x, MLA_HEADS * LANES), BF16),
        grid=(b, MLA_HEADS, qt),
        in_specs=[
            pl.BlockSpec((tq, MLA_SLAB), lambda i, h, t: (i * qt + t, h)),
            pl.BlockSpec((tc, MLA_SLAB), lambda i, h, t: (ctx0 + i, h)),
            pl.BlockSpec((s, MLA_SLAB), lambda i, h, t: (i, h)),
            pl.BlockSpec((tc, LANES), lambda i, h, t: (ctx0 + i, h)),
            pl.BlockSpec((s, LANES), lambda i, h, t: (i, h)),
        ],
        out_specs=pl.BlockSpec((tq, LANES), lambda i, h, t: (i * qt + t, h)),
        compiler_params=_cparams(("parallel", "parallel", "arbitrary")),
        name="attn_latent",
    )(qo, ko, ko, vo, vo)


def _attention_ctx(qo, ko, vo, b, s, tc):
    ctx0 = b * s // tc
    return pl.pallas_call(
        functools.partial(_attn_kernel, 1),
        out_shape=jax.ShapeDtypeStruct((b * tc, MLA_HEADS * LANES), BF16),
        grid=(b, MLA_HEADS),
        in_specs=[
            pl.BlockSpec((tc, MLA_SLAB), lambda i, h: (ctx0 + i, h)),
            pl.BlockSpec((tc, MLA_SLAB), lambda i, h: (ctx0 + i, h)),
            pl.BlockSpec((tc, LANES), lambda i, h: (ctx0 + i, h)),
        ],
        out_specs=pl.BlockSpec((tc, LANES), lambda i, h: (i, h)),
        compiler_params=_cparams(("parallel", "parallel")),
        name="attn_ctx",
    )(qo, ko, vo)


S5_HALF = S5_BLOCK_GROUPS * S5_STATE


def _s5_scan_rows(bu_ref, tab_ref, carry, reverse):
    n_groups = bu_ref.shape[0] // SUBLANES
    last = 0 if reverse else SUBLANES - 1

    def cmul_add(ar, ai, cr, ci, xr, xi):
        return ar + cr * xr - ci * xi, ai + cr * xi + ci * xr

    def body(i, carry):
        cre, cim = carry
        gi = (n_groups - 1 - i) if reverse else i
        rows = pl.ds(pl.multiple_of(gi * SUBLANES, SUBLANES), SUBLANES)
        re = bu_ref[rows, :S5_HALF]
        im = bu_ref[rows, S5_HALF:]
        for lvl, shift in enumerate((1, 2, 4)):
            sh = (SUBLANES - shift) if reverse else shift
            re, im = cmul_add(re, im, tab_ref[2 * lvl], tab_ref[2 * lvl + 1],
                              pltpu.roll(re, sh, axis=0), pltpu.roll(im, sh, axis=0))
        re, im = cmul_add(re, im, tab_ref[6], tab_ref[7], cre, cim)
        bu_ref[rows, :S5_HALF] = re
        bu_ref[rows, S5_HALF:] = im
        return (jnp.broadcast_to(re[last:last + 1, :], re.shape), jnp.broadcast_to(im[last:last + 1, :], im.shape))

    return lax.fori_loop(0, n_groups, body, carry)


def _s5_kernel(n_batch, s, tc, u_ref, bmat_ref, cmat_ref, tab_ref, y_ref, bu_ref, st_ref):
    d = pl.program_id(1)
    c = pl.program_id(2)
    n_x_chunks = s // S5_CHUNK
    n_c_chunks = tc // S5_CHUNK

    def run(reverse):
        @pl.when(c == 0)
        def _():
            st_ref[...] = jnp.zeros_like(st_ref)

        cx = c - n_c_chunks
        ci_c = (n_c_chunks - 1 - c) if reverse else c
        ci_x = (n_x_chunks - 1 - cx) if reverse else cx

        def one_batch(bi, carry):
            row0 = jnp.where(c < n_c_chunks, n_batch * s + bi * tc + ci_c * S5_CHUNK, bi * s + ci_x * S5_CHUNK)
            rows = pl.ds(pl.multiple_of(row0, S5_CHUNK), S5_CHUNK)
            bu_ref[...] = jnp.dot(u_ref[rows, :], bmat_ref[0, 0], preferred_element_type=F32)
            state = (st_ref[bi, :, :S5_HALF], st_ref[bi, :, S5_HALF:])
            cre, cim = _s5_scan_rows(bu_ref, tab_ref.at[0, 0], state, reverse)
            st_ref[bi, :, :S5_HALF] = cre
            st_ref[bi, :, S5_HALF:] = cim
            y = jnp.dot(bu_ref[...].astype(BF16), cmat_ref[0, 0], preferred_element_type=F32)
            if reverse:
                y_ref[rows, :] += y
            else:
                y_ref[rows, :] = y
            return carry

        lax.fori_loop(0, n_batch, one_batch, 0)

    @pl.when(d == 0)
    def _():
        run(False)

    @pl.when(d == 1)
    def _():
        run(True)


def _s5_scan(z, bmat, cmat, tabs, b, s, tc, col_u):
    r = z.shape[0]
    n_blocks = bmat.shape[1]
    n_chunks = (s + tc) // S5_CHUNK
    return pl.pallas_call(
        functools.partial(_s5_kernel, b, s, tc),
        out_shape=jax.ShapeDtypeStruct((r, n_blocks * LANES), F32),
        grid=(n_blocks, 2, n_chunks),
        in_specs=[
            pl.BlockSpec((r, LANES), lambda cb, d, c: (0, col_u // LANES + cb)),
            pl.BlockSpec((1, 1, LANES, 2 * S5_HALF), lambda cb, d, c: (d, cb, 0, 0)),
            pl.BlockSpec((1, 1, 2 * S5_HALF, LANES), lambda cb, d, c: (d, cb, 0, 0)),
            pl.BlockSpec((1, 1, 8, SUBLANES, S5_HALF), lambda cb, d, c: (d, cb, 0, 0, 0)),
        ],
        out_specs=pl.BlockSpec((r, LANES), lambda cb, d, c: (0, cb)),
        scratch_shapes=[pltpu.VMEM((S5_CHUNK, 2 * S5_HALF), F32), pltpu.VMEM((b, SUBLANES, 2 * S5_HALF), F32)],
        compiler_params=_cparams(("parallel", "arbitrary", "arbitrary")),
        name="s5_scan",
    )(z, bmat, cmat, tabs)


def _glu_kernel(y_ref, u_ref, d_ref, w_ref, b_ref, o_ref, g_ref):
    tm = y_ref.shape[0]

    def body(r, carry):
        rows = pl.ds(pl.multiple_of(r * ROW_CHUNK, ROW_CHUNK), ROW_CHUNK)
        y = y_ref[rows, :] + d_ref[...] * u_ref[rows, :].astype(F32)
        g_ref[rows, :] = jax.nn.gelu(y).astype(BF16)
        return carry

    lax.fori_loop(0, tm // ROW_CHUNK, body, 0)
    g = g_ref[...]
    gate = jax.nn.sigmoid(jnp.dot(g, w_ref[...], preferred_element_type=F32) + b_ref[...])
    o_ref[...] = (g.astype(F32) * gate).astype(BF16)


def _s5_glu(y, z, d_skip, w_glu, b_glu, tm, col_u, n_row_tiles):
    width = y.shape[1]
    return pl.pallas_call(
        _glu_kernel,
        out_shape=jax.ShapeDtypeStruct((n_row_tiles * tm, width), BF16),
        grid=(n_row_tiles,),
        in_specs=[
            pl.BlockSpec((tm, width), lambda i: (i, 0)),
            pl.BlockSpec((tm, width), lambda i: (i, col_u // width)),
            pl.BlockSpec((1, width), lambda i: (0, 0)),
            pl.BlockSpec((width, width), lambda i: (0, 0)),
            pl.BlockSpec((1, width), lambda i: (0, 0)),
        ],
        out_specs=pl.BlockSpec((tm, width), lambda i: (i, 0)),
        scratch_shapes=[pltpu.VMEM((tm, width), BF16)],
        compiler_params=_cparams(("parallel",)),
        name="s5_glu",
    )(y, z, d_skip, w_glu, b_glu)


def _merge_kernel(a_ref, b_ref, c_ref, ga_ref, gb_ref, gc_ref, w_ref, o_ref):
    acc = None
    for r, (br_ref, gate_ref) in enumerate(((a_ref, ga_ref), (b_ref, gb_ref), (c_ref, gc_ref))):
        proj = jnp.dot(br_ref[...], w_ref[r], preferred_element_type=F32)
        term = jax.nn.sigmoid(gate_ref[...].astype(F32)) * proj
        acc = term if acc is None else acc + term
    o_ref[...] = acc.astype(BF16)


def _merge(a, bb, cc, z, w_branch, tm, tn, col_g, n_row_tiles):
    width = a.shape[1]
    d = w_branch.shape[2]

    def gate_spec(r):
        return pl.BlockSpec((tm, tn), lambda i, j: (i, (col_g + r * d) // tn + j))

    return pl.pallas_call(
        _merge_kernel,
        out_shape=jax.ShapeDtypeStruct((n_row_tiles * tm, d), BF16),
        grid=(n_row_tiles, d // tn),
        in_specs=[
            pl.BlockSpec((tm, width), lambda i, j: (i, 0)),
            pl.BlockSpec((tm, width), lambda i, j: (i, 0)),
            pl.BlockSpec((tm, width), lambda i, j: (i, 0)),
            gate_spec(0), gate_spec(1), gate_spec(2),
            pl.BlockSpec((N_BRANCH, width, tn), lambda i, j: (0, 0, j)),
        ],
        out_specs=pl.BlockSpec((tm, tn), lambda i, j: (i, j)),
        compiler_params=_cparams(("parallel", "arbitrary")),
        name="merge",
    )(a, bb, cc, z, z, z, w_branch)


def _resid_kernel(m_ref, w_ref, x_ref, al_ref, o_ref):
    o_ref[...] = x_ref[...] + al_ref[0] * jnp.dot(m_ref[...], w_ref[...], preferred_element_type=F32)


def _out_proj_residual(m, w, xs, mod3, tm, tn, n_row_tiles, n_x_tiles, tiles_per_batch, ctx_row, k_alpha):
    kdim, d = w.shape
    nt = d // tn

    def alpha_map(i, j):
        r = jnp.where(i < n_x_tiles, i // tiles_per_batch, ctx_row)
        return (r * N_MOD + k_alpha, 0, j)

    return pl.pallas_call(
        _resid_kernel,
        out_shape=jax.ShapeDtypeStruct((n_row_tiles * tm, d), F32),
        grid=(n_row_tiles, nt),
        in_specs=[
            pl.BlockSpec((tm, kdim), lambda i, j: (i, 0)),
            pl.BlockSpec((kdim, tn), lambda i, j: (0, j)),
            pl.BlockSpec((tm, tn), lambda i, j: (i, j)),
            pl.BlockSpec((1, 1, tn), alpha_map),
        ],
        out_specs=pl.BlockSpec((tm, tn), lambda i, j: (i, j)),
        compiler_params=_cparams(("parallel", "arbitrary")),
        name="out_proj",
    )(m, w, xs, mod3)


def _ff1_kernel(x_ref, g_ref, sh_ref, sc_ref, w_ref, h_ref, xn_ref):
    @pl.when(pl.program_id(1) == 0)
    def _():
        _norm_mod_rows(x_ref, g_ref, sh_ref, sc_ref, xn_ref)

    a = jnp.maximum(jnp.dot(xn_ref[...], w_ref[...], preferred_element_type=F32), 0.0)
    h_ref[...] = (a * a).astype(BF16)


def _ff1(xs, g, mod3, w, tm, tn, n_row_tiles, n_x_tiles, tiles_per_batch, ctx_row):
    d, dff = w.shape
    return pl.pallas_call(
        _ff1_kernel,
        out_shape=jax.ShapeDtypeStruct((n_row_tiles * tm, dff), BF16),
        grid=(n_row_tiles, dff // tn),
        in_specs=[
            pl.BlockSpec((tm, d), lambda i, j: (i, 0)),
            pl.BlockSpec((1, d), lambda i, j: (0, 0)),
            pl.BlockSpec((1, 1, d), _mod_row_map(n_x_tiles, tiles_per_batch, ctx_row, 3)),
            pl.BlockSpec((1, 1, d), _mod_row_map(n_x_tiles, tiles_per_batch, ctx_row, 4)),
            pl.BlockSpec((d, tn), lambda i, j: (0, j)),
        ],
        out_specs=pl.BlockSpec((tm, tn), lambda i, j: (i, j)),
        scratch_shapes=[pltpu.VMEM((tm, d), BF16)],
        compiler_params=_cparams(("parallel", "arbitrary")),
        name="ff1",
    )(xs, g, mod3, mod3, w)


def _ff2_kernel(h_ref, w_ref, x_ref, al_ref, o_ref):
    k = pl.program_id(2)
    part = jnp.dot(h_ref[...], w_ref[...], preferred_element_type=F32)

    @pl.when(k == 0)
    def _():
        o_ref[...] = part

    @pl.when(k > 0)
    def _():
        o_ref[...] += part

    @pl.when(k == pl.num_programs(2) - 1)
    def _():
        o_ref[...] = x_ref[...] + al_ref[0] * o_ref[...]


def _ff2(h, w, xs, mod3, tm, tn, tk, n_row_tiles, n_x_tiles, tiles_per_batch, ctx_row):
    dff, d = w.shape

    def alpha_map(i, j, k):
        r = jnp.where(i < n_x_tiles, i // tiles_per_batch, ctx_row)
        return (r * N_MOD + 5, 0, j)

    return pl.pallas_call(
        _ff2_kernel,
        out_shape=jax.ShapeDtypeStruct((n_row_tiles * tm, d), F32),
        grid=(n_row_tiles, d // tn, dff // tk),
        in_specs=[
            pl.BlockSpec((tm, tk), lambda i, j, k: (i, k)),
            pl.BlockSpec((tk, tn), lambda i, j, k: (k, j)),
            pl.BlockSpec((tm, tn), lambda i, j, k: (i, j)),
            pl.BlockSpec((1, 1, tn), alpha_map),
        ],
        out_specs=pl.BlockSpec((tm, tn), lambda i, j, k: (i, j)),
        compiler_params=_cparams(("parallel", "parallel", "arbitrary")),
        name="ff2",
    )(h, w, xs, mod3)


def _rope_partner():
    j = np.arange(MLA_ROPE)
    quarter = MLA_ROPE // 4
    return np.where((j // quarter) % 2 == 0, j + quarter, j - quarter)


def _rope_table(s, tm):
    pos = jnp.arange(s)
    row = (pos // GRID_W).astype(F32)
    col = (pos % GRID_W).astype(F32)
    n_freq = MLA_ROPE // 4
    inv_freq = ROPE_THETA ** (-jnp.arange(n_freq, dtype=F32) / n_freq)
    ang_r = row[:, None] * inv_freq
    ang_c = col[:, None] * inv_freq
    cos = jnp.concatenate([jnp.cos(ang_r)] * 2 + [jnp.cos(ang_c)] * 2, axis=-1)
    sin = jnp.concatenate([-jnp.sin(ang_r), jnp.sin(ang_r), -jnp.sin(ang_c), jnp.sin(ang_c)], axis=-1)
    ident = jnp.concatenate([jnp.ones((tm, MLA_ROPE), F32), jnp.zeros((tm, MLA_ROPE), F32)], axis=-1)
    return jnp.concatenate([jnp.concatenate([cos, sin], axis=-1), ident], axis=0)


def _pack_w_in(w_in, b_in, gate_b, sizes, n_pad):
    bounds = np.cumsum((0,) + sizes)
    seg = [slice(int(bounds[i]), int(bounds[i + 1])) for i in range(len(sizes))]
    partner = _rope_partner()
    order = (0, 1, 2, 3, 5, 6, 8, 9)
    w_parts = [w_in[:, seg[i]] for i in order]
    b_parts = [b_in[seg[i]] for i in order]
    kpe_w = w_in[:, seg[7]]
    kpe_b = b_in[seg[7]]
    w_parts += [kpe_w, kpe_w[:, partner]]
    b_parts += [kpe_b, kpe_b[partner]]
    offs = {}
    pos = 0
    for name, part in zip(("q", "k", "v", "o", "qa", "kva", "u", "gates", "kpe", "kpe_sw"), w_parts):
        offs[name] = pos
        pos += part.shape[1]
    pad = n_pad - pos
    w_parts.append(jnp.zeros((w_in.shape[0], pad), w_in.dtype))
    b_parts.append(jnp.zeros((pad,), b_in.dtype))
    w = jnp.concatenate(w_parts, axis=1).astype(BF16)
    b = jnp.concatenate(b_parts)[None, :]
    n_g = sizes[4]
    wg = jnp.concatenate([w_in[:, seg[4]], jnp.zeros((w_in.shape[0], LANES - n_g), w_in.dtype)], axis=1).astype(BF16)
    bg = jnp.concatenate([b_in[seg[4]] + gate_b.reshape(-1), jnp.zeros((LANES - n_g,), F32)])[None, :]
    return w, b, wg, bg, offs


def _pack_mla(w_uq, w_ukv, qn_g, kn_g):
    partner = _rope_partner()
    lora = w_uq.shape[0]
    wq = w_uq.reshape(lora, MLA_HEADS, MLA_DQK)
    wq = jnp.concatenate([wq, wq[:, :, MLA_NOPE + partner]], axis=-1).reshape(lora, MLA_HEADS * MLA_SLAB)
    wkv = w_ukv.reshape(w_ukv.shape[0], MLA_HEADS, -1)
    wkv = jnp.concatenate([wkv[:, :, :MLA_NOPE].reshape(lora, -1), wkv[:, :, MLA_NOPE:].reshape(lora, -1)], axis=-1)

    def gains(g):
        return jnp.concatenate([g, g[MLA_NOPE + partner]])[None, :]

    return wq.astype(BF16), wkv.astype(BF16), gains(qn_g), gains(kn_g)


def _pack_s5(a_re, a_im, log_dt, b_re, b_im, c_re, c_im):
    n_dir, n_groups, n_state = a_re.shape
    gc = b_re.shape[-1]
    nb = n_groups // S5_BLOCK_GROUPS
    lam_re = jnp.minimum(a_re.astype(F32), -1e-4)
    lam_im = a_im.astype(F32)
    dt = jnp.exp(log_dt.astype(F32))[..., None]

    def pole_power(k):
        mag = jnp.exp(k * lam_re * dt)
        return mag * jnp.cos(k * lam_im * dt), mag * jnp.sin(k * lam_im * dt)

    bar_re, bar_im = pole_power(1.0)
    den = lam_re * lam_re + lam_im * lam_im
    f_re = ((bar_re - 1.0) * lam_re + bar_im * lam_im) / den
    f_im = (bar_im * lam_re - (bar_re - 1.0) * lam_im) / den
    bb_re = f_re[..., None] * b_re.astype(F32) - f_im[..., None] * b_im.astype(F32)
    bb_im = f_re[..., None] * b_im.astype(F32) + f_im[..., None] * b_re.astype(F32)
    eye = jnp.eye(S5_BLOCK_GROUPS, dtype=F32)

    def block_b(part):
        p = part.reshape(n_dir, nb, S5_BLOCK_GROUPS, n_state, gc)
        m = jnp.einsum('dbgnc,gh->dbgchn', p, eye)
        return m.reshape(n_dir, nb, S5_BLOCK_GROUPS * gc, S5_BLOCK_GROUPS * n_state)

    bmat = jnp.concatenate([block_b(bb_re), block_b(bb_im)], axis=-1).astype(BF16)

    def block_c(part):
        p = part.reshape(n_dir, nb, S5_BLOCK_GROUPS, gc, n_state)
        m = jnp.einsum('dbgcn,gh->dbgnhc', p, eye)
        return m.reshape(n_dir, nb, S5_BLOCK_GROUPS * n_state, S5_BLOCK_GROUPS * gc)

    cmat = jnp.concatenate([block_c(c_re.astype(F32)), block_c(-c_im.astype(F32))], axis=-2).astype(BF16)

    def per_lane(a):
        return a.reshape(n_dir, nb, 1, S5_BLOCK_GROUPS * n_state)

    t = jnp.arange(SUBLANES)
    tabs = []
    for d in range(n_dir):
        per_dir = []
        for shift in (1, 2, 4):
            keep = ((t <= SUBLANES - 1 - shift) if d == 1 else (t >= shift))[None, :, None]
            p_re, p_im = pole_power(float(shift))
            per_dir += [jnp.where(keep, per_lane(p_re)[d], 0.0), jnp.where(keep, per_lane(p_im)[d], 0.0)]
        expo = ((SUBLANES - t) if d == 1 else (t + 1)).astype(F32)
        p_re, p_im = pole_power(expo[:, None, None, None])
        per_dir += [jnp.moveaxis(p_re[:, d].reshape(SUBLANES, nb, -1), 0, 1),
                    jnp.moveaxis(p_im[:, d].reshape(SUBLANES, nb, -1), 0, 1)]
        tabs.append(jnp.stack(per_dir, axis=1))
    return bmat, cmat, jnp.stack(tabs).astype(F32)


def _tri_matrices():
    t = np.arange(ML_CHUNK)
    lower = (t[None, :] <= t[:, None]).astype(np.float32)
    upper = lower.T
    return jnp.asarray(np.stack([lower, upper, lower.T, upper.T]))


def kernel(x, c, ctx, c_ctx, w_mod, b_mod, norm_g, w_in, b_in, ml_gate_b, ml_norm_g, mla_qa_g, mla_kva_g, mla_w_uq, mla_w_ukv, mla_qn_g, mla_kn_g, s5_a_re, s5_a_im, s5_log_dt, s5_b_re, s5_b_im, s5_c_re, s5_c_im, s5_d, s5_w_glu, s5_b_glu, w_branch, w_out, w_ff1, w_ff2):
    b, s, d = x.shape
    tc = ctx.shape[1]
    depth = w_mod.shape[0]
    dv = ml_norm_g.shape[2]
    dk = dv // 2
    lora = mla_qa_g.shape[1]
    s5_width = s5_d.shape[1]
    branch_w = w_branch.shape[2]
    sizes = (ML_HEADS * dk, ML_HEADS * dk, ML_HEADS * dv, ML_HEADS * dv, 4 * ML_HEADS, lora, lora, MLA_ROPE,
             s5_width, N_BRANCH * d)
    assert sum(sizes) == w_in.shape[2] and b + 1 <= SUBLANES
    assert s % ML_CHUNK == 0 and tc % ML_CHUNK == 0 and branch_w == ML_HEADS * dv == MLA_HEADS * LANES == s5_width

    n_x = b * s
    n_c = b * tc
    tm = _row_tile(s, n_c)
    n_x_tiles = n_x // tm
    n_tiles = n_x_tiles + n_c // tm
    tiles_per_batch = s // tm
    tile_args = (n_x_tiles, tiles_per_batch, b)

    xs = jnp.concatenate([x.reshape(n_x, d), ctx.reshape(n_c, d)], axis=0)
    cc = jnp.concatenate([c, c_ctx[None, :], jnp.zeros((SUBLANES - b - 1, d), F32)], axis=0)
    mod = _modulation(cc, w_mod, b_mod)
    tm_q = min(tm, 512)
    tab = _rope_table(s, tm_q)
    tri = _tri_matrices()
    n_used = sum(sizes) - sizes[4] + MLA_ROPE
    tn_in = 1280
    n_pad = -(-n_used // tn_in) * tn_in

    for l in range(depth):
        with_ctx_out = l < depth - 1
        mod3 = mod[l].reshape(SUBLANES * N_MOD, 1, d)
        w_p, b_p, wg, bg, offs = _pack_w_in(w_in[l], b_in[l], ml_gate_b[l], sizes, n_pad)
        z, gz = _in_proj(xs, norm_g[l, 0][None, :], mod3, w_p, b_p, wg, bg, tm, tn_in, *tile_args)

        a_x, a_c = _mlstm(z, gz, tri, ml_norm_g[l], b, s, tc, dk, dv,
                          (offs["q"], offs["k"], offs["v"], offs["o"]), with_ctx_out)

        wq, wkv, gq, gk = _pack_mla(mla_w_uq[l], mla_w_ukv[l], mla_qn_g[l], mla_kn_g[l])
        qo, ko, vo = _mla_proj(z, tab, mla_qa_g[l][None, :], mla_kva_g[l][None, :], wq, wkv, gq, gk, tm_q,
                               (offs["qa"], offs["kva"], offs["kpe"]), n_x // tm_q, s // tm_q)
        b_x = _attention_x(qo, ko, vo, b, s, tc, min(256, s))

        bmat, cmat, tabs = _pack_s5(s5_a_re[l], s5_a_im[l], s5_log_dt[l], s5_b_re[l], s5_b_im[l], s5_c_re[l],
                                    s5_c_im[l])
        y = _s5_scan(z, bmat, cmat, tabs, b, s, tc, offs["u"])
        n_out_tiles = n_tiles if with_ctx_out else n_x_tiles
        c_all = _s5_glu(y, z, s5_d[l][None, :], s5_w_glu[l].astype(BF16), s5_b_glu[l][None, :], tm, offs["u"],
                        n_out_tiles)

        if with_ctx_out:
            b_c = _attention_ctx(qo, ko, vo, b, s, tc)
            a_all = jnp.concatenate([a_x, a_c], axis=0)
            b_all = jnp.concatenate([b_x, b_c], axis=0)
        else:
            a_all, b_all = a_x, b_x
        merged = _merge(a_all, b_all, c_all, z, w_branch[l].astype(BF16), tm, 1024, offs["gates"], n_out_tiles)
        xs1 = _out_proj_residual(merged, w_out[l].astype(BF16), xs, mod3, tm, 1024, n_out_tiles, *tile_args, 2)
        hid = _ff1(xs1, norm_g[l, 1][None, :], mod3, w_ff1[l].astype(BF16), tm, 1024, n_out_tiles, *tile_args)
        xs = _ff2(hid, w_ff2[l].astype(BF16), xs1, mod3, tm, 1024, 1024, n_out_tiles, *tile_args)

    return xs[:n_x].reshape(b, s, d)
```

```python
import functools
import math

import jax
import jax.numpy as jnp
import numpy as np
from jax import lax
from jax.experimental import pallas as pl
from jax.experimental.pallas import tpu as pltpu

F32 = jnp.float32
BF16 = jnp.bfloat16
HIGHEST = lax.Precision.HIGHEST

N_MOD = 6
N_BRANCH = 3
ML_HEADS = 4
MLA_HEADS = 8
MLA_NOPE = 128
MLA_ROPE = 64
MLA_DQK = MLA_NOPE + MLA_ROPE
MLA_SLAB = 256
GRID_W = 64
ROPE_THETA = 10000.0
S5_GROUP = 16
S5_STATE = 64
S5_BLOCK_GROUPS = 8
EPS = 1e-6
NEG_BIG = -1e30

LANES = 128
SUBLANES = 8
VMEM_LIMIT = 56 * 1024 * 1024

ML_CHUNK = 256
S5_SUB = 8
ATTN_KEY_CHUNK = 512
Q_PRESCALE = MLA_DQK ** -0.5 * math.log2(math.e)
ROW_CHUNK = 64


def _cparams(sem):
    return pltpu.CompilerParams(dimension_semantics=sem, vmem_limit_bytes=VMEM_LIMIT)


def _row_tile(n_x_rows_per_batch, n_ctx_rows):
    tm = 1024
    while n_x_rows_per_batch % tm or n_ctx_rows % tm:
        tm //= 2
    return tm


def _mod_kernel(c_ref, w_ref, b_ref, o_ref):
    s = c_ref[...]
    s = s * jax.nn.sigmoid(s)
    o_ref[0] = jnp.dot(s.astype(BF16), w_ref[0].astype(BF16), preferred_element_type=F32) + b_ref[0]


def _modulation(cc, w_mod, b_mod):
    n_layers, d, n = w_mod.shape
    tn = 1024
    return pl.pallas_call(
        _mod_kernel,
        out_shape=jax.ShapeDtypeStruct((n_layers, SUBLANES, n), F32),
        grid=(n_layers, n // tn),
        in_specs=[
            pl.BlockSpec((SUBLANES, d), lambda l, j: (0, 0)),
            pl.BlockSpec((1, d, tn), lambda l, j: (l, 0, j)),
            pl.BlockSpec((1, 1, tn), lambda l, j: (l, 0, j)),
        ],
        out_specs=pl.BlockSpec((1, SUBLANES, tn), lambda l, j: (l, 0, j)),
        compiler_params=_cparams(("parallel", "parallel")),
        name="adaln_mod",
    )(cc, w_mod, b_mod.reshape(n_layers, 1, n))


def _norm_mod_rows(x_ref, g_ref, sh_ref, sc_ref, xn_ref):
    tm = x_ref.shape[0]
    g = g_ref[...]
    sc = 1.0 + sc_ref[0]
    sh = sh_ref[0]

    def body(r, carry):
        rows = pl.ds(pl.multiple_of(r * ROW_CHUNK, ROW_CHUNK), ROW_CHUNK)
        x = x_ref[rows, :]
        ms = jnp.mean(x * x, axis=-1, keepdims=True)
        y = x * lax.rsqrt(ms + EPS) * g
        xn_ref[rows, :] = (y * sc + sh).astype(BF16)
        return carry

    lax.fori_loop(0, tm // ROW_CHUNK, body, 0)


def _mod_row_map(n_x_tiles, tiles_per_batch, ctx_row, k):
    def index_map(i, j):
        r = jnp.where(i < n_x_tiles, i // tiles_per_batch, ctx_row)
        return (r * N_MOD + k, 0, 0)
    return index_map


def _stream_specs(block, n_x_tiles, ctx_tile0, col_map):
    x_spec = pl.BlockSpec(block, lambda i, *r: (jnp.minimum(i, n_x_tiles - 1), col_map(*r)))
    c_spec = pl.BlockSpec(block, lambda i, *r: (ctx_tile0 + jnp.maximum(i - n_x_tiles, 0), col_map(*r)),
                          pipeline_mode=pl.Buffered(1))
    return x_spec, c_spec


def _in_kernel(x_ref, g_ref, sh_ref, sc_ref, w_ref, b_ref, wg_ref, bg_ref, z_ref, gz_ref, xn_ref):
    @pl.when(pl.program_id(1) == 0)
    def _():
        _norm_mod_rows(x_ref, g_ref, sh_ref, sc_ref, xn_ref)
        gz_ref[...] = jnp.dot(xn_ref[...], wg_ref[...], preferred_element_type=F32) + bg_ref[...]

    z_ref[...] = (jnp.dot(xn_ref[...], w_ref[...], preferred_element_type=F32) + b_ref[...]).astype(BF16)


def _in_proj(xs, g, mod3, w, b, wg, bg, tm, tn, n_x_tiles, tiles_per_batch, ctx_row):
    r, d = xs.shape
    nz = w.shape[1]
    return pl.pallas_call(
        _in_kernel,
        out_shape=(jax.ShapeDtypeStruct((r, nz), BF16), jax.ShapeDtypeStruct((r, LANES), F32)),
        grid=(r // tm, nz // tn),
        in_specs=[
            pl.BlockSpec((tm, d), lambda i, j: (i, 0)),
            pl.BlockSpec((1, d), lambda i, j: (0, 0)),
            pl.BlockSpec((1, 1, d), _mod_row_map(n_x_tiles, tiles_per_batch, ctx_row, 0)),
            pl.BlockSpec((1, 1, d), _mod_row_map(n_x_tiles, tiles_per_batch, ctx_row, 1)),
            pl.BlockSpec((d, tn), lambda i, j: (0, j)),
            pl.BlockSpec((1, tn), lambda i, j: (0, j)),
            pl.BlockSpec((d, LANES), lambda i, j: (0, 0)),
            pl.BlockSpec((1, LANES), lambda i, j: (0, 0)),
        ],
        out_specs=(
            pl.BlockSpec((tm, tn), lambda i, j: (i, j)),
            pl.BlockSpec((tm, LANES), lambda i, j: (i, 0)),
        ),
        scratch_shapes=[pltpu.VMEM((tm, d), BF16)],
        compiler_params=_cparams(("parallel", "arbitrary")),
        name="in_proj",
    )(xs, g, mod3, mod3, w, b, wg, bg)


def _log_sigmoid(x):
    return jnp.minimum(x, 0.0) - jnp.log1p(jnp.exp(-jnp.abs(x)))


def _ml_chunk(q, k, v, lg_c, lg_r, tri, tri_t, carry, reverse, scale):
    c_mat, n_vec, m = carry
    length = q.shape[0]
    gi = 2 if reverse else 0
    cum_all_c = jnp.dot(tri, lg_c, precision=HIGHEST, preferred_element_type=F32)
    cum_all_r = jnp.dot(lg_r, tri_t, precision=HIGHEST, preferred_element_type=F32)
    li_c = lg_c[:, gi:gi + 1]
    cum_c = cum_all_c[:, gi + 1:gi + 2]
    li_r = lg_r[gi:gi + 1, :]
    cum_r = cum_all_r[gi + 1:gi + 2, :]
    total = jnp.sum(lg_r[gi + 1:gi + 2, :], axis=-1, keepdims=True)

    t_idx = lax.broadcasted_iota(jnp.int32, (length, length), 0)
    s_idx = lax.broadcasted_iota(jnp.int32, (length, length), 1)
    keep = (s_idx >= t_idx) if reverse else (s_idx <= t_idx)
    log_w = jnp.where(keep, cum_c - cum_r + li_r, NEG_BIG)
    log_inter = cum_c + m
    m_t = jnp.maximum(log_inter, jnp.max(log_w, axis=-1, keepdims=True))
    w_inter = jnp.exp(log_inter - m_t) * scale
    qk = lax.dot_general(q, k, (((1,), (1,)), ((), ())), preferred_element_type=F32)
    s = qk * (jnp.exp(log_w - m_t) * scale)
    num = w_inter * jnp.dot(q, c_mat.astype(BF16), preferred_element_type=F32)
    num = num + jnp.dot(s.astype(BF16), v, preferred_element_type=F32)
    qn = jnp.sum(q.astype(F32) * n_vec, axis=-1, keepdims=True)
    den = w_inter * qn + jnp.sum(s, axis=-1, keepdims=True)
    h = num / jnp.maximum(jnp.abs(den), jnp.exp(-m_t))

    log_end_r = total - cum_r + li_r
    m_new = jnp.maximum(total + m, jnp.max(log_end_r, axis=-1, keepdims=True))
    decay = jnp.exp(total + m - m_new)
    w_end_c = jnp.exp(total - cum_c + li_c - m_new)
    kw = k.astype(F32) * w_end_c
    c_new = decay * c_mat + lax.dot_general(kw.astype(BF16), v, (((0,), (0,)), ((), ())),
                                            preferred_element_type=F32)
    n_new = decay * n_vec + jnp.sum(kw, axis=0, keepdims=True)
    return h, (c_new, n_new, m_new)


def _mlstm_kernel(with_ctx_out, qx, kx, vx, ox, qc, kc, vc, oc, gx, gc, tri_ref, ng_ref, *rest):
    if with_ctx_out:
        ax_ref, ac_ref, lgx_c, lgx_r, lgc_c, lgc_r, hx, hc = rest
    else:
        ax_ref, lgx_c, lgx_r, lgc_c, lgc_r, hx, hc = rest
        ac_ref = None
    head = pl.program_id(1)
    dk = qx.shape[1]
    scale = dk ** -0.5
    n_x_chunks = qx.shape[0] // ML_CHUNK
    n_c_chunks = qc.shape[0] // ML_CHUNK

    r_idx = lax.broadcasted_iota(jnp.int32, (LANES, LANES), 0)
    c_idx = lax.broadcasted_iota(jnp.int32, (LANES, LANES), 1)
    sel = jnp.where((r_idx == c_idx * ML_HEADS + head) & (c_idx < 4), 1.0, 0.0).astype(F32)
    r8 = lax.broadcasted_iota(jnp.int32, (SUBLANES, LANES), 0)
    c8 = lax.broadcasted_iota(jnp.int32, (SUBLANES, LANES), 1)
    sel_t = jnp.where((c8 == r8 * ML_HEADS + head) & (r8 < 4), 1.0, 0.0).astype(F32)

    def prep(g_ref, lg_c_ref, lg_r_ref):
        def body(i, carry):
            rows = pl.ds(pl.multiple_of(i * ML_CHUNK, ML_CHUNK), ML_CHUNK)
            g = g_ref[rows, :]
            gs = jnp.dot(g, sel, precision=HIGHEST, preferred_element_type=F32)
            col = lax.broadcasted_iota(jnp.int32, gs.shape, 1)
            lg_c_ref[rows, :] = jnp.where(col % 2 == 1, _log_sigmoid(gs), gs)
            gt = lax.dot_general(sel_t, g, (((1,), (1,)), ((), ())), precision=HIGHEST,
                                 preferred_element_type=F32)
            row = lax.broadcasted_iota(jnp.int32, gt.shape, 0)
            lg_r_ref[i] = jnp.where(row % 2 == 1, _log_sigmoid(gt), gt)
            return carry
        lax.fori_loop(0, g_ref.shape[0] // ML_CHUNK, body, 0)

    prep(gx, lgx_c, lgx_r)
    prep(gc, lgc_c, lgc_r)

    for d in range(2):
        reverse = d == 1
        tri = tri_ref[d]
        tri_t = tri_ref[2 + d]

        def step(q_ref, k_ref, v_ref, lg_c_ref, lg_r_ref, h_ref, ci, carry):
            rows = pl.ds(pl.multiple_of(ci * ML_CHUNK, ML_CHUNK), ML_CHUNK)
            h, carry = _ml_chunk(q_ref[rows, :], k_ref[rows, :], v_ref[rows, :], lg_c_ref[rows, :],
                                 lg_r_ref[ci], tri, tri_t, carry, reverse, scale)
            if reverse:
                h_ref[rows, :] += h
            else:
                h_ref[rows, :] = h
            return carry

        carry = (jnp.zeros((dk, vx.shape[1]), F32), jnp.zeros((1, dk), F32), jnp.zeros((1, 1), F32))

        def ctx_body(i, carry):
            ci = (n_c_chunks - 1 - i) if reverse else i
            return step(qc, kc, vc, lgc_c, lgc_r, hc, ci, carry)

        def x_body(i, carry):
            ci = (n_x_chunks - 1 - i) if reverse else i
            return step(qx, kx, vx, lgx_c, lgx_r, hx, ci, carry)

        carry = lax.fori_loop(0, n_c_chunks, ctx_body, carry)
        lax.fori_loop(0, n_x_chunks, x_body, carry)

    ng = ng_ref[0]

    def finish(h_ref, o_ref, a_ref):
        def body(i, carry):
            rows = pl.ds(pl.multiple_of(i * ML_CHUNK, ML_CHUNK), ML_CHUNK)
            h = h_ref[rows, :]
            hn = h * lax.rsqrt(jnp.mean(h * h, axis=-1, keepdims=True) + EPS) * ng
            a_ref[rows, :] = (hn * jax.nn.sigmoid(o_ref[rows, :].astype(F32))).astype(BF16)
            return carry
        lax.fori_loop(0, h_ref.shape[0] // ML_CHUNK, body, 0)

    finish(hx, ox, ax_ref)
    if with_ctx_out:
        finish(hc, oc, ac_ref)


def _mlstm(z, gz, tri, ml_norm_g, b, s, tc, dk, dv, cols, with_ctx_out):
    n_x = b * s
    cq, ck, cv, co = cols
    ctx0 = n_x // tc

    def xspec(width, col0):
        return pl.BlockSpec((s, width), lambda i, h: (i, col0 // width + h))

    def cspec(width, col0):
        return pl.BlockSpec((tc, width), lambda i, h: (ctx0 + i, col0 // width + h))

    out_shape = [jax.ShapeDtypeStruct((n_x, ML_HEADS * dv), BF16)]
    out_specs = [pl.BlockSpec((s, dv), lambda i, h: (i, h))]
    if with_ctx_out:
        out_shape.append(jax.ShapeDtypeStruct((b * tc, ML_HEADS * dv), BF16))
        out_specs.append(pl.BlockSpec((tc, dv), lambda i, h: (i, h)))
    res = pl.pallas_call(
        functools.partial(_mlstm_kernel, with_ctx_out),
        out_shape=tuple(out_shape),
        grid=(b, ML_HEADS),
        in_specs=[
            xspec(dk, cq), xspec(dk, ck), xspec(dv, cv), xspec(dv, co),
            cspec(dk, cq), cspec(dk, ck), cspec(dv, cv), cspec(dv, co),
            pl.BlockSpec((s, LANES), lambda i, h: (i, 0)),
            pl.BlockSpec((tc, LANES), lambda i, h: (ctx0 + i, 0)),
            pl.BlockSpec((4, ML_CHUNK, ML_CHUNK), lambda i, h: (0, 0, 0)),
            pl.BlockSpec((1, 1, dv), lambda i, h: (h, 0, 0)),
        ],
        out_specs=tuple(out_specs),
        scratch_shapes=[
            pltpu.VMEM((s, LANES), F32), pltpu.VMEM((s // ML_CHUNK, SUBLANES, ML_CHUNK), F32),
            pltpu.VMEM((tc, LANES), F32), pltpu.VMEM((tc // ML_CHUNK, SUBLANES, ML_CHUNK), F32),
            pltpu.VMEM((s, dv), F32), pltpu.VMEM((tc, dv), F32),
        ],
        compiler_params=_cparams(("parallel", "parallel")),
        name="mlstm",
    )(z, z, z, z, z, z, z, z, gz, gz, tri, ml_norm_g.reshape(ML_HEADS, 1, dv))
    return res if with_ctx_out else (res[0], None)


def _mla_proj_kernel(qa_ref, kva_ref, kpe_ref, tab_ref, qag_ref, kvag_ref, wq_ref, wkv_ref,
                     gq_ref, gk_ref, q_ref, k_ref, v_ref):
    def normed(a_ref, g_ref):
        a = a_ref[...].astype(F32)
        return (a * lax.rsqrt(jnp.mean(a * a, axis=-1, keepdims=True) + EPS) * g_ref[...]).astype(BF16)

    q_all = jnp.dot(normed(qa_ref, qag_ref), wq_ref[...], preferred_element_type=F32)
    kv_all = jnp.dot(normed(kva_ref, kvag_ref), wkv_ref[...], preferred_element_type=F32)
    tab = tab_ref[...]
    lane = lax.broadcasted_iota(jnp.int32, tab.shape, 1)
    first_half = lane < MLA_ROPE
    gq = gq_ref[...]
    gk = gk_ref[...]
    inv_dqk = 1.0 / MLA_DQK

    kpe = kpe_ref[...].astype(F32)
    ss_kpe = jnp.sum(jnp.where(first_half, kpe * kpe, 0.0), axis=-1, keepdims=True)
    kpe_t = kpe * (tab * gk[:, LANES:])
    kpe_rot = jnp.where(first_half, kpe_t + pltpu.roll(kpe_t, MLA_ROPE, axis=1), 0.0)

    for h in range(MLA_HEADS):
        qn = q_all[:, h * MLA_SLAB:h * MLA_SLAB + LANES]
        qp = q_all[:, h * MLA_SLAB + LANES:(h + 1) * MLA_SLAB]
        ss = jnp.sum(qn * qn, axis=-1, keepdims=True) + jnp.sum(jnp.where(first_half, qp * qp, 0.0), axis=-1,
                                                                keepdims=True)
        r = lax.rsqrt(ss * inv_dqk + EPS) * Q_PRESCALE
        qp_t = qp * (tab * gq[:, LANES:])
        qp_rot = qp_t + pltpu.roll(qp_t, MLA_ROPE, axis=1)
        q_ref[:, h * MLA_SLAB:h * MLA_SLAB + LANES] = (qn * r * gq[:, :LANES]).astype(BF16)
        q_ref[:, h * MLA_SLAB + LANES:(h + 1) * MLA_SLAB] = (qp_rot * r).astype(BF16)

        kn = kv_all[:, h * LANES:(h + 1) * LANES]
        rk = lax.rsqrt((jnp.sum(kn * kn, axis=-1, keepdims=True) + ss_kpe) * inv_dqk + EPS)
        k_ref[:, h * MLA_SLAB:h * MLA_SLAB + LANES] = (kn * rk * gk[:, :LANES]).astype(BF16)
        k_ref[:, h * MLA_SLAB + LANES:(h + 1) * MLA_SLAB] = (kpe_rot * rk).astype(BF16)

    ones_col = jnp.where(lane == 0, 1.0, 0.0).astype(BF16)
    for h in range(MLA_HEADS):
        v_ref[:, h * MLA_SLAB:h * MLA_SLAB + LANES] = kv_all[:, (MLA_HEADS + h) * LANES:(MLA_HEADS + h + 1) * LANES
                                                             ].astype(BF16)
        v_ref[:, h * MLA_SLAB + LANES:(h + 1) * MLA_SLAB] = ones_col


def _mla_proj(z, tab, qag, kvag, wq, wkv, gq, gk, tm, cols, n_x_tiles, tab_tiles):
    r = z.shape[0]
    cqa, ckva, ckpe = cols
    lora = qag.shape[1]
    hs = MLA_HEADS * MLA_SLAB
    hv = MLA_HEADS * LANES
    return pl.pallas_call(
        _mla_proj_kernel,
        out_shape=(jax.ShapeDtypeStruct((r, hs), BF16), jax.ShapeDtypeStruct((r, hs), BF16),
                   jax.ShapeDtypeStruct((r, hs), BF16)),
        grid=(r // tm,),
        in_specs=[
            pl.BlockSpec((tm, lora), lambda i: (i, cqa // lora)),
            pl.BlockSpec((tm, lora), lambda i: (i, ckva // lora)),
            pl.BlockSpec((tm, LANES), lambda i: (i, ckpe // LANES)),
            pl.BlockSpec((tm, LANES), lambda i: (jnp.where(i < n_x_tiles, i % tab_tiles, tab_tiles), 0)),
            pl.BlockSpec((1, lora), lambda i: (0, 0)),
            pl.BlockSpec((1, lora), lambda i: (0, 0)),
            pl.BlockSpec((lora, hs), lambda i: (0, 0)),
            pl.BlockSpec((lora, 2 * hv), lambda i: (0, 0)),
            pl.BlockSpec((1, MLA_SLAB), lambda i: (0, 0)),
            pl.BlockSpec((1, MLA_SLAB), lambda i: (0, 0)),
        ],
        out_specs=(pl.BlockSpec((tm, hs), lambda i: (i, 0)), pl.BlockSpec((tm, hs), lambda i: (i, 0)),
                   pl.BlockSpec((tm, hs), lambda i: (i, 0))),
        compiler_params=_cparams(("parallel",)),
        name="mla_qkv",
    )(z, z, z, tab, qag, kvag, wq, wkv, gq, gk)


def _attn_kernel(n_kv, tq, q_ref, *refs):
    k_refs = refs[:n_kv]
    v_refs = refs[n_kv:2 * n_kv]
    o_ref, s0_ref, s1_ref, m0_ref, m1_ref = refs[2 * n_kv:]
    slots = ((s0_ref, m0_ref), (s1_ref, m1_ref))
    n_tiles = q_ref.shape[0] // tq
    chunks = []
    col = 0
    for kv, k_ref in enumerate(k_refs):
        n_keys = k_ref.shape[0]
        step = min(ATTN_KEY_CHUNK, n_keys)
        for off in range(0, n_keys, step):
            chunks.append((kv, off, col, step))
            col += step

    def scores(t, slot):
        s_ref, m_ref = slots[slot]
        rows = pl.ds(pl.multiple_of(t * tq, tq), tq)
        q = q_ref[rows, :]
        run = None
        for kv, off, c0, size in chunks:
            s = lax.dot_general(q, k_refs[kv][off:off + size, :], (((1,), (1,)), ((), ())),
                                preferred_element_type=F32)
            s_ref[:, c0:c0 + size] = s
            for lb in range(size // LANES):
                blk = s[:, lb * LANES:(lb + 1) * LANES]
                run = blk if run is None else jnp.maximum(run, blk)
        m_ref[...] = run

    def finish(t, slot):
        s_ref, m_ref = slots[slot]
        rows = pl.ds(pl.multiple_of(t * tq, tq), tq)
        m = jnp.max(m_ref[...], axis=-1, keepdims=True)
        acc = None
        for kv, off, c0, size in chunks:
            p = jnp.exp2(s_ref[:, c0:c0 + size] - m).astype(BF16)
            pv = jnp.dot(p, v_refs[kv][off:off + size, :], preferred_element_type=F32)
            acc = pv if acc is None else acc + pv
        o_ref[rows, :] = (acc[:, :LANES] / acc[:, LANES:LANES + 1]).astype(BF16)

    scores(0, 0)

    def body(k, carry):
        scores(2 * k + 1, 1)
        finish(2 * k, 0)
        scores(jnp.minimum(2 * k + 2, n_tiles - 1), 0)
        finish(2 * k + 1, 1)
        return carry

    lax.fori_loop(0, n_tiles // 2, body, 0)
    if n_tiles % 2:
        finish(n_tiles - 1, 0)


def _attention(qo, ko, vo, b, s, tc, tq, latent):
    n_x = b * s
    ctx0 = n_x // tc
    cspec = pl.BlockSpec((tc, MLA_SLAB), lambda i, h: (ctx0 + i, h))
    xspec = pl.BlockSpec((s, MLA_SLAB), lambda i, h: (i, h))
    if latent:
        n_q, n_keys = s, s + tc
        in_specs = [xspec, cspec, xspec, cspec, xspec]
        args = (qo, ko, ko, vo, vo)
    else:
        n_q, n_keys = tc, tc
        in_specs = [cspec, cspec, cspec]
        args = (qo, ko, vo)
    return pl.pallas_call(
        functools.partial(_attn_kernel, (len(args) - 1) // 2, tq),
        out_shape=jax.ShapeDtypeStruct((b * n_q, MLA_HEADS * LANES), BF16),
        grid=(b, MLA_HEADS),
        in_specs=in_specs,
        out_specs=pl.BlockSpec((n_q, LANES), lambda i, h: (i, h)),
        scratch_shapes=[pltpu.VMEM((tq, n_keys), F32), pltpu.VMEM((tq, n_keys), F32),
                        pltpu.VMEM((tq, LANES), F32), pltpu.VMEM((tq, LANES), F32)],
        compiler_params=_cparams(("parallel", "parallel")),
        name="attn_latent" if latent else "attn_ctx",
    )(*args)


S5_HALF = S5_BLOCK_GROUPS * S5_STATE


def _s5_kernel(n_batch, rows_x, rows_c, dot_rows, u_ref, r_ref, m_ref, o_ref, tab_ref, y_ref, v_ref):
    d = pl.program_id(1)
    n_dot = u_ref.shape[0] // dot_rows

    def dot_rows_of(i):
        return pl.ds(pl.multiple_of(i * dot_rows, dot_rows), dot_rows)

    def increments(i, carry):
        rows = dot_rows_of(i)
        v_ref[rows, :] = jnp.dot(u_ref[rows, :], r_ref[0, 0], preferred_element_type=F32)
        return carry

    lax.fori_loop(0, n_dot, increments, 0)

    def cmul_add(ar, ai, cr, ci, xr, xi):
        return ar + cr * xr - ci * xi, ai + cr * xi + ci * xr

    def run(reverse):
        tab = tab_ref.at[0, 0]
        last = 0 if reverse else SUBLANES - 1
        first_row = lax.broadcasted_iota(jnp.int32, (SUBLANES, S5_HALF), 0) == (SUBLANES - 1 - last)

        def segment(bases, n_groups, carry):
            def body(i, carry):
                gi = (n_groups - 1 - i) if reverse else i
                out = []
                for base, (cre, cim) in zip(bases, carry):
                    rows = pl.ds(pl.multiple_of(base + gi * SUBLANES, SUBLANES), SUBLANES)
                    re = v_ref[rows, :S5_HALF]
                    im = v_ref[rows, S5_HALF:]
                    for lvl, shift in enumerate((1, 2, 4)):
                        sh = (SUBLANES - shift) if reverse else shift
                        re, im = cmul_add(re, im, tab[2 * lvl], tab[2 * lvl + 1],
                                          pltpu.roll(re, sh, axis=0), pltpu.roll(im, sh, axis=0))
                    re, im = cmul_add(re, im, tab[6], tab[7], cre, cim)
                    sh1 = (SUBLANES - 1) if reverse else 1
                    v_ref[rows, :S5_HALF] = jnp.where(first_row, cre, pltpu.roll(re, sh1, axis=0))
                    v_ref[rows, S5_HALF:] = jnp.where(first_row, cim, pltpu.roll(im, sh1, axis=0))
                    out.append((jnp.broadcast_to(re[last:last + 1, :], re.shape),
                                jnp.broadcast_to(im[last:last + 1, :], im.shape)))
                return tuple(out)
            return lax.fori_loop(0, n_groups, body, carry)

        zero = jnp.zeros((SUBLANES, S5_HALF), F32)
        carry = tuple((zero, zero) for _ in range(n_batch))
        carry = segment([n_batch * rows_x + bi * rows_c for bi in range(n_batch)], rows_c // SUBLANES, carry)
        segment([bi * rows_x for bi in range(n_batch)], rows_x // SUBLANES, carry)

        def outputs(i, carry):
            rows = dot_rows_of(i)
            y = jnp.dot(u_ref[rows, :], m_ref[0, 0], preferred_element_type=F32)
            y = y + jnp.dot(v_ref[rows, :].astype(BF16), o_ref[0, 0], preferred_element_type=F32)
            if reverse:
                y_ref[rows, :] += y
            else:
                y_ref[rows, :] = y
            return carry

        lax.fori_loop(0, n_dot, outputs, 0)

    @pl.when(d == 0)
    def _():
        run(False)

    @pl.when(d == 1)
    def _():
        run(True)


def _s5_scan(u2, rmat, mmat, omat, tabs, b, s, tc):
    rc = u2.shape[0]
    n_blocks = rmat.shape[1]
    width = S5_SUB * LANES
    dot_rows = max(r for r in range(16, 641, 16) if rc % r == 0)
    mat_spec = pl.BlockSpec((1, 1, width, width), lambda cb, d: (d, cb, 0, 0), pipeline_mode=pl.Buffered(1))
    return pl.pallas_call(
        functools.partial(_s5_kernel, b, s // S5_SUB, tc // S5_SUB, dot_rows),
        out_shape=jax.ShapeDtypeStruct((rc, n_blocks * width), F32),
        grid=(n_blocks, 2),
        in_specs=[
            pl.BlockSpec((rc, width), lambda cb, d: (0, cb), pipeline_mode=pl.Buffered(1)),
            mat_spec, mat_spec, mat_spec,
            pl.BlockSpec((1, 1, 8, SUBLANES, S5_HALF), lambda cb, d: (d, cb, 0, 0, 0)),
        ],
        out_specs=pl.BlockSpec((rc, width), lambda cb, d: (0, cb)),
        scratch_shapes=[pltpu.VMEM((rc, width), F32)],
        compiler_params=_cparams(("parallel", "arbitrary")),
        name="s5_scan",
    )(u2, rmat, mmat, omat, tabs)


def _glu_kernel(y_ref, u_ref, d_ref, w_ref, b_ref, o_ref, g_ref):
    tm = y_ref.shape[0]

    def body(r, carry):
        rows = pl.ds(pl.multiple_of(r * ROW_CHUNK, ROW_CHUNK), ROW_CHUNK)
        y = y_ref[rows, :] + d_ref[...] * u_ref[rows, :].astype(F32)
        g_ref[rows, :] = jax.nn.gelu(y).astype(BF16)
        return carry

    lax.fori_loop(0, tm // ROW_CHUNK, body, 0)
    g = g_ref[...]
    gate = jax.nn.sigmoid(jnp.dot(g, w_ref[...], preferred_element_type=F32) + b_ref[...])
    o_ref[...] = (g.astype(F32) * gate).astype(BF16)


def _s5_glu(y, z, d_skip, w_glu, b_glu, tm, col_u, n_row_tiles):
    width = y.shape[1]
    return pl.pallas_call(
        _glu_kernel,
        out_shape=jax.ShapeDtypeStruct((n_row_tiles * tm, width), BF16),
        grid=(n_row_tiles,),
        in_specs=[
            pl.BlockSpec((tm, width), lambda i: (i, 0)),
            pl.BlockSpec((tm, width), lambda i: (i, col_u // width)),
            pl.BlockSpec((1, width), lambda i: (0, 0)),
            pl.BlockSpec((width, width), lambda i: (0, 0)),
            pl.BlockSpec((1, width), lambda i: (0, 0)),
        ],
        out_specs=pl.BlockSpec((tm, width), lambda i: (i, 0)),
        scratch_shapes=[pltpu.VMEM((tm, width), BF16)],
        compiler_params=_cparams(("parallel",)),
        name="s5_glu",
    )(y, z, d_skip, w_glu, b_glu)


def _merge_kernel(n_x_tiles, ax_ref, ac_ref, bx_ref, bc_ref, c_ref, ga_ref, gb_ref, gc_ref, w_ref, o_ref):
    def combine(a_ref, b_ref):
        acc = None
        for r, (br_ref, gate_ref) in enumerate(((a_ref, ga_ref), (b_ref, gb_ref), (c_ref, gc_ref))):
            proj = jnp.dot(br_ref[...], w_ref[r], preferred_element_type=F32)
            term = jax.nn.sigmoid(gate_ref[...].astype(F32)) * proj
            acc = term if acc is None else acc + term
        o_ref[...] = acc.astype(BF16)

    @pl.when(pl.program_id(0) < n_x_tiles)
    def _():
        combine(ax_ref, bx_ref)

    @pl.when(pl.program_id(0) >= n_x_tiles)
    def _():
        combine(ac_ref, bc_ref)


def _merge(a_x, a_c, b_x, b_c, cc, z, w_branch, tm, tn, col_g, n_row_tiles, n_x_tiles):
    width = a_x.shape[1]
    d = w_branch.shape[2]

    def gate_spec(r):
        return pl.BlockSpec((tm, tn), lambda i, j: (i, (col_g + r * d) // tn + j))

    ax_spec, ac_spec = _stream_specs((tm, width), n_x_tiles, 0, lambda j: 0)
    return pl.pallas_call(
        functools.partial(_merge_kernel, n_x_tiles),
        out_shape=jax.ShapeDtypeStruct((n_row_tiles * tm, d), BF16),
        grid=(n_row_tiles, d // tn),
        in_specs=[
            ax_spec, ac_spec, ax_spec, ac_spec,
            pl.BlockSpec((tm, width), lambda i, j: (i, 0)),
            gate_spec(0), gate_spec(1), gate_spec(2),
            pl.BlockSpec((N_BRANCH, width, tn), lambda i, j: (0, 0, j)),
        ],
        out_specs=pl.BlockSpec((tm, tn), lambda i, j: (i, j)),
        compiler_params=_cparams(("parallel", "arbitrary")),
        name="merge",
    )(a_x, a_c, b_x, b_c, cc, z, z, z, w_branch)


def _resid_kernel(m_ref, w_ref, x_ref, al_ref, o_ref):
    o_ref[...] = x_ref[...] + al_ref[0] * jnp.dot(m_ref[...], w_ref[...], preferred_element_type=F32)


def _out_proj_residual(m, w, xs, mod3, tm, tn, n_row_tiles, n_x_tiles, tiles_per_batch, ctx_row, k_alpha):
    kdim, d = w.shape
    nt = d // tn

    def alpha_map(i, j):
        r = jnp.where(i < n_x_tiles, i // tiles_per_batch, ctx_row)
        return (r * N_MOD + k_alpha, 0, j)

    return pl.pallas_call(
        _resid_kernel,
        out_shape=jax.ShapeDtypeStruct((n_row_tiles * tm, d), F32),
        grid=(n_row_tiles, nt),
        in_specs=[
            pl.BlockSpec((tm, kdim), lambda i, j: (i, 0)),
            pl.BlockSpec((kdim, tn), lambda i, j: (0, j)),
            pl.BlockSpec((tm, tn), lambda i, j: (i, j)),
            pl.BlockSpec((1, 1, tn), alpha_map),
        ],
        out_specs=pl.BlockSpec((tm, tn), lambda i, j: (i, j)),
        compiler_params=_cparams(("parallel", "arbitrary")),
        name="out_proj",
    )(m, w, xs, mod3)


def _ff1_kernel(x_ref, g_ref, sh_ref, sc_ref, w_ref, h_ref, xn_ref):
    @pl.when(pl.program_id(1) == 0)
    def _():
        _norm_mod_rows(x_ref, g_ref, sh_ref, sc_ref, xn_ref)

    a = jnp.maximum(jnp.dot(xn_ref[...], w_ref[...], preferred_element_type=F32), 0.0)
    h_ref[...] = (a * a).astype(BF16)


def _ff1(xs, g, mod3, w, tm, tn, n_row_tiles, n_x_tiles, tiles_per_batch, ctx_row):
    d, dff = w.shape
    return pl.pallas_call(
        _ff1_kernel,
        out_shape=jax.ShapeDtypeStruct((n_row_tiles * tm, dff), BF16),
        grid=(n_row_tiles, dff // tn),
        in_specs=[
            pl.BlockSpec((tm, d), lambda i, j: (i, 0)),
            pl.BlockSpec((1, d), lambda i, j: (0, 0)),
            pl.BlockSpec((1, 1, d), _mod_row_map(n_x_tiles, tiles_per_batch, ctx_row, 3)),
            pl.BlockSpec((1, 1, d), _mod_row_map(n_x_tiles, tiles_per_batch, ctx_row, 4)),
            pl.BlockSpec((d, tn), lambda i, j: (0, j)),
        ],
        out_specs=pl.BlockSpec((tm, tn), lambda i, j: (i, j)),
        scratch_shapes=[pltpu.VMEM((tm, d), BF16)],
        compiler_params=_cparams(("parallel", "arbitrary")),
        name="ff1",
    )(xs, g, mod3, mod3, w)


def _ff2_kernel(h_ref, w_ref, x_ref, al_ref, o_ref):
    k = pl.program_id(2)
    part = jnp.dot(h_ref[...], w_ref[...], preferred_element_type=F32)

    @pl.when(k == 0)
    def _():
        o_ref[...] = part

    @pl.when(k > 0)
    def _():
        o_ref[...] += part

    @pl.when(k == pl.num_programs(2) - 1)
    def _():
        o_ref[...] = x_ref[...] + al_ref[0] * o_ref[...]


def _ff2(h, w, xs, mod3, tm, tn, tk, n_row_tiles, n_x_tiles, tiles_per_batch, ctx_row):
    dff, d = w.shape

    def alpha_map(i, j, k):
        r = jnp.where(i < n_x_tiles, i // tiles_per_batch, ctx_row)
        return (r * N_MOD + 5, 0, j)

    return pl.pallas_call(
        _ff2_kernel,
        out_shape=jax.ShapeDtypeStruct((n_row_tiles * tm, d), F32),
        grid=(n_row_tiles, d // tn, dff // tk),
        in_specs=[
            pl.BlockSpec((tm, tk), lambda i, j, k: (i, k)),
            pl.BlockSpec((tk, tn), lambda i, j, k: (k, j)),
            pl.BlockSpec((tm, tn), lambda i, j, k: (i, j)),
            pl.BlockSpec((1, 1, tn), alpha_map),
        ],
        out_specs=pl.BlockSpec((tm, tn), lambda i, j, k: (i, j)),
        compiler_params=_cparams(("parallel", "parallel", "arbitrary")),
        name="ff2",
    )(h, w, xs, mod3)


def _rope_partner():
    j = np.arange(MLA_ROPE)
    quarter = MLA_ROPE // 4
    return np.where((j // quarter) % 2 == 0, j + quarter, j - quarter)


def _rope_table(s, tm):
    pos = jnp.arange(s)
    row = (pos // GRID_W).astype(F32)
    col = (pos % GRID_W).astype(F32)
    n_freq = MLA_ROPE // 4
    inv_freq = ROPE_THETA ** (-jnp.arange(n_freq, dtype=F32) / n_freq)
    ang_r = row[:, None] * inv_freq
    ang_c = col[:, None] * inv_freq
    cos = jnp.concatenate([jnp.cos(ang_r)] * 2 + [jnp.cos(ang_c)] * 2, axis=-1)
    sin = jnp.concatenate([-jnp.sin(ang_r), jnp.sin(ang_r), -jnp.sin(ang_c), jnp.sin(ang_c)], axis=-1)
    ident = jnp.concatenate([jnp.ones((tm, MLA_ROPE), F32), jnp.zeros((tm, MLA_ROPE), F32)], axis=-1)
    return jnp.concatenate([jnp.concatenate([cos, sin], axis=-1), ident], axis=0)


def _pack_w_in(w_in, b_in, gate_b, sizes, n_pad):
    bounds = np.cumsum((0,) + sizes)
    seg = [slice(int(bounds[i]), int(bounds[i + 1])) for i in range(len(sizes))]
    partner = _rope_partner()
    order = (0, 1, 2, 3, 5, 6, 8, 9)
    w_parts = [w_in[:, seg[i]] for i in order]
    b_parts = [b_in[seg[i]] for i in order]
    kpe_w = w_in[:, seg[7]]
    kpe_b = b_in[seg[7]]
    w_parts += [kpe_w, kpe_w[:, partner]]
    b_parts += [kpe_b, kpe_b[partner]]
    offs = {}
    pos = 0
    for name, part in zip(("q", "k", "v", "o", "qa", "kva", "u", "gates", "kpe", "kpe_sw"), w_parts):
        offs[name] = pos
        pos += part.shape[1]
    pad = n_pad - pos
    w_parts.append(jnp.zeros((w_in.shape[0], pad), w_in.dtype))
    b_parts.append(jnp.zeros((pad,), b_in.dtype))
    w = jnp.concatenate(w_parts, axis=1).astype(BF16)
    b = jnp.concatenate(b_parts)[None, :]
    n_g = sizes[4]
    wg = jnp.concatenate([w_in[:, seg[4]], jnp.zeros((w_in.shape[0], LANES - n_g), w_in.dtype)], axis=1).astype(BF16)
    bg = jnp.concatenate([b_in[seg[4]] + gate_b.reshape(-1), jnp.zeros((LANES - n_g,), F32)])[None, :]
    return w, b, wg, bg, offs


def _pack_mla(w_uq, w_ukv, qn_g, kn_g):
    partner = _rope_partner()
    lora = w_uq.shape[0]
    wq = w_uq.reshape(lora, MLA_HEADS, MLA_DQK)
    wq = jnp.concatenate([wq, wq[:, :, MLA_NOPE + partner]], axis=-1).reshape(lora, MLA_HEADS * MLA_SLAB)
    wkv = w_ukv.reshape(w_ukv.shape[0], MLA_HEADS, -1)
    wkv = jnp.concatenate([wkv[:, :, :MLA_NOPE].reshape(lora, -1), wkv[:, :, MLA_NOPE:].reshape(lora, -1)], axis=-1)

    def gains(g):
        return jnp.concatenate([g, g[MLA_NOPE + partner]])[None, :]

    return wq.astype(BF16), wkv.astype(BF16), gains(qn_g), gains(kn_g)


def _pack_s5(a_re, a_im, log_dt, b_re, b_im, c_re, c_im):
    n_dir, n_groups, n_state = a_re.shape
    gc = b_re.shape[-1]
    nb = n_groups // S5_BLOCK_GROUPS
    lam_re = jnp.minimum(a_re.astype(F32), -1e-4)
    lam_im = a_im.astype(F32)
    dt = jnp.exp(log_dt.astype(F32))[..., None]

    def pole_power(k):
        mag = jnp.exp(k * lam_re * dt)
        return mag * jnp.cos(k * lam_im * dt), mag * jnp.sin(k * lam_im * dt)

    bar_re, bar_im = pole_power(1.0)
    den = lam_re * lam_re + lam_im * lam_im
    f_re = ((bar_re - 1.0) * lam_re + bar_im * lam_im) / den
    f_im = (bar_im * lam_re - (bar_re - 1.0) * lam_im) / den
    bb_re = f_re[..., None] * b_re.astype(F32) - f_im[..., None] * b_im.astype(F32)
    bb_im = f_re[..., None] * b_im.astype(F32) + f_im[..., None] * b_re.astype(F32)
    eye = jnp.eye(S5_BLOCK_GROUPS, dtype=F32)
    sub = S5_SUB
    c_re = c_re.astype(F32)
    c_im = c_im.astype(F32)

    tau = jnp.arange(sub + 1, dtype=F32)[:, None, None, None]
    p_re, p_im = pole_power(tau)
    w_re = p_re[:sub, ..., None] * bb_re - p_im[:sub, ..., None] * bb_im
    w_im = p_re[:sub, ..., None] * bb_im + p_im[:sub, ..., None] * bb_re
    taps = (jnp.einsum('dgcn,zdgnk->zdgck', c_re, w_re, precision=HIGHEST)
            - jnp.einsum('dgcn,zdgnk->zdgck', c_im, w_im, precision=HIGHEST))
    q_re = p_re[1:, :, :, None, :]
    q_im = p_im[1:, :, :, None, :]
    o_re = c_re[None] * q_re - c_im[None] * q_im
    o_im = -(c_re[None] * q_im + c_im[None] * q_re)

    t = np.arange(sub)
    rmats, mmats, omats = [], [], []
    for d in range(n_dir):
        if d == 0:
            lag = (t[None, :] - t[:, None])[None] == t[:, None, None]
            to_exit = (sub - 1 - t)[None, :] == t[:, None]
            age = t[None, :] == t[:, None]
        else:
            lag = (t[:, None] - t[None, :])[None] == t[:, None, None]
            to_exit = t[None, :] == t[:, None]
            age = (sub - 1 - t)[None, :] == t[:, None]
        lag, to_exit, age = (jnp.asarray(a, F32) for a in (lag, to_exit, age))

        def blocks(a):
            return a.reshape((sub, nb, S5_BLOCK_GROUPS) + a.shape[2:])

        mm = jnp.einsum('zst,zbgck,gh->bsgkthc', lag, blocks(taps[:, d]), eye)
        mmats.append(mm.reshape(nb, sub * LANES, sub * LANES))
        rr = [jnp.einsum('zs,zbgnk,gh->bsgkhn', to_exit, blocks(w[:, d]), eye).reshape(nb, sub * LANES, S5_HALF)
              for w in (w_re, w_im)]
        rmats.append(jnp.concatenate(rr, axis=-1))
        oo = [jnp.einsum('zt,zbgcn,gh->bgnthc', age, blocks(o[:, d]), eye).reshape(nb, S5_HALF, sub * LANES)
              for o in (o_re, o_im)]
        omats.append(jnp.concatenate(oo, axis=-2))

    def per_block(a):
        return a.reshape(a.shape[:-2] + (nb, S5_BLOCK_GROUPS * n_state))

    rows = jnp.arange(SUBLANES)
    tabs = []
    for d in range(n_dir):
        per_dir = []
        for shift in (1, 2, 4):
            keep = ((rows <= SUBLANES - 1 - shift) if d == 1 else (rows >= shift))[None, :, None]
            s_re, s_im = pole_power(float(shift * sub))
            per_dir += [jnp.where(keep, per_block(s_re[d])[:, None, :], 0.0),
                        jnp.where(keep, per_block(s_im[d])[:, None, :], 0.0)]
        expo = (((SUBLANES - rows) if d == 1 else (rows + 1)) * sub).astype(F32)
        s_re, s_im = pole_power(expo[:, None, None, None])
        per_dir += [jnp.moveaxis(per_block(s_re[:, d]), 0, 1), jnp.moveaxis(per_block(s_im[:, d]), 0, 1)]
        tabs.append(jnp.stack(per_dir, axis=1))
    return (jnp.stack(rmats).astype(BF16), jnp.stack(mmats).astype(BF16), jnp.stack(omats).astype(BF16),
            jnp.stack(tabs).astype(F32))


def _tri_matrices():
    t = np.arange(ML_CHUNK)
    lower = (t[None, :] <= t[:, None]).astype(np.float32)
    upper = lower.T
    return jnp.asarray(np.stack([lower, upper, lower.T, upper.T]))


def kernel(x, c, ctx, c_ctx, w_mod, b_mod, norm_g, w_in, b_in, ml_gate_b, ml_norm_g, mla_qa_g, mla_kva_g, mla_w_uq, mla_w_ukv, mla_qn_g, mla_kn_g, s5_a_re, s5_a_im, s5_log_dt, s5_b_re, s5_b_im, s5_c_re, s5_c_im, s5_d, s5_w_glu, s5_b_glu, w_branch, w_out, w_ff1, w_ff2):
    b, s, d = x.shape
    tc = ctx.shape[1]
    depth = w_mod.shape[0]
    dv = ml_norm_g.shape[2]
    dk = dv // 2
    lora = mla_qa_g.shape[1]
    s5_width = s5_d.shape[1]
    branch_w = w_branch.shape[2]
    sizes = (ML_HEADS * dk, ML_HEADS * dk, ML_HEADS * dv, ML_HEADS * dv, 4 * ML_HEADS, lora, lora, MLA_ROPE,
             s5_width, N_BRANCH * d)
    assert sum(sizes) == w_in.shape[2] and b + 1 <= SUBLANES
    assert s % ML_CHUNK == 0 and tc % ML_CHUNK == 0 and branch_w == ML_HEADS * dv == MLA_HEADS * LANES == s5_width

    n_x = b * s
    n_c = b * tc
    tm = _row_tile(s, n_c)
    n_x_tiles = n_x // tm
    n_tiles = n_x_tiles + n_c // tm
    tiles_per_batch = s // tm
    tile_args = (n_x_tiles, tiles_per_batch, b)

    xs = jnp.concatenate([x.reshape(n_x, d), ctx.reshape(n_c, d)], axis=0)
    r_total = n_x + n_c
    n_blocks = s5_width // LANES
    cc = jnp.concatenate([c, c_ctx[None, :], jnp.zeros((SUBLANES - b - 1, d), F32)], axis=0)
    mod = _modulation(cc, w_mod, b_mod)
    tm_q = min(tm, 512)
    tab = _rope_table(s, tm_q)
    tri = _tri_matrices()
    n_used = sum(sizes) - sizes[4] + MLA_ROPE
    tn_in = 1280
    n_pad = -(-n_used // tn_in) * tn_in

    for l in range(depth):
        with_ctx_out = l < depth - 1
        mod3 = mod[l].reshape(SUBLANES * N_MOD, 1, d)
        w_p, b_p, wg, bg, offs = _pack_w_in(w_in[l], b_in[l], ml_gate_b[l], sizes, n_pad)
        z, gz = _in_proj(xs, norm_g[l, 0][None, :], mod3, w_p, b_p, wg, bg, tm, tn_in, *tile_args)

        a_x, a_c = _mlstm(z, gz, tri, ml_norm_g[l], b, s, tc, dk, dv,
                          (offs["q"], offs["k"], offs["v"], offs["o"]), with_ctx_out)

        wq, wkv, gq, gk = _pack_mla(mla_w_uq[l], mla_w_ukv[l], mla_qn_g[l], mla_kn_g[l])
        qo, ko, vo = _mla_proj(z, tab, mla_qa_g[l][None, :], mla_kva_g[l][None, :], wq, wkv, gq, gk, tm_q,
                               (offs["qa"], offs["kva"], offs["kpe"]), n_x // tm_q, s // tm_q)
        tq = min(256, s)
        b_x = _attention(qo, ko, vo, b, s, tc, tq, True)

        rmat, mmat, omat, tabs = _pack_s5(s5_a_re[l], s5_a_im[l], s5_log_dt[l], s5_b_re[l], s5_b_im[l],
                                          s5_c_re[l], s5_c_im[l])
        u2 = z[:, offs["u"]:offs["u"] + s5_width].reshape(r_total // S5_SUB, S5_SUB, n_blocks, LANES)
        u2 = u2.transpose(0, 2, 1, 3).reshape(r_total // S5_SUB, n_blocks * S5_SUB * LANES)
        y2 = _s5_scan(u2, rmat, mmat, omat, tabs, b, s, tc)
        y = y2.reshape(r_total // S5_SUB, n_blocks, S5_SUB, LANES).transpose(0, 2, 1, 3).reshape(r_total, s5_width)
        n_out_tiles = n_tiles if with_ctx_out else n_x_tiles
        c_all = _s5_glu(y, z, s5_d[l][None, :], s5_w_glu[l].astype(BF16), s5_b_glu[l][None, :], tm, offs["u"],
                        n_out_tiles)

        if with_ctx_out:
            b_c = _attention(qo, ko, vo, b, s, tc, min(tq, tc), False)
        else:
            a_c, b_c = a_x, b_x
        merged = _merge(a_x, a_c, b_x, b_c, c_all, z, w_branch[l].astype(BF16), tm, 512, offs["gates"],
                        n_out_tiles, n_x_tiles)
        xs1 = _out_proj_residual(merged, w_out[l].astype(BF16), xs, mod3, tm, 1024, n_out_tiles, *tile_args, 2)
        hid = _ff1(xs1, norm_g[l, 1][None, :], mod3, w_ff1[l].astype(BF16), tm, 1024, n_out_tiles, *tile_args)
        xs = _ff2(hid, w_ff2[l].astype(BF16), xs1, mod3, tm, 1024, 2048, n_out_tiles, *tile_args)

    return xs.reshape(b, s, d)
```

```python
import functools
import math

import jax
import jax.numpy as jnp
import numpy as np
from jax import lax
from jax.experimental import pallas as pl
from jax.experimental.pallas import tpu as pltpu

F32 = jnp.float32
BF16 = jnp.bfloat16
HIGHEST = lax.Precision.HIGHEST

N_MOD = 6
N_BRANCH = 3
ML_HEADS = 4
MLA_HEADS = 8
MLA_NOPE = 128
MLA_ROPE = 64
MLA_DQK = MLA_NOPE + MLA_ROPE
MLA_SLAB = 256
GRID_W = 64
ROPE_THETA = 10000.0
S5_GROUP = 16
S5_STATE = 64
S5_BLOCK_GROUPS = 8
EPS = 1e-6
NEG_BIG = -1e30

LANES = 128
SUBLANES = 8
VMEM_LIMIT = 56 * 1024 * 1024

ML_CHUNK = 256
S5_SUB = 8
ATTN_KEY_CHUNK = 512
Q_PRESCALE = MLA_DQK ** -0.5 * math.log2(math.e)
ROW_CHUNK = 64


def _cparams(sem):
    return pltpu.CompilerParams(dimension_semantics=sem, vmem_limit_bytes=VMEM_LIMIT)


def _row_tile(n_x_rows_per_batch, n_ctx_rows):
    tm = 1024
    while n_x_rows_per_batch % tm or n_ctx_rows % tm:
        tm //= 2
    return tm


def _mod_kernel(c_ref, w_ref, b_ref, o_ref):
    s = c_ref[...]
    s = s * jax.nn.sigmoid(s)
    o_ref[0] = jnp.dot(s.astype(BF16), w_ref[0].astype(BF16), preferred_element_type=F32) + b_ref[0]


def _modulation(cc, w_mod, b_mod):
    n_layers, d, n = w_mod.shape
    tn = 1024
    return pl.pallas_call(
        _mod_kernel,
        out_shape=jax.ShapeDtypeStruct((n_layers, SUBLANES, n), F32),
        grid=(n_layers, n // tn),
        in_specs=[
            pl.BlockSpec((SUBLANES, d), lambda l, j: (0, 0)),
            pl.BlockSpec((1, d, tn), lambda l, j: (l, 0, j)),
            pl.BlockSpec((1, 1, tn), lambda l, j: (l, 0, j)),
        ],
        out_specs=pl.BlockSpec((1, SUBLANES, tn), lambda l, j: (l, 0, j)),
        compiler_params=_cparams(("parallel", "parallel")),
        name="adaln_mod",
    )(cc, w_mod, b_mod.reshape(n_layers, 1, n))


def _norm_mod_rows(x_ref, g_ref, sh_ref, sc_ref, xn_ref):
    tm = x_ref.shape[0]
    g = g_ref[...]
    sc = 1.0 + sc_ref[0]
    sh = sh_ref[0]

    def body(r, carry):
        rows = pl.ds(pl.multiple_of(r * ROW_CHUNK, ROW_CHUNK), ROW_CHUNK)
        x = x_ref[rows, :]
        ms = jnp.mean(x * x, axis=-1, keepdims=True)
        y = x * lax.rsqrt(ms + EPS) * g
        xn_ref[rows, :] = (y * sc + sh).astype(BF16)
        return carry

    lax.fori_loop(0, tm // ROW_CHUNK, body, 0)


def _mod_row_map(n_x_tiles, tiles_per_batch, ctx_row, k):
    def index_map(i, j):
        r = jnp.where(i < n_x_tiles, i // tiles_per_batch, ctx_row)
        return (r * N_MOD + k, 0, 0)
    return index_map


def _stream_specs(block, n_x_tiles, ctx_tile0, col_map):
    x_spec = pl.BlockSpec(block, lambda i, *r: (jnp.minimum(i, n_x_tiles - 1), col_map(*r)))
    c_spec = pl.BlockSpec(block, lambda i, *r: (ctx_tile0 + jnp.maximum(i - n_x_tiles, 0), col_map(*r)),
                          pipeline_mode=pl.Buffered(1))
    return x_spec, c_spec


def _in_kernel(x_ref, g_ref, sh_ref, sc_ref, w_ref, b_ref, wg_ref, bg_ref, z_ref, gz_ref, xn_ref):
    @pl.when(pl.program_id(1) == 0)
    def _():
        _norm_mod_rows(x_ref, g_ref, sh_ref, sc_ref, xn_ref)
        gz_ref[...] = jnp.dot(xn_ref[...], wg_ref[...], preferred_element_type=F32) + bg_ref[...]

    z_ref[...] = (jnp.dot(xn_ref[...], w_ref[...], preferred_element_type=F32) + b_ref[...]).astype(BF16)


def _in_proj(xs, g, mod3, w, b, wg, bg, tm, tn, n_x_tiles, tiles_per_batch, ctx_row):
    r, d = xs.shape
    nz = w.shape[1]
    return pl.pallas_call(
        _in_kernel,
        out_shape=(jax.ShapeDtypeStruct((r, nz), BF16), jax.ShapeDtypeStruct((r, LANES), F32)),
        grid=(r // tm, nz // tn),
        in_specs=[
            pl.BlockSpec((tm, d), lambda i, j: (i, 0)),
            pl.BlockSpec((1, d), lambda i, j: (0, 0)),
            pl.BlockSpec((1, 1, d), _mod_row_map(n_x_tiles, tiles_per_batch, ctx_row, 0)),
            pl.BlockSpec((1, 1, d), _mod_row_map(n_x_tiles, tiles_per_batch, ctx_row, 1)),
            pl.BlockSpec((d, tn), lambda i, j: (0, j)),
            pl.BlockSpec((1, tn), lambda i, j: (0, j)),
            pl.BlockSpec((d, LANES), lambda i, j: (0, 0)),
            pl.BlockSpec((1, LANES), lambda i, j: (0, 0)),
        ],
        out_specs=(
            pl.BlockSpec((tm, tn), lambda i, j: (i, j)),
            pl.BlockSpec((tm, LANES), lambda i, j: (i, 0)),
        ),
        scratch_shapes=[pltpu.VMEM((tm, d), BF16)],
        compiler_params=_cparams(("parallel", "arbitrary")),
        name="in_proj",
    )(xs, g, mod3, mod3, w, b, wg, bg)


def _log_sigmoid(x):
    return jnp.minimum(x, 0.0) - jnp.log1p(jnp.exp(-jnp.abs(x)))


def _split3(a):
    hi = a.astype(BF16)
    r1 = a - hi.astype(F32)
    mid = r1.astype(BF16)
    lo = (r1 - mid.astype(F32)).astype(BF16)
    return hi, mid, lo


def _ml_chunk(q, k, v, lg_c, lg_r, carry, reverse, scale):
    c_mat, n_vec, m = carry
    length = q.shape[0]
    gi = 2 if reverse else 0
    ci = 5 if reverse else 4
    li_c = lg_c[:, gi:gi + 1]
    cum_c = lg_c[:, ci:ci + 1]
    li_r = lg_r[gi:gi + 1, :]
    cum_r = lg_r[ci:ci + 1, :]
    total = jnp.sum(lg_r[gi + 1:gi + 2, :], axis=-1, keepdims=True)

    t_idx = lax.broadcasted_iota(jnp.int32, (length, length), 0)
    s_idx = lax.broadcasted_iota(jnp.int32, (length, length), 1)
    keep = (s_idx >= t_idx) if reverse else (s_idx <= t_idx)
    log_w = jnp.where(keep, cum_c - cum_r + li_r, NEG_BIG)
    log_inter = cum_c + m
    m_t = jnp.maximum(log_inter, jnp.max(log_w, axis=-1, keepdims=True))
    w_inter = jnp.exp(log_inter - m_t) * scale
    qk = lax.dot_general(q, k, (((1,), (1,)), ((), ())), preferred_element_type=F32)
    s = qk * (jnp.exp(log_w - m_t) * scale)
    num = w_inter * jnp.dot(q, c_mat.astype(BF16), preferred_element_type=F32)
    num = num + jnp.dot(s.astype(BF16), v, preferred_element_type=F32)
    qn = jnp.sum(q.astype(F32) * n_vec, axis=-1, keepdims=True)
    den = w_inter * qn + jnp.sum(s, axis=-1, keepdims=True)
    h = num / jnp.maximum(jnp.abs(den), jnp.exp(-m_t))

    log_end_r = total - cum_r + li_r
    m_new = jnp.maximum(total + m, jnp.max(log_end_r, axis=-1, keepdims=True))
    decay = jnp.exp(total + m - m_new)
    w_end_c = jnp.exp(total - cum_c + li_c - m_new)
    kw = k.astype(F32) * w_end_c
    c_new = decay * c_mat + lax.dot_general(kw.astype(BF16), v, (((0,), (0,)), ((), ())),
                                            preferred_element_type=F32)
    n_new = decay * n_vec + jnp.sum(kw, axis=0, keepdims=True)
    return h, (c_new, n_new, m_new)


def _mlstm_kernel(with_ctx_out, qx, kx, vx, ox, qc, kc, vc, oc, gx, gc, tri_ref, ng_ref, *rest):
    if with_ctx_out:
        ax_ref, ac_ref, lgx_c, lgx_r, lgc_c, lgc_r, hx, hc = rest
    else:
        ax_ref, lgx_c, lgx_r, lgc_c, lgc_r, hx, hc = rest
        ac_ref = None
    head = pl.program_id(1)
    dk = qx.shape[1]
    scale = dk ** -0.5
    n_x_chunks = qx.shape[0] // ML_CHUNK
    n_c_chunks = qc.shape[0] // ML_CHUNK

    r_idx = lax.broadcasted_iota(jnp.int32, (LANES, LANES), 0)
    c_idx = lax.broadcasted_iota(jnp.int32, (LANES, LANES), 1)
    sel = jnp.where((r_idx == c_idx * ML_HEADS + head) & (c_idx < 4), 1.0, 0.0).astype(BF16)
    tri = tri_ref[...]

    def prep(g_ref, lg_c_ref, lg_r_ref):
        def body(i, carry):
            rows = pl.ds(pl.multiple_of(i * ML_CHUNK, ML_CHUNK), ML_CHUNK)
            gs = sum(jnp.dot(p, sel, preferred_element_type=F32) for p in _split3(g_ref[rows, :]))
            col = lax.broadcasted_iota(jnp.int32, gs.shape, 1)
            lg = jnp.where(col % 2 == 1, _log_sigmoid(gs), gs)
            pre = sum(jnp.dot(tri, p, preferred_element_type=F32) for p in _split3(lg))
            suf = pre[ML_CHUNK - 1:ML_CHUNK, :] - pre + lg
            lg = jnp.where(col == 4, pltpu.roll(pre, 3, axis=1), jnp.where(col == 5, pltpu.roll(suf, 2, axis=1), lg))
            lg_c_ref[rows, :] = lg
            lg_r_ref[i] = lg.T[:SUBLANES, :]
            return carry
        n_chunks = g_ref.shape[0] // ML_CHUNK
        lax.fori_loop(0, n_chunks, body, 0, unroll=2 if n_chunks % 2 == 0 else 1)

    prep(gx, lgx_c, lgx_r)
    prep(gc, lgc_c, lgc_r)

    for d in range(2):
        reverse = d == 1

        def step(q_ref, k_ref, v_ref, lg_c_ref, lg_r_ref, h_ref, ci, carry):
            rows = pl.ds(pl.multiple_of(ci * ML_CHUNK, ML_CHUNK), ML_CHUNK)
            h, carry = _ml_chunk(q_ref[rows, :], k_ref[rows, :], v_ref[rows, :], lg_c_ref[rows, :],
                                 lg_r_ref[ci], carry, reverse, scale)
            if reverse:
                h_ref[rows, :] += h
            else:
                h_ref[rows, :] = h
            return carry

        carry = (jnp.zeros((dk, vx.shape[1]), F32), jnp.zeros((1, dk), F32), jnp.zeros((1, 1), F32))

        def ctx_body(i, carry):
            ci = (n_c_chunks - 1 - i) if reverse else i
            return step(qc, kc, vc, lgc_c, lgc_r, hc, ci, carry)

        def x_body(i, carry):
            ci = (n_x_chunks - 1 - i) if reverse else i
            return step(qx, kx, vx, lgx_c, lgx_r, hx, ci, carry)

        carry = lax.fori_loop(0, n_c_chunks, ctx_body, carry)
        lax.fori_loop(0, n_x_chunks, x_body, carry)

    ng = ng_ref[0]

    def finish(h_ref, o_ref, a_ref):
        def body(i, carry):
            rows = pl.ds(pl.multiple_of(i * ML_CHUNK, ML_CHUNK), ML_CHUNK)
            h = h_ref[rows, :]
            hn = h * lax.rsqrt(jnp.mean(h * h, axis=-1, keepdims=True) + EPS) * ng
            a_ref[rows, :] = (hn * jax.nn.sigmoid(o_ref[rows, :].astype(F32))).astype(BF16)
            return carry
        lax.fori_loop(0, h_ref.shape[0] // ML_CHUNK, body, 0)

    finish(hx, ox, ax_ref)
    if with_ctx_out:
        finish(hc, oc, ac_ref)


def _mlstm(z, gz, tri, ml_norm_g, b, s, tc, dk, dv, cols, with_ctx_out):
    n_x = b * s
    cq, ck, cv, co = cols
    ctx0 = n_x // tc

    def xspec(width, col0):
        return pl.BlockSpec((s, width), lambda i, h: (i, col0 // width + h))

    def cspec(width, col0):
        return pl.BlockSpec((tc, width), lambda i, h: (ctx0 + i, col0 // width + h))

    out_shape = [jax.ShapeDtypeStruct((n_x, ML_HEADS * dv), BF16)]
    out_specs = [pl.BlockSpec((s, dv), lambda i, h: (i, h))]
    if with_ctx_out:
        out_shape.append(jax.ShapeDtypeStruct((b * tc, ML_HEADS * dv), BF16))
        out_specs.append(pl.BlockSpec((tc, dv), lambda i, h: (i, h)))
    res = pl.pallas_call(
        functools.partial(_mlstm_kernel, with_ctx_out),
        out_shape=tuple(out_shape),
        grid=(b, ML_HEADS),
        in_specs=[
            xspec(dk, cq), xspec(dk, ck), xspec(dv, cv), xspec(dv, co),
            cspec(dk, cq), cspec(dk, ck), cspec(dv, cv), cspec(dv, co),
            pl.BlockSpec((s, LANES), lambda i, h: (i, 0)),
            pl.BlockSpec((tc, LANES), lambda i, h: (ctx0 + i, 0)),
            pl.BlockSpec((ML_CHUNK, ML_CHUNK), lambda i, h: (0, 0)),
            pl.BlockSpec((1, 1, dv), lambda i, h: (h, 0, 0)),
        ],
        out_specs=tuple(out_specs),
        scratch_shapes=[
            pltpu.VMEM((s, LANES), F32), pltpu.VMEM((s // ML_CHUNK, SUBLANES, ML_CHUNK), F32),
            pltpu.VMEM((tc, LANES), F32), pltpu.VMEM((tc // ML_CHUNK, SUBLANES, ML_CHUNK), F32),
            pltpu.VMEM((s, dv), F32), pltpu.VMEM((tc, dv), F32),
        ],
        compiler_params=_cparams(("parallel", "parallel")),
        name="mlstm",
    )(z, z, z, z, z, z, z, z, gz, gz, tri, ml_norm_g.reshape(ML_HEADS, 1, dv))
    return res if with_ctx_out else (res[0], None)


def _mla_proj_kernel(qa_ref, kva_ref, kpe_ref, tab_ref, qag_ref, kvag_ref, wq_ref, wkv_ref,
                     gq_ref, gk_ref, q_ref, k_ref, v_ref):
    def normed(a_ref, g_ref):
        a = a_ref[...].astype(F32)
        return (a * lax.rsqrt(jnp.mean(a * a, axis=-1, keepdims=True) + EPS) * g_ref[...]).astype(BF16)

    q_all = jnp.dot(normed(qa_ref, qag_ref), wq_ref[...], preferred_element_type=F32)
    kv_all = jnp.dot(normed(kva_ref, kvag_ref), wkv_ref[...], preferred_element_type=F32)
    tab = tab_ref[...]
    lane = lax.broadcasted_iota(jnp.int32, tab.shape, 1)
    first_half = lane < MLA_ROPE
    gq = gq_ref[...]
    gk = gk_ref[...]
    inv_dqk = 1.0 / MLA_DQK

    kpe = kpe_ref[...].astype(F32)
    ss_kpe = jnp.sum(jnp.where(first_half, kpe * kpe, 0.0), axis=-1, keepdims=True)
    kpe_t = kpe * (tab * gk[:, LANES:])
    kpe_rot = jnp.where(first_half, kpe_t + pltpu.roll(kpe_t, MLA_ROPE, axis=1), 0.0)

    for h in range(MLA_HEADS):
        qn = q_all[:, h * MLA_SLAB:h * MLA_SLAB + LANES]
        qp = q_all[:, h * MLA_SLAB + LANES:(h + 1) * MLA_SLAB]
        ss = jnp.sum(qn * qn, axis=-1, keepdims=True) + jnp.sum(jnp.where(first_half, qp * qp, 0.0), axis=-1,
                                                                keepdims=True)
        r = lax.rsqrt(ss * inv_dqk + EPS) * Q_PRESCALE
        qp_t = qp * (tab * gq[:, LANES:])
        qp_rot = qp_t + pltpu.roll(qp_t, MLA_ROPE, axis=1)
        q_ref[:, h * MLA_SLAB:h * MLA_SLAB + LANES] = (qn * r * gq[:, :LANES]).astype(BF16)
        q_ref[:, h * MLA_SLAB + LANES:(h + 1) * MLA_SLAB] = (qp_rot * r).astype(BF16)

        kn = kv_all[:, h * LANES:(h + 1) * LANES]
        rk = lax.rsqrt((jnp.sum(kn * kn, axis=-1, keepdims=True) + ss_kpe) * inv_dqk + EPS)
        k_ref[:, h * MLA_SLAB:h * MLA_SLAB + LANES] = (kn * rk * gk[:, :LANES]).astype(BF16)
        k_ref[:, h * MLA_SLAB + LANES:(h + 1) * MLA_SLAB] = (kpe_rot * rk).astype(BF16)

    ones_col = jnp.where(lane == 0, 1.0, 0.0).astype(BF16)
    for h in range(MLA_HEADS):
        v_ref[:, h * MLA_SLAB:h * MLA_SLAB + LANES] = kv_all[:, (MLA_HEADS + h) * LANES:(MLA_HEADS + h + 1) * LANES
                                                             ].astype(BF16)
        v_ref[:, h * MLA_SLAB + LANES:(h + 1) * MLA_SLAB] = ones_col


def _mla_proj(z, tab, qag, kvag, wq, wkv, gq, gk, tm, cols, n_x_tiles, tab_tiles):
    r = z.shape[0]
    cqa, ckva, ckpe = cols
    lora = qag.shape[1]
    hs = MLA_HEADS * MLA_SLAB
    hv = MLA_HEADS * LANES
    return pl.pallas_call(
        _mla_proj_kernel,
        out_shape=(jax.ShapeDtypeStruct((r, hs), BF16), jax.ShapeDtypeStruct((r, hs), BF16),
                   jax.ShapeDtypeStruct((r, hs), BF16)),
        grid=(r // tm,),
        in_specs=[
            pl.BlockSpec((tm, lora), lambda i: (i, cqa // lora)),
            pl.BlockSpec((tm, lora), lambda i: (i, ckva // lora)),
            pl.BlockSpec((tm, LANES), lambda i: (i, ckpe // LANES)),
            pl.BlockSpec((tm, LANES), lambda i: (jnp.where(i < n_x_tiles, i % tab_tiles, tab_tiles), 0)),
            pl.BlockSpec((1, lora), lambda i: (0, 0)),
            pl.BlockSpec((1, lora), lambda i: (0, 0)),
            pl.BlockSpec((lora, hs), lambda i: (0, 0)),
            pl.BlockSpec((lora, 2 * hv), lambda i: (0, 0)),
            pl.BlockSpec((1, MLA_SLAB), lambda i: (0, 0)),
            pl.BlockSpec((1, MLA_SLAB), lambda i: (0, 0)),
        ],
        out_specs=(pl.BlockSpec((tm, hs), lambda i: (i, 0)), pl.BlockSpec((tm, hs), lambda i: (i, 0)),
                   pl.BlockSpec((tm, hs), lambda i: (i, 0))),
        compiler_params=_cparams(("parallel",)),
        name="mla_qkv",
    )(z, z, z, tab, qag, kvag, wq, wkv, gq, gk)


def _attn_kernel(n_kv, tq, q_ref, *refs):
    k_refs = refs[:n_kv]
    v_refs = refs[n_kv:2 * n_kv]
    o_ref, s0_ref, s1_ref, m0_ref, m1_ref = refs[2 * n_kv:]
    slots = ((s0_ref, m0_ref), (s1_ref, m1_ref))
    n_tiles = q_ref.shape[0] // tq
    chunks = []
    col = 0
    for kv, k_ref in enumerate(k_refs):
        n_keys = k_ref.shape[0]
        step = min(ATTN_KEY_CHUNK, n_keys)
        for off in range(0, n_keys, step):
            chunks.append((kv, off, col, step))
            col += step

    def scores(t, slot):
        s_ref, m_ref = slots[slot]
        rows = pl.ds(pl.multiple_of(t * tq, tq), tq)
        q = q_ref[rows, :]
        run = None
        for kv, off, c0, size in chunks:
            s = lax.dot_general(q, k_refs[kv][off:off + size, :], (((1,), (1,)), ((), ())),
                                preferred_element_type=F32)
            s_ref[:, c0:c0 + size] = s
            for lb in range(size // LANES):
                blk = s[:, lb * LANES:(lb + 1) * LANES]
                run = blk if run is None else jnp.maximum(run, blk)
        m_ref[...] = run

    def finish(t, slot):
        s_ref, m_ref = slots[slot]
        rows = pl.ds(pl.multiple_of(t * tq, tq), tq)
        m = jnp.max(m_ref[...], axis=-1, keepdims=True)
        acc = None
        for kv, off, c0, size in chunks:
            p = jnp.exp2(s_ref[:, c0:c0 + size] - m).astype(BF16)
            pv = jnp.dot(p, v_refs[kv][off:off + size, :], preferred_element_type=F32)
            acc = pv if acc is None else acc + pv
        o_ref[rows, :] = (acc[:, :LANES] / acc[:, LANES:LANES + 1]).astype(BF16)

    scores(0, 0)

    def body(k, carry):
        scores(2 * k + 1, 1)
        finish(2 * k, 0)
        scores(jnp.minimum(2 * k + 2, n_tiles - 1), 0)
        finish(2 * k + 1, 1)
        return carry

    lax.fori_loop(0, n_tiles // 2, body, 0)
    if n_tiles % 2:
        finish(n_tiles - 1, 0)


def _attention(qo, ko, vo, b, s, tc, tq, latent):
    n_x = b * s
    ctx0 = n_x // tc
    cspec = pl.BlockSpec((tc, MLA_SLAB), lambda i, h: (ctx0 + i, h))
    xspec = pl.BlockSpec((s, MLA_SLAB), lambda i, h: (i, h))
    if latent:
        n_q, n_keys = s, s + tc
        in_specs = [xspec, cspec, xspec, cspec, xspec]
        args = (qo, ko, ko, vo, vo)
    else:
        n_q, n_keys = tc, tc
        in_specs = [cspec, cspec, cspec]
        args = (qo, ko, vo)
    return pl.pallas_call(
        functools.partial(_attn_kernel, (len(args) - 1) // 2, tq),
        out_shape=jax.ShapeDtypeStruct((b * n_q, MLA_HEADS * LANES), BF16),
        grid=(b, MLA_HEADS),
        in_specs=in_specs,
        out_specs=pl.BlockSpec((n_q, LANES), lambda i, h: (i, h)),
        scratch_shapes=[pltpu.VMEM((tq, n_keys), F32), pltpu.VMEM((tq, n_keys), F32),
                        pltpu.VMEM((tq, LANES), F32), pltpu.VMEM((tq, LANES), F32)],
        compiler_params=_cparams(("parallel", "parallel")),
        name="attn_latent" if latent else "attn_ctx",
    )(*args)


S5_HALF = S5_BLOCK_GROUPS * S5_STATE


def _s5_kernel(n_batch, rows_x, rows_c, dot_rows, u_ref, r_ref, m_ref, o_ref, tab_ref, y_ref, u2_ref, v_ref):
    d = pl.program_id(1)
    n_dot = u2_ref.shape[0] // dot_rows

    def dot_rows_of(i):
        return pl.ds(pl.multiple_of(i * dot_rows, dot_rows), dot_rows)

    def token_rows_of(i, s):
        return pl.ds(i * (dot_rows * S5_SUB) + s, dot_rows, stride=S5_SUB)

    @pl.when(d == 0)
    def _():
        def stage(i, carry):
            rows = pl.ds(pl.multiple_of(i * dot_rows, dot_rows), dot_rows)
            y_ref[rows, :] = u_ref[rows, :].astype(F32)
            return carry

        lax.fori_loop(0, n_dot * S5_SUB, stage, 0)

        def regroup(i, carry):
            for s in range(S5_SUB):
                u2_ref[dot_rows_of(i), s * LANES:(s + 1) * LANES] = y_ref[token_rows_of(i, s), :].astype(BF16)
            return carry

        lax.fori_loop(0, n_dot, regroup, 0)

    def increments(i, carry):
        rows = dot_rows_of(i)
        v_ref[rows, :] = jnp.dot(u2_ref[rows, :], r_ref[0, 0], preferred_element_type=F32)
        return carry

    lax.fori_loop(0, n_dot, increments, 0)

    def cmul_add(ar, ai, cr, ci, xr, xi):
        return ar + cr * xr - ci * xi, ai + cr * xi + ci * xr

    def run(reverse):
        tab = tab_ref.at[0, 0]
        last = 0 if reverse else SUBLANES - 1
        first_row = lax.broadcasted_iota(jnp.int32, (SUBLANES, S5_HALF), 0) == (SUBLANES - 1 - last)

        def segment(bases, n_groups, carry):
            def body(i, carry):
                gi = (n_groups - 1 - i) if reverse else i
                out = []
                for base, (cre, cim) in zip(bases, carry):
                    rows = pl.ds(pl.multiple_of(base + gi * SUBLANES, SUBLANES), SUBLANES)
                    re = v_ref[rows, :S5_HALF]
                    im = v_ref[rows, S5_HALF:]
                    for lvl, shift in enumerate((1, 2, 4)):
                        sh = (SUBLANES - shift) if reverse else shift
                        re, im = cmul_add(re, im, tab[2 * lvl], tab[2 * lvl + 1],
                                          pltpu.roll(re, sh, axis=0), pltpu.roll(im, sh, axis=0))
                    re, im = cmul_add(re, im, tab[6], tab[7], cre, cim)
                    sh1 = (SUBLANES - 1) if reverse else 1
                    v_ref[rows, :S5_HALF] = jnp.where(first_row, cre, pltpu.roll(re, sh1, axis=0))
                    v_ref[rows, S5_HALF:] = jnp.where(first_row, cim, pltpu.roll(im, sh1, axis=0))
                    out.append((jnp.broadcast_to(re[last:last + 1, :], re.shape),
                                jnp.broadcast_to(im[last:last + 1, :], im.shape)))
                return tuple(out)
            return lax.fori_loop(0, n_groups, body, carry)

        zero = jnp.zeros((SUBLANES, S5_HALF), F32)
        carry = tuple((zero, zero) for _ in range(n_batch))
        carry = segment([n_batch * rows_x + bi * rows_c for bi in range(n_batch)], rows_c // SUBLANES, carry)
        segment([bi * rows_x for bi in range(n_batch)], rows_x // SUBLANES, carry)

        def outputs(i, carry):
            rows = dot_rows_of(i)
            y = jnp.dot(u2_ref[rows, :], m_ref[0, 0], preferred_element_type=F32)
            y = y + jnp.dot(v_ref[rows, :].astype(BF16), o_ref[0, 0], preferred_element_type=F32)
            for s in range(S5_SUB):
                part = y[:, s * LANES:(s + 1) * LANES]
                if reverse:
                    y_ref[token_rows_of(i, s), :] += part
                else:
                    y_ref[token_rows_of(i, s), :] = part
            return carry

        lax.fori_loop(0, n_dot, outputs, 0)

    @pl.when(d == 0)
    def _():
        run(False)

    @pl.when(d == 1)
    def _():
        run(True)


def _s5_scan(z, rmat, mmat, omat, tabs, b, s, tc, col_u):
    r = z.shape[0]
    rc = r // S5_SUB
    n_blocks = rmat.shape[1]
    width = S5_SUB * LANES
    dot_rows = max(n for n in range(16, 641, 16) if rc % n == 0)
    mat_spec = pl.BlockSpec((1, 1, width, width), lambda cb, d: (d, cb, 0, 0), pipeline_mode=pl.Buffered(1))
    return pl.pallas_call(
        functools.partial(_s5_kernel, b, s // S5_SUB, tc // S5_SUB, dot_rows),
        out_shape=jax.ShapeDtypeStruct((r, n_blocks * LANES), F32),
        grid=(n_blocks, 2),
        in_specs=[
            pl.BlockSpec((r, LANES), lambda cb, d: (0, col_u // LANES + cb), pipeline_mode=pl.Buffered(1)),
            mat_spec, mat_spec, mat_spec,
            pl.BlockSpec((1, 1, 8, SUBLANES, S5_HALF), lambda cb, d: (d, cb, 0, 0, 0)),
        ],
        out_specs=pl.BlockSpec((r, LANES), lambda cb, d: (0, cb)),
        scratch_shapes=[pltpu.VMEM((rc, width), BF16), pltpu.VMEM((rc, width), F32)],
        compiler_params=_cparams(("parallel", "arbitrary")),
        name="s5_scan",
    )(z, rmat, mmat, omat, tabs)


def _glu_kernel(y_ref, u_ref, d_ref, w_ref, b_ref, o_ref, g_ref):
    tm = y_ref.shape[0]

    def body(r, carry):
        rows = pl.ds(pl.multiple_of(r * ROW_CHUNK, ROW_CHUNK), ROW_CHUNK)
        y = y_ref[rows, :] + d_ref[...] * u_ref[rows, :].astype(F32)
        g_ref[rows, :] = jax.nn.gelu(y).astype(BF16)
        return carry

    lax.fori_loop(0, tm // ROW_CHUNK, body, 0)
    g = g_ref[...]
    gate = jax.nn.sigmoid(jnp.dot(g, w_ref[...], preferred_element_type=F32) + b_ref[...])
    o_ref[...] = (g.astype(F32) * gate).astype(BF16)


def _s5_glu(y, z, d_skip, w_glu, b_glu, tm, col_u, n_row_tiles):
    width = y.shape[1]
    return pl.pallas_call(
        _glu_kernel,
        out_shape=jax.ShapeDtypeStruct((n_row_tiles * tm, width), BF16),
        grid=(n_row_tiles,),
        in_specs=[
            pl.BlockSpec((tm, width), lambda i: (i, 0)),
            pl.BlockSpec((tm, width), lambda i: (i, col_u // width)),
            pl.BlockSpec((1, width), lambda i: (0, 0)),
            pl.BlockSpec((width, width), lambda i: (0, 0)),
            pl.BlockSpec((1, width), lambda i: (0, 0)),
        ],
        out_specs=pl.BlockSpec((tm, width), lambda i: (i, 0)),
        scratch_shapes=[pltpu.VMEM((tm, width), BF16)],
        compiler_params=_cparams(("parallel",)),
        name="s5_glu",
    )(y, z, d_skip, w_glu, b_glu)


def _merge_kernel(n_x_tiles, ax_ref, ac_ref, bx_ref, bc_ref, c_ref, ga_ref, gb_ref, gc_ref, w_ref, o_ref):
    def combine(a_ref, b_ref):
        acc = None
        for r, (br_ref, gate_ref) in enumerate(((a_ref, ga_ref), (b_ref, gb_ref), (c_ref, gc_ref))):
            proj = jnp.dot(br_ref[...], w_ref[r], preferred_element_type=F32)
            term = jax.nn.sigmoid(gate_ref[...].astype(F32)) * proj
            acc = term if acc is None else acc + term
        o_ref[...] = acc.astype(BF16)

    @pl.when(pl.program_id(0) < n_x_tiles)
    def _():
        combine(ax_ref, bx_ref)

    @pl.when(pl.program_id(0) >= n_x_tiles)
    def _():
        combine(ac_ref, bc_ref)


def _merge(a_x, a_c, b_x, b_c, cc, z, w_branch, tm, tn, col_g, n_row_tiles, n_x_tiles):
    width = a_x.shape[1]
    d = w_branch.shape[2]

    def gate_spec(r):
        return pl.BlockSpec((tm, tn), lambda i, j: (i, (col_g + r * d) // tn + j))

    ax_spec, ac_spec = _stream_specs((tm, width), n_x_tiles, 0, lambda j: 0)
    return pl.pallas_call(
        functools.partial(_merge_kernel, n_x_tiles),
        out_shape=jax.ShapeDtypeStruct((n_row_tiles * tm, d), BF16),
        grid=(n_row_tiles, d // tn),
        in_specs=[
            ax_spec, ac_spec, ax_spec, ac_spec,
            pl.BlockSpec((tm, width), lambda i, j: (i, 0)),
            gate_spec(0), gate_spec(1), gate_spec(2),
            pl.BlockSpec((N_BRANCH, width, tn), lambda i, j: (0, 0, j)),
        ],
        out_specs=pl.BlockSpec((tm, tn), lambda i, j: (i, j)),
        compiler_params=_cparams(("parallel", "arbitrary")),
        name="merge",
    )(a_x, a_c, b_x, b_c, cc, z, z, z, w_branch)


def _resid_kernel(m_ref, w_ref, x_ref, al_ref, o_ref):
    o_ref[...] = x_ref[...] + al_ref[0] * jnp.dot(m_ref[...], w_ref[...], preferred_element_type=F32)


def _out_proj_residual(m, w, xs, mod3, tm, tn, n_row_tiles, n_x_tiles, tiles_per_batch, ctx_row, k_alpha):
    kdim, d = w.shape
    nt = d // tn

    def alpha_map(i, j):
        r = jnp.where(i < n_x_tiles, i // tiles_per_batch, ctx_row)
        return (r * N_MOD + k_alpha, 0, j)

    return pl.pallas_call(
        _resid_kernel,
        out_shape=jax.ShapeDtypeStruct((n_row_tiles * tm, d), F32),
        grid=(n_row_tiles, nt),
        in_specs=[
            pl.BlockSpec((tm, kdim), lambda i, j: (i, 0)),
            pl.BlockSpec((kdim, tn), lambda i, j: (0, j)),
            pl.BlockSpec((tm, tn), lambda i, j: (i, j)),
            pl.BlockSpec((1, 1, tn), alpha_map),
        ],
        out_specs=pl.BlockSpec((tm, tn), lambda i, j: (i, j)),
        compiler_params=_cparams(("parallel", "arbitrary")),
        name="out_proj",
    )(m, w, xs, mod3)


def _ff1_kernel(x_ref, g_ref, sh_ref, sc_ref, w_ref, h_ref, xn_ref):
    @pl.when(pl.program_id(1) == 0)
    def _():
        _norm_mod_rows(x_ref, g_ref, sh_ref, sc_ref, xn_ref)

    a = jnp.maximum(jnp.dot(xn_ref[...], w_ref[...], preferred_element_type=F32), 0.0)
    h_ref[...] = (a * a).astype(BF16)


def _ff1(xs, g, mod3, w, tm, tn, n_row_tiles, n_x_tiles, tiles_per_batch, ctx_row):
    d, dff = w.shape
    return pl.pallas_call(
        _ff1_kernel,
        out_shape=jax.ShapeDtypeStruct((n_row_tiles * tm, dff), BF16),
        grid=(n_row_tiles, dff // tn),
        in_specs=[
            pl.BlockSpec((tm, d), lambda i, j: (i, 0)),
            pl.BlockSpec((1, d), lambda i, j: (0, 0)),
            pl.BlockSpec((1, 1, d), _mod_row_map(n_x_tiles, tiles_per_batch, ctx_row, 3)),
            pl.BlockSpec((1, 1, d), _mod_row_map(n_x_tiles, tiles_per_batch, ctx_row, 4)),
            pl.BlockSpec((d, tn), lambda i, j: (0, j)),
        ],
        out_specs=pl.BlockSpec((tm, tn), lambda i, j: (i, j)),
        scratch_shapes=[pltpu.VMEM((tm, d), BF16)],
        compiler_params=_cparams(("parallel", "arbitrary")),
        name="ff1",
    )(xs, g, mod3, mod3, w)


def _ff2_kernel(h_ref, w_ref, x_ref, al_ref, o_ref):
    k = pl.program_id(2)
    part = jnp.dot(h_ref[...], w_ref[...], preferred_element_type=F32)

    @pl.when(k == 0)
    def _():
        o_ref[...] = part

    @pl.when(k > 0)
    def _():
        o_ref[...] += part

    @pl.when(k == pl.num_programs(2) - 1)
    def _():
        o_ref[...] = x_ref[...] + al_ref[0] * o_ref[...]


def _ff2(h, w, xs, mod3, tm, tn, tk, n_row_tiles, n_x_tiles, tiles_per_batch, ctx_row):
    dff, d = w.shape

    def alpha_map(i, j, k):
        r = jnp.where(i < n_x_tiles, i // tiles_per_batch, ctx_row)
        return (r * N_MOD + 5, 0, j)

    return pl.pallas_call(
        _ff2_kernel,
        out_shape=jax.ShapeDtypeStruct((n_row_tiles * tm, d), F32),
        grid=(n_row_tiles, d // tn, dff // tk),
        in_specs=[
            pl.BlockSpec((tm, tk), lambda i, j, k: (i, k)),
            pl.BlockSpec((tk, tn), lambda i, j, k: (k, j)),
            pl.BlockSpec((tm, tn), lambda i, j, k: (i, j)),
            pl.BlockSpec((1, 1, tn), alpha_map),
        ],
        out_specs=pl.BlockSpec((tm, tn), lambda i, j, k: (i, j)),
        compiler_params=_cparams(("parallel", "parallel", "arbitrary")),
        name="ff2",
    )(h, w, xs, mod3)


def _rope_partner():
    j = np.arange(MLA_ROPE)
    quarter = MLA_ROPE // 4
    return np.where((j // quarter) % 2 == 0, j + quarter, j - quarter)


def _rope_table(s, tm):
    pos = jnp.arange(s)
    row = (pos // GRID_W).astype(F32)
    col = (pos % GRID_W).astype(F32)
    n_freq = MLA_ROPE // 4
    inv_freq = ROPE_THETA ** (-jnp.arange(n_freq, dtype=F32) / n_freq)
    ang_r = row[:, None] * inv_freq
    ang_c = col[:, None] * inv_freq
    cos = jnp.concatenate([jnp.cos(ang_r)] * 2 + [jnp.cos(ang_c)] * 2, axis=-1)
    sin = jnp.concatenate([-jnp.sin(ang_r), jnp.sin(ang_r), -jnp.sin(ang_c), jnp.sin(ang_c)], axis=-1)
    ident = jnp.concatenate([jnp.ones((tm, MLA_ROPE), F32), jnp.zeros((tm, MLA_ROPE), F32)], axis=-1)
    return jnp.concatenate([jnp.concatenate([cos, sin], axis=-1), ident], axis=0)


def _pack_w_in(w_in, b_in, gate_b, sizes, n_pad):
    bounds = np.cumsum((0,) + sizes)
    seg = [slice(int(bounds[i]), int(bounds[i + 1])) for i in range(len(sizes))]
    partner = _rope_partner()
    order = (0, 1, 2, 3, 5, 6, 8, 9)
    w_parts = [w_in[:, seg[i]] for i in order]
    b_parts = [b_in[seg[i]] for i in order]
    kpe_w = w_in[:, seg[7]]
    kpe_b = b_in[seg[7]]
    w_parts += [kpe_w, kpe_w[:, partner]]
    b_parts += [kpe_b, kpe_b[partner]]
    offs = {}
    pos = 0
    for name, part in zip(("q", "k", "v", "o", "qa", "kva", "u", "gates", "kpe", "kpe_sw"), w_parts):
        offs[name] = pos
        pos += part.shape[1]
    pad = n_pad - pos
    w_parts.append(jnp.zeros((w_in.shape[0], pad), w_in.dtype))
    b_parts.append(jnp.zeros((pad,), b_in.dtype))
    w = jnp.concatenate(w_parts, axis=1).astype(BF16)
    b = jnp.concatenate(b_parts)[None, :]
    n_g = sizes[4]
    wg = jnp.concatenate([w_in[:, seg[4]], jnp.zeros((w_in.shape[0], LANES - n_g), w_in.dtype)], axis=1).astype(BF16)
    bg = jnp.concatenate([b_in[seg[4]] + gate_b.reshape(-1), jnp.zeros((LANES - n_g,), F32)])[None, :]
    return w, b, wg, bg, offs


def _pack_mla(w_uq, w_ukv, qn_g, kn_g):
    partner = _rope_partner()
    lora = w_uq.shape[0]
    wq = w_uq.reshape(lora, MLA_HEADS, MLA_DQK)
    wq = jnp.concatenate([wq, wq[:, :, MLA_NOPE + partner]], axis=-1).reshape(lora, MLA_HEADS * MLA_SLAB)
    wkv = w_ukv.reshape(w_ukv.shape[0], MLA_HEADS, -1)
    wkv = jnp.concatenate([wkv[:, :, :MLA_NOPE].reshape(lora, -1), wkv[:, :, MLA_NOPE:].reshape(lora, -1)], axis=-1)

    def gains(g):
        return jnp.concatenate([g, g[MLA_NOPE + partner]])[None, :]

    return wq.astype(BF16), wkv.astype(BF16), gains(qn_g), gains(kn_g)


def _pack_s5(a_re, a_im, log_dt, b_re, b_im, c_re, c_im):
    n_dir, n_groups, n_state = a_re.shape
    gc = b_re.shape[-1]
    nb = n_groups // S5_BLOCK_GROUPS
    lam_re = jnp.minimum(a_re.astype(F32), -1e-4)
    lam_im = a_im.astype(F32)
    dt = jnp.exp(log_dt.astype(F32))[..., None]

    def pole_power(k):
        mag = jnp.exp(k * lam_re * dt)
        return mag * jnp.cos(k * lam_im * dt), mag * jnp.sin(k * lam_im * dt)

    bar_re, bar_im = pole_power(1.0)
    den = lam_re * lam_re + lam_im * lam_im
    f_re = ((bar_re - 1.0) * lam_re + bar_im * lam_im) / den
    f_im = (bar_im * lam_re - (bar_re - 1.0) * lam_im) / den
    bb_re = f_re[..., None] * b_re.astype(F32) - f_im[..., None] * b_im.astype(F32)
    bb_im = f_re[..., None] * b_im.astype(F32) + f_im[..., None] * b_re.astype(F32)
    eye = jnp.eye(S5_BLOCK_GROUPS, dtype=F32)
    sub = S5_SUB
    c_re = c_re.astype(F32)
    c_im = c_im.astype(F32)

    tau = jnp.arange(sub + 1, dtype=F32)[:, None, None, None]
    p_re, p_im = pole_power(tau)
    w_re = p_re[:sub, ..., None] * bb_re - p_im[:sub, ..., None] * bb_im
    w_im = p_re[:sub, ..., None] * bb_im + p_im[:sub, ..., None] * bb_re
    taps = (jnp.einsum('dgcn,zdgnk->zdgck', c_re, w_re, precision=HIGHEST)
            - jnp.einsum('dgcn,zdgnk->zdgck', c_im, w_im, precision=HIGHEST))
    q_re = p_re[1:, :, :, None, :]
    q_im = p_im[1:, :, :, None, :]
    o_re = c_re[None] * q_re - c_im[None] * q_im
    o_im = -(c_re[None] * q_im + c_im[None] * q_re)

    t = np.arange(sub)
    rmats, mmats, omats = [], [], []
    for d in range(n_dir):
        if d == 0:
            lag = (t[None, :] - t[:, None])[None] == t[:, None, None]
            to_exit = (sub - 1 - t)[None, :] == t[:, None]
            age = t[None, :] == t[:, None]
        else:
            lag = (t[:, None] - t[None, :])[None] == t[:, None, None]
            to_exit = t[None, :] == t[:, None]
            age = (sub - 1 - t)[None, :] == t[:, None]
        lag, to_exit, age = (jnp.asarray(a, F32) for a in (lag, to_exit, age))

        def blocks(a):
            return a.reshape((sub, nb, S5_BLOCK_GROUPS) + a.shape[2:])

        mc = jnp.einsum('zst,zbgck->bsgktc', lag, blocks(taps[:, d]), precision=HIGHEST)
        mm = mc[:, :, :, :, :, None, :] * eye[None, None, :, None, None, :, None]
        mmats.append(mm.astype(BF16).reshape(nb, sub * LANES, sub * LANES))
        rr = []
        for w in (w_re, w_im):
            rc_ = jnp.einsum('zs,zbgnk->bsgkn', to_exit, blocks(w[:, d]), precision=HIGHEST)
            rr.append((rc_[:, :, :, :, None, :] * eye[None, None, :, None, :, None]).astype(BF16)
                      .reshape(nb, sub * LANES, S5_HALF))
        rmats.append(jnp.concatenate(rr, axis=-1))
        oo = []
        for o in (o_re, o_im):
            oc = jnp.einsum('zt,zbgcn->bgntc', age, blocks(o[:, d]), precision=HIGHEST)
            oo.append((oc[:, :, :, :, None, :] * eye[None, :, None, None, :, None]).astype(BF16)
                      .reshape(nb, S5_HALF, sub * LANES))
        omats.append(jnp.concatenate(oo, axis=-2))

    def per_block(a):
        return a.reshape(a.shape[:-2] + (nb, S5_BLOCK_GROUPS * n_state))

    rows = jnp.arange(SUBLANES)
    tabs = []
    for d in range(n_dir):
        per_dir = []
        for shift in (1, 2, 4):
            keep = ((rows <= SUBLANES - 1 - shift) if d == 1 else (rows >= shift))[None, :, None]
            s_re, s_im = pole_power(float(shift * sub))
            per_dir += [jnp.where(keep, per_block(s_re[d])[:, None, :], 0.0),
                        jnp.where(keep, per_block(s_im[d])[:, None, :], 0.0)]
        expo = (((SUBLANES - rows) if d == 1 else (rows + 1)) * sub).astype(F32)
        s_re, s_im = pole_power(expo[:, None, None, None])
        per_dir += [jnp.moveaxis(per_block(s_re[:, d]), 0, 1), jnp.moveaxis(per_block(s_im[:, d]), 0, 1)]
        tabs.append(jnp.stack(per_dir, axis=1))
    return jnp.stack(rmats), jnp.stack(mmats), jnp.stack(omats), jnp.stack(tabs).astype(F32)


def _tri_matrices():
    t = np.arange(ML_CHUNK)
    return jnp.asarray((t[None, :] <= t[:, None]).astype(np.float32), dtype=BF16)


def kernel(x, c, ctx, c_ctx, w_mod, b_mod, norm_g, w_in, b_in, ml_gate_b, ml_norm_g, mla_qa_g, mla_kva_g, mla_w_uq, mla_w_ukv, mla_qn_g, mla_kn_g, s5_a_re, s5_a_im, s5_log_dt, s5_b_re, s5_b_im, s5_c_re, s5_c_im, s5_d, s5_w_glu, s5_b_glu, w_branch, w_out, w_ff1, w_ff2):
    b, s, d = x.shape
    tc = ctx.shape[1]
    depth = w_mod.shape[0]
    dv = ml_norm_g.shape[2]
    dk = dv // 2
    lora = mla_qa_g.shape[1]
    s5_width = s5_d.shape[1]
    branch_w = w_branch.shape[2]
    sizes = (ML_HEADS * dk, ML_HEADS * dk, ML_HEADS * dv, ML_HEADS * dv, 4 * ML_HEADS, lora, lora, MLA_ROPE,
             s5_width, N_BRANCH * d)
    assert sum(sizes) == w_in.shape[2] and b + 1 <= SUBLANES
    assert s % ML_CHUNK == 0 and tc % ML_CHUNK == 0 and branch_w == ML_HEADS * dv == MLA_HEADS * LANES == s5_width

    n_x = b * s
    n_c = b * tc
    tm = _row_tile(s, n_c)
    n_x_tiles = n_x // tm
    n_tiles = n_x_tiles + n_c // tm
    tiles_per_batch = s // tm
    tile_args = (n_x_tiles, tiles_per_batch, b)

    xs = jnp.concatenate([x.reshape(n_x, d), ctx.reshape(n_c, d)], axis=0)
    cc = jnp.concatenate([c, c_ctx[None, :], jnp.zeros((SUBLANES - b - 1, d), F32)], axis=0)
    mod = _modulation(cc, w_mod, b_mod)
    tm_q = min(tm, 512)
    tab = _rope_table(s, tm_q)
    tri = _tri_matrices()
    n_used = sum(sizes) - sizes[4] + MLA_ROPE
    tn_in = 1280
    n_pad = -(-n_used // tn_in) * tn_in

    for l in range(depth):
        with_ctx_out = l < depth - 1
        mod3 = mod[l].reshape(SUBLANES * N_MOD, 1, d)
        w_p, b_p, wg, bg, offs = _pack_w_in(w_in[l], b_in[l], ml_gate_b[l], sizes, n_pad)
        z, gz = _in_proj(xs, norm_g[l, 0][None, :], mod3, w_p, b_p, wg, bg, tm, tn_in, *tile_args)

        a_x, a_c = _mlstm(z, gz, tri, ml_norm_g[l], b, s, tc, dk, dv,
                          (offs["q"], offs["k"], offs["v"], offs["o"]), with_ctx_out)

        wq, wkv, gq, gk = _pack_mla(mla_w_uq[l], mla_w_ukv[l], mla_qn_g[l], mla_kn_g[l])
        qo, ko, vo = _mla_proj(z, tab, mla_qa_g[l][None, :], mla_kva_g[l][None, :], wq, wkv, gq, gk, tm_q,
                               (offs["qa"], offs["kva"], offs["kpe"]), n_x // tm_q, s // tm_q)
        tq = min(256, s)
        b_x = _attention(qo, ko, vo, b, s, tc, tq, True)

        rmat, mmat, omat, tabs = _pack_s5(s5_a_re[l], s5_a_im[l], s5_log_dt[l], s5_b_re[l], s5_b_im[l],
                                          s5_c_re[l], s5_c_im[l])
        y = _s5_scan(z, rmat, mmat, omat, tabs, b, s, tc, offs["u"])
        n_out_tiles = n_tiles if with_ctx_out else n_x_tiles
        c_all = _s5_glu(y, z, s5_d[l][None, :], s5_w_glu[l].astype(BF16), s5_b_glu[l][None, :], tm, offs["u"],
                        n_out_tiles)

        if with_ctx_out:
            b_c = _attention(qo, ko, vo, b, s, tc, min(tq, tc), False)
        else:
            a_c, b_c = a_x, b_x
        merged = _merge(a_x, a_c, b_x, b_c, c_all, z, w_branch[l].astype(BF16), tm, 512, offs["gates"],
                        n_out_tiles, n_x_tiles)
        xs1 = _out_proj_residual(merged, w_out[l].astype(BF16), xs, mod3, tm, 1024, n_out_tiles, *tile_args, 2)
        hid = _ff1(xs1, norm_g[l, 1][None, :], mod3, w_ff1[l].astype(BF16), tm, 1024, n_out_tiles, *tile_args)
        xs = _ff2(hid, w_ff2[l].astype(BF16), xs1, mod3, tm, 1024, 2048, n_out_tiles, *tile_args)

    return xs.reshape(b, s, d)
```

```python
import functools
import math

import jax
import jax.numpy as jnp
import numpy as np
from jax import lax
from jax.experimental import pallas as pl
from jax.experimental.pallas import tpu as pltpu

F32 = jnp.float32
BF16 = jnp.bfloat16
HIGHEST = lax.Precision.HIGHEST

N_MOD = 6
N_BRANCH = 3
ML_HEADS = 4
MLA_HEADS = 8
MLA_NOPE = 128
MLA_ROPE = 64
MLA_DQK = MLA_NOPE + MLA_ROPE
MLA_SLAB = 256
GRID_W = 64
ROPE_THETA = 10000.0
S5_GROUP = 16
S5_STATE = 64
S5_BLOCK_GROUPS = 8
EPS = 1e-6
NEG_BIG = -1e30

LANES = 128
SUBLANES = 8
VMEM_LIMIT = 56 * 1024 * 1024

ML_CHUNK = 256
S5_SUB = 8
ATTN_KEY_CHUNK = 512
Q_PRESCALE = MLA_DQK ** -0.5 * math.log2(math.e)
ROW_CHUNK = 64


def _cparams(sem):
    return pltpu.CompilerParams(dimension_semantics=sem, vmem_limit_bytes=VMEM_LIMIT)


def _row_tile(n_x_rows_per_batch, n_ctx_rows):
    tm = 1024
    while n_x_rows_per_batch % tm or n_ctx_rows % tm:
        tm //= 2
    return tm


def _mod_kernel(c_ref, w_ref, b_ref, o_ref):
    s = c_ref[...]
    s = s * jax.nn.sigmoid(s)
    o_ref[0] = jnp.dot(s.astype(BF16), w_ref[0].astype(BF16), preferred_element_type=F32) + b_ref[0]


def _modulation(cc, w_mod, b_mod):
    n_layers, d, n = w_mod.shape
    tn = 1024
    return pl.pallas_call(
        _mod_kernel,
        out_shape=jax.ShapeDtypeStruct((n_layers, SUBLANES, n), F32),
        grid=(n_layers, n // tn),
        in_specs=[
            pl.BlockSpec((SUBLANES, d), lambda l, j: (0, 0)),
            pl.BlockSpec((1, d, tn), lambda l, j: (l, 0, j)),
            pl.BlockSpec((1, 1, tn), lambda l, j: (l, 0, j)),
        ],
        out_specs=pl.BlockSpec((1, SUBLANES, tn), lambda l, j: (l, 0, j)),
        compiler_params=_cparams(("parallel", "parallel")),
        name="adaln_mod",
    )(cc, w_mod, b_mod.reshape(n_layers, 1, n))


def _norm_mod_rows(x_ref, g_ref, sh_ref, sc_ref, xn_ref):
    tm = x_ref.shape[0]
    g = g_ref[...]
    sc = 1.0 + sc_ref[0]
    sh = sh_ref[0]

    def body(r, carry):
        rows = pl.ds(pl.multiple_of(r * ROW_CHUNK, ROW_CHUNK), ROW_CHUNK)
        x = x_ref[rows, :]
        ms = jnp.mean(x * x, axis=-1, keepdims=True)
        y = x * lax.rsqrt(ms + EPS) * g
        xn_ref[rows, :] = (y * sc + sh).astype(BF16)
        return carry

    lax.fori_loop(0, tm // ROW_CHUNK, body, 0)


def _mod_row_map(n_x_tiles, tiles_per_batch, ctx_row, k):
    def index_map(i, j):
        r = jnp.where(i < n_x_tiles, i // tiles_per_batch, ctx_row)
        return (r * N_MOD + k, 0, 0)
    return index_map


def _stream_specs(block, n_x_tiles, ctx_tile0, col_map):
    x_spec = pl.BlockSpec(block, lambda i, *r: (jnp.minimum(i, n_x_tiles - 1), col_map(*r)))
    c_spec = pl.BlockSpec(block, lambda i, *r: (ctx_tile0 + jnp.maximum(i - n_x_tiles, 0), col_map(*r)),
                          pipeline_mode=pl.Buffered(1))
    return x_spec, c_spec


def _in_kernel(x_ref, g_ref, sh_ref, sc_ref, w_ref, b_ref, wg_ref, bg_ref, z_ref, gz_ref, xn_ref):
    @pl.when(pl.program_id(1) == 0)
    def _():
        _norm_mod_rows(x_ref, g_ref, sh_ref, sc_ref, xn_ref)
        gz_ref[...] = jnp.dot(xn_ref[...], wg_ref[...], preferred_element_type=F32) + bg_ref[...]

    z_ref[...] = (jnp.dot(xn_ref[...], w_ref[...], preferred_element_type=F32) + b_ref[...]).astype(BF16)


def _in_proj(xs, g, mod3, w, b, wg, bg, tm, tn, n_x_tiles, tiles_per_batch, ctx_row):
    r, d = xs.shape
    nz = w.shape[1]
    return pl.pallas_call(
        _in_kernel,
        out_shape=(jax.ShapeDtypeStruct((r, nz), BF16), jax.ShapeDtypeStruct((r, LANES), F32)),
        grid=(r // tm, nz // tn),
        in_specs=[
            pl.BlockSpec((tm, d), lambda i, j: (i, 0)),
            pl.BlockSpec((1, d), lambda i, j: (0, 0)),
            pl.BlockSpec((1, 1, d), _mod_row_map(n_x_tiles, tiles_per_batch, ctx_row, 0)),
            pl.BlockSpec((1, 1, d), _mod_row_map(n_x_tiles, tiles_per_batch, ctx_row, 1)),
            pl.BlockSpec((d, tn), lambda i, j: (0, j)),
            pl.BlockSpec((1, tn), lambda i, j: (0, j)),
            pl.BlockSpec((d, LANES), lambda i, j: (0, 0)),
            pl.BlockSpec((1, LANES), lambda i, j: (0, 0)),
        ],
        out_specs=(
            pl.BlockSpec((tm, tn), lambda i, j: (i, j)),
            pl.BlockSpec((tm, LANES), lambda i, j: (i, 0)),
        ),
        scratch_shapes=[pltpu.VMEM((tm, d), BF16)],
        compiler_params=_cparams(("parallel", "arbitrary")),
        name="in_proj",
    )(xs, g, mod3, mod3, w, b, wg, bg)


def _log_sigmoid(x):
    return jnp.minimum(x, 0.0) - jnp.log1p(jnp.exp(-jnp.abs(x)))


def _split3(a):
    hi = a.astype(BF16)
    r1 = a - hi.astype(F32)
    mid = r1.astype(BF16)
    lo = (r1 - mid.astype(F32)).astype(BF16)
    return hi, mid, lo


def _ml_chunk(q, k, v, lg_c, lg_r, carry, reverse, scale):
    c_mat, n_vec, m = carry
    length = q.shape[0]
    gi = 2 if reverse else 0
    ci = 5 if reverse else 4
    li_c = lg_c[:, gi:gi + 1]
    cum_c = lg_c[:, ci:ci + 1]
    li_r = lg_r[gi:gi + 1, :]
    cum_r = lg_r[ci:ci + 1, :]
    total = jnp.sum(lg_r[gi + 1:gi + 2, :], axis=-1, keepdims=True)

    t_idx = lax.broadcasted_iota(jnp.int32, (length, length), 0)
    s_idx = lax.broadcasted_iota(jnp.int32, (length, length), 1)
    keep = (s_idx >= t_idx) if reverse else (s_idx <= t_idx)
    log_w = jnp.where(keep, cum_c - cum_r + li_r, NEG_BIG)
    log_inter = cum_c + m
    m_t = jnp.maximum(log_inter, jnp.max(log_w, axis=-1, keepdims=True))
    w_inter = jnp.exp(log_inter - m_t) * scale
    qk = lax.dot_general(q, k, (((1,), (1,)), ((), ())), preferred_element_type=F32)
    s = qk * (jnp.exp(log_w - m_t) * scale)
    num = w_inter * jnp.dot(q, c_mat.astype(BF16), preferred_element_type=F32)
    num = num + jnp.dot(s.astype(BF16), v, preferred_element_type=F32)
    qn = jnp.sum(q.astype(F32) * n_vec, axis=-1, keepdims=True)
    den = w_inter * qn + jnp.sum(s, axis=-1, keepdims=True)
    h = num / jnp.maximum(jnp.abs(den), jnp.exp(-m_t))

    log_end_r = total - cum_r + li_r
    m_new = jnp.maximum(total + m, jnp.max(log_end_r, axis=-1, keepdims=True))
    decay = jnp.exp(total + m - m_new)
    w_end_c = jnp.exp(total - cum_c + li_c - m_new)
    kw = k.astype(F32) * w_end_c
    c_new = decay * c_mat + lax.dot_general(kw.astype(BF16), v, (((0,), (0,)), ((), ())),
                                            preferred_element_type=F32)
    n_new = decay * n_vec + jnp.sum(kw, axis=0, keepdims=True)
    return h, (c_new, n_new, m_new)


def _mlstm_kernel(with_ctx_out, qx, kx, vx, ox, qc, kc, vc, oc, gx, gc, tri_ref, ng_ref, *rest):
    if with_ctx_out:
        ax_ref, ac_ref, lgx_c, lgx_r, lgc_c, lgc_r, hx, hc = rest
    else:
        ax_ref, lgx_c, lgx_r, lgc_c, lgc_r, hx, hc = rest
        ac_ref = None
    head = pl.program_id(1)
    dk = qx.shape[1]
    scale = dk ** -0.5
    n_x_chunks = qx.shape[0] // ML_CHUNK
    n_c_chunks = qc.shape[0] // ML_CHUNK

    r_idx = lax.broadcasted_iota(jnp.int32, (LANES, LANES), 0)
    c_idx = lax.broadcasted_iota(jnp.int32, (LANES, LANES), 1)
    sel = jnp.where((r_idx == c_idx * ML_HEADS + head) & (c_idx < 4), 1.0, 0.0).astype(BF16)
    tri = tri_ref[...]

    def prep(g_ref, lg_c_ref, lg_r_ref):
        def body(i, carry):
            rows = pl.ds(pl.multiple_of(i * ML_CHUNK, ML_CHUNK), ML_CHUNK)
            gs = sum(jnp.dot(p, sel, preferred_element_type=F32) for p in _split3(g_ref[rows, :]))
            col = lax.broadcasted_iota(jnp.int32, gs.shape, 1)
            lg = jnp.where(col % 2 == 1, _log_sigmoid(gs), gs)
            pre = sum(jnp.dot(tri, p, preferred_element_type=F32) for p in _split3(lg))
            suf = pre[ML_CHUNK - 1:ML_CHUNK, :] - pre + lg
            lg = jnp.where(col == 4, pltpu.roll(pre, 3, axis=1), jnp.where(col == 5, pltpu.roll(suf, 2, axis=1), lg))
            lg_c_ref[rows, :] = lg
            lg_r_ref[i] = lg.T[:SUBLANES, :]
            return carry
        n_chunks = g_ref.shape[0] // ML_CHUNK
        lax.fori_loop(0, n_chunks, body, 0, unroll=2 if n_chunks % 2 == 0 else 1)

    prep(gx, lgx_c, lgx_r)
    prep(gc, lgc_c, lgc_r)

    for d in range(2):
        reverse = d == 1

        def step(q_ref, k_ref, v_ref, lg_c_ref, lg_r_ref, h_ref, ci, carry):
            rows = pl.ds(pl.multiple_of(ci * ML_CHUNK, ML_CHUNK), ML_CHUNK)
            h, carry = _ml_chunk(q_ref[rows, :], k_ref[rows, :], v_ref[rows, :], lg_c_ref[rows, :],
                                 lg_r_ref[ci], carry, reverse, scale)
            if reverse:
                h_ref[rows, :] += h
            else:
                h_ref[rows, :] = h
            return carry

        carry = (jnp.zeros((dk, vx.shape[1]), F32), jnp.zeros((1, dk), F32), jnp.zeros((1, 1), F32))

        def ctx_body(i, carry):
            ci = (n_c_chunks - 1 - i) if reverse else i
            return step(qc, kc, vc, lgc_c, lgc_r, hc, ci, carry)

        def x_body(i, carry):
            ci = (n_x_chunks - 1 - i) if reverse else i
            return step(qx, kx, vx, lgx_c, lgx_r, hx, ci, carry)

        carry = lax.fori_loop(0, n_c_chunks, ctx_body, carry)
        lax.fori_loop(0, n_x_chunks, x_body, carry)

    ng = ng_ref[0]

    def finish(h_ref, o_ref, a_ref):
        def body(i, carry):
            rows = pl.ds(pl.multiple_of(i * ML_CHUNK, ML_CHUNK), ML_CHUNK)
            h = h_ref[rows, :]
            hn = h * lax.rsqrt(jnp.mean(h * h, axis=-1, keepdims=True) + EPS) * ng
            a_ref[rows, :] = (hn * jax.nn.sigmoid(o_ref[rows, :].astype(F32))).astype(BF16)
            return carry
        lax.fori_loop(0, h_ref.shape[0] // ML_CHUNK, body, 0)

    finish(hx, ox, ax_ref)
    if with_ctx_out:
        finish(hc, oc, ac_ref)


def _mlstm(z, gz, tri, ml_norm_g, b, s, tc, dk, dv, cols, with_ctx_out):
    n_x = b * s
    cq, ck, cv, co = cols
    ctx0 = n_x // tc

    def xspec(width, col0):
        return pl.BlockSpec((s, width), lambda i, h: (i, col0 // width + h))

    def cspec(width, col0):
        return pl.BlockSpec((tc, width), lambda i, h: (ctx0 + i, col0 // width + h))

    out_shape = [jax.ShapeDtypeStruct((n_x, ML_HEADS * dv), BF16)]
    out_specs = [pl.BlockSpec((s, dv), lambda i, h: (i, h))]
    if with_ctx_out:
        out_shape.append(jax.ShapeDtypeStruct((b * tc, ML_HEADS * dv), BF16))
        out_specs.append(pl.BlockSpec((tc, dv), lambda i, h: (i, h)))
    res = pl.pallas_call(
        functools.partial(_mlstm_kernel, with_ctx_out),
        out_shape=tuple(out_shape),
        grid=(b, ML_HEADS),
        in_specs=[
            xspec(dk, cq), xspec(dk, ck), xspec(dv, cv), xspec(dv, co),
            cspec(dk, cq), cspec(dk, ck), cspec(dv, cv), cspec(dv, co),
            pl.BlockSpec((s, LANES), lambda i, h: (i, 0)),
            pl.BlockSpec((tc, LANES), lambda i, h: (ctx0 + i, 0)),
            pl.BlockSpec((ML_CHUNK, ML_CHUNK), lambda i, h: (0, 0)),
            pl.BlockSpec((1, 1, dv), lambda i, h: (h, 0, 0)),
        ],
        out_specs=tuple(out_specs),
        scratch_shapes=[
            pltpu.VMEM((s, LANES), F32), pltpu.VMEM((s // ML_CHUNK, SUBLANES, ML_CHUNK), F32),
            pltpu.VMEM((tc, LANES), F32), pltpu.VMEM((tc // ML_CHUNK, SUBLANES, ML_CHUNK), F32),
            pltpu.VMEM((s, dv), F32), pltpu.VMEM((tc, dv), F32),
        ],
        compiler_params=_cparams(("parallel", "parallel")),
        name="mlstm",
    )(z, z, z, z, z, z, z, z, gz, gz, tri, ml_norm_g.reshape(ML_HEADS, 1, dv))
    return res if with_ctx_out else (res[0], None)


def _mla_proj_kernel(qa_ref, kva_ref, kpe_ref, tab_ref, qag_ref, kvag_ref, wq_ref, wkv_ref,
                     gq_ref, gk_ref, q_ref, k_ref, v_ref):
    def normed(a_ref, g_ref):
        a = a_ref[...].astype(F32)
        return (a * lax.rsqrt(jnp.mean(a * a, axis=-1, keepdims=True) + EPS) * g_ref[...]).astype(BF16)

    q_all = jnp.dot(normed(qa_ref, qag_ref), wq_ref[...], preferred_element_type=F32)
    kv_all = jnp.dot(normed(kva_ref, kvag_ref), wkv_ref[...], preferred_element_type=F32)
    tab = tab_ref[...]
    lane = lax.broadcasted_iota(jnp.int32, tab.shape, 1)
    first_half = lane < MLA_ROPE
    gq = gq_ref[...]
    gk = gk_ref[...]
    inv_dqk = 1.0 / MLA_DQK

    kpe = kpe_ref[...].astype(F32)
    ss_kpe = jnp.sum(jnp.where(first_half, kpe * kpe, 0.0), axis=-1, keepdims=True)
    kpe_t = kpe * (tab * gk[:, LANES:])
    kpe_rot = jnp.where(first_half, kpe_t + pltpu.roll(kpe_t, MLA_ROPE, axis=1), 0.0)

    for h in range(MLA_HEADS):
        qn = q_all[:, h * MLA_SLAB:h * MLA_SLAB + LANES]
        qp = q_all[:, h * MLA_SLAB + LANES:(h + 1) * MLA_SLAB]
        ss = jnp.sum(qn * qn, axis=-1, keepdims=True) + jnp.sum(jnp.where(first_half, qp * qp, 0.0), axis=-1,
                                                                keepdims=True)
        r = lax.rsqrt(ss * inv_dqk + EPS) * Q_PRESCALE
        qp_t = qp * (tab * gq[:, LANES:])
        qp_rot = qp_t + pltpu.roll(qp_t, MLA_ROPE, axis=1)
        q_ref[:, h * MLA_SLAB:h * MLA_SLAB + LANES] = (qn * r * gq[:, :LANES]).astype(BF16)
        q_ref[:, h * MLA_SLAB + LANES:(h + 1) * MLA_SLAB] = (qp_rot * r).astype(BF16)

        kn = kv_all[:, h * LANES:(h + 1) * LANES]
        rk = lax.rsqrt((jnp.sum(kn * kn, axis=-1, keepdims=True) + ss_kpe) * inv_dqk + EPS)
        k_ref[:, h * MLA_SLAB:h * MLA_SLAB + LANES] = (kn * rk * gk[:, :LANES]).astype(BF16)
        k_ref[:, h * MLA_SLAB + LANES:(h + 1) * MLA_SLAB] = (kpe_rot * rk).astype(BF16)

    ones_col = jnp.where(lane == 0, 1.0, 0.0).astype(BF16)
    for h in range(MLA_HEADS):
        v_ref[:, h * MLA_SLAB:h * MLA_SLAB + LANES] = kv_all[:, (MLA_HEADS + h) * LANES:(MLA_HEADS + h + 1) * LANES
                                                             ].astype(BF16)
        v_ref[:, h * MLA_SLAB + LANES:(h + 1) * MLA_SLAB] = ones_col


def _mla_proj(z, tab, qag, kvag, wq, wkv, gq, gk, tm, cols, n_x_tiles, tab_tiles):
    r = z.shape[0]
    cqa, ckva, ckpe = cols
    lora = qag.shape[1]
    hs = MLA_HEADS * MLA_SLAB
    hv = MLA_HEADS * LANES
    return pl.pallas_call(
        _mla_proj_kernel,
        out_shape=(jax.ShapeDtypeStruct((r, hs), BF16), jax.ShapeDtypeStruct((r, hs), BF16),
                   jax.ShapeDtypeStruct((r, hs), BF16)),
        grid=(r // tm,),
        in_specs=[
            pl.BlockSpec((tm, lora), lambda i: (i, cqa // lora)),
            pl.BlockSpec((tm, lora), lambda i: (i, ckva // lora)),
            pl.BlockSpec((tm, LANES), lambda i: (i, ckpe // LANES)),
            pl.BlockSpec((tm, LANES), lambda i: (jnp.where(i < n_x_tiles, i % tab_tiles, tab_tiles), 0)),
            pl.BlockSpec((1, lora), lambda i: (0, 0)),
            pl.BlockSpec((1, lora), lambda i: (0, 0)),
            pl.BlockSpec((lora, hs), lambda i: (0, 0)),
            pl.BlockSpec((lora, 2 * hv), lambda i: (0, 0)),
            pl.BlockSpec((1, MLA_SLAB), lambda i: (0, 0)),
            pl.BlockSpec((1, MLA_SLAB), lambda i: (0, 0)),
        ],
        out_specs=(pl.BlockSpec((tm, hs), lambda i: (i, 0)), pl.BlockSpec((tm, hs), lambda i: (i, 0)),
                   pl.BlockSpec((tm, hs), lambda i: (i, 0))),
        compiler_params=_cparams(("parallel",)),
        name="mla_qkv",
    )(z, z, z, tab, qag, kvag, wq, wkv, gq, gk)


def _attn_kernel(n_kv, tq, q_ref, *refs):
    k_refs = refs[:n_kv]
    v_refs = refs[n_kv:2 * n_kv]
    o_ref, s0_ref, s1_ref, m0_ref, m1_ref = refs[2 * n_kv:]
    slots = ((s0_ref, m0_ref), (s1_ref, m1_ref))
    n_tiles = q_ref.shape[0] // tq
    chunks = []
    col = 0
    for kv, k_ref in enumerate(k_refs):
        n_keys = k_ref.shape[0]
        step = min(ATTN_KEY_CHUNK, n_keys)
        for off in range(0, n_keys, step):
            chunks.append((kv, off, col, step))
            col += step

    def scores(t, slot):
        s_ref, m_ref = slots[slot]
        rows = pl.ds(pl.multiple_of(t * tq, tq), tq)
        q = q_ref[rows, :]
        run = None
        for kv, off, c0, size in chunks:
            s = lax.dot_general(q, k_refs[kv][off:off + size, :], (((1,), (1,)), ((), ())),
                                preferred_element_type=F32)
            s_ref[:, c0:c0 + size] = s
            for lb in range(size // LANES):
                blk = s[:, lb * LANES:(lb + 1) * LANES]
                run = blk if run is None else jnp.maximum(run, blk)
        m_ref[...] = run

    def finish(t, slot):
        s_ref, m_ref = slots[slot]
        rows = pl.ds(pl.multiple_of(t * tq, tq), tq)
        m = jnp.max(m_ref[...], axis=-1, keepdims=True)
        acc = None
        for kv, off, c0, size in chunks:
            p = jnp.exp2(s_ref[:, c0:c0 + size] - m).astype(BF16)
            pv = jnp.dot(p, v_refs[kv][off:off + size, :], preferred_element_type=F32)
            acc = pv if acc is None else acc + pv
        o_ref[rows, :] = (acc[:, :LANES] / acc[:, LANES:LANES + 1]).astype(BF16)

    scores(0, 0)

    def body(k, carry):
        scores(2 * k + 1, 1)
        finish(2 * k, 0)
        scores(jnp.minimum(2 * k + 2, n_tiles - 1), 0)
        finish(2 * k + 1, 1)
        return carry

    lax.fori_loop(0, n_tiles // 2, body, 0)
    if n_tiles % 2:
        finish(n_tiles - 1, 0)


def _attention(qo, ko, vo, b, s, tc, tq, latent):
    n_x = b * s
    ctx0 = n_x // tc
    cspec = pl.BlockSpec((tc, MLA_SLAB), lambda i, h: (ctx0 + i, h))
    xspec = pl.BlockSpec((s, MLA_SLAB), lambda i, h: (i, h))
    if latent:
        n_q, n_keys = s, s + tc
        in_specs = [xspec, cspec, xspec, cspec, xspec]
        args = (qo, ko, ko, vo, vo)
    else:
        n_q, n_keys = tc, tc
        in_specs = [cspec, cspec, cspec]
        args = (qo, ko, vo)
    return pl.pallas_call(
        functools.partial(_attn_kernel, (len(args) - 1) // 2, tq),
        out_shape=jax.ShapeDtypeStruct((b * n_q, MLA_HEADS * LANES), BF16),
        grid=(b, MLA_HEADS),
        in_specs=in_specs,
        out_specs=pl.BlockSpec((n_q, LANES), lambda i, h: (i, h)),
        scratch_shapes=[pltpu.VMEM((tq, n_keys), F32), pltpu.VMEM((tq, n_keys), F32),
                        pltpu.VMEM((tq, LANES), F32), pltpu.VMEM((tq, LANES), F32)],
        compiler_params=_cparams(("parallel", "parallel")),
        name="attn_latent" if latent else "attn_ctx",
    )(*args)


S5_HALF = S5_BLOCK_GROUPS * S5_STATE


def _s5_kernel(n_batch, rows_x, rows_c, dot_rows, u_ref, r_ref, m_ref, o_ref, tab_ref, y_ref, u2_ref, v_ref):
    d = pl.program_id(1)
    n_dot = u2_ref.shape[0] // dot_rows

    def dot_rows_of(i):
        return pl.ds(pl.multiple_of(i * dot_rows, dot_rows), dot_rows)

    def token_rows_of(i, s):
        return pl.ds(i * (dot_rows * S5_SUB) + s, dot_rows, stride=S5_SUB)

    @pl.when(d == 0)
    def _():
        def stage(i, carry):
            rows = pl.ds(pl.multiple_of(i * dot_rows, dot_rows), dot_rows)
            y_ref[rows, :] = u_ref[rows, :].astype(F32)
            return carry

        lax.fori_loop(0, n_dot * S5_SUB, stage, 0)

        def regroup(i, carry):
            for s in range(S5_SUB):
                u2_ref[dot_rows_of(i), s * LANES:(s + 1) * LANES] = y_ref[token_rows_of(i, s), :].astype(BF16)
            return carry

        lax.fori_loop(0, n_dot, regroup, 0)

    def increments(i, carry):
        rows = dot_rows_of(i)
        v_ref[rows, :] = jnp.dot(u2_ref[rows, :], r_ref[0, 0], preferred_element_type=F32)
        return carry

    lax.fori_loop(0, n_dot, increments, 0)

    def cmul_add(ar, ai, cr, ci, xr, xi):
        return ar + cr * xr - ci * xi, ai + cr * xi + ci * xr

    def run(reverse):
        tab = tab_ref.at[0, 0]
        last = 0 if reverse else SUBLANES - 1
        first_row = lax.broadcasted_iota(jnp.int32, (SUBLANES, S5_HALF), 0) == (SUBLANES - 1 - last)

        def segment(bases, n_groups, carry):
            def body(i, carry):
                gi = (n_groups - 1 - i) if reverse else i
                out = []
                for base, (cre, cim) in zip(bases, carry):
                    rows = pl.ds(pl.multiple_of(base + gi * SUBLANES, SUBLANES), SUBLANES)
                    re = v_ref[rows, :S5_HALF]
                    im = v_ref[rows, S5_HALF:]
                    for lvl, shift in enumerate((1, 2, 4)):
                        sh = (SUBLANES - shift) if reverse else shift
                        re, im = cmul_add(re, im, tab[2 * lvl], tab[2 * lvl + 1],
                                          pltpu.roll(re, sh, axis=0), pltpu.roll(im, sh, axis=0))
                    re, im = cmul_add(re, im, tab[6], tab[7], cre, cim)
                    sh1 = (SUBLANES - 1) if reverse else 1
                    v_ref[rows, :S5_HALF] = jnp.where(first_row, cre, pltpu.roll(re, sh1, axis=0))
                    v_ref[rows, S5_HALF:] = jnp.where(first_row, cim, pltpu.roll(im, sh1, axis=0))
                    out.append((jnp.broadcast_to(re[last:last + 1, :], re.shape),
                                jnp.broadcast_to(im[last:last + 1, :], im.shape)))
                return tuple(out)
            return lax.fori_loop(0, n_groups, body, carry)

        zero = jnp.zeros((SUBLANES, S5_HALF), F32)
        carry = tuple((zero, zero) for _ in range(n_batch))
        carry = segment([n_batch * rows_x + bi * rows_c for bi in range(n_batch)], rows_c // SUBLANES, carry)
        segment([bi * rows_x for bi in range(n_batch)], rows_x // SUBLANES, carry)

        def outputs(i, carry):
            rows = dot_rows_of(i)
            y = jnp.dot(u2_ref[rows, :], m_ref[0, 0], preferred_element_type=F32)
            y = y + jnp.dot(v_ref[rows, :].astype(BF16), o_ref[0, 0], preferred_element_type=F32)
            for s in range(S5_SUB):
                part = y[:, s * LANES:(s + 1) * LANES]
                if reverse:
                    y_ref[token_rows_of(i, s), :] += part
                else:
                    y_ref[token_rows_of(i, s), :] = part
            return carry

        lax.fori_loop(0, n_dot, outputs, 0)

    @pl.when(d == 0)
    def _():
        run(False)

    @pl.when(d == 1)
    def _():
        run(True)


def _s5_scan(z, rmat, mmat, omat, tabs, b, s, tc, col_u):
    r = z.shape[0]
    rc = r // S5_SUB
    n_blocks = rmat.shape[1]
    width = S5_SUB * LANES
    dot_rows = max(n for n in range(16, 641, 16) if rc % n == 0)
    mat_spec = pl.BlockSpec((1, 1, width, width), lambda cb, d: (d, cb, 0, 0), pipeline_mode=pl.Buffered(1))
    return pl.pallas_call(
        functools.partial(_s5_kernel, b, s // S5_SUB, tc // S5_SUB, dot_rows),
        out_shape=jax.ShapeDtypeStruct((r, n_blocks * LANES), F32),
        grid=(n_blocks, 2),
        in_specs=[
            pl.BlockSpec((r, LANES), lambda cb, d: (0, col_u // LANES + cb), pipeline_mode=pl.Buffered(1)),
            mat_spec, mat_spec, mat_spec,
            pl.BlockSpec((1, 1, 8, SUBLANES, S5_HALF), lambda cb, d: (d, cb, 0, 0, 0)),
        ],
        out_specs=pl.BlockSpec((r, LANES), lambda cb, d: (0, cb)),
        scratch_shapes=[pltpu.VMEM((rc, width), BF16), pltpu.VMEM((rc, width), F32)],
        compiler_params=_cparams(("parallel", "arbitrary")),
        name="s5_scan",
    )(z, rmat, mmat, omat, tabs)


def _glu_kernel(y_ref, u_ref, d_ref, w_ref, b_ref, o_ref, g_ref):
    tm = y_ref.shape[0]

    def body(r, carry):
        rows = pl.ds(pl.multiple_of(r * ROW_CHUNK, ROW_CHUNK), ROW_CHUNK)
        y = y_ref[rows, :] + d_ref[...] * u_ref[rows, :].astype(F32)
        g_ref[rows, :] = jax.nn.gelu(y).astype(BF16)
        return carry

    lax.fori_loop(0, tm // ROW_CHUNK, body, 0)
    g = g_ref[...]
    gate = jax.nn.sigmoid(jnp.dot(g, w_ref[...], preferred_element_type=F32) + b_ref[...])
    o_ref[...] = (g.astype(F32) * gate).astype(BF16)


def _s5_glu(y, z, d_skip, w_glu, b_glu, tm, col_u, n_row_tiles):
    width = y.shape[1]
    return pl.pallas_call(
        _glu_kernel,
        out_shape=jax.ShapeDtypeStruct((n_row_tiles * tm, width), BF16),
        grid=(n_row_tiles,),
        in_specs=[
            pl.BlockSpec((tm, width), lambda i: (i, 0)),
            pl.BlockSpec((tm, width), lambda i: (i, col_u // width)),
            pl.BlockSpec((1, width), lambda i: (0, 0)),
            pl.BlockSpec((width, width), lambda i: (0, 0)),
            pl.BlockSpec((1, width), lambda i: (0, 0)),
        ],
        out_specs=pl.BlockSpec((tm, width), lambda i: (i, 0)),
        scratch_shapes=[pltpu.VMEM((tm, width), BF16)],
        compiler_params=_cparams(("parallel",)),
        name="s5_glu",
    )(y, z, d_skip, w_glu, b_glu)


def _merge_kernel(n_x_tiles, ax_ref, ac_ref, bx_ref, bc_ref, c_ref, ga_ref, gb_ref, gc_ref, w_ref, o_ref):
    def combine(a_ref, b_ref):
        acc = None
        for r, (br_ref, gate_ref) in enumerate(((a_ref, ga_ref), (b_ref, gb_ref), (c_ref, gc_ref))):
            proj = jnp.dot(br_ref[...], w_ref[r], preferred_element_type=F32)
            term = jax.nn.sigmoid(gate_ref[...].astype(F32)) * proj
            acc = term if acc is None else acc + term
        o_ref[...] = acc.astype(BF16)

    @pl.when(pl.program_id(0) < n_x_tiles)
    def _():
        combine(ax_ref, bx_ref)

    @pl.when(pl.program_id(0) >= n_x_tiles)
    def _():
        combine(ac_ref, bc_ref)


def _merge(a_x, a_c, b_x, b_c, cc, z, w_branch, tm, tn, col_g, n_row_tiles, n_x_tiles):
    width = a_x.shape[1]
    d = w_branch.shape[2]

    def gate_spec(r):
        return pl.BlockSpec((tm, tn), lambda i, j: (i, (col_g + r * d) // tn + j))

    ax_spec, ac_spec = _stream_specs((tm, width), n_x_tiles, 0, lambda j: 0)
    return pl.pallas_call(
        functools.partial(_merge_kernel, n_x_tiles),
        out_shape=jax.ShapeDtypeStruct((n_row_tiles * tm, d), BF16),
        grid=(n_row_tiles, d // tn),
        in_specs=[
            ax_spec, ac_spec, ax_spec, ac_spec,
            pl.BlockSpec((tm, width), lambda i, j: (i, 0)),
            gate_spec(0), gate_spec(1), gate_spec(2),
            pl.BlockSpec((N_BRANCH, width, tn), lambda i, j: (0, 0, j)),
        ],
        out_specs=pl.BlockSpec((tm, tn), lambda i, j: (i, j)),
        compiler_params=_cparams(("parallel", "arbitrary")),
        name="merge",
    )(a_x, a_c, b_x, b_c, cc, z, z, z, w_branch)


def _resid_kernel(m_ref, w_ref, x_ref, al_ref, o_ref):
    o_ref[...] = x_ref[...] + al_ref[0] * jnp.dot(m_ref[...], w_ref[...], preferred_element_type=F32)


def _out_proj_residual(m, w, xs, mod3, tm, tn, n_row_tiles, n_x_tiles, tiles_per_batch, ctx_row, k_alpha):
    kdim, d = w.shape
    nt = d // tn

    def alpha_map(i, j):
        r = jnp.where(i < n_x_tiles, i // tiles_per_batch, ctx_row)
        return (r * N_MOD + k_alpha, 0, j)

    return pl.pallas_call(
        _resid_kernel,
        out_shape=jax.ShapeDtypeStruct((n_row_tiles * tm, d), F32),
        grid=(n_row_tiles, nt),
        in_specs=[
            pl.BlockSpec((tm, kdim), lambda i, j: (i, 0)),
            pl.BlockSpec((kdim, tn), lambda i, j: (0, j)),
            pl.BlockSpec((tm, tn), lambda i, j: (i, j)),
            pl.BlockSpec((1, 1, tn), alpha_map),
        ],
        out_specs=pl.BlockSpec((tm, tn), lambda i, j: (i, j)),
        compiler_params=_cparams(("parallel", "arbitrary")),
        name="out_proj",
    )(m, w, xs, mod3)


def _ff1_kernel(x_ref, g_ref, sh_ref, sc_ref, w_ref, h_ref, xn_ref):
    @pl.when(pl.program_id(1) == 0)
    def _():
        _norm_mod_rows(x_ref, g_ref, sh_ref, sc_ref, xn_ref)

    a = jnp.maximum(jnp.dot(xn_ref[...], w_ref[...], preferred_element_type=F32), 0.0)
    h_ref[...] = (a * a).astype(BF16)


def _ff1(xs, g, mod3, w, tm, tn, n_row_tiles, n_x_tiles, tiles_per_batch, ctx_row):
    d, dff = w.shape
    return pl.pallas_call(
        _ff1_kernel,
        out_shape=jax.ShapeDtypeStruct((n_row_tiles * tm, dff), BF16),
        grid=(n_row_tiles, dff // tn),
        in_specs=[
            pl.BlockSpec((tm, d), lambda i, j: (i, 0)),
            pl.BlockSpec((1, d), lambda i, j: (0, 0)),
            pl.BlockSpec((1, 1, d), _mod_row_map(n_x_tiles, tiles_per_batch, ctx_row, 3)),
            pl.BlockSpec((1, 1, d), _mod_row_map(n_x_tiles, tiles_per_batch, ctx_row, 4)),
            pl.BlockSpec((d, tn), lambda i, j: (0, j)),
        ],
        out_specs=pl.BlockSpec((tm, tn), lambda i, j: (i, j)),
        scratch_shapes=[pltpu.VMEM((tm, d), BF16)],
        compiler_params=_cparams(("parallel", "arbitrary")),
        name="ff1",
    )(xs, g, mod3, mod3, w)


def _ff2_kernel(h_ref, w_ref, x_ref, al_ref, o_ref):
    k = pl.program_id(2)
    part = jnp.dot(h_ref[...], w_ref[...], preferred_element_type=F32)

    @pl.when(k == 0)
    def _():
        o_ref[...] = part

    @pl.when(k > 0)
    def _():
        o_ref[...] += part

    @pl.when(k == pl.num_programs(2) - 1)
    def _():
        o_ref[...] = x_ref[...] + al_ref[0] * o_ref[...]


def _ff2(h, w, xs, mod3, tm, tn, tk, n_row_tiles, n_x_tiles, tiles_per_batch, ctx_row):
    dff, d = w.shape

    def alpha_map(i, j, k):
        r = jnp.where(i < n_x_tiles, i // tiles_per_batch, ctx_row)
        return (r * N_MOD + 5, 0, j)

    return pl.pallas_call(
        _ff2_kernel,
        out_shape=jax.ShapeDtypeStruct((n_row_tiles * tm, d), F32),
        grid=(n_row_tiles, d // tn, dff // tk),
        in_specs=[
            pl.BlockSpec((tm, tk), lambda i, j, k: (i, k)),
            pl.BlockSpec((tk, tn), lambda i, j, k: (k, j)),
            pl.BlockSpec((tm, tn), lambda i, j, k: (i, j)),
            pl.BlockSpec((1, 1, tn), alpha_map),
        ],
        out_specs=pl.BlockSpec((tm, tn), lambda i, j, k: (i, j)),
        compiler_params=_cparams(("parallel", "parallel", "arbitrary")),
        name="ff2",
    )(h, w, xs, mod3)


def _rope_partner():
    j = np.arange(MLA_ROPE)
    quarter = MLA_ROPE // 4
    return np.where((j // quarter) % 2 == 0, j + quarter, j - quarter)


def _rope_table(s, tm):
    pos = jnp.arange(s)
    row = (pos // GRID_W).astype(F32)
    col = (pos % GRID_W).astype(F32)
    n_freq = MLA_ROPE // 4
    inv_freq = ROPE_THETA ** (-jnp.arange(n_freq, dtype=F32) / n_freq)
    ang_r = row[:, None] * inv_freq
    ang_c = col[:, None] * inv_freq
    cos = jnp.concatenate([jnp.cos(ang_r)] * 2 + [jnp.cos(ang_c)] * 2, axis=-1)
    sin = jnp.concatenate([-jnp.sin(ang_r), jnp.sin(ang_r), -jnp.sin(ang_c), jnp.sin(ang_c)], axis=-1)
    ident = jnp.concatenate([jnp.ones((tm, MLA_ROPE), F32), jnp.zeros((tm, MLA_ROPE), F32)], axis=-1)
    return jnp.concatenate([jnp.concatenate([cos, sin], axis=-1), ident], axis=0)


def _pack_w_in(w_in, b_in, gate_b, sizes, n_pad):
    bounds = np.cumsum((0,) + sizes)
    seg = [slice(int(bounds[i]), int(bounds[i + 1])) for i in range(len(sizes))]
    partner = _rope_partner()
    order = (0, 1, 2, 3, 5, 6, 8, 9)
    w_parts = [w_in[:, seg[i]] for i in order]
    b_parts = [b_in[seg[i]] for i in order]
    kpe_w = w_in[:, seg[7]]
    kpe_b = b_in[seg[7]]
    w_parts += [kpe_w, kpe_w[:, partner]]
    b_parts += [kpe_b, kpe_b[partner]]
    offs = {}
    pos = 0
    for name, part in zip(("q", "k", "v", "o", "qa", "kva", "u", "gates", "kpe", "kpe_sw"), w_parts):
        offs[name] = pos
        pos += part.shape[1]
    pad = n_pad - pos
    w_parts.append(jnp.zeros((w_in.shape[0], pad), w_in.dtype))
    b_parts.append(jnp.zeros((pad,), b_in.dtype))
    w = jnp.concatenate(w_parts, axis=1).astype(BF16)
    b = jnp.concatenate(b_parts)[None, :]
    n_g = sizes[4]
    wg = jnp.concatenate([w_in[:, seg[4]], jnp.zeros((w_in.shape[0], LANES - n_g), w_in.dtype)], axis=1).astype(BF16)
    bg = jnp.concatenate([b_in[seg[4]] + gate_b.reshape(-1), jnp.zeros((LANES - n_g,), F32)])[None, :]
    return w, b, wg, bg, offs


def _pack_mla(w_uq, w_ukv, qn_g, kn_g):
    partner = _rope_partner()
    lora = w_uq.shape[0]
    wq = w_uq.reshape(lora, MLA_HEADS, MLA_DQK)
    wq = jnp.concatenate([wq, wq[:, :, MLA_NOPE + partner]], axis=-1).reshape(lora, MLA_HEADS * MLA_SLAB)
    wkv = w_ukv.reshape(w_ukv.shape[0], MLA_HEADS, -1)
    wkv = jnp.concatenate([wkv[:, :, :MLA_NOPE].reshape(lora, -1), wkv[:, :, MLA_NOPE:].reshape(lora, -1)], axis=-1)

    def gains(g):
        return jnp.concatenate([g, g[MLA_NOPE + partner]])[None, :]

    return wq.astype(BF16), wkv.astype(BF16), gains(qn_g), gains(kn_g)


def _pack_s5(a_re, a_im, log_dt, b_re, b_im, c_re, c_im):
    n_dir, n_groups, n_state = a_re.shape
    gc = b_re.shape[-1]
    nb = n_groups // S5_BLOCK_GROUPS
    lam_re = jnp.minimum(a_re.astype(F32), -1e-4)
    lam_im = a_im.astype(F32)
    dt = jnp.exp(log_dt.astype(F32))[..., None]

    def pole_power(k):
        mag = jnp.exp(k * lam_re * dt)
        return mag * jnp.cos(k * lam_im * dt), mag * jnp.sin(k * lam_im * dt)

    bar_re, bar_im = pole_power(1.0)
    den = lam_re * lam_re + lam_im * lam_im
    f_re = ((bar_re - 1.0) * lam_re + bar_im * lam_im) / den
    f_im = (bar_im * lam_re - (bar_re - 1.0) * lam_im) / den
    bb_re = f_re[..., None] * b_re.astype(F32) - f_im[..., None] * b_im.astype(F32)
    bb_im = f_re[..., None] * b_im.astype(F32) + f_im[..., None] * b_re.astype(F32)
    eye = jnp.eye(S5_BLOCK_GROUPS, dtype=F32)
    sub = S5_SUB

    def per_block(a):
        return a.reshape(a.shape[:-2] + (nb, S5_BLOCK_GROUPS * n_state))

    def block_b(part):
        p = part.reshape(n_dir, nb, S5_BLOCK_GROUPS, n_state, gc)
        m = jnp.einsum('dbgnc,gh->dbgchn', p, eye, precision=HIGHEST)
        return m.reshape(n_dir, nb, LANES, S5_HALF)

    def block_c(part):
        p = part.astype(F32).reshape(n_dir, nb, S5_BLOCK_GROUPS, gc, n_state)
        m = jnp.einsum('dbgcn,gh->dbgnhc', p, eye, precision=HIGHEST)
        return m.reshape(n_dir, nb, S5_HALF, LANES)

    bm_re, bm_im = block_b(bb_re), block_b(bb_im)
    cm_re, cm_im = block_c(c_re), block_c(c_im)

    tau = jnp.arange(sub + 1, dtype=F32)[:, None, None, None]
    p_re, p_im = (per_block(p) for p in pole_power(tau))
    pr, pi = p_re[:sub, :, :, None, :], p_im[:sub, :, :, None, :]
    w_all = jnp.concatenate([bm_re * pr - bm_im * pi, bm_re * pi + bm_im * pr], axis=-1)
    qr, qi = p_re[1:, :, :, :, None], p_im[1:, :, :, :, None]
    o_all = jnp.concatenate([cm_re * qr - cm_im * qi, -(cm_re * qi + cm_im * qr)], axis=-2)
    c_all = jnp.concatenate([cm_re, -cm_im], axis=-2)
    taps = jnp.einsum('zdbkn,dbnc->zdbkc', w_all, c_all, precision=HIGHEST)
    taps = jnp.concatenate([taps, jnp.zeros_like(taps[:1])], axis=0)

    t = np.arange(sub)
    width = sub * LANES
    rmats, mmats, omats = [], [], []
    for d in range(n_dir):
        if d == 0:
            lag = t[None, :] - t[:, None]
            to_exit = sub - 1 - t
            age = t
        else:
            lag = t[:, None] - t[None, :]
            to_exit = t
            age = sub - 1 - t
        lag = np.where(lag >= 0, lag, sub)
        mm = taps[:, d][lag]
        mmats.append(mm.transpose(2, 0, 3, 1, 4).reshape(nb, width, width).astype(BF16))
        rmats.append(w_all[:, d][to_exit].transpose(1, 0, 2, 3).reshape(nb, width, 2 * S5_HALF).astype(BF16))
        omats.append(o_all[:, d][age].transpose(1, 2, 0, 3).reshape(nb, 2 * S5_HALF, width).astype(BF16))

    rows = jnp.arange(SUBLANES)
    tabs = []
    for d in range(n_dir):
        per_dir = []
        for shift in (1, 2, 4):
            keep = ((rows <= SUBLANES - 1 - shift) if d == 1 else (rows >= shift))[None, :, None]
            s_re, s_im = pole_power(float(shift * sub))
            per_dir += [jnp.where(keep, per_block(s_re[d])[:, None, :], 0.0),
                        jnp.where(keep, per_block(s_im[d])[:, None, :], 0.0)]
        expo = (((SUBLANES - rows) if d == 1 else (rows + 1)) * sub).astype(F32)
        s_re, s_im = pole_power(expo[:, None, None, None])
        per_dir += [jnp.moveaxis(per_block(s_re[:, d]), 0, 1), jnp.moveaxis(per_block(s_im[:, d]), 0, 1)]
        tabs.append(jnp.stack(per_dir, axis=1))
    return jnp.stack(rmats), jnp.stack(mmats), jnp.stack(omats), jnp.stack(tabs).astype(F32)


def _tri_matrices():
    t = np.arange(ML_CHUNK)
    return jnp.asarray((t[None, :] <= t[:, None]).astype(np.float32), dtype=BF16)


def kernel(x, c, ctx, c_ctx, w_mod, b_mod, norm_g, w_in, b_in, ml_gate_b, ml_norm_g, mla_qa_g, mla_kva_g, mla_w_uq, mla_w_ukv, mla_qn_g, mla_kn_g, s5_a_re, s5_a_im, s5_log_dt, s5_b_re, s5_b_im, s5_c_re, s5_c_im, s5_d, s5_w_glu, s5_b_glu, w_branch, w_out, w_ff1, w_ff2):
    b, s, d = x.shape
    tc = ctx.shape[1]
    depth = w_mod.shape[0]
    dv = ml_norm_g.shape[2]
    dk = dv // 2
    lora = mla_qa_g.shape[1]
    s5_width = s5_d.shape[1]
    branch_w = w_branch.shape[2]
    sizes = (ML_HEADS * dk, ML_HEADS * dk, ML_HEADS * dv, ML_HEADS * dv, 4 * ML_HEADS, lora, lora, MLA_ROPE,
             s5_width, N_BRANCH * d)
    assert sum(sizes) == w_in.shape[2] and b + 1 <= SUBLANES
    assert s % ML_CHUNK == 0 and tc % ML_CHUNK == 0 and branch_w == ML_HEADS * dv == MLA_HEADS * LANES == s5_width

    n_x = b * s
    n_c = b * tc
    tm = _row_tile(s, n_c)
    n_x_tiles = n_x // tm
    n_tiles = n_x_tiles + n_c // tm
    tiles_per_batch = s // tm
    tile_args = (n_x_tiles, tiles_per_batch, b)

    xs = jnp.concatenate([x.reshape(n_x, d), ctx.reshape(n_c, d)], axis=0)
    cc = jnp.concatenate([c, c_ctx[None, :], jnp.zeros((SUBLANES - b - 1, d), F32)], axis=0)
    mod = _modulation(cc, w_mod, b_mod)
    tm_q = min(tm, 512)
    tab = _rope_table(s, tm_q)
    tri = _tri_matrices()
    n_used = sum(sizes) - sizes[4] + MLA_ROPE
    tn_in = 1280
    n_pad = -(-n_used // tn_in) * tn_in

    for l in range(depth):
        with_ctx_out = l < depth - 1
        mod3 = mod[l].reshape(SUBLANES * N_MOD, 1, d)
        w_p, b_p, wg, bg, offs = _pack_w_in(w_in[l], b_in[l], ml_gate_b[l], sizes, n_pad)
        z, gz = _in_proj(xs, norm_g[l, 0][None, :], mod3, w_p, b_p, wg, bg, tm, tn_in, *tile_args)

        a_x, a_c = _mlstm(z, gz, tri, ml_norm_g[l], b, s, tc, dk, dv,
                          (offs["q"], offs["k"], offs["v"], offs["o"]), with_ctx_out)

        wq, wkv, gq, gk = _pack_mla(mla_w_uq[l], mla_w_ukv[l], mla_qn_g[l], mla_kn_g[l])
        qo, ko, vo = _mla_proj(z, tab, mla_qa_g[l][None, :], mla_kva_g[l][None, :], wq, wkv, gq, gk, tm_q,
                               (offs["qa"], offs["kva"], offs["kpe"]), n_x // tm_q, s // tm_q)
        tq = min(256, s)
        b_x = _attention(qo, ko, vo, b, s, tc, tq, True)

        rmat, mmat, omat, tabs = _pack_s5(s5_a_re[l], s5_a_im[l], s5_log_dt[l], s5_b_re[l], s5_b_im[l],
                                          s5_c_re[l], s5_c_im[l])
        y = _s5_scan(z, rmat, mmat, omat, tabs, b, s, tc, offs["u"])
        n_out_tiles = n_tiles if with_ctx_out else n_x_tiles
        c_all = _s5_glu(y, z, s5_d[l][None, :], s5_w_glu[l].astype(BF16), s5_b_glu[l][None, :], tm, offs["u"],
                        n_out_tiles)

        if with_ctx_out:
            b_c = _attention(qo, ko, vo, b, s, tc, min(tq, tc), False)
        else:
            a_c, b_c = a_x, b_x
        merged = _merge(a_x, a_c, b_x, b_c, c_all, z, w_branch[l].astype(BF16), tm, 512, offs["gates"],
                        n_out_tiles, n_x_tiles)
        xs1 = _out_proj_residual(merged, w_out[l].astype(BF16), xs, mod3, tm, 1024, n_out_tiles, *tile_args, 2)
        hid = _ff1(xs1, norm_g[l, 1][None, :], mod3, w_ff1[l].astype(BF16), tm, 1024, n_out_tiles, *tile_args)
        xs = _ff2(hid, w_ff2[l].astype(BF16), xs1, mod3, tm, 1024, 2048, n_out_tiles, *tile_args)

    return xs.reshape(b, s, d)
```

```python
import functools
import math

import jax
import jax.numpy as jnp
import numpy as np
from jax import lax
from jax.experimental import pallas as pl
from jax.experimental.pallas import tpu as pltpu

F32 = jnp.float32
BF16 = jnp.bfloat16
HIGHEST = lax.Precision.HIGHEST

N_MOD = 6
N_BRANCH = 3
ML_HEADS = 4
MLA_HEADS = 8
MLA_NOPE = 128
MLA_ROPE = 64
MLA_DQK = MLA_NOPE + MLA_ROPE
MLA_SLAB = 256
GRID_W = 64
ROPE_THETA = 10000.0
S5_GROUP = 16
S5_STATE = 64
S5_BLOCK_GROUPS = 8
EPS = 1e-6
NEG_BIG = -1e30

LANES = 128
SUBLANES = 8
VMEM_LIMIT = 56 * 1024 * 1024

ML_CHUNK = 256
S5_SUB = 8
ATTN_KEY_CHUNK = 512
Q_PRESCALE = MLA_DQK ** -0.5 * math.log2(math.e)
ROW_CHUNK = 64


def _cparams(sem):
    return pltpu.CompilerParams(dimension_semantics=sem, vmem_limit_bytes=VMEM_LIMIT)


def _row_tile(n_x_rows_per_batch, n_ctx_rows):
    tm = 1024
    while n_x_rows_per_batch % tm or n_ctx_rows % tm:
        tm //= 2
    return tm


def _mod_kernel(c_ref, w_ref, b_ref, o_ref):
    s = c_ref[...]
    s = s * jax.nn.sigmoid(s)
    o_ref[0] = jnp.dot(s.astype(BF16), w_ref[0].astype(BF16), preferred_element_type=F32) + b_ref[0]


def _modulation(cc, w_mod, b_mod):
    n_layers, d, n = w_mod.shape
    tn = 1024
    return pl.pallas_call(
        _mod_kernel,
        out_shape=jax.ShapeDtypeStruct((n_layers, SUBLANES, n), F32),
        grid=(n_layers, n // tn),
        in_specs=[
            pl.BlockSpec((SUBLANES, d), lambda l, j: (0, 0)),
            pl.BlockSpec((1, d, tn), lambda l, j: (l, 0, j)),
            pl.BlockSpec((1, 1, tn), lambda l, j: (l, 0, j)),
        ],
        out_specs=pl.BlockSpec((1, SUBLANES, tn), lambda l, j: (l, 0, j)),
        compiler_params=_cparams(("parallel", "parallel")),
        name="adaln_mod",
    )(cc, w_mod, b_mod.reshape(n_layers, 1, n))


def _norm_mod_rows(x_ref, g_ref, sh_ref, sc_ref, xn_ref):
    tm = x_ref.shape[0]
    g = g_ref[...]
    sc = 1.0 + sc_ref[0]
    sh = sh_ref[0]

    def body(r, carry):
        rows = pl.ds(pl.multiple_of(r * ROW_CHUNK, ROW_CHUNK), ROW_CHUNK)
        x = x_ref[rows, :]
        ms = jnp.mean(x * x, axis=-1, keepdims=True)
        y = x * lax.rsqrt(ms + EPS) * g
        xn_ref[rows, :] = (y * sc + sh).astype(BF16)
        return carry

    lax.fori_loop(0, tm // ROW_CHUNK, body, 0)


def _mod_row_map(n_x_tiles, tiles_per_batch, ctx_row, k):
    def index_map(i, j):
        r = jnp.where(i < n_x_tiles, i // tiles_per_batch, ctx_row)
        return (r * N_MOD + k, 0, 0)
    return index_map


def _stream_specs(block, n_x_tiles, ctx_tile0, col_map):
    x_spec = pl.BlockSpec(block, lambda i, *r: (jnp.minimum(i, n_x_tiles - 1), col_map(*r)))
    c_spec = pl.BlockSpec(block, lambda i, *r: (ctx_tile0 + jnp.maximum(i - n_x_tiles, 0), col_map(*r)),
                          pipeline_mode=pl.Buffered(1))
    return x_spec, c_spec


def _in_kernel(x_ref, g_ref, sh_ref, sc_ref, w_ref, b_ref, wg_ref, bg_ref, z_ref, gz_ref, xn_ref):
    @pl.when(pl.program_id(1) == 0)
    def _():
        _norm_mod_rows(x_ref, g_ref, sh_ref, sc_ref, xn_ref)
        gz_ref[...] = jnp.dot(xn_ref[...], wg_ref[...], preferred_element_type=F32) + bg_ref[...]

    z_ref[...] = (jnp.dot(xn_ref[...], w_ref[...], preferred_element_type=F32) + b_ref[...]).astype(BF16)


def _in_proj(xs, g, mod3, w, b, wg, bg, tm, tn, n_x_tiles, tiles_per_batch, ctx_row):
    r, d = xs.shape
    nz = w.shape[1]
    return pl.pallas_call(
        _in_kernel,
        out_shape=(jax.ShapeDtypeStruct((r, nz), BF16), jax.ShapeDtypeStruct((r, LANES), F32)),
        grid=(r // tm, nz // tn),
        in_specs=[
            pl.BlockSpec((tm, d), lambda i, j: (i, 0)),
            pl.BlockSpec((1, d), lambda i, j: (0, 0)),
            pl.BlockSpec((1, 1, d), _mod_row_map(n_x_tiles, tiles_per_batch, ctx_row, 0)),
            pl.BlockSpec((1, 1, d), _mod_row_map(n_x_tiles, tiles_per_batch, ctx_row, 1)),
            pl.BlockSpec((d, tn), lambda i, j: (0, j)),
            pl.BlockSpec((1, tn), lambda i, j: (0, j)),
            pl.BlockSpec((d, LANES), lambda i, j: (0, 0)),
            pl.BlockSpec((1, LANES), lambda i, j: (0, 0)),
        ],
        out_specs=(
            pl.BlockSpec((tm, tn), lambda i, j: (i, j)),
            pl.BlockSpec((tm, LANES), lambda i, j: (i, 0)),
        ),
        scratch_shapes=[pltpu.VMEM((tm, d), BF16)],
        compiler_params=_cparams(("parallel", "arbitrary")),
        name="in_proj",
    )(xs, g, mod3, mod3, w, b, wg, bg)


def _log_sigmoid(x):
    return jnp.minimum(x, 0.0) - jnp.log1p(jnp.exp(-jnp.abs(x)))


def _split3(a):
    hi = a.astype(BF16)
    r1 = a - hi.astype(F32)
    mid = r1.astype(BF16)
    lo = (r1 - mid.astype(F32)).astype(BF16)
    return hi, mid, lo


def _ml_chunk(q, k, v, lg_c, lg_r, carry, reverse, scale):
    c_mat, n_vec, m = carry
    length = q.shape[0]
    gi = 2 if reverse else 0
    ci = 5 if reverse else 4
    li_c = lg_c[:, gi:gi + 1]
    cum_c = lg_c[:, ci:ci + 1]
    li_r = lg_r[gi:gi + 1, :]
    cum_r = lg_r[ci:ci + 1, :]
    total = jnp.sum(lg_r[gi + 1:gi + 2, :], axis=-1, keepdims=True)

    t_idx = lax.broadcasted_iota(jnp.int32, (length, length), 0)
    s_idx = lax.broadcasted_iota(jnp.int32, (length, length), 1)
    keep = (s_idx >= t_idx) if reverse else (s_idx <= t_idx)
    log_w = jnp.where(keep, cum_c - cum_r + li_r, NEG_BIG)
    log_inter = cum_c + m
    m_t = jnp.maximum(log_inter, jnp.max(log_w, axis=-1, keepdims=True))
    w_inter = jnp.exp(log_inter - m_t) * scale
    qk = lax.dot_general(q, k, (((1,), (1,)), ((), ())), preferred_element_type=F32)
    s = qk * (jnp.exp(log_w - m_t) * scale)
    num = w_inter * jnp.dot(q, c_mat.astype(BF16), preferred_element_type=F32)
    num = num + jnp.dot(s.astype(BF16), v, preferred_element_type=F32)
    qn = jnp.sum(q.astype(F32) * n_vec, axis=-1, keepdims=True)
    den = w_inter * qn + jnp.sum(s, axis=-1, keepdims=True)
    h = num / jnp.maximum(jnp.abs(den), jnp.exp(-m_t))

    log_end_r = total - cum_r + li_r
    m_new = jnp.maximum(total + m, jnp.max(log_end_r, axis=-1, keepdims=True))
    decay = jnp.exp(total + m - m_new)
    w_end_c = jnp.exp(total - cum_c + li_c - m_new)
    kw = k.astype(F32) * w_end_c
    c_new = decay * c_mat + lax.dot_general(kw.astype(BF16), v, (((0,), (0,)), ((), ())),
                                            preferred_element_type=F32)
    n_new = decay * n_vec + jnp.sum(kw, axis=0, keepdims=True)
    return h, (c_new, n_new, m_new)


def _mlstm_kernel(with_ctx_out, qx, kx, vx, ox, qc, kc, vc, oc, gx, gc, tri_ref, ng_ref, *rest):
    if with_ctx_out:
        ax_ref, ac_ref, lgx_c, lgx_r, lgc_c, lgc_r, hx, hc = rest
    else:
        ax_ref, lgx_c, lgx_r, lgc_c, lgc_r, hx, hc = rest
        ac_ref = None
    head = pl.program_id(1)
    dk = qx.shape[1]
    scale = dk ** -0.5
    n_x_chunks = qx.shape[0] // ML_CHUNK
    n_c_chunks = qc.shape[0] // ML_CHUNK

    r_idx = lax.broadcasted_iota(jnp.int32, (LANES, LANES), 0)
    c_idx = lax.broadcasted_iota(jnp.int32, (LANES, LANES), 1)
    sel = jnp.where((r_idx == c_idx * ML_HEADS + head) & (c_idx < 4), 1.0, 0.0).astype(BF16)
    tri = tri_ref[...]

    def prep(g_ref, lg_c_ref, lg_r_ref):
        def body(i, carry):
            rows = pl.ds(pl.multiple_of(i * ML_CHUNK, ML_CHUNK), ML_CHUNK)
            gs = sum(jnp.dot(p, sel, preferred_element_type=F32) for p in _split3(g_ref[rows, :]))
            col = lax.broadcasted_iota(jnp.int32, gs.shape, 1)
            lg = jnp.where(col % 2 == 1, _log_sigmoid(gs), gs)
            pre = sum(jnp.dot(tri, p, preferred_element_type=F32) for p in _split3(lg))
            suf = pre[ML_CHUNK - 1:ML_CHUNK, :] - pre + lg
            lg = jnp.where(col == 4, pltpu.roll(pre, 3, axis=1), jnp.where(col == 5, pltpu.roll(suf, 2, axis=1), lg))
            lg_c_ref[rows, :] = lg
            lg_r_ref[i] = lg.T[:SUBLANES, :]
            return carry
        n_chunks = g_ref.shape[0] // ML_CHUNK
        lax.fori_loop(0, n_chunks, body, 0, unroll=2 if n_chunks % 2 == 0 else 1)

    prep(gx, lgx_c, lgx_r)
    prep(gc, lgc_c, lgc_r)

    for d in range(2):
        reverse = d == 1

        def step(q_ref, k_ref, v_ref, lg_c_ref, lg_r_ref, h_ref, ci, carry):
            rows = pl.ds(pl.multiple_of(ci * ML_CHUNK, ML_CHUNK), ML_CHUNK)
            h, carry = _ml_chunk(q_ref[rows, :], k_ref[rows, :], v_ref[rows, :], lg_c_ref[rows, :],
                                 lg_r_ref[ci], carry, reverse, scale)
            if reverse:
                h_ref[rows, :] += h
            else:
                h_ref[rows, :] = h
            return carry

        carry = (jnp.zeros((dk, vx.shape[1]), F32), jnp.zeros((1, dk), F32), jnp.zeros((1, 1), F32))

        def ctx_body(i, carry):
            ci = (n_c_chunks - 1 - i) if reverse else i
            return step(qc, kc, vc, lgc_c, lgc_r, hc, ci, carry)

        def x_body(i, carry):
            ci = (n_x_chunks - 1 - i) if reverse else i
            return step(qx, kx, vx, lgx_c, lgx_r, hx, ci, carry)

        carry = lax.fori_loop(0, n_c_chunks, ctx_body, carry)
        lax.fori_loop(0, n_x_chunks, x_body, carry)

    ng = ng_ref[0]

    def finish(h_ref, o_ref, a_ref):
        def body(i, carry):
            rows = pl.ds(pl.multiple_of(i * ML_CHUNK, ML_CHUNK), ML_CHUNK)
            h = h_ref[rows, :]
            hn = h * lax.rsqrt(jnp.mean(h * h, axis=-1, keepdims=True) + EPS) * ng
            a_ref[rows, :] = (hn * jax.nn.sigmoid(o_ref[rows, :].astype(F32))).astype(BF16)
            return carry
        lax.fori_loop(0, h_ref.shape[0] // ML_CHUNK, body, 0)

    finish(hx, ox, ax_ref)
    if with_ctx_out:
        finish(hc, oc, ac_ref)


def _mlstm(z, gz, tri, ml_norm_g, b, s, tc, dk, dv, cols, with_ctx_out):
    n_x = b * s
    cq, ck, cv, co = cols
    ctx0 = n_x // tc

    def xspec(width, col0):
        return pl.BlockSpec((s, width), lambda i, h: (i, col0 // width + h))

    def cspec(width, col0):
        return pl.BlockSpec((tc, width), lambda i, h: (ctx0 + i, col0 // width + h))

    out_shape = [jax.ShapeDtypeStruct((n_x, ML_HEADS * dv), BF16)]
    out_specs = [pl.BlockSpec((s, dv), lambda i, h: (i, h))]
    if with_ctx_out:
        out_shape.append(jax.ShapeDtypeStruct((b * tc, ML_HEADS * dv), BF16))
        out_specs.append(pl.BlockSpec((tc, dv), lambda i, h: (i, h)))
    res = pl.pallas_call(
        functools.partial(_mlstm_kernel, with_ctx_out),
        out_shape=tuple(out_shape),
        grid=(b, ML_HEADS),
        in_specs=[
            xspec(dk, cq), xspec(dk, ck), xspec(dv, cv), xspec(dv, co),
            cspec(dk, cq), cspec(dk, ck), cspec(dv, cv), cspec(dv, co),
            pl.BlockSpec((s, LANES), lambda i, h: (i, 0)),
            pl.BlockSpec((tc, LANES), lambda i, h: (ctx0 + i, 0)),
            pl.BlockSpec((ML_CHUNK, ML_CHUNK), lambda i, h: (0, 0)),
            pl.BlockSpec((1, 1, dv), lambda i, h: (h, 0, 0)),
        ],
        out_specs=tuple(out_specs),
        scratch_shapes=[
            pltpu.VMEM((s, LANES), F32), pltpu.VMEM((s // ML_CHUNK, SUBLANES, ML_CHUNK), F32),
            pltpu.VMEM((tc, LANES), F32), pltpu.VMEM((tc // ML_CHUNK, SUBLANES, ML_CHUNK), F32),
            pltpu.VMEM((s, dv), F32), pltpu.VMEM((tc, dv), F32),
        ],
        compiler_params=_cparams(("parallel", "parallel")),
        name="mlstm",
    )(z, z, z, z, z, z, z, z, gz, gz, tri, ml_norm_g.reshape(ML_HEADS, 1, dv))
    return res if with_ctx_out else (res[0], None)


def _mla_proj_kernel(qa_ref, kva_ref, kpe_ref, tab_ref, qag_ref, kvag_ref, wq_ref, wkv_ref,
                     gq_ref, gk_ref, q_ref, k_ref, v_ref):
    def normed(a_ref, g_ref):
        a = a_ref[...].astype(F32)
        return (a * lax.rsqrt(jnp.mean(a * a, axis=-1, keepdims=True) + EPS) * g_ref[...]).astype(BF16)

    q_all = jnp.dot(normed(qa_ref, qag_ref), wq_ref[...], preferred_element_type=F32)
    kv_all = jnp.dot(normed(kva_ref, kvag_ref), wkv_ref[...], preferred_element_type=F32)
    tab = tab_ref[...]
    lane = lax.broadcasted_iota(jnp.int32, tab.shape, 1)
    first_half = lane < MLA_ROPE
    gq = gq_ref[...]
    gk = gk_ref[...]
    inv_dqk = 1.0 / MLA_DQK

    kpe = kpe_ref[...].astype(F32)
    ss_kpe = jnp.sum(jnp.where(first_half, kpe * kpe, 0.0), axis=-1, keepdims=True)
    kpe_t = kpe * (tab * gk[:, LANES:])
    kpe_rot = jnp.where(first_half, kpe_t + pltpu.roll(kpe_t, MLA_ROPE, axis=1), 0.0)

    for h in range(MLA_HEADS):
        qn = q_all[:, h * MLA_SLAB:h * MLA_SLAB + LANES]
        qp = q_all[:, h * MLA_SLAB + LANES:(h + 1) * MLA_SLAB]
        ss = jnp.sum(qn * qn, axis=-1, keepdims=True) + jnp.sum(jnp.where(first_half, qp * qp, 0.0), axis=-1,
                                                                keepdims=True)
        r = lax.rsqrt(ss * inv_dqk + EPS) * Q_PRESCALE
        qp_t = qp * (tab * gq[:, LANES:])
        qp_rot = qp_t + pltpu.roll(qp_t, MLA_ROPE, axis=1)
        q_ref[:, h * MLA_SLAB:h * MLA_SLAB + LANES] = (qn * r * gq[:, :LANES]).astype(BF16)
        q_ref[:, h * MLA_SLAB + LANES:(h + 1) * MLA_SLAB] = (qp_rot * r).astype(BF16)

        kn = kv_all[:, h * LANES:(h + 1) * LANES]
        rk = lax.rsqrt((jnp.sum(kn * kn, axis=-1, keepdims=True) + ss_kpe) * inv_dqk + EPS)
        k_ref[:, h * MLA_SLAB:h * MLA_SLAB + LANES] = (kn * rk * gk[:, :LANES]).astype(BF16)
        k_ref[:, h * MLA_SLAB + LANES:(h + 1) * MLA_SLAB] = (kpe_rot * rk).astype(BF16)

    ones_col = jnp.where(lane == 0, 1.0, 0.0).astype(BF16)
    for h in range(MLA_HEADS):
        v_ref[:, h * MLA_SLAB:h * MLA_SLAB + LANES] = kv_all[:, (MLA_HEADS + h) * LANES:(MLA_HEADS + h + 1) * LANES
                                                             ].astype(BF16)
        v_ref[:, h * MLA_SLAB + LANES:(h + 1) * MLA_SLAB] = ones_col


def _mla_proj(z, tab, qag, kvag, wq, wkv, gq, gk, tm, cols, n_x_tiles, tab_tiles):
    r = z.shape[0]
    cqa, ckva, ckpe = cols
    lora = qag.shape[1]
    hs = MLA_HEADS * MLA_SLAB
    hv = MLA_HEADS * LANES
    return pl.pallas_call(
        _mla_proj_kernel,
        out_shape=(jax.ShapeDtypeStruct((r, hs), BF16), jax.ShapeDtypeStruct((r, hs), BF16),
                   jax.ShapeDtypeStruct((r, hs), BF16)),
        grid=(r // tm,),
        in_specs=[
            pl.BlockSpec((tm, lora), lambda i: (i, cqa // lora)),
            pl.BlockSpec((tm, lora), lambda i: (i, ckva // lora)),
            pl.BlockSpec((tm, LANES), lambda i: (i, ckpe // LANES)),
            pl.BlockSpec((tm, LANES), lambda i: (jnp.where(i < n_x_tiles, i % tab_tiles, tab_tiles), 0)),
            pl.BlockSpec((1, lora), lambda i: (0, 0)),
            pl.BlockSpec((1, lora), lambda i: (0, 0)),
            pl.BlockSpec((lora, hs), lambda i: (0, 0)),
            pl.BlockSpec((lora, 2 * hv), lambda i: (0, 0)),
            pl.BlockSpec((1, MLA_SLAB), lambda i: (0, 0)),
            pl.BlockSpec((1, MLA_SLAB), lambda i: (0, 0)),
        ],
        out_specs=(pl.BlockSpec((tm, hs), lambda i: (i, 0)), pl.BlockSpec((tm, hs), lambda i: (i, 0)),
                   pl.BlockSpec((tm, hs), lambda i: (i, 0))),
        compiler_params=_cparams(("parallel",)),
        name="mla_qkv",
    )(z, z, z, tab, qag, kvag, wq, wkv, gq, gk)


def _attn_kernel(n_kv, tq, q_ref, *refs):
    k_refs = refs[:n_kv]
    v_refs = refs[n_kv:2 * n_kv]
    o_ref, s0_ref, s1_ref, m0_ref, m1_ref = refs[2 * n_kv:]
    slots = ((s0_ref, m0_ref), (s1_ref, m1_ref))
    n_tiles = q_ref.shape[0] // tq
    chunks = []
    col = 0
    for kv, k_ref in enumerate(k_refs):
        n_keys = k_ref.shape[0]
        step = min(ATTN_KEY_CHUNK, n_keys)
        for off in range(0, n_keys, step):
            chunks.append((kv, off, col, step))
            col += step

    def scores(t, slot):
        s_ref, m_ref = slots[slot]
        rows = pl.ds(pl.multiple_of(t * tq, tq), tq)
        q = q_ref[rows, :]
        run = None
        for kv, off, c0, size in chunks:
            s = lax.dot_general(q, k_refs[kv][off:off + size, :], (((1,), (1,)), ((), ())),
                                preferred_element_type=F32)
            s_ref[:, c0:c0 + size] = s
            for lb in range(size // LANES):
                blk = s[:, lb * LANES:(lb + 1) * LANES]
                run = blk if run is None else jnp.maximum(run, blk)
        m_ref[...] = run

    def finish(t, slot):
        s_ref, m_ref = slots[slot]
        rows = pl.ds(pl.multiple_of(t * tq, tq), tq)
        m = jnp.max(m_ref[...], axis=-1, keepdims=True)
        acc = None
        for kv, off, c0, size in chunks:
            p = jnp.exp2(s_ref[:, c0:c0 + size] - m).astype(BF16)
            pv = jnp.dot(p, v_refs[kv][off:off + size, :], preferred_element_type=F32)
            acc = pv if acc is None else acc + pv
        o_ref[rows, :] = (acc[:, :LANES] / acc[:, LANES:LANES + 1]).astype(BF16)

    scores(0, 0)

    def body(k, carry):
        scores(2 * k + 1, 1)
        finish(2 * k, 0)
        scores(jnp.minimum(2 * k + 2, n_tiles - 1), 0)
        finish(2 * k + 1, 1)
        return carry

    lax.fori_loop(0, n_tiles // 2, body, 0)
    if n_tiles % 2:
        finish(n_tiles - 1, 0)


def _attention(qo, ko, vo, b, s, tc, tq, latent):
    n_x = b * s
    ctx0 = n_x // tc
    cspec = pl.BlockSpec((tc, MLA_SLAB), lambda i, h: (ctx0 + i, h))
    xspec = pl.BlockSpec((s, MLA_SLAB), lambda i, h: (i, h))
    if latent:
        n_q, n_keys = s, s + tc
        in_specs = [xspec, cspec, xspec, cspec, xspec]
        args = (qo, ko, ko, vo, vo)
    else:
        n_q, n_keys = tc, tc
        in_specs = [cspec, cspec, cspec]
        args = (qo, ko, vo)
    return pl.pallas_call(
        functools.partial(_attn_kernel, (len(args) - 1) // 2, tq),
        out_shape=jax.ShapeDtypeStruct((b * n_q, MLA_HEADS * LANES), BF16),
        grid=(b, MLA_HEADS),
        in_specs=in_specs,
        out_specs=pl.BlockSpec((n_q, LANES), lambda i, h: (i, h)),
        scratch_shapes=[pltpu.VMEM((tq, n_keys), F32), pltpu.VMEM((tq, n_keys), F32),
                        pltpu.VMEM((tq, LANES), F32), pltpu.VMEM((tq, LANES), F32)],
        compiler_params=_cparams(("parallel", "parallel")),
        name="attn_latent" if latent else "attn_ctx",
    )(*args)


S5_HALF = S5_BLOCK_GROUPS * S5_STATE


def _s5_kernel(n_batch, rows_x, rows_c, dot_rows, u_ref, r_ref, m_ref, o_ref, tab_ref, y_ref, u2_ref, v_ref):
    d = pl.program_id(1)
    n_dot = u2_ref.shape[0] // dot_rows

    def dot_rows_of(i):
        return pl.ds(pl.multiple_of(i * dot_rows, dot_rows), dot_rows)

    def token_rows_of(i, s):
        return pl.ds(i * (dot_rows * S5_SUB) + s, dot_rows, stride=S5_SUB)

    @pl.when(d == 0)
    def _():
        def stage(i, carry):
            rows = pl.ds(pl.multiple_of(i * dot_rows, dot_rows), dot_rows)
            y_ref[rows, :] = u_ref[rows, :].astype(F32)
            return carry

        lax.fori_loop(0, n_dot * S5_SUB, stage, 0)

        def regroup(i, carry):
            for s in range(S5_SUB):
                u2_ref[dot_rows_of(i), s * LANES:(s + 1) * LANES] = y_ref[token_rows_of(i, s), :].astype(BF16)
            return carry

        lax.fori_loop(0, n_dot, regroup, 0)

    def increments(i, carry):
        rows = dot_rows_of(i)
        v_ref[rows, :] = jnp.dot(u2_ref[rows, :], r_ref[0, 0], preferred_element_type=F32)
        return carry

    lax.fori_loop(0, n_dot, increments, 0)

    def cmul_add(ar, ai, cr, ci, xr, xi):
        return ar + cr * xr - ci * xi, ai + cr * xi + ci * xr

    def run(reverse):
        tab = tab_ref.at[0, 0]
        last = 0 if reverse else SUBLANES - 1
        first_row = lax.broadcasted_iota(jnp.int32, (SUBLANES, S5_HALF), 0) == (SUBLANES - 1 - last)

        def segment(bases, n_groups, carry):
            def body(i, carry):
                gi = (n_groups - 1 - i) if reverse else i
                out = []
                for base, (cre, cim) in zip(bases, carry):
                    rows = pl.ds(pl.multiple_of(base + gi * SUBLANES, SUBLANES), SUBLANES)
                    re = v_ref[rows, :S5_HALF]
                    im = v_ref[rows, S5_HALF:]
                    for lvl, shift in enumerate((1, 2, 4)):
                        sh = (SUBLANES - shift) if reverse else shift
                        re, im = cmul_add(re, im, tab[2 * lvl], tab[2 * lvl + 1],
                                          pltpu.roll(re, sh, axis=0), pltpu.roll(im, sh, axis=0))
                    re, im = cmul_add(re, im, tab[6], tab[7], cre, cim)
                    sh1 = (SUBLANES - 1) if reverse else 1
                    v_ref[rows, :S5_HALF] = jnp.where(first_row, cre, pltpu.roll(re, sh1, axis=0))
                    v_ref[rows, S5_HALF:] = jnp.where(first_row, cim, pltpu.roll(im, sh1, axis=0))
                    out.append((jnp.broadcast_to(re[last:last + 1, :], re.shape),
                                jnp.broadcast_to(im[last:last + 1, :], im.shape)))
                return tuple(out)
            return lax.fori_loop(0, n_groups, body, carry)

        zero = jnp.zeros((SUBLANES, S5_HALF), F32)
        carry = tuple((zero, zero) for _ in range(n_batch))
        carry = segment([n_batch * rows_x + bi * rows_c for bi in range(n_batch)], rows_c // SUBLANES, carry)
        segment([bi * rows_x for bi in range(n_batch)], rows_x // SUBLANES, carry)

        def outputs(i, carry):
            rows = dot_rows_of(i)
            y = jnp.dot(u2_ref[rows, :], m_ref[0, 0], preferred_element_type=F32)
            y = y + jnp.dot(v_ref[rows, :].astype(BF16), o_ref[0, 0], preferred_element_type=F32)
            for s in range(S5_SUB):
                part = y[:, s * LANES:(s + 1) * LANES]
                if reverse:
                    y_ref[token_rows_of(i, s), :] += part
                else:
                    y_ref[token_rows_of(i, s), :] = part
            return carry

        lax.fori_loop(0, n_dot, outputs, 0)

    @pl.when(d == 0)
    def _():
        run(False)

    @pl.when(d == 1)
    def _():
        run(True)


def _s5_scan(z, rmat, mmat, omat, tabs, b, s, tc, col_u):
    r = z.shape[0]
    rc = r // S5_SUB
    n_blocks = rmat.shape[1]
    width = S5_SUB * LANES
    dot_rows = max(n for n in range(16, 641, 16) if rc % n == 0)
    mat_spec = pl.BlockSpec((1, 1, width, width), lambda cb, d: (d, cb, 0, 0), pipeline_mode=pl.Buffered(1))
    return pl.pallas_call(
        functools.partial(_s5_kernel, b, s // S5_SUB, tc // S5_SUB, dot_rows),
        out_shape=jax.ShapeDtypeStruct((r, n_blocks * LANES), F32),
        grid=(n_blocks, 2),
        in_specs=[
            pl.BlockSpec((r, LANES), lambda cb, d: (0, col_u // LANES + cb), pipeline_mode=pl.Buffered(1)),
            mat_spec, mat_spec, mat_spec,
            pl.BlockSpec((1, 1, 8, SUBLANES, S5_HALF), lambda cb, d: (d, cb, 0, 0, 0)),
        ],
        out_specs=pl.BlockSpec((r, LANES), lambda cb, d: (0, cb)),
        scratch_shapes=[pltpu.VMEM((rc, width), BF16), pltpu.VMEM((rc, width), F32)],
        compiler_params=_cparams(("parallel", "arbitrary")),
        name="s5_scan",
    )(z, rmat, mmat, omat, tabs)


def _glu_kernel(y_ref, u_ref, d_ref, w_ref, b_ref, o_ref, g_ref):
    tm = y_ref.shape[0]

    def body(r, carry):
        rows = pl.ds(pl.multiple_of(r * ROW_CHUNK, ROW_CHUNK), ROW_CHUNK)
        y = y_ref[rows, :] + d_ref[...] * u_ref[rows, :].astype(F32)
        g_ref[rows, :] = jax.nn.gelu(y).astype(BF16)
        return carry

    lax.fori_loop(0, tm // ROW_CHUNK, body, 0)
    g = g_ref[...]
    gate = jax.nn.sigmoid(jnp.dot(g, w_ref[...], preferred_element_type=F32) + b_ref[...])
    o_ref[...] = (g.astype(F32) * gate).astype(BF16)


def _s5_glu(y, z, d_skip, w_glu, b_glu, tm, col_u, n_row_tiles):
    width = y.shape[1]
    return pl.pallas_call(
        _glu_kernel,
        out_shape=jax.ShapeDtypeStruct((n_row_tiles * tm, width), BF16),
        grid=(n_row_tiles,),
        in_specs=[
            pl.BlockSpec((tm, width), lambda i: (i, 0)),
            pl.BlockSpec((tm, width), lambda i: (i, col_u // width)),
            pl.BlockSpec((1, width), lambda i: (0, 0)),
            pl.BlockSpec((width, width), lambda i: (0, 0)),
            pl.BlockSpec((1, width), lambda i: (0, 0)),
        ],
        out_specs=pl.BlockSpec((tm, width), lambda i: (i, 0)),
        scratch_shapes=[pltpu.VMEM((tm, width), BF16)],
        compiler_params=_cparams(("parallel",)),
        name="s5_glu",
    )(y, z, d_skip, w_glu, b_glu)


def _merge_kernel(n_x_tiles, ax_ref, ac_ref, bx_ref, bc_ref, c_ref, ga_ref, gb_ref, gc_ref, w_ref, o_ref):
    def combine(a_ref, b_ref):
        acc = None
        for r, (br_ref, gate_ref) in enumerate(((a_ref, ga_ref), (b_ref, gb_ref), (c_ref, gc_ref))):
            proj = jnp.dot(br_ref[...], w_ref[r], preferred_element_type=F32)
            term = jax.nn.sigmoid(gate_ref[...].astype(F32)) * proj
            acc = term if acc is None else acc + term
        o_ref[...] = acc.astype(BF16)

    @pl.when(pl.program_id(0) < n_x_tiles)
    def _():
        combine(ax_ref, bx_ref)

    @pl.when(pl.program_id(0) >= n_x_tiles)
    def _():
        combine(ac_ref, bc_ref)


def _merge(a_x, a_c, b_x, b_c, cc, z, w_branch, tm, tn, col_g, n_row_tiles, n_x_tiles):
    width = a_x.shape[1]
    d = w_branch.shape[2]

    def gate_spec(r):
        return pl.BlockSpec((tm, tn), lambda i, j: (i, (col_g + r * d) // tn + j))

    ax_spec, ac_spec = _stream_specs((tm, width), n_x_tiles, 0, lambda j: 0)
    return pl.pallas_call(
        functools.partial(_merge_kernel, n_x_tiles),
        out_shape=jax.ShapeDtypeStruct((n_row_tiles * tm, d), BF16),
        grid=(n_row_tiles, d // tn),
        in_specs=[
            ax_spec, ac_spec, ax_spec, ac_spec,
            pl.BlockSpec((tm, width), lambda i, j: (i, 0)),
            gate_spec(0), gate_spec(1), gate_spec(2),
            pl.BlockSpec((N_BRANCH, width, tn), lambda i, j: (0, 0, j)),
        ],
        out_specs=pl.BlockSpec((tm, tn), lambda i, j: (i, j)),
        compiler_params=_cparams(("parallel", "arbitrary")),
        name="merge",
    )(a_x, a_c, b_x, b_c, cc, z, z, z, w_branch)


def _resid_kernel(m_ref, w_ref, x_ref, al_ref, o_ref):
    o_ref[...] = x_ref[...] + al_ref[0] * jnp.dot(m_ref[...], w_ref[...], preferred_element_type=F32)


def _out_proj_residual(m, w, xs, mod3, tm, tn, n_row_tiles, n_x_tiles, tiles_per_batch, ctx_row, k_alpha):
    kdim, d = w.shape
    nt = d // tn

    def alpha_map(i, j):
        r = jnp.where(i < n_x_tiles, i // tiles_per_batch, ctx_row)
        return (r * N_MOD + k_alpha, 0, j)

    return pl.pallas_call(
        _resid_kernel,
        out_shape=jax.ShapeDtypeStruct((n_row_tiles * tm, d), F32),
        grid=(n_row_tiles, nt),
        in_specs=[
            pl.BlockSpec((tm, kdim), lambda i, j: (i, 0)),
            pl.BlockSpec((kdim, tn), lambda i, j: (0, j)),
            pl.BlockSpec((tm, tn), lambda i, j: (i, j)),
            pl.BlockSpec((1, 1, tn), alpha_map),
        ],
        out_specs=pl.BlockSpec((tm, tn), lambda i, j: (i, j)),
        compiler_params=_cparams(("parallel", "arbitrary")),
        name="out_proj",
    )(m, w, xs, mod3)


def _ff1_kernel(x_ref, g_ref, sh_ref, sc_ref, w_ref, h_ref, xn_ref):
    @pl.when(pl.program_id(1) == 0)
    def _():
        _norm_mod_rows(x_ref, g_ref, sh_ref, sc_ref, xn_ref)

    a = jnp.maximum(jnp.dot(xn_ref[...], w_ref[...], preferred_element_type=F32), 0.0)
    h_ref[...] = (a * a).astype(BF16)


def _ff1(xs, g, mod3, w, tm, tn, n_row_tiles, n_x_tiles, tiles_per_batch, ctx_row):
    d, dff = w.shape
    return pl.pallas_call(
        _ff1_kernel,
        out_shape=jax.ShapeDtypeStruct((n_row_tiles * tm, dff), BF16),
        grid=(n_row_tiles, dff // tn),
        in_specs=[
            pl.BlockSpec((tm, d), lambda i, j: (i, 0)),
            pl.BlockSpec((1, d), lambda i, j: (0, 0)),
            pl.BlockSpec((1, 1, d), _mod_row_map(n_x_tiles, tiles_per_batch, ctx_row, 3)),
            pl.BlockSpec((1, 1, d), _mod_row_map(n_x_tiles, tiles_per_batch, ctx_row, 4)),
            pl.BlockSpec((d, tn), lambda i, j: (0, j)),
        ],
        out_specs=pl.BlockSpec((tm, tn), lambda i, j: (i, j)),
        scratch_shapes=[pltpu.VMEM((tm, d), BF16)],
        compiler_params=_cparams(("parallel", "arbitrary")),
        name="ff1",
    )(xs, g, mod3, mod3, w)


def _ff2_kernel(h_ref, w_ref, x_ref, al_ref, o_ref):
    k = pl.program_id(2)
    part = jnp.dot(h_ref[...], w_ref[...], preferred_element_type=F32)

    @pl.when(k == 0)
    def _():
        o_ref[...] = part

    @pl.when(k > 0)
    def _():
        o_ref[...] += part

    @pl.when(k == pl.num_programs(2) - 1)
    def _():
        o_ref[...] = x_ref[...] + al_ref[0] * o_ref[...]


def _ff2(h, w, xs, mod3, tm, tn, tk, n_row_tiles, n_x_tiles, tiles_per_batch, ctx_row):
    dff, d = w.shape

    def alpha_map(i, j, k):
        r = jnp.where(i < n_x_tiles, i // tiles_per_batch, ctx_row)
        return (r * N_MOD + 5, 0, j)

    return pl.pallas_call(
        _ff2_kernel,
        out_shape=jax.ShapeDtypeStruct((n_row_tiles * tm, d), F32),
        grid=(n_row_tiles, d // tn, dff // tk),
        in_specs=[
            pl.BlockSpec((tm, tk), lambda i, j, k: (i, k)),
            pl.BlockSpec((tk, tn), lambda i, j, k: (k, j)),
            pl.BlockSpec((tm, tn), lambda i, j, k: (i, j)),
            pl.BlockSpec((1, 1, tn), alpha_map),
        ],
        out_specs=pl.BlockSpec((tm, tn), lambda i, j, k: (i, j)),
        compiler_params=_cparams(("parallel", "parallel", "arbitrary")),
        name="ff2",
    )(h, w, xs, mod3)


def _rope_partner():
    j = np.arange(MLA_ROPE)
    quarter = MLA_ROPE // 4
    return np.where((j // quarter) % 2 == 0, j + quarter, j - quarter)


def _rope_table(s, tm):
    pos = jnp.arange(s)
    row = (pos // GRID_W).astype(F32)
    col = (pos % GRID_W).astype(F32)
    n_freq = MLA_ROPE // 4
    inv_freq = ROPE_THETA ** (-jnp.arange(n_freq, dtype=F32) / n_freq)
    ang_r = row[:, None] * inv_freq
    ang_c = col[:, None] * inv_freq
    cos = jnp.concatenate([jnp.cos(ang_r)] * 2 + [jnp.cos(ang_c)] * 2, axis=-1)
    sin = jnp.concatenate([-jnp.sin(ang_r), jnp.sin(ang_r), -jnp.sin(ang_c), jnp.sin(ang_c)], axis=-1)
    ident = jnp.concatenate([jnp.ones((tm, MLA_ROPE), F32), jnp.zeros((tm, MLA_ROPE), F32)], axis=-1)
    return jnp.concatenate([jnp.concatenate([cos, sin], axis=-1), ident], axis=0)


def _pack_w_in(w_in, b_in, gate_b, sizes, n_pad):
    bounds = np.cumsum((0,) + sizes)
    seg = [slice(int(bounds[i]), int(bounds[i + 1])) for i in range(len(sizes))]
    partner = _rope_partner()
    order = (0, 1, 2, 3, 5, 6, 8, 9)
    w_parts = [w_in[:, seg[i]] for i in order]
    b_parts = [b_in[seg[i]] for i in order]
    kpe_w = w_in[:, seg[7]]
    kpe_b = b_in[seg[7]]
    w_parts += [kpe_w, kpe_w[:, partner]]
    b_parts += [kpe_b, kpe_b[partner]]
    offs = {}
    pos = 0
    for name, part in zip(("q", "k", "v", "o", "qa", "kva", "u", "gates", "kpe", "kpe_sw"), w_parts):
        offs[name] = pos
        pos += part.shape[1]
    pad = n_pad - pos
    w_parts.append(jnp.zeros((w_in.shape[0], pad), w_in.dtype))
    b_parts.append(jnp.zeros((pad,), b_in.dtype))
    w = jnp.concatenate(w_parts, axis=1).astype(BF16)
    b = jnp.concatenate(b_parts)[None, :]
    n_g = sizes[4]
    wg = jnp.concatenate([w_in[:, seg[4]], jnp.zeros((w_in.shape[0], LANES - n_g), w_in.dtype)], axis=1).astype(BF16)
    bg = jnp.concatenate([b_in[seg[4]] + gate_b.reshape(-1), jnp.zeros((LANES - n_g,), F32)])[None, :]
    return w, b, wg, bg, offs


def _pack_mla(w_uq, w_ukv, qn_g, kn_g):
    partner = _rope_partner()
    lora = w_uq.shape[0]
    wq = w_uq.reshape(lora, MLA_HEADS, MLA_DQK)
    wq = jnp.concatenate([wq, wq[:, :, MLA_NOPE + partner]], axis=-1).reshape(lora, MLA_HEADS * MLA_SLAB)
    wkv = w_ukv.reshape(w_ukv.shape[0], MLA_HEADS, -1)
    wkv = jnp.concatenate([wkv[:, :, :MLA_NOPE].reshape(lora, -1), wkv[:, :, MLA_NOPE:].reshape(lora, -1)], axis=-1)

    def gains(g):
        return jnp.concatenate([g, g[MLA_NOPE + partner]])[None, :]

    return wq.astype(BF16), wkv.astype(BF16), gains(qn_g), gains(kn_g)


def _pack_s5(a_re, a_im, log_dt, b_re, b_im, c_re, c_im):
    n_dir, n_groups, n_state = a_re.shape
    gc = b_re.shape[-1]
    nb = n_groups // S5_BLOCK_GROUPS
    lam_re = jnp.minimum(a_re.astype(F32), -1e-4)
    lam_im = a_im.astype(F32)
    dt = jnp.exp(log_dt.astype(F32))[..., None]

    def pole_power(k):
        mag = jnp.exp(k * lam_re * dt)
        return mag * jnp.cos(k * lam_im * dt), mag * jnp.sin(k * lam_im * dt)

    bar_re, bar_im = pole_power(1.0)
    den = lam_re * lam_re + lam_im * lam_im
    f_re = ((bar_re - 1.0) * lam_re + bar_im * lam_im) / den
    f_im = (bar_im * lam_re - (bar_re - 1.0) * lam_im) / den
    bb_re = f_re[..., None] * b_re.astype(F32) - f_im[..., None] * b_im.astype(F32)
    bb_im = f_re[..., None] * b_im.astype(F32) + f_im[..., None] * b_re.astype(F32)
    eye = jnp.eye(S5_BLOCK_GROUPS, dtype=F32)
    sub = S5_SUB

    def per_block(a):
        return a.reshape(a.shape[:-2] + (nb, S5_BLOCK_GROUPS * n_state))

    def block_b(part):
        p = part.reshape(n_dir, nb, S5_BLOCK_GROUPS, n_state, gc)
        m = jnp.einsum('dbgnc,gh->dbgchn', p, eye, precision=HIGHEST)
        return m.reshape(n_dir, nb, LANES, S5_HALF)

    def block_c(part):
        p = part.astype(F32).reshape(n_dir, nb, S5_BLOCK_GROUPS, gc, n_state)
        m = jnp.einsum('dbgcn,gh->dbgnhc', p, eye, precision=HIGHEST)
        return m.reshape(n_dir, nb, S5_HALF, LANES)

    bm_re, bm_im = block_b(bb_re), block_b(bb_im)
    cm_re, cm_im = block_c(c_re), block_c(c_im)

    tau = jnp.arange(sub + 1, dtype=F32)[:, None, None, None]
    p_re, p_im = (per_block(p) for p in pole_power(tau))
    pr, pi = p_re[:sub, :, :, None, :], p_im[:sub, :, :, None, :]
    taps = (jnp.einsum('zdbkn,dbnc->zdbkc', bm_re * pr - bm_im * pi, cm_re, precision=HIGHEST)
            - jnp.einsum('zdbkn,dbnc->zdbkc', bm_re * pi + bm_im * pr, cm_im, precision=HIGHEST))

    t = np.arange(sub)
    width = sub * LANES
    rmats, mmats, omats = [], [], []
    for d in range(n_dir):
        if d == 0:
            lag = t[None, :] - t[:, None]
            to_exit = sub - 1 - t
            age = t + 1
        else:
            lag = t[:, None] - t[None, :]
            to_exit = t
            age = sub - t
        lag_sel = jnp.asarray(lag[None] == t[:, None, None], F32)
        mm = jnp.einsum('zst,zbkc->bsktc', lag_sel, taps[:, d])
        mmats.append(mm.reshape(nb, width, width).astype(BF16))
        er = p_re[to_exit, d].transpose(1, 0, 2)[:, :, None, :]
        ei = p_im[to_exit, d].transpose(1, 0, 2)[:, :, None, :]
        b_r, b_i = bm_re[d][:, None], bm_im[d][:, None]
        rr = jnp.concatenate([b_r * er - b_i * ei, b_r * ei + b_i * er], axis=-1)
        rmats.append(rr.astype(BF16).reshape(nb, width, 2 * S5_HALF))
        ar = p_re[age, d].transpose(1, 2, 0)[:, :, :, None]
        ai = p_im[age, d].transpose(1, 2, 0)[:, :, :, None]
        c_r, c_i = cm_re[d][:, :, None, :], cm_im[d][:, :, None, :]
        oo = jnp.concatenate([(c_r * ar - c_i * ai).astype(BF16).reshape(nb, S5_HALF, width),
                              (-(c_r * ai + c_i * ar)).astype(BF16).reshape(nb, S5_HALF, width)], axis=1)
        omats.append(oo)

    rows = jnp.arange(SUBLANES)
    tabs = []
    for d in range(n_dir):
        per_dir = []
        for shift in (1, 2, 4):
            keep = ((rows <= SUBLANES - 1 - shift) if d == 1 else (rows >= shift))[None, :, None]
            s_re, s_im = pole_power(float(shift * sub))
            per_dir += [jnp.where(keep, per_block(s_re[d])[:, None, :], 0.0),
                        jnp.where(keep, per_block(s_im[d])[:, None, :], 0.0)]
        expo = (((SUBLANES - rows) if d == 1 else (rows + 1)) * sub).astype(F32)
        s_re, s_im = pole_power(expo[:, None, None, None])
        per_dir += [jnp.moveaxis(per_block(s_re[:, d]), 0, 1), jnp.moveaxis(per_block(s_im[:, d]), 0, 1)]
        tabs.append(jnp.stack(per_dir, axis=1))
    return jnp.stack(rmats), jnp.stack(mmats), jnp.stack(omats), jnp.stack(tabs).astype(F32)


def _tri_matrices():
    t = np.arange(ML_CHUNK)
    return jnp.asarray((t[None, :] <= t[:, None]).astype(np.float32), dtype=BF16)


def kernel(x, c, ctx, c_ctx, w_mod, b_mod, norm_g, w_in, b_in, ml_gate_b, ml_norm_g, mla_qa_g, mla_kva_g, mla_w_uq, mla_w_ukv, mla_qn_g, mla_kn_g, s5_a_re, s5_a_im, s5_log_dt, s5_b_re, s5_b_im, s5_c_re, s5_c_im, s5_d, s5_w_glu, s5_b_glu, w_branch, w_out, w_ff1, w_ff2):
    b, s, d = x.shape
    tc = ctx.shape[1]
    depth = w_mod.shape[0]
    dv = ml_norm_g.shape[2]
    dk = dv // 2
    lora = mla_qa_g.shape[1]
    s5_width = s5_d.shape[1]
    branch_w = w_branch.shape[2]
    sizes = (ML_HEADS * dk, ML_HEADS * dk, ML_HEADS * dv, ML_HEADS * dv, 4 * ML_HEADS, lora, lora, MLA_ROPE,
             s5_width, N_BRANCH * d)
    assert sum(sizes) == w_in.shape[2] and b + 1 <= SUBLANES
    assert s % ML_CHUNK == 0 and tc % ML_CHUNK == 0 and branch_w == ML_HEADS * dv == MLA_HEADS * LANES == s5_width

    n_x = b * s
    n_c = b * tc
    tm = _row_tile(s, n_c)
    n_x_tiles = n_x // tm
    n_tiles = n_x_tiles + n_c // tm
    tiles_per_batch = s // tm
    tile_args = (n_x_tiles, tiles_per_batch, b)

    xs = jnp.concatenate([x.reshape(n_x, d), ctx.reshape(n_c, d)], axis=0)
    cc = jnp.concatenate([c, c_ctx[None, :], jnp.zeros((SUBLANES - b - 1, d), F32)], axis=0)
    mod = _modulation(cc, w_mod, b_mod)
    tm_q = min(tm, 512)
    tab = _rope_table(s, tm_q)
    tri = _tri_matrices()
    n_used = sum(sizes) - sizes[4] + MLA_ROPE
    tn_in = 1280
    n_pad = -(-n_used // tn_in) * tn_in

    for l in range(depth):
        with_ctx_out = l < depth - 1
        mod3 = mod[l].reshape(SUBLANES * N_MOD, 1, d)
        w_p, b_p, wg, bg, offs = _pack_w_in(w_in[l], b_in[l], ml_gate_b[l], sizes, n_pad)
        z, gz = _in_proj(xs, norm_g[l, 0][None, :], mod3, w_p, b_p, wg, bg, tm, tn_in, *tile_args)

        a_x, a_c = _mlstm(z, gz, tri, ml_norm_g[l], b, s, tc, dk, dv,
                          (offs["q"], offs["k"], offs["v"], offs["o"]), with_ctx_out)

        wq, wkv, gq, gk = _pack_mla(mla_w_uq[l], mla_w_ukv[l], mla_qn_g[l], mla_kn_g[l])
        qo, ko, vo = _mla_proj(z, tab, mla_qa_g[l][None, :], mla_kva_g[l][None, :], wq, wkv, gq, gk, tm_q,
                               (offs["qa"], offs["kva"], offs["kpe"]), n_x // tm_q, s // tm_q)
        tq = min(256, s)
        b_x = _attention(qo, ko, vo, b, s, tc, tq, True)

        rmat, mmat, omat, tabs = _pack_s5(s5_a_re[l], s5_a_im[l], s5_log_dt[l], s5_b_re[l], s5_b_im[l],
                                          s5_c_re[l], s5_c_im[l])
        y = _s5_scan(z, rmat, mmat, omat, tabs, b, s, tc, offs["u"])
        n_out_tiles = n_tiles if with_ctx_out else n_x_tiles
        c_all = _s5_glu(y, z, s5_d[l][None, :], s5_w_glu[l].astype(BF16), s5_b_glu[l][None, :], tm, offs["u"],
                        n_out_tiles)

        if with_ctx_out:
            b_c = _attention(qo, ko, vo, b, s, tc, min(tq, tc), False)
        else:
            a_c, b_c = a_x, b_x
        merged = _merge(a_x, a_c, b_x, b_c, c_all, z, w_branch[l].astype(BF16), tm, 512, offs["gates"],
                        n_out_tiles, n_x_tiles)
        xs1 = _out_proj_residual(merged, w_out[l].astype(BF16), xs, mod3, tm, 1024, n_out_tiles, *tile_args, 2)
        hid = _ff1(xs1, norm_g[l, 1][None, :], mod3, w_ff1[l].astype(BF16), tm, 1024, n_out_tiles, *tile_args)
        xs = _ff2(hid, w_ff2[l].astype(BF16), xs1, mod3, tm, 1024, 2048, n_out_tiles, *tile_args)

    return xs.reshape(b, s, d)
```

```python
import functools
import math

import jax
import jax.numpy as jnp
import numpy as np
from jax import lax
from jax.experimental import pallas as pl
from jax.experimental.pallas import tpu as pltpu

F32 = jnp.float32
BF16 = jnp.bfloat16
HIGHEST = lax.Precision.HIGHEST

N_MOD = 6
N_BRANCH = 3
ML_HEADS = 4
MLA_HEADS = 8
MLA_NOPE = 128
MLA_ROPE = 64
MLA_DQK = MLA_NOPE + MLA_ROPE
MLA_SLAB = 256
GRID_W = 64
ROPE_THETA = 10000.0
S5_GROUP = 16
S5_STATE = 64
S5_BLOCK_GROUPS = 8
EPS = 1e-6
NEG_BIG = -1e30

LANES = 128
SUBLANES = 8
VMEM_LIMIT = 56 * 1024 * 1024

ML_CHUNK = 256
S5_SUB = 8
ATTN_KEY_CHUNK = 512
Q_PRESCALE = MLA_DQK ** -0.5 * math.log2(math.e)
ROW_CHUNK = 64


def _cparams(sem):
    return pltpu.CompilerParams(dimension_semantics=sem, vmem_limit_bytes=VMEM_LIMIT)


def _row_tile(n_x_rows_per_batch, n_ctx_rows):
    tm = 1024
    while n_x_rows_per_batch % tm or n_ctx_rows % tm:
        tm //= 2
    return tm


def _mod_kernel(c_ref, w_ref, b_ref, o_ref):
    s = c_ref[...]
    s = s * jax.nn.sigmoid(s)
    o_ref[0] = jnp.dot(s.astype(BF16), w_ref[0].astype(BF16), preferred_element_type=F32) + b_ref[0]


def _modulation(cc, w_mod, b_mod):
    n_layers, d, n = w_mod.shape
    tn = 1024
    return pl.pallas_call(
        _mod_kernel,
        out_shape=jax.ShapeDtypeStruct((n_layers, SUBLANES, n), F32),
        grid=(n_layers, n // tn),
        in_specs=[
            pl.BlockSpec((SUBLANES, d), lambda l, j: (0, 0)),
            pl.BlockSpec((1, d, tn), lambda l, j: (l, 0, j)),
            pl.BlockSpec((1, 1, tn), lambda l, j: (l, 0, j)),
        ],
        out_specs=pl.BlockSpec((1, SUBLANES, tn), lambda l, j: (l, 0, j)),
        compiler_params=_cparams(("parallel", "parallel")),
        name="adaln_mod",
    )(cc, w_mod, b_mod.reshape(n_layers, 1, n))


def _norm_mod_rows(x_ref, g_ref, sh_ref, sc_ref, xn_ref):
    tm = x_ref.shape[0]
    g = g_ref[...]
    sc = 1.0 + sc_ref[0]
    sh = sh_ref[0]

    def body(r, carry):
        rows = pl.ds(pl.multiple_of(r * ROW_CHUNK, ROW_CHUNK), ROW_CHUNK)
        x = x_ref[rows, :]
        ms = jnp.mean(x * x, axis=-1, keepdims=True)
        y = x * lax.rsqrt(ms + EPS) * g
        xn_ref[rows, :] = (y * sc + sh).astype(BF16)
        return carry

    lax.fori_loop(0, tm // ROW_CHUNK, body, 0)


def _mod_row_map(n_x_tiles, tiles_per_batch, ctx_row, k):
    def index_map(i, j):
        r = jnp.where(i < n_x_tiles, i // tiles_per_batch, ctx_row)
        return (r * N_MOD + k, 0, 0)
    return index_map


def _stream_specs(block, n_x_tiles, ctx_tile0, col_map):
    x_spec = pl.BlockSpec(block, lambda i, *r: (jnp.minimum(i, n_x_tiles - 1), col_map(*r)))
    c_spec = pl.BlockSpec(block, lambda i, *r: (ctx_tile0 + jnp.maximum(i - n_x_tiles, 0), col_map(*r)),
                          pipeline_mode=pl.Buffered(1))
    return x_spec, c_spec


def _in_kernel(x_ref, g_ref, sh_ref, sc_ref, w_ref, b_ref, wg_ref, bg_ref, z_ref, gz_ref, xn_ref):
    @pl.when(pl.program_id(1) == 0)
    def _():
        _norm_mod_rows(x_ref, g_ref, sh_ref, sc_ref, xn_ref)
        gz_ref[...] = jnp.dot(xn_ref[...], wg_ref[...], preferred_element_type=F32) + bg_ref[...]

    z_ref[...] = (jnp.dot(xn_ref[...], w_ref[...], preferred_element_type=F32) + b_ref[...]).astype(BF16)


def _in_proj(xs, g, mod3, w, b, wg, bg, tm, tn, n_x_tiles, tiles_per_batch, ctx_row):
    r, d = xs.shape
    nz = w.shape[1]
    return pl.pallas_call(
        _in_kernel,
        out_shape=(jax.ShapeDtypeStruct((r, nz), BF16), jax.ShapeDtypeStruct((r, LANES), F32)),
        grid=(r // tm, nz // tn),
        in_specs=[
            pl.BlockSpec((tm, d), lambda i, j: (i, 0)),
            pl.BlockSpec((1, d), lambda i, j: (0, 0)),
            pl.BlockSpec((1, 1, d), _mod_row_map(n_x_tiles, tiles_per_batch, ctx_row, 0)),
            pl.BlockSpec((1, 1, d), _mod_row_map(n_x_tiles, tiles_per_batch, ctx_row, 1)),
            pl.BlockSpec((d, tn), lambda i, j: (0, j)),
            pl.BlockSpec((1, tn), lambda i, j: (0, j)),
            pl.BlockSpec((d, LANES), lambda i, j: (0, 0)),
            pl.BlockSpec((1, LANES), lambda i, j: (0, 0)),
        ],
        out_specs=(
            pl.BlockSpec((tm, tn), lambda i, j: (i, j)),
            pl.BlockSpec((tm, LANES), lambda i, j: (i, 0)),
        ),
        scratch_shapes=[pltpu.VMEM((tm, d), BF16)],
        compiler_params=_cparams(("parallel", "arbitrary")),
        name="in_proj",
    )(xs, g, mod3, mod3, w, b, wg, bg)


def _log_sigmoid(x):
    return jnp.minimum(x, 0.0) - jnp.log1p(jnp.exp(-jnp.abs(x)))


def _split3(a):
    hi = a.astype(BF16)
    r1 = a - hi.astype(F32)
    mid = r1.astype(BF16)
    lo = (r1 - mid.astype(F32)).astype(BF16)
    return hi, mid, lo


def _ml_chunk(q, k, v, lg_c, lg_r, carry, reverse, scale):
    c_mat, n_vec, m = carry
    length = q.shape[0]
    gi = 2 if reverse else 0
    ci = 5 if reverse else 4
    li_c = lg_c[:, gi:gi + 1]
    cum_c = lg_c[:, ci:ci + 1]
    li_r = lg_r[gi:gi + 1, :]
    cum_r = lg_r[ci:ci + 1, :]
    total = jnp.sum(lg_r[gi + 1:gi + 2, :], axis=-1, keepdims=True)

    t_idx = lax.broadcasted_iota(jnp.int32, (length, length), 0)
    s_idx = lax.broadcasted_iota(jnp.int32, (length, length), 1)
    keep = (s_idx >= t_idx) if reverse else (s_idx <= t_idx)
    log_w = jnp.where(keep, cum_c - cum_r + li_r, NEG_BIG)
    log_inter = cum_c + m
    m_t = jnp.maximum(log_inter, jnp.max(log_w, axis=-1, keepdims=True))
    w_inter = jnp.exp(log_inter - m_t) * scale
    qk = lax.dot_general(q, k, (((1,), (1,)), ((), ())), preferred_element_type=F32)
    s = qk * (jnp.exp(log_w - m_t) * scale)
    num = w_inter * jnp.dot(q, c_mat.astype(BF16), preferred_element_type=F32)
    num = num + jnp.dot(s.astype(BF16), v, preferred_element_type=F32)
    qn = jnp.sum(q.astype(F32) * n_vec, axis=-1, keepdims=True)
    den = w_inter * qn + jnp.sum(s, axis=-1, keepdims=True)
    h = num / jnp.maximum(jnp.abs(den), jnp.exp(-m_t))

    log_end_r = total - cum_r + li_r
    m_new = jnp.maximum(total + m, jnp.max(log_end_r, axis=-1, keepdims=True))
    decay = jnp.exp(total + m - m_new)
    w_end_c = jnp.exp(total - cum_c + li_c - m_new)
    kw = k.astype(F32) * w_end_c
    c_new = decay * c_mat + lax.dot_general(kw.astype(BF16), v, (((0,), (0,)), ((), ())),
                                            preferred_element_type=F32)
    n_new = decay * n_vec + jnp.sum(kw, axis=0, keepdims=True)
    return h, (c_new, n_new, m_new)


def _mlstm_kernel(with_ctx_out, qx, kx, vx, ox, qc, kc, vc, oc, gx, gc, tri_ref, ng_ref, *rest):
    if with_ctx_out:
        ax_ref, ac_ref, lgx_c, lgx_r, lgc_c, lgc_r, hx, hc = rest
    else:
        ax_ref, lgx_c, lgx_r, lgc_c, lgc_r, hx, hc = rest
        ac_ref = None
    head = pl.program_id(1)
    dk = qx.shape[1]
    scale = dk ** -0.5
    n_x_chunks = qx.shape[0] // ML_CHUNK
    n_c_chunks = qc.shape[0] // ML_CHUNK

    r_idx = lax.broadcasted_iota(jnp.int32, (LANES, LANES), 0)
    c_idx = lax.broadcasted_iota(jnp.int32, (LANES, LANES), 1)
    sel = jnp.where((r_idx == c_idx * ML_HEADS + head) & (c_idx < 4), 1.0, 0.0).astype(BF16)
    tri = tri_ref[...]

    def prep(g_ref, lg_c_ref, lg_r_ref):
        def body(i, carry):
            rows = pl.ds(pl.multiple_of(i * ML_CHUNK, ML_CHUNK), ML_CHUNK)
            gs = sum(jnp.dot(p, sel, preferred_element_type=F32) for p in _split3(g_ref[rows, :]))
            col = lax.broadcasted_iota(jnp.int32, gs.shape, 1)
            lg = jnp.where(col % 2 == 1, _log_sigmoid(gs), gs)
            pre = sum(jnp.dot(tri, p, preferred_element_type=F32) for p in _split3(lg))
            suf = pre[ML_CHUNK - 1:ML_CHUNK, :] - pre + lg
            lg = jnp.where(col == 4, pltpu.roll(pre, 3, axis=1), jnp.where(col == 5, pltpu.roll(suf, 2, axis=1), lg))
            lg_c_ref[rows, :] = lg
            lg_r_ref[i] = lg.T[:SUBLANES, :]
            return carry
        n_chunks = g_ref.shape[0] // ML_CHUNK
        lax.fori_loop(0, n_chunks, body, 0, unroll=2 if n_chunks % 2 == 0 else 1)

    prep(gx, lgx_c, lgx_r)
    prep(gc, lgc_c, lgc_r)

    for d in range(2):
        reverse = d == 1

        def step(q_ref, k_ref, v_ref, lg_c_ref, lg_r_ref, h_ref, ci, carry):
            rows = pl.ds(pl.multiple_of(ci * ML_CHUNK, ML_CHUNK), ML_CHUNK)
            h, carry = _ml_chunk(q_ref[rows, :], k_ref[rows, :], v_ref[rows, :], lg_c_ref[rows, :],
                                 lg_r_ref[ci], carry, reverse, scale)
            if reverse:
                h_ref[rows, :] += h
            else:
                h_ref[rows, :] = h
            return carry

        carry = (jnp.zeros((dk, vx.shape[1]), F32), jnp.zeros((1, dk), F32), jnp.zeros((1, 1), F32))

        def ctx_body(i, carry):
            ci = (n_c_chunks - 1 - i) if reverse else i
            return step(qc, kc, vc, lgc_c, lgc_r, hc, ci, carry)

        def x_body(i, carry):
            ci = (n_x_chunks - 1 - i) if reverse else i
            return step(qx, kx, vx, lgx_c, lgx_r, hx, ci, carry)

        carry = lax.fori_loop(0, n_c_chunks, ctx_body, carry)
        lax.fori_loop(0, n_x_chunks, x_body, carry)

    ng = ng_ref[0]

    def finish(h_ref, o_ref, a_ref):
        def body(i, carry):
            rows = pl.ds(pl.multiple_of(i * ML_CHUNK, ML_CHUNK), ML_CHUNK)
            h = h_ref[rows, :]
            hn = h * lax.rsqrt(jnp.mean(h * h, axis=-1, keepdims=True) + EPS) * ng
            a_ref[rows, :] = (hn * jax.nn.sigmoid(o_ref[rows, :].astype(F32))).astype(BF16)
            return carry
        lax.fori_loop(0, h_ref.shape[0] // ML_CHUNK, body, 0)

    finish(hx, ox, ax_ref)
    if with_ctx_out:
        finish(hc, oc, ac_ref)


def _mlstm(z, gz, tri, ml_norm_g, b, s, tc, dk, dv, cols, with_ctx_out):
    n_x = b * s
    cq, ck, cv, co = cols
    ctx0 = n_x // tc

    def xspec(width, col0):
        return pl.BlockSpec((s, width), lambda i, h: (i, col0 // width + h))

    def cspec(width, col0):
        return pl.BlockSpec((tc, width), lambda i, h: (ctx0 + i, col0 // width + h))

    out_shape = [jax.ShapeDtypeStruct((n_x, ML_HEADS * dv), BF16)]
    out_specs = [pl.BlockSpec((s, dv), lambda i, h: (i, h))]
    if with_ctx_out:
        out_shape.append(jax.ShapeDtypeStruct((b * tc, ML_HEADS * dv), BF16))
        out_specs.append(pl.BlockSpec((tc, dv), lambda i, h: (i, h)))
    res = pl.pallas_call(
        functools.partial(_mlstm_kernel, with_ctx_out),
        out_shape=tuple(out_shape),
        grid=(b, ML_HEADS),
        in_specs=[
            xspec(dk, cq), xspec(dk, ck), xspec(dv, cv), xspec(dv, co),
            cspec(dk, cq), cspec(dk, ck), cspec(dv, cv), cspec(dv, co),
            pl.BlockSpec((s, LANES), lambda i, h: (i, 0)),
            pl.BlockSpec((tc, LANES), lambda i, h: (ctx0 + i, 0)),
            pl.BlockSpec((ML_CHUNK, ML_CHUNK), lambda i, h: (0, 0)),
            pl.BlockSpec((1, 1, dv), lambda i, h: (h, 0, 0)),
        ],
        out_specs=tuple(out_specs),
        scratch_shapes=[
            pltpu.VMEM((s, LANES), F32), pltpu.VMEM((s // ML_CHUNK, SUBLANES, ML_CHUNK), F32),
            pltpu.VMEM((tc, LANES), F32), pltpu.VMEM((tc // ML_CHUNK, SUBLANES, ML_CHUNK), F32),
            pltpu.VMEM((s, dv), F32), pltpu.VMEM((tc, dv), F32),
        ],
        compiler_params=_cparams(("parallel", "parallel")),
        name="mlstm",
    )(z, z, z, z, z, z, z, z, gz, gz, tri, ml_norm_g.reshape(ML_HEADS, 1, dv))
    return res if with_ctx_out else (res[0], None)


def _mla_proj_kernel(qa_ref, kva_ref, kpe_ref, tab_ref, qag_ref, kvag_ref, wq_ref, wkv_ref,
                     gq_ref, gk_ref, q_ref, k_ref, v_ref):
    def normed(a_ref, g_ref):
        a = a_ref[...].astype(F32)
        return (a * lax.rsqrt(jnp.mean(a * a, axis=-1, keepdims=True) + EPS) * g_ref[...]).astype(BF16)

    q_all = jnp.dot(normed(qa_ref, qag_ref), wq_ref[...], preferred_element_type=F32)
    kv_all = jnp.dot(normed(kva_ref, kvag_ref), wkv_ref[...], preferred_element_type=F32)
    tab = tab_ref[...]
    lane = lax.broadcasted_iota(jnp.int32, tab.shape, 1)
    first_half = lane < MLA_ROPE
    gq = gq_ref[...]
    gk = gk_ref[...]
    inv_dqk = 1.0 / MLA_DQK

    kpe = kpe_ref[...].astype(F32)
    ss_kpe = jnp.sum(jnp.where(first_half, kpe * kpe, 0.0), axis=-1, keepdims=True)
    kpe_t = kpe * (tab * gk[:, LANES:])
    kpe_rot = jnp.where(first_half, kpe_t + pltpu.roll(kpe_t, MLA_ROPE, axis=1), 0.0)

    for h in range(MLA_HEADS):
        qn = q_all[:, h * MLA_SLAB:h * MLA_SLAB + LANES]
        qp = q_all[:, h * MLA_SLAB + LANES:(h + 1) * MLA_SLAB]
        ss = jnp.sum(qn * qn, axis=-1, keepdims=True) + jnp.sum(jnp.where(first_half, qp * qp, 0.0), axis=-1,
                                                                keepdims=True)
        r = lax.rsqrt(ss * inv_dqk + EPS) * Q_PRESCALE
        qp_t = qp * (tab * gq[:, LANES:])
        qp_rot = qp_t + pltpu.roll(qp_t, MLA_ROPE, axis=1)
        q_ref[:, h * MLA_SLAB:h * MLA_SLAB + LANES] = (qn * r * gq[:, :LANES]).astype(BF16)
        q_ref[:, h * MLA_SLAB + LANES:(h + 1) * MLA_SLAB] = (qp_rot * r).astype(BF16)

        kn = kv_all[:, h * LANES:(h + 1) * LANES]
        rk = lax.rsqrt((jnp.sum(kn * kn, axis=-1, keepdims=True) + ss_kpe) * inv_dqk + EPS)
        k_ref[:, h * MLA_SLAB:h * MLA_SLAB + LANES] = (kn * rk * gk[:, :LANES]).astype(BF16)
        k_ref[:, h * MLA_SLAB + LANES:(h + 1) * MLA_SLAB] = (kpe_rot * rk).astype(BF16)

    ones_col = jnp.where(lane == 0, 1.0, 0.0).astype(BF16)
    for h in range(MLA_HEADS):
        v_ref[:, h * MLA_SLAB:h * MLA_SLAB + LANES] = kv_all[:, (MLA_HEADS + h) * LANES:(MLA_HEADS + h + 1) * LANES
                                                             ].astype(BF16)
        v_ref[:, h * MLA_SLAB + LANES:(h + 1) * MLA_SLAB] = ones_col


def _mla_proj(z, tab, qag, kvag, wq, wkv, gq, gk, tm, cols, n_x_tiles, tab_tiles):
    r = z.shape[0]
    cqa, ckva, ckpe = cols
    lora = qag.shape[1]
    hs = MLA_HEADS * MLA_SLAB
    hv = MLA_HEADS * LANES
    return pl.pallas_call(
        _mla_proj_kernel,
        out_shape=(jax.ShapeDtypeStruct((r, hs), BF16), jax.ShapeDtypeStruct((r, hs), BF16),
                   jax.ShapeDtypeStruct((r, hs), BF16)),
        grid=(r // tm,),
        in_specs=[
            pl.BlockSpec((tm, lora), lambda i: (i, cqa // lora)),
            pl.BlockSpec((tm, lora), lambda i: (i, ckva // lora)),
            pl.BlockSpec((tm, LANES), lambda i: (i, ckpe // LANES)),
            pl.BlockSpec((tm, LANES), lambda i: (jnp.where(i < n_x_tiles, i % tab_tiles, tab_tiles), 0)),
            pl.BlockSpec((1, lora), lambda i: (0, 0)),
            pl.BlockSpec((1, lora), lambda i: (0, 0)),
            pl.BlockSpec((lora, hs), lambda i: (0, 0)),
            pl.BlockSpec((lora, 2 * hv), lambda i: (0, 0)),
            pl.BlockSpec((1, MLA_SLAB), lambda i: (0, 0)),
            pl.BlockSpec((1, MLA_SLAB), lambda i: (0, 0)),
        ],
        out_specs=(pl.BlockSpec((tm, hs), lambda i: (i, 0)), pl.BlockSpec((tm, hs), lambda i: (i, 0)),
                   pl.BlockSpec((tm, hs), lambda i: (i, 0))),
        compiler_params=_cparams(("parallel",)),
        name="mla_qkv",
    )(z, z, z, tab, qag, kvag, wq, wkv, gq, gk)


def _attn_kernel(n_kv, tq, q_ref, *refs):
    k_refs = refs[:n_kv]
    v_refs = refs[n_kv:2 * n_kv]
    o_ref, s0_ref, s1_ref, m0_ref, m1_ref = refs[2 * n_kv:]
    slots = ((s0_ref, m0_ref), (s1_ref, m1_ref))
    n_tiles = q_ref.shape[0] // tq
    chunks = []
    col = 0
    for kv, k_ref in enumerate(k_refs):
        n_keys = k_ref.shape[0]
        step = min(ATTN_KEY_CHUNK, n_keys)
        for off in range(0, n_keys, step):
            chunks.append((kv, off, col, step))
            col += step

    def scores(t, slot):
        s_ref, m_ref = slots[slot]
        rows = pl.ds(pl.multiple_of(t * tq, tq), tq)
        q = q_ref[rows, :]
        run = None
        for kv, off, c0, size in chunks:
            s = lax.dot_general(q, k_refs[kv][off:off + size, :], (((1,), (1,)), ((), ())),
                                preferred_element_type=F32)
            s_ref[:, c0:c0 + size] = s
            for lb in range(size // LANES):
                blk = s[:, lb * LANES:(lb + 1) * LANES]
                run = blk if run is None else jnp.maximum(run, blk)
        m_ref[...] = run

    def finish(t, slot):
        s_ref, m_ref = slots[slot]
        rows = pl.ds(pl.multiple_of(t * tq, tq), tq)
        m = jnp.max(m_ref[...], axis=-1, keepdims=True)
        acc = None
        for kv, off, c0, size in chunks:
            p = jnp.exp2(s_ref[:, c0:c0 + size] - m).astype(BF16)
            pv = jnp.dot(p, v_refs[kv][off:off + size, :], preferred_element_type=F32)
            acc = pv if acc is None else acc + pv
        o_ref[rows, :] = (acc[:, :LANES] / acc[:, LANES:LANES + 1]).astype(BF16)

    scores(0, 0)

    def body(k, carry):
        scores(2 * k + 1, 1)
        finish(2 * k, 0)
        scores(jnp.minimum(2 * k + 2, n_tiles - 1), 0)
        finish(2 * k + 1, 1)
        return carry

    lax.fori_loop(0, n_tiles // 2, body, 0)
    if n_tiles % 2:
        finish(n_tiles - 1, 0)


def _attention(qo, ko, vo, b, s, tc, tq, latent):
    n_x = b * s
    ctx0 = n_x // tc
    cspec = pl.BlockSpec((tc, MLA_SLAB), lambda i, h: (ctx0 + i, h))
    xspec = pl.BlockSpec((s, MLA_SLAB), lambda i, h: (i, h))
    if latent:
        n_q, n_keys = s, s + tc
        in_specs = [xspec, cspec, xspec, cspec, xspec]
        args = (qo, ko, ko, vo, vo)
    else:
        n_q, n_keys = tc, tc
        in_specs = [cspec, cspec, cspec]
        args = (qo, ko, vo)
    return pl.pallas_call(
        functools.partial(_attn_kernel, (len(args) - 1) // 2, tq),
        out_shape=jax.ShapeDtypeStruct((b * n_q, MLA_HEADS * LANES), BF16),
        grid=(b, MLA_HEADS),
        in_specs=in_specs,
        out_specs=pl.BlockSpec((n_q, LANES), lambda i, h: (i, h)),
        scratch_shapes=[pltpu.VMEM((tq, n_keys), F32), pltpu.VMEM((tq, n_keys), F32),
                        pltpu.VMEM((tq, LANES), F32), pltpu.VMEM((tq, LANES), F32)],
        compiler_params=_cparams(("parallel", "parallel")),
        name="attn_latent" if latent else "attn_ctx",
    )(*args)


S5_HALF = S5_BLOCK_GROUPS * S5_STATE


def _s5_kernel(n_batch, rows_x, rows_c, dot_rows, u_ref, fac_ref, pw_ref, tap_ref, tab_ref, y_ref,
               u2_ref, v_ref, r_ref, m_ref, ot_ref):
    d = pl.program_id(1)
    n_dot = u2_ref.shape[0] // dot_rows

    for s in range(S5_SUB):
        blk = slice(s * LANES, (s + 1) * LANES)
        r_ref[blk, :] = (fac_ref[0, 0, 0] * pw_ref[0, 0, 0, s:s + 1, :]
                         + fac_ref[0, 0, 1] * pw_ref[0, 0, 1, s:s + 1, :]).astype(BF16)
        ot_ref[blk, :] = (fac_ref[0, 0, 2] * pw_ref[0, 0, 2, s:s + 1, :]
                          + fac_ref[0, 0, 3] * pw_ref[0, 0, 3, s:s + 1, :]).astype(BF16)
        for t in range(S5_SUB):
            m_ref[blk, t * LANES:(t + 1) * LANES] = tap_ref[0, 0, t - s + S5_SUB - 1].astype(BF16)

    def dot_rows_of(i):
        return pl.ds(pl.multiple_of(i * dot_rows, dot_rows), dot_rows)

    def token_rows_of(i, s):
        return pl.ds(i * (dot_rows * S5_SUB) + s, dot_rows, stride=S5_SUB)

    @pl.when(d == 0)
    def _():
        def stage(i, carry):
            rows = pl.ds(pl.multiple_of(i * dot_rows, dot_rows), dot_rows)
            y_ref[rows, :] = u_ref[rows, :].astype(F32)
            return carry

        lax.fori_loop(0, n_dot * S5_SUB, stage, 0)

        def regroup(i, carry):
            for s in range(S5_SUB):
                u2_ref[dot_rows_of(i), s * LANES:(s + 1) * LANES] = y_ref[token_rows_of(i, s), :].astype(BF16)
            return carry

        lax.fori_loop(0, n_dot, regroup, 0)

    def increments(i, carry):
        rows = dot_rows_of(i)
        v_ref[rows, :] = jnp.dot(u2_ref[rows, :], r_ref[...], preferred_element_type=F32)
        return carry

    lax.fori_loop(0, n_dot, increments, 0)

    def cmul_add(ar, ai, cr, ci, xr, xi):
        return ar + cr * xr - ci * xi, ai + cr * xi + ci * xr

    def run(reverse):
        tab = tab_ref.at[0, 0]
        last = 0 if reverse else SUBLANES - 1
        first_row = lax.broadcasted_iota(jnp.int32, (SUBLANES, S5_HALF), 0) == (SUBLANES - 1 - last)

        def segment(bases, n_groups, carry):
            def body(i, carry):
                gi = (n_groups - 1 - i) if reverse else i
                out = []
                for base, (cre, cim) in zip(bases, carry):
                    rows = pl.ds(pl.multiple_of(base + gi * SUBLANES, SUBLANES), SUBLANES)
                    re = v_ref[rows, :S5_HALF]
                    im = v_ref[rows, S5_HALF:]
                    for lvl, shift in enumerate((1, 2, 4)):
                        sh = (SUBLANES - shift) if reverse else shift
                        re, im = cmul_add(re, im, tab[2 * lvl], tab[2 * lvl + 1],
                                          pltpu.roll(re, sh, axis=0), pltpu.roll(im, sh, axis=0))
                    re, im = cmul_add(re, im, tab[6], tab[7], cre, cim)
                    sh1 = (SUBLANES - 1) if reverse else 1
                    v_ref[rows, :S5_HALF] = jnp.where(first_row, cre, pltpu.roll(re, sh1, axis=0))
                    v_ref[rows, S5_HALF:] = jnp.where(first_row, cim, pltpu.roll(im, sh1, axis=0))
                    out.append((jnp.broadcast_to(re[last:last + 1, :], re.shape),
                                jnp.broadcast_to(im[last:last + 1, :], im.shape)))
                return tuple(out)
            return lax.fori_loop(0, n_groups, body, carry)

        zero = jnp.zeros((SUBLANES, S5_HALF), F32)
        carry = tuple((zero, zero) for _ in range(n_batch))
        carry = segment([n_batch * rows_x + bi * rows_c for bi in range(n_batch)], rows_c // SUBLANES, carry)
        segment([bi * rows_x for bi in range(n_batch)], rows_x // SUBLANES, carry)

        def outputs(i, carry):
            rows = dot_rows_of(i)
            y = jnp.dot(u2_ref[rows, :], m_ref[...], preferred_element_type=F32)
            y = y + lax.dot_general(v_ref[rows, :].astype(BF16), ot_ref[...], (((1,), (1,)), ((), ())),
                                    preferred_element_type=F32)
            for s in range(S5_SUB):
                part = y[:, s * LANES:(s + 1) * LANES]
                if reverse:
                    y_ref[token_rows_of(i, s), :] += part
                else:
                    y_ref[token_rows_of(i, s), :] = part
            return carry

        lax.fori_loop(0, n_dot, outputs, 0)

    @pl.when(d == 0)
    def _():
        run(False)

    @pl.when(d == 1)
    def _():
        run(True)


def _s5_scan(z, fac, pw, taps, tabs, b, s, tc, col_u):
    r = z.shape[0]
    rc = r // S5_SUB
    n_blocks = fac.shape[1]
    width = S5_SUB * LANES
    assert width == 2 * S5_HALF
    dot_rows = max(n for n in range(16, 641, 16) if rc % n == 0)
    return pl.pallas_call(
        functools.partial(_s5_kernel, b, s // S5_SUB, tc // S5_SUB, dot_rows),
        out_shape=jax.ShapeDtypeStruct((r, n_blocks * LANES), F32),
        grid=(n_blocks, 2),
        in_specs=[
            pl.BlockSpec((r, LANES), lambda cb, d: (0, col_u // LANES + cb), pipeline_mode=pl.Buffered(1)),
            pl.BlockSpec((1, 1, 4, LANES, width), lambda cb, d: (d, cb, 0, 0, 0), pipeline_mode=pl.Buffered(1)),
            pl.BlockSpec((1, 1, 4, S5_SUB, width), lambda cb, d: (d, cb, 0, 0, 0)),
            pl.BlockSpec((1, 1, 2 * S5_SUB - 1, LANES, LANES), lambda cb, d: (d, cb, 0, 0, 0)),
            pl.BlockSpec((1, 1, 8, SUBLANES, S5_HALF), lambda cb, d: (d, cb, 0, 0, 0)),
        ],
        out_specs=pl.BlockSpec((r, LANES), lambda cb, d: (0, cb)),
        scratch_shapes=[pltpu.VMEM((rc, width), BF16), pltpu.VMEM((rc, width), F32),
                        pltpu.VMEM((width, width), BF16), pltpu.VMEM((width, width), BF16),
                        pltpu.VMEM((width, width), BF16)],
        compiler_params=_cparams(("parallel", "arbitrary")),
        name="s5_scan",
    )(z, fac, pw, taps, tabs)


def _glu_kernel(y_ref, u_ref, d_ref, w_ref, b_ref, o_ref, g_ref):
    tm = y_ref.shape[0]

    def body(r, carry):
        rows = pl.ds(pl.multiple_of(r * ROW_CHUNK, ROW_CHUNK), ROW_CHUNK)
        y = y_ref[rows, :] + d_ref[...] * u_ref[rows, :].astype(F32)
        g_ref[rows, :] = jax.nn.gelu(y).astype(BF16)
        return carry

    lax.fori_loop(0, tm // ROW_CHUNK, body, 0)
    g = g_ref[...]
    gate = jax.nn.sigmoid(jnp.dot(g, w_ref[...], preferred_element_type=F32) + b_ref[...])
    o_ref[...] = (g.astype(F32) * gate).astype(BF16)


def _s5_glu(y, z, d_skip, w_glu, b_glu, tm, col_u, n_row_tiles):
    width = y.shape[1]
    return pl.pallas_call(
        _glu_kernel,
        out_shape=jax.ShapeDtypeStruct((n_row_tiles * tm, width), BF16),
        grid=(n_row_tiles,),
        in_specs=[
            pl.BlockSpec((tm, width), lambda i: (i, 0)),
            pl.BlockSpec((tm, width), lambda i: (i, col_u // width)),
            pl.BlockSpec((1, width), lambda i: (0, 0)),
            pl.BlockSpec((width, width), lambda i: (0, 0)),
            pl.BlockSpec((1, width), lambda i: (0, 0)),
        ],
        out_specs=pl.BlockSpec((tm, width), lambda i: (i, 0)),
        scratch_shapes=[pltpu.VMEM((tm, width), BF16)],
        compiler_params=_cparams(("parallel",)),
        name="s5_glu",
    )(y, z, d_skip, w_glu, b_glu)


def _merge_kernel(n_x_tiles, ax_ref, ac_ref, bx_ref, bc_ref, c_ref, ga_ref, gb_ref, gc_ref, w_ref, o_ref):
    def combine(a_ref, b_ref):
        acc = None
        for r, (br_ref, gate_ref) in enumerate(((a_ref, ga_ref), (b_ref, gb_ref), (c_ref, gc_ref))):
            proj = jnp.dot(br_ref[...], w_ref[r], preferred_element_type=F32)
            term = jax.nn.sigmoid(gate_ref[...].astype(F32)) * proj
            acc = term if acc is None else acc + term
        o_ref[...] = acc.astype(BF16)

    @pl.when(pl.program_id(0) < n_x_tiles)
    def _():
        combine(ax_ref, bx_ref)

    @pl.when(pl.program_id(0) >= n_x_tiles)
    def _():
        combine(ac_ref, bc_ref)


def _merge(a_x, a_c, b_x, b_c, cc, z, w_branch, tm, tn, col_g, n_row_tiles, n_x_tiles):
    width = a_x.shape[1]
    d = w_branch.shape[2]

    def gate_spec(r):
        return pl.BlockSpec((tm, tn), lambda i, j: (i, (col_g + r * d) // tn + j))

    ax_spec, ac_spec = _stream_specs((tm, width), n_x_tiles, 0, lambda j: 0)
    return pl.pallas_call(
        functools.partial(_merge_kernel, n_x_tiles),
        out_shape=jax.ShapeDtypeStruct((n_row_tiles * tm, d), BF16),
        grid=(n_row_tiles, d // tn),
        in_specs=[
            ax_spec, ac_spec, ax_spec, ac_spec,
            pl.BlockSpec((tm, width), lambda i, j: (i, 0)),
            gate_spec(0), gate_spec(1), gate_spec(2),
            pl.BlockSpec((N_BRANCH, width, tn), lambda i, j: (0, 0, j)),
        ],
        out_specs=pl.BlockSpec((tm, tn), lambda i, j: (i, j)),
        compiler_params=_cparams(("parallel", "arbitrary")),
        name="merge",
    )(a_x, a_c, b_x, b_c, cc, z, z, z, w_branch)


def _resid_kernel(m_ref, w_ref, x_ref, al_ref, o_ref):
    o_ref[...] = x_ref[...] + al_ref[0] * jnp.dot(m_ref[...], w_ref[...], preferred_element_type=F32)


def _out_proj_residual(m, w, xs, mod3, tm, tn, n_row_tiles, n_x_tiles, tiles_per_batch, ctx_row, k_alpha):
    kdim, d = w.shape
    nt = d // tn

    def alpha_map(i, j):
        r = jnp.where(i < n_x_tiles, i // tiles_per_batch, ctx_row)
        return (r * N_MOD + k_alpha, 0, j)

    return pl.pallas_call(
        _resid_kernel,
        out_shape=jax.ShapeDtypeStruct((n_row_tiles * tm, d), F32),
        grid=(n_row_tiles, nt),
        in_specs=[
            pl.BlockSpec((tm, kdim), lambda i, j: (i, 0)),
            pl.BlockSpec((kdim, tn), lambda i, j: (0, j)),
            pl.BlockSpec((tm, tn), lambda i, j: (i, j)),
            pl.BlockSpec((1, 1, tn), alpha_map),
        ],
        out_specs=pl.BlockSpec((tm, tn), lambda i, j: (i, j)),
        compiler_params=_cparams(("parallel", "arbitrary")),
        name="out_proj",
    )(m, w, xs, mod3)


def _ff1_kernel(x_ref, g_ref, sh_ref, sc_ref, w_ref, h_ref, xn_ref):
    @pl.when(pl.program_id(1) == 0)
    def _():
        _norm_mod_rows(x_ref, g_ref, sh_ref, sc_ref, xn_ref)

    a = jnp.maximum(jnp.dot(xn_ref[...], w_ref[...], preferred_element_type=F32), 0.0)
    h_ref[...] = (a * a).astype(BF16)


def _ff1(xs, g, mod3, w, tm, tn, n_row_tiles, n_x_tiles, tiles_per_batch, ctx_row):
    d, dff = w.shape
    return pl.pallas_call(
        _ff1_kernel,
        out_shape=jax.ShapeDtypeStruct((n_row_tiles * tm, dff), BF16),
        grid=(n_row_tiles, dff // tn),
        in_specs=[
            pl.BlockSpec((tm, d), lambda i, j: (i, 0)),
            pl.BlockSpec((1, d), lambda i, j: (0, 0)),
            pl.BlockSpec((1, 1, d), _mod_row_map(n_x_tiles, tiles_per_batch, ctx_row, 3)),
            pl.BlockSpec((1, 1, d), _mod_row_map(n_x_tiles, tiles_per_batch, ctx_row, 4)),
            pl.BlockSpec((d, tn), lambda i, j: (0, j)),
        ],
        out_specs=pl.BlockSpec((tm, tn), lambda i, j: (i, j)),
        scratch_shapes=[pltpu.VMEM((tm, d), BF16)],
        compiler_params=_cparams(("parallel", "arbitrary")),
        name="ff1",
    )(xs, g, mod3, mod3, w)


def _ff2_kernel(h_ref, w_ref, x_ref, al_ref, o_ref):
    k = pl.program_id(2)
    part = jnp.dot(h_ref[...], w_ref[...], preferred_element_type=F32)

    @pl.when(k == 0)
    def _():
        o_ref[...] = part

    @pl.when(k > 0)
    def _():
        o_ref[...] += part

    @pl.when(k == pl.num_programs(2) - 1)
    def _():
        o_ref[...] = x_ref[...] + al_ref[0] * o_ref[...]


def _ff2(h, w, xs, mod3, tm, tn, tk, n_row_tiles, n_x_tiles, tiles_per_batch, ctx_row):
    dff, d = w.shape

    def alpha_map(i, j, k):
        r = jnp.where(i < n_x_tiles, i // tiles_per_batch, ctx_row)
        return (r * N_MOD + 5, 0, j)

    return pl.pallas_call(
        _ff2_kernel,
        out_shape=jax.ShapeDtypeStruct((n_row_tiles * tm, d), F32),
        grid=(n_row_tiles, d // tn, dff // tk),
        in_specs=[
            pl.BlockSpec((tm, tk), lambda i, j, k: (i, k)),
            pl.BlockSpec((tk, tn), lambda i, j, k: (k, j)),
            pl.BlockSpec((tm, tn), lambda i, j, k: (i, j)),
            pl.BlockSpec((1, 1, tn), alpha_map),
        ],
        out_specs=pl.BlockSpec((tm, tn), lambda i, j, k: (i, j)),
        compiler_params=_cparams(("parallel", "parallel", "arbitrary")),
        name="ff2",
    )(h, w, xs, mod3)


def _rope_partner():
    j = np.arange(MLA_ROPE)
    quarter = MLA_ROPE // 4
    return np.where((j // quarter) % 2 == 0, j + quarter, j - quarter)


def _rope_table(s, tm):
    pos = jnp.arange(s)
    row = (pos // GRID_W).astype(F32)
    col = (pos % GRID_W).astype(F32)
    n_freq = MLA_ROPE // 4
    inv_freq = ROPE_THETA ** (-jnp.arange(n_freq, dtype=F32) / n_freq)
    ang_r = row[:, None] * inv_freq
    ang_c = col[:, None] * inv_freq
    cos = jnp.concatenate([jnp.cos(ang_r)] * 2 + [jnp.cos(ang_c)] * 2, axis=-1)
    sin = jnp.concatenate([-jnp.sin(ang_r), jnp.sin(ang_r), -jnp.sin(ang_c), jnp.sin(ang_c)], axis=-1)
    ident = jnp.concatenate([jnp.ones((tm, MLA_ROPE), F32), jnp.zeros((tm, MLA_ROPE), F32)], axis=-1)
    return jnp.concatenate([jnp.concatenate([cos, sin], axis=-1), ident], axis=0)


def _pack_w_in(w_in, b_in, gate_b, sizes, n_pad):
    bounds = np.cumsum((0,) + sizes)
    seg = [slice(int(bounds[i]), int(bounds[i + 1])) for i in range(len(sizes))]
    partner = _rope_partner()
    order = (0, 1, 2, 3, 5, 6, 8, 9)
    w_parts = [w_in[:, seg[i]] for i in order]
    b_parts = [b_in[seg[i]] for i in order]
    kpe_w = w_in[:, seg[7]]
    kpe_b = b_in[seg[7]]
    w_parts += [kpe_w, kpe_w[:, partner]]
    b_parts += [kpe_b, kpe_b[partner]]
    offs = {}
    pos = 0
    for name, part in zip(("q", "k", "v", "o", "qa", "kva", "u", "gates", "kpe", "kpe_sw"), w_parts):
        offs[name] = pos
        pos += part.shape[1]
    pad = n_pad - pos
    w_parts.append(jnp.zeros((w_in.shape[0], pad), w_in.dtype))
    b_parts.append(jnp.zeros((pad,), b_in.dtype))
    w = jnp.concatenate(w_parts, axis=1).astype(BF16)
    b = jnp.concatenate(b_parts)[None, :]
    n_g = sizes[4]
    wg = jnp.concatenate([w_in[:, seg[4]], jnp.zeros((w_in.shape[0], LANES - n_g), w_in.dtype)], axis=1).astype(BF16)
    bg = jnp.concatenate([b_in[seg[4]] + gate_b.reshape(-1), jnp.zeros((LANES - n_g,), F32)])[None, :]
    return w, b, wg, bg, offs


def _pack_mla(w_uq, w_ukv, qn_g, kn_g):
    partner = _rope_partner()
    lora = w_uq.shape[0]
    wq = w_uq.reshape(lora, MLA_HEADS, MLA_DQK)
    wq = jnp.concatenate([wq, wq[:, :, MLA_NOPE + partner]], axis=-1).reshape(lora, MLA_HEADS * MLA_SLAB)
    wkv = w_ukv.reshape(w_ukv.shape[0], MLA_HEADS, -1)
    wkv = jnp.concatenate([wkv[:, :, :MLA_NOPE].reshape(lora, -1), wkv[:, :, MLA_NOPE:].reshape(lora, -1)], axis=-1)

    def gains(g):
        return jnp.concatenate([g, g[MLA_NOPE + partner]])[None, :]

    return wq.astype(BF16), wkv.astype(BF16), gains(qn_g), gains(kn_g)


def _pack_s5(a_re, a_im, log_dt, b_re, b_im, c_re, c_im):
    n_dir, n_groups, n_state = a_re.shape
    gc = b_re.shape[-1]
    nb = n_groups // S5_BLOCK_GROUPS
    lam_re = jnp.minimum(a_re.astype(F32), -1e-4)
    lam_im = a_im.astype(F32)
    dt = jnp.exp(log_dt.astype(F32))[..., None]

    def pole_power(k):
        mag = jnp.exp(k * lam_re * dt)
        return mag * jnp.cos(k * lam_im * dt), mag * jnp.sin(k * lam_im * dt)

    bar_re, bar_im = pole_power(1.0)
    den = lam_re * lam_re + lam_im * lam_im
    f_re = ((bar_re - 1.0) * lam_re + bar_im * lam_im) / den
    f_im = (bar_im * lam_re - (bar_re - 1.0) * lam_im) / den
    bb_re = f_re[..., None] * b_re.astype(F32) - f_im[..., None] * b_im.astype(F32)
    bb_im = f_re[..., None] * b_im.astype(F32) + f_im[..., None] * b_re.astype(F32)
    eye = jnp.eye(S5_BLOCK_GROUPS, dtype=F32)
    sub = S5_SUB

    def per_block(a):
        return a.reshape(a.shape[:-2] + (nb, S5_BLOCK_GROUPS * n_state))

    def block_b(part):
        p = part.reshape(n_dir, nb, S5_BLOCK_GROUPS, n_state, gc)
        m = jnp.einsum('dbgnc,gh->dbgchn', p, eye, precision=HIGHEST)
        return m.reshape(n_dir, nb, LANES, S5_HALF)

    def block_c(part):
        p = part.astype(F32).reshape(n_dir, nb, S5_BLOCK_GROUPS, gc, n_state)
        m = jnp.einsum('dbgcn,gh->dbgnhc', p, eye, precision=HIGHEST)
        return m.reshape(n_dir, nb, S5_HALF, LANES)

    bm_re, bm_im = block_b(bb_re), block_b(bb_im)
    cm_re, cm_im = block_c(c_re), block_c(c_im)

    tau = jnp.arange(sub + 1, dtype=F32)[:, None, None, None]
    p_re, p_im = (per_block(p) for p in pole_power(tau))
    pr, pi = p_re[:sub, :, :, None, :], p_im[:sub, :, :, None, :]
    taps = (jnp.einsum('zdbkn,dbnc->zdbkc', bm_re * pr - bm_im * pi, cm_re, precision=HIGHEST)
            - jnp.einsum('zdbkn,dbnc->zdbkc', bm_re * pi + bm_im * pr, cm_im, precision=HIGHEST))

    ct_re, ct_im = jnp.swapaxes(cm_re, -1, -2), jnp.swapaxes(cm_im, -1, -2)
    cat = functools.partial(jnp.concatenate, axis=-1)
    fac = jnp.stack([cat([bm_re, bm_im]), cat([-bm_im, bm_re]), cat([ct_re, -ct_im]), cat([-ct_im, -ct_re])],
                    axis=2)
    t = np.arange(sub)
    zero_taps = jnp.zeros((sub - 1,) + taps.shape[2:], F32)
    pws, taps2 = [], []
    for d in range(n_dir):
        to_exit = (sub - 1 - t) if d == 0 else t
        age = (t + 1) if d == 0 else (sub - t)
        rows = [p_re[to_exit, d], p_im[to_exit, d], p_re[age, d], p_im[age, d]]
        pws.append(jnp.stack([cat([a, a]).transpose(1, 0, 2) for a in rows], axis=1))
        taps2.append(jnp.concatenate([zero_taps, taps[:, 0]] if d == 0 else [taps[::-1, 1], zero_taps], axis=0))
    pw = jnp.stack(pws)
    taps2 = jnp.stack(taps2).transpose(0, 2, 1, 3, 4)

    rows = jnp.arange(SUBLANES)
    tabs = []
    for d in range(n_dir):
        per_dir = []
        for shift in (1, 2, 4):
            keep = ((rows <= SUBLANES - 1 - shift) if d == 1 else (rows >= shift))[None, :, None]
            s_re, s_im = pole_power(float(shift * sub))
            per_dir += [jnp.where(keep, per_block(s_re[d])[:, None, :], 0.0),
                        jnp.where(keep, per_block(s_im[d])[:, None, :], 0.0)]
        expo = (((SUBLANES - rows) if d == 1 else (rows + 1)) * sub).astype(F32)
        s_re, s_im = pole_power(expo[:, None, None, None])
        per_dir += [jnp.moveaxis(per_block(s_re[:, d]), 0, 1), jnp.moveaxis(per_block(s_im[:, d]), 0, 1)]
        tabs.append(jnp.stack(per_dir, axis=1))
    return fac, pw, taps2, jnp.stack(tabs).astype(F32)


def _tri_matrices():
    t = np.arange(ML_CHUNK)
    return jnp.asarray((t[None, :] <= t[:, None]).astype(np.float32), dtype=BF16)


def kernel(x, c, ctx, c_ctx, w_mod, b_mod, norm_g, w_in, b_in, ml_gate_b, ml_norm_g, mla_qa_g, mla_kva_g, mla_w_uq, mla_w_ukv, mla_qn_g, mla_kn_g, s5_a_re, s5_a_im, s5_log_dt, s5_b_re, s5_b_im, s5_c_re, s5_c_im, s5_d, s5_w_glu, s5_b_glu, w_branch, w_out, w_ff1, w_ff2):
    b, s, d = x.shape
    tc = ctx.shape[1]
    depth = w_mod.shape[0]
    dv = ml_norm_g.shape[2]
    dk = dv // 2
    lora = mla_qa_g.shape[1]
    s5_width = s5_d.shape[1]
    branch_w = w_branch.shape[2]
    sizes = (ML_HEADS * dk, ML_HEADS * dk, ML_HEADS * dv, ML_HEADS * dv, 4 * ML_HEADS, lora, lora, MLA_ROPE,
             s5_width, N_BRANCH * d)
    assert sum(sizes) == w_in.shape[2] and b + 1 <= SUBLANES
    assert s % ML_CHUNK == 0 and tc % ML_CHUNK == 0 and branch_w == ML_HEADS * dv == MLA_HEADS * LANES == s5_width

    n_x = b * s
    n_c = b * tc
    tm = _row_tile(s, n_c)
    n_x_tiles = n_x // tm
    n_tiles = n_x_tiles + n_c // tm
    tiles_per_batch = s // tm
    tile_args = (n_x_tiles, tiles_per_batch, b)

    xs = jnp.concatenate([x.reshape(n_x, d), ctx.reshape(n_c, d)], axis=0)
    cc = jnp.concatenate([c, c_ctx[None, :], jnp.zeros((SUBLANES - b - 1, d), F32)], axis=0)
    mod = _modulation(cc, w_mod, b_mod)
    tm_q = min(tm, 512)
    tab = _rope_table(s, tm_q)
    tri = _tri_matrices()
    n_used = sum(sizes) - sizes[4] + MLA_ROPE
    tn_in = 1280
    n_pad = -(-n_used // tn_in) * tn_in

    for l in range(depth):
        with_ctx_out = l < depth - 1
        mod3 = mod[l].reshape(SUBLANES * N_MOD, 1, d)
        w_p, b_p, wg, bg, offs = _pack_w_in(w_in[l], b_in[l], ml_gate_b[l], sizes, n_pad)
        z, gz = _in_proj(xs, norm_g[l, 0][None, :], mod3, w_p, b_p, wg, bg, tm, tn_in, *tile_args)

        a_x, a_c = _mlstm(z, gz, tri, ml_norm_g[l], b, s, tc, dk, dv,
                          (offs["q"], offs["k"], offs["v"], offs["o"]), with_ctx_out)

        wq, wkv, gq, gk = _pack_mla(mla_w_uq[l], mla_w_ukv[l], mla_qn_g[l], mla_kn_g[l])
        qo, ko, vo = _mla_proj(z, tab, mla_qa_g[l][None, :], mla_kva_g[l][None, :], wq, wkv, gq, gk, tm_q,
                               (offs["qa"], offs["kva"], offs["kpe"]), n_x // tm_q, s // tm_q)
        tq = min(256, s)
        b_x = _attention(qo, ko, vo, b, s, tc, tq, True)

        fac, pw, taps, tabs = _pack_s5(s5_a_re[l], s5_a_im[l], s5_log_dt[l], s5_b_re[l], s5_b_im[l],
                                       s5_c_re[l], s5_c_im[l])
        y = _s5_scan(z, fac, pw, taps, tabs, b, s, tc, offs["u"])
        n_out_tiles = n_tiles if with_ctx_out else n_x_tiles
        c_all = _s5_glu(y, z, s5_d[l][None, :], s5_w_glu[l].astype(BF16), s5_b_glu[l][None, :], tm, offs["u"],
                        n_out_tiles)

        if with_ctx_out:
            b_c = _attention(qo, ko, vo, b, s, tc, min(tq, tc), False)
        else:
            a_c, b_c = a_x, b_x
        merged = _merge(a_x, a_c, b_x, b_c, c_all, z, w_branch[l].astype(BF16), tm, 512, offs["gates"],
                        n_out_tiles, n_x_tiles)
        xs1 = _out_proj_residual(merged, w_out[l].astype(BF16), xs, mod3, tm, 1024, n_out_tiles, *tile_args, 2)
        hid = _ff1(xs1, norm_g[l, 1][None, :], mod3, w_ff1[l].astype(BF16), tm, 1024, n_out_tiles, *tile_args)
        xs = _ff2(hid, w_ff2[l].astype(BF16), xs1, mod3, tm, 1024, 2048, n_out_tiles, *tile_args)

    return xs.reshape(b, s, d)
```

```python
import functools
import math

import jax
import jax.numpy as jnp
import numpy as np
from jax import lax
from jax.experimental import pallas as pl
from jax.experimental.pallas import tpu as pltpu

F32 = jnp.float32
BF16 = jnp.bfloat16
HIGHEST = lax.Precision.HIGHEST

N_MOD = 6
N_BRANCH = 3
ML_HEADS = 4
MLA_HEADS = 8
MLA_NOPE = 128
MLA_ROPE = 64
MLA_DQK = MLA_NOPE + MLA_ROPE
MLA_SLAB = 256
GRID_W = 64
ROPE_THETA = 10000.0
S5_GROUP = 16
S5_STATE = 64
S5_BLOCK_GROUPS = 8
EPS = 1e-6
NEG_BIG = -1e30

LANES = 128
SUBLANES = 8
VMEM_LIMIT = 56 * 1024 * 1024

ML_CHUNK = 256
S5_SUB = 8
ATTN_KEY_CHUNK = 512
Q_PRESCALE = MLA_DQK ** -0.5 * math.log2(math.e)
ROW_CHUNK = 64


def _cparams(sem):
    return pltpu.CompilerParams(dimension_semantics=sem, vmem_limit_bytes=VMEM_LIMIT)


def _row_tile(n_x_rows_per_batch, n_ctx_rows):
    tm = 1024
    while n_x_rows_per_batch % tm or n_ctx_rows % tm:
        tm //= 2
    return tm


def _mod_kernel(c_ref, w_ref, b_ref, o_ref):
    s = c_ref[...]
    s = s * jax.nn.sigmoid(s)
    o_ref[0] = jnp.dot(s.astype(BF16), w_ref[0].astype(BF16), preferred_element_type=F32) + b_ref[0]


def _modulation(cc, w_mod, b_mod):
    n_layers, d, n = w_mod.shape
    tn = 1024
    return pl.pallas_call(
        _mod_kernel,
        out_shape=jax.ShapeDtypeStruct((n_layers, SUBLANES, n), F32),
        grid=(n_layers, n // tn),
        in_specs=[
            pl.BlockSpec((SUBLANES, d), lambda l, j: (0, 0)),
            pl.BlockSpec((1, d, tn), lambda l, j: (l, 0, j)),
            pl.BlockSpec((1, 1, tn), lambda l, j: (l, 0, j)),
        ],
        out_specs=pl.BlockSpec((1, SUBLANES, tn), lambda l, j: (l, 0, j)),
        compiler_params=_cparams(("parallel", "parallel")),
        name="adaln_mod",
    )(cc, w_mod, b_mod.reshape(n_layers, 1, n))


def _norm_mod_rows(x_ref, g_ref, sh_ref, sc_ref, xn_ref):
    tm = x_ref.shape[0]
    g = g_ref[...]
    sc = 1.0 + sc_ref[0]
    sh = sh_ref[0]

    def body(r, carry):
        rows = pl.ds(pl.multiple_of(r * ROW_CHUNK, ROW_CHUNK), ROW_CHUNK)
        x = x_ref[rows, :]
        ms = jnp.mean(x * x, axis=-1, keepdims=True)
        y = x * lax.rsqrt(ms + EPS) * g
        xn_ref[rows, :] = (y * sc + sh).astype(BF16)
        return carry

    lax.fori_loop(0, tm // ROW_CHUNK, body, 0)


def _mod_row_map(n_x_tiles, tiles_per_batch, ctx_row, k):
    def index_map(i, j):
        r = jnp.where(i < n_x_tiles, i // tiles_per_batch, ctx_row)
        return (r * N_MOD + k, 0, 0)
    return index_map


def _stream_specs(block, n_x_tiles, ctx_tile0, col_map):
    x_spec = pl.BlockSpec(block, lambda i, *r: (jnp.minimum(i, n_x_tiles - 1), col_map(*r)))
    c_spec = pl.BlockSpec(block, lambda i, *r: (ctx_tile0 + jnp.maximum(i - n_x_tiles, 0), col_map(*r)),
                          pipeline_mode=pl.Buffered(1))
    return x_spec, c_spec


def _in_kernel(x_ref, g_ref, sh_ref, sc_ref, w_ref, b_ref, wg_ref, bg_ref, z_ref, gz_ref, xn_ref):
    @pl.when(pl.program_id(1) == 0)
    def _():
        _norm_mod_rows(x_ref, g_ref, sh_ref, sc_ref, xn_ref)
        gz_ref[...] = jnp.dot(xn_ref[...], wg_ref[...], preferred_element_type=F32) + bg_ref[...]

    z_ref[...] = (jnp.dot(xn_ref[...], w_ref[...], preferred_element_type=F32) + b_ref[...]).astype(BF16)


def _in_proj(xs, g, mod3, w, b, wg, bg, tm, tn, n_x_tiles, tiles_per_batch, ctx_row):
    r, d = xs.shape
    nz = w.shape[1]
    return pl.pallas_call(
        _in_kernel,
        out_shape=(jax.ShapeDtypeStruct((r, nz), BF16), jax.ShapeDtypeStruct((r, LANES), F32)),
        grid=(r // tm, nz // tn),
        in_specs=[
            pl.BlockSpec((tm, d), lambda i, j: (i, 0)),
            pl.BlockSpec((1, d), lambda i, j: (0, 0)),
            pl.BlockSpec((1, 1, d), _mod_row_map(n_x_tiles, tiles_per_batch, ctx_row, 0)),
            pl.BlockSpec((1, 1, d), _mod_row_map(n_x_tiles, tiles_per_batch, ctx_row, 1)),
            pl.BlockSpec((d, tn), lambda i, j: (0, j)),
            pl.BlockSpec((1, tn), lambda i, j: (0, j)),
            pl.BlockSpec((d, LANES), lambda i, j: (0, 0)),
            pl.BlockSpec((1, LANES), lambda i, j: (0, 0)),
        ],
        out_specs=(
            pl.BlockSpec((tm, tn), lambda i, j: (i, j)),
            pl.BlockSpec((tm, LANES), lambda i, j: (i, 0)),
        ),
        scratch_shapes=[pltpu.VMEM((tm, d), BF16)],
        compiler_params=_cparams(("parallel", "arbitrary")),
        name="in_proj",
    )(xs, g, mod3, mod3, w, b, wg, bg)


def _log_sigmoid(x):
    return jnp.minimum(x, 0.0) - jnp.log1p(jnp.exp(-jnp.abs(x)))


def _split3(a):
    hi = a.astype(BF16)
    r1 = a - hi.astype(F32)
    mid = r1.astype(BF16)
    lo = (r1 - mid.astype(F32)).astype(BF16)
    return hi, mid, lo


def _ml_chunk(q, k_t, v, cum_cb, lg_r, mask_add, carry, reverse, inv_scale):
    c_mat, n_mat, m = carry
    length, dv = v.shape
    gi = 2 if reverse else 0
    ci = 5 if reverse else 4
    crow = lg_r[ci:ci + 1, :] - lg_r[gi:gi + 1, :]
    total = jnp.sum(lg_r[gi + 1:gi + 2, :], axis=-1, keepdims=True)
    ones = jnp.ones((length, LANES), BF16)
    n_blk = length // LANES

    def wide(a):
        return jnp.concatenate([a] * (dv // LANES), axis=1)

    log_w = [cum_cb - crow[:, j * LANES:(j + 1) * LANES] + mask_add[:, j * LANES:(j + 1) * LANES]
             for j in range(n_blk)]
    row_max = jnp.max(functools.reduce(jnp.maximum, log_w), axis=-1, keepdims=True)
    log_inter = cum_cb + m
    m_t = jnp.maximum(log_inter, row_max)
    w_inter = jnp.exp(log_inter - m_t)
    qk = jnp.dot(q, k_t, preferred_element_type=F32)
    s = jnp.concatenate([qk[:, j * LANES:(j + 1) * LANES] * jnp.exp(log_w[j] - m_t) for j in range(n_blk)],
                        axis=1).astype(BF16)
    num = wide(w_inter) * jnp.dot(q, c_mat.astype(BF16), preferred_element_type=F32)
    num = num + jnp.dot(s, v, preferred_element_type=F32)
    den = w_inter * jnp.dot(q, n_mat.astype(BF16), preferred_element_type=F32)
    den = den + jnp.dot(s, ones, preferred_element_type=F32)
    h = num * wide(1.0 / jnp.maximum(jnp.abs(den), jnp.exp(-m_t) * inv_scale))

    log_end = total - crow
    m_new = jnp.maximum(total + m, jnp.max(log_end, axis=-1, keepdims=True))
    decay = jnp.exp(total + m - m_new)
    kw_t = (k_t.astype(F32) * jnp.exp(log_end - m_new)).astype(BF16)
    c_new = decay * c_mat + jnp.dot(kw_t, v, preferred_element_type=F32)
    n_new = decay * n_mat + jnp.dot(kw_t, ones, preferred_element_type=F32)
    return h, (c_new, n_new, m_new)


def _mlstm_kernel(with_ctx_out, qx, kx, vx, ox, qc, kc, vc, oc, gx, gc, tri_ref, mask_ref, ng_ref, *rest):
    if with_ctx_out:
        ax_ref, ac_ref, cumx, lgx_r, ktx, cumc, lgc_r, ktc, hx, hc = rest
    else:
        ax_ref, cumx, lgx_r, ktx, cumc, lgc_r, ktc, hx, hc = rest
        ac_ref = None
    head = pl.program_id(1)
    dk = qx.shape[1]
    inv_scale = float(dk) ** 0.5
    n_x_chunks = qx.shape[0] // ML_CHUNK
    n_c_chunks = qc.shape[0] // ML_CHUNK

    r_idx = lax.broadcasted_iota(jnp.int32, (LANES, LANES), 0)
    c_idx = lax.broadcasted_iota(jnp.int32, (LANES, LANES), 1)
    sel = jnp.where((r_idx == c_idx * ML_HEADS + head) & (c_idx < 4), 1.0, 0.0).astype(BF16)
    tri = tri_ref[...]

    def prep(g_ref, k_ref, cum_ref, lg_r_ref, kt_ref):
        def body(i, carry):
            rows = pl.ds(pl.multiple_of(i * ML_CHUNK, ML_CHUNK), ML_CHUNK)
            gs = sum(jnp.dot(p, sel, preferred_element_type=F32) for p in _split3(g_ref[rows, :]))
            col = lax.broadcasted_iota(jnp.int32, gs.shape, 1)
            lg = jnp.where(col % 2 == 1, _log_sigmoid(gs), gs)
            pre = sum(jnp.dot(tri, p, preferred_element_type=F32) for p in _split3(lg))
            suf = pre[ML_CHUNK - 1:ML_CHUNK, :] - pre + lg
            cum_ref[0, rows, :] = jnp.broadcast_to(pre[:, 1:2], pre.shape)
            cum_ref[1, rows, :] = jnp.broadcast_to(suf[:, 3:4], suf.shape)
            lg = jnp.where(col == 4, pltpu.roll(pre, 3, axis=1), jnp.where(col == 5, pltpu.roll(suf, 2, axis=1), lg))
            lg_r_ref[i] = lg.T[:SUBLANES, :]
            kt_ref[i] = k_ref[rows, :].astype(F32).T.astype(BF16)
            return carry
        n_chunks = g_ref.shape[0] // ML_CHUNK
        lax.fori_loop(0, n_chunks, body, 0, unroll=2 if n_chunks % 2 == 0 else 1)

    prep(gx, kx, cumx, lgx_r, ktx)
    prep(gc, kc, cumc, lgc_r, ktc)

    for d in range(2):
        reverse = d == 1
        mask_add = mask_ref[d]

        def step(q_ref, v_ref, cum_ref, lg_r_ref, kt_ref, h_ref, ci, carry):
            rows = pl.ds(pl.multiple_of(ci * ML_CHUNK, ML_CHUNK), ML_CHUNK)
            h, carry = _ml_chunk(q_ref[rows, :], kt_ref[ci], v_ref[rows, :], cum_ref[d, rows, :], lg_r_ref[ci],
                                 mask_add, carry, reverse, inv_scale)
            if reverse:
                h_ref[rows, :] += h
            else:
                h_ref[rows, :] = h
            return carry

        carry = (jnp.zeros((dk, vx.shape[1]), F32), jnp.zeros((dk, LANES), F32), jnp.zeros((1, 1), F32))

        def ctx_body(i, carry):
            ci = (n_c_chunks - 1 - i) if reverse else i
            return step(qc, vc, cumc, lgc_r, ktc, hc, ci, carry)

        def x_body(i, carry):
            ci = (n_x_chunks - 1 - i) if reverse else i
            return step(qx, vx, cumx, lgx_r, ktx, hx, ci, carry)

        carry = lax.fori_loop(0, n_c_chunks, ctx_body, carry)
        lax.fori_loop(0, n_x_chunks, x_body, carry, unroll=2 if n_x_chunks % 2 == 0 else 1)

    ng = ng_ref[0]

    def finish(h_ref, o_ref, a_ref):
        def body(i, carry):
            rows = pl.ds(pl.multiple_of(i * ML_CHUNK, ML_CHUNK), ML_CHUNK)
            h = h_ref[rows, :]
            hn = h * lax.rsqrt(jnp.mean(h * h, axis=-1, keepdims=True) + EPS) * ng
            a_ref[rows, :] = (hn * jax.nn.sigmoid(o_ref[rows, :].astype(F32))).astype(BF16)
            return carry
        lax.fori_loop(0, h_ref.shape[0] // ML_CHUNK, body, 0)

    finish(hx, ox, ax_ref)
    if with_ctx_out:
        finish(hc, oc, ac_ref)


def _mlstm(z, gz, tri, mask, ml_norm_g, b, s, tc, dk, dv, cols, with_ctx_out):
    n_x = b * s
    cq, ck, cv, co = cols
    ctx0 = n_x // tc

    def xspec(width, col0):
        return pl.BlockSpec((s, width), lambda i, h: (i, col0 // width + h))

    def cspec(width, col0):
        return pl.BlockSpec((tc, width), lambda i, h: (ctx0 + i, col0 // width + h))

    out_shape = [jax.ShapeDtypeStruct((n_x, ML_HEADS * dv), BF16)]
    out_specs = [pl.BlockSpec((s, dv), lambda i, h: (i, h))]
    if with_ctx_out:
        out_shape.append(jax.ShapeDtypeStruct((b * tc, ML_HEADS * dv), BF16))
        out_specs.append(pl.BlockSpec((tc, dv), lambda i, h: (i, h)))
    res = pl.pallas_call(
        functools.partial(_mlstm_kernel, with_ctx_out),
        out_shape=tuple(out_shape),
        grid=(b, ML_HEADS),
        in_specs=[
            xspec(dk, cq), xspec(dk, ck), xspec(dv, cv), xspec(dv, co),
            cspec(dk, cq), cspec(dk, ck), cspec(dv, cv), cspec(dv, co),
            pl.BlockSpec((s, LANES), lambda i, h: (i, 0)),
            pl.BlockSpec((tc, LANES), lambda i, h: (ctx0 + i, 0)),
            pl.BlockSpec((ML_CHUNK, ML_CHUNK), lambda i, h: (0, 0)),
            pl.BlockSpec((2, ML_CHUNK, ML_CHUNK), lambda i, h: (0, 0, 0)),
            pl.BlockSpec((1, 1, dv), lambda i, h: (h, 0, 0)),
        ],
        out_specs=tuple(out_specs),
        scratch_shapes=[
            pltpu.VMEM((2, s, LANES), F32), pltpu.VMEM((s // ML_CHUNK, SUBLANES, ML_CHUNK), F32),
            pltpu.VMEM((s // ML_CHUNK, dk, ML_CHUNK), BF16),
            pltpu.VMEM((2, tc, LANES), F32), pltpu.VMEM((tc // ML_CHUNK, SUBLANES, ML_CHUNK), F32),
            pltpu.VMEM((tc // ML_CHUNK, dk, ML_CHUNK), BF16),
            pltpu.VMEM((s, dv), F32), pltpu.VMEM((tc, dv), F32),
        ],
        compiler_params=_cparams(("parallel", "parallel")),
        name="mlstm",
    )(z, z, z, z, z, z, z, z, gz, gz, tri, mask, ml_norm_g.reshape(ML_HEADS, 1, dv))
    return res if with_ctx_out else (res[0], None)


def _mla_proj_kernel(qa_ref, kva_ref, kpe_ref, tab_ref, qag_ref, kvag_ref, wq_ref, wkv_ref,
                     gq_ref, gk_ref, q_ref, k_ref, v_ref):
    def normed(a_ref, g_ref):
        a = a_ref[...].astype(F32)
        return (a * lax.rsqrt(jnp.mean(a * a, axis=-1, keepdims=True) + EPS) * g_ref[...]).astype(BF16)

    q_all = jnp.dot(normed(qa_ref, qag_ref), wq_ref[...], preferred_element_type=F32)
    kv_all = jnp.dot(normed(kva_ref, kvag_ref), wkv_ref[...], preferred_element_type=F32)
    tab = tab_ref[...]
    lane = lax.broadcasted_iota(jnp.int32, tab.shape, 1)
    first_half = lane < MLA_ROPE
    gq = gq_ref[...]
    gk = gk_ref[...]
    inv_dqk = 1.0 / MLA_DQK

    kpe = kpe_ref[...].astype(F32)
    ss_kpe = jnp.sum(jnp.where(first_half, kpe * kpe, 0.0), axis=-1, keepdims=True)
    kpe_t = kpe * (tab * gk[:, LANES:])
    kpe_rot = jnp.where(first_half, kpe_t + pltpu.roll(kpe_t, MLA_ROPE, axis=1), 0.0)

    for h in range(MLA_HEADS):
        qn = q_all[:, h * MLA_SLAB:h * MLA_SLAB + LANES]
        qp = q_all[:, h * MLA_SLAB + LANES:(h + 1) * MLA_SLAB]
        ss = jnp.sum(qn * qn, axis=-1, keepdims=True) + jnp.sum(jnp.where(first_half, qp * qp, 0.0), axis=-1,
                                                                keepdims=True)
        r = lax.rsqrt(ss * inv_dqk + EPS) * Q_PRESCALE
        qp_t = qp * (tab * gq[:, LANES:])
        qp_rot = qp_t + pltpu.roll(qp_t, MLA_ROPE, axis=1)
        q_ref[:, h * MLA_SLAB:h * MLA_SLAB + LANES] = (qn * r * gq[:, :LANES]).astype(BF16)
        q_ref[:, h * MLA_SLAB + LANES:(h + 1) * MLA_SLAB] = (qp_rot * r).astype(BF16)

        kn = kv_all[:, h * LANES:(h + 1) * LANES]
        rk = lax.rsqrt((jnp.sum(kn * kn, axis=-1, keepdims=True) + ss_kpe) * inv_dqk + EPS)
        k_ref[:, h * MLA_SLAB:h * MLA_SLAB + LANES] = (kn * rk * gk[:, :LANES]).astype(BF16)
        k_ref[:, h * MLA_SLAB + LANES:(h + 1) * MLA_SLAB] = (kpe_rot * rk).astype(BF16)

    ones_col = jnp.where(lane == 0, 1.0, 0.0).astype(BF16)
    for h in range(MLA_HEADS):
        v_ref[:, h * MLA_SLAB:h * MLA_SLAB + LANES] = kv_all[:, (MLA_HEADS + h) * LANES:(MLA_HEADS + h + 1) * LANES
                                                             ].astype(BF16)
        v_ref[:, h * MLA_SLAB + LANES:(h + 1) * MLA_SLAB] = ones_col


def _mla_proj(z, tab, qag, kvag, wq, wkv, gq, gk, tm, cols, n_x_tiles, tab_tiles):
    r = z.shape[0]
    cqa, ckva, ckpe = cols
    lora = qag.shape[1]
    hs = MLA_HEADS * MLA_SLAB
    hv = MLA_HEADS * LANES
    return pl.pallas_call(
        _mla_proj_kernel,
        out_shape=(jax.ShapeDtypeStruct((r, hs), BF16), jax.ShapeDtypeStruct((r, hs), BF16),
                   jax.ShapeDtypeStruct((r, hs), BF16)),
        grid=(r // tm,),
        in_specs=[
            pl.BlockSpec((tm, lora), lambda i: (i, cqa // lora)),
            pl.BlockSpec((tm, lora), lambda i: (i, ckva // lora)),
            pl.BlockSpec((tm, LANES), lambda i: (i, ckpe // LANES)),
            pl.BlockSpec((tm, LANES), lambda i: (jnp.where(i < n_x_tiles, i % tab_tiles, tab_tiles), 0)),
            pl.BlockSpec((1, lora), lambda i: (0, 0)),
            pl.BlockSpec((1, lora), lambda i: (0, 0)),
            pl.BlockSpec((lora, hs), lambda i: (0, 0)),
            pl.BlockSpec((lora, 2 * hv), lambda i: (0, 0)),
            pl.BlockSpec((1, MLA_SLAB), lambda i: (0, 0)),
            pl.BlockSpec((1, MLA_SLAB), lambda i: (0, 0)),
        ],
        out_specs=(pl.BlockSpec((tm, hs), lambda i: (i, 0)), pl.BlockSpec((tm, hs), lambda i: (i, 0)),
                   pl.BlockSpec((tm, hs), lambda i: (i, 0))),
        compiler_params=_cparams(("parallel",)),
        name="mla_qkv",
    )(z, z, z, tab, qag, kvag, wq, wkv, gq, gk)


def _attn_kernel(n_kv, tq, q_ref, *refs):
    k_refs = refs[:n_kv]
    v_refs = refs[n_kv:2 * n_kv]
    o_ref, s0_ref, s1_ref, m0_ref, m1_ref = refs[2 * n_kv:]
    slots = ((s0_ref, m0_ref), (s1_ref, m1_ref))
    n_tiles = q_ref.shape[0] // tq
    chunks = []
    col = 0
    for kv, k_ref in enumerate(k_refs):
        n_keys = k_ref.shape[0]
        step = min(ATTN_KEY_CHUNK, n_keys)
        for off in range(0, n_keys, step):
            chunks.append((kv, off, col, step))
            col += step

    def scores(t, slot):
        s_ref, m_ref = slots[slot]
        rows = pl.ds(pl.multiple_of(t * tq, tq), tq)
        q = q_ref[rows, :]
        run = None
        for kv, off, c0, size in chunks:
            s = lax.dot_general(q, k_refs[kv][off:off + size, :], (((1,), (1,)), ((), ())),
                                preferred_element_type=F32)
            s_ref[:, c0:c0 + size] = s
            for lb in range(size // LANES):
                blk = s[:, lb * LANES:(lb + 1) * LANES]
                run = blk if run is None else jnp.maximum(run, blk)
        m_ref[...] = run

    def finish(t, slot):
        s_ref, m_ref = slots[slot]
        rows = pl.ds(pl.multiple_of(t * tq, tq), tq)
        m = jnp.max(m_ref[...], axis=-1, keepdims=True)
        acc = None
        for kv, off, c0, size in chunks:
            p = jnp.exp2(s_ref[:, c0:c0 + size] - m).astype(BF16)
            pv = jnp.dot(p, v_refs[kv][off:off + size, :], preferred_element_type=F32)
            acc = pv if acc is None else acc + pv
        o_ref[rows, :] = (acc[:, :LANES] / acc[:, LANES:LANES + 1]).astype(BF16)

    scores(0, 0)

    def body(k, carry):
        scores(2 * k + 1, 1)
        finish(2 * k, 0)
        scores(jnp.minimum(2 * k + 2, n_tiles - 1), 0)
        finish(2 * k + 1, 1)
        return carry

    lax.fori_loop(0, n_tiles // 2, body, 0)
    if n_tiles % 2:
        finish(n_tiles - 1, 0)


def _attention(qo, ko, vo, b, s, tc, tq, latent):
    n_x = b * s
    ctx0 = n_x // tc
    cspec = pl.BlockSpec((tc, MLA_SLAB), lambda i, h: (ctx0 + i, h))
    xspec = pl.BlockSpec((s, MLA_SLAB), lambda i, h: (i, h))
    if latent:
        n_q, n_keys = s, s + tc
        in_specs = [xspec, cspec, xspec, cspec, xspec]
        args = (qo, ko, ko, vo, vo)
    else:
        n_q, n_keys = tc, tc
        in_specs = [cspec, cspec, cspec]
        args = (qo, ko, vo)
    return pl.pallas_call(
        functools.partial(_attn_kernel, (len(args) - 1) // 2, tq),
        out_shape=jax.ShapeDtypeStruct((b * n_q, MLA_HEADS * LANES), BF16),
        grid=(b, MLA_HEADS),
        in_specs=in_specs,
        out_specs=pl.BlockSpec((n_q, LANES), lambda i, h: (i, h)),
        scratch_shapes=[pltpu.VMEM((tq, n_keys), F32), pltpu.VMEM((tq, n_keys), F32),
                        pltpu.VMEM((tq, LANES), F32), pltpu.VMEM((tq, LANES), F32)],
        compiler_params=_cparams(("parallel", "parallel")),
        name="attn_latent" if latent else "attn_ctx",
    )(*args)


S5_HALF = S5_BLOCK_GROUPS * S5_STATE


def _s5_kernel(n_batch, rows_x, rows_c, dot_rows, u_ref, fac_ref, pw_ref, tap_ref, tab_ref, y_ref,
               u2_ref, v_ref, r_ref, m_ref, ot_ref):
    d = pl.program_id(1)
    n_dot = u2_ref.shape[0] // dot_rows

    for s in range(S5_SUB):
        blk = slice(s * LANES, (s + 1) * LANES)
        r_ref[blk, :] = (fac_ref[0, 0, 0] * pw_ref[0, 0, 0, s:s + 1, :]
                         + fac_ref[0, 0, 1] * pw_ref[0, 0, 1, s:s + 1, :]).astype(BF16)
        ot_ref[blk, :] = (fac_ref[0, 0, 2] * pw_ref[0, 0, 2, s:s + 1, :]
                          + fac_ref[0, 0, 3] * pw_ref[0, 0, 3, s:s + 1, :]).astype(BF16)
        for t in range(S5_SUB):
            m_ref[blk, t * LANES:(t + 1) * LANES] = tap_ref[0, 0, t - s + S5_SUB - 1].astype(BF16)

    def dot_rows_of(i):
        return pl.ds(pl.multiple_of(i * dot_rows, dot_rows), dot_rows)

    def token_rows_of(i, s):
        return pl.ds(i * (dot_rows * S5_SUB) + s, dot_rows, stride=S5_SUB)

    @pl.when(d == 0)
    def _():
        def stage(i, carry):
            rows = pl.ds(pl.multiple_of(i * dot_rows, dot_rows), dot_rows)
            y_ref[rows, :] = u_ref[rows, :].astype(F32)
            return carry

        lax.fori_loop(0, n_dot * S5_SUB, stage, 0)

        def regroup(i, carry):
            for s in range(S5_SUB):
                u2_ref[dot_rows_of(i), s * LANES:(s + 1) * LANES] = y_ref[token_rows_of(i, s), :].astype(BF16)
            return carry

        lax.fori_loop(0, n_dot, regroup, 0)

    def increments(i, carry):
        rows = dot_rows_of(i)
        v_ref[rows, :] = jnp.dot(u2_ref[rows, :], r_ref[...], preferred_element_type=F32)
        return carry

    lax.fori_loop(0, n_dot, increments, 0)

    def cmul_add(ar, ai, cr, ci, xr, xi):
        return ar + cr * xr - ci * xi, ai + cr * xi + ci * xr

    def run(reverse):
        tab = tab_ref.at[0, 0]
        last = 0 if reverse else SUBLANES - 1
        first_row = lax.broadcasted_iota(jnp.int32, (SUBLANES, S5_HALF), 0) == (SUBLANES - 1 - last)

        def segment(bases, n_groups, carry):
            def body(i, carry):
                gi = (n_groups - 1 - i) if reverse else i
                out = []
                for base, (cre, cim) in zip(bases, carry):
                    rows = pl.ds(pl.multiple_of(base + gi * SUBLANES, SUBLANES), SUBLANES)
                    re = v_ref[rows, :S5_HALF]
                    im = v_ref[rows, S5_HALF:]
                    for lvl, shift in enumerate((1, 2, 4)):
                        sh = (SUBLANES - shift) if reverse else shift
                        re, im = cmul_add(re, im, tab[2 * lvl], tab[2 * lvl + 1],
                                          pltpu.roll(re, sh, axis=0), pltpu.roll(im, sh, axis=0))
                    re, im = cmul_add(re, im, tab[6], tab[7], cre, cim)
                    sh1 = (SUBLANES - 1) if reverse else 1
                    v_ref[rows, :S5_HALF] = jnp.where(first_row, cre, pltpu.roll(re, sh1, axis=0))
                    v_ref[rows, S5_HALF:] = jnp.where(first_row, cim, pltpu.roll(im, sh1, axis=0))
                    out.append((jnp.broadcast_to(re[last:last + 1, :], re.shape),
                                jnp.broadcast_to(im[last:last + 1, :], im.shape)))
                return tuple(out)
            return lax.fori_loop(0, n_groups, body, carry)

        zero = jnp.zeros((SUBLANES, S5_HALF), F32)
        carry = tuple((zero, zero) for _ in range(n_batch))
        carry = segment([n_batch * rows_x + bi * rows_c for bi in range(n_batch)], rows_c // SUBLANES, carry)
        segment([bi * rows_x for bi in range(n_batch)], rows_x // SUBLANES, carry)

        def outputs(i, carry):
            rows = dot_rows_of(i)
            y = jnp.dot(u2_ref[rows, :], m_ref[...], preferred_element_type=F32)
            y = y + lax.dot_general(v_ref[rows, :].astype(BF16), ot_ref[...], (((1,), (1,)), ((), ())),
                                    preferred_element_type=F32)
            for s in range(S5_SUB):
                part = y[:, s * LANES:(s + 1) * LANES]
                if reverse:
                    y_ref[token_rows_of(i, s), :] += part
                else:
                    y_ref[token_rows_of(i, s), :] = part
            return carry

        lax.fori_loop(0, n_dot, outputs, 0)

    @pl.when(d == 0)
    def _():
        run(False)

    @pl.when(d == 1)
    def _():
        run(True)


def _s5_scan(z, fac, pw, taps, tabs, b, s, tc, col_u):
    r = z.shape[0]
    rc = r // S5_SUB
    n_blocks = fac.shape[1]
    width = S5_SUB * LANES
    assert width == 2 * S5_HALF
    dot_rows = max(n for n in range(16, 641, 16) if rc % n == 0)
    return pl.pallas_call(
        functools.partial(_s5_kernel, b, s // S5_SUB, tc // S5_SUB, dot_rows),
        out_shape=jax.ShapeDtypeStruct((r, n_blocks * LANES), F32),
        grid=(n_blocks, 2),
        in_specs=[
            pl.BlockSpec((r, LANES), lambda cb, d: (0, col_u // LANES + cb), pipeline_mode=pl.Buffered(1)),
            pl.BlockSpec((1, 1, 4, LANES, width), lambda cb, d: (d, cb, 0, 0, 0), pipeline_mode=pl.Buffered(1)),
            pl.BlockSpec((1, 1, 4, S5_SUB, width), lambda cb, d: (d, cb, 0, 0, 0)),
            pl.BlockSpec((1, 1, 2 * S5_SUB - 1, LANES, LANES), lambda cb, d: (d, cb, 0, 0, 0)),
            pl.BlockSpec((1, 1, 8, SUBLANES, S5_HALF), lambda cb, d: (d, cb, 0, 0, 0)),
        ],
        out_specs=pl.BlockSpec((r, LANES), lambda cb, d: (0, cb)),
        scratch_shapes=[pltpu.VMEM((rc, width), BF16), pltpu.VMEM((rc, width), F32),
                        pltpu.VMEM((width, width), BF16), pltpu.VMEM((width, width), BF16),
                        pltpu.VMEM((width, width), BF16)],
        compiler_params=_cparams(("parallel", "arbitrary")),
        name="s5_scan",
    )(z, fac, pw, taps, tabs)


def _glu_kernel(y_ref, u_ref, d_ref, w_ref, b_ref, o_ref, g_ref):
    tm = y_ref.shape[0]

    def body(r, carry):
        rows = pl.ds(pl.multiple_of(r * ROW_CHUNK, ROW_CHUNK), ROW_CHUNK)
        y = y_ref[rows, :] + d_ref[...] * u_ref[rows, :].astype(F32)
        g_ref[rows, :] = jax.nn.gelu(y).astype(BF16)
        return carry

    lax.fori_loop(0, tm // ROW_CHUNK, body, 0)
    g = g_ref[...]
    gate = jax.nn.sigmoid(jnp.dot(g, w_ref[...], preferred_element_type=F32) + b_ref[...])
    o_ref[...] = (g.astype(F32) * gate).astype(BF16)


def _s5_glu(y, z, d_skip, w_glu, b_glu, tm, col_u, n_row_tiles):
    width = y.shape[1]
    return pl.pallas_call(
        _glu_kernel,
        out_shape=jax.ShapeDtypeStruct((n_row_tiles * tm, width), BF16),
        grid=(n_row_tiles,),
        in_specs=[
            pl.BlockSpec((tm, width), lambda i: (i, 0)),
            pl.BlockSpec((tm, width), lambda i: (i, col_u // width)),
            pl.BlockSpec((1, width), lambda i: (0, 0)),
            pl.BlockSpec((width, width), lambda i: (0, 0)),
            pl.BlockSpec((1, width), lambda i: (0, 0)),
        ],
        out_specs=pl.BlockSpec((tm, width), lambda i: (i, 0)),
        scratch_shapes=[pltpu.VMEM((tm, width), BF16)],
        compiler_params=_cparams(("parallel",)),
        name="s5_glu",
    )(y, z, d_skip, w_glu, b_glu)


def _merge_kernel(n_x_tiles, ax_ref, ac_ref, bx_ref, bc_ref, c_ref, ga_ref, gb_ref, gc_ref, w_ref, o_ref):
    def combine(a_ref, b_ref):
        acc = None
        for r, (br_ref, gate_ref) in enumerate(((a_ref, ga_ref), (b_ref, gb_ref), (c_ref, gc_ref))):
            proj = jnp.dot(br_ref[...], w_ref[r], preferred_element_type=F32)
            term = jax.nn.sigmoid(gate_ref[...].astype(F32)) * proj
            acc = term if acc is None else acc + term
        o_ref[...] = acc.astype(BF16)

    @pl.when(pl.program_id(0) < n_x_tiles)
    def _():
        combine(ax_ref, bx_ref)

    @pl.when(pl.program_id(0) >= n_x_tiles)
    def _():
        combine(ac_ref, bc_ref)


def _merge(a_x, a_c, b_x, b_c, cc, z, w_branch, tm, tn, col_g, n_row_tiles, n_x_tiles):
    width = a_x.shape[1]
    d = w_branch.shape[2]

    def gate_spec(r):
        return pl.BlockSpec((tm, tn), lambda i, j: (i, (col_g + r * d) // tn + j))

    ax_spec, ac_spec = _stream_specs((tm, width), n_x_tiles, 0, lambda j: 0)
    return pl.pallas_call(
        functools.partial(_merge_kernel, n_x_tiles),
        out_shape=jax.ShapeDtypeStruct((n_row_tiles * tm, d), BF16),
        grid=(n_row_tiles, d // tn),
        in_specs=[
            ax_spec, ac_spec, ax_spec, ac_spec,
            pl.BlockSpec((tm, width), lambda i, j: (i, 0)),
            gate_spec(0), gate_spec(1), gate_spec(2),
            pl.BlockSpec((N_BRANCH, width, tn), lambda i, j: (0, 0, j)),
        ],
        out_specs=pl.BlockSpec((tm, tn), lambda i, j: (i, j)),
        compiler_params=_cparams(("parallel", "arbitrary")),
        name="merge",
    )(a_x, a_c, b_x, b_c, cc, z, z, z, w_branch)


def _resid_kernel(m_ref, w_ref, x_ref, al_ref, o_ref):
    o_ref[...] = x_ref[...] + al_ref[0] * jnp.dot(m_ref[...], w_ref[...], preferred_element_type=F32)


def _out_proj_residual(m, w, xs, mod3, tm, tn, n_row_tiles, n_x_tiles, tiles_per_batch, ctx_row, k_alpha):
    kdim, d = w.shape
    nt = d // tn

    def alpha_map(i, j):
        r = jnp.where(i < n_x_tiles, i // tiles_per_batch, ctx_row)
        return (r * N_MOD + k_alpha, 0, j)

    return pl.pallas_call(
        _resid_kernel,
        out_shape=jax.ShapeDtypeStruct((n_row_tiles * tm, d), F32),
        grid=(n_row_tiles, nt),
        in_specs=[
            pl.BlockSpec((tm, kdim), lambda i, j: (i, 0)),
            pl.BlockSpec((kdim, tn), lambda i, j: (0, j)),
            pl.BlockSpec((tm, tn), lambda i, j: (i, j)),
            pl.BlockSpec((1, 1, tn), alpha_map),
        ],
        out_specs=pl.BlockSpec((tm, tn), lambda i, j: (i, j)),
        compiler_params=_cparams(("parallel", "arbitrary")),
        name="out_proj",
    )(m, w, xs, mod3)


def _ff1_kernel(x_ref, g_ref, sh_ref, sc_ref, w_ref, h_ref, xn_ref):
    @pl.when(pl.program_id(1) == 0)
    def _():
        _norm_mod_rows(x_ref, g_ref, sh_ref, sc_ref, xn_ref)

    a = jnp.maximum(jnp.dot(xn_ref[...], w_ref[...], preferred_element_type=F32), 0.0)
    h_ref[...] = (a * a).astype(BF16)


def _ff1(xs, g, mod3, w, tm, tn, n_row_tiles, n_x_tiles, tiles_per_batch, ctx_row):
    d, dff = w.shape
    return pl.pallas_call(
        _ff1_kernel,
        out_shape=jax.ShapeDtypeStruct((n_row_tiles * tm, dff), BF16),
        grid=(n_row_tiles, dff // tn),
        in_specs=[
            pl.BlockSpec((tm, d), lambda i, j: (i, 0)),
            pl.BlockSpec((1, d), lambda i, j: (0, 0)),
            pl.BlockSpec((1, 1, d), _mod_row_map(n_x_tiles, tiles_per_batch, ctx_row, 3)),
            pl.BlockSpec((1, 1, d), _mod_row_map(n_x_tiles, tiles_per_batch, ctx_row, 4)),
            pl.BlockSpec((d, tn), lambda i, j: (0, j)),
        ],
        out_specs=pl.BlockSpec((tm, tn), lambda i, j: (i, j)),
        scratch_shapes=[pltpu.VMEM((tm, d), BF16)],
        compiler_params=_cparams(("parallel", "arbitrary")),
        name="ff1",
    )(xs, g, mod3, mod3, w)


def _ff2_kernel(h_ref, w_ref, x_ref, al_ref, o_ref):
    k = pl.program_id(2)
    part = jnp.dot(h_ref[...], w_ref[...], preferred_element_type=F32)

    @pl.when(k == 0)
    def _():
        o_ref[...] = part

    @pl.when(k > 0)
    def _():
        o_ref[...] += part

    @pl.when(k == pl.num_programs(2) - 1)
    def _():
        o_ref[...] = x_ref[...] + al_ref[0] * o_ref[...]


def _ff2(h, w, xs, mod3, tm, tn, tk, n_row_tiles, n_x_tiles, tiles_per_batch, ctx_row):
    dff, d = w.shape

    def alpha_map(i, j, k):
        r = jnp.where(i < n_x_tiles, i // tiles_per_batch, ctx_row)
        return (r * N_MOD + 5, 0, j)

    return pl.pallas_call(
        _ff2_kernel,
        out_shape=jax.ShapeDtypeStruct((n_row_tiles * tm, d), F32),
        grid=(n_row_tiles, d // tn, dff // tk),
        in_specs=[
            pl.BlockSpec((tm, tk), lambda i, j, k: (i, k)),
            pl.BlockSpec((tk, tn), lambda i, j, k: (k, j)),
            pl.BlockSpec((tm, tn), lambda i, j, k: (i, j)),
            pl.BlockSpec((1, 1, tn), alpha_map),
        ],
        out_specs=pl.BlockSpec((tm, tn), lambda i, j, k: (i, j)),
        compiler_params=_cparams(("parallel", "parallel", "arbitrary")),
        name="ff2",
    )(h, w, xs, mod3)


def _rope_partner():
    j = np.arange(MLA_ROPE)
    quarter = MLA_ROPE // 4
    return np.where((j // quarter) % 2 == 0, j + quarter, j - quarter)


def _rope_table(s, tm):
    pos = jnp.arange(s)
    row = (pos // GRID_W).astype(F32)
    col = (pos % GRID_W).astype(F32)
    n_freq = MLA_ROPE // 4
    inv_freq = ROPE_THETA ** (-jnp.arange(n_freq, dtype=F32) / n_freq)
    ang_r = row[:, None] * inv_freq
    ang_c = col[:, None] * inv_freq
    cos = jnp.concatenate([jnp.cos(ang_r)] * 2 + [jnp.cos(ang_c)] * 2, axis=-1)
    sin = jnp.concatenate([-jnp.sin(ang_r), jnp.sin(ang_r), -jnp.sin(ang_c), jnp.sin(ang_c)], axis=-1)
    ident = jnp.concatenate([jnp.ones((tm, MLA_ROPE), F32), jnp.zeros((tm, MLA_ROPE), F32)], axis=-1)
    return jnp.concatenate([jnp.concatenate([cos, sin], axis=-1), ident], axis=0)


def _pack_w_in(w_in, b_in, gate_b, sizes, n_pad):
    bounds = np.cumsum((0,) + sizes)
    seg = [slice(int(bounds[i]), int(bounds[i + 1])) for i in range(len(sizes))]
    partner = _rope_partner()
    order = (0, 1, 2, 3, 5, 6, 8, 9)
    w_bf = w_in.astype(BF16)
    w_parts = [w_bf[:, seg[i]] for i in order]
    b_parts = [b_in[seg[i]] for i in order]
    kpe_w = w_bf[:, seg[7]]
    kpe_b = b_in[seg[7]]
    w_parts += [kpe_w, kpe_w[:, partner]]
    b_parts += [kpe_b, kpe_b[partner]]
    offs = {}
    pos = 0
    for name, part in zip(("q", "k", "v", "o", "qa", "kva", "u", "gates", "kpe", "kpe_sw"), w_parts):
        offs[name] = pos
        pos += part.shape[1]
    pad = n_pad - pos
    w_parts.append(jnp.zeros((w_in.shape[0], pad), BF16))
    b_parts.append(jnp.zeros((pad,), b_in.dtype))
    w = jnp.concatenate(w_parts, axis=1)
    b = jnp.concatenate(b_parts)[None, :]
    n_g = sizes[4]
    wg = jnp.concatenate([w_bf[:, seg[4]], jnp.zeros((w_in.shape[0], LANES - n_g), BF16)], axis=1)
    bg = jnp.concatenate([b_in[seg[4]] + gate_b.reshape(-1), jnp.zeros((LANES - n_g,), F32)])[None, :]
    return w, b, wg, bg, offs


def _pack_mla(w_uq, w_ukv, qn_g, kn_g):
    partner = _rope_partner()
    lora = w_uq.shape[0]
    wq = w_uq.reshape(lora, MLA_HEADS, MLA_DQK)
    wq = jnp.concatenate([wq, wq[:, :, MLA_NOPE + partner]], axis=-1).reshape(lora, MLA_HEADS * MLA_SLAB)
    wkv = w_ukv.reshape(w_ukv.shape[0], MLA_HEADS, -1)
    wkv = jnp.concatenate([wkv[:, :, :MLA_NOPE].reshape(lora, -1), wkv[:, :, MLA_NOPE:].reshape(lora, -1)], axis=-1)

    def gains(g):
        return jnp.concatenate([g, g[MLA_NOPE + partner]])[None, :]

    return wq.astype(BF16), wkv.astype(BF16), gains(qn_g), gains(kn_g)


def _pack_s5(a_re, a_im, log_dt, b_re, b_im, c_re, c_im):
    n_dir, n_groups, n_state = a_re.shape
    gc = b_re.shape[-1]
    nb = n_groups // S5_BLOCK_GROUPS
    lam_re = jnp.minimum(a_re.astype(F32), -1e-4)
    lam_im = a_im.astype(F32)
    dt = jnp.exp(log_dt.astype(F32))[..., None]

    def pole_power(k):
        mag = jnp.exp(k * lam_re * dt)
        return mag * jnp.cos(k * lam_im * dt), mag * jnp.sin(k * lam_im * dt)

    bar_re, bar_im = pole_power(1.0)
    den = lam_re * lam_re + lam_im * lam_im
    f_re = ((bar_re - 1.0) * lam_re + bar_im * lam_im) / den
    f_im = (bar_im * lam_re - (bar_re - 1.0) * lam_im) / den
    bb_re = f_re[..., None] * b_re.astype(F32) - f_im[..., None] * b_im.astype(F32)
    bb_im = f_re[..., None] * b_im.astype(F32) + f_im[..., None] * b_re.astype(F32)
    eye = jnp.eye(S5_BLOCK_GROUPS, dtype=F32)
    sub = S5_SUB

    def per_block(a):
        return a.reshape(a.shape[:-2] + (nb, S5_BLOCK_GROUPS * n_state))

    def block_b(part):
        p = part.reshape(n_dir, nb, S5_BLOCK_GROUPS, n_state, gc)
        m = jnp.einsum('dbgnc,gh->dbgchn', p, eye, precision=HIGHEST)
        return m.reshape(n_dir, nb, LANES, S5_HALF)

    def block_c(part):
        p = part.astype(F32).reshape(n_dir, nb, S5_BLOCK_GROUPS, gc, n_state)
        m = jnp.einsum('dbgcn,gh->dbgnhc', p, eye, precision=HIGHEST)
        return m.reshape(n_dir, nb, S5_HALF, LANES)

    bm_re, bm_im = block_b(bb_re), block_b(bb_im)
    cm_re, cm_im = block_c(c_re), block_c(c_im)

    tau = jnp.arange(sub + 1, dtype=F32)[:, None, None, None]
    p_re, p_im = (per_block(p) for p in pole_power(tau))
    pr, pi = p_re[:sub, :, :, None, :], p_im[:sub, :, :, None, :]
    taps = (jnp.einsum('zdbkn,dbnc->zdbkc', bm_re * pr - bm_im * pi, cm_re, precision=HIGHEST)
            - jnp.einsum('zdbkn,dbnc->zdbkc', bm_re * pi + bm_im * pr, cm_im, precision=HIGHEST))

    ct_re, ct_im = jnp.swapaxes(cm_re, -1, -2), jnp.swapaxes(cm_im, -1, -2)
    cat = functools.partial(jnp.concatenate, axis=-1)
    fac = jnp.stack([cat([bm_re, bm_im]), cat([-bm_im, bm_re]), cat([ct_re, -ct_im]), cat([-ct_im, -ct_re])],
                    axis=2)
    t = np.arange(sub)
    zero_taps = jnp.zeros((sub - 1,) + taps.shape[2:], F32)
    pws, taps2 = [], []
    for d in range(n_dir):
        to_exit = (sub - 1 - t) if d == 0 else t
        age = (t + 1) if d == 0 else (sub - t)
        rows = [p_re[to_exit, d], p_im[to_exit, d], p_re[age, d], p_im[age, d]]
        pws.append(jnp.stack([cat([a, a]).transpose(1, 0, 2) for a in rows], axis=1))
        taps2.append(jnp.concatenate([zero_taps, taps[:, 0]] if d == 0 else [taps[::-1, 1], zero_taps], axis=0))
    pw = jnp.stack(pws)
    taps2 = jnp.stack(taps2).transpose(0, 2, 1, 3, 4)

    rows = jnp.arange(SUBLANES)
    tabs = []
    for d in range(n_dir):
        per_dir = []
        for shift in (1, 2, 4):
            keep = ((rows <= SUBLANES - 1 - shift) if d == 1 else (rows >= shift))[None, :, None]
            s_re, s_im = pole_power(float(shift * sub))
            per_dir += [jnp.where(keep, per_block(s_re[d])[:, None, :], 0.0),
                        jnp.where(keep, per_block(s_im[d])[:, None, :], 0.0)]
        expo = (((SUBLANES - rows) if d == 1 else (rows + 1)) * sub).astype(F32)
        s_re, s_im = pole_power(expo[:, None, None, None])
        per_dir += [jnp.moveaxis(per_block(s_re[:, d]), 0, 1), jnp.moveaxis(per_block(s_im[:, d]), 0, 1)]
        tabs.append(jnp.stack(per_dir, axis=1))
    return fac, pw, taps2, jnp.stack(tabs).astype(F32)


def _tri_matrices():
    t = np.arange(ML_CHUNK)
    lower = t[None, :] <= t[:, None]
    mask = np.where(np.stack([lower, lower.T]), 0.0, NEG_BIG).astype(np.float32)
    return jnp.asarray(lower.astype(np.float32), dtype=BF16), jnp.asarray(mask)


def kernel(x, c, ctx, c_ctx, w_mod, b_mod, norm_g, w_in, b_in, ml_gate_b, ml_norm_g, mla_qa_g, mla_kva_g, mla_w_uq, mla_w_ukv, mla_qn_g, mla_kn_g, s5_a_re, s5_a_im, s5_log_dt, s5_b_re, s5_b_im, s5_c_re, s5_c_im, s5_d, s5_w_glu, s5_b_glu, w_branch, w_out, w_ff1, w_ff2):
    b, s, d = x.shape
    tc = ctx.shape[1]
    depth = w_mod.shape[0]
    dv = ml_norm_g.shape[2]
    dk = dv // 2
    lora = mla_qa_g.shape[1]
    s5_width = s5_d.shape[1]
    branch_w = w_branch.shape[2]
    sizes = (ML_HEADS * dk, ML_HEADS * dk, ML_HEADS * dv, ML_HEADS * dv, 4 * ML_HEADS, lora, lora, MLA_ROPE,
             s5_width, N_BRANCH * d)
    assert sum(sizes) == w_in.shape[2] and b + 1 <= SUBLANES
    assert s % ML_CHUNK == 0 and tc % ML_CHUNK == 0 and branch_w == ML_HEADS * dv == MLA_HEADS * LANES == s5_width

    n_x = b * s
    n_c = b * tc
    tm = _row_tile(s, n_c)
    n_x_tiles = n_x // tm
    n_tiles = n_x_tiles + n_c // tm
    tiles_per_batch = s // tm
    tile_args = (n_x_tiles, tiles_per_batch, b)

    xs = jnp.concatenate([x.reshape(n_x, d), ctx.reshape(n_c, d)], axis=0)
    cc = jnp.concatenate([c, c_ctx[None, :], jnp.zeros((SUBLANES - b - 1, d), F32)], axis=0)
    mod = _modulation(cc, w_mod, b_mod)
    tm_q = min(tm, 512)
    tab = _rope_table(s, tm_q)
    tri, ml_mask = _tri_matrices()
    n_used = sum(sizes) - sizes[4] + MLA_ROPE
    tn_in = 1280
    n_pad = -(-n_used // tn_in) * tn_in

    for l in range(depth):
        with_ctx_out = l < depth - 1
        mod3 = mod[l].reshape(SUBLANES * N_MOD, 1, d)
        w_p, b_p, wg, bg, offs = _pack_w_in(w_in[l], b_in[l], ml_gate_b[l], sizes, n_pad)
        z, gz = _in_proj(xs, norm_g[l, 0][None, :], mod3, w_p, b_p, wg, bg, tm, tn_in, *tile_args)

        a_x, a_c = _mlstm(z, gz, tri, ml_mask, ml_norm_g[l], b, s, tc, dk, dv,
                          (offs["q"], offs["k"], offs["v"], offs["o"]), with_ctx_out)

        wq, wkv, gq, gk = _pack_mla(mla_w_uq[l], mla_w_ukv[l], mla_qn_g[l], mla_kn_g[l])
        qo, ko, vo = _mla_proj(z, tab, mla_qa_g[l][None, :], mla_kva_g[l][None, :], wq, wkv, gq, gk, tm_q,
                               (offs["qa"], offs["kva"], offs["kpe"]), n_x // tm_q, s // tm_q)
        tq = min(256, s)
        b_x = _attention(qo, ko, vo, b, s, tc, tq, True)

        fac, pw, taps, tabs = _pack_s5(s5_a_re[l], s5_a_im[l], s5_log_dt[l], s5_b_re[l], s5_b_im[l],
                                       s5_c_re[l], s5_c_im[l])
        y = _s5_scan(z, fac, pw, taps, tabs, b, s, tc, offs["u"])
        n_out_tiles = n_tiles if with_ctx_out else n_x_tiles
        c_all = _s5_glu(y, z, s5_d[l][None, :], s5_w_glu[l].astype(BF16), s5_b_glu[l][None, :], tm, offs["u"],
                        n_out_tiles)

        if with_ctx_out:
            b_c = _attention(qo, ko, vo, b, s, tc, min(tq, tc), False)
        else:
            a_c, b_c = a_x, b_x
        merged = _merge(a_x, a_c, b_x, b_c, c_all, z, w_branch[l].astype(BF16), tm, 512, offs["gates"],
                        n_out_tiles, n_x_tiles)
        xs1 = _out_proj_residual(merged, w_out[l].astype(BF16), xs, mod3, tm, 1024, n_out_tiles, *tile_args, 2)
        hid = _ff1(xs1, norm_g[l, 1][None, :], mod3, w_ff1[l].astype(BF16), tm, 1024, n_out_tiles, *tile_args)
        xs = _ff2(hid, w_ff2[l].astype(BF16), xs1, mod3, tm, 1024, 2048, n_out_tiles, *tile_args)

    return xs.reshape(b, s, d)
```

```python
import functools
import math

import jax
import jax.numpy as jnp
import numpy as np
from jax import lax
from jax.experimental import pallas as pl
from jax.experimental.pallas import tpu as pltpu

F32 = jnp.float32
BF16 = jnp.bfloat16
HIGHEST = lax.Precision.HIGHEST

N_MOD = 6
N_BRANCH = 3
ML_HEADS = 4
MLA_HEADS = 8
MLA_NOPE = 128
MLA_ROPE = 64
MLA_DQK = MLA_NOPE + MLA_ROPE
MLA_SLAB = 256
GRID_W = 64
ROPE_THETA = 10000.0
S5_GROUP = 16
S5_STATE = 64
S5_BLOCK_GROUPS = 8
EPS = 1e-6
NEG_BIG = -1e30

LANES = 128
SUBLANES = 8
VMEM_LIMIT = 56 * 1024 * 1024

ML_CHUNK = 256
S5_SUB = 8
ATTN_KEY_CHUNK = 512
Q_PRESCALE = MLA_DQK ** -0.5 * math.log2(math.e)
ROW_CHUNK = 64


def _cparams(sem):
    return pltpu.CompilerParams(dimension_semantics=sem, vmem_limit_bytes=VMEM_LIMIT)


def _row_tile(n_x_rows_per_batch, n_ctx_rows):
    tm = 1024
    while n_x_rows_per_batch % tm or n_ctx_rows % tm:
        tm //= 2
    return tm


def _mod_kernel(c_ref, w_ref, b_ref, o_ref):
    s = c_ref[...]
    s = s * jax.nn.sigmoid(s)
    o_ref[0] = jnp.dot(s.astype(BF16), w_ref[0].astype(BF16), preferred_element_type=F32) + b_ref[0]


def _modulation(cc, w_mod, b_mod):
    n_layers, d, n = w_mod.shape
    tn = 1024
    return pl.pallas_call(
        _mod_kernel,
        out_shape=jax.ShapeDtypeStruct((n_layers, SUBLANES, n), F32),
        grid=(n_layers, n // tn),
        in_specs=[
            pl.BlockSpec((SUBLANES, d), lambda l, j: (0, 0)),
            pl.BlockSpec((1, d, tn), lambda l, j: (l, 0, j)),
            pl.BlockSpec((1, 1, tn), lambda l, j: (l, 0, j)),
        ],
        out_specs=pl.BlockSpec((1, SUBLANES, tn), lambda l, j: (l, 0, j)),
        compiler_params=_cparams(("parallel", "parallel")),
        name="adaln_mod",
    )(cc, w_mod, b_mod.reshape(n_layers, 1, n))


def _norm_mod_rows(x_ref, g_ref, sh_ref, sc_ref, xn_ref):
    tm = x_ref.shape[0]
    g = g_ref[...]
    sc = 1.0 + sc_ref[0]
    sh = sh_ref[0]

    def body(r, carry):
        rows = pl.ds(pl.multiple_of(r * ROW_CHUNK, ROW_CHUNK), ROW_CHUNK)
        x = x_ref[rows, :]
        ms = jnp.mean(x * x, axis=-1, keepdims=True)
        y = x * lax.rsqrt(ms + EPS) * g
        xn_ref[rows, :] = (y * sc + sh).astype(BF16)
        return carry

    lax.fori_loop(0, tm // ROW_CHUNK, body, 0)


def _mod_row_map(n_x_tiles, tiles_per_batch, ctx_row, k):
    def index_map(i, j):
        r = jnp.where(i < n_x_tiles, i // tiles_per_batch, ctx_row)
        return (r * N_MOD + k, 0, 0)
    return index_map


def _stream_specs(block, n_x_tiles, ctx_tile0, col_map):
    x_spec = pl.BlockSpec(block, lambda i, *r: (jnp.minimum(i, n_x_tiles - 1), col_map(*r)))
    c_spec = pl.BlockSpec(block, lambda i, *r: (ctx_tile0 + jnp.maximum(i - n_x_tiles, 0), col_map(*r)),
                          pipeline_mode=pl.Buffered(1))
    return x_spec, c_spec


def _in_kernel(x_ref, g_ref, sh_ref, sc_ref, w_ref, b_ref, wg_ref, bg_ref, z_ref, gz_ref, xn_ref):
    @pl.when(pl.program_id(1) == 0)
    def _():
        _norm_mod_rows(x_ref, g_ref, sh_ref, sc_ref, xn_ref)
        gz_ref[...] = jnp.dot(xn_ref[...], wg_ref[...], preferred_element_type=F32) + bg_ref[...]

    z_ref[...] = (jnp.dot(xn_ref[...], w_ref[...], preferred_element_type=F32) + b_ref[...]).astype(BF16)


def _in_proj(xs, g, mod3, w, b, wg, bg, tm, tn, n_x_tiles, tiles_per_batch, ctx_row):
    r, d = xs.shape
    nz = w.shape[1]
    return pl.pallas_call(
        _in_kernel,
        out_shape=(jax.ShapeDtypeStruct((r, nz), BF16), jax.ShapeDtypeStruct((r, LANES), F32)),
        grid=(r // tm, nz // tn),
        in_specs=[
            pl.BlockSpec((tm, d), lambda i, j: (i, 0)),
            pl.BlockSpec((1, d), lambda i, j: (0, 0)),
            pl.BlockSpec((1, 1, d), _mod_row_map(n_x_tiles, tiles_per_batch, ctx_row, 0)),
            pl.BlockSpec((1, 1, d), _mod_row_map(n_x_tiles, tiles_per_batch, ctx_row, 1)),
            pl.BlockSpec((d, tn), lambda i, j: (0, j)),
            pl.BlockSpec((1, tn), lambda i, j: (0, j)),
            pl.BlockSpec((d, LANES), lambda i, j: (0, 0)),
            pl.BlockSpec((1, LANES), lambda i, j: (0, 0)),
        ],
        out_specs=(
            pl.BlockSpec((tm, tn), lambda i, j: (i, j)),
            pl.BlockSpec((tm, LANES), lambda i, j: (i, 0)),
        ),
        scratch_shapes=[pltpu.VMEM((tm, d), BF16)],
        compiler_params=_cparams(("parallel", "arbitrary")),
        name="in_proj",
    )(xs, g, mod3, mod3, w, b, wg, bg)


def _log_sigmoid(x):
    return jnp.minimum(x, 0.0) - jnp.log1p(jnp.exp(-jnp.abs(x)))


def _split3(a):
    hi = a.astype(BF16)
    r1 = a - hi.astype(F32)
    mid = r1.astype(BF16)
    lo = (r1 - mid.astype(F32)).astype(BF16)
    return hi, mid, lo


def _ml_chunk(q, k_t, v, cum_cb, lg_r, mask_add, carry, reverse, inv_scale):
    c_mat, n_mat, m = carry
    length, dv = v.shape
    gi = 2 if reverse else 0
    ci = 5 if reverse else 4
    crow = lg_r[ci:ci + 1, :] - lg_r[gi:gi + 1, :]
    total = jnp.sum(lg_r[gi + 1:gi + 2, :], axis=-1, keepdims=True)
    ones = jnp.ones((length, LANES), BF16)
    n_blk = length // LANES

    def wide(a):
        return jnp.concatenate([a] * (dv // LANES), axis=1)

    log_w = [cum_cb - crow[:, j * LANES:(j + 1) * LANES] + mask_add[:, j * LANES:(j + 1) * LANES]
             for j in range(n_blk)]
    row_max = jnp.max(functools.reduce(jnp.maximum, log_w), axis=-1, keepdims=True)
    log_inter = cum_cb + m
    m_t = jnp.maximum(log_inter, row_max)
    w_inter = jnp.exp(log_inter - m_t)
    qk = jnp.dot(q, k_t, preferred_element_type=F32)
    s = jnp.concatenate([qk[:, j * LANES:(j + 1) * LANES] * jnp.exp(log_w[j] - m_t) for j in range(n_blk)],
                        axis=1).astype(BF16)
    num = wide(w_inter) * jnp.dot(q, c_mat.astype(BF16), preferred_element_type=F32)
    num = num + jnp.dot(s, v, preferred_element_type=F32)
    den = w_inter * jnp.dot(q, n_mat.astype(BF16), preferred_element_type=F32)
    den = den + jnp.dot(s, ones, preferred_element_type=F32)
    h = num * wide(1.0 / jnp.maximum(jnp.abs(den), jnp.exp(-m_t) * inv_scale))

    log_end = total - crow
    m_new = jnp.maximum(total + m, jnp.max(log_end, axis=-1, keepdims=True))
    decay = jnp.exp(total + m - m_new)
    kw_t = (k_t.astype(F32) * jnp.exp(log_end - m_new)).astype(BF16)
    c_new = decay * c_mat + jnp.dot(kw_t, v, preferred_element_type=F32)
    n_new = decay * n_mat + jnp.dot(kw_t, ones, preferred_element_type=F32)
    return h, (c_new, n_new, m_new)


def _mlstm_kernel(with_ctx_out, qx, kx, vx, ox, qc, kc, vc, oc, gx, gc, tri_ref, mask_ref, ng_ref, *rest):
    if with_ctx_out:
        ax_ref, ac_ref, cumx, lgx_r, ktx, cumc, lgc_r, ktc, hx, hc = rest
    else:
        ax_ref, cumx, lgx_r, ktx, cumc, lgc_r, ktc, hx, hc = rest
        ac_ref = None
    head = pl.program_id(1)
    dk = qx.shape[1]
    inv_scale = float(dk) ** 0.5
    n_x_chunks = qx.shape[0] // ML_CHUNK
    n_c_chunks = qc.shape[0] // ML_CHUNK

    r_idx = lax.broadcasted_iota(jnp.int32, (LANES, LANES), 0)
    c_idx = lax.broadcasted_iota(jnp.int32, (LANES, LANES), 1)
    sel = jnp.where((r_idx == c_idx * ML_HEADS + head) & (c_idx < 4), 1.0, 0.0).astype(BF16)
    tri = tri_ref[...]

    def prep(g_ref, k_ref, cum_ref, lg_r_ref, kt_ref):
        def body(i, carry):
            rows = pl.ds(pl.multiple_of(i * ML_CHUNK, ML_CHUNK), ML_CHUNK)
            gs = sum(jnp.dot(p, sel, preferred_element_type=F32) for p in _split3(g_ref[rows, :]))
            col = lax.broadcasted_iota(jnp.int32, gs.shape, 1)
            lg = jnp.where(col % 2 == 1, _log_sigmoid(gs), gs)
            pre = sum(jnp.dot(tri, p, preferred_element_type=F32) for p in _split3(lg))
            suf = pre[ML_CHUNK - 1:ML_CHUNK, :] - pre + lg
            cum_ref[0, rows, :] = jnp.broadcast_to(pre[:, 1:2], pre.shape)
            cum_ref[1, rows, :] = jnp.broadcast_to(suf[:, 3:4], suf.shape)
            lg = jnp.where(col == 4, pltpu.roll(pre, 3, axis=1), jnp.where(col == 5, pltpu.roll(suf, 2, axis=1), lg))
            lg_r_ref[i] = lg.T[:SUBLANES, :]
            kt_ref[i] = k_ref[rows, :].astype(F32).T.astype(BF16)
            return carry
        n_chunks = g_ref.shape[0] // ML_CHUNK
        lax.fori_loop(0, n_chunks, body, 0, unroll=2 if n_chunks % 2 == 0 else 1)

    prep(gx, kx, cumx, lgx_r, ktx)
    prep(gc, kc, cumc, lgc_r, ktc)

    for d in range(2):
        reverse = d == 1
        mask_add = mask_ref[d]

        def step(q_ref, v_ref, cum_ref, lg_r_ref, kt_ref, h_ref, ci, carry):
            rows = pl.ds(pl.multiple_of(ci * ML_CHUNK, ML_CHUNK), ML_CHUNK)
            h, carry = _ml_chunk(q_ref[rows, :], kt_ref[ci], v_ref[rows, :], cum_ref[d, rows, :], lg_r_ref[ci],
                                 mask_add, carry, reverse, inv_scale)
            if reverse:
                h_ref[rows, :] += h
            else:
                h_ref[rows, :] = h
            return carry

        carry = (jnp.zeros((dk, vx.shape[1]), F32), jnp.zeros((dk, LANES), F32), jnp.zeros((1, 1), F32))

        def ctx_body(i, carry):
            ci = (n_c_chunks - 1 - i) if reverse else i
            return step(qc, vc, cumc, lgc_r, ktc, hc, ci, carry)

        def x_body(i, carry):
            ci = (n_x_chunks - 1 - i) if reverse else i
            return step(qx, vx, cumx, lgx_r, ktx, hx, ci, carry)

        carry = lax.fori_loop(0, n_c_chunks, ctx_body, carry)
        lax.fori_loop(0, n_x_chunks, x_body, carry, unroll=2 if n_x_chunks % 2 == 0 else 1)

    ng = ng_ref[0]

    def finish(h_ref, o_ref, a_ref):
        def body(i, carry):
            rows = pl.ds(pl.multiple_of(i * ML_CHUNK, ML_CHUNK), ML_CHUNK)
            h = h_ref[rows, :]
            hn = h * lax.rsqrt(jnp.mean(h * h, axis=-1, keepdims=True) + EPS) * ng
            a_ref[rows, :] = (hn * jax.nn.sigmoid(o_ref[rows, :].astype(F32))).astype(BF16)
            return carry
        lax.fori_loop(0, h_ref.shape[0] // ML_CHUNK, body, 0)

    finish(hx, ox, ax_ref)
    if with_ctx_out:
        finish(hc, oc, ac_ref)


def _mlstm(z, gz, tri, mask, ml_norm_g, b, s, tc, dk, dv, cols, with_ctx_out):
    n_x = b * s
    cq, ck, cv, co = cols
    ctx0 = n_x // tc

    def xspec(width, col0):
        return pl.BlockSpec((s, width), lambda i, h: (i, col0 // width + h))

    def cspec(width, col0):
        return pl.BlockSpec((tc, width), lambda i, h: (ctx0 + i, col0 // width + h))

    out_shape = [jax.ShapeDtypeStruct((n_x, ML_HEADS * dv), BF16)]
    out_specs = [pl.BlockSpec((s, dv), lambda i, h: (i, h))]
    if with_ctx_out:
        out_shape.append(jax.ShapeDtypeStruct((b * tc, ML_HEADS * dv), BF16))
        out_specs.append(pl.BlockSpec((tc, dv), lambda i, h: (i, h)))
    res = pl.pallas_call(
        functools.partial(_mlstm_kernel, with_ctx_out),
        out_shape=tuple(out_shape),
        grid=(b, ML_HEADS),
        in_specs=[
            xspec(dk, cq), xspec(dk, ck), xspec(dv, cv), xspec(dv, co),
            cspec(dk, cq), cspec(dk, ck), cspec(dv, cv), cspec(dv, co),
            pl.BlockSpec((s, LANES), lambda i, h: (i, 0)),
            pl.BlockSpec((tc, LANES), lambda i, h: (ctx0 + i, 0)),
            pl.BlockSpec((ML_CHUNK, ML_CHUNK), lambda i, h: (0, 0)),
            pl.BlockSpec((2, ML_CHUNK, ML_CHUNK), lambda i, h: (0, 0, 0)),
            pl.BlockSpec((1, 1, dv), lambda i, h: (h, 0, 0)),
        ],
        out_specs=tuple(out_specs),
        scratch_shapes=[
            pltpu.VMEM((2, s, LANES), F32), pltpu.VMEM((s // ML_CHUNK, SUBLANES, ML_CHUNK), F32),
            pltpu.VMEM((s // ML_CHUNK, dk, ML_CHUNK), BF16),
            pltpu.VMEM((2, tc, LANES), F32), pltpu.VMEM((tc // ML_CHUNK, SUBLANES, ML_CHUNK), F32),
            pltpu.VMEM((tc // ML_CHUNK, dk, ML_CHUNK), BF16),
            pltpu.VMEM((s, dv), F32), pltpu.VMEM((tc, dv), F32),
        ],
        compiler_params=_cparams(("parallel", "parallel")),
        name="mlstm",
    )(z, z, z, z, z, z, z, z, gz, gz, tri, mask, ml_norm_g.reshape(ML_HEADS, 1, dv))
    return res if with_ctx_out else (res[0], None)


def _mla_proj_kernel(qa_ref, kva_ref, kpe_ref, tab_ref, qag_ref, kvag_ref, wq_ref, wkv_ref,
                     gq_ref, gk_ref, q_ref, k_ref, v_ref):
    def normed(a_ref, g_ref):
        a = a_ref[...].astype(F32)
        return (a * lax.rsqrt(jnp.mean(a * a, axis=-1, keepdims=True) + EPS) * g_ref[...]).astype(BF16)

    q_all = jnp.dot(normed(qa_ref, qag_ref), wq_ref[...], preferred_element_type=F32)
    kv_all = jnp.dot(normed(kva_ref, kvag_ref), wkv_ref[...], preferred_element_type=F32)
    tab = tab_ref[...]
    lane = lax.broadcasted_iota(jnp.int32, tab.shape, 1)
    first_half = lane < MLA_ROPE
    gq = gq_ref[...]
    gk = gk_ref[...]
    inv_dqk = 1.0 / MLA_DQK

    kpe = kpe_ref[...].astype(F32)
    ss_kpe = jnp.sum(jnp.where(first_half, kpe * kpe, 0.0), axis=-1, keepdims=True)
    kpe_t = kpe * (tab * gk[:, LANES:])
    kpe_rot = jnp.where(first_half, kpe_t + pltpu.roll(kpe_t, MLA_ROPE, axis=1), 0.0)

    for h in range(MLA_HEADS):
        qn = q_all[:, h * MLA_SLAB:h * MLA_SLAB + LANES]
        qp = q_all[:, h * MLA_SLAB + LANES:(h + 1) * MLA_SLAB]
        ss = jnp.sum(qn * qn, axis=-1, keepdims=True) + jnp.sum(jnp.where(first_half, qp * qp, 0.0), axis=-1,
                                                                keepdims=True)
        r = lax.rsqrt(ss * inv_dqk + EPS) * Q_PRESCALE
        qp_t = qp * (tab * gq[:, LANES:])
        qp_rot = qp_t + pltpu.roll(qp_t, MLA_ROPE, axis=1)
        q_ref[:, h * MLA_SLAB:h * MLA_SLAB + LANES] = (qn * r * gq[:, :LANES]).astype(BF16)
        q_ref[:, h * MLA_SLAB + LANES:(h + 1) * MLA_SLAB] = (qp_rot * r).astype(BF16)

        kn = kv_all[:, h * LANES:(h + 1) * LANES]
        rk = lax.rsqrt((jnp.sum(kn * kn, axis=-1, keepdims=True) + ss_kpe) * inv_dqk + EPS)
        k_ref[:, h * MLA_SLAB:h * MLA_SLAB + LANES] = (kn * rk * gk[:, :LANES]).astype(BF16)
        k_ref[:, h * MLA_SLAB + LANES:(h + 1) * MLA_SLAB] = (kpe_rot * rk).astype(BF16)

    ones_col = jnp.where(lane == 0, 1.0, 0.0).astype(BF16)
    for h in range(MLA_HEADS):
        v_ref[:, h * MLA_SLAB:h * MLA_SLAB + LANES] = kv_all[:, (MLA_HEADS + h) * LANES:(MLA_HEADS + h + 1) * LANES
                                                             ].astype(BF16)
        v_ref[:, h * MLA_SLAB + LANES:(h + 1) * MLA_SLAB] = ones_col


def _mla_proj(z, tab, qag, kvag, wq, wkv, gq, gk, tm, cols, n_x_tiles, tab_tiles):
    r = z.shape[0]
    cqa, ckva, ckpe = cols
    lora = qag.shape[1]
    hs = MLA_HEADS * MLA_SLAB
    hv = MLA_HEADS * LANES
    return pl.pallas_call(
        _mla_proj_kernel,
        out_shape=(jax.ShapeDtypeStruct((r, hs), BF16), jax.ShapeDtypeStruct((r, hs), BF16),
                   jax.ShapeDtypeStruct((r, hs), BF16)),
        grid=(r // tm,),
        in_specs=[
            pl.BlockSpec((tm, lora), lambda i: (i, cqa // lora)),
            pl.BlockSpec((tm, lora), lambda i: (i, ckva // lora)),
            pl.BlockSpec((tm, LANES), lambda i: (i, ckpe // LANES)),
            pl.BlockSpec((tm, LANES), lambda i: (jnp.where(i < n_x_tiles, i % tab_tiles, tab_tiles), 0)),
            pl.BlockSpec((1, lora), lambda i: (0, 0)),
            pl.BlockSpec((1, lora), lambda i: (0, 0)),
            pl.BlockSpec((lora, hs), lambda i: (0, 0)),
            pl.BlockSpec((lora, 2 * hv), lambda i: (0, 0)),
            pl.BlockSpec((1, MLA_SLAB), lambda i: (0, 0)),
            pl.BlockSpec((1, MLA_SLAB), lambda i: (0, 0)),
        ],
        out_specs=(pl.BlockSpec((tm, hs), lambda i: (i, 0)), pl.BlockSpec((tm, hs), lambda i: (i, 0)),
                   pl.BlockSpec((tm, hs), lambda i: (i, 0))),
        compiler_params=_cparams(("parallel",)),
        name="mla_qkv",
    )(z, z, z, tab, qag, kvag, wq, wkv, gq, gk)


def _attn_kernel(n_kv, tq, q_ref, *refs):
    k_refs = refs[:n_kv]
    v_refs = refs[n_kv:2 * n_kv]
    o_ref, s0_ref, s1_ref, m0_ref, m1_ref = refs[2 * n_kv:]
    slots = ((s0_ref, m0_ref), (s1_ref, m1_ref))
    n_tiles = q_ref.shape[0] // tq
    chunks = []
    col = 0
    for kv, k_ref in enumerate(k_refs):
        n_keys = k_ref.shape[0]
        step = min(ATTN_KEY_CHUNK, n_keys)
        for off in range(0, n_keys, step):
            chunks.append((kv, off, col, step))
            col += step

    def scores(t, slot):
        s_ref, m_ref = slots[slot]
        rows = pl.ds(pl.multiple_of(t * tq, tq), tq)
        q = q_ref[rows, :]
        run = None
        for kv, off, c0, size in chunks:
            s = lax.dot_general(q, k_refs[kv][off:off + size, :], (((1,), (1,)), ((), ())),
                                preferred_element_type=F32)
            s_ref[:, c0:c0 + size] = s
            for lb in range(size // LANES):
                blk = s[:, lb * LANES:(lb + 1) * LANES]
                run = blk if run is None else jnp.maximum(run, blk)
        m_ref[...] = run

    def finish(t, slot):
        s_ref, m_ref = slots[slot]
        rows = pl.ds(pl.multiple_of(t * tq, tq), tq)
        m = jnp.max(m_ref[...], axis=-1, keepdims=True)
        acc = None
        for kv, off, c0, size in chunks:
            p = jnp.exp2(s_ref[:, c0:c0 + size] - m).astype(BF16)
            pv = jnp.dot(p, v_refs[kv][off:off + size, :], preferred_element_type=F32)
            acc = pv if acc is None else acc + pv
        o_ref[rows, :] = (acc[:, :LANES] / acc[:, LANES:LANES + 1]).astype(BF16)

    scores(0, 0)

    def body(k, carry):
        scores(2 * k + 1, 1)
        finish(2 * k, 0)
        scores(jnp.minimum(2 * k + 2, n_tiles - 1), 0)
        finish(2 * k + 1, 1)
        return carry

    lax.fori_loop(0, n_tiles // 2, body, 0)
    if n_tiles % 2:
        finish(n_tiles - 1, 0)


def _attention(qo, ko, vo, b, s, tc, tq, latent):
    n_x = b * s
    ctx0 = n_x // tc
    cspec = pl.BlockSpec((tc, MLA_SLAB), lambda i, h: (ctx0 + i, h))
    xspec = pl.BlockSpec((s, MLA_SLAB), lambda i, h: (i, h))
    if latent:
        n_q, n_keys = s, s + tc
        in_specs = [xspec, cspec, xspec, cspec, xspec]
        args = (qo, ko, ko, vo, vo)
    else:
        n_q, n_keys = tc, tc
        in_specs = [cspec, cspec, cspec]
        args = (qo, ko, vo)
    return pl.pallas_call(
        functools.partial(_attn_kernel, (len(args) - 1) // 2, tq),
        out_shape=jax.ShapeDtypeStruct((b * n_q, MLA_HEADS * LANES), BF16),
        grid=(b, MLA_HEADS),
        in_specs=in_specs,
        out_specs=pl.BlockSpec((n_q, LANES), lambda i, h: (i, h)),
        scratch_shapes=[pltpu.VMEM((tq, n_keys), F32), pltpu.VMEM((tq, n_keys), F32),
                        pltpu.VMEM((tq, LANES), F32), pltpu.VMEM((tq, LANES), F32)],
        compiler_params=_cparams(("parallel", "parallel")),
        name="attn_latent" if latent else "attn_ctx",
    )(*args)


S5_HALF = S5_BLOCK_GROUPS * S5_STATE


def _s5_kernel(n_batch, rows_x, rows_c, dot_rows, u_ref, fac_ref, pw_ref, tap_ref, tab_ref, y_ref,
               u2_ref, v_ref, r_ref, m_ref, ot_ref):
    d = pl.program_id(1)
    n_dot = u2_ref.shape[0] // dot_rows

    for s in range(S5_SUB):
        blk = slice(s * LANES, (s + 1) * LANES)
        r_ref[blk, :] = (fac_ref[0, 0, 0] * pw_ref[0, 0, 0, s:s + 1, :]
                         + fac_ref[0, 0, 1] * pw_ref[0, 0, 1, s:s + 1, :]).astype(BF16)
        ot_ref[blk, :] = (fac_ref[0, 0, 2] * pw_ref[0, 0, 2, s:s + 1, :]
                          + fac_ref[0, 0, 3] * pw_ref[0, 0, 3, s:s + 1, :]).astype(BF16)
        for t in range(S5_SUB):
            m_ref[blk, t * LANES:(t + 1) * LANES] = tap_ref[0, 0, t - s + S5_SUB - 1].astype(BF16)

    def dot_rows_of(i):
        return pl.ds(pl.multiple_of(i * dot_rows, dot_rows), dot_rows)

    def token_rows_of(i, s):
        return pl.ds(i * (dot_rows * S5_SUB) + s, dot_rows, stride=S5_SUB)

    @pl.when(d == 0)
    def _():
        def stage(i, carry):
            rows = pl.ds(pl.multiple_of(i * dot_rows, dot_rows), dot_rows)
            y_ref[rows, :] = u_ref[rows, :].astype(F32)
            return carry

        lax.fori_loop(0, n_dot * S5_SUB, stage, 0)

        def regroup(i, carry):
            for s in range(S5_SUB):
                u2_ref[dot_rows_of(i), s * LANES:(s + 1) * LANES] = y_ref[token_rows_of(i, s), :].astype(BF16)
            return carry

        lax.fori_loop(0, n_dot, regroup, 0)

    def increments(i, carry):
        rows = dot_rows_of(i)
        v_ref[rows, :] = jnp.dot(u2_ref[rows, :], r_ref[...], preferred_element_type=F32)
        return carry

    lax.fori_loop(0, n_dot, increments, 0)

    def cmul_add(ar, ai, cr, ci, xr, xi):
        return ar + cr * xr - ci * xi, ai + cr * xi + ci * xr

    def run(reverse):
        tab = tab_ref.at[0, 0]
        last = 0 if reverse else SUBLANES - 1
        first_row = lax.broadcasted_iota(jnp.int32, (SUBLANES, S5_HALF), 0) == (SUBLANES - 1 - last)

        def segment(bases, n_groups, carry):
            def body(i, carry):
                gi = (n_groups - 1 - i) if reverse else i
                out = []
                for base, (cre, cim) in zip(bases, carry):
                    rows = pl.ds(pl.multiple_of(base + gi * SUBLANES, SUBLANES), SUBLANES)
                    re = v_ref[rows, :S5_HALF]
                    im = v_ref[rows, S5_HALF:]
                    for lvl, shift in enumerate((1, 2, 4)):
                        sh = (SUBLANES - shift) if reverse else shift
                        re, im = cmul_add(re, im, tab[2 * lvl], tab[2 * lvl + 1],
                                          pltpu.roll(re, sh, axis=0), pltpu.roll(im, sh, axis=0))
                    re, im = cmul_add(re, im, tab[6], tab[7], cre, cim)
                    sh1 = (SUBLANES - 1) if reverse else 1
                    v_ref[rows, :S5_HALF] = jnp.where(first_row, cre, pltpu.roll(re, sh1, axis=0))
                    v_ref[rows, S5_HALF:] = jnp.where(first_row, cim, pltpu.roll(im, sh1, axis=0))
                    out.append((jnp.broadcast_to(re[last:last + 1, :], re.shape),
                                jnp.broadcast_to(im[last:last + 1, :], im.shape)))
                return tuple(out)
            return lax.fori_loop(0, n_groups, body, carry)

        zero = jnp.zeros((SUBLANES, S5_HALF), F32)
        carry = tuple((zero, zero) for _ in range(n_batch))
        carry = segment([n_batch * rows_x + bi * rows_c for bi in range(n_batch)], rows_c // SUBLANES, carry)
        segment([bi * rows_x for bi in range(n_batch)], rows_x // SUBLANES, carry)

        def outputs(i, carry):
            rows = dot_rows_of(i)
            y = jnp.dot(u2_ref[rows, :], m_ref[...], preferred_element_type=F32)
            y = y + lax.dot_general(v_ref[rows, :].astype(BF16), ot_ref[...], (((1,), (1,)), ((), ())),
                                    preferred_element_type=F32)
            for s in range(S5_SUB):
                part = y[:, s * LANES:(s + 1) * LANES]
                if reverse:
                    y_ref[token_rows_of(i, s), :] += part
                else:
                    y_ref[token_rows_of(i, s), :] = part
            return carry

        lax.fori_loop(0, n_dot, outputs, 0)

    @pl.when(d == 0)
    def _():
        run(False)

    @pl.when(d == 1)
    def _():
        run(True)


def _s5_scan(z, fac, pw, taps, tabs, b, s, tc, col_u):
    r = z.shape[0]
    rc = r // S5_SUB
    n_blocks = fac.shape[1]
    width = S5_SUB * LANES
    assert width == 2 * S5_HALF
    dot_rows = max(n for n in range(16, 641, 16) if rc % n == 0)
    return pl.pallas_call(
        functools.partial(_s5_kernel, b, s // S5_SUB, tc // S5_SUB, dot_rows),
        out_shape=jax.ShapeDtypeStruct((r, n_blocks * LANES), F32),
        grid=(n_blocks, 2),
        in_specs=[
            pl.BlockSpec((r, LANES), lambda cb, d: (0, col_u // LANES + cb), pipeline_mode=pl.Buffered(1)),
            pl.BlockSpec((1, 1, 4, LANES, width), lambda cb, d: (d, cb, 0, 0, 0), pipeline_mode=pl.Buffered(1)),
            pl.BlockSpec((1, 1, 4, S5_SUB, width), lambda cb, d: (d, cb, 0, 0, 0)),
            pl.BlockSpec((1, 1, 2 * S5_SUB - 1, LANES, LANES), lambda cb, d: (d, cb, 0, 0, 0)),
            pl.BlockSpec((1, 1, 8, SUBLANES, S5_HALF), lambda cb, d: (d, cb, 0, 0, 0)),
        ],
        out_specs=pl.BlockSpec((r, LANES), lambda cb, d: (0, cb)),
        scratch_shapes=[pltpu.VMEM((rc, width), BF16), pltpu.VMEM((rc, width), F32),
                        pltpu.VMEM((width, width), BF16), pltpu.VMEM((width, width), BF16),
                        pltpu.VMEM((width, width), BF16)],
        compiler_params=_cparams(("parallel", "arbitrary")),
        name="s5_scan",
    )(z, fac, pw, taps, tabs)


def _glu_kernel(y_ref, u_ref, d_ref, w_ref, b_ref, o_ref, g_ref):
    tm = y_ref.shape[0]

    def body(r, carry):
        rows = pl.ds(pl.multiple_of(r * ROW_CHUNK, ROW_CHUNK), ROW_CHUNK)
        y = y_ref[rows, :] + d_ref[...] * u_ref[rows, :].astype(F32)
        g_ref[rows, :] = jax.nn.gelu(y).astype(BF16)
        return carry

    lax.fori_loop(0, tm // ROW_CHUNK, body, 0)
    g = g_ref[...]
    gate = jax.nn.sigmoid(jnp.dot(g, w_ref[...], preferred_element_type=F32) + b_ref[...])
    o_ref[...] = (g.astype(F32) * gate).astype(BF16)


def _s5_glu(y, z, d_skip, w_glu, b_glu, tm, col_u, n_row_tiles):
    width = y.shape[1]
    return pl.pallas_call(
        _glu_kernel,
        out_shape=jax.ShapeDtypeStruct((n_row_tiles * tm, width), BF16),
        grid=(n_row_tiles,),
        in_specs=[
            pl.BlockSpec((tm, width), lambda i: (i, 0)),
            pl.BlockSpec((tm, width), lambda i: (i, col_u // width)),
            pl.BlockSpec((1, width), lambda i: (0, 0)),
            pl.BlockSpec((width, width), lambda i: (0, 0)),
            pl.BlockSpec((1, width), lambda i: (0, 0)),
        ],
        out_specs=pl.BlockSpec((tm, width), lambda i: (i, 0)),
        scratch_shapes=[pltpu.VMEM((tm, width), BF16)],
        compiler_params=_cparams(("parallel",)),
        name="s5_glu",
    )(y, z, d_skip, w_glu, b_glu)


def _merge_kernel(n_x_tiles, ax_ref, ac_ref, bx_ref, bc_ref, c_ref, ga_ref, gb_ref, gc_ref, w_ref, o_ref):
    def combine(a_ref, b_ref):
        acc = None
        for r, (br_ref, gate_ref) in enumerate(((a_ref, ga_ref), (b_ref, gb_ref), (c_ref, gc_ref))):
            proj = jnp.dot(br_ref[...], w_ref[0, r].astype(BF16), preferred_element_type=F32)
            term = jax.nn.sigmoid(gate_ref[...].astype(F32)) * proj
            acc = term if acc is None else acc + term
        o_ref[...] = acc.astype(BF16)

    @pl.when(pl.program_id(0) < n_x_tiles)
    def _():
        combine(ax_ref, bx_ref)

    @pl.when(pl.program_id(0) >= n_x_tiles)
    def _():
        combine(ac_ref, bc_ref)


def _merge(a_x, a_c, b_x, b_c, cc, z, w_branch, layer, tm, tn, col_g, n_row_tiles, n_x_tiles):
    width = a_x.shape[1]
    d = w_branch.shape[3]

    def gate_spec(r):
        return pl.BlockSpec((tm, tn), lambda i, j: (i, (col_g + r * d) // tn + j))

    ax_spec, ac_spec = _stream_specs((tm, width), n_x_tiles, 0, lambda j: 0)
    return pl.pallas_call(
        functools.partial(_merge_kernel, n_x_tiles),
        out_shape=jax.ShapeDtypeStruct((n_row_tiles * tm, d), BF16),
        grid=(n_row_tiles, d // tn),
        in_specs=[
            ax_spec, ac_spec, ax_spec, ac_spec,
            pl.BlockSpec((tm, width), lambda i, j: (i, 0)),
            gate_spec(0), gate_spec(1), gate_spec(2),
            pl.BlockSpec((1, N_BRANCH, width, tn), lambda i, j: (layer, 0, 0, j)),
        ],
        out_specs=pl.BlockSpec((tm, tn), lambda i, j: (i, j)),
        compiler_params=_cparams(("parallel", "arbitrary")),
        name="merge",
    )(a_x, a_c, b_x, b_c, cc, z, z, z, w_branch)


def _resid_kernel(m_ref, w_ref, x_ref, al_ref, o_ref):
    o_ref[...] = x_ref[...] + al_ref[0] * jnp.dot(m_ref[...], w_ref[0].astype(BF16), preferred_element_type=F32)


def _out_proj_residual(m, w, layer, xs, mod3, tm, tn, n_row_tiles, n_x_tiles, tiles_per_batch, ctx_row, k_alpha):
    _, kdim, d = w.shape
    nt = d // tn

    def alpha_map(i, j):
        r = jnp.where(i < n_x_tiles, i // tiles_per_batch, ctx_row)
        return (r * N_MOD + k_alpha, 0, j)

    return pl.pallas_call(
        _resid_kernel,
        out_shape=jax.ShapeDtypeStruct((n_row_tiles * tm, d), F32),
        grid=(n_row_tiles, nt),
        in_specs=[
            pl.BlockSpec((tm, kdim), lambda i, j: (i, 0)),
            pl.BlockSpec((1, kdim, tn), lambda i, j: (layer, 0, j)),
            pl.BlockSpec((tm, tn), lambda i, j: (i, j)),
            pl.BlockSpec((1, 1, tn), alpha_map),
        ],
        out_specs=pl.BlockSpec((tm, tn), lambda i, j: (i, j)),
        compiler_params=_cparams(("parallel", "arbitrary")),
        name="out_proj",
    )(m, w, xs, mod3)


def _ff1_kernel(x_ref, g_ref, sh_ref, sc_ref, w_ref, h_ref, xn_ref):
    @pl.when(pl.program_id(1) == 0)
    def _():
        _norm_mod_rows(x_ref, g_ref, sh_ref, sc_ref, xn_ref)

    a = jnp.maximum(jnp.dot(xn_ref[...], w_ref[0].astype(BF16), preferred_element_type=F32), 0.0)
    h_ref[...] = (a * a).astype(BF16)


def _ff1(xs, g, mod3, w, layer, tm, tn, n_row_tiles, n_x_tiles, tiles_per_batch, ctx_row):
    _, d, dff = w.shape
    return pl.pallas_call(
        _ff1_kernel,
        out_shape=jax.ShapeDtypeStruct((n_row_tiles * tm, dff), BF16),
        grid=(n_row_tiles, dff // tn),
        in_specs=[
            pl.BlockSpec((tm, d), lambda i, j: (i, 0)),
            pl.BlockSpec((1, d), lambda i, j: (0, 0)),
            pl.BlockSpec((1, 1, d), _mod_row_map(n_x_tiles, tiles_per_batch, ctx_row, 3)),
            pl.BlockSpec((1, 1, d), _mod_row_map(n_x_tiles, tiles_per_batch, ctx_row, 4)),
            pl.BlockSpec((1, d, tn), lambda i, j: (layer, 0, j)),
        ],
        out_specs=pl.BlockSpec((tm, tn), lambda i, j: (i, j)),
        scratch_shapes=[pltpu.VMEM((tm, d), BF16)],
        compiler_params=_cparams(("parallel", "arbitrary")),
        name="ff1",
    )(xs, g, mod3, mod3, w)


def _ff2_kernel(h_ref, w_ref, x_ref, al_ref, o_ref):
    k = pl.program_id(2)
    part = jnp.dot(h_ref[...], w_ref[0].astype(BF16), preferred_element_type=F32)

    @pl.when(k == 0)
    def _():
        o_ref[...] = part

    @pl.when(k > 0)
    def _():
        o_ref[...] += part

    @pl.when(k == pl.num_programs(2) - 1)
    def _():
        o_ref[...] = x_ref[...] + al_ref[0] * o_ref[...]


def _ff2(h, w, layer, xs, mod3, tm, tn, tk, n_row_tiles, n_x_tiles, tiles_per_batch, ctx_row):
    _, dff, d = w.shape

    def alpha_map(i, j, k):
        r = jnp.where(i < n_x_tiles, i // tiles_per_batch, ctx_row)
        return (r * N_MOD + 5, 0, j)

    return pl.pallas_call(
        _ff2_kernel,
        out_shape=jax.ShapeDtypeStruct((n_row_tiles * tm, d), F32),
        grid=(n_row_tiles, d // tn, dff // tk),
        in_specs=[
            pl.BlockSpec((tm, tk), lambda i, j, k: (i, k)),
            pl.BlockSpec((1, tk, tn), lambda i, j, k: (layer, k, j)),
            pl.BlockSpec((tm, tn), lambda i, j, k: (i, j)),
            pl.BlockSpec((1, 1, tn), alpha_map),
        ],
        out_specs=pl.BlockSpec((tm, tn), lambda i, j, k: (i, j)),
        compiler_params=_cparams(("parallel", "parallel", "arbitrary")),
        name="ff2",
    )(h, w, xs, mod3)


def _rope_partner():
    j = np.arange(MLA_ROPE)
    quarter = MLA_ROPE // 4
    return np.where((j // quarter) % 2 == 0, j + quarter, j - quarter)


def _rope_table(s, tm):
    pos = jnp.arange(s)
    row = (pos // GRID_W).astype(F32)
    col = (pos % GRID_W).astype(F32)
    n_freq = MLA_ROPE // 4
    inv_freq = ROPE_THETA ** (-jnp.arange(n_freq, dtype=F32) / n_freq)
    ang_r = row[:, None] * inv_freq
    ang_c = col[:, None] * inv_freq
    cos = jnp.concatenate([jnp.cos(ang_r)] * 2 + [jnp.cos(ang_c)] * 2, axis=-1)
    sin = jnp.concatenate([-jnp.sin(ang_r), jnp.sin(ang_r), -jnp.sin(ang_c), jnp.sin(ang_c)], axis=-1)
    ident = jnp.concatenate([jnp.ones((tm, MLA_ROPE), F32), jnp.zeros((tm, MLA_ROPE), F32)], axis=-1)
    return jnp.concatenate([jnp.concatenate([cos, sin], axis=-1), ident], axis=0)


def _pack_w_in(w_in, b_in, gate_b, sizes, n_pad):
    bounds = np.cumsum((0,) + sizes)
    seg = [slice(int(bounds[i]), int(bounds[i + 1])) for i in range(len(sizes))]
    partner = _rope_partner()
    order = (0, 1, 2, 3, 5, 6, 8, 9)
    w_bf = w_in.astype(BF16)
    w_parts = [w_bf[:, seg[i]] for i in order]
    b_parts = [b_in[seg[i]] for i in order]
    kpe_w = w_bf[:, seg[7]]
    kpe_b = b_in[seg[7]]
    w_parts += [kpe_w, kpe_w[:, partner]]
    b_parts += [kpe_b, kpe_b[partner]]
    offs = {}
    pos = 0
    for name, part in zip(("q", "k", "v", "o", "qa", "kva", "u", "gates", "kpe", "kpe_sw"), w_parts):
        offs[name] = pos
        pos += part.shape[1]
    pad = n_pad - pos
    w_parts.append(jnp.zeros((w_in.shape[0], pad), BF16))
    b_parts.append(jnp.zeros((pad,), b_in.dtype))
    w = jnp.concatenate(w_parts, axis=1)
    b = jnp.concatenate(b_parts)[None, :]
    n_g = sizes[4]
    wg = jnp.concatenate([w_bf[:, seg[4]], jnp.zeros((w_in.shape[0], LANES - n_g), BF16)], axis=1)
    bg = jnp.concatenate([b_in[seg[4]] + gate_b.reshape(-1), jnp.zeros((LANES - n_g,), F32)])[None, :]
    return w, b, wg, bg, offs


def _pack_mla(w_uq, w_ukv, qn_g, kn_g):
    partner = _rope_partner()
    lora = w_uq.shape[0]
    wq = w_uq.reshape(lora, MLA_HEADS, MLA_DQK)
    wq = jnp.concatenate([wq, wq[:, :, MLA_NOPE + partner]], axis=-1).reshape(lora, MLA_HEADS * MLA_SLAB)
    wkv = w_ukv.reshape(w_ukv.shape[0], MLA_HEADS, -1)
    wkv = jnp.concatenate([wkv[:, :, :MLA_NOPE].reshape(lora, -1), wkv[:, :, MLA_NOPE:].reshape(lora, -1)], axis=-1)

    def gains(g):
        return jnp.concatenate([g, g[MLA_NOPE + partner]])[None, :]

    return wq.astype(BF16), wkv.astype(BF16), gains(qn_g), gains(kn_g)


def _pack_s5(a_re, a_im, log_dt, b_re, b_im, c_re, c_im):
    n_dir, n_groups, n_state = a_re.shape
    gc = b_re.shape[-1]
    nb = n_groups // S5_BLOCK_GROUPS
    lam_re = jnp.minimum(a_re.astype(F32), -1e-4)
    lam_im = a_im.astype(F32)
    dt = jnp.exp(log_dt.astype(F32))[..., None]

    def pole_power(k):
        mag = jnp.exp(k * lam_re * dt)
        return mag * jnp.cos(k * lam_im * dt), mag * jnp.sin(k * lam_im * dt)

    bar_re, bar_im = pole_power(1.0)
    den = lam_re * lam_re + lam_im * lam_im
    f_re = ((bar_re - 1.0) * lam_re + bar_im * lam_im) / den
    f_im = (bar_im * lam_re - (bar_re - 1.0) * lam_im) / den
    bb_re = f_re[..., None] * b_re.astype(F32) - f_im[..., None] * b_im.astype(F32)
    bb_im = f_re[..., None] * b_im.astype(F32) + f_im[..., None] * b_re.astype(F32)
    eye = jnp.eye(S5_BLOCK_GROUPS, dtype=F32)
    sub = S5_SUB

    def per_block(a):
        return a.reshape(a.shape[:-2] + (nb, S5_BLOCK_GROUPS * n_state))

    def block_b(part):
        p = part.reshape(n_dir, nb, S5_BLOCK_GROUPS, n_state, gc)
        m = jnp.einsum('dbgnc,gh->dbgchn', p, eye, precision=HIGHEST)
        return m.reshape(n_dir, nb, LANES, S5_HALF)

    def block_c(part):
        p = part.astype(F32).reshape(n_dir, nb, S5_BLOCK_GROUPS, gc, n_state)
        m = jnp.einsum('dbgcn,gh->dbgnhc', p, eye, precision=HIGHEST)
        return m.reshape(n_dir, nb, S5_HALF, LANES)

    bm_re, bm_im = block_b(bb_re), block_b(bb_im)
    cm_re, cm_im = block_c(c_re), block_c(c_im)

    tau = jnp.arange(sub + 1, dtype=F32)[:, None, None, None]
    p_re, p_im = (per_block(p) for p in pole_power(tau))
    pr, pi = p_re[:sub, :, :, None, :], p_im[:sub, :, :, None, :]
    taps = (jnp.einsum('zdbkn,dbnc->zdbkc', bm_re * pr - bm_im * pi, cm_re, precision=HIGHEST)
            - jnp.einsum('zdbkn,dbnc->zdbkc', bm_re * pi + bm_im * pr, cm_im, precision=HIGHEST))

    ct_re, ct_im = jnp.swapaxes(cm_re, -1, -2), jnp.swapaxes(cm_im, -1, -2)
    cat = functools.partial(jnp.concatenate, axis=-1)
    fac = jnp.stack([cat([bm_re, bm_im]), cat([-bm_im, bm_re]), cat([ct_re, -ct_im]), cat([-ct_im, -ct_re])],
                    axis=2)
    t = np.arange(sub)
    zero_taps = jnp.zeros((sub - 1,) + taps.shape[2:], F32)
    pws, taps2 = [], []
    for d in range(n_dir):
        to_exit = (sub - 1 - t) if d == 0 else t
        age = (t + 1) if d == 0 else (sub - t)
        rows = [p_re[to_exit, d], p_im[to_exit, d], p_re[age, d], p_im[age, d]]
        pws.append(jnp.stack([cat([a, a]).transpose(1, 0, 2) for a in rows], axis=1))
        taps2.append(jnp.concatenate([zero_taps, taps[:, 0]] if d == 0 else [taps[::-1, 1], zero_taps], axis=0))
    pw = jnp.stack(pws)
    taps2 = jnp.stack(taps2).transpose(0, 2, 1, 3, 4)

    rows = jnp.arange(SUBLANES)
    tabs = []
    for d in range(n_dir):
        per_dir = []
        for shift in (1, 2, 4):
            keep = ((rows <= SUBLANES - 1 - shift) if d == 1 else (rows >= shift))[None, :, None]
            s_re, s_im = pole_power(float(shift * sub))
            per_dir += [jnp.where(keep, per_block(s_re[d])[:, None, :], 0.0),
                        jnp.where(keep, per_block(s_im[d])[:, None, :], 0.0)]
        expo = (((SUBLANES - rows) if d == 1 else (rows + 1)) * sub).astype(F32)
        s_re, s_im = pole_power(expo[:, None, None, None])
        per_dir += [jnp.moveaxis(per_block(s_re[:, d]), 0, 1), jnp.moveaxis(per_block(s_im[:, d]), 0, 1)]
        tabs.append(jnp.stack(per_dir, axis=1))
    return fac, pw, taps2, jnp.stack(tabs).astype(F32)


def _tri_matrices():
    t = np.arange(ML_CHUNK)
    lower = t[None, :] <= t[:, None]
    mask = np.where(np.stack([lower, lower.T]), 0.0, NEG_BIG).astype(np.float32)
    return jnp.asarray(lower.astype(np.float32), dtype=BF16), jnp.asarray(mask)


def kernel(x, c, ctx, c_ctx, w_mod, b_mod, norm_g, w_in, b_in, ml_gate_b, ml_norm_g, mla_qa_g, mla_kva_g, mla_w_uq, mla_w_ukv, mla_qn_g, mla_kn_g, s5_a_re, s5_a_im, s5_log_dt, s5_b_re, s5_b_im, s5_c_re, s5_c_im, s5_d, s5_w_glu, s5_b_glu, w_branch, w_out, w_ff1, w_ff2):
    b, s, d = x.shape
    tc = ctx.shape[1]
    depth = w_mod.shape[0]
    dv = ml_norm_g.shape[2]
    dk = dv // 2
    lora = mla_qa_g.shape[1]
    s5_width = s5_d.shape[1]
    branch_w = w_branch.shape[2]
    sizes = (ML_HEADS * dk, ML_HEADS * dk, ML_HEADS * dv, ML_HEADS * dv, 4 * ML_HEADS, lora, lora, MLA_ROPE,
             s5_width, N_BRANCH * d)
    assert sum(sizes) == w_in.shape[2] and b + 1 <= SUBLANES
    assert s % ML_CHUNK == 0 and tc % ML_CHUNK == 0 and branch_w == ML_HEADS * dv == MLA_HEADS * LANES == s5_width

    n_x = b * s
    n_c = b * tc
    tm = _row_tile(s, n_c)
    n_x_tiles = n_x // tm
    n_tiles = n_x_tiles + n_c // tm
    tiles_per_batch = s // tm
    tile_args = (n_x_tiles, tiles_per_batch, b)

    xs = jnp.concatenate([x.reshape(n_x, d), ctx.reshape(n_c, d)], axis=0)
    cc = jnp.concatenate([c, c_ctx[None, :], jnp.zeros((SUBLANES - b - 1, d), F32)], axis=0)
    mod = _modulation(cc, w_mod, b_mod)
    tm_q = min(tm, 512)
    tab = _rope_table(s, tm_q)
    tri, ml_mask = _tri_matrices()
    n_used = sum(sizes) - sizes[4] + MLA_ROPE
    tn_in = 1280
    n_pad = -(-n_used // tn_in) * tn_in

    for l in range(depth):
        with_ctx_out = l < depth - 1
        mod3 = mod[l].reshape(SUBLANES * N_MOD, 1, d)
        w_p, b_p, wg, bg, offs = _pack_w_in(w_in[l], b_in[l], ml_gate_b[l], sizes, n_pad)
        z, gz = _in_proj(xs, norm_g[l, 0][None, :], mod3, w_p, b_p, wg, bg, tm, tn_in, *tile_args)

        a_x, a_c = _mlstm(z, gz, tri, ml_mask, ml_norm_g[l], b, s, tc, dk, dv,
                          (offs["q"], offs["k"], offs["v"], offs["o"]), with_ctx_out)

        wq, wkv, gq, gk = _pack_mla(mla_w_uq[l], mla_w_ukv[l], mla_qn_g[l], mla_kn_g[l])
        qo, ko, vo = _mla_proj(z, tab, mla_qa_g[l][None, :], mla_kva_g[l][None, :], wq, wkv, gq, gk, tm_q,
                               (offs["qa"], offs["kva"], offs["kpe"]), n_x // tm_q, s // tm_q)
        tq = min(256, s)
        b_x = _attention(qo, ko, vo, b, s, tc, tq, True)

        fac, pw, taps, tabs = _pack_s5(s5_a_re[l], s5_a_im[l], s5_log_dt[l], s5_b_re[l], s5_b_im[l],
                                       s5_c_re[l], s5_c_im[l])
        y = _s5_scan(z, fac, pw, taps, tabs, b, s, tc, offs["u"])
        n_out_tiles = n_tiles if with_ctx_out else n_x_tiles
        c_all = _s5_glu(y, z, s5_d[l][None, :], s5_w_glu[l].astype(BF16), s5_b_glu[l][None, :], tm, offs["u"],
                        n_out_tiles)

        if with_ctx_out:
            b_c = _attention(qo, ko, vo, b, s, tc, min(tq, tc), False)
        else:
            a_c, b_c = a_x, b_x
        merged = _merge(a_x, a_c, b_x, b_c, c_all, z, w_branch, l, tm, 512, offs["gates"],
                        n_out_tiles, n_x_tiles)
        xs1 = _out_proj_residual(merged, w_out, l, xs, mod3, tm, 1024, n_out_tiles, *tile_args, 2)
        hid = _ff1(xs1, norm_g[l, 1][None, :], mod3, w_ff1, l, tm, 1024, n_out_tiles, *tile_args)
        xs = _ff2(hid, w_ff2, l, xs1, mod3, tm, 512, 2048, n_out_tiles, *tile_args)

    return xs.reshape(b, s, d)
```

```python
import functools
import math

import jax
import jax.numpy as jnp
import numpy as np
from jax import lax
from jax.experimental import pallas as pl
from jax.experimental.pallas import tpu as pltpu

F32 = jnp.float32
BF16 = jnp.bfloat16
HIGHEST = lax.Precision.HIGHEST

N_MOD = 6
N_BRANCH = 3
ML_HEADS = 4
MLA_HEADS = 8
MLA_NOPE = 128
MLA_ROPE = 64
MLA_DQK = MLA_NOPE + MLA_ROPE
MLA_SLAB = 256
GRID_W = 64
ROPE_THETA = 10000.0
S5_GROUP = 16
S5_STATE = 64
S5_BLOCK_GROUPS = 8
EPS = 1e-6
NEG_BIG = -1e30

LANES = 128
SUBLANES = 8
VMEM_LIMIT = 56 * 1024 * 1024

ML_CHUNK = 256
S5_SUB = 8
ATTN_KEY_CHUNK = 512
Q_PRESCALE = MLA_DQK ** -0.5 * math.log2(math.e)
ROW_CHUNK = 64


def _cparams(sem):
    return pltpu.CompilerParams(dimension_semantics=sem, vmem_limit_bytes=VMEM_LIMIT)


def _row_tile(n_x_rows_per_batch, n_ctx_rows):
    tm = 1024
    while n_x_rows_per_batch % tm or n_ctx_rows % tm:
        tm //= 2
    return tm


def _mod_kernel(c_ref, w_ref, b_ref, o_ref):
    s = c_ref[...]
    s = s * jax.nn.sigmoid(s)
    o_ref[0] = jnp.dot(s.astype(BF16), w_ref[0].astype(BF16), preferred_element_type=F32) + b_ref[0]


def _modulation(cc, w_mod, b_mod):
    n_layers, d, n = w_mod.shape
    tn = 1024
    return pl.pallas_call(
        _mod_kernel,
        out_shape=jax.ShapeDtypeStruct((n_layers, SUBLANES, n), F32),
        grid=(n_layers, n // tn),
        in_specs=[
            pl.BlockSpec((SUBLANES, d), lambda l, j: (0, 0)),
            pl.BlockSpec((1, d, tn), lambda l, j: (l, 0, j)),
            pl.BlockSpec((1, 1, tn), lambda l, j: (l, 0, j)),
        ],
        out_specs=pl.BlockSpec((1, SUBLANES, tn), lambda l, j: (l, 0, j)),
        compiler_params=_cparams(("parallel", "parallel")),
        name="adaln_mod",
    )(cc, w_mod, b_mod.reshape(n_layers, 1, n))


def _norm_mod_rows(x_ref, g_ref, sh_ref, sc_ref, xn_ref):
    tm = x_ref.shape[0]
    g = g_ref[...]
    sc = 1.0 + sc_ref[0]
    sh = sh_ref[0]

    def body(r, carry):
        rows = pl.ds(pl.multiple_of(r * ROW_CHUNK, ROW_CHUNK), ROW_CHUNK)
        x = x_ref[rows, :]
        ms = jnp.mean(x * x, axis=-1, keepdims=True)
        y = x * lax.rsqrt(ms + EPS) * g
        xn_ref[rows, :] = (y * sc + sh).astype(BF16)
        return carry

    lax.fori_loop(0, tm // ROW_CHUNK, body, 0)


def _mod_row_map(n_x_tiles, tiles_per_batch, ctx_row, k):
    def index_map(i, j):
        r = jnp.where(i < n_x_tiles, i // tiles_per_batch, ctx_row)
        return (r * N_MOD + k, 0, 0)
    return index_map


def _stream_specs(block, n_x_tiles, ctx_tile0, col_map):
    x_spec = pl.BlockSpec(block, lambda i, *r: (jnp.minimum(i, n_x_tiles - 1), col_map(*r)))
    c_spec = pl.BlockSpec(block, lambda i, *r: (ctx_tile0 + jnp.maximum(i - n_x_tiles, 0), col_map(*r)),
                          pipeline_mode=pl.Buffered(1))
    return x_spec, c_spec


def _in_kernel(x_ref, g_ref, sh_ref, sc_ref, w_ref, b_ref, wg_ref, bg_ref, z_ref, gz_ref, xn_ref):
    @pl.when(pl.program_id(1) == 0)
    def _():
        _norm_mod_rows(x_ref, g_ref, sh_ref, sc_ref, xn_ref)
        gz_ref[...] = jnp.dot(xn_ref[...], wg_ref[...], preferred_element_type=F32) + bg_ref[...]

    z_ref[...] = (jnp.dot(xn_ref[...], w_ref[...], preferred_element_type=F32) + b_ref[...]).astype(BF16)


def _in_proj(xs, g, mod3, w, b, wg, bg, tm, tn, n_x_tiles, tiles_per_batch, ctx_row):
    r, d = xs.shape
    nz = w.shape[1]
    return pl.pallas_call(
        _in_kernel,
        out_shape=(jax.ShapeDtypeStruct((r, nz), BF16), jax.ShapeDtypeStruct((r, LANES), F32)),
        grid=(r // tm, nz // tn),
        in_specs=[
            pl.BlockSpec((tm, d), lambda i, j: (i, 0)),
            pl.BlockSpec((1, d), lambda i, j: (0, 0)),
            pl.BlockSpec((1, 1, d), _mod_row_map(n_x_tiles, tiles_per_batch, ctx_row, 0)),
            pl.BlockSpec((1, 1, d), _mod_row_map(n_x_tiles, tiles_per_batch, ctx_row, 1)),
            pl.BlockSpec((d, tn), lambda i, j: (0, j)),
            pl.BlockSpec((1, tn), lambda i, j: (0, j)),
            pl.BlockSpec((d, LANES), lambda i, j: (0, 0)),
            pl.BlockSpec((1, LANES), lambda i, j: (0, 0)),
        ],
        out_specs=(
            pl.BlockSpec((tm, tn), lambda i, j: (i, j)),
            pl.BlockSpec((tm, LANES), lambda i, j: (i, 0)),
        ),
        scratch_shapes=[pltpu.VMEM((tm, d), BF16)],
        compiler_params=_cparams(("parallel", "arbitrary")),
        name="in_proj",
    )(xs, g, mod3, mod3, w, b, wg, bg)


def _log_sigmoid(x):
    return jnp.minimum(x, 0.0) - jnp.log1p(jnp.exp(-jnp.abs(x)))


def _split3(a):
    hi = a.astype(BF16)
    r1 = a - hi.astype(F32)
    mid = r1.astype(BF16)
    lo = (r1 - mid.astype(F32)).astype(BF16)
    return hi, mid, lo


def _ml_chunk(q, k_t, v, cum_cb, lg_r, mask_add, carry, reverse, inv_scale):
    c_mat, n_mat, m = carry
    length, dv = v.shape
    gi = 2 if reverse else 0
    ci = 5 if reverse else 4
    crow = lg_r[ci:ci + 1, :] - lg_r[gi:gi + 1, :]
    total = jnp.sum(lg_r[gi + 1:gi + 2, :], axis=-1, keepdims=True)
    ones = jnp.ones((length, LANES), BF16)
    n_blk = length // LANES

    def wide(a):
        return jnp.concatenate([a] * (dv // LANES), axis=1)

    log_w = [cum_cb - crow[:, j * LANES:(j + 1) * LANES] + mask_add[:, j * LANES:(j + 1) * LANES]
             for j in range(n_blk)]
    row_max = jnp.max(functools.reduce(jnp.maximum, log_w), axis=-1, keepdims=True)
    log_inter = cum_cb + m
    m_t = jnp.maximum(log_inter, row_max)
    w_inter = jnp.exp(log_inter - m_t)
    qk = jnp.dot(q, k_t, preferred_element_type=F32)
    s = jnp.concatenate([qk[:, j * LANES:(j + 1) * LANES] * jnp.exp(log_w[j] - m_t) for j in range(n_blk)],
                        axis=1).astype(BF16)
    num = wide(w_inter) * jnp.dot(q, c_mat.astype(BF16), preferred_element_type=F32)
    num = num + jnp.dot(s, v, preferred_element_type=F32)
    den = w_inter * jnp.dot(q, n_mat.astype(BF16), preferred_element_type=F32)
    den = den + jnp.dot(s, ones, preferred_element_type=F32)
    h = num * wide(1.0 / jnp.maximum(jnp.abs(den), jnp.exp(-m_t) * inv_scale))

    log_end = total - crow
    m_new = jnp.maximum(total + m, jnp.max(log_end, axis=-1, keepdims=True))
    decay = jnp.exp(total + m - m_new)
    kw_t = (k_t.astype(F32) * jnp.exp(log_end - m_new)).astype(BF16)
    c_new = decay * c_mat + jnp.dot(kw_t, v, preferred_element_type=F32)
    n_new = decay * n_mat + jnp.dot(kw_t, ones, preferred_element_type=F32)
    return h, (c_new, n_new, m_new)


def _mlstm_kernel(with_ctx_out, qx, kx, vx, ox, qc, kc, vc, oc, gx, gc, tri_ref, mask_ref, ng_ref, *rest):
    if with_ctx_out:
        ax_ref, ac_ref, cumx, lgx_r, ktx, cumc, lgc_r, ktc, hx, hc = rest
    else:
        ax_ref, cumx, lgx_r, ktx, cumc, lgc_r, ktc, hx, hc = rest
        ac_ref = None
    head = pl.program_id(1)
    dk = qx.shape[1]
    inv_scale = float(dk) ** 0.5
    n_x_chunks = qx.shape[0] // ML_CHUNK
    n_c_chunks = qc.shape[0] // ML_CHUNK

    r_idx = lax.broadcasted_iota(jnp.int32, (LANES, LANES), 0)
    c_idx = lax.broadcasted_iota(jnp.int32, (LANES, LANES), 1)
    sel = jnp.where((r_idx == c_idx * ML_HEADS + head) & (c_idx < 4), 1.0, 0.0).astype(BF16)
    tri = tri_ref[...]

    def prep(g_ref, k_ref, cum_ref, lg_r_ref, kt_ref):
        def body(i, carry):
            rows = pl.ds(pl.multiple_of(i * ML_CHUNK, ML_CHUNK), ML_CHUNK)
            gs = sum(jnp.dot(p, sel, preferred_element_type=F32) for p in _split3(g_ref[rows, :]))
            col = lax.broadcasted_iota(jnp.int32, gs.shape, 1)
            lg = jnp.where(col % 2 == 1, _log_sigmoid(gs), gs)
            pre = sum(jnp.dot(tri, p, preferred_element_type=F32) for p in _split3(lg))
            suf = pre[ML_CHUNK - 1:ML_CHUNK, :] - pre + lg
            cum_ref[0, rows, :] = jnp.broadcast_to(pre[:, 1:2], pre.shape)
            cum_ref[1, rows, :] = jnp.broadcast_to(suf[:, 3:4], suf.shape)
            lg = jnp.where(col == 4, pltpu.roll(pre, 3, axis=1), jnp.where(col == 5, pltpu.roll(suf, 2, axis=1), lg))
            lg_r_ref[i] = lg.T[:SUBLANES, :]
            kt_ref[i] = k_ref[rows, :].astype(F32).T.astype(BF16)
            return carry
        n_chunks = g_ref.shape[0] // ML_CHUNK
        lax.fori_loop(0, n_chunks, body, 0, unroll=2 if n_chunks % 2 == 0 else 1)

    prep(gx, kx, cumx, lgx_r, ktx)
    prep(gc, kc, cumc, lgc_r, ktc)

    for d in range(2):
        reverse = d == 1
        mask_add = mask_ref[d]

        def step(q_ref, v_ref, cum_ref, lg_r_ref, kt_ref, h_ref, ci, carry):
            rows = pl.ds(pl.multiple_of(ci * ML_CHUNK, ML_CHUNK), ML_CHUNK)
            h, carry = _ml_chunk(q_ref[rows, :], kt_ref[ci], v_ref[rows, :], cum_ref[d, rows, :], lg_r_ref[ci],
                                 mask_add, carry, reverse, inv_scale)
            if reverse:
                h_ref[rows, :] += h
            else:
                h_ref[rows, :] = h
            return carry

        carry = (jnp.zeros((dk, vx.shape[1]), F32), jnp.zeros((dk, LANES), F32), jnp.zeros((1, 1), F32))

        def ctx_body(i, carry):
            ci = (n_c_chunks - 1 - i) if reverse else i
            return step(qc, vc, cumc, lgc_r, ktc, hc, ci, carry)

        def x_body(i, carry):
            ci = (n_x_chunks - 1 - i) if reverse else i
            return step(qx, vx, cumx, lgx_r, ktx, hx, ci, carry)

        carry = lax.fori_loop(0, n_c_chunks, ctx_body, carry)
        lax.fori_loop(0, n_x_chunks, x_body, carry, unroll=2 if n_x_chunks % 2 == 0 else 1)

    ng = ng_ref[0]

    def finish(h_ref, o_ref, a_ref):
        def body(i, carry):
            rows = pl.ds(pl.multiple_of(i * ML_CHUNK, ML_CHUNK), ML_CHUNK)
            h = h_ref[rows, :]
            hn = h * lax.rsqrt(jnp.mean(h * h, axis=-1, keepdims=True) + EPS) * ng
            a_ref[rows, :] = (hn * jax.nn.sigmoid(o_ref[rows, :].astype(F32))).astype(BF16)
            return carry
        lax.fori_loop(0, h_ref.shape[0] // ML_CHUNK, body, 0)

    finish(hx, ox, ax_ref)
    if with_ctx_out:
        finish(hc, oc, ac_ref)


def _mlstm(z, gz, tri, mask, ml_norm_g, b, s, tc, dk, dv, cols, with_ctx_out):
    n_x = b * s
    cq, ck, cv, co = cols
    ctx0 = n_x // tc

    def xspec(width, col0):
        return pl.BlockSpec((s, width), lambda i, h: (i, col0 // width + h))

    def cspec(width, col0):
        return pl.BlockSpec((tc, width), lambda i, h: (ctx0 + i, col0 // width + h))

    out_shape = [jax.ShapeDtypeStruct((n_x, ML_HEADS * dv), BF16)]
    out_specs = [pl.BlockSpec((s, dv), lambda i, h: (i, h))]
    if with_ctx_out:
        out_shape.append(jax.ShapeDtypeStruct((b * tc, ML_HEADS * dv), BF16))
        out_specs.append(pl.BlockSpec((tc, dv), lambda i, h: (i, h)))
    res = pl.pallas_call(
        functools.partial(_mlstm_kernel, with_ctx_out),
        out_shape=tuple(out_shape),
        grid=(b, ML_HEADS),
        in_specs=[
            xspec(dk, cq), xspec(dk, ck), xspec(dv, cv), xspec(dv, co),
            cspec(dk, cq), cspec(dk, ck), cspec(dv, cv), cspec(dv, co),
            pl.BlockSpec((s, LANES), lambda i, h: (i, 0)),
            pl.BlockSpec((tc, LANES), lambda i, h: (ctx0 + i, 0)),
            pl.BlockSpec((ML_CHUNK, ML_CHUNK), lambda i, h: (0, 0)),
            pl.BlockSpec((2, ML_CHUNK, ML_CHUNK), lambda i, h: (0, 0, 0)),
            pl.BlockSpec((1, 1, dv), lambda i, h: (h, 0, 0)),
        ],
        out_specs=tuple(out_specs),
        scratch_shapes=[
            pltpu.VMEM((2, s, LANES), F32), pltpu.VMEM((s // ML_CHUNK, SUBLANES, ML_CHUNK), F32),
            pltpu.VMEM((s // ML_CHUNK, dk, ML_CHUNK), BF16),
            pltpu.VMEM((2, tc, LANES), F32), pltpu.VMEM((tc // ML_CHUNK, SUBLANES, ML_CHUNK), F32),
            pltpu.VMEM((tc // ML_CHUNK, dk, ML_CHUNK), BF16),
            pltpu.VMEM((s, dv), F32), pltpu.VMEM((tc, dv), F32),
        ],
        compiler_params=_cparams(("parallel", "parallel")),
        name="mlstm",
    )(z, z, z, z, z, z, z, z, gz, gz, tri, mask, ml_norm_g.reshape(ML_HEADS, 1, dv))
    return res if with_ctx_out else (res[0], None)


def _mla_proj_kernel(qa_ref, kva_ref, kpe_ref, tab_ref, qag_ref, kvag_ref, wq_ref, wkv_ref,
                     gq_ref, gk_ref, q_ref, k_ref, v_ref):
    def normed(a_ref, g_ref):
        a = a_ref[...].astype(F32)
        return (a * lax.rsqrt(jnp.mean(a * a, axis=-1, keepdims=True) + EPS) * g_ref[...]).astype(BF16)

    q_all = jnp.dot(normed(qa_ref, qag_ref), wq_ref[...], preferred_element_type=F32)
    kv_all = jnp.dot(normed(kva_ref, kvag_ref), wkv_ref[...], preferred_element_type=F32)
    tab = tab_ref[...]
    lane = lax.broadcasted_iota(jnp.int32, tab.shape, 1)
    first_half = lane < MLA_ROPE
    gq = gq_ref[...]
    gk = gk_ref[...]
    inv_dqk = 1.0 / MLA_DQK

    kpe = kpe_ref[...].astype(F32)
    ss_kpe = jnp.sum(jnp.where(first_half, kpe * kpe, 0.0), axis=-1, keepdims=True)
    kpe_t = kpe * (tab * gk[:, LANES:])
    kpe_rot = jnp.where(first_half, kpe_t + pltpu.roll(kpe_t, MLA_ROPE, axis=1), 0.0)

    for h in range(MLA_HEADS):
        qn = q_all[:, h * MLA_SLAB:h * MLA_SLAB + LANES]
        qp = q_all[:, h * MLA_SLAB + LANES:(h + 1) * MLA_SLAB]
        ss = jnp.sum(qn * qn, axis=-1, keepdims=True) + jnp.sum(jnp.where(first_half, qp * qp, 0.0), axis=-1,
                                                                keepdims=True)
        r = lax.rsqrt(ss * inv_dqk + EPS) * Q_PRESCALE
        qp_t = qp * (tab * gq[:, LANES:])
        qp_rot = qp_t + pltpu.roll(qp_t, MLA_ROPE, axis=1)
        q_ref[:, h * MLA_SLAB:h * MLA_SLAB + LANES] = (qn * r * gq[:, :LANES]).astype(BF16)
        q_ref[:, h * MLA_SLAB + LANES:(h + 1) * MLA_SLAB] = (qp_rot * r).astype(BF16)

        kn = kv_all[:, h * LANES:(h + 1) * LANES]
        rk = lax.rsqrt((jnp.sum(kn * kn, axis=-1, keepdims=True) + ss_kpe) * inv_dqk + EPS)
        k_ref[:, h * MLA_SLAB:h * MLA_SLAB + LANES] = (kn * rk * gk[:, :LANES]).astype(BF16)
        k_ref[:, h * MLA_SLAB + LANES:(h + 1) * MLA_SLAB] = (kpe_rot * rk).astype(BF16)

    ones_col = jnp.where(lane == 0, 1.0, 0.0).astype(BF16)
    for h in range(MLA_HEADS):
        v_ref[:, h * MLA_SLAB:h * MLA_SLAB + LANES] = kv_all[:, (MLA_HEADS + h) * LANES:(MLA_HEADS + h + 1) * LANES
                                                             ].astype(BF16)
        v_ref[:, h * MLA_SLAB + LANES:(h + 1) * MLA_SLAB] = ones_col


def _mla_proj(z, tab, qag, kvag, wq, wkv, gq, gk, tm, cols, n_x_tiles, tab_tiles):
    r = z.shape[0]
    cqa, ckva, ckpe = cols
    lora = qag.shape[1]
    hs = MLA_HEADS * MLA_SLAB
    hv = MLA_HEADS * LANES
    return pl.pallas_call(
        _mla_proj_kernel,
        out_shape=(jax.ShapeDtypeStruct((r, hs), BF16), jax.ShapeDtypeStruct((r, hs), BF16),
                   jax.ShapeDtypeStruct((r, hs), BF16)),
        grid=(r // tm,),
        in_specs=[
            pl.BlockSpec((tm, lora), lambda i: (i, cqa // lora)),
            pl.BlockSpec((tm, lora), lambda i: (i, ckva // lora)),
            pl.BlockSpec((tm, LANES), lambda i: (i, ckpe // LANES)),
            pl.BlockSpec((tm, LANES), lambda i: (jnp.where(i < n_x_tiles, i % tab_tiles, tab_tiles), 0)),
            pl.BlockSpec((1, lora), lambda i: (0, 0)),
            pl.BlockSpec((1, lora), lambda i: (0, 0)),
            pl.BlockSpec((lora, hs), lambda i: (0, 0)),
            pl.BlockSpec((lora, 2 * hv), lambda i: (0, 0)),
            pl.BlockSpec((1, MLA_SLAB), lambda i: (0, 0)),
            pl.BlockSpec((1, MLA_SLAB), lambda i: (0, 0)),
        ],
        out_specs=(pl.BlockSpec((tm, hs), lambda i: (i, 0)), pl.BlockSpec((tm, hs), lambda i: (i, 0)),
                   pl.BlockSpec((tm, hs), lambda i: (i, 0))),
        compiler_params=_cparams(("parallel",)),
        name="mla_qkv",
    )(z, z, z, tab, qag, kvag, wq, wkv, gq, gk)


def _attn_kernel(n_kv, tq, q_ref, *refs):
    k_refs = refs[:n_kv]
    v_refs = refs[n_kv:2 * n_kv]
    o_ref, s0_ref, s1_ref, m0_ref, m1_ref = refs[2 * n_kv:]
    slots = ((s0_ref, m0_ref), (s1_ref, m1_ref))
    n_tiles = q_ref.shape[0] // tq
    chunks = []
    col = 0
    for kv, k_ref in enumerate(k_refs):
        n_keys = k_ref.shape[0]
        step = min(ATTN_KEY_CHUNK, n_keys)
        for off in range(0, n_keys, step):
            chunks.append((kv, off, col, step))
            col += step

    def scores(t, slot):
        s_ref, m_ref = slots[slot]
        rows = pl.ds(pl.multiple_of(t * tq, tq), tq)
        q = q_ref[rows, :]
        run = None
        for kv, off, c0, size in chunks:
            s = lax.dot_general(q, k_refs[kv][off:off + size, :], (((1,), (1,)), ((), ())),
                                preferred_element_type=F32)
            s_ref[:, c0:c0 + size] = s
            for lb in range(size // LANES):
                blk = s[:, lb * LANES:(lb + 1) * LANES]
                run = blk if run is None else jnp.maximum(run, blk)
        m_ref[...] = run

    def finish(t, slot):
        s_ref, m_ref = slots[slot]
        rows = pl.ds(pl.multiple_of(t * tq, tq), tq)
        m = jnp.max(m_ref[...], axis=-1, keepdims=True)
        acc = None
        for kv, off, c0, size in chunks:
            p = jnp.exp2(s_ref[:, c0:c0 + size] - m).astype(BF16)
            pv = jnp.dot(p, v_refs[kv][off:off + size, :], preferred_element_type=F32)
            acc = pv if acc is None else acc + pv
        o_ref[rows, :] = (acc[:, :LANES] / acc[:, LANES:LANES + 1]).astype(BF16)

    scores(0, 0)

    def body(k, carry):
        scores(2 * k + 1, 1)
        finish(2 * k, 0)
        scores(jnp.minimum(2 * k + 2, n_tiles - 1), 0)
        finish(2 * k + 1, 1)
        return carry

    lax.fori_loop(0, n_tiles // 2, body, 0)
    if n_tiles % 2:
        finish(n_tiles - 1, 0)


def _attention(qo, ko, vo, b, s, tc, tq, latent):
    n_x = b * s
    ctx0 = n_x // tc
    cspec = pl.BlockSpec((tc, MLA_SLAB), lambda i, h: (ctx0 + i, h))
    xspec = pl.BlockSpec((s, MLA_SLAB), lambda i, h: (i, h))
    if latent:
        n_q, n_keys = s, s + tc
        in_specs = [xspec, cspec, xspec, cspec, xspec]
        args = (qo, ko, ko, vo, vo)
    else:
        n_q, n_keys = tc, tc
        in_specs = [cspec, cspec, cspec]
        args = (qo, ko, vo)
    return pl.pallas_call(
        functools.partial(_attn_kernel, (len(args) - 1) // 2, tq),
        out_shape=jax.ShapeDtypeStruct((b * n_q, MLA_HEADS * LANES), BF16),
        grid=(b, MLA_HEADS),
        in_specs=in_specs,
        out_specs=pl.BlockSpec((n_q, LANES), lambda i, h: (i, h)),
        scratch_shapes=[pltpu.VMEM((tq, n_keys), F32), pltpu.VMEM((tq, n_keys), F32),
                        pltpu.VMEM((tq, LANES), F32), pltpu.VMEM((tq, LANES), F32)],
        compiler_params=_cparams(("parallel", "parallel")),
        name="attn_latent" if latent else "attn_ctx",
    )(*args)


S5_HALF = S5_BLOCK_GROUPS * S5_STATE


def _s5_kernel(n_batch, rows_x, rows_c, dot_rows, u_ref, fac_ref, pw_ref, tap_ref, tab_ref, y_ref,
               u2_ref, v_ref, r_ref, m_ref, ot_ref):
    d = pl.program_id(1)
    n_dot = u2_ref.shape[0] // dot_rows

    for s in range(S5_SUB):
        blk = slice(s * LANES, (s + 1) * LANES)
        r_ref[blk, :] = (fac_ref[0, 0, 0] * pw_ref[0, 0, 0, s:s + 1, :]
                         + fac_ref[0, 0, 1] * pw_ref[0, 0, 1, s:s + 1, :]).astype(BF16)
        ot_ref[blk, :] = (fac_ref[0, 0, 2] * pw_ref[0, 0, 2, s:s + 1, :]
                          + fac_ref[0, 0, 3] * pw_ref[0, 0, 3, s:s + 1, :]).astype(BF16)
        for t in range(S5_SUB):
            m_ref[blk, t * LANES:(t + 1) * LANES] = tap_ref[0, 0, t - s + S5_SUB - 1].astype(BF16)

    def dot_rows_of(i):
        return pl.ds(pl.multiple_of(i * dot_rows, dot_rows), dot_rows)

    def token_rows_of(i, s):
        return pl.ds(i * (dot_rows * S5_SUB) + s, dot_rows, stride=S5_SUB)

    @pl.when(d == 0)
    def _():
        def stage(i, carry):
            rows = pl.ds(pl.multiple_of(i * dot_rows, dot_rows), dot_rows)
            y_ref[rows, :] = u_ref[rows, :].astype(F32)
            return carry

        lax.fori_loop(0, n_dot * S5_SUB, stage, 0)

        def regroup(i, carry):
            for s in range(S5_SUB):
                u2_ref[dot_rows_of(i), s * LANES:(s + 1) * LANES] = y_ref[token_rows_of(i, s), :].astype(BF16)
            return carry

        lax.fori_loop(0, n_dot, regroup, 0)

    def increments(i, carry):
        rows = dot_rows_of(i)
        v_ref[rows, :] = jnp.dot(u2_ref[rows, :], r_ref[...], preferred_element_type=F32)
        return carry

    lax.fori_loop(0, n_dot, increments, 0)

    def cmul_add(ar, ai, cr, ci, xr, xi):
        return ar + cr * xr - ci * xi, ai + cr * xi + ci * xr

    def run(reverse):
        tab = tab_ref.at[0, 0]
        last = 0 if reverse else SUBLANES - 1
        first_row = lax.broadcasted_iota(jnp.int32, (SUBLANES, S5_HALF), 0) == (SUBLANES - 1 - last)

        def segment(bases, n_groups, carry):
            def body(i, carry):
                gi = (n_groups - 1 - i) if reverse else i
                out = []
                for base, (cre, cim) in zip(bases, carry):
                    rows = pl.ds(pl.multiple_of(base + gi * SUBLANES, SUBLANES), SUBLANES)
                    re = v_ref[rows, :S5_HALF]
                    im = v_ref[rows, S5_HALF:]
                    for lvl, shift in enumerate((1, 2, 4)):
                        sh = (SUBLANES - shift) if reverse else shift
                        re, im = cmul_add(re, im, tab[2 * lvl], tab[2 * lvl + 1],
                                          pltpu.roll(re, sh, axis=0), pltpu.roll(im, sh, axis=0))
                    re, im = cmul_add(re, im, tab[6], tab[7], cre, cim)
                    sh1 = (SUBLANES - 1) if reverse else 1
                    v_ref[rows, :S5_HALF] = jnp.where(first_row, cre, pltpu.roll(re, sh1, axis=0))
                    v_ref[rows, S5_HALF:] = jnp.where(first_row, cim, pltpu.roll(im, sh1, axis=0))
                    out.append((jnp.broadcast_to(re[last:last + 1, :], re.shape),
                                jnp.broadcast_to(im[last:last + 1, :], im.shape)))
                return tuple(out)
            return lax.fori_loop(0, n_groups, body, carry)

        zero = jnp.zeros((SUBLANES, S5_HALF), F32)
        carry = tuple((zero, zero) for _ in range(n_batch))
        carry = segment([n_batch * rows_x + bi * rows_c for bi in range(n_batch)], rows_c // SUBLANES, carry)
        segment([bi * rows_x for bi in range(n_batch)], rows_x // SUBLANES, carry)

        def outputs(i, carry):
            rows = dot_rows_of(i)
            y = jnp.dot(u2_ref[rows, :], m_ref[...], preferred_element_type=F32)
            y = y + lax.dot_general(v_ref[rows, :].astype(BF16), ot_ref[...], (((1,), (1,)), ((), ())),
                                    preferred_element_type=F32)
            for s in range(S5_SUB):
                part = y[:, s * LANES:(s + 1) * LANES]
                if reverse:
                    y_ref[token_rows_of(i, s), :] += part
                else:
                    y_ref[token_rows_of(i, s), :] = part
            return carry

        lax.fori_loop(0, n_dot, outputs, 0)

    @pl.when(d == 0)
    def _():
        run(False)

    @pl.when(d == 1)
    def _():
        run(True)


def _s5_scan(z, fac, pw, taps, tabs, b, s, tc, col_u):
    r = z.shape[0]
    rc = r // S5_SUB
    n_blocks = fac.shape[1]
    width = S5_SUB * LANES
    assert width == 2 * S5_HALF
    dot_rows = max(n for n in range(16, 641, 16) if rc % n == 0)
    return pl.pallas_call(
        functools.partial(_s5_kernel, b, s // S5_SUB, tc // S5_SUB, dot_rows),
        out_shape=jax.ShapeDtypeStruct((r, n_blocks * LANES), F32),
        grid=(n_blocks, 2),
        in_specs=[
            pl.BlockSpec((r, LANES), lambda cb, d: (0, col_u // LANES + cb), pipeline_mode=pl.Buffered(1)),
            pl.BlockSpec((1, 1, 4, LANES, width), lambda cb, d: (d, cb, 0, 0, 0), pipeline_mode=pl.Buffered(1)),
            pl.BlockSpec((1, 1, 4, S5_SUB, width), lambda cb, d: (d, cb, 0, 0, 0)),
            pl.BlockSpec((1, 1, 2 * S5_SUB - 1, LANES, LANES), lambda cb, d: (d, cb, 0, 0, 0)),
            pl.BlockSpec((1, 1, 8, SUBLANES, S5_HALF), lambda cb, d: (d, cb, 0, 0, 0)),
        ],
        out_specs=pl.BlockSpec((r, LANES), lambda cb, d: (0, cb)),
        scratch_shapes=[pltpu.VMEM((rc, width), BF16), pltpu.VMEM((rc, width), F32),
                        pltpu.VMEM((width, width), BF16), pltpu.VMEM((width, width), BF16),
                        pltpu.VMEM((width, width), BF16)],
        compiler_params=_cparams(("parallel", "arbitrary")),
        name="s5_scan",
    )(z, fac, pw, taps, tabs)


def _glu_kernel(y_ref, u_ref, d_ref, w_ref, b_ref, o_ref, g_ref):
    tm = y_ref.shape[0]

    def body(r, carry):
        rows = pl.ds(pl.multiple_of(r * ROW_CHUNK, ROW_CHUNK), ROW_CHUNK)
        y = y_ref[rows, :] + d_ref[...] * u_ref[rows, :].astype(F32)
        g_ref[rows, :] = jax.nn.gelu(y).astype(BF16)
        return carry

    lax.fori_loop(0, tm // ROW_CHUNK, body, 0)
    g = g_ref[...]
    gate = jax.nn.sigmoid(jnp.dot(g, w_ref[...], preferred_element_type=F32) + b_ref[...])
    o_ref[...] = (g.astype(F32) * gate).astype(BF16)


def _s5_glu(y, z, d_skip, w_glu, b_glu, tm, col_u, n_row_tiles):
    width = y.shape[1]
    return pl.pallas_call(
        _glu_kernel,
        out_shape=jax.ShapeDtypeStruct((n_row_tiles * tm, width), BF16),
        grid=(n_row_tiles,),
        in_specs=[
            pl.BlockSpec((tm, width), lambda i: (i, 0)),
            pl.BlockSpec((tm, width), lambda i: (i, col_u // width)),
            pl.BlockSpec((1, width), lambda i: (0, 0)),
            pl.BlockSpec((width, width), lambda i: (0, 0)),
            pl.BlockSpec((1, width), lambda i: (0, 0)),
        ],
        out_specs=pl.BlockSpec((tm, width), lambda i: (i, 0)),
        scratch_shapes=[pltpu.VMEM((tm, width), BF16)],
        compiler_params=_cparams(("parallel",)),
        name="s5_glu",
    )(y, z, d_skip, w_glu, b_glu)


def _merge_kernel(n_x_tiles, ax_ref, ac_ref, bx_ref, bc_ref, c_ref, ga_ref, gb_ref, gc_ref, w_ref, o_ref):
    def combine(a_ref, b_ref):
        acc = None
        for r, (br_ref, gate_ref) in enumerate(((a_ref, ga_ref), (b_ref, gb_ref), (c_ref, gc_ref))):
            proj = jnp.dot(br_ref[...], w_ref[r], preferred_element_type=F32)
            term = jax.nn.sigmoid(gate_ref[...].astype(F32)) * proj
            acc = term if acc is None else acc + term
        o_ref[...] = acc.astype(BF16)

    @pl.when(pl.program_id(0) < n_x_tiles)
    def _():
        combine(ax_ref, bx_ref)

    @pl.when(pl.program_id(0) >= n_x_tiles)
    def _():
        combine(ac_ref, bc_ref)


def _merge(a_x, a_c, b_x, b_c, cc, z, w_branch, tm, tn, col_g, n_row_tiles, n_x_tiles):
    width = a_x.shape[1]
    d = w_branch.shape[2]

    def gate_spec(r):
        return pl.BlockSpec((tm, tn), lambda i, j: (i, (col_g + r * d) // tn + j))

    ax_spec, ac_spec = _stream_specs((tm, width), n_x_tiles, 0, lambda j: 0)
    return pl.pallas_call(
        functools.partial(_merge_kernel, n_x_tiles),
        out_shape=jax.ShapeDtypeStruct((n_row_tiles * tm, d), BF16),
        grid=(n_row_tiles, d // tn),
        in_specs=[
            ax_spec, ac_spec, ax_spec, ac_spec,
            pl.BlockSpec((tm, width), lambda i, j: (i, 0)),
            gate_spec(0), gate_spec(1), gate_spec(2),
            pl.BlockSpec((N_BRANCH, width, tn), lambda i, j: (0, 0, j)),
        ],
        out_specs=pl.BlockSpec((tm, tn), lambda i, j: (i, j)),
        compiler_params=_cparams(("parallel", "arbitrary")),
        name="merge",
    )(a_x, a_c, b_x, b_c, cc, z, z, z, w_branch)


def _resid_kernel(m_ref, w_ref, x_ref, al_ref, o_ref):
    o_ref[...] = x_ref[...] + al_ref[0] * jnp.dot(m_ref[...], w_ref[...], preferred_element_type=F32)


def _out_proj_residual(m, w, xs, mod3, tm, tn, n_row_tiles, n_x_tiles, tiles_per_batch, ctx_row, k_alpha):
    kdim, d = w.shape
    nt = d // tn

    def alpha_map(i, j):
        r = jnp.where(i < n_x_tiles, i // tiles_per_batch, ctx_row)
        return (r * N_MOD + k_alpha, 0, j)

    return pl.pallas_call(
        _resid_kernel,
        out_shape=jax.ShapeDtypeStruct((n_row_tiles * tm, d), F32),
        grid=(n_row_tiles, nt),
        in_specs=[
            pl.BlockSpec((tm, kdim), lambda i, j: (i, 0)),
            pl.BlockSpec((kdim, tn), lambda i, j: (0, j)),
            pl.BlockSpec((tm, tn), lambda i, j: (i, j)),
            pl.BlockSpec((1, 1, tn), alpha_map),
        ],
        out_specs=pl.BlockSpec((tm, tn), lambda i, j: (i, j)),
        compiler_params=_cparams(("parallel", "arbitrary")),
        name="out_proj",
    )(m, w, xs, mod3)


def _ff1_kernel(x_ref, g_ref, sh_ref, sc_ref, w_ref, h_ref, xn_ref):
    @pl.when(pl.program_id(1) == 0)
    def _():
        _norm_mod_rows(x_ref, g_ref, sh_ref, sc_ref, xn_ref)

    a = jnp.maximum(jnp.dot(xn_ref[...], w_ref[...], preferred_element_type=F32), 0.0)
    h_ref[...] = (a * a).astype(BF16)


def _ff1(xs, g, mod3, w, tm, tn, n_row_tiles, n_x_tiles, tiles_per_batch, ctx_row):
    d, dff = w.shape
    return pl.pallas_call(
        _ff1_kernel,
        out_shape=jax.ShapeDtypeStruct((n_row_tiles * tm, dff), BF16),
        grid=(n_row_tiles, dff // tn),
        in_specs=[
            pl.BlockSpec((tm, d), lambda i, j: (i, 0)),
            pl.BlockSpec((1, d), lambda i, j: (0, 0)),
            pl.BlockSpec((1, 1, d), _mod_row_map(n_x_tiles, tiles_per_batch, ctx_row, 3)),
            pl.BlockSpec((1, 1, d), _mod_row_map(n_x_tiles, tiles_per_batch, ctx_row, 4)),
            pl.BlockSpec((d, tn), lambda i, j: (0, j)),
        ],
        out_specs=pl.BlockSpec((tm, tn), lambda i, j: (i, j)),
        scratch_shapes=[pltpu.VMEM((tm, d), BF16)],
        compiler_params=_cparams(("parallel", "arbitrary")),
        name="ff1",
    )(xs, g, mod3, mod3, w)


def _ff2_kernel(h_ref, w_ref, x_ref, al_ref, o_ref):
    k = pl.program_id(2)
    part = jnp.dot(h_ref[...], w_ref[...], preferred_element_type=F32)

    @pl.when(k == 0)
    def _():
        o_ref[...] = part

    @pl.when(k > 0)
    def _():
        o_ref[...] += part

    @pl.when(k == pl.num_programs(2) - 1)
    def _():
        o_ref[...] = x_ref[...] + al_ref[0] * o_ref[...]


def _ff2(h, w, xs, mod3, tm, tn, tk, n_row_tiles, n_x_tiles, tiles_per_batch, ctx_row):
    dff, d = w.shape

    def alpha_map(i, j, k):
        r = jnp.where(i < n_x_tiles, i // tiles_per_batch, ctx_row)
        return (r * N_MOD + 5, 0, j)

    return pl.pallas_call(
        _ff2_kernel,
        out_shape=jax.ShapeDtypeStruct((n_row_tiles * tm, d), F32),
        grid=(n_row_tiles, d // tn, dff // tk),
        in_specs=[
            pl.BlockSpec((tm, tk), lambda i, j, k: (i, k)),
            pl.BlockSpec((tk, tn), lambda i, j, k: (k, j)),
            pl.BlockSpec((tm, tn), lambda i, j, k: (i, j)),
            pl.BlockSpec((1, 1, tn), alpha_map),
        ],
        out_specs=pl.BlockSpec((tm, tn), lambda i, j, k: (i, j)),
        compiler_params=_cparams(("parallel", "parallel", "arbitrary")),
        name="ff2",
    )(h, w, xs, mod3)


def _rope_partner():
    j = np.arange(MLA_ROPE)
    quarter = MLA_ROPE // 4
    return np.where((j // quarter) % 2 == 0, j + quarter, j - quarter)


def _rope_table(s, tm):
    pos = jnp.arange(s)
    row = (pos // GRID_W).astype(F32)
    col = (pos % GRID_W).astype(F32)
    n_freq = MLA_ROPE // 4
    inv_freq = ROPE_THETA ** (-jnp.arange(n_freq, dtype=F32) / n_freq)
    ang_r = row[:, None] * inv_freq
    ang_c = col[:, None] * inv_freq
    cos = jnp.concatenate([jnp.cos(ang_r)] * 2 + [jnp.cos(ang_c)] * 2, axis=-1)
    sin = jnp.concatenate([-jnp.sin(ang_r), jnp.sin(ang_r), -jnp.sin(ang_c), jnp.sin(ang_c)], axis=-1)
    ident = jnp.concatenate([jnp.ones((tm, MLA_ROPE), F32), jnp.zeros((tm, MLA_ROPE), F32)], axis=-1)
    return jnp.concatenate([jnp.concatenate([cos, sin], axis=-1), ident], axis=0)


REPACK_ROWS = 256
REPACK_COLS = 1024


def _repack_kernel(runs, w_ref, tail_ref, o_ref):
    for out0, in0, width in runs:
        for c in range(0, width, REPACK_COLS):
            w = min(REPACK_COLS, width - c)
            src = in0 + c
            base = src - src % LANES
            win = w_ref[0, :, base:src + w]
            o_ref[:, out0 + c:out0 + c + w] = win[:, src - base:].astype(BF16)
    tail0 = o_ref.shape[1] - tail_ref.shape[1]
    o_ref[:, tail0:] = tail_ref[...]


def _pack_w_in(w_in_all, layer, b_in, gate_b, sizes, n_pad):
    w_in = w_in_all[layer]
    d_model = w_in.shape[0]
    bounds = np.cumsum((0,) + sizes)
    seg = [slice(int(bounds[i]), int(bounds[i + 1])) for i in range(len(sizes))]
    partner = _rope_partner()
    order = (0, 1, 2, 3, 5, 6, 8, 9)
    names = ("q", "k", "v", "o", "qa", "kva", "u", "gates")
    offs, runs, pos = {}, [], 0
    for name, i in zip(names, order):
        offs[name] = pos
        if runs and runs[-1][1] + runs[-1][2] == seg[i].start:
            runs[-1][2] += sizes[i]
        else:
            runs.append([pos, seg[i].start, sizes[i]])
        pos += sizes[i]
    assert all(r[0] % LANES == 0 for r in runs) and pos % LANES == 0
    offs["kpe"], offs["kpe_sw"] = pos, pos + MLA_ROPE
    kpe_w = w_in[:, seg[7]]
    tail = jnp.concatenate([kpe_w, kpe_w[:, partner], jnp.zeros((d_model, n_pad - pos - 2 * MLA_ROPE), F32)],
                           axis=1).astype(BF16)
    w = pl.pallas_call(
        functools.partial(_repack_kernel, tuple(tuple(r) for r in runs)),
        out_shape=jax.ShapeDtypeStruct((d_model, n_pad), BF16),
        grid=(d_model // REPACK_ROWS,),
        in_specs=[
            pl.BlockSpec((1, REPACK_ROWS, w_in.shape[1]), lambda i: (layer, i, 0)),
            pl.BlockSpec((REPACK_ROWS, tail.shape[1]), lambda i: (i, 0)),
        ],
        out_specs=pl.BlockSpec((REPACK_ROWS, n_pad), lambda i: (i, 0)),
        compiler_params=_cparams(("parallel",)),
        name="repack_w_in",
    )(w_in_all, tail)
    kpe_b = b_in[seg[7]]
    b = jnp.concatenate([b_in[seg[i]] for i in order] + [kpe_b, kpe_b[partner], jnp.zeros((n_pad - pos - 2 * MLA_ROPE,), F32)])
    n_g = sizes[4]
    wg = jnp.concatenate([w_in[:, seg[4]], jnp.zeros((d_model, LANES - n_g), F32)], axis=1).astype(BF16)
    bg = jnp.concatenate([b_in[seg[4]] + gate_b.reshape(-1), jnp.zeros((LANES - n_g,), F32)])[None, :]
    return w, b[None, :], wg, bg, offs


def _pack_mla(w_uq, w_ukv, qn_g, kn_g):
    partner = _rope_partner()
    lora = w_uq.shape[0]
    wq = w_uq.reshape(lora, MLA_HEADS, MLA_DQK)
    wq = jnp.concatenate([wq, wq[:, :, MLA_NOPE + partner]], axis=-1).reshape(lora, MLA_HEADS * MLA_SLAB)
    wkv = w_ukv.reshape(w_ukv.shape[0], MLA_HEADS, -1)
    wkv = jnp.concatenate([wkv[:, :, :MLA_NOPE].reshape(lora, -1), wkv[:, :, MLA_NOPE:].reshape(lora, -1)], axis=-1)

    def gains(g):
        return jnp.concatenate([g, g[MLA_NOPE + partner]])[None, :]

    return wq.astype(BF16), wkv.astype(BF16), gains(qn_g), gains(kn_g)


def _pack_s5(a_re, a_im, log_dt, b_re, b_im, c_re, c_im):
    n_dir, n_groups, n_state = a_re.shape
    gc = b_re.shape[-1]
    nb = n_groups // S5_BLOCK_GROUPS
    lam_re = jnp.minimum(a_re.astype(F32), -1e-4)
    lam_im = a_im.astype(F32)
    dt = jnp.exp(log_dt.astype(F32))[..., None]

    def pole_power(k):
        mag = jnp.exp(k * lam_re * dt)
        return mag * jnp.cos(k * lam_im * dt), mag * jnp.sin(k * lam_im * dt)

    bar_re, bar_im = pole_power(1.0)
    den = lam_re * lam_re + lam_im * lam_im
    f_re = ((bar_re - 1.0) * lam_re + bar_im * lam_im) / den
    f_im = (bar_im * lam_re - (bar_re - 1.0) * lam_im) / den
    bb_re = f_re[..., None] * b_re.astype(F32) - f_im[..., None] * b_im.astype(F32)
    bb_im = f_re[..., None] * b_im.astype(F32) + f_im[..., None] * b_re.astype(F32)
    eye = jnp.eye(S5_BLOCK_GROUPS, dtype=F32)
    sub = S5_SUB

    def per_block(a):
        return a.reshape(a.shape[:-2] + (nb, S5_BLOCK_GROUPS * n_state))

    def block_b(part):
        p = part.reshape(n_dir, nb, S5_BLOCK_GROUPS, n_state, gc)
        m = jnp.einsum('dbgnc,gh->dbgchn', p, eye, precision=HIGHEST)
        return m.reshape(n_dir, nb, LANES, S5_HALF)

    def block_c(part):
        p = part.astype(F32).reshape(n_dir, nb, S5_BLOCK_GROUPS, gc, n_state)
        m = jnp.einsum('dbgcn,gh->dbgnhc', p, eye, precision=HIGHEST)
        return m.reshape(n_dir, nb, S5_HALF, LANES)

    bm_re, bm_im = block_b(bb_re), block_b(bb_im)
    cm_re, cm_im = block_c(c_re), block_c(c_im)

    tau = jnp.arange(sub + 1, dtype=F32)[:, None, None, None]
    p_re, p_im = (per_block(p) for p in pole_power(tau))
    pr, pi = p_re[:sub, :, :, None, :], p_im[:sub, :, :, None, :]
    taps = (jnp.einsum('zdbkn,dbnc->zdbkc', bm_re * pr - bm_im * pi, cm_re, precision=HIGHEST)
            - jnp.einsum('zdbkn,dbnc->zdbkc', bm_re * pi + bm_im * pr, cm_im, precision=HIGHEST))

    ct_re, ct_im = jnp.swapaxes(cm_re, -1, -2), jnp.swapaxes(cm_im, -1, -2)
    cat = functools.partial(jnp.concatenate, axis=-1)
    fac = jnp.stack([cat([bm_re, bm_im]), cat([-bm_im, bm_re]), cat([ct_re, -ct_im]), cat([-ct_im, -ct_re])],
                    axis=2)
    t = np.arange(sub)
    zero_taps = jnp.zeros((sub - 1,) + taps.shape[2:], F32)
    pws, taps2 = [], []
    for d in range(n_dir):
        to_exit = (sub - 1 - t) if d == 0 else t
        age = (t + 1) if d == 0 else (sub - t)
        rows = [p_re[to_exit, d], p_im[to_exit, d], p_re[age, d], p_im[age, d]]
        pws.append(jnp.stack([cat([a, a]).transpose(1, 0, 2) for a in rows], axis=1))
        taps2.append(jnp.concatenate([zero_taps, taps[:, 0]] if d == 0 else [taps[::-1, 1], zero_taps], axis=0))
    pw = jnp.stack(pws)
    taps2 = jnp.stack(taps2).transpose(0, 2, 1, 3, 4)

    rows = jnp.arange(SUBLANES)
    tabs = []
    for d in range(n_dir):
        per_dir = []
        for shift in (1, 2, 4):
            keep = ((rows <= SUBLANES - 1 - shift) if d == 1 else (rows >= shift))[None, :, None]
            s_re, s_im = pole_power(float(shift * sub))
            per_dir += [jnp.where(keep, per_block(s_re[d])[:, None, :], 0.0),
                        jnp.where(keep, per_block(s_im[d])[:, None, :], 0.0)]
        expo = (((SUBLANES - rows) if d == 1 else (rows + 1)) * sub).astype(F32)
        s_re, s_im = pole_power(expo[:, None, None, None])
        per_dir += [jnp.moveaxis(per_block(s_re[:, d]), 0, 1), jnp.moveaxis(per_block(s_im[:, d]), 0, 1)]
        tabs.append(jnp.stack(per_dir, axis=1))
    return fac, pw, taps2, jnp.stack(tabs).astype(F32)


def _tri_matrices():
    t = np.arange(ML_CHUNK)
    lower = t[None, :] <= t[:, None]
    mask = np.where(np.stack([lower, lower.T]), 0.0, NEG_BIG).astype(np.float32)
    return jnp.asarray(lower.astype(np.float32), dtype=BF16), jnp.asarray(mask)


def kernel(x, c, ctx, c_ctx, w_mod, b_mod, norm_g, w_in, b_in, ml_gate_b, ml_norm_g, mla_qa_g, mla_kva_g, mla_w_uq, mla_w_ukv, mla_qn_g, mla_kn_g, s5_a_re, s5_a_im, s5_log_dt, s5_b_re, s5_b_im, s5_c_re, s5_c_im, s5_d, s5_w_glu, s5_b_glu, w_branch, w_out, w_ff1, w_ff2):
    b, s, d = x.shape
    tc = ctx.shape[1]
    depth = w_mod.shape[0]
    dv = ml_norm_g.shape[2]
    dk = dv // 2
    lora = mla_qa_g.shape[1]
    s5_width = s5_d.shape[1]
    branch_w = w_branch.shape[2]
    sizes = (ML_HEADS * dk, ML_HEADS * dk, ML_HEADS * dv, ML_HEADS * dv, 4 * ML_HEADS, lora, lora, MLA_ROPE,
             s5_width, N_BRANCH * d)
    assert sum(sizes) == w_in.shape[2] and b + 1 <= SUBLANES
    assert s % ML_CHUNK == 0 and tc % ML_CHUNK == 0 and branch_w == ML_HEADS * dv == MLA_HEADS * LANES == s5_width

    n_x = b * s
    n_c = b * tc
    tm = _row_tile(s, n_c)
    n_x_tiles = n_x // tm
    n_tiles = n_x_tiles + n_c // tm
    tiles_per_batch = s // tm
    tile_args = (n_x_tiles, tiles_per_batch, b)

    xs = jnp.concatenate([x.reshape(n_x, d), ctx.reshape(n_c, d)], axis=0)
    cc = jnp.concatenate([c, c_ctx[None, :], jnp.zeros((SUBLANES - b - 1, d), F32)], axis=0)
    mod = _modulation(cc, w_mod, b_mod)
    tm_q = min(tm, 512)
    tab = _rope_table(s, tm_q)
    tri, ml_mask = _tri_matrices()
    n_used = sum(sizes) - sizes[4] + MLA_ROPE
    tn_in = 1280
    n_pad = -(-n_used // tn_in) * tn_in

    for l in range(depth):
        with_ctx_out = l < depth - 1
        mod3 = mod[l].reshape(SUBLANES * N_MOD, 1, d)
        w_p, b_p, wg, bg, offs = _pack_w_in(w_in, l, b_in[l], ml_gate_b[l], sizes, n_pad)
        z, gz = _in_proj(xs, norm_g[l, 0][None, :], mod3, w_p, b_p, wg, bg, tm, tn_in, *tile_args)

        a_x, a_c = _mlstm(z, gz, tri, ml_mask, ml_norm_g[l], b, s, tc, dk, dv,
                          (offs["q"], offs["k"], offs["v"], offs["o"]), with_ctx_out)

        wq, wkv, gq, gk = _pack_mla(mla_w_uq[l], mla_w_ukv[l], mla_qn_g[l], mla_kn_g[l])
        qo, ko, vo = _mla_proj(z, tab, mla_qa_g[l][None, :], mla_kva_g[l][None, :], wq, wkv, gq, gk, tm_q,
                               (offs["qa"], offs["kva"], offs["kpe"]), n_x // tm_q, s // tm_q)
        tq = min(256, s)
        b_x = _attention(qo, ko, vo, b, s, tc, tq, True)

        fac, pw, taps, tabs = _pack_s5(s5_a_re[l], s5_a_im[l], s5_log_dt[l], s5_b_re[l], s5_b_im[l],
                                       s5_c_re[l], s5_c_im[l])
        y = _s5_scan(z, fac, pw, taps, tabs, b, s, tc, offs["u"])
        n_out_tiles = n_tiles if with_ctx_out else n_x_tiles
        c_all = _s5_glu(y, z, s5_d[l][None, :], s5_w_glu[l].astype(BF16), s5_b_glu[l][None, :], tm, offs["u"],
                        n_out_tiles)

        if with_ctx_out:
            b_c = _attention(qo, ko, vo, b, s, tc, min(tq, tc), False)
        else:
            a_c, b_c = a_x, b_x
        merged = _merge(a_x, a_c, b_x, b_c, c_all, z, w_branch[l].astype(BF16), tm, 512, offs["gates"],
                        n_out_tiles, n_x_tiles)
        xs1 = _out_proj_residual(merged, w_out[l].astype(BF16), xs, mod3, tm, 1024, n_out_tiles, *tile_args, 2)
        hid = _ff1(xs1, norm_g[l, 1][None, :], mod3, w_ff1[l].astype(BF16), tm, 1024, n_out_tiles, *tile_args)
        xs = _ff2(hid, w_ff2[l].astype(BF16), xs1, mod3, tm, 1024, 2048, n_out_tiles, *tile_args)

    return xs.reshape(b, s, d)
```

```python
import functools
import math

import jax
import jax.numpy as jnp
import numpy as np
from jax import lax
from jax.experimental import pallas as pl
from jax.experimental.pallas import tpu as pltpu

F32 = jnp.float32
BF16 = jnp.bfloat16
HIGHEST = lax.Precision.HIGHEST

N_MOD = 6
N_BRANCH = 3
ML_HEADS = 4
MLA_HEADS = 8
MLA_NOPE = 128
MLA_ROPE = 64
MLA_DQK = MLA_NOPE + MLA_ROPE
MLA_SLAB = 256
GRID_W = 64
ROPE_THETA = 10000.0
S5_GROUP = 16
S5_STATE = 64
S5_BLOCK_GROUPS = 8
EPS = 1e-6
NEG_BIG = -1e30

LANES = 128
SUBLANES = 8
VMEM_LIMIT = 56 * 1024 * 1024

ML_CHUNK = 256
S5_SUB = 8
ATTN_KEY_CHUNK = 512
Q_PRESCALE = MLA_DQK ** -0.5 * math.log2(math.e)
ROW_CHUNK = 64


def _cparams(sem):
    return pltpu.CompilerParams(dimension_semantics=sem, vmem_limit_bytes=VMEM_LIMIT)


def _row_tile(n_x_rows_per_batch, n_ctx_rows):
    tm = 1024
    while n_x_rows_per_batch % tm or n_ctx_rows % tm:
        tm //= 2
    return tm


def _mod_kernel(c_ref, w_ref, b_ref, o_ref):
    s = c_ref[...]
    s = s * jax.nn.sigmoid(s)
    o_ref[0] = jnp.dot(s.astype(BF16), w_ref[0].astype(BF16), preferred_element_type=F32) + b_ref[0]


def _modulation(cc, w_mod, b_mod):
    n_layers, d, n = w_mod.shape
    tn = 1024
    return pl.pallas_call(
        _mod_kernel,
        out_shape=jax.ShapeDtypeStruct((n_layers, SUBLANES, n), F32),
        grid=(n_layers, n // tn),
        in_specs=[
            pl.BlockSpec((SUBLANES, d), lambda l, j: (0, 0)),
            pl.BlockSpec((1, d, tn), lambda l, j: (l, 0, j)),
            pl.BlockSpec((1, 1, tn), lambda l, j: (l, 0, j)),
        ],
        out_specs=pl.BlockSpec((1, SUBLANES, tn), lambda l, j: (l, 0, j)),
        compiler_params=_cparams(("parallel", "parallel")),
        name="adaln_mod",
    )(cc, w_mod, b_mod.reshape(n_layers, 1, n))


def _norm_mod_rows(x_ref, g_ref, sh_ref, sc_ref, xn_ref):
    tm = x_ref.shape[0]
    g = g_ref[...]
    sc = 1.0 + sc_ref[0]
    sh = sh_ref[0]

    def body(r, carry):
        rows = pl.ds(pl.multiple_of(r * ROW_CHUNK, ROW_CHUNK), ROW_CHUNK)
        x = x_ref[rows, :]
        ms = jnp.mean(x * x, axis=-1, keepdims=True)
        y = x * lax.rsqrt(ms + EPS) * g
        xn_ref[rows, :] = (y * sc + sh).astype(BF16)
        return carry

    lax.fori_loop(0, tm // ROW_CHUNK, body, 0)


def _mod_row_map(n_x_tiles, tiles_per_batch, ctx_row, k):
    def index_map(i, j):
        r = jnp.where(i < n_x_tiles, i // tiles_per_batch, ctx_row)
        return (r * N_MOD + k, 0, 0)
    return index_map


def _stream_specs(block, n_x_tiles, ctx_tile0, col_map):
    x_spec = pl.BlockSpec(block, lambda i, *r: (jnp.minimum(i, n_x_tiles - 1), col_map(*r)))
    c_spec = pl.BlockSpec(block, lambda i, *r: (ctx_tile0 + jnp.maximum(i - n_x_tiles, 0), col_map(*r)),
                          pipeline_mode=pl.Buffered(1))
    return x_spec, c_spec


def _in_kernel(x_ref, g_ref, sh_ref, sc_ref, w_ref, b_ref, wg_ref, bg_ref, z_ref, gz_ref, xn_ref):
    @pl.when(pl.program_id(1) == 0)
    def _():
        _norm_mod_rows(x_ref, g_ref, sh_ref, sc_ref, xn_ref)
        gz_ref[...] = jnp.dot(xn_ref[...], wg_ref[...], preferred_element_type=F32) + bg_ref[...]

    z_ref[...] = (jnp.dot(xn_ref[...], w_ref[...], preferred_element_type=F32) + b_ref[...]).astype(BF16)


def _in_proj(xs, g, mod3, w, b, wg, bg, tm, tn, n_x_tiles, tiles_per_batch, ctx_row):
    r, d = xs.shape
    nz = w.shape[1]
    return pl.pallas_call(
        _in_kernel,
        out_shape=(jax.ShapeDtypeStruct((r, nz), BF16), jax.ShapeDtypeStruct((r, LANES), F32)),
        grid=(r // tm, nz // tn),
        in_specs=[
            pl.BlockSpec((tm, d), lambda i, j: (i, 0)),
            pl.BlockSpec((1, d), lambda i, j: (0, 0)),
            pl.BlockSpec((1, 1, d), _mod_row_map(n_x_tiles, tiles_per_batch, ctx_row, 0)),
            pl.BlockSpec((1, 1, d), _mod_row_map(n_x_tiles, tiles_per_batch, ctx_row, 1)),
            pl.BlockSpec((d, tn), lambda i, j: (0, j)),
            pl.BlockSpec((1, tn), lambda i, j: (0, j)),
            pl.BlockSpec((d, LANES), lambda i, j: (0, 0)),
            pl.BlockSpec((1, LANES), lambda i, j: (0, 0)),
        ],
        out_specs=(
            pl.BlockSpec((tm, tn), lambda i, j: (i, j)),
            pl.BlockSpec((tm, LANES), lambda i, j: (i, 0)),
        ),
        scratch_shapes=[pltpu.VMEM((tm, d), BF16)],
        compiler_params=_cparams(("parallel", "arbitrary")),
        name="in_proj",
    )(xs, g, mod3, mod3, w, b, wg, bg)


def _log_sigmoid(x):
    return jnp.minimum(x, 0.0) - jnp.log1p(jnp.exp(-jnp.abs(x)))


def _split3(a):
    hi = a.astype(BF16)
    r1 = a - hi.astype(F32)
    mid = r1.astype(BF16)
    lo = (r1 - mid.astype(F32)).astype(BF16)
    return hi, mid, lo


def _ml_chunk(q, k_t, v, cum_cb, lg_r, mask_add, carry, reverse, inv_scale):
    c_mat, n_mat, m = carry
    length, dv = v.shape
    gi = 2 if reverse else 0
    ci = 5 if reverse else 4
    crow = lg_r[ci:ci + 1, :] - lg_r[gi:gi + 1, :]
    total = jnp.sum(lg_r[gi + 1:gi + 2, :], axis=-1, keepdims=True)
    ones = jnp.ones((length, LANES), BF16)
    n_blk = length // LANES

    def wide(a):
        return jnp.concatenate([a] * (dv // LANES), axis=1)

    log_w = [cum_cb - crow[:, j * LANES:(j + 1) * LANES] + mask_add[:, j * LANES:(j + 1) * LANES]
             for j in range(n_blk)]
    row_max = jnp.max(functools.reduce(jnp.maximum, log_w), axis=-1, keepdims=True)
    log_inter = cum_cb + m
    m_t = jnp.maximum(log_inter, row_max)
    w_inter = jnp.exp(log_inter - m_t)
    qk = jnp.dot(q, k_t, preferred_element_type=F32)
    s = jnp.concatenate([qk[:, j * LANES:(j + 1) * LANES] * jnp.exp(log_w[j] - m_t) for j in range(n_blk)],
                        axis=1).astype(BF16)
    num = wide(w_inter) * jnp.dot(q, c_mat.astype(BF16), preferred_element_type=F32)
    num = num + jnp.dot(s, v, preferred_element_type=F32)
    den = w_inter * jnp.dot(q, n_mat.astype(BF16), preferred_element_type=F32)
    den = den + jnp.dot(s, ones, preferred_element_type=F32)
    h = num * wide(1.0 / jnp.maximum(jnp.abs(den), jnp.exp(-m_t) * inv_scale))

    log_end = total - crow
    m_new = jnp.maximum(total + m, jnp.max(log_end, axis=-1, keepdims=True))
    decay = jnp.exp(total + m - m_new)
    kw_t = (k_t.astype(F32) * jnp.exp(log_end - m_new)).astype(BF16)
    c_new = decay * c_mat + jnp.dot(kw_t, v, preferred_element_type=F32)
    n_new = decay * n_mat + jnp.dot(kw_t, ones, preferred_element_type=F32)
    return h, (c_new, n_new, m_new)


def _mlstm_kernel(with_ctx_out, qx, kx, vx, ox, qc, kc, vc, oc, gx, gc, tri_ref, mask_ref, ng_ref, *rest):
    if with_ctx_out:
        ax_ref, ac_ref, cumx, lgx_r, ktx, cumc, lgc_r, ktc, hx, hc = rest
    else:
        ax_ref, cumx, lgx_r, ktx, cumc, lgc_r, ktc, hx, hc = rest
        ac_ref = None
    head = pl.program_id(1)
    dk = qx.shape[1]
    inv_scale = float(dk) ** 0.5
    n_x_chunks = qx.shape[0] // ML_CHUNK
    n_c_chunks = qc.shape[0] // ML_CHUNK

    r_idx = lax.broadcasted_iota(jnp.int32, (LANES, LANES), 0)
    c_idx = lax.broadcasted_iota(jnp.int32, (LANES, LANES), 1)
    sel = jnp.where((r_idx == c_idx * ML_HEADS + head) & (c_idx < 4), 1.0, 0.0).astype(BF16)
    tri = tri_ref[...]

    def prep(g_ref, k_ref, cum_ref, lg_r_ref, kt_ref):
        def body(i, carry):
            rows = pl.ds(pl.multiple_of(i * ML_CHUNK, ML_CHUNK), ML_CHUNK)
            gs = sum(jnp.dot(p, sel, preferred_element_type=F32) for p in _split3(g_ref[rows, :]))
            col = lax.broadcasted_iota(jnp.int32, gs.shape, 1)
            lg = jnp.where(col % 2 == 1, _log_sigmoid(gs), gs)
            pre = sum(jnp.dot(tri, p, preferred_element_type=F32) for p in _split3(lg))
            suf = pre[ML_CHUNK - 1:ML_CHUNK, :] - pre + lg
            cum_ref[0, rows, :] = jnp.broadcast_to(pre[:, 1:2], pre.shape)
            cum_ref[1, rows, :] = jnp.broadcast_to(suf[:, 3:4], suf.shape)
            lg = jnp.where(col == 4, pltpu.roll(pre, 3, axis=1), jnp.where(col == 5, pltpu.roll(suf, 2, axis=1), lg))
            lg_r_ref[i] = lg.T[:SUBLANES, :]
            kt_ref[i] = k_ref[rows, :].astype(F32).T.astype(BF16)
            return carry
        n_chunks = g_ref.shape[0] // ML_CHUNK
        lax.fori_loop(0, n_chunks, body, 0, unroll=2 if n_chunks % 2 == 0 else 1)

    prep(gx, kx, cumx, lgx_r, ktx)
    prep(gc, kc, cumc, lgc_r, ktc)

    for d in range(2):
        reverse = d == 1
        mask_add = mask_ref[d]

        def step(q_ref, v_ref, cum_ref, lg_r_ref, kt_ref, h_ref, ci, carry):
            rows = pl.ds(pl.multiple_of(ci * ML_CHUNK, ML_CHUNK), ML_CHUNK)
            h, carry = _ml_chunk(q_ref[rows, :], kt_ref[ci], v_ref[rows, :], cum_ref[d, rows, :], lg_r_ref[ci],
                                 mask_add, carry, reverse, inv_scale)
            if reverse:
                h_ref[rows, :] += h
            else:
                h_ref[rows, :] = h
            return carry

        carry = (jnp.zeros((dk, vx.shape[1]), F32), jnp.zeros((dk, LANES), F32), jnp.zeros((1, 1), F32))

        def ctx_body(i, carry):
            ci = (n_c_chunks - 1 - i) if reverse else i
            return step(qc, vc, cumc, lgc_r, ktc, hc, ci, carry)

        def x_body(i, carry):
            ci = (n_x_chunks - 1 - i) if reverse else i
            return step(qx, vx, cumx, lgx_r, ktx, hx, ci, carry)

        carry = lax.fori_loop(0, n_c_chunks, ctx_body, carry)
        lax.fori_loop(0, n_x_chunks, x_body, carry, unroll=2 if n_x_chunks % 2 == 0 else 1)

    ng = ng_ref[0]

    def finish(h_ref, o_ref, a_ref):
        def body(i, carry):
            rows = pl.ds(pl.multiple_of(i * ML_CHUNK, ML_CHUNK), ML_CHUNK)
            h = h_ref[rows, :]
            hn = h * lax.rsqrt(jnp.mean(h * h, axis=-1, keepdims=True) + EPS) * ng
            a_ref[rows, :] = (hn * jax.nn.sigmoid(o_ref[rows, :].astype(F32))).astype(BF16)
            return carry
        lax.fori_loop(0, h_ref.shape[0] // ML_CHUNK, body, 0)

    finish(hx, ox, ax_ref)
    if with_ctx_out:
        finish(hc, oc, ac_ref)


def _mlstm(z, gz, tri, mask, ml_norm_g, b, s, tc, dk, dv, cols, with_ctx_out):
    n_x = b * s
    cq, ck, cv, co = cols
    ctx0 = n_x // tc

    def xspec(width, col0):
        return pl.BlockSpec((s, width), lambda i, h: (i, col0 // width + h))

    def cspec(width, col0):
        return pl.BlockSpec((tc, width), lambda i, h: (ctx0 + i, col0 // width + h))

    out_shape = [jax.ShapeDtypeStruct((n_x, ML_HEADS * dv), BF16)]
    out_specs = [pl.BlockSpec((s, dv), lambda i, h: (i, h))]
    if with_ctx_out:
        out_shape.append(jax.ShapeDtypeStruct((b * tc, ML_HEADS * dv), BF16))
        out_specs.append(pl.BlockSpec((tc, dv), lambda i, h: (i, h)))
    res = pl.pallas_call(
        functools.partial(_mlstm_kernel, with_ctx_out),
        out_shape=tuple(out_shape),
        grid=(b, ML_HEADS),
        in_specs=[
            xspec(dk, cq), xspec(dk, ck), xspec(dv, cv), xspec(dv, co),
            cspec(dk, cq), cspec(dk, ck), cspec(dv, cv), cspec(dv, co),
            pl.BlockSpec((s, LANES), lambda i, h: (i, 0)),
            pl.BlockSpec((tc, LANES), lambda i, h: (ctx0 + i, 0)),
            pl.BlockSpec((ML_CHUNK, ML_CHUNK), lambda i, h: (0, 0)),
            pl.BlockSpec((2, ML_CHUNK, ML_CHUNK), lambda i, h: (0, 0, 0)),
            pl.BlockSpec((1, 1, dv), lambda i, h: (h, 0, 0)),
        ],
        out_specs=tuple(out_specs),
        scratch_shapes=[
            pltpu.VMEM((2, s, LANES), F32), pltpu.VMEM((s // ML_CHUNK, SUBLANES, ML_CHUNK), F32),
            pltpu.VMEM((s // ML_CHUNK, dk, ML_CHUNK), BF16),
            pltpu.VMEM((2, tc, LANES), F32), pltpu.VMEM((tc // ML_CHUNK, SUBLANES, ML_CHUNK), F32),
            pltpu.VMEM((tc // ML_CHUNK, dk, ML_CHUNK), BF16),
            pltpu.VMEM((s, dv), F32), pltpu.VMEM((tc, dv), F32),
        ],
        compiler_params=_cparams(("parallel", "parallel")),
        name="mlstm",
    )(z, z, z, z, z, z, z, z, gz, gz, tri, mask, ml_norm_g.reshape(ML_HEADS, 1, dv))
    return res if with_ctx_out else (res[0], None)


def _mla_proj_kernel(qa_ref, kva_ref, kpe_ref, tab_ref, qag_ref, kvag_ref, wq_ref, wkv_ref,
                     gq_ref, gk_ref, q_ref, k_ref, v_ref):
    def normed(a_ref, g_ref):
        a = a_ref[...].astype(F32)
        return (a * lax.rsqrt(jnp.mean(a * a, axis=-1, keepdims=True) + EPS) * g_ref[...]).astype(BF16)

    q_all = jnp.dot(normed(qa_ref, qag_ref), wq_ref[...], preferred_element_type=F32)
    kv_all = jnp.dot(normed(kva_ref, kvag_ref), wkv_ref[...], preferred_element_type=F32)
    tab = tab_ref[...]
    lane = lax.broadcasted_iota(jnp.int32, tab.shape, 1)
    first_half = lane < MLA_ROPE
    gq = gq_ref[...]
    gk = gk_ref[...]
    inv_dqk = 1.0 / MLA_DQK

    kpe = kpe_ref[...].astype(F32)
    ss_kpe = jnp.sum(jnp.where(first_half, kpe * kpe, 0.0), axis=-1, keepdims=True)
    kpe_t = kpe * (tab * gk[:, LANES:])
    kpe_rot = jnp.where(first_half, kpe_t + pltpu.roll(kpe_t, MLA_ROPE, axis=1), 0.0)

    for h in range(MLA_HEADS):
        qn = q_all[:, h * MLA_SLAB:h * MLA_SLAB + LANES]
        qp = q_all[:, h * MLA_SLAB + LANES:(h + 1) * MLA_SLAB]
        ss = jnp.sum(qn * qn, axis=-1, keepdims=True) + jnp.sum(jnp.where(first_half, qp * qp, 0.0), axis=-1,
                                                                keepdims=True)
        r = lax.rsqrt(ss * inv_dqk + EPS) * Q_PRESCALE
        qp_t = qp * (tab * gq[:, LANES:])
        qp_rot = qp_t + pltpu.roll(qp_t, MLA_ROPE, axis=1)
        q_ref[:, h * MLA_SLAB:h * MLA_SLAB + LANES] = (qn * r * gq[:, :LANES]).astype(BF16)
        q_ref[:, h * MLA_SLAB + LANES:(h + 1) * MLA_SLAB] = (qp_rot * r).astype(BF16)

        kn = kv_all[:, h * LANES:(h + 1) * LANES]
        rk = lax.rsqrt((jnp.sum(kn * kn, axis=-1, keepdims=True) + ss_kpe) * inv_dqk + EPS)
        k_ref[:, h * MLA_SLAB:h * MLA_SLAB + LANES] = (kn * rk * gk[:, :LANES]).astype(BF16)
        k_ref[:, h * MLA_SLAB + LANES:(h + 1) * MLA_SLAB] = (kpe_rot * rk).astype(BF16)

    ones_col = jnp.where(lane == 0, 1.0, 0.0).astype(BF16)
    for h in range(MLA_HEADS):
        v_ref[:, h * MLA_SLAB:h * MLA_SLAB + LANES] = kv_all[:, (MLA_HEADS + h) * LANES:(MLA_HEADS + h + 1) * LANES
                                                             ].astype(BF16)
        v_ref[:, h * MLA_SLAB + LANES:(h + 1) * MLA_SLAB] = ones_col


def _mla_proj(z, tab, qag, kvag, wq, wkv, gq, gk, tm, cols, n_x_tiles, tab_tiles):
    r = z.shape[0]
    cqa, ckva, ckpe = cols
    lora = qag.shape[1]
    hs = MLA_HEADS * MLA_SLAB
    hv = MLA_HEADS * LANES
    return pl.pallas_call(
        _mla_proj_kernel,
        out_shape=(jax.ShapeDtypeStruct((r, hs), BF16), jax.ShapeDtypeStruct((r, hs), BF16),
                   jax.ShapeDtypeStruct((r, hs), BF16)),
        grid=(r // tm,),
        in_specs=[
            pl.BlockSpec((tm, lora), lambda i: (i, cqa // lora)),
            pl.BlockSpec((tm, lora), lambda i: (i, ckva // lora)),
            pl.BlockSpec((tm, LANES), lambda i: (i, ckpe // LANES)),
            pl.BlockSpec((tm, LANES), lambda i: (jnp.where(i < n_x_tiles, i % tab_tiles, tab_tiles), 0)),
            pl.BlockSpec((1, lora), lambda i: (0, 0)),
            pl.BlockSpec((1, lora), lambda i: (0, 0)),
            pl.BlockSpec((lora, hs), lambda i: (0, 0)),
            pl.BlockSpec((lora, 2 * hv), lambda i: (0, 0)),
            pl.BlockSpec((1, MLA_SLAB), lambda i: (0, 0)),
            pl.BlockSpec((1, MLA_SLAB), lambda i: (0, 0)),
        ],
        out_specs=(pl.BlockSpec((tm, hs), lambda i: (i, 0)), pl.BlockSpec((tm, hs), lambda i: (i, 0)),
                   pl.BlockSpec((tm, hs), lambda i: (i, 0))),
        compiler_params=_cparams(("parallel",)),
        name="mla_qkv",
    )(z, z, z, tab, qag, kvag, wq, wkv, gq, gk)


def _attn_kernel(n_kv, tq, q_ref, *refs):
    k_refs = refs[:n_kv]
    v_refs = refs[n_kv:2 * n_kv]
    o_ref, s0_ref, s1_ref, m0_ref, m1_ref = refs[2 * n_kv:]
    slots = ((s0_ref, m0_ref), (s1_ref, m1_ref))
    n_tiles = q_ref.shape[0] // tq
    chunks = []
    col = 0
    for kv, k_ref in enumerate(k_refs):
        n_keys = k_ref.shape[0]
        step = min(ATTN_KEY_CHUNK, n_keys)
        for off in range(0, n_keys, step):
            chunks.append((kv, off, col, step))
            col += step

    def scores(t, slot):
        s_ref, m_ref = slots[slot]
        rows = pl.ds(pl.multiple_of(t * tq, tq), tq)
        q = q_ref[rows, :]
        run = None
        for kv, off, c0, size in chunks:
            s = lax.dot_general(q, k_refs[kv][off:off + size, :], (((1,), (1,)), ((), ())),
                                preferred_element_type=F32)
            s_ref[:, c0:c0 + size] = s
            for lb in range(size // LANES):
                blk = s[:, lb * LANES:(lb + 1) * LANES]
                run = blk if run is None else jnp.maximum(run, blk)
        m_ref[...] = run

    def finish(t, slot):
        s_ref, m_ref = slots[slot]
        rows = pl.ds(pl.multiple_of(t * tq, tq), tq)
        m = jnp.max(m_ref[...], axis=-1, keepdims=True)
        acc = None
        for kv, off, c0, size in chunks:
            p = jnp.exp2(s_ref[:, c0:c0 + size] - m).astype(BF16)
            pv = jnp.dot(p, v_refs[kv][off:off + size, :], preferred_element_type=F32)
            acc = pv if acc is None else acc + pv
        o_ref[rows, :] = (acc[:, :LANES] / acc[:, LANES:LANES + 1]).astype(BF16)

    scores(0, 0)

    def body(k, carry):
        scores(2 * k + 1, 1)
        finish(2 * k, 0)
        scores(jnp.minimum(2 * k + 2, n_tiles - 1), 0)
        finish(2 * k + 1, 1)
        return carry

    lax.fori_loop(0, n_tiles // 2, body, 0)
    if n_tiles % 2:
        finish(n_tiles - 1, 0)


def _attention(qo, ko, vo, b, s, tc, tq, latent):
    n_x = b * s
    ctx0 = n_x // tc
    cspec = pl.BlockSpec((tc, MLA_SLAB), lambda i, h: (ctx0 + i, h))
    xspec = pl.BlockSpec((s, MLA_SLAB), lambda i, h: (i, h))
    if latent:
        n_q, n_keys = s, s + tc
        in_specs = [xspec, cspec, xspec, cspec, xspec]
        args = (qo, ko, ko, vo, vo)
    else:
        n_q, n_keys = tc, tc
        in_specs = [cspec, cspec, cspec]
        args = (qo, ko, vo)
    return pl.pallas_call(
        functools.partial(_attn_kernel, (len(args) - 1) // 2, tq),
        out_shape=jax.ShapeDtypeStruct((b * n_q, MLA_HEADS * LANES), BF16),
        grid=(b, MLA_HEADS),
        in_specs=in_specs,
        out_specs=pl.BlockSpec((n_q, LANES), lambda i, h: (i, h)),
        scratch_shapes=[pltpu.VMEM((tq, n_keys), F32), pltpu.VMEM((tq, n_keys), F32),
                        pltpu.VMEM((tq, LANES), F32), pltpu.VMEM((tq, LANES), F32)],
        compiler_params=_cparams(("parallel", "parallel")),
        name="attn_latent" if latent else "attn_ctx",
    )(*args)


S5_HALF = S5_BLOCK_GROUPS * S5_STATE


def _s5_kernel(n_batch, rows_x, rows_c, dot_rows, u_ref, fac_ref, pw_ref, tap_ref, tab_ref, y_ref,
               u2_ref, v_ref, r_ref, m_ref, ot_ref):
    d = pl.program_id(1)
    n_dot = u2_ref.shape[0] // dot_rows

    for s in range(S5_SUB):
        blk = slice(s * LANES, (s + 1) * LANES)
        r_ref[blk, :] = (fac_ref[0, 0, 0] * pw_ref[0, 0, 0, s:s + 1, :]
                         + fac_ref[0, 0, 1] * pw_ref[0, 0, 1, s:s + 1, :]).astype(BF16)
        ot_ref[blk, :] = (fac_ref[0, 0, 2] * pw_ref[0, 0, 2, s:s + 1, :]
                          + fac_ref[0, 0, 3] * pw_ref[0, 0, 3, s:s + 1, :]).astype(BF16)
        for t in range(S5_SUB):
            m_ref[blk, t * LANES:(t + 1) * LANES] = tap_ref[0, 0, t - s + S5_SUB - 1].astype(BF16)

    def dot_rows_of(i):
        return pl.ds(pl.multiple_of(i * dot_rows, dot_rows), dot_rows)

    def token_rows_of(i, s):
        return pl.ds(i * (dot_rows * S5_SUB) + s, dot_rows, stride=S5_SUB)

    @pl.when(d == 0)
    def _():
        def stage(i, carry):
            rows = pl.ds(pl.multiple_of(i * dot_rows, dot_rows), dot_rows)
            y_ref[rows, :] = u_ref[rows, :].astype(F32)
            return carry

        lax.fori_loop(0, n_dot * S5_SUB, stage, 0)

        def regroup(i, carry):
            for s in range(S5_SUB):
                u2_ref[dot_rows_of(i), s * LANES:(s + 1) * LANES] = y_ref[token_rows_of(i, s), :].astype(BF16)
            return carry

        lax.fori_loop(0, n_dot, regroup, 0)

    def increments(i, carry):
        rows = dot_rows_of(i)
        v_ref[rows, :] = jnp.dot(u2_ref[rows, :], r_ref[...], preferred_element_type=F32)
        return carry

    lax.fori_loop(0, n_dot, increments, 0)

    def cmul_add(ar, ai, cr, ci, xr, xi):
        return ar + cr * xr - ci * xi, ai + cr * xi + ci * xr

    def run(reverse):
        tab = tab_ref.at[0, 0]
        last = 0 if reverse else SUBLANES - 1
        first_row = lax.broadcasted_iota(jnp.int32, (SUBLANES, S5_HALF), 0) == (SUBLANES - 1 - last)

        def segment(bases, n_groups, carry):
            def body(i, carry):
                gi = (n_groups - 1 - i) if reverse else i
                out = []
                for base, (cre, cim) in zip(bases, carry):
                    rows = pl.ds(pl.multiple_of(base + gi * SUBLANES, SUBLANES), SUBLANES)
                    re = v_ref[rows, :S5_HALF]
                    im = v_ref[rows, S5_HALF:]
                    for lvl, shift in enumerate((1, 2, 4)):
                        sh = (SUBLANES - shift) if reverse else shift
                        re, im = cmul_add(re, im, tab[2 * lvl], tab[2 * lvl + 1],
                                          pltpu.roll(re, sh, axis=0), pltpu.roll(im, sh, axis=0))
                    re, im = cmul_add(re, im, tab[6], tab[7], cre, cim)
                    sh1 = (SUBLANES - 1) if reverse else 1
                    v_ref[rows, :S5_HALF] = jnp.where(first_row, cre, pltpu.roll(re, sh1, axis=0))
                    v_ref[rows, S5_HALF:] = jnp.where(first_row, cim, pltpu.roll(im, sh1, axis=0))
                    out.append((jnp.broadcast_to(re[last:last + 1, :], re.shape),
                                jnp.broadcast_to(im[last:last + 1, :], im.shape)))
                return tuple(out)
            return lax.fori_loop(0, n_groups, body, carry)

        zero = jnp.zeros((SUBLANES, S5_HALF), F32)
        carry = tuple((zero, zero) for _ in range(n_batch))
        carry = segment([n_batch * rows_x + bi * rows_c for bi in range(n_batch)], rows_c // SUBLANES, carry)
        segment([bi * rows_x for bi in range(n_batch)], rows_x // SUBLANES, carry)

        def outputs(i, carry):
            rows = dot_rows_of(i)
            y = jnp.dot(u2_ref[rows, :], m_ref[...], preferred_element_type=F32)
            y = y + lax.dot_general(v_ref[rows, :].astype(BF16), ot_ref[...], (((1,), (1,)), ((), ())),
                                    preferred_element_type=F32)
            for s in range(S5_SUB):
                part = y[:, s * LANES:(s + 1) * LANES]
                if reverse:
                    y_ref[token_rows_of(i, s), :] += part
                else:
                    y_ref[token_rows_of(i, s), :] = part
            return carry

        lax.fori_loop(0, n_dot, outputs, 0)

    @pl.when(d == 0)
    def _():
        run(False)

    @pl.when(d == 1)
    def _():
        run(True)


def _s5_scan(z, fac, pw, taps, tabs, b, s, tc, col_u):
    r = z.shape[0]
    rc = r // S5_SUB
    n_blocks = fac.shape[1]
    width = S5_SUB * LANES
    assert width == 2 * S5_HALF
    dot_rows = max(n for n in range(16, 641, 16) if rc % n == 0)
    return pl.pallas_call(
        functools.partial(_s5_kernel, b, s // S5_SUB, tc // S5_SUB, dot_rows),
        out_shape=jax.ShapeDtypeStruct((r, n_blocks * LANES), F32),
        grid=(n_blocks, 2),
        in_specs=[
            pl.BlockSpec((r, LANES), lambda cb, d: (0, col_u // LANES + cb), pipeline_mode=pl.Buffered(1)),
            pl.BlockSpec((1, 1, 4, LANES, width), lambda cb, d: (d, cb, 0, 0, 0), pipeline_mode=pl.Buffered(1)),
            pl.BlockSpec((1, 1, 4, S5_SUB, width), lambda cb, d: (d, cb, 0, 0, 0)),
            pl.BlockSpec((1, 1, 2 * S5_SUB - 1, LANES, LANES), lambda cb, d: (d, cb, 0, 0, 0)),
            pl.BlockSpec((1, 1, 8, SUBLANES, S5_HALF), lambda cb, d: (d, cb, 0, 0, 0)),
        ],
        out_specs=pl.BlockSpec((r, LANES), lambda cb, d: (0, cb)),
        scratch_shapes=[pltpu.VMEM((rc, width), BF16), pltpu.VMEM((rc, width), F32),
                        pltpu.VMEM((width, width), BF16), pltpu.VMEM((width, width), BF16),
                        pltpu.VMEM((width, width), BF16)],
        compiler_params=_cparams(("parallel", "arbitrary")),
        name="s5_scan",
    )(z, fac, pw, taps, tabs)


def _glu_kernel(y_ref, u_ref, d_ref, w_ref, b_ref, o_ref, g_ref):
    tm = y_ref.shape[0]

    def body(r, carry):
        rows = pl.ds(pl.multiple_of(r * ROW_CHUNK, ROW_CHUNK), ROW_CHUNK)
        y = y_ref[rows, :] + d_ref[...] * u_ref[rows, :].astype(F32)
        g_ref[rows, :] = jax.nn.gelu(y).astype(BF16)
        return carry

    lax.fori_loop(0, tm // ROW_CHUNK, body, 0)
    g = g_ref[...]
    gate = jax.nn.sigmoid(jnp.dot(g, w_ref[...], preferred_element_type=F32) + b_ref[...])
    o_ref[...] = (g.astype(F32) * gate).astype(BF16)


def _s5_glu(y, z, d_skip, w_glu, b_glu, tm, col_u, n_row_tiles):
    width = y.shape[1]
    return pl.pallas_call(
        _glu_kernel,
        out_shape=jax.ShapeDtypeStruct((n_row_tiles * tm, width), BF16),
        grid=(n_row_tiles,),
        in_specs=[
            pl.BlockSpec((tm, width), lambda i: (i, 0)),
            pl.BlockSpec((tm, width), lambda i: (i, col_u // width)),
            pl.BlockSpec((1, width), lambda i: (0, 0)),
            pl.BlockSpec((width, width), lambda i: (0, 0)),
            pl.BlockSpec((1, width), lambda i: (0, 0)),
        ],
        out_specs=pl.BlockSpec((tm, width), lambda i: (i, 0)),
        scratch_shapes=[pltpu.VMEM((tm, width), BF16)],
        compiler_params=_cparams(("parallel",)),
        name="s5_glu",
    )(y, z, d_skip, w_glu, b_glu)


def _merge_kernel(n_x_tiles, ax_ref, ac_ref, bx_ref, bc_ref, c_ref, ga_ref, gb_ref, gc_ref, w_ref, o_ref):
    def combine(a_ref, b_ref):
        acc = None
        for r, (br_ref, gate_ref) in enumerate(((a_ref, ga_ref), (b_ref, gb_ref), (c_ref, gc_ref))):
            proj = jnp.dot(br_ref[...], w_ref[r], preferred_element_type=F32)
            term = jax.nn.sigmoid(gate_ref[...].astype(F32)) * proj
            acc = term if acc is None else acc + term
        o_ref[...] = acc.astype(BF16)

    @pl.when(pl.program_id(0) < n_x_tiles)
    def _():
        combine(ax_ref, bx_ref)

    @pl.when(pl.program_id(0) >= n_x_tiles)
    def _():
        combine(ac_ref, bc_ref)


def _merge(a_x, a_c, b_x, b_c, cc, z, w_branch, tm, tn, col_g, n_row_tiles, n_x_tiles):
    width = a_x.shape[1]
    d = w_branch.shape[2]

    def gate_spec(r):
        return pl.BlockSpec((tm, tn), lambda i, j: (i, (col_g + r * d) // tn + j))

    ax_spec, ac_spec = _stream_specs((tm, width), n_x_tiles, 0, lambda j: 0)
    return pl.pallas_call(
        functools.partial(_merge_kernel, n_x_tiles),
        out_shape=jax.ShapeDtypeStruct((n_row_tiles * tm, d), BF16),
        grid=(n_row_tiles, d // tn),
        in_specs=[
            ax_spec, ac_spec, ax_spec, ac_spec,
            pl.BlockSpec((tm, width), lambda i, j: (i, 0)),
            gate_spec(0), gate_spec(1), gate_spec(2),
            pl.BlockSpec((N_BRANCH, width, tn), lambda i, j: (0, 0, j)),
        ],
        out_specs=pl.BlockSpec((tm, tn), lambda i, j: (i, j)),
        compiler_params=_cparams(("parallel", "arbitrary")),
        name="merge",
    )(a_x, a_c, b_x, b_c, cc, z, z, z, w_branch)


def _resid_kernel(m_ref, w_ref, x_ref, al_ref, o_ref):
    o_ref[...] = x_ref[...] + al_ref[0] * jnp.dot(m_ref[...], w_ref[...], preferred_element_type=F32)


def _out_proj_residual(m, w, xs, mod3, tm, tn, n_row_tiles, n_x_tiles, tiles_per_batch, ctx_row, k_alpha):
    kdim, d = w.shape
    nt = d // tn

    def alpha_map(i, j):
        r = jnp.where(i < n_x_tiles, i // tiles_per_batch, ctx_row)
        return (r * N_MOD + k_alpha, 0, j)

    return pl.pallas_call(
        _resid_kernel,
        out_shape=jax.ShapeDtypeStruct((n_row_tiles * tm, d), F32),
        grid=(n_row_tiles, nt),
        in_specs=[
            pl.BlockSpec((tm, kdim), lambda i, j: (i, 0)),
            pl.BlockSpec((kdim, tn), lambda i, j: (0, j)),
            pl.BlockSpec((tm, tn), lambda i, j: (i, j)),
            pl.BlockSpec((1, 1, tn), alpha_map),
        ],
        out_specs=pl.BlockSpec((tm, tn), lambda i, j: (i, j)),
        compiler_params=_cparams(("parallel", "arbitrary")),
        name="out_proj",
    )(m, w, xs, mod3)


def _ff1_kernel(x_ref, g_ref, sh_ref, sc_ref, w_ref, h_ref, xn_ref):
    @pl.when(pl.program_id(1) == 0)
    def _():
        _norm_mod_rows(x_ref, g_ref, sh_ref, sc_ref, xn_ref)

    a = jnp.maximum(jnp.dot(xn_ref[...], w_ref[...], preferred_element_type=F32), 0.0)
    h_ref[...] = (a * a).astype(BF16)


def _ff1(xs, g, mod3, w, tm, tn, n_row_tiles, n_x_tiles, tiles_per_batch, ctx_row):
    d, dff = w.shape
    return pl.pallas_call(
        _ff1_kernel,
        out_shape=jax.ShapeDtypeStruct((n_row_tiles * tm, dff), BF16),
        grid=(n_row_tiles, dff // tn),
        in_specs=[
            pl.BlockSpec((tm, d), lambda i, j: (i, 0)),
            pl.BlockSpec((1, d), lambda i, j: (0, 0)),
            pl.BlockSpec((1, 1, d), _mod_row_map(n_x_tiles, tiles_per_batch, ctx_row, 3)),
            pl.BlockSpec((1, 1, d), _mod_row_map(n_x_tiles, tiles_per_batch, ctx_row, 4)),
            pl.BlockSpec((d, tn), lambda i, j: (0, j)),
        ],
        out_specs=pl.BlockSpec((tm, tn), lambda i, j: (i, j)),
        scratch_shapes=[pltpu.VMEM((tm, d), BF16)],
        compiler_params=_cparams(("parallel", "arbitrary")),
        name="ff1",
    )(xs, g, mod3, mod3, w)


def _ff2_kernel(h_ref, w_ref, x_ref, al_ref, o_ref):
    k = pl.program_id(2)
    part = jnp.dot(h_ref[...], w_ref[...], preferred_element_type=F32)

    @pl.when(k == 0)
    def _():
        o_ref[...] = part

    @pl.when(k > 0)
    def _():
        o_ref[...] += part

    @pl.when(k == pl.num_programs(2) - 1)
    def _():
        o_ref[...] = x_ref[...] + al_ref[0] * o_ref[...]


def _ff2(h, w, xs, mod3, tm, tn, tk, n_row_tiles, n_x_tiles, tiles_per_batch, ctx_row):
    dff, d = w.shape

    def alpha_map(i, j, k):
        r = jnp.where(i < n_x_tiles, i // tiles_per_batch, ctx_row)
        return (r * N_MOD + 5, 0, j)

    return pl.pallas_call(
        _ff2_kernel,
        out_shape=jax.ShapeDtypeStruct((n_row_tiles * tm, d), F32),
        grid=(n_row_tiles, d // tn, dff // tk),
        in_specs=[
            pl.BlockSpec((tm, tk), lambda i, j, k: (i, k)),
            pl.BlockSpec((tk, tn), lambda i, j, k: (k, j)),
            pl.BlockSpec((tm, tn), lambda i, j, k: (i, j)),
            pl.BlockSpec((1, 1, tn), alpha_map),
        ],
        out_specs=pl.BlockSpec((tm, tn), lambda i, j, k: (i, j)),
        compiler_params=_cparams(("parallel", "parallel", "arbitrary")),
        name="ff2",
    )(h, w, xs, mod3)


def _rope_partner():
    j = np.arange(MLA_ROPE)
    quarter = MLA_ROPE // 4
    return np.where((j // quarter) % 2 == 0, j + quarter, j - quarter)


def _rope_table(s, tm):
    pos = jnp.arange(s)
    row = (pos // GRID_W).astype(F32)
    col = (pos % GRID_W).astype(F32)
    n_freq = MLA_ROPE // 4
    inv_freq = ROPE_THETA ** (-jnp.arange(n_freq, dtype=F32) / n_freq)
    ang_r = row[:, None] * inv_freq
    ang_c = col[:, None] * inv_freq
    cos = jnp.concatenate([jnp.cos(ang_r)] * 2 + [jnp.cos(ang_c)] * 2, axis=-1)
    sin = jnp.concatenate([-jnp.sin(ang_r), jnp.sin(ang_r), -jnp.sin(ang_c), jnp.sin(ang_c)], axis=-1)
    ident = jnp.concatenate([jnp.ones((tm, MLA_ROPE), F32), jnp.zeros((tm, MLA_ROPE), F32)], axis=-1)
    return jnp.concatenate([jnp.concatenate([cos, sin], axis=-1), ident], axis=0)


REPACK_ROWS = 256
REPACK_COLS = 1024


def _repack_kernel(runs, w_ref, tail_ref, o_ref):
    for out0, in0, width in runs:
        for c in range(0, width, REPACK_COLS):
            w = min(REPACK_COLS, width - c)
            src = in0 + c
            base = src - src % LANES
            win = w_ref[0, :, base:src + w]
            o_ref[:, out0 + c:out0 + c + w] = win[:, src - base:].astype(BF16)
    tail0 = o_ref.shape[1] - tail_ref.shape[1]
    o_ref[:, tail0:] = tail_ref[...]


def _pack_w_in(w_in_all, layer, b_in, gate_b, sizes, n_pad):
    w_in = w_in_all[layer]
    d_model = w_in.shape[0]
    bounds = np.cumsum((0,) + sizes)
    seg = [slice(int(bounds[i]), int(bounds[i + 1])) for i in range(len(sizes))]
    partner = _rope_partner()
    order = (0, 1, 2, 3, 5, 6, 8, 9)
    names = ("q", "k", "v", "o", "qa", "kva", "u", "gates")
    offs, runs, pos = {}, [], 0
    for name, i in zip(names, order):
        offs[name] = pos
        if runs and runs[-1][1] + runs[-1][2] == seg[i].start:
            runs[-1][2] += sizes[i]
        else:
            runs.append([pos, seg[i].start, sizes[i]])
        pos += sizes[i]
    assert all(r[0] % LANES == 0 for r in runs) and pos % LANES == 0
    offs["kpe"], offs["kpe_sw"] = pos, pos + MLA_ROPE
    kpe_w = w_in[:, seg[7]]
    tail = jnp.concatenate([kpe_w, kpe_w[:, partner], jnp.zeros((d_model, n_pad - pos - 2 * MLA_ROPE), F32)],
                           axis=1).astype(BF16)
    w = pl.pallas_call(
        functools.partial(_repack_kernel, tuple(tuple(r) for r in runs)),
        out_shape=jax.ShapeDtypeStruct((d_model, n_pad), BF16),
        grid=(d_model // REPACK_ROWS,),
        in_specs=[
            pl.BlockSpec((1, REPACK_ROWS, w_in.shape[1]), lambda i: (layer, i, 0)),
            pl.BlockSpec((REPACK_ROWS, tail.shape[1]), lambda i: (i, 0)),
        ],
        out_specs=pl.BlockSpec((REPACK_ROWS, n_pad), lambda i: (i, 0)),
        compiler_params=_cparams(("parallel",)),
        name="repack_w_in",
    )(w_in_all, tail)
    kpe_b = b_in[seg[7]]
    b = jnp.concatenate([b_in[seg[i]] for i in order] + [kpe_b, kpe_b[partner], jnp.zeros((n_pad - pos - 2 * MLA_ROPE,), F32)])
    n_g = sizes[4]
    wg = jnp.concatenate([w_in[:, seg[4]], jnp.zeros((d_model, LANES - n_g), F32)], axis=1).astype(BF16)
    bg = jnp.concatenate([b_in[seg[4]] + gate_b.reshape(-1), jnp.zeros((LANES - n_g,), F32)])[None, :]
    return w, b[None, :], wg, bg, offs


def _pack_mla(w_uq, w_ukv, qn_g, kn_g):
    partner = _rope_partner()
    lora = w_uq.shape[0]
    wq = w_uq.reshape(lora, MLA_HEADS, MLA_DQK)
    wq = jnp.concatenate([wq, wq[:, :, MLA_NOPE + partner]], axis=-1).reshape(lora, MLA_HEADS * MLA_SLAB)
    wkv = w_ukv.reshape(w_ukv.shape[0], MLA_HEADS, -1)
    wkv = jnp.concatenate([wkv[:, :, :MLA_NOPE].reshape(lora, -1), wkv[:, :, MLA_NOPE:].reshape(lora, -1)], axis=-1)

    def gains(g):
        return jnp.concatenate([g, g[MLA_NOPE + partner]])[None, :]

    return wq.astype(BF16), wkv.astype(BF16), gains(qn_g), gains(kn_g)


def _pack_s5(a_re, a_im, log_dt, b_re, b_im, c_re, c_im):
    n_dir, n_groups, n_state = a_re.shape
    gc = b_re.shape[-1]
    nb = n_groups // S5_BLOCK_GROUPS
    lam_re = jnp.minimum(a_re.astype(F32), -1e-4)
    lam_im = a_im.astype(F32)
    dt = jnp.exp(log_dt.astype(F32))[..., None]

    def pole_power(k):
        mag = jnp.exp(k * lam_re * dt)
        return mag * jnp.cos(k * lam_im * dt), mag * jnp.sin(k * lam_im * dt)

    bar_re, bar_im = pole_power(1.0)
    den = lam_re * lam_re + lam_im * lam_im
    f_re = ((bar_re - 1.0) * lam_re + bar_im * lam_im) / den
    f_im = (bar_im * lam_re - (bar_re - 1.0) * lam_im) / den
    bb_re = f_re[..., None] * b_re.astype(F32) - f_im[..., None] * b_im.astype(F32)
    bb_im = f_re[..., None] * b_im.astype(F32) + f_im[..., None] * b_re.astype(F32)
    eye = jnp.eye(S5_BLOCK_GROUPS, dtype=F32)
    sub = S5_SUB

    def per_block(a):
        return a.reshape(a.shape[:-2] + (nb, S5_BLOCK_GROUPS * n_state))

    def block_b(part):
        p = part.reshape(n_dir, nb, S5_BLOCK_GROUPS, n_state, gc)
        m = jnp.einsum('dbgnc,gh->dbgchn', p, eye, precision=HIGHEST)
        return m.reshape(n_dir, nb, LANES, S5_HALF)

    def block_c(part):
        p = part.astype(F32).reshape(n_dir, nb, S5_BLOCK_GROUPS, gc, n_state)
        m = jnp.einsum('dbgcn,gh->dbgnhc', p, eye, precision=HIGHEST)
        return m.reshape(n_dir, nb, S5_HALF, LANES)

    bm_re, bm_im = block_b(bb_re), block_b(bb_im)
    cm_re, cm_im = block_c(c_re), block_c(c_im)

    tau = jnp.arange(sub + 1, dtype=F32)[:, None, None, None]
    p_re, p_im = (per_block(p) for p in pole_power(tau))
    pr, pi = p_re[:sub, :, :, None, :], p_im[:sub, :, :, None, :]
    taps = (jnp.einsum('zdbkn,dbnc->zdbkc', bm_re * pr - bm_im * pi, cm_re, precision=HIGHEST)
            - jnp.einsum('zdbkn,dbnc->zdbkc', bm_re * pi + bm_im * pr, cm_im, precision=HIGHEST))

    ct_re, ct_im = jnp.swapaxes(cm_re, -1, -2), jnp.swapaxes(cm_im, -1, -2)
    cat = functools.partial(jnp.concatenate, axis=-1)
    fac = jnp.stack([cat([bm_re, bm_im]), cat([-bm_im, bm_re]), cat([ct_re, -ct_im]), cat([-ct_im, -ct_re])],
                    axis=2)
    t = np.arange(sub)
    zero_taps = jnp.zeros((sub - 1,) + taps.shape[2:], F32)
    pws, taps2 = [], []
    for d in range(n_dir):
        to_exit = (sub - 1 - t) if d == 0 else t
        age = (t + 1) if d == 0 else (sub - t)
        rows = [p_re[to_exit, d], p_im[to_exit, d], p_re[age, d], p_im[age, d]]
        pws.append(jnp.stack([cat([a, a]).transpose(1, 0, 2) for a in rows], axis=1))
        taps2.append(jnp.concatenate([zero_taps, taps[:, 0]] if d == 0 else [taps[::-1, 1], zero_taps], axis=0))
    pw = jnp.stack(pws)
    taps2 = jnp.stack(taps2).transpose(0, 2, 1, 3, 4)

    rows = jnp.arange(SUBLANES)
    tabs = []
    for d in range(n_dir):
        per_dir = []
        for shift in (1, 2, 4):
            keep = ((rows <= SUBLANES - 1 - shift) if d == 1 else (rows >= shift))[None, :, None]
            s_re, s_im = pole_power(float(shift * sub))
            per_dir += [jnp.where(keep, per_block(s_re[d])[:, None, :], 0.0),
                        jnp.where(keep, per_block(s_im[d])[:, None, :], 0.0)]
        expo = (((SUBLANES - rows) if d == 1 else (rows + 1)) * sub).astype(F32)
        s_re, s_im = pole_power(expo[:, None, None, None])
        per_dir += [jnp.moveaxis(per_block(s_re[:, d]), 0, 1), jnp.moveaxis(per_block(s_im[:, d]), 0, 1)]
        tabs.append(jnp.stack(per_dir, axis=1))
    return fac, pw, taps2, jnp.stack(tabs).astype(F32)


def _tri_matrices():
    t = np.arange(ML_CHUNK)
    lower = t[None, :] <= t[:, None]
    mask = np.where(np.stack([lower, lower.T]), 0.0, NEG_BIG).astype(np.float32)
    return jnp.asarray(lower.astype(np.float32), dtype=BF16), jnp.asarray(mask)


def kernel(x, c, ctx, c_ctx, w_mod, b_mod, norm_g, w_in, b_in, ml_gate_b, ml_norm_g, mla_qa_g, mla_kva_g, mla_w_uq, mla_w_ukv, mla_qn_g, mla_kn_g, s5_a_re, s5_a_im, s5_log_dt, s5_b_re, s5_b_im, s5_c_re, s5_c_im, s5_d, s5_w_glu, s5_b_glu, w_branch, w_out, w_ff1, w_ff2):
    b, s, d = x.shape
    tc = ctx.shape[1]
    depth = w_mod.shape[0]
    dv = ml_norm_g.shape[2]
    dk = dv // 2
    lora = mla_qa_g.shape[1]
    s5_width = s5_d.shape[1]
    branch_w = w_branch.shape[2]
    sizes = (ML_HEADS * dk, ML_HEADS * dk, ML_HEADS * dv, ML_HEADS * dv, 4 * ML_HEADS, lora, lora, MLA_ROPE,
             s5_width, N_BRANCH * d)
    assert sum(sizes) == w_in.shape[2] and b + 1 <= SUBLANES
    assert s % ML_CHUNK == 0 and tc % ML_CHUNK == 0 and branch_w == ML_HEADS * dv == MLA_HEADS * LANES == s5_width

    n_x = b * s
    n_c = b * tc
    tm = _row_tile(s, n_c)
    n_x_tiles = n_x // tm
    n_tiles = n_x_tiles + n_c // tm
    tiles_per_batch = s // tm
    tile_args = (n_x_tiles, tiles_per_batch, b)

    xs = jnp.concatenate([x.reshape(n_x, d), ctx.reshape(n_c, d)], axis=0)
    cc = jnp.concatenate([c, c_ctx[None, :], jnp.zeros((SUBLANES - b - 1, d), F32)], axis=0)
    mod = _modulation(cc, w_mod, b_mod)
    tm_q = min(tm, 512)
    tab = _rope_table(s, tm_q)
    tri, ml_mask = _tri_matrices()
    n_used = sum(sizes) - sizes[4] + MLA_ROPE
    tn_in = 1280
    n_pad = -(-n_used // tn_in) * tn_in

    for l in range(depth):
        with_ctx_out = l < depth - 1
        mod3 = mod[l].reshape(SUBLANES * N_MOD, 1, d)
        w_p, b_p, wg, bg, offs = _pack_w_in(w_in, l, b_in[l], ml_gate_b[l], sizes, n_pad)
        z, gz = _in_proj(xs, norm_g[l, 0][None, :], mod3, w_p, b_p, wg, bg, tm, tn_in, *tile_args)

        a_x, a_c = _mlstm(z, gz, tri, ml_mask, ml_norm_g[l], b, s, tc, dk, dv,
                          (offs["q"], offs["k"], offs["v"], offs["o"]), with_ctx_out)

        wq, wkv, gq, gk = _pack_mla(mla_w_uq[l], mla_w_ukv[l], mla_qn_g[l], mla_kn_g[l])
        qo, ko, vo = _mla_proj(z, tab, mla_qa_g[l][None, :], mla_kva_g[l][None, :], wq, wkv, gq, gk, tm_q,
                               (offs["qa"], offs["kva"], offs["kpe"]), n_x // tm_q, s // tm_q)
        tq = min(512, s)
        b_x = _attention(qo, ko, vo, b, s, tc, tq, True)

        fac, pw, taps, tabs = _pack_s5(s5_a_re[l], s5_a_im[l], s5_log_dt[l], s5_b_re[l], s5_b_im[l],
                                       s5_c_re[l], s5_c_im[l])
        y = _s5_scan(z, fac, pw, taps, tabs, b, s, tc, offs["u"])
        n_out_tiles = n_tiles if with_ctx_out else n_x_tiles
        c_all = _s5_glu(y, z, s5_d[l][None, :], s5_w_glu[l].astype(BF16), s5_b_glu[l][None, :], tm, offs["u"],
                        n_out_tiles)

        if with_ctx_out:
            b_c = _attention(qo, ko, vo, b, s, tc, min(tq, tc), False)
        else:
            a_c, b_c = a_x, b_x
        merged = _merge(a_x, a_c, b_x, b_c, c_all, z, w_branch[l].astype(BF16), tm, 512, offs["gates"],
                        n_out_tiles, n_x_tiles)
        xs1 = _out_proj_residual(merged, w_out[l].astype(BF16), xs, mod3, tm, 1024, n_out_tiles, *tile_args, 2)
        hid = _ff1(xs1, norm_g[l, 1][None, :], mod3, w_ff1[l].astype(BF16), tm, 1024, n_out_tiles, *tile_args)
        xs = _ff2(hid, w_ff2[l].astype(BF16), xs1, mod3, tm, 1024, 2048, n_out_tiles, *tile_args)

    return xs.reshape(b, s, d)
```

```python
import functools
import math

import jax
import jax.numpy as jnp
import numpy as np
from jax import lax
from jax.experimental import pallas as pl
from jax.experimental.pallas import tpu as pltpu

F32 = jnp.float32
BF16 = jnp.bfloat16
HIGHEST = lax.Precision.HIGHEST

N_MOD = 6
N_BRANCH = 3
ML_HEADS = 4
MLA_HEADS = 8
MLA_NOPE = 128
MLA_ROPE = 64
MLA_DQK = MLA_NOPE + MLA_ROPE
MLA_SLAB = 256
GRID_W = 64
ROPE_THETA = 10000.0
S5_GROUP = 16
S5_STATE = 64
S5_BLOCK_GROUPS = 8
EPS = 1e-6
NEG_BIG = -1e30

LANES = 128
SUBLANES = 8
VMEM_LIMIT = 56 * 1024 * 1024

ML_CHUNK = 256
ML_GATE_ROWS = 32
S5_SUB = 8
ATTN_KEY_CHUNK = 512
Q_PRESCALE = MLA_DQK ** -0.5 * math.log2(math.e)
ROW_CHUNK = 64


def _cparams(sem):
    return pltpu.CompilerParams(dimension_semantics=sem, vmem_limit_bytes=VMEM_LIMIT)


def _row_tile(n_x_rows_per_batch, n_ctx_rows):
    tm = 1024
    while n_x_rows_per_batch % tm or n_ctx_rows % tm:
        tm //= 2
    return tm


def _mod_kernel(c_ref, w_ref, b_ref, o_ref):
    s = c_ref[...]
    s = s * jax.nn.sigmoid(s)
    o_ref[0] = jnp.dot(s.astype(BF16), w_ref[0].astype(BF16), preferred_element_type=F32) + b_ref[0]


def _modulation(cc, w_mod, b_mod):
    n_layers, d, n = w_mod.shape
    tn = 1024
    return pl.pallas_call(
        _mod_kernel,
        out_shape=jax.ShapeDtypeStruct((n_layers, SUBLANES, n), F32),
        grid=(n_layers, n // tn),
        in_specs=[
            pl.BlockSpec((SUBLANES, d), lambda l, j: (0, 0)),
            pl.BlockSpec((1, d, tn), lambda l, j: (l, 0, j)),
            pl.BlockSpec((1, 1, tn), lambda l, j: (l, 0, j)),
        ],
        out_specs=pl.BlockSpec((1, SUBLANES, tn), lambda l, j: (l, 0, j)),
        compiler_params=_cparams(("parallel", "parallel")),
        name="adaln_mod",
    )(cc, w_mod, b_mod.reshape(n_layers, 1, n))


def _norm_mod_rows(x_ref, g_ref, sh_ref, sc_ref, xn_ref):
    tm = x_ref.shape[0]
    g = g_ref[...]
    sc = 1.0 + sc_ref[0]
    sh = sh_ref[0]

    def body(r, carry):
        rows = pl.ds(pl.multiple_of(r * ROW_CHUNK, ROW_CHUNK), ROW_CHUNK)
        x = x_ref[rows, :]
        ms = jnp.mean(x * x, axis=-1, keepdims=True)
        y = x * lax.rsqrt(ms + EPS) * g
        xn_ref[rows, :] = (y * sc + sh).astype(BF16)
        return carry

    lax.fori_loop(0, tm // ROW_CHUNK, body, 0)


def _mod_row_map(n_x_tiles, tiles_per_batch, ctx_row, k):
    def index_map(i, j):
        r = jnp.where(i < n_x_tiles, i // tiles_per_batch, ctx_row)
        return (r * N_MOD + k, 0, 0)
    return index_map


def _stream_specs(block, n_x_tiles, ctx_tile0, col_map):
    x_spec = pl.BlockSpec(block, lambda i, *r: (jnp.minimum(i, n_x_tiles - 1), col_map(*r)))
    c_spec = pl.BlockSpec(block, lambda i, *r: (ctx_tile0 + jnp.maximum(i - n_x_tiles, 0), col_map(*r)),
                          pipeline_mode=pl.Buffered(1))
    return x_spec, c_spec


def _in_kernel(x_ref, g_ref, sh_ref, sc_ref, w_ref, b_ref, wg_ref, bg_ref, z_ref, gz_ref, xn_ref):
    @pl.when(pl.program_id(1) == 0)
    def _():
        _norm_mod_rows(x_ref, g_ref, sh_ref, sc_ref, xn_ref)
        gz_ref[...] = jnp.dot(xn_ref[...], wg_ref[...], preferred_element_type=F32) + bg_ref[...]

    z_ref[...] = (jnp.dot(xn_ref[...], w_ref[...], preferred_element_type=F32) + b_ref[...]).astype(BF16)


def _in_proj(xs, g, mod3, w, b, wg, bg, tm, tn, n_x_tiles, tiles_per_batch, ctx_row):
    r, d = xs.shape
    nz = w.shape[1]
    return pl.pallas_call(
        _in_kernel,
        out_shape=(jax.ShapeDtypeStruct((r, nz), BF16), jax.ShapeDtypeStruct((r, LANES), F32)),
        grid=(r // tm, nz // tn),
        in_specs=[
            pl.BlockSpec((tm, d), lambda i, j: (i, 0)),
            pl.BlockSpec((1, d), lambda i, j: (0, 0)),
            pl.BlockSpec((1, 1, d), _mod_row_map(n_x_tiles, tiles_per_batch, ctx_row, 0)),
            pl.BlockSpec((1, 1, d), _mod_row_map(n_x_tiles, tiles_per_batch, ctx_row, 1)),
            pl.BlockSpec((d, tn), lambda i, j: (0, j)),
            pl.BlockSpec((1, tn), lambda i, j: (0, j)),
            pl.BlockSpec((d, LANES), lambda i, j: (0, 0)),
            pl.BlockSpec((1, LANES), lambda i, j: (0, 0)),
        ],
        out_specs=(
            pl.BlockSpec((tm, tn), lambda i, j: (i, j)),
            pl.BlockSpec((tm, LANES), lambda i, j: (i, 0)),
        ),
        scratch_shapes=[pltpu.VMEM((tm, d), BF16)],
        compiler_params=_cparams(("parallel", "arbitrary")),
        name="in_proj",
    )(xs, g, mod3, mod3, w, b, wg, bg)


def _log_sigmoid(x):
    return jnp.minimum(x, 0.0) - jnp.log1p(jnp.exp(-jnp.abs(x)))


def _split3(a):
    hi = a.astype(BF16)
    r1 = a - hi.astype(F32)
    mid = r1.astype(BF16)
    lo = (r1 - mid.astype(F32)).astype(BF16)
    return hi, mid, lo


def _gate_prep_kernel(g_ref, tri_ref, a_ref, at_ref):
    g = g_ref[...]
    kind = lax.broadcasted_iota(jnp.int32, g.shape, 1) // ML_HEADS
    lg = jnp.where((kind == 1) | (kind == 3), _log_sigmoid(g), g)
    pre = sum(jnp.dot(tri_ref[...], p, preferred_element_type=F32) for p in _split3(lg))
    suf = pre[ML_CHUNK - 1:ML_CHUNK, :] - pre + lg
    a = jnp.where(kind == 4, pltpu.roll(pre, 3 * ML_HEADS, axis=1),
                  jnp.where(kind == 5, pltpu.roll(suf, 2 * ML_HEADS, axis=1), lg))
    a_ref[...] = a
    at_ref[0] = a.T[:ML_GATE_ROWS, :]


def _gate_prep(gz, tri):
    r = gz.shape[0]
    return pl.pallas_call(
        _gate_prep_kernel,
        out_shape=(jax.ShapeDtypeStruct((r, LANES), F32),
                   jax.ShapeDtypeStruct((r // ML_CHUNK, ML_GATE_ROWS, ML_CHUNK), F32)),
        grid=(r // ML_CHUNK,),
        in_specs=[pl.BlockSpec((ML_CHUNK, LANES), lambda i: (i, 0)),
                  pl.BlockSpec((ML_CHUNK, ML_CHUNK), lambda i: (0, 0))],
        out_specs=(pl.BlockSpec((ML_CHUNK, LANES), lambda i: (i, 0)),
                   pl.BlockSpec((1, ML_GATE_ROWS, ML_CHUNK), lambda i: (i, 0, 0))),
        compiler_params=_cparams(("parallel",)),
        name="mlstm_gates",
    )(gz, tri)


def _ml_chunk(q, k_t, v, cum_cb, li_r, lf_r, cum_r, mask_add, carry, inv_scale):
    c_mat, n_mat, m = carry
    length, dv = v.shape
    crow = cum_r - li_r
    total = jnp.sum(lf_r, axis=-1, keepdims=True)
    ones = jnp.ones((length, LANES), BF16)
    n_blk = length // LANES

    def wide(a):
        return jnp.concatenate([a] * (dv // LANES), axis=1)

    log_w = [cum_cb - crow[:, j * LANES:(j + 1) * LANES] + mask_add[:, j * LANES:(j + 1) * LANES]
             for j in range(n_blk)]
    row_max = jnp.max(functools.reduce(jnp.maximum, log_w), axis=-1, keepdims=True)
    log_inter = cum_cb + m
    m_t = jnp.maximum(log_inter, row_max)
    w_inter = jnp.exp(log_inter - m_t)
    qk = jnp.dot(q, k_t, preferred_element_type=F32)
    s = jnp.concatenate([qk[:, j * LANES:(j + 1) * LANES] * jnp.exp(log_w[j] - m_t) for j in range(n_blk)],
                        axis=1).astype(BF16)
    num = wide(w_inter) * jnp.dot(q, c_mat.astype(BF16), preferred_element_type=F32)
    num = num + jnp.dot(s, v, preferred_element_type=F32)
    den = w_inter * jnp.dot(q, n_mat.astype(BF16), preferred_element_type=F32)
    den = den + jnp.dot(s, ones, preferred_element_type=F32)
    h = num * wide(1.0 / jnp.maximum(jnp.abs(den), jnp.exp(-m_t) * inv_scale))

    log_end = total - crow
    m_new = jnp.maximum(total + m, jnp.max(log_end, axis=-1, keepdims=True))
    decay = jnp.exp(total + m - m_new)
    kw_t = (k_t.astype(F32) * jnp.exp(log_end - m_new)).astype(BF16)
    c_new = decay * c_mat + jnp.dot(kw_t, v, preferred_element_type=F32)
    n_new = decay * n_mat + jnp.dot(kw_t, ones, preferred_element_type=F32)
    return h, (c_new, n_new, m_new)


def _mlstm_kernel(with_ctx_out, qx, kx, vx, ox, qc, kc, vc, oc, gx, gc, gtx, gtc, mask_ref, ng_ref, *rest):
    if with_ctx_out:
        ax_ref, ac_ref, cumx, ktx, cumc, ktc, hx, hc = rest
    else:
        ax_ref, cumx, ktx, cumc, ktc, hx, hc = rest
        ac_ref = None
    head = pl.program_id(1)
    dk = qx.shape[1]
    inv_scale = float(dk) ** 0.5
    n_x_chunks = qx.shape[0] // ML_CHUNK
    n_c_chunks = qc.shape[0] // ML_CHUNK

    def prep(g_ref, k_ref, cum_ref, kt_ref):
        def body(i, carry):
            rows = pl.ds(pl.multiple_of(i * ML_CHUNK, ML_CHUNK), ML_CHUNK)
            a = g_ref[rows, :]
            col = lax.broadcasted_iota(jnp.int32, a.shape, 1)
            for d in range(2):
                pick = jnp.sum(jnp.where(col == (4 + d) * ML_HEADS + head, a, 0.0), axis=-1, keepdims=True)
                cum_ref[d, rows, :] = jnp.broadcast_to(pick, a.shape)
            kt_ref[i] = k_ref[rows, :].astype(F32).T.astype(BF16)
            return carry
        n_chunks = g_ref.shape[0] // ML_CHUNK
        lax.fori_loop(0, n_chunks, body, 0, unroll=2 if n_chunks % 2 == 0 else 1)

    prep(gx, kx, cumx, ktx)
    prep(gc, kc, cumc, ktc)

    for d in range(2):
        reverse = d == 1
        mask_add = mask_ref[d]

        def step(q_ref, v_ref, cum_ref, gt_ref, kt_ref, h_ref, ci, carry):
            rows = pl.ds(pl.multiple_of(ci * ML_CHUNK, ML_CHUNK), ML_CHUNK)
            li_r, lf_r, cum_r = (gt_ref[ci, pl.ds(kind * ML_HEADS + head, 1), :] for kind in (2 * d, 2 * d + 1, 4 + d))
            h, carry = _ml_chunk(q_ref[rows, :], kt_ref[ci], v_ref[rows, :], cum_ref[d, rows, :], li_r, lf_r, cum_r,
                                 mask_add, carry, inv_scale)
            if reverse:
                h_ref[rows, :] += h
            else:
                h_ref[rows, :] = h
            return carry

        carry = (jnp.zeros((dk, vx.shape[1]), F32), jnp.zeros((dk, LANES), F32), jnp.zeros((1, 1), F32))

        def ctx_body(i, carry):
            ci = (n_c_chunks - 1 - i) if reverse else i
            return step(qc, vc, cumc, gtc, ktc, hc, ci, carry)

        def x_body(i, carry):
            ci = (n_x_chunks - 1 - i) if reverse else i
            return step(qx, vx, cumx, gtx, ktx, hx, ci, carry)

        carry = lax.fori_loop(0, n_c_chunks, ctx_body, carry)
        lax.fori_loop(0, n_x_chunks, x_body, carry, unroll=2 if n_x_chunks % 2 == 0 else 1)

    ng = ng_ref[0]

    def finish(h_ref, o_ref, a_ref):
        def body(i, carry):
            rows = pl.ds(pl.multiple_of(i * ML_CHUNK, ML_CHUNK), ML_CHUNK)
            h = h_ref[rows, :]
            hn = h * lax.rsqrt(jnp.mean(h * h, axis=-1, keepdims=True) + EPS) * ng
            a_ref[rows, :] = (hn * jax.nn.sigmoid(o_ref[rows, :].astype(F32))).astype(BF16)
            return carry
        lax.fori_loop(0, h_ref.shape[0] // ML_CHUNK, body, 0)

    finish(hx, ox, ax_ref)
    if with_ctx_out:
        finish(hc, oc, ac_ref)


def _mlstm(z, gates, gates_t, mask, ml_norm_g, b, s, tc, dk, dv, cols, with_ctx_out):
    n_x = b * s
    cq, ck, cv, co = cols
    ctx0 = n_x // tc
    xc, cc = s // ML_CHUNK, tc // ML_CHUNK

    def xspec(width, col0):
        return pl.BlockSpec((s, width), lambda i, h: (i, col0 // width + h))

    def cspec(width, col0):
        return pl.BlockSpec((tc, width), lambda i, h: (ctx0 + i, col0 // width + h))

    out_shape = [jax.ShapeDtypeStruct((n_x, ML_HEADS * dv), BF16)]
    out_specs = [pl.BlockSpec((s, dv), lambda i, h: (i, h))]
    if with_ctx_out:
        out_shape.append(jax.ShapeDtypeStruct((b * tc, ML_HEADS * dv), BF16))
        out_specs.append(pl.BlockSpec((tc, dv), lambda i, h: (i, h)))
    res = pl.pallas_call(
        functools.partial(_mlstm_kernel, with_ctx_out),
        out_shape=tuple(out_shape),
        grid=(b, ML_HEADS),
        in_specs=[
            xspec(dk, cq), xspec(dk, ck), xspec(dv, cv), xspec(dv, co),
            cspec(dk, cq), cspec(dk, ck), cspec(dv, cv), cspec(dv, co),
            pl.BlockSpec((s, LANES), lambda i, h: (i, 0)),
            pl.BlockSpec((tc, LANES), lambda i, h: (ctx0 + i, 0)),
            pl.BlockSpec((xc, ML_GATE_ROWS, ML_CHUNK), lambda i, h: (i, 0, 0)),
            pl.BlockSpec((cc, ML_GATE_ROWS, ML_CHUNK), lambda i, h: (b * xc // cc + i, 0, 0)),
            pl.BlockSpec((2, ML_CHUNK, ML_CHUNK), lambda i, h: (0, 0, 0)),
            pl.BlockSpec((1, 1, dv), lambda i, h: (h, 0, 0)),
        ],
        out_specs=tuple(out_specs),
        scratch_shapes=[
            pltpu.VMEM((2, s, LANES), F32), pltpu.VMEM((xc, dk, ML_CHUNK), BF16),
            pltpu.VMEM((2, tc, LANES), F32), pltpu.VMEM((cc, dk, ML_CHUNK), BF16),
            pltpu.VMEM((s, dv), F32), pltpu.VMEM((tc, dv), F32),
        ],
        compiler_params=_cparams(("parallel", "parallel")),
        name="mlstm",
    )(z, z, z, z, z, z, z, z, gates, gates, gates_t, gates_t, mask, ml_norm_g.reshape(ML_HEADS, 1, dv))
    return res if with_ctx_out else (res[0], None)


def _mla_proj_kernel(qa_ref, kva_ref, kpe_ref, tab_ref, qag_ref, kvag_ref, wq_ref, wkv_ref,
                     gq_ref, gk_ref, q_ref, k_ref, v_ref):
    def normed(a_ref, g_ref):
        a = a_ref[...].astype(F32)
        return (a * lax.rsqrt(jnp.mean(a * a, axis=-1, keepdims=True) + EPS) * g_ref[...]).astype(BF16)

    q_all = jnp.dot(normed(qa_ref, qag_ref), wq_ref[...], preferred_element_type=F32)
    kv_all = jnp.dot(normed(kva_ref, kvag_ref), wkv_ref[...], preferred_element_type=F32)
    tab = tab_ref[...]
    lane = lax.broadcasted_iota(jnp.int32, tab.shape, 1)
    first_half = lane < MLA_ROPE
    gq = gq_ref[...]
    gk = gk_ref[...]
    inv_dqk = 1.0 / MLA_DQK

    kpe = kpe_ref[...].astype(F32)
    ss_kpe = jnp.sum(jnp.where(first_half, kpe * kpe, 0.0), axis=-1, keepdims=True)
    kpe_t = kpe * (tab * gk[:, LANES:])
    kpe_rot = jnp.where(first_half, kpe_t + pltpu.roll(kpe_t, MLA_ROPE, axis=1), 0.0)

    for h in range(MLA_HEADS):
        qn = q_all[:, h * MLA_SLAB:h * MLA_SLAB + LANES]
        qp = q_all[:, h * MLA_SLAB + LANES:(h + 1) * MLA_SLAB]
        ss = jnp.sum(qn * qn, axis=-1, keepdims=True) + jnp.sum(jnp.where(first_half, qp * qp, 0.0), axis=-1,
                                                                keepdims=True)
        r = lax.rsqrt(ss * inv_dqk + EPS) * Q_PRESCALE
        qp_t = qp * (tab * gq[:, LANES:])
        qp_rot = qp_t + pltpu.roll(qp_t, MLA_ROPE, axis=1)
        q_ref[:, h * MLA_SLAB:h * MLA_SLAB + LANES] = (qn * r * gq[:, :LANES]).astype(BF16)
        q_ref[:, h * MLA_SLAB + LANES:(h + 1) * MLA_SLAB] = (qp_rot * r).astype(BF16)

        kn = kv_all[:, h * LANES:(h + 1) * LANES]
        rk = lax.rsqrt((jnp.sum(kn * kn, axis=-1, keepdims=True) + ss_kpe) * inv_dqk + EPS)
        k_ref[:, h * MLA_SLAB:h * MLA_SLAB + LANES] = (kn * rk * gk[:, :LANES]).astype(BF16)
        k_ref[:, h * MLA_SLAB + LANES:(h + 1) * MLA_SLAB] = (kpe_rot * rk).astype(BF16)

    ones_col = jnp.where(lane == 0, 1.0, 0.0).astype(BF16)
    for h in range(MLA_HEADS):
        v_ref[:, h * MLA_SLAB:h * MLA_SLAB + LANES] = kv_all[:, (MLA_HEADS + h) * LANES:(MLA_HEADS + h + 1) * LANES
                                                             ].astype(BF16)
        v_ref[:, h * MLA_SLAB + LANES:(h + 1) * MLA_SLAB] = ones_col


def _mla_proj(z, tab, qag, kvag, wq, wkv, gq, gk, tm, cols, n_x_tiles, tab_tiles):
    r = z.shape[0]
    cqa, ckva, ckpe = cols
    lora = qag.shape[1]
    hs = MLA_HEADS * MLA_SLAB
    hv = MLA_HEADS * LANES
    return pl.pallas_call(
        _mla_proj_kernel,
        out_shape=(jax.ShapeDtypeStruct((r, hs), BF16), jax.ShapeDtypeStruct((r, hs), BF16),
                   jax.ShapeDtypeStruct((r, hs), BF16)),
        grid=(r // tm,),
        in_specs=[
            pl.BlockSpec((tm, lora), lambda i: (i, cqa // lora)),
            pl.BlockSpec((tm, lora), lambda i: (i, ckva // lora)),
            pl.BlockSpec((tm, LANES), lambda i: (i, ckpe // LANES)),
            pl.BlockSpec((tm, LANES), lambda i: (jnp.where(i < n_x_tiles, i % tab_tiles, tab_tiles), 0)),
            pl.BlockSpec((1, lora), lambda i: (0, 0)),
            pl.BlockSpec((1, lora), lambda i: (0, 0)),
            pl.BlockSpec((lora, hs), lambda i: (0, 0)),
            pl.BlockSpec((lora, 2 * hv), lambda i: (0, 0)),
            pl.BlockSpec((1, MLA_SLAB), lambda i: (0, 0)),
            pl.BlockSpec((1, MLA_SLAB), lambda i: (0, 0)),
        ],
        out_specs=(pl.BlockSpec((tm, hs), lambda i: (i, 0)), pl.BlockSpec((tm, hs), lambda i: (i, 0)),
                   pl.BlockSpec((tm, hs), lambda i: (i, 0))),
        compiler_params=_cparams(("parallel",)),
        name="mla_qkv",
    )(z, z, z, tab, qag, kvag, wq, wkv, gq, gk)


def _attn_kernel(n_kv, tq, q_ref, *refs):
    k_refs = refs[:n_kv]
    v_refs = refs[n_kv:2 * n_kv]
    o_ref, s0_ref, s1_ref, m0_ref, m1_ref = refs[2 * n_kv:]
    slots = ((s0_ref, m0_ref), (s1_ref, m1_ref))
    n_tiles = q_ref.shape[0] // tq
    chunks = []
    col = 0
    for kv, k_ref in enumerate(k_refs):
        n_keys = k_ref.shape[0]
        step = min(ATTN_KEY_CHUNK, n_keys)
        for off in range(0, n_keys, step):
            chunks.append((kv, off, col, step))
            col += step

    def scores(t, slot):
        s_ref, m_ref = slots[slot]
        rows = pl.ds(pl.multiple_of(t * tq, tq), tq)
        q = q_ref[rows, :]
        run = None
        for kv, off, c0, size in chunks:
            s = lax.dot_general(q, k_refs[kv][off:off + size, :], (((1,), (1,)), ((), ())),
                                preferred_element_type=F32)
            s_ref[:, c0:c0 + size] = s
            for lb in range(size // LANES):
                blk = s[:, lb * LANES:(lb + 1) * LANES]
                run = blk if run is None else jnp.maximum(run, blk)
        m_ref[...] = run

    def finish(t, slot):
        s_ref, m_ref = slots[slot]
        rows = pl.ds(pl.multiple_of(t * tq, tq), tq)
        m = jnp.max(m_ref[...], axis=-1, keepdims=True)
        acc = None
        for kv, off, c0, size in chunks:
            p = jnp.exp2(s_ref[:, c0:c0 + size] - m).astype(BF16)
            pv = jnp.dot(p, v_refs[kv][off:off + size, :], preferred_element_type=F32)
            acc = pv if acc is None else acc + pv
        o_ref[rows, :] = (acc[:, :LANES] / acc[:, LANES:LANES + 1]).astype(BF16)

    scores(0, 0)

    def body(k, carry):
        scores(2 * k + 1, 1)
        finish(2 * k, 0)
        scores(jnp.minimum(2 * k + 2, n_tiles - 1), 0)
        finish(2 * k + 1, 1)
        return carry

    lax.fori_loop(0, n_tiles // 2, body, 0)
    if n_tiles % 2:
        finish(n_tiles - 1, 0)


def _attention(qo, ko, vo, b, s, tc, tq, latent):
    n_x = b * s
    ctx0 = n_x // tc
    cspec = pl.BlockSpec((tc, MLA_SLAB), lambda i, h: (ctx0 + i, h))
    xspec = pl.BlockSpec((s, MLA_SLAB), lambda i, h: (i, h))
    if latent:
        n_q, n_keys = s, s + tc
        in_specs = [xspec, cspec, xspec, cspec, xspec]
        args = (qo, ko, ko, vo, vo)
    else:
        n_q, n_keys = tc, tc
        in_specs = [cspec, cspec, cspec]
        args = (qo, ko, vo)
    return pl.pallas_call(
        functools.partial(_attn_kernel, (len(args) - 1) // 2, tq),
        out_shape=jax.ShapeDtypeStruct((b * n_q, MLA_HEADS * LANES), BF16),
        grid=(b, MLA_HEADS),
        in_specs=in_specs,
        out_specs=pl.BlockSpec((n_q, LANES), lambda i, h: (i, h)),
        scratch_shapes=[pltpu.VMEM((tq, n_keys), F32), pltpu.VMEM((tq, n_keys), F32),
                        pltpu.VMEM((tq, LANES), F32), pltpu.VMEM((tq, LANES), F32)],
        compiler_params=_cparams(("parallel", "parallel")),
        name="attn_latent" if latent else "attn_ctx",
    )(*args)


S5_HALF = S5_BLOCK_GROUPS * S5_STATE


def _s5_kernel(n_batch, rows_x, rows_c, dot_rows, u_ref, fac_ref, pw_ref, tap_ref, tab_ref, y_ref,
               u2_ref, v_ref, r_ref, m_ref, ot_ref):
    d = pl.program_id(1)
    n_dot = u2_ref.shape[0] // dot_rows

    for s in range(S5_SUB):
        blk = slice(s * LANES, (s + 1) * LANES)
        r_ref[blk, :] = (fac_ref[0, 0, 0] * pw_ref[0, 0, 0, s:s + 1, :]
                         + fac_ref[0, 0, 1] * pw_ref[0, 0, 1, s:s + 1, :]).astype(BF16)
        ot_ref[blk, :] = (fac_ref[0, 0, 2] * pw_ref[0, 0, 2, s:s + 1, :]
                          + fac_ref[0, 0, 3] * pw_ref[0, 0, 3, s:s + 1, :]).astype(BF16)
        for t in range(S5_SUB):
            m_ref[blk, t * LANES:(t + 1) * LANES] = tap_ref[0, 0, t - s + S5_SUB - 1].astype(BF16)

    def dot_rows_of(i):
        return pl.ds(pl.multiple_of(i * dot_rows, dot_rows), dot_rows)

    def token_rows_of(i, s):
        return pl.ds(i * (dot_rows * S5_SUB) + s, dot_rows, stride=S5_SUB)

    @pl.when(d == 0)
    def _():
        def stage(i, carry):
            rows = pl.ds(pl.multiple_of(i * dot_rows, dot_rows), dot_rows)
            y_ref[rows, :] = u_ref[rows, :].astype(F32)
            return carry

        lax.fori_loop(0, n_dot * S5_SUB, stage, 0)

        def regroup(i, carry):
            for s in range(S5_SUB):
                u2_ref[dot_rows_of(i), s * LANES:(s + 1) * LANES] = y_ref[token_rows_of(i, s), :].astype(BF16)
            return carry

        lax.fori_loop(0, n_dot, regroup, 0)

    def increments(i, carry):
        rows = dot_rows_of(i)
        v_ref[rows, :] = jnp.dot(u2_ref[rows, :], r_ref[...], preferred_element_type=F32)
        return carry

    lax.fori_loop(0, n_dot, increments, 0)

    def cmul_add(ar, ai, cr, ci, xr, xi):
        return ar + cr * xr - ci * xi, ai + cr * xi + ci * xr

    def run(reverse):
        tab = tab_ref.at[0, 0]
        last = 0 if reverse else SUBLANES - 1
        first_row = lax.broadcasted_iota(jnp.int32, (SUBLANES, S5_HALF), 0) == (SUBLANES - 1 - last)

        def segment(bases, n_groups, carry):
            def body(i, carry):
                gi = (n_groups - 1 - i) if reverse else i
                out = []
                for base, (cre, cim) in zip(bases, carry):
                    rows = pl.ds(pl.multiple_of(base + gi * SUBLANES, SUBLANES), SUBLANES)
                    re = v_ref[rows, :S5_HALF]
                    im = v_ref[rows, S5_HALF:]
                    for lvl, shift in enumerate((1, 2, 4)):
                        sh = (SUBLANES - shift) if reverse else shift
                        re, im = cmul_add(re, im, tab[2 * lvl], tab[2 * lvl + 1],
                                          pltpu.roll(re, sh, axis=0), pltpu.roll(im, sh, axis=0))
                    re, im = cmul_add(re, im, tab[6], tab[7], cre, cim)
                    sh1 = (SUBLANES - 1) if reverse else 1
                    v_ref[rows, :S5_HALF] = jnp.where(first_row, cre, pltpu.roll(re, sh1, axis=0))
                    v_ref[rows, S5_HALF:] = jnp.where(first_row, cim, pltpu.roll(im, sh1, axis=0))
                    out.append((jnp.broadcast_to(re[last:last + 1, :], re.shape),
                                jnp.broadcast_to(im[last:last + 1, :], im.shape)))
                return tuple(out)
            return lax.fori_loop(0, n_groups, body, carry)

        zero = jnp.zeros((SUBLANES, S5_HALF), F32)
        carry = tuple((zero, zero) for _ in range(n_batch))
        carry = segment([n_batch * rows_x + bi * rows_c for bi in range(n_batch)], rows_c // SUBLANES, carry)
        segment([bi * rows_x for bi in range(n_batch)], rows_x // SUBLANES, carry)

        def outputs(i, carry):
            rows = dot_rows_of(i)
            y = jnp.dot(u2_ref[rows, :], m_ref[...], preferred_element_type=F32)
            y = y + lax.dot_general(v_ref[rows, :].astype(BF16), ot_ref[...], (((1,), (1,)), ((), ())),
                                    preferred_element_type=F32)
            for s in range(S5_SUB):
                part = y[:, s * LANES:(s + 1) * LANES]
                if reverse:
                    y_ref[token_rows_of(i, s), :] += part
                else:
                    y_ref[token_rows_of(i, s), :] = part
            return carry

        lax.fori_loop(0, n_dot, outputs, 0)

    @pl.when(d == 0)
    def _():
        run(False)

    @pl.when(d == 1)
    def _():
        run(True)


def _s5_scan(z, fac, pw, taps, tabs, b, s, tc, col_u):
    r = z.shape[0]
    rc = r // S5_SUB
    n_blocks = fac.shape[1]
    width = S5_SUB * LANES
    assert width == 2 * S5_HALF
    dot_rows = max(n for n in range(16, 641, 16) if rc % n == 0)
    return pl.pallas_call(
        functools.partial(_s5_kernel, b, s // S5_SUB, tc // S5_SUB, dot_rows),
        out_shape=jax.ShapeDtypeStruct((r, n_blocks * LANES), F32),
        grid=(n_blocks, 2),
        in_specs=[
            pl.BlockSpec((r, LANES), lambda cb, d: (0, col_u // LANES + cb), pipeline_mode=pl.Buffered(1)),
            pl.BlockSpec((1, 1, 4, LANES, width), lambda cb, d: (d, cb, 0, 0, 0), pipeline_mode=pl.Buffered(1)),
            pl.BlockSpec((1, 1, 4, S5_SUB, width), lambda cb, d: (d, cb, 0, 0, 0)),
            pl.BlockSpec((1, 1, 2 * S5_SUB - 1, LANES, LANES), lambda cb, d: (d, cb, 0, 0, 0)),
            pl.BlockSpec((1, 1, 8, SUBLANES, S5_HALF), lambda cb, d: (d, cb, 0, 0, 0)),
        ],
        out_specs=pl.BlockSpec((r, LANES), lambda cb, d: (0, cb)),
        scratch_shapes=[pltpu.VMEM((rc, width), BF16), pltpu.VMEM((rc, width), F32),
                        pltpu.VMEM((width, width), BF16), pltpu.VMEM((width, width), BF16),
                        pltpu.VMEM((width, width), BF16)],
        compiler_params=_cparams(("parallel", "arbitrary")),
        name="s5_scan",
    )(z, fac, pw, taps, tabs)


def _glu_kernel(y_ref, u_ref, d_ref, w_ref, b_ref, o_ref, g_ref):
    tm = y_ref.shape[0]

    def body(r, carry):
        rows = pl.ds(pl.multiple_of(r * ROW_CHUNK, ROW_CHUNK), ROW_CHUNK)
        y = y_ref[rows, :] + d_ref[...] * u_ref[rows, :].astype(F32)
        g_ref[rows, :] = jax.nn.gelu(y).astype(BF16)
        return carry

    lax.fori_loop(0, tm // ROW_CHUNK, body, 0)
    g = g_ref[...]
    gate = jax.nn.sigmoid(jnp.dot(g, w_ref[...], preferred_element_type=F32) + b_ref[...])
    o_ref[...] = (g.astype(F32) * gate).astype(BF16)


def _s5_glu(y, z, d_skip, w_glu, b_glu, tm, col_u, n_row_tiles):
    width = y.shape[1]
    return pl.pallas_call(
        _glu_kernel,
        out_shape=jax.ShapeDtypeStruct((n_row_tiles * tm, width), BF16),
        grid=(n_row_tiles,),
        in_specs=[
            pl.BlockSpec((tm, width), lambda i: (i, 0)),
            pl.BlockSpec((tm, width), lambda i: (i, col_u // width)),
            pl.BlockSpec((1, width), lambda i: (0, 0)),
            pl.BlockSpec((width, width), lambda i: (0, 0)),
            pl.BlockSpec((1, width), lambda i: (0, 0)),
        ],
        out_specs=pl.BlockSpec((tm, width), lambda i: (i, 0)),
        scratch_shapes=[pltpu.VMEM((tm, width), BF16)],
        compiler_params=_cparams(("parallel",)),
        name="s5_glu",
    )(y, z, d_skip, w_glu, b_glu)


def _merge_kernel(n_x_tiles, ax_ref, ac_ref, bx_ref, bc_ref, c_ref, ga_ref, gb_ref, gc_ref, w_ref, o_ref):
    def combine(a_ref, b_ref):
        acc = None
        for r, (br_ref, gate_ref) in enumerate(((a_ref, ga_ref), (b_ref, gb_ref), (c_ref, gc_ref))):
            proj = jnp.dot(br_ref[...], w_ref[r], preferred_element_type=F32)
            term = jax.nn.sigmoid(gate_ref[...].astype(F32)) * proj
            acc = term if acc is None else acc + term
        o_ref[...] = acc.astype(BF16)

    @pl.when(pl.program_id(0) < n_x_tiles)
    def _():
        combine(ax_ref, bx_ref)

    @pl.when(pl.program_id(0) >= n_x_tiles)
    def _():
        combine(ac_ref, bc_ref)


def _merge(a_x, a_c, b_x, b_c, cc, z, w_branch, tm, tn, col_g, n_row_tiles, n_x_tiles):
    width = a_x.shape[1]
    d = w_branch.shape[2]

    def gate_spec(r):
        return pl.BlockSpec((tm, tn), lambda i, j: (i, (col_g + r * d) // tn + j))

    ax_spec, ac_spec = _stream_specs((tm, width), n_x_tiles, 0, lambda j: 0)
    return pl.pallas_call(
        functools.partial(_merge_kernel, n_x_tiles),
        out_shape=jax.ShapeDtypeStruct((n_row_tiles * tm, d), BF16),
        grid=(n_row_tiles, d // tn),
        in_specs=[
            ax_spec, ac_spec, ax_spec, ac_spec,
            pl.BlockSpec((tm, width), lambda i, j: (i, 0)),
            gate_spec(0), gate_spec(1), gate_spec(2),
            pl.BlockSpec((N_BRANCH, width, tn), lambda i, j: (0, 0, j)),
        ],
        out_specs=pl.BlockSpec((tm, tn), lambda i, j: (i, j)),
        compiler_params=_cparams(("parallel", "arbitrary")),
        name="merge",
    )(a_x, a_c, b_x, b_c, cc, z, z, z, w_branch)


def _resid_kernel(m_ref, w_ref, x_ref, al_ref, o_ref):
    o_ref[...] = x_ref[...] + al_ref[0] * jnp.dot(m_ref[...], w_ref[...], preferred_element_type=F32)


def _out_proj_residual(m, w, xs, mod3, tm, tn, n_row_tiles, n_x_tiles, tiles_per_batch, ctx_row, k_alpha):
    kdim, d = w.shape
    nt = d // tn

    def alpha_map(i, j):
        r = jnp.where(i < n_x_tiles, i // tiles_per_batch, ctx_row)
        return (r * N_MOD + k_alpha, 0, j)

    return pl.pallas_call(
        _resid_kernel,
        out_shape=jax.ShapeDtypeStruct((n_row_tiles * tm, d), F32),
        grid=(n_row_tiles, nt),
        in_specs=[
            pl.BlockSpec((tm, kdim), lambda i, j: (i, 0)),
            pl.BlockSpec((kdim, tn), lambda i, j: (0, j)),
            pl.BlockSpec((tm, tn), lambda i, j: (i, j)),
            pl.BlockSpec((1, 1, tn), alpha_map),
        ],
        out_specs=pl.BlockSpec((tm, tn), lambda i, j: (i, j)),
        compiler_params=_cparams(("parallel", "arbitrary")),
        name="out_proj",
    )(m, w, xs, mod3)


def _ff1_kernel(x_ref, g_ref, sh_ref, sc_ref, w_ref, h_ref, xn_ref):
    @pl.when(pl.program_id(1) == 0)
    def _():
        _norm_mod_rows(x_ref, g_ref, sh_ref, sc_ref, xn_ref)

    a = jnp.maximum(jnp.dot(xn_ref[...], w_ref[...], preferred_element_type=F32), 0.0)
    h_ref[...] = (a * a).astype(BF16)


def _ff1(xs, g, mod3, w, tm, tn, n_row_tiles, n_x_tiles, tiles_per_batch, ctx_row):
    d, dff = w.shape
    return pl.pallas_call(
        _ff1_kernel,
        out_shape=jax.ShapeDtypeStruct((n_row_tiles * tm, dff), BF16),
        grid=(n_row_tiles, dff // tn),
        in_specs=[
            pl.BlockSpec((tm, d), lambda i, j: (i, 0)),
            pl.BlockSpec((1, d), lambda i, j: (0, 0)),
            pl.BlockSpec((1, 1, d), _mod_row_map(n_x_tiles, tiles_per_batch, ctx_row, 3)),
            pl.BlockSpec((1, 1, d), _mod_row_map(n_x_tiles, tiles_per_batch, ctx_row, 4)),
            pl.BlockSpec((d, tn), lambda i, j: (0, j)),
        ],
        out_specs=pl.BlockSpec((tm, tn), lambda i, j: (i, j)),
        scratch_shapes=[pltpu.VMEM((tm, d), BF16)],
        compiler_params=_cparams(("parallel", "arbitrary")),
        name="ff1",
    )(xs, g, mod3, mod3, w)


def _ff2_kernel(h_ref, w_ref, x_ref, al_ref, o_ref):
    k = pl.program_id(2)
    part = jnp.dot(h_ref[...], w_ref[...], preferred_element_type=F32)

    @pl.when(k == 0)
    def _():
        o_ref[...] = part

    @pl.when(k > 0)
    def _():
        o_ref[...] += part

    @pl.when(k == pl.num_programs(2) - 1)
    def _():
        o_ref[...] = x_ref[...] + al_ref[0] * o_ref[...]


def _ff2(h, w, xs, mod3, tm, tn, tk, n_row_tiles, n_x_tiles, tiles_per_batch, ctx_row):
    dff, d = w.shape

    def alpha_map(i, j, k):
        r = jnp.where(i < n_x_tiles, i // tiles_per_batch, ctx_row)
        return (r * N_MOD + 5, 0, j)

    return pl.pallas_call(
        _ff2_kernel,
        out_shape=jax.ShapeDtypeStruct((n_row_tiles * tm, d), F32),
        grid=(n_row_tiles, d // tn, dff // tk),
        in_specs=[
            pl.BlockSpec((tm, tk), lambda i, j, k: (i, k)),
            pl.BlockSpec((tk, tn), lambda i, j, k: (k, j)),
            pl.BlockSpec((tm, tn), lambda i, j, k: (i, j)),
            pl.BlockSpec((1, 1, tn), alpha_map),
        ],
        out_specs=pl.BlockSpec((tm, tn), lambda i, j, k: (i, j)),
        compiler_params=_cparams(("parallel", "parallel", "arbitrary")),
        name="ff2",
    )(h, w, xs, mod3)


def _rope_partner():
    j = np.arange(MLA_ROPE)
    quarter = MLA_ROPE // 4
    return np.where((j // quarter) % 2 == 0, j + quarter, j - quarter)


def _rope_table(s, tm):
    pos = jnp.arange(s)
    row = (pos // GRID_W).astype(F32)
    col = (pos % GRID_W).astype(F32)
    n_freq = MLA_ROPE // 4
    inv_freq = ROPE_THETA ** (-jnp.arange(n_freq, dtype=F32) / n_freq)
    ang_r = row[:, None] * inv_freq
    ang_c = col[:, None] * inv_freq
    cos = jnp.concatenate([jnp.cos(ang_r)] * 2 + [jnp.cos(ang_c)] * 2, axis=-1)
    sin = jnp.concatenate([-jnp.sin(ang_r), jnp.sin(ang_r), -jnp.sin(ang_c), jnp.sin(ang_c)], axis=-1)
    ident = jnp.concatenate([jnp.ones((tm, MLA_ROPE), F32), jnp.zeros((tm, MLA_ROPE), F32)], axis=-1)
    return jnp.concatenate([jnp.concatenate([cos, sin], axis=-1), ident], axis=0)


REPACK_ROWS = 256
REPACK_COLS = 1024


def _repack_kernel(runs, w_ref, tail_ref, o_ref):
    for out0, in0, width in runs:
        for c in range(0, width, REPACK_COLS):
            w = min(REPACK_COLS, width - c)
            src = in0 + c
            base = src - src % LANES
            win = w_ref[0, :, base:src + w]
            o_ref[:, out0 + c:out0 + c + w] = win[:, src - base:].astype(BF16)
    tail0 = o_ref.shape[1] - tail_ref.shape[1]
    o_ref[:, tail0:] = tail_ref[...]


def _pack_w_in(w_in_all, layer, b_in, gate_b, sizes, n_pad):
    w_in = w_in_all[layer]
    d_model = w_in.shape[0]
    bounds = np.cumsum((0,) + sizes)
    seg = [slice(int(bounds[i]), int(bounds[i + 1])) for i in range(len(sizes))]
    partner = _rope_partner()
    order = (0, 1, 2, 3, 5, 6, 8, 9)
    names = ("q", "k", "v", "o", "qa", "kva", "u", "gates")
    offs, runs, pos = {}, [], 0
    for name, i in zip(names, order):
        offs[name] = pos
        if runs and runs[-1][1] + runs[-1][2] == seg[i].start:
            runs[-1][2] += sizes[i]
        else:
            runs.append([pos, seg[i].start, sizes[i]])
        pos += sizes[i]
    assert all(r[0] % LANES == 0 for r in runs) and pos % LANES == 0
    offs["kpe"], offs["kpe_sw"] = pos, pos + MLA_ROPE
    kpe_w = w_in[:, seg[7]]
    tail = jnp.concatenate([kpe_w, kpe_w[:, partner], jnp.zeros((d_model, n_pad - pos - 2 * MLA_ROPE), F32)],
                           axis=1).astype(BF16)
    w = pl.pallas_call(
        functools.partial(_repack_kernel, tuple(tuple(r) for r in runs)),
        out_shape=jax.ShapeDtypeStruct((d_model, n_pad), BF16),
        grid=(d_model // REPACK_ROWS,),
        in_specs=[
            pl.BlockSpec((1, REPACK_ROWS, w_in.shape[1]), lambda i: (layer, i, 0)),
            pl.BlockSpec((REPACK_ROWS, tail.shape[1]), lambda i: (i, 0)),
        ],
        out_specs=pl.BlockSpec((REPACK_ROWS, n_pad), lambda i: (i, 0)),
        compiler_params=_cparams(("parallel",)),
        name="repack_w_in",
    )(w_in_all, tail)
    kpe_b = b_in[seg[7]]
    b = jnp.concatenate([b_in[seg[i]] for i in order] + [kpe_b, kpe_b[partner], jnp.zeros((n_pad - pos - 2 * MLA_ROPE,), F32)])
    n_g = sizes[4]
    wg = jnp.concatenate([w_in[:, seg[4]], jnp.zeros((d_model, LANES - n_g), F32)], axis=1).astype(BF16)
    bg = jnp.concatenate([b_in[seg[4]] + gate_b.reshape(-1), jnp.zeros((LANES - n_g,), F32)])[None, :]
    return w, b[None, :], wg, bg, offs


def _pack_mla(w_uq, w_ukv, qn_g, kn_g):
    partner = _rope_partner()
    lora = w_uq.shape[0]
    wq = w_uq.reshape(lora, MLA_HEADS, MLA_DQK)
    wq = jnp.concatenate([wq, wq[:, :, MLA_NOPE + partner]], axis=-1).reshape(lora, MLA_HEADS * MLA_SLAB)
    wkv = w_ukv.reshape(w_ukv.shape[0], MLA_HEADS, -1)
    wkv = jnp.concatenate([wkv[:, :, :MLA_NOPE].reshape(lora, -1), wkv[:, :, MLA_NOPE:].reshape(lora, -1)], axis=-1)

    def gains(g):
        return jnp.concatenate([g, g[MLA_NOPE + partner]])[None, :]

    return wq.astype(BF16), wkv.astype(BF16), gains(qn_g), gains(kn_g)


def _pack_s5(a_re, a_im, log_dt, b_re, b_im, c_re, c_im):
    n_dir, n_groups, n_state = a_re.shape
    gc = b_re.shape[-1]
    nb = n_groups // S5_BLOCK_GROUPS
    lam_re = jnp.minimum(a_re.astype(F32), -1e-4)
    lam_im = a_im.astype(F32)
    dt = jnp.exp(log_dt.astype(F32))[..., None]

    def pole_power(k):
        mag = jnp.exp(k * lam_re * dt)
        return mag * jnp.cos(k * lam_im * dt), mag * jnp.sin(k * lam_im * dt)

    bar_re, bar_im = pole_power(1.0)
    den = lam_re * lam_re + lam_im * lam_im
    f_re = ((bar_re - 1.0) * lam_re + bar_im * lam_im) / den
    f_im = (bar_im * lam_re - (bar_re - 1.0) * lam_im) / den
    bb_re = f_re[..., None] * b_re.astype(F32) - f_im[..., None] * b_im.astype(F32)
    bb_im = f_re[..., None] * b_im.astype(F32) + f_im[..., None] * b_re.astype(F32)
    eye = jnp.eye(S5_BLOCK_GROUPS, dtype=F32)
    sub = S5_SUB

    def per_block(a):
        return a.reshape(a.shape[:-2] + (nb, S5_BLOCK_GROUPS * n_state))

    def block_b(part):
        p = part.reshape(n_dir, nb, S5_BLOCK_GROUPS, n_state, gc)
        m = jnp.einsum('dbgnc,gh->dbgchn', p, eye, precision=HIGHEST)
        return m.reshape(n_dir, nb, LANES, S5_HALF)

    def block_c(part):
        p = part.astype(F32).reshape(n_dir, nb, S5_BLOCK_GROUPS, gc, n_state)
        m = jnp.einsum('dbgcn,gh->dbgnhc', p, eye, precision=HIGHEST)
        return m.reshape(n_dir, nb, S5_HALF, LANES)

    bm_re, bm_im = block_b(bb_re), block_b(bb_im)
    cm_re, cm_im = block_c(c_re), block_c(c_im)

    tau = jnp.arange(sub + 1, dtype=F32)[:, None, None, None]
    p_re, p_im = (per_block(p) for p in pole_power(tau))
    pr, pi = p_re[:sub, :, :, None, :], p_im[:sub, :, :, None, :]
    taps = (jnp.einsum('zdbkn,dbnc->zdbkc', bm_re * pr - bm_im * pi, cm_re, precision=HIGHEST)
            - jnp.einsum('zdbkn,dbnc->zdbkc', bm_re * pi + bm_im * pr, cm_im, precision=HIGHEST))

    ct_re, ct_im = jnp.swapaxes(cm_re, -1, -2), jnp.swapaxes(cm_im, -1, -2)
    cat = functools.partial(jnp.concatenate, axis=-1)
    fac = jnp.stack([cat([bm_re, bm_im]), cat([-bm_im, bm_re]), cat([ct_re, -ct_im]), cat([-ct_im, -ct_re])],
                    axis=2)
    t = np.arange(sub)
    zero_taps = jnp.zeros((sub - 1,) + taps.shape[2:], F32)
    pws, taps2 = [], []
    for d in range(n_dir):
        to_exit = (sub - 1 - t) if d == 0 else t
        age = (t + 1) if d == 0 else (sub - t)
        rows = [p_re[to_exit, d], p_im[to_exit, d], p_re[age, d], p_im[age, d]]
        pws.append(jnp.stack([cat([a, a]).transpose(1, 0, 2) for a in rows], axis=1))
        taps2.append(jnp.concatenate([zero_taps, taps[:, 0]] if d == 0 else [taps[::-1, 1], zero_taps], axis=0))
    pw = jnp.stack(pws)
    taps2 = jnp.stack(taps2).transpose(0, 2, 1, 3, 4)

    rows = jnp.arange(SUBLANES)
    tabs = []
    for d in range(n_dir):
        per_dir = []
        for shift in (1, 2, 4):
            keep = ((rows <= SUBLANES - 1 - shift) if d == 1 else (rows >= shift))[None, :, None]
            s_re, s_im = pole_power(float(shift * sub))
            per_dir += [jnp.where(keep, per_block(s_re[d])[:, None, :], 0.0),
                        jnp.where(keep, per_block(s_im[d])[:, None, :], 0.0)]
        expo = (((SUBLANES - rows) if d == 1 else (rows + 1)) * sub).astype(F32)
        s_re, s_im = pole_power(expo[:, None, None, None])
        per_dir += [jnp.moveaxis(per_block(s_re[:, d]), 0, 1), jnp.moveaxis(per_block(s_im[:, d]), 0, 1)]
        tabs.append(jnp.stack(per_dir, axis=1))
    return fac, pw, taps2, jnp.stack(tabs).astype(F32)


def _tri_matrices():
    t = np.arange(ML_CHUNK)
    lower = t[None, :] <= t[:, None]
    mask = np.where(np.stack([lower, lower.T]), 0.0, NEG_BIG).astype(np.float32)
    return jnp.asarray(lower.astype(np.float32), dtype=BF16), jnp.asarray(mask)


def kernel(x, c, ctx, c_ctx, w_mod, b_mod, norm_g, w_in, b_in, ml_gate_b, ml_norm_g, mla_qa_g, mla_kva_g, mla_w_uq, mla_w_ukv, mla_qn_g, mla_kn_g, s5_a_re, s5_a_im, s5_log_dt, s5_b_re, s5_b_im, s5_c_re, s5_c_im, s5_d, s5_w_glu, s5_b_glu, w_branch, w_out, w_ff1, w_ff2):
    b, s, d = x.shape
    tc = ctx.shape[1]
    depth = w_mod.shape[0]
    dv = ml_norm_g.shape[2]
    dk = dv // 2
    lora = mla_qa_g.shape[1]
    s5_width = s5_d.shape[1]
    branch_w = w_branch.shape[2]
    sizes = (ML_HEADS * dk, ML_HEADS * dk, ML_HEADS * dv, ML_HEADS * dv, 4 * ML_HEADS, lora, lora, MLA_ROPE,
             s5_width, N_BRANCH * d)
    assert sum(sizes) == w_in.shape[2] and b + 1 <= SUBLANES
    assert s % ML_CHUNK == 0 and tc % ML_CHUNK == 0 and branch_w == ML_HEADS * dv == MLA_HEADS * LANES == s5_width

    n_x = b * s
    n_c = b * tc
    tm = _row_tile(s, n_c)
    n_x_tiles = n_x // tm
    n_tiles = n_x_tiles + n_c // tm
    tiles_per_batch = s // tm
    tile_args = (n_x_tiles, tiles_per_batch, b)

    xs = jnp.concatenate([x.reshape(n_x, d), ctx.reshape(n_c, d)], axis=0)
    cc = jnp.concatenate([c, c_ctx[None, :], jnp.zeros((SUBLANES - b - 1, d), F32)], axis=0)
    mod = _modulation(cc, w_mod, b_mod)
    tm_q = min(tm, 512)
    tab = _rope_table(s, tm_q)
    tri, ml_mask = _tri_matrices()
    n_used = sum(sizes) - sizes[4] + MLA_ROPE
    tn_in = 1280
    n_pad = -(-n_used // tn_in) * tn_in

    for l in range(depth):
        with_ctx_out = l < depth - 1
        mod3 = mod[l].reshape(SUBLANES * N_MOD, 1, d)
        w_p, b_p, wg, bg, offs = _pack_w_in(w_in, l, b_in[l], ml_gate_b[l], sizes, n_pad)
        z, gz = _in_proj(xs, norm_g[l, 0][None, :], mod3, w_p, b_p, wg, bg, tm, tn_in, *tile_args)

        gates, gates_t = _gate_prep(gz, tri)
        a_x, a_c = _mlstm(z, gates, gates_t, ml_mask, ml_norm_g[l], b, s, tc, dk, dv,
                          (offs["q"], offs["k"], offs["v"], offs["o"]), with_ctx_out)

        wq, wkv, gq, gk = _pack_mla(mla_w_uq[l], mla_w_ukv[l], mla_qn_g[l], mla_kn_g[l])
        qo, ko, vo = _mla_proj(z, tab, mla_qa_g[l][None, :], mla_kva_g[l][None, :], wq, wkv, gq, gk, tm_q,
                               (offs["qa"], offs["kva"], offs["kpe"]), n_x // tm_q, s // tm_q)
        tq = min(512, s)
        b_x = _attention(qo, ko, vo, b, s, tc, tq, True)

        fac, pw, taps, tabs = _pack_s5(s5_a_re[l], s5_a_im[l], s5_log_dt[l], s5_b_re[l], s5_b_im[l],
                                       s5_c_re[l], s5_c_im[l])
        y = _s5_scan(z, fac, pw, taps, tabs, b, s, tc, offs["u"])
        n_out_tiles = n_tiles if with_ctx_out else n_x_tiles
        c_all = _s5_glu(y, z, s5_d[l][None, :], s5_w_glu[l].astype(BF16), s5_b_glu[l][None, :], tm, offs["u"],
                        n_out_tiles)

        if with_ctx_out:
            b_c = _attention(qo, ko, vo, b, s, tc, min(tq, tc), False)
        else:
            a_c, b_c = a_x, b_x
        merged = _merge(a_x, a_c, b_x, b_c, c_all, z, w_branch[l].astype(BF16), tm, 512, offs["gates"],
                        n_out_tiles, n_x_tiles)
        xs1 = _out_proj_residual(merged, w_out[l].astype(BF16), xs, mod3, tm, 1024, n_out_tiles, *tile_args, 2)
        hid = _ff1(xs1, norm_g[l, 1][None, :], mod3, w_ff1[l].astype(BF16), tm, 1024, n_out_tiles, *tile_args)
        xs = _ff2(hid, w_ff2[l].astype(BF16), xs1, mod3, tm, 1024, 2048, n_out_tiles, *tile_args)

    return xs.reshape(b, s, d)
```

```python
import functools
import math

import jax
import jax.numpy as jnp
import numpy as np
from jax import lax
from jax.experimental import pallas as pl
from jax.experimental.pallas import tpu as pltpu

F32 = jnp.float32
BF16 = jnp.bfloat16
HIGHEST = lax.Precision.HIGHEST

N_MOD = 6
N_BRANCH = 3
ML_HEADS = 4
MLA_HEADS = 8
MLA_NOPE = 128
MLA_ROPE = 64
MLA_DQK = MLA_NOPE + MLA_ROPE
MLA_SLAB = 256
GRID_W = 64
ROPE_THETA = 10000.0
S5_GROUP = 16
S5_STATE = 64
S5_BLOCK_GROUPS = 8
EPS = 1e-6
NEG_BIG = -1e30

LANES = 128
SUBLANES = 8
VMEM_LIMIT = 56 * 1024 * 1024

ML_CHUNK = 256
ML_GATE_ROWS = 32
S5_SUB = 8
ATTN_KEY_CHUNK = 512
Q_PRESCALE = MLA_DQK ** -0.5 * math.log2(math.e)
ROW_CHUNK = 64


def _cparams(sem):
    return pltpu.CompilerParams(dimension_semantics=sem, vmem_limit_bytes=VMEM_LIMIT)


def _row_tile(n_x_rows_per_batch, n_ctx_rows):
    tm = 1024
    while n_x_rows_per_batch % tm or n_ctx_rows % tm:
        tm //= 2
    return tm


def _mod_kernel(c_ref, w_ref, b_ref, o_ref):
    s = c_ref[...]
    s = s * jax.nn.sigmoid(s)
    o_ref[0] = jnp.dot(s.astype(BF16), w_ref[0].astype(BF16), preferred_element_type=F32) + b_ref[0]


def _modulation(cc, w_mod, b_mod):
    n_layers, d, n = w_mod.shape
    tn = 1024
    return pl.pallas_call(
        _mod_kernel,
        out_shape=jax.ShapeDtypeStruct((n_layers, SUBLANES, n), F32),
        grid=(n_layers, n // tn),
        in_specs=[
            pl.BlockSpec((SUBLANES, d), lambda l, j: (0, 0)),
            pl.BlockSpec((1, d, tn), lambda l, j: (l, 0, j)),
            pl.BlockSpec((1, 1, tn), lambda l, j: (l, 0, j)),
        ],
        out_specs=pl.BlockSpec((1, SUBLANES, tn), lambda l, j: (l, 0, j)),
        compiler_params=_cparams(("parallel", "parallel")),
        name="adaln_mod",
    )(cc, w_mod, b_mod.reshape(n_layers, 1, n))


def _norm_mod_rows(x_ref, g_ref, sh_ref, sc_ref, xn_ref):
    tm = x_ref.shape[0]
    g = g_ref[...]
    sc = 1.0 + sc_ref[0]
    sh = sh_ref[0]

    def body(r, carry):
        rows = pl.ds(pl.multiple_of(r * ROW_CHUNK, ROW_CHUNK), ROW_CHUNK)
        x = x_ref[rows, :]
        ms = jnp.mean(x * x, axis=-1, keepdims=True)
        y = x * lax.rsqrt(ms + EPS) * g
        xn_ref[rows, :] = (y * sc + sh).astype(BF16)
        return carry

    lax.fori_loop(0, tm // ROW_CHUNK, body, 0)


def _mod_row_map(n_x_tiles, tiles_per_batch, ctx_row, k):
    def index_map(i, j):
        r = jnp.where(i < n_x_tiles, i // tiles_per_batch, ctx_row)
        return (r * N_MOD + k, 0, 0)
    return index_map


def _stream_specs(block, n_x_tiles, ctx_tile0, col_map):
    x_spec = pl.BlockSpec(block, lambda i, *r: (jnp.minimum(i, n_x_tiles - 1), col_map(*r)))
    c_spec = pl.BlockSpec(block, lambda i, *r: (ctx_tile0 + jnp.maximum(i - n_x_tiles, 0), col_map(*r)),
                          pipeline_mode=pl.Buffered(1))
    return x_spec, c_spec


def _dot_nt(a, b_t):
    return lax.dot_general(a, b_t, (((1,), (1,)), ((), ())), preferred_element_type=F32)


def _in_kernel(x_ref, g_ref, sh_ref, sc_ref, w_ref, b_ref, wg_ref, bg_ref, z_ref, gz_ref, xn_ref):
    @pl.when(pl.program_id(1) == 0)
    def _():
        _norm_mod_rows(x_ref, g_ref, sh_ref, sc_ref, xn_ref)
        gz_ref[...] = _dot_nt(xn_ref[...], wg_ref[...]) + bg_ref[...]

    z_ref[...] = (_dot_nt(xn_ref[...], w_ref[...]) + b_ref[...]).astype(BF16)


def _in_proj(xs, g, mod3, w, b, wg, bg, tm, tn, n_x_tiles, tiles_per_batch, ctx_row):
    r, d = xs.shape
    nz = w.shape[0]
    return pl.pallas_call(
        _in_kernel,
        out_shape=(jax.ShapeDtypeStruct((r, nz), BF16), jax.ShapeDtypeStruct((r, LANES), F32)),
        grid=(r // tm, nz // tn),
        in_specs=[
            pl.BlockSpec((tm, d), lambda i, j: (i, 0)),
            pl.BlockSpec((1, d), lambda i, j: (0, 0)),
            pl.BlockSpec((1, 1, d), _mod_row_map(n_x_tiles, tiles_per_batch, ctx_row, 0)),
            pl.BlockSpec((1, 1, d), _mod_row_map(n_x_tiles, tiles_per_batch, ctx_row, 1)),
            pl.BlockSpec((tn, d), lambda i, j: (j, 0)),
            pl.BlockSpec((1, tn), lambda i, j: (0, j)),
            pl.BlockSpec((LANES, d), lambda i, j: (0, 0)),
            pl.BlockSpec((1, LANES), lambda i, j: (0, 0)),
        ],
        out_specs=(
            pl.BlockSpec((tm, tn), lambda i, j: (i, j)),
            pl.BlockSpec((tm, LANES), lambda i, j: (i, 0)),
        ),
        scratch_shapes=[pltpu.VMEM((tm, d), BF16)],
        compiler_params=_cparams(("parallel", "arbitrary")),
        name="in_proj",
    )(xs, g, mod3, mod3, w, b, wg, bg)


def _log_sigmoid(x):
    return jnp.minimum(x, 0.0) - jnp.log1p(jnp.exp(-jnp.abs(x)))


def _split3(a):
    hi = a.astype(BF16)
    r1 = a - hi.astype(F32)
    mid = r1.astype(BF16)
    lo = (r1 - mid.astype(F32)).astype(BF16)
    return hi, mid, lo


def _gate_prep_kernel(g_ref, tri_ref, a_ref, at_ref):
    g = g_ref[...]
    kind = lax.broadcasted_iota(jnp.int32, g.shape, 1) // ML_HEADS
    lg = jnp.where((kind == 1) | (kind == 3), _log_sigmoid(g), g)
    pre = sum(jnp.dot(tri_ref[...], p, preferred_element_type=F32) for p in _split3(lg))
    suf = pre[ML_CHUNK - 1:ML_CHUNK, :] - pre + lg
    a = jnp.where(kind == 4, pltpu.roll(pre, 3 * ML_HEADS, axis=1),
                  jnp.where(kind == 5, pltpu.roll(suf, 2 * ML_HEADS, axis=1), lg))
    a_ref[...] = a
    at_ref[0] = a.T[:ML_GATE_ROWS, :]


def _gate_prep(gz, tri):
    r = gz.shape[0]
    return pl.pallas_call(
        _gate_prep_kernel,
        out_shape=(jax.ShapeDtypeStruct((r, LANES), F32),
                   jax.ShapeDtypeStruct((r // ML_CHUNK, ML_GATE_ROWS, ML_CHUNK), F32)),
        grid=(r // ML_CHUNK,),
        in_specs=[pl.BlockSpec((ML_CHUNK, LANES), lambda i: (i, 0)),
                  pl.BlockSpec((ML_CHUNK, ML_CHUNK), lambda i: (0, 0))],
        out_specs=(pl.BlockSpec((ML_CHUNK, LANES), lambda i: (i, 0)),
                   pl.BlockSpec((1, ML_GATE_ROWS, ML_CHUNK), lambda i: (i, 0, 0))),
        compiler_params=_cparams(("parallel",)),
        name="mlstm_gates",
    )(gz, tri)


def _ml_chunk(q, k_t, v, cum_cb, li_r, lf_r, cum_r, mask_add, carry, inv_scale):
    c_mat, n_mat, m = carry
    length, dv = v.shape
    crow = cum_r - li_r
    total = jnp.sum(lf_r, axis=-1, keepdims=True)
    ones = jnp.ones((length, LANES), BF16)
    n_blk = length // LANES

    def wide(a):
        return jnp.concatenate([a] * (dv // LANES), axis=1)

    log_w = [cum_cb - crow[:, j * LANES:(j + 1) * LANES] + mask_add[:, j * LANES:(j + 1) * LANES]
             for j in range(n_blk)]
    row_max = jnp.max(functools.reduce(jnp.maximum, log_w), axis=-1, keepdims=True)
    log_inter = cum_cb + m
    m_t = jnp.maximum(log_inter, row_max)
    w_inter = jnp.exp(log_inter - m_t)
    qk = jnp.dot(q, k_t, preferred_element_type=F32)
    s = jnp.concatenate([qk[:, j * LANES:(j + 1) * LANES] * jnp.exp(log_w[j] - m_t) for j in range(n_blk)],
                        axis=1).astype(BF16)
    num = wide(w_inter) * jnp.dot(q, c_mat.astype(BF16), preferred_element_type=F32)
    num = num + jnp.dot(s, v, preferred_element_type=F32)
    den = w_inter * jnp.dot(q, n_mat.astype(BF16), preferred_element_type=F32)
    den = den + jnp.dot(s, ones, preferred_element_type=F32)
    h = num * wide(1.0 / jnp.maximum(jnp.abs(den), jnp.exp(-m_t) * inv_scale))

    log_end = total - crow
    m_new = jnp.maximum(total + m, jnp.max(log_end, axis=-1, keepdims=True))
    decay = jnp.exp(total + m - m_new)
    kw_t = (k_t.astype(F32) * jnp.exp(log_end - m_new)).astype(BF16)
    c_new = decay * c_mat + jnp.dot(kw_t, v, preferred_element_type=F32)
    n_new = decay * n_mat + jnp.dot(kw_t, ones, preferred_element_type=F32)
    return h, (c_new, n_new, m_new)


def _mlstm_kernel(with_ctx_out, qx, kx, vx, ox, qc, kc, vc, oc, gx, gc, gtx, gtc, mask_ref, ng_ref, *rest):
    if with_ctx_out:
        ax_ref, ac_ref, cumx, ktx, cumc, ktc, hx, hc = rest
    else:
        ax_ref, cumx, ktx, cumc, ktc, hx, hc = rest
        ac_ref = None
    head = pl.program_id(1)
    dk = qx.shape[1]
    inv_scale = float(dk) ** 0.5
    n_x_chunks = qx.shape[0] // ML_CHUNK
    n_c_chunks = qc.shape[0] // ML_CHUNK

    def prep(g_ref, k_ref, cum_ref, kt_ref):
        def body(i, carry):
            rows = pl.ds(pl.multiple_of(i * ML_CHUNK, ML_CHUNK), ML_CHUNK)
            a = g_ref[rows, :]
            col = lax.broadcasted_iota(jnp.int32, a.shape, 1)
            for d in range(2):
                pick = jnp.sum(jnp.where(col == (4 + d) * ML_HEADS + head, a, 0.0), axis=-1, keepdims=True)
                cum_ref[d, rows, :] = jnp.broadcast_to(pick, a.shape)
            kt_ref[i] = k_ref[rows, :].astype(F32).T.astype(BF16)
            return carry
        n_chunks = g_ref.shape[0] // ML_CHUNK
        lax.fori_loop(0, n_chunks, body, 0, unroll=2 if n_chunks % 2 == 0 else 1)

    prep(gx, kx, cumx, ktx)
    prep(gc, kc, cumc, ktc)

    for d in range(2):
        reverse = d == 1
        mask_add = mask_ref[d]

        def step(q_ref, v_ref, cum_ref, gt_ref, kt_ref, h_ref, ci, carry):
            rows = pl.ds(pl.multiple_of(ci * ML_CHUNK, ML_CHUNK), ML_CHUNK)
            li_r, lf_r, cum_r = (gt_ref[ci, pl.ds(kind * ML_HEADS + head, 1), :] for kind in (2 * d, 2 * d + 1, 4 + d))
            h, carry = _ml_chunk(q_ref[rows, :], kt_ref[ci], v_ref[rows, :], cum_ref[d, rows, :], li_r, lf_r, cum_r,
                                 mask_add, carry, inv_scale)
            if reverse:
                h_ref[rows, :] += h
            else:
                h_ref[rows, :] = h
            return carry

        carry = (jnp.zeros((dk, vx.shape[1]), F32), jnp.zeros((dk, LANES), F32), jnp.zeros((1, 1), F32))

        def ctx_body(i, carry):
            ci = (n_c_chunks - 1 - i) if reverse else i
            return step(qc, vc, cumc, gtc, ktc, hc, ci, carry)

        def x_body(i, carry):
            ci = (n_x_chunks - 1 - i) if reverse else i
            return step(qx, vx, cumx, gtx, ktx, hx, ci, carry)

        carry = lax.fori_loop(0, n_c_chunks, ctx_body, carry)
        lax.fori_loop(0, n_x_chunks, x_body, carry, unroll=2 if n_x_chunks % 2 == 0 else 1)

    ng = ng_ref[0]

    def finish(h_ref, o_ref, a_ref):
        def body(i, carry):
            rows = pl.ds(pl.multiple_of(i * ML_CHUNK, ML_CHUNK), ML_CHUNK)
            h = h_ref[rows, :]
            hn = h * lax.rsqrt(jnp.mean(h * h, axis=-1, keepdims=True) + EPS) * ng
            a_ref[rows, :] = (hn * jax.nn.sigmoid(o_ref[rows, :].astype(F32))).astype(BF16)
            return carry
        lax.fori_loop(0, h_ref.shape[0] // ML_CHUNK, body, 0)

    finish(hx, ox, ax_ref)
    if with_ctx_out:
        finish(hc, oc, ac_ref)


def _mlstm(z, gates, gates_t, mask, ml_norm_g, b, s, tc, dk, dv, cols, with_ctx_out):
    n_x = b * s
    cq, ck, cv, co = cols
    ctx0 = n_x // tc
    xc, cc = s // ML_CHUNK, tc // ML_CHUNK

    def xspec(width, col0):
        return pl.BlockSpec((s, width), lambda i, h: (i, col0 // width + h))

    def cspec(width, col0):
        return pl.BlockSpec((tc, width), lambda i, h: (ctx0 + i, col0 // width + h))

    out_shape = [jax.ShapeDtypeStruct((n_x, ML_HEADS * dv), BF16)]
    out_specs = [pl.BlockSpec((s, dv), lambda i, h: (i, h))]
    if with_ctx_out:
        out_shape.append(jax.ShapeDtypeStruct((b * tc, ML_HEADS * dv), BF16))
        out_specs.append(pl.BlockSpec((tc, dv), lambda i, h: (i, h)))
    res = pl.pallas_call(
        functools.partial(_mlstm_kernel, with_ctx_out),
        out_shape=tuple(out_shape),
        grid=(b, ML_HEADS),
        in_specs=[
            xspec(dk, cq), xspec(dk, ck), xspec(dv, cv), xspec(dv, co),
            cspec(dk, cq), cspec(dk, ck), cspec(dv, cv), cspec(dv, co),
            pl.BlockSpec((s, LANES), lambda i, h: (i, 0)),
            pl.BlockSpec((tc, LANES), lambda i, h: (ctx0 + i, 0)),
            pl.BlockSpec((xc, ML_GATE_ROWS, ML_CHUNK), lambda i, h: (i, 0, 0)),
            pl.BlockSpec((cc, ML_GATE_ROWS, ML_CHUNK), lambda i, h: (b * xc // cc + i, 0, 0)),
            pl.BlockSpec((2, ML_CHUNK, ML_CHUNK), lambda i, h: (0, 0, 0)),
            pl.BlockSpec((1, 1, dv), lambda i, h: (h, 0, 0)),
        ],
        out_specs=tuple(out_specs),
        scratch_shapes=[
            pltpu.VMEM((2, s, LANES), F32), pltpu.VMEM((xc, dk, ML_CHUNK), BF16),
            pltpu.VMEM((2, tc, LANES), F32), pltpu.VMEM((cc, dk, ML_CHUNK), BF16),
            pltpu.VMEM((s, dv), F32), pltpu.VMEM((tc, dv), F32),
        ],
        compiler_params=_cparams(("parallel", "parallel")),
        name="mlstm",
    )(z, z, z, z, z, z, z, z, gates, gates, gates_t, gates_t, mask, ml_norm_g.reshape(ML_HEADS, 1, dv))
    return res if with_ctx_out else (res[0], None)


def _mla_proj_kernel(qa_ref, kva_ref, kpe_ref, tab_ref, qag_ref, kvag_ref, wq_ref, wkv_ref,
                     gq_ref, gk_ref, q_ref, k_ref, v_ref):
    def normed(a_ref, g_ref):
        a = a_ref[...].astype(F32)
        return (a * lax.rsqrt(jnp.mean(a * a, axis=-1, keepdims=True) + EPS) * g_ref[...]).astype(BF16)

    q_all = jnp.dot(normed(qa_ref, qag_ref), wq_ref[...], preferred_element_type=F32)
    kv_all = jnp.dot(normed(kva_ref, kvag_ref), wkv_ref[...], preferred_element_type=F32)
    tab = tab_ref[...]
    lane = lax.broadcasted_iota(jnp.int32, tab.shape, 1)
    first_half = lane < MLA_ROPE
    gq = gq_ref[...]
    gk = gk_ref[...]
    inv_dqk = 1.0 / MLA_DQK

    kpe = kpe_ref[...].astype(F32)
    ss_kpe = jnp.sum(jnp.where(first_half, kpe * kpe, 0.0), axis=-1, keepdims=True)
    kpe_t = kpe * (tab * gk[:, LANES:])
    kpe_rot = jnp.where(first_half, kpe_t + pltpu.roll(kpe_t, MLA_ROPE, axis=1), 0.0)

    for h in range(MLA_HEADS):
        qn = q_all[:, h * MLA_SLAB:h * MLA_SLAB + LANES]
        qp = q_all[:, h * MLA_SLAB + LANES:(h + 1) * MLA_SLAB]
        ss = jnp.sum(qn * qn, axis=-1, keepdims=True) + jnp.sum(jnp.where(first_half, qp * qp, 0.0), axis=-1,
                                                                keepdims=True)
        r = lax.rsqrt(ss * inv_dqk + EPS) * Q_PRESCALE
        qp_t = qp * (tab * gq[:, LANES:])
        qp_rot = qp_t + pltpu.roll(qp_t, MLA_ROPE, axis=1)
        q_ref[:, h * MLA_SLAB:h * MLA_SLAB + LANES] = (qn * r * gq[:, :LANES]).astype(BF16)
        q_ref[:, h * MLA_SLAB + LANES:(h + 1) * MLA_SLAB] = (qp_rot * r).astype(BF16)

        kn = kv_all[:, h * LANES:(h + 1) * LANES]
        rk = lax.rsqrt((jnp.sum(kn * kn, axis=-1, keepdims=True) + ss_kpe) * inv_dqk + EPS)
        k_ref[:, h * MLA_SLAB:h * MLA_SLAB + LANES] = (kn * rk * gk[:, :LANES]).astype(BF16)
        k_ref[:, h * MLA_SLAB + LANES:(h + 1) * MLA_SLAB] = (kpe_rot * rk).astype(BF16)

    ones_col = jnp.where(lane == 0, 1.0, 0.0).astype(BF16)
    for h in range(MLA_HEADS):
        v_ref[:, h * MLA_SLAB:h * MLA_SLAB + LANES] = kv_all[:, (MLA_HEADS + h) * LANES:(MLA_HEADS + h + 1) * LANES
                                                             ].astype(BF16)
        v_ref[:, h * MLA_SLAB + LANES:(h + 1) * MLA_SLAB] = ones_col


def _mla_proj(z, tab, qag, kvag, wq, wkv, gq, gk, tm, cols, n_x_tiles, tab_tiles):
    r = z.shape[0]
    cqa, ckva, ckpe = cols
    lora = qag.shape[1]
    hs = MLA_HEADS * MLA_SLAB
    hv = MLA_HEADS * LANES
    return pl.pallas_call(
        _mla_proj_kernel,
        out_shape=(jax.ShapeDtypeStruct((r, hs), BF16), jax.ShapeDtypeStruct((r, hs), BF16),
                   jax.ShapeDtypeStruct((r, hs), BF16)),
        grid=(r // tm,),
        in_specs=[
            pl.BlockSpec((tm, lora), lambda i: (i, cqa // lora)),
            pl.BlockSpec((tm, lora), lambda i: (i, ckva // lora)),
            pl.BlockSpec((tm, LANES), lambda i: (i, ckpe // LANES)),
            pl.BlockSpec((tm, LANES), lambda i: (jnp.where(i < n_x_tiles, i % tab_tiles, tab_tiles), 0)),
            pl.BlockSpec((1, lora), lambda i: (0, 0)),
            pl.BlockSpec((1, lora), lambda i: (0, 0)),
            pl.BlockSpec((lora, hs), lambda i: (0, 0)),
            pl.BlockSpec((lora, 2 * hv), lambda i: (0, 0)),
            pl.BlockSpec((1, MLA_SLAB), lambda i: (0, 0)),
            pl.BlockSpec((1, MLA_SLAB), lambda i: (0, 0)),
        ],
        out_specs=(pl.BlockSpec((tm, hs), lambda i: (i, 0)), pl.BlockSpec((tm, hs), lambda i: (i, 0)),
                   pl.BlockSpec((tm, hs), lambda i: (i, 0))),
        compiler_params=_cparams(("parallel",)),
        name="mla_qkv",
    )(z, z, z, tab, qag, kvag, wq, wkv, gq, gk)


def _attn_kernel(n_kv, tq, q_ref, *refs):
    k_refs = refs[:n_kv]
    v_refs = refs[n_kv:2 * n_kv]
    o_ref, s0_ref, s1_ref, m0_ref, m1_ref = refs[2 * n_kv:]
    slots = ((s0_ref, m0_ref), (s1_ref, m1_ref))
    n_tiles = q_ref.shape[0] // tq
    chunks = []
    col = 0
    for kv, k_ref in enumerate(k_refs):
        n_keys = k_ref.shape[0]
        step = min(ATTN_KEY_CHUNK, n_keys)
        for off in range(0, n_keys, step):
            chunks.append((kv, off, col, step))
            col += step

    def scores(t, slot):
        s_ref, m_ref = slots[slot]
        rows = pl.ds(pl.multiple_of(t * tq, tq), tq)
        q = q_ref[rows, :]
        run = None
        for kv, off, c0, size in chunks:
            s = lax.dot_general(q, k_refs[kv][off:off + size, :], (((1,), (1,)), ((), ())),
                                preferred_element_type=F32)
            s_ref[:, c0:c0 + size] = s
            for lb in range(size // LANES):
                blk = s[:, lb * LANES:(lb + 1) * LANES]
                run = blk if run is None else jnp.maximum(run, blk)
        m_ref[...] = run

    def finish(t, slot):
        s_ref, m_ref = slots[slot]
        rows = pl.ds(pl.multiple_of(t * tq, tq), tq)
        m = jnp.max(m_ref[...], axis=-1, keepdims=True)
        acc = None
        for kv, off, c0, size in chunks:
            p = jnp.exp2(s_ref[:, c0:c0 + size] - m).astype(BF16)
            pv = jnp.dot(p, v_refs[kv][off:off + size, :], preferred_element_type=F32)
            acc = pv if acc is None else acc + pv
        o_ref[rows, :] = (acc[:, :LANES] / acc[:, LANES:LANES + 1]).astype(BF16)

    scores(0, 0)

    def body(k, carry):
        scores(2 * k + 1, 1)
        finish(2 * k, 0)
        scores(jnp.minimum(2 * k + 2, n_tiles - 1), 0)
        finish(2 * k + 1, 1)
        return carry

    lax.fori_loop(0, n_tiles // 2, body, 0)
    if n_tiles % 2:
        finish(n_tiles - 1, 0)


def _attention(qo, ko, vo, b, s, tc, tq, latent):
    n_x = b * s
    ctx0 = n_x // tc
    cspec = pl.BlockSpec((tc, MLA_SLAB), lambda i, h: (ctx0 + i, h))
    xspec = pl.BlockSpec((s, MLA_SLAB), lambda i, h: (i, h))
    if latent:
        n_q, n_keys = s, s + tc
        in_specs = [xspec, cspec, xspec, cspec, xspec]
        args = (qo, ko, ko, vo, vo)
    else:
        n_q, n_keys = tc, tc
        in_specs = [cspec, cspec, cspec]
        args = (qo, ko, vo)
    return pl.pallas_call(
        functools.partial(_attn_kernel, (len(args) - 1) // 2, tq),
        out_shape=jax.ShapeDtypeStruct((b * n_q, MLA_HEADS * LANES), BF16),
        grid=(b, MLA_HEADS),
        in_specs=in_specs,
        out_specs=pl.BlockSpec((n_q, LANES), lambda i, h: (i, h)),
        scratch_shapes=[pltpu.VMEM((tq, n_keys), F32), pltpu.VMEM((tq, n_keys), F32),
                        pltpu.VMEM((tq, LANES), F32), pltpu.VMEM((tq, LANES), F32)],
        compiler_params=_cparams(("parallel", "parallel")),
        name="attn_latent" if latent else "attn_ctx",
    )(*args)


S5_HALF = S5_BLOCK_GROUPS * S5_STATE


def _s5_kernel(n_batch, rows_x, rows_c, dot_rows, u_ref, fac_ref, pw_ref, tap_ref, tab_ref, y_ref,
               u2_ref, v_ref, r_ref, m_ref, ot_ref):
    d = pl.program_id(1)
    n_dot = u2_ref.shape[0] // dot_rows

    for s in range(S5_SUB):
        blk = slice(s * LANES, (s + 1) * LANES)
        r_ref[blk, :] = (fac_ref[0, 0, 0] * pw_ref[0, 0, 0, s:s + 1, :]
                         + fac_ref[0, 0, 1] * pw_ref[0, 0, 1, s:s + 1, :]).astype(BF16)
        ot_ref[blk, :] = (fac_ref[0, 0, 2] * pw_ref[0, 0, 2, s:s + 1, :]
                          + fac_ref[0, 0, 3] * pw_ref[0, 0, 3, s:s + 1, :]).astype(BF16)
        for t in range(S5_SUB):
            m_ref[blk, t * LANES:(t + 1) * LANES] = tap_ref[0, 0, t - s + S5_SUB - 1].astype(BF16)

    def dot_rows_of(i):
        return pl.ds(pl.multiple_of(i * dot_rows, dot_rows), dot_rows)

    def token_rows_of(i, s):
        return pl.ds(i * (dot_rows * S5_SUB) + s, dot_rows, stride=S5_SUB)

    @pl.when(d == 0)
    def _():
        def stage(i, carry):
            rows = pl.ds(pl.multiple_of(i * dot_rows, dot_rows), dot_rows)
            y_ref[rows, :] = u_ref[rows, :].astype(F32)
            return carry

        lax.fori_loop(0, n_dot * S5_SUB, stage, 0)

        def regroup(i, carry):
            for s in range(S5_SUB):
                u2_ref[dot_rows_of(i), s * LANES:(s + 1) * LANES] = y_ref[token_rows_of(i, s), :].astype(BF16)
            return carry

        lax.fori_loop(0, n_dot, regroup, 0)

    def increments(i, carry):
        rows = dot_rows_of(i)
        v_ref[rows, :] = jnp.dot(u2_ref[rows, :], r_ref[...], preferred_element_type=F32)
        return carry

    lax.fori_loop(0, n_dot, increments, 0)

    def cmul_add(ar, ai, cr, ci, xr, xi):
        return ar + cr * xr - ci * xi, ai + cr * xi + ci * xr

    def run(reverse):
        tab = tab_ref.at[0, 0]
        last = 0 if reverse else SUBLANES - 1
        first_row = lax.broadcasted_iota(jnp.int32, (SUBLANES, S5_HALF), 0) == (SUBLANES - 1 - last)

        def segment(bases, n_groups, carry):
            def body(i, carry):
                gi = (n_groups - 1 - i) if reverse else i
                out = []
                for base, (cre, cim) in zip(bases, carry):
                    rows = pl.ds(pl.multiple_of(base + gi * SUBLANES, SUBLANES), SUBLANES)
                    re = v_ref[rows, :S5_HALF]
                    im = v_ref[rows, S5_HALF:]
                    for lvl, shift in enumerate((1, 2, 4)):
                        sh = (SUBLANES - shift) if reverse else shift
                        re, im = cmul_add(re, im, tab[2 * lvl], tab[2 * lvl + 1],
                                          pltpu.roll(re, sh, axis=0), pltpu.roll(im, sh, axis=0))
                    re, im = cmul_add(re, im, tab[6], tab[7], cre, cim)
                    sh1 = (SUBLANES - 1) if reverse else 1
                    v_ref[rows, :S5_HALF] = jnp.where(first_row, cre, pltpu.roll(re, sh1, axis=0))
                    v_ref[rows, S5_HALF:] = jnp.where(first_row, cim, pltpu.roll(im, sh1, axis=0))
                    out.append((jnp.broadcast_to(re[last:last + 1, :], re.shape),
                                jnp.broadcast_to(im[last:last + 1, :], im.shape)))
                return tuple(out)
            return lax.fori_loop(0, n_groups, body, carry)

        zero = jnp.zeros((SUBLANES, S5_HALF), F32)
        carry = tuple((zero, zero) for _ in range(n_batch))
        carry = segment([n_batch * rows_x + bi * rows_c for bi in range(n_batch)], rows_c // SUBLANES, carry)
        segment([bi * rows_x for bi in range(n_batch)], rows_x // SUBLANES, carry)

        def outputs(i, carry):
            rows = dot_rows_of(i)
            y = jnp.dot(u2_ref[rows, :], m_ref[...], preferred_element_type=F32)
            y = y + lax.dot_general(v_ref[rows, :].astype(BF16), ot_ref[...], (((1,), (1,)), ((), ())),
                                    preferred_element_type=F32)
            for s in range(S5_SUB):
                part = y[:, s * LANES:(s + 1) * LANES]
                if reverse:
                    y_ref[token_rows_of(i, s), :] += part
                else:
                    y_ref[token_rows_of(i, s), :] = part
            return carry

        lax.fori_loop(0, n_dot, outputs, 0)

    @pl.when(d == 0)
    def _():
        run(False)

    @pl.when(d == 1)
    def _():
        run(True)


def _s5_scan(z, fac, pw, taps, tabs, b, s, tc, col_u):
    r = z.shape[0]
    rc = r // S5_SUB
    n_blocks = fac.shape[1]
    width = S5_SUB * LANES
    assert width == 2 * S5_HALF
    dot_rows = max(n for n in range(16, 641, 16) if rc % n == 0)
    return pl.pallas_call(
        functools.partial(_s5_kernel, b, s // S5_SUB, tc // S5_SUB, dot_rows),
        out_shape=jax.ShapeDtypeStruct((r, n_blocks * LANES), F32),
        grid=(n_blocks, 2),
        in_specs=[
            pl.BlockSpec((r, LANES), lambda cb, d: (0, col_u // LANES + cb), pipeline_mode=pl.Buffered(1)),
            pl.BlockSpec((1, 1, 4, LANES, width), lambda cb, d: (d, cb, 0, 0, 0), pipeline_mode=pl.Buffered(1)),
            pl.BlockSpec((1, 1, 4, S5_SUB, width), lambda cb, d: (d, cb, 0, 0, 0)),
            pl.BlockSpec((1, 1, 2 * S5_SUB - 1, LANES, LANES), lambda cb, d: (d, cb, 0, 0, 0)),
            pl.BlockSpec((1, 1, 8, SUBLANES, S5_HALF), lambda cb, d: (d, cb, 0, 0, 0)),
        ],
        out_specs=pl.BlockSpec((r, LANES), lambda cb, d: (0, cb)),
        scratch_shapes=[pltpu.VMEM((rc, width), BF16), pltpu.VMEM((rc, width), F32),
                        pltpu.VMEM((width, width), BF16), pltpu.VMEM((width, width), BF16),
                        pltpu.VMEM((width, width), BF16)],
        compiler_params=_cparams(("parallel", "arbitrary")),
        name="s5_scan",
    )(z, fac, pw, taps, tabs)


def _glu_kernel(y_ref, u_ref, d_ref, w_ref, b_ref, o_ref, g_ref):
    tm = y_ref.shape[0]

    def body(r, carry):
        rows = pl.ds(pl.multiple_of(r * ROW_CHUNK, ROW_CHUNK), ROW_CHUNK)
        y = y_ref[rows, :] + d_ref[...] * u_ref[rows, :].astype(F32)
        g_ref[rows, :] = jax.nn.gelu(y).astype(BF16)
        return carry

    lax.fori_loop(0, tm // ROW_CHUNK, body, 0)
    g = g_ref[...]
    gate = jax.nn.sigmoid(jnp.dot(g, w_ref[...], preferred_element_type=F32) + b_ref[...])
    o_ref[...] = (g.astype(F32) * gate).astype(BF16)


def _s5_glu(y, z, d_skip, w_glu, b_glu, tm, col_u, n_row_tiles):
    width = y.shape[1]
    return pl.pallas_call(
        _glu_kernel,
        out_shape=jax.ShapeDtypeStruct((n_row_tiles * tm, width), BF16),
        grid=(n_row_tiles,),
        in_specs=[
            pl.BlockSpec((tm, width), lambda i: (i, 0)),
            pl.BlockSpec((tm, width), lambda i: (i, col_u // width)),
            pl.BlockSpec((1, width), lambda i: (0, 0)),
            pl.BlockSpec((width, width), lambda i: (0, 0)),
            pl.BlockSpec((1, width), lambda i: (0, 0)),
        ],
        out_specs=pl.BlockSpec((tm, width), lambda i: (i, 0)),
        scratch_shapes=[pltpu.VMEM((tm, width), BF16)],
        compiler_params=_cparams(("parallel",)),
        name="s5_glu",
    )(y, z, d_skip, w_glu, b_glu)


def _merge_kernel(n_x_tiles, ax_ref, ac_ref, bx_ref, bc_ref, c_ref, ga_ref, gb_ref, gc_ref, w_ref, o_ref):
    def combine(a_ref, b_ref):
        acc = None
        for r, (br_ref, gate_ref) in enumerate(((a_ref, ga_ref), (b_ref, gb_ref), (c_ref, gc_ref))):
            proj = jnp.dot(br_ref[...], w_ref[r], preferred_element_type=F32)
            term = jax.nn.sigmoid(gate_ref[...].astype(F32)) * proj
            acc = term if acc is None else acc + term
        o_ref[...] = acc.astype(BF16)

    @pl.when(pl.program_id(0) < n_x_tiles)
    def _():
        combine(ax_ref, bx_ref)

    @pl.when(pl.program_id(0) >= n_x_tiles)
    def _():
        combine(ac_ref, bc_ref)


def _merge(a_x, a_c, b_x, b_c, cc, z, w_branch, tm, tn, col_g, n_row_tiles, n_x_tiles):
    width = a_x.shape[1]
    d = w_branch.shape[2]

    def gate_spec(r):
        return pl.BlockSpec((tm, tn), lambda i, j: (i, (col_g + r * d) // tn + j))

    ax_spec, ac_spec = _stream_specs((tm, width), n_x_tiles, 0, lambda j: 0)
    return pl.pallas_call(
        functools.partial(_merge_kernel, n_x_tiles),
        out_shape=jax.ShapeDtypeStruct((n_row_tiles * tm, d), BF16),
        grid=(n_row_tiles, d // tn),
        in_specs=[
            ax_spec, ac_spec, ax_spec, ac_spec,
            pl.BlockSpec((tm, width), lambda i, j: (i, 0)),
            gate_spec(0), gate_spec(1), gate_spec(2),
            pl.BlockSpec((N_BRANCH, width, tn), lambda i, j: (0, 0, j)),
        ],
        out_specs=pl.BlockSpec((tm, tn), lambda i, j: (i, j)),
        compiler_params=_cparams(("parallel", "arbitrary")),
        name="merge",
    )(a_x, a_c, b_x, b_c, cc, z, z, z, w_branch)


def _resid_kernel(m_ref, w_ref, x_ref, al_ref, o_ref):
    o_ref[...] = x_ref[...] + al_ref[0] * jnp.dot(m_ref[...], w_ref[...], preferred_element_type=F32)


def _out_proj_residual(m, w, xs, mod3, tm, tn, n_row_tiles, n_x_tiles, tiles_per_batch, ctx_row, k_alpha):
    kdim, d = w.shape
    nt = d // tn

    def alpha_map(i, j):
        r = jnp.where(i < n_x_tiles, i // tiles_per_batch, ctx_row)
        return (r * N_MOD + k_alpha, 0, j)

    return pl.pallas_call(
        _resid_kernel,
        out_shape=jax.ShapeDtypeStruct((n_row_tiles * tm, d), F32),
        grid=(n_row_tiles, nt),
        in_specs=[
            pl.BlockSpec((tm, kdim), lambda i, j: (i, 0)),
            pl.BlockSpec((kdim, tn), lambda i, j: (0, j)),
            pl.BlockSpec((tm, tn), lambda i, j: (i, j)),
            pl.BlockSpec((1, 1, tn), alpha_map),
        ],
        out_specs=pl.BlockSpec((tm, tn), lambda i, j: (i, j)),
        compiler_params=_cparams(("parallel", "arbitrary")),
        name="out_proj",
    )(m, w, xs, mod3)


def _ff1_kernel(x_ref, g_ref, sh_ref, sc_ref, w_ref, h_ref, xn_ref):
    @pl.when(pl.program_id(1) == 0)
    def _():
        _norm_mod_rows(x_ref, g_ref, sh_ref, sc_ref, xn_ref)

    a = jnp.maximum(jnp.dot(xn_ref[...], w_ref[...], preferred_element_type=F32), 0.0)
    h_ref[...] = (a * a).astype(BF16)


def _ff1(xs, g, mod3, w, tm, tn, n_row_tiles, n_x_tiles, tiles_per_batch, ctx_row):
    d, dff = w.shape
    return pl.pallas_call(
        _ff1_kernel,
        out_shape=jax.ShapeDtypeStruct((n_row_tiles * tm, dff), BF16),
        grid=(n_row_tiles, dff // tn),
        in_specs=[
            pl.BlockSpec((tm, d), lambda i, j: (i, 0)),
            pl.BlockSpec((1, d), lambda i, j: (0, 0)),
            pl.BlockSpec((1, 1, d), _mod_row_map(n_x_tiles, tiles_per_batch, ctx_row, 3)),
            pl.BlockSpec((1, 1, d), _mod_row_map(n_x_tiles, tiles_per_batch, ctx_row, 4)),
            pl.BlockSpec((d, tn), lambda i, j: (0, j)),
        ],
        out_specs=pl.BlockSpec((tm, tn), lambda i, j: (i, j)),
        scratch_shapes=[pltpu.VMEM((tm, d), BF16)],
        compiler_params=_cparams(("parallel", "arbitrary")),
        name="ff1",
    )(xs, g, mod3, mod3, w)


def _ff2_kernel(h_ref, w_ref, x_ref, al_ref, o_ref):
    k = pl.program_id(2)
    part = jnp.dot(h_ref[...], w_ref[...], preferred_element_type=F32)

    @pl.when(k == 0)
    def _():
        o_ref[...] = part

    @pl.when(k > 0)
    def _():
        o_ref[...] += part

    @pl.when(k == pl.num_programs(2) - 1)
    def _():
        o_ref[...] = x_ref[...] + al_ref[0] * o_ref[...]


def _ff2(h, w, xs, mod3, tm, tn, tk, n_row_tiles, n_x_tiles, tiles_per_batch, ctx_row):
    dff, d = w.shape

    def alpha_map(i, j, k):
        r = jnp.where(i < n_x_tiles, i // tiles_per_batch, ctx_row)
        return (r * N_MOD + 5, 0, j)

    return pl.pallas_call(
        _ff2_kernel,
        out_shape=jax.ShapeDtypeStruct((n_row_tiles * tm, d), F32),
        grid=(n_row_tiles, d // tn, dff // tk),
        in_specs=[
            pl.BlockSpec((tm, tk), lambda i, j, k: (i, k)),
            pl.BlockSpec((tk, tn), lambda i, j, k: (k, j)),
            pl.BlockSpec((tm, tn), lambda i, j, k: (i, j)),
            pl.BlockSpec((1, 1, tn), alpha_map),
        ],
        out_specs=pl.BlockSpec((tm, tn), lambda i, j, k: (i, j)),
        compiler_params=_cparams(("parallel", "parallel", "arbitrary")),
        name="ff2",
    )(h, w, xs, mod3)


def _rope_partner():
    j = np.arange(MLA_ROPE)
    quarter = MLA_ROPE // 4
    return np.where((j // quarter) % 2 == 0, j + quarter, j - quarter)


def _rope_table(s, tm):
    pos = jnp.arange(s)
    row = (pos // GRID_W).astype(F32)
    col = (pos % GRID_W).astype(F32)
    n_freq = MLA_ROPE // 4
    inv_freq = ROPE_THETA ** (-jnp.arange(n_freq, dtype=F32) / n_freq)
    ang_r = row[:, None] * inv_freq
    ang_c = col[:, None] * inv_freq
    cos = jnp.concatenate([jnp.cos(ang_r)] * 2 + [jnp.cos(ang_c)] * 2, axis=-1)
    sin = jnp.concatenate([-jnp.sin(ang_r), jnp.sin(ang_r), -jnp.sin(ang_c), jnp.sin(ang_c)], axis=-1)
    ident = jnp.concatenate([jnp.ones((tm, MLA_ROPE), F32), jnp.zeros((tm, MLA_ROPE), F32)], axis=-1)
    return jnp.concatenate([jnp.concatenate([cos, sin], axis=-1), ident], axis=0)


def _pack_w_in(w_in, b_in, gate_b, sizes, n_pad):
    w_t = w_in.T
    d_model = w_in.shape[0]
    bounds = np.cumsum((0,) + sizes)
    seg = [slice(int(bounds[i]), int(bounds[i + 1])) for i in range(len(sizes))]
    partner = _rope_partner()
    order = (0, 1, 2, 3, 5, 6, 8, 9)
    names = ("q", "k", "v", "o", "qa", "kva", "u", "gates")
    offs, pos = {}, 0
    for name, i in zip(names, order):
        offs[name] = pos
        pos += sizes[i]
    offs["kpe"], offs["kpe_sw"] = pos, pos + MLA_ROPE
    n_zero = n_pad - pos - 2 * MLA_ROPE
    kpe_w, kpe_b = w_t[seg[7]], b_in[seg[7]]
    w = jnp.concatenate([w_t[seg[i]] for i in order] + [kpe_w, kpe_w[partner], jnp.zeros((n_zero, d_model), F32)],
                        axis=0).astype(BF16)
    b = jnp.concatenate([b_in[seg[i]] for i in order] + [kpe_b, kpe_b[partner], jnp.zeros((n_zero,), F32)])
    n_g = sizes[4]
    wg = jnp.concatenate([w_t[seg[4]], jnp.zeros((LANES - n_g, d_model), F32)], axis=0).astype(BF16)
    bg = jnp.concatenate([b_in[seg[4]] + gate_b.reshape(-1), jnp.zeros((LANES - n_g,), F32)])[None, :]
    return w, b[None, :], wg, bg, offs


def _pack_mla(w_uq, w_ukv, qn_g, kn_g):
    partner = _rope_partner()
    lora = w_uq.shape[0]
    wq = w_uq.reshape(lora, MLA_HEADS, MLA_DQK)
    wq = jnp.concatenate([wq, wq[:, :, MLA_NOPE + partner]], axis=-1).reshape(lora, MLA_HEADS * MLA_SLAB)
    wkv = w_ukv.reshape(w_ukv.shape[0], MLA_HEADS, -1)
    wkv = jnp.concatenate([wkv[:, :, :MLA_NOPE].reshape(lora, -1), wkv[:, :, MLA_NOPE:].reshape(lora, -1)], axis=-1)

    def gains(g):
        return jnp.concatenate([g, g[MLA_NOPE + partner]])[None, :]

    return wq.astype(BF16), wkv.astype(BF16), gains(qn_g), gains(kn_g)


def _pack_s5(a_re, a_im, log_dt, b_re, b_im, c_re, c_im):
    n_dir, n_groups, n_state = a_re.shape
    gc = b_re.shape[-1]
    nb = n_groups // S5_BLOCK_GROUPS
    lam_re = jnp.minimum(a_re.astype(F32), -1e-4)
    lam_im = a_im.astype(F32)
    dt = jnp.exp(log_dt.astype(F32))[..., None]

    def pole_power(k):
        mag = jnp.exp(k * lam_re * dt)
        return mag * jnp.cos(k * lam_im * dt), mag * jnp.sin(k * lam_im * dt)

    bar_re, bar_im = pole_power(1.0)
    den = lam_re * lam_re + lam_im * lam_im
    f_re = ((bar_re - 1.0) * lam_re + bar_im * lam_im) / den
    f_im = (bar_im * lam_re - (bar_re - 1.0) * lam_im) / den
    bb_re = f_re[..., None] * b_re.astype(F32) - f_im[..., None] * b_im.astype(F32)
    bb_im = f_re[..., None] * b_im.astype(F32) + f_im[..., None] * b_re.astype(F32)
    eye = jnp.eye(S5_BLOCK_GROUPS, dtype=F32)
    sub = S5_SUB

    def per_block(a):
        return a.reshape(a.shape[:-2] + (nb, S5_BLOCK_GROUPS * n_state))

    def block_b(part):
        p = part.reshape(n_dir, nb, S5_BLOCK_GROUPS, n_state, gc)
        m = jnp.einsum('dbgnc,gh->dbgchn', p, eye, precision=HIGHEST)
        return m.reshape(n_dir, nb, LANES, S5_HALF)

    def block_c(part):
        p = part.astype(F32).reshape(n_dir, nb, S5_BLOCK_GROUPS, gc, n_state)
        m = jnp.einsum('dbgcn,gh->dbgnhc', p, eye, precision=HIGHEST)
        return m.reshape(n_dir, nb, S5_HALF, LANES)

    bm_re, bm_im = block_b(bb_re), block_b(bb_im)
    cm_re, cm_im = block_c(c_re), block_c(c_im)

    tau = jnp.arange(sub + 1, dtype=F32)[:, None, None, None]
    p_re, p_im = (per_block(p) for p in pole_power(tau))
    pr, pi = p_re[:sub, :, :, None, :], p_im[:sub, :, :, None, :]
    taps = (jnp.einsum('zdbkn,dbnc->zdbkc', bm_re * pr - bm_im * pi, cm_re, precision=HIGHEST)
            - jnp.einsum('zdbkn,dbnc->zdbkc', bm_re * pi + bm_im * pr, cm_im, precision=HIGHEST))

    ct_re, ct_im = jnp.swapaxes(cm_re, -1, -2), jnp.swapaxes(cm_im, -1, -2)
    cat = functools.partial(jnp.concatenate, axis=-1)
    fac = jnp.stack([cat([bm_re, bm_im]), cat([-bm_im, bm_re]), cat([ct_re, -ct_im]), cat([-ct_im, -ct_re])],
                    axis=2)
    t = np.arange(sub)
    zero_taps = jnp.zeros((sub - 1,) + taps.shape[2:], F32)
    pws, taps2 = [], []
    for d in range(n_dir):
        to_exit = (sub - 1 - t) if d == 0 else t
        age = (t + 1) if d == 0 else (sub - t)
        rows = [p_re[to_exit, d], p_im[to_exit, d], p_re[age, d], p_im[age, d]]
        pws.append(jnp.stack([cat([a, a]).transpose(1, 0, 2) for a in rows], axis=1))
        taps2.append(jnp.concatenate([zero_taps, taps[:, 0]] if d == 0 else [taps[::-1, 1], zero_taps], axis=0))
    pw = jnp.stack(pws)
    taps2 = jnp.stack(taps2).transpose(0, 2, 1, 3, 4)

    rows = jnp.arange(SUBLANES)
    tabs = []
    for d in range(n_dir):
        per_dir = []
        for shift in (1, 2, 4):
            keep = ((rows <= SUBLANES - 1 - shift) if d == 1 else (rows >= shift))[None, :, None]
            s_re, s_im = pole_power(float(shift * sub))
            per_dir += [jnp.where(keep, per_block(s_re[d])[:, None, :], 0.0),
                        jnp.where(keep, per_block(s_im[d])[:, None, :], 0.0)]
        expo = (((SUBLANES - rows) if d == 1 else (rows + 1)) * sub).astype(F32)
        s_re, s_im = pole_power(expo[:, None, None, None])
        per_dir += [jnp.moveaxis(per_block(s_re[:, d]), 0, 1), jnp.moveaxis(per_block(s_im[:, d]), 0, 1)]
        tabs.append(jnp.stack(per_dir, axis=1))
    return fac, pw, taps2, jnp.stack(tabs).astype(F32)


def _tri_matrices():
    t = np.arange(ML_CHUNK)
    lower = t[None, :] <= t[:, None]
    mask = np.where(np.stack([lower, lower.T]), 0.0, NEG_BIG).astype(np.float32)
    return jnp.asarray(lower.astype(np.float32), dtype=BF16), jnp.asarray(mask)


def kernel(x, c, ctx, c_ctx, w_mod, b_mod, norm_g, w_in, b_in, ml_gate_b, ml_norm_g, mla_qa_g, mla_kva_g, mla_w_uq, mla_w_ukv, mla_qn_g, mla_kn_g, s5_a_re, s5_a_im, s5_log_dt, s5_b_re, s5_b_im, s5_c_re, s5_c_im, s5_d, s5_w_glu, s5_b_glu, w_branch, w_out, w_ff1, w_ff2):
    b, s, d = x.shape
    tc = ctx.shape[1]
    depth = w_mod.shape[0]
    dv = ml_norm_g.shape[2]
    dk = dv // 2
    lora = mla_qa_g.shape[1]
    s5_width = s5_d.shape[1]
    branch_w = w_branch.shape[2]
    sizes = (ML_HEADS * dk, ML_HEADS * dk, ML_HEADS * dv, ML_HEADS * dv, 4 * ML_HEADS, lora, lora, MLA_ROPE,
             s5_width, N_BRANCH * d)
    assert sum(sizes) == w_in.shape[2] and b + 1 <= SUBLANES
    assert s % ML_CHUNK == 0 and tc % ML_CHUNK == 0 and branch_w == ML_HEADS * dv == MLA_HEADS * LANES == s5_width

    n_x = b * s
    n_c = b * tc
    tm = _row_tile(s, n_c)
    n_x_tiles = n_x // tm
    n_tiles = n_x_tiles + n_c // tm
    tiles_per_batch = s // tm
    tile_args = (n_x_tiles, tiles_per_batch, b)

    xs = jnp.concatenate([x.reshape(n_x, d), ctx.reshape(n_c, d)], axis=0)
    cc = jnp.concatenate([c, c_ctx[None, :], jnp.zeros((SUBLANES - b - 1, d), F32)], axis=0)
    mod = _modulation(cc, w_mod, b_mod)
    tm_q = min(tm, 512)
    tab = _rope_table(s, tm_q)
    tri, ml_mask = _tri_matrices()
    n_used = sum(sizes) - sizes[4] + MLA_ROPE
    tn_in = 1280
    n_pad = -(-n_used // tn_in) * tn_in

    for l in range(depth):
        with_ctx_out = l < depth - 1
        mod3 = mod[l].reshape(SUBLANES * N_MOD, 1, d)
        w_p, b_p, wg, bg, offs = _pack_w_in(w_in[l], b_in[l], ml_gate_b[l], sizes, n_pad)
        z, gz = _in_proj(xs, norm_g[l, 0][None, :], mod3, w_p, b_p, wg, bg, tm, tn_in, *tile_args)

        gates, gates_t = _gate_prep(gz, tri)
        a_x, a_c = _mlstm(z, gates, gates_t, ml_mask, ml_norm_g[l], b, s, tc, dk, dv,
                          (offs["q"], offs["k"], offs["v"], offs["o"]), with_ctx_out)

        wq, wkv, gq, gk = _pack_mla(mla_w_uq[l], mla_w_ukv[l], mla_qn_g[l], mla_kn_g[l])
        qo, ko, vo = _mla_proj(z, tab, mla_qa_g[l][None, :], mla_kva_g[l][None, :], wq, wkv, gq, gk, tm_q,
                               (offs["qa"], offs["kva"], offs["kpe"]), n_x // tm_q, s // tm_q)
        tq = min(512, s)
        b_x = _attention(qo, ko, vo, b, s, tc, tq, True)

        fac, pw, taps, tabs = _pack_s5(s5_a_re[l], s5_a_im[l], s5_log_dt[l], s5_b_re[l], s5_b_im[l],
                                       s5_c_re[l], s5_c_im[l])
        y = _s5_scan(z, fac, pw, taps, tabs, b, s, tc, offs["u"])
        n_out_tiles = n_tiles if with_ctx_out else n_x_tiles
        c_all = _s5_glu(y, z, s5_d[l][None, :], s5_w_glu[l].astype(BF16), s5_b_glu[l][None, :], tm, offs["u"],
                        n_out_tiles)

        if with_ctx_out:
            b_c = _attention(qo, ko, vo, b, s, tc, min(tq, tc), False)
        else:
            a_c, b_c = a_x, b_x
        merged = _merge(a_x, a_c, b_x, b_c, c_all, z, w_branch[l].astype(BF16), tm, 512, offs["gates"],
                        n_out_tiles, n_x_tiles)
        xs1 = _out_proj_residual(merged, w_out[l].astype(BF16), xs, mod3, tm, 1024, n_out_tiles, *tile_args, 2)
        hid = _ff1(xs1, norm_g[l, 1][None, :], mod3, w_ff1[l].astype(BF16), tm, 1024, n_out_tiles, *tile_args)
        xs = _ff2(hid, w_ff2[l].astype(BF16), xs1, mod3, tm, 1024, 2048, n_out_tiles, *tile_args)

    return xs.reshape(b, s, d)
```

```python
import functools
import math

import jax
import jax.numpy as jnp
import numpy as np
from jax import lax
from jax.experimental import pallas as pl
from jax.experimental.pallas import tpu as pltpu

F32 = jnp.float32
BF16 = jnp.bfloat16
HIGHEST = lax.Precision.HIGHEST

N_MOD = 6
N_BRANCH = 3
ML_HEADS = 4
MLA_HEADS = 8
MLA_NOPE = 128
MLA_ROPE = 64
MLA_DQK = MLA_NOPE + MLA_ROPE
MLA_SLAB = 256
GRID_W = 64
ROPE_THETA = 10000.0
S5_GROUP = 16
S5_STATE = 64
S5_BLOCK_GROUPS = 8
EPS = 1e-6
NEG_BIG = -1e30

LANES = 128
SUBLANES = 8
VMEM_LIMIT = 56 * 1024 * 1024

ML_CHUNK = 256
ML_GATE_ROWS = 32
S5_SUB = 8
ATTN_KEY_CHUNK = 512
Q_PRESCALE = MLA_DQK ** -0.5 * math.log2(math.e)
ROW_CHUNK = 64


def _cparams(sem):
    return pltpu.CompilerParams(dimension_semantics=sem, vmem_limit_bytes=VMEM_LIMIT)


def _row_tile(n_x_rows_per_batch, n_ctx_rows):
    tm = 1024
    while n_x_rows_per_batch % tm or n_ctx_rows % tm:
        tm //= 2
    return tm


def _mod_kernel(c_ref, w_ref, b_ref, o_ref):
    s = c_ref[...]
    s = s * jax.nn.sigmoid(s)
    o_ref[0] = jnp.dot(s.astype(BF16), w_ref[0].astype(BF16), preferred_element_type=F32) + b_ref[0]


def _modulation(cc, w_mod, b_mod):
    n_layers, d, n = w_mod.shape
    tn = 1024
    return pl.pallas_call(
        _mod_kernel,
        out_shape=jax.ShapeDtypeStruct((n_layers, SUBLANES, n), F32),
        grid=(n_layers, n // tn),
        in_specs=[
            pl.BlockSpec((SUBLANES, d), lambda l, j: (0, 0)),
            pl.BlockSpec((1, d, tn), lambda l, j: (l, 0, j)),
            pl.BlockSpec((1, 1, tn), lambda l, j: (l, 0, j)),
        ],
        out_specs=pl.BlockSpec((1, SUBLANES, tn), lambda l, j: (l, 0, j)),
        compiler_params=_cparams(("parallel", "parallel")),
        name="adaln_mod",
    )(cc, w_mod, b_mod.reshape(n_layers, 1, n))


def _norm_mod_rows(x_ref, g_ref, sh_ref, sc_ref, xn_ref):
    tm = x_ref.shape[0]
    g = g_ref[...]
    sc = 1.0 + sc_ref[0]
    sh = sh_ref[0]

    def body(r, carry):
        rows = pl.ds(pl.multiple_of(r * ROW_CHUNK, ROW_CHUNK), ROW_CHUNK)
        x = x_ref[rows, :]
        ms = jnp.mean(x * x, axis=-1, keepdims=True)
        y = x * lax.rsqrt(ms + EPS) * g
        xn_ref[rows, :] = (y * sc + sh).astype(BF16)
        return carry

    lax.fori_loop(0, tm // ROW_CHUNK, body, 0)


def _mod_row_map(n_x_tiles, tiles_per_batch, ctx_row, k):
    def index_map(i, j):
        r = jnp.where(i < n_x_tiles, i // tiles_per_batch, ctx_row)
        return (r * N_MOD + k, 0, 0)
    return index_map


def _stream_specs(block, n_x_tiles, ctx_tile0, col_map):
    x_spec = pl.BlockSpec(block, lambda i, *r: (jnp.minimum(i, n_x_tiles - 1), col_map(*r)))
    c_spec = pl.BlockSpec(block, lambda i, *r: (ctx_tile0 + jnp.maximum(i - n_x_tiles, 0), col_map(*r)),
                          pipeline_mode=pl.Buffered(1))
    return x_spec, c_spec


def _dot_nt(a, b_t):
    return lax.dot_general(a, b_t, (((1,), (1,)), ((), ())), preferred_element_type=F32)


def _in_kernel(x_ref, g_ref, sh_ref, sc_ref, w_ref, b_ref, wg_ref, bg_ref, z_ref, gz_ref, xn_ref):
    @pl.when(pl.program_id(1) == 0)
    def _():
        _norm_mod_rows(x_ref, g_ref, sh_ref, sc_ref, xn_ref)
        gz_ref[...] = _dot_nt(xn_ref[...], wg_ref[...]) + bg_ref[...]

    z_ref[...] = (_dot_nt(xn_ref[...], w_ref[...]) + b_ref[...]).astype(BF16)


def _in_proj(xs, g, mod3, w, b, wg, bg, tm, tn, n_x_tiles, tiles_per_batch, ctx_row):
    r, d = xs.shape
    nz = w.shape[0]
    return pl.pallas_call(
        _in_kernel,
        out_shape=(jax.ShapeDtypeStruct((r, nz), BF16), jax.ShapeDtypeStruct((r, LANES), F32)),
        grid=(r // tm, nz // tn),
        in_specs=[
            pl.BlockSpec((tm, d), lambda i, j: (i, 0)),
            pl.BlockSpec((1, d), lambda i, j: (0, 0)),
            pl.BlockSpec((1, 1, d), _mod_row_map(n_x_tiles, tiles_per_batch, ctx_row, 0)),
            pl.BlockSpec((1, 1, d), _mod_row_map(n_x_tiles, tiles_per_batch, ctx_row, 1)),
            pl.BlockSpec((tn, d), lambda i, j: (j, 0)),
            pl.BlockSpec((1, tn), lambda i, j: (0, j)),
            pl.BlockSpec((LANES, d), lambda i, j: (0, 0)),
            pl.BlockSpec((1, LANES), lambda i, j: (0, 0)),
        ],
        out_specs=(
            pl.BlockSpec((tm, tn), lambda i, j: (i, j)),
            pl.BlockSpec((tm, LANES), lambda i, j: (i, 0)),
        ),
        scratch_shapes=[pltpu.VMEM((tm, d), BF16)],
        compiler_params=_cparams(("parallel", "arbitrary")),
        name="in_proj",
    )(xs, g, mod3, mod3, w, b, wg, bg)


def _log_sigmoid(x):
    return jnp.minimum(x, 0.0) - jnp.log1p(jnp.exp(-jnp.abs(x)))


def _split3(a):
    hi = a.astype(BF16)
    r1 = a - hi.astype(F32)
    mid = r1.astype(BF16)
    lo = (r1 - mid.astype(F32)).astype(BF16)
    return hi, mid, lo


def _gate_prep_kernel(g_ref, tri_ref, a_ref, at_ref):
    g = g_ref[...]
    kind = lax.broadcasted_iota(jnp.int32, g.shape, 1) // ML_HEADS
    lg = jnp.where((kind == 1) | (kind == 3), _log_sigmoid(g), g)
    pre = sum(jnp.dot(tri_ref[...], p, preferred_element_type=F32) for p in _split3(lg))
    suf = pre[ML_CHUNK - 1:ML_CHUNK, :] - pre + lg
    a = jnp.where(kind == 4, pltpu.roll(pre, 3 * ML_HEADS, axis=1),
                  jnp.where(kind == 5, pltpu.roll(suf, 2 * ML_HEADS, axis=1), lg))
    a_ref[...] = a
    at_ref[0] = a.T[:ML_GATE_ROWS, :]


def _gate_prep(gz, tri):
    r = gz.shape[0]
    return pl.pallas_call(
        _gate_prep_kernel,
        out_shape=(jax.ShapeDtypeStruct((r, LANES), F32),
                   jax.ShapeDtypeStruct((r // ML_CHUNK, ML_GATE_ROWS, ML_CHUNK), F32)),
        grid=(r // ML_CHUNK,),
        in_specs=[pl.BlockSpec((ML_CHUNK, LANES), lambda i: (i, 0)),
                  pl.BlockSpec((ML_CHUNK, ML_CHUNK), lambda i: (0, 0))],
        out_specs=(pl.BlockSpec((ML_CHUNK, LANES), lambda i: (i, 0)),
                   pl.BlockSpec((1, ML_GATE_ROWS, ML_CHUNK), lambda i: (i, 0, 0))),
        compiler_params=_cparams(("parallel",)),
        name="mlstm_gates",
    )(gz, tri)


def _ml_chunk(q, k_t, v, cum_cb, li_r, lf_r, cum_r, mask_add, carry, inv_scale):
    c_mat, n_mat, m = carry
    length, dv = v.shape
    crow = cum_r - li_r
    total = jnp.sum(lf_r, axis=-1, keepdims=True)
    ones = jnp.ones((length, LANES), BF16)
    n_blk = length // LANES

    def wide(a):
        return jnp.concatenate([a] * (dv // LANES), axis=1)

    log_w = [cum_cb - crow[:, j * LANES:(j + 1) * LANES] + mask_add[:, j * LANES:(j + 1) * LANES]
             for j in range(n_blk)]
    row_max = jnp.max(functools.reduce(jnp.maximum, log_w), axis=-1, keepdims=True)
    log_inter = cum_cb + m
    m_t = jnp.maximum(log_inter, row_max)
    w_inter = jnp.exp(log_inter - m_t)
    qk = jnp.dot(q, k_t, preferred_element_type=F32)
    s = jnp.concatenate([qk[:, j * LANES:(j + 1) * LANES] * jnp.exp(log_w[j] - m_t) for j in range(n_blk)],
                        axis=1).astype(BF16)
    num = wide(w_inter) * jnp.dot(q, c_mat.astype(BF16), preferred_element_type=F32)
    num = num + jnp.dot(s, v, preferred_element_type=F32)
    den = w_inter * jnp.dot(q, n_mat.astype(BF16), preferred_element_type=F32)
    den = den + jnp.dot(s, ones, preferred_element_type=F32)
    h = num * wide(1.0 / jnp.maximum(jnp.abs(den), jnp.exp(-m_t) * inv_scale))

    log_end = total - crow
    m_new = jnp.maximum(total + m, jnp.max(log_end, axis=-1, keepdims=True))
    decay = jnp.exp(total + m - m_new)
    kw_t = (k_t.astype(F32) * jnp.exp(log_end - m_new)).astype(BF16)
    c_new = decay * c_mat + jnp.dot(kw_t, v, preferred_element_type=F32)
    n_new = decay * n_mat + jnp.dot(kw_t, ones, preferred_element_type=F32)
    return h, (c_new, n_new, m_new)


def _mlstm_kernel(with_ctx_out, qx, kx, vx, ox, qc, kc, vc, oc, gx, gc, gtx, gtc, mask_ref, ng_ref, *rest):
    if with_ctx_out:
        ax_ref, ac_ref, cumx, ktx, cumc, ktc, hx, hc = rest
    else:
        ax_ref, cumx, ktx, cumc, ktc, hx, hc = rest
        ac_ref = None
    head = pl.program_id(1)
    dk = qx.shape[1]
    inv_scale = float(dk) ** 0.5
    n_x_chunks = qx.shape[0] // ML_CHUNK
    n_c_chunks = qc.shape[0] // ML_CHUNK

    def prep(g_ref, k_ref, cum_ref, kt_ref):
        def body(i, carry):
            rows = pl.ds(pl.multiple_of(i * ML_CHUNK, ML_CHUNK), ML_CHUNK)
            a = g_ref[rows, :]
            col = lax.broadcasted_iota(jnp.int32, a.shape, 1)
            for d in range(2):
                pick = jnp.sum(jnp.where(col == (4 + d) * ML_HEADS + head, a, 0.0), axis=-1, keepdims=True)
                cum_ref[d, rows, :] = jnp.broadcast_to(pick, a.shape)
            kt_ref[i] = k_ref[rows, :].astype(F32).T.astype(BF16)
            return carry
        n_chunks = g_ref.shape[0] // ML_CHUNK
        lax.fori_loop(0, n_chunks, body, 0, unroll=2 if n_chunks % 2 == 0 else 1)

    prep(gx, kx, cumx, ktx)
    prep(gc, kc, cumc, ktc)

    for d in range(2):
        reverse = d == 1
        mask_add = mask_ref[d]

        def step(q_ref, v_ref, cum_ref, gt_ref, kt_ref, h_ref, ci, carry):
            rows = pl.ds(pl.multiple_of(ci * ML_CHUNK, ML_CHUNK), ML_CHUNK)
            li_r, lf_r, cum_r = (gt_ref[ci, pl.ds(kind * ML_HEADS + head, 1), :] for kind in (2 * d, 2 * d + 1, 4 + d))
            h, carry = _ml_chunk(q_ref[rows, :], kt_ref[ci], v_ref[rows, :], cum_ref[d, rows, :], li_r, lf_r, cum_r,
                                 mask_add, carry, inv_scale)
            if reverse:
                h_ref[rows, :] += h
            else:
                h_ref[rows, :] = h
            return carry

        carry = (jnp.zeros((dk, vx.shape[1]), F32), jnp.zeros((dk, LANES), F32), jnp.zeros((1, 1), F32))

        def ctx_body(i, carry):
            ci = (n_c_chunks - 1 - i) if reverse else i
            return step(qc, vc, cumc, gtc, ktc, hc, ci, carry)

        def x_body(i, carry):
            ci = (n_x_chunks - 1 - i) if reverse else i
            return step(qx, vx, cumx, gtx, ktx, hx, ci, carry)

        carry = lax.fori_loop(0, n_c_chunks, ctx_body, carry)
        lax.fori_loop(0, n_x_chunks, x_body, carry, unroll=2 if n_x_chunks % 2 == 0 else 1)

    ng = ng_ref[0]

    def finish(h_ref, o_ref, a_ref):
        def body(i, carry):
            rows = pl.ds(pl.multiple_of(i * ML_CHUNK, ML_CHUNK), ML_CHUNK)
            h = h_ref[rows, :]
            hn = h * lax.rsqrt(jnp.mean(h * h, axis=-1, keepdims=True) + EPS) * ng
            a_ref[rows, :] = (hn * jax.nn.sigmoid(o_ref[rows, :].astype(F32))).astype(BF16)
            return carry
        lax.fori_loop(0, h_ref.shape[0] // ML_CHUNK, body, 0)

    finish(hx, ox, ax_ref)
    if with_ctx_out:
        finish(hc, oc, ac_ref)


def _mlstm(z, gates, gates_t, mask, ml_norm_g, b, s, tc, dk, dv, cols, with_ctx_out):
    n_x = b * s
    cq, ck, cv, co = cols
    ctx0 = n_x // tc
    xc, cc = s // ML_CHUNK, tc // ML_CHUNK

    def xspec(width, col0):
        return pl.BlockSpec((s, width), lambda i, h: (i, col0 // width + h))

    def cspec(width, col0):
        return pl.BlockSpec((tc, width), lambda i, h: (ctx0 + i, col0 // width + h))

    out_shape = [jax.ShapeDtypeStruct((n_x, ML_HEADS * dv), BF16)]
    out_specs = [pl.BlockSpec((s, dv), lambda i, h: (i, h))]
    if with_ctx_out:
        out_shape.append(jax.ShapeDtypeStruct((b * tc, ML_HEADS * dv), BF16))
        out_specs.append(pl.BlockSpec((tc, dv), lambda i, h: (i, h)))
    res = pl.pallas_call(
        functools.partial(_mlstm_kernel, with_ctx_out),
        out_shape=tuple(out_shape),
        grid=(b, ML_HEADS),
        in_specs=[
            xspec(dk, cq), xspec(dk, ck), xspec(dv, cv), xspec(dv, co),
            cspec(dk, cq), cspec(dk, ck), cspec(dv, cv), cspec(dv, co),
            pl.BlockSpec((s, LANES), lambda i, h: (i, 0)),
            pl.BlockSpec((tc, LANES), lambda i, h: (ctx0 + i, 0)),
            pl.BlockSpec((xc, ML_GATE_ROWS, ML_CHUNK), lambda i, h: (i, 0, 0)),
            pl.BlockSpec((cc, ML_GATE_ROWS, ML_CHUNK), lambda i, h: (b * xc // cc + i, 0, 0)),
            pl.BlockSpec((2, ML_CHUNK, ML_CHUNK), lambda i, h: (0, 0, 0)),
            pl.BlockSpec((1, 1, dv), lambda i, h: (h, 0, 0)),
        ],
        out_specs=tuple(out_specs),
        scratch_shapes=[
            pltpu.VMEM((2, s, LANES), F32), pltpu.VMEM((xc, dk, ML_CHUNK), BF16),
            pltpu.VMEM((2, tc, LANES), F32), pltpu.VMEM((cc, dk, ML_CHUNK), BF16),
            pltpu.VMEM((s, dv), F32), pltpu.VMEM((tc, dv), F32),
        ],
        compiler_params=_cparams(("parallel", "parallel")),
        name="mlstm",
    )(z, z, z, z, z, z, z, z, gates, gates, gates_t, gates_t, mask, ml_norm_g.reshape(ML_HEADS, 1, dv))
    return res if with_ctx_out else (res[0], None)


def _mla_proj_kernel(qa_ref, kva_ref, kpe_ref, tab_ref, qag_ref, kvag_ref, wq_ref, wkv_ref,
                     gq_ref, gk_ref, q_ref, k_ref, v_ref):
    def normed(a_ref, g_ref):
        a = a_ref[...].astype(F32)
        return (a * lax.rsqrt(jnp.mean(a * a, axis=-1, keepdims=True) + EPS) * g_ref[...]).astype(BF16)

    q_all = jnp.dot(normed(qa_ref, qag_ref), wq_ref[...], preferred_element_type=F32)
    kv_all = jnp.dot(normed(kva_ref, kvag_ref), wkv_ref[...], preferred_element_type=F32)
    tab = tab_ref[...]
    lane = lax.broadcasted_iota(jnp.int32, tab.shape, 1)
    first_half = lane < MLA_ROPE
    gq = gq_ref[...]
    gk = gk_ref[...]
    inv_dqk = 1.0 / MLA_DQK

    kpe = kpe_ref[...].astype(F32)
    ss_kpe = jnp.sum(jnp.where(first_half, kpe * kpe, 0.0), axis=-1, keepdims=True)
    kpe_t = kpe * (tab * gk[:, LANES:])
    kpe_rot = jnp.where(first_half, kpe_t + pltpu.roll(kpe_t, MLA_ROPE, axis=1), 0.0)

    for h in range(MLA_HEADS):
        qn = q_all[:, h * MLA_SLAB:h * MLA_SLAB + LANES]
        qp = q_all[:, h * MLA_SLAB + LANES:(h + 1) * MLA_SLAB]
        ss = jnp.sum(qn * qn, axis=-1, keepdims=True) + jnp.sum(jnp.where(first_half, qp * qp, 0.0), axis=-1,
                                                                keepdims=True)
        r = lax.rsqrt(ss * inv_dqk + EPS) * Q_PRESCALE
        qp_t = qp * (tab * gq[:, LANES:])
        qp_rot = qp_t + pltpu.roll(qp_t, MLA_ROPE, axis=1)
        q_ref[:, h * MLA_SLAB:h * MLA_SLAB + LANES] = (qn * r * gq[:, :LANES]).astype(BF16)
        q_ref[:, h * MLA_SLAB + LANES:(h + 1) * MLA_SLAB] = (qp_rot * r).astype(BF16)

        kn = kv_all[:, h * LANES:(h + 1) * LANES]
        rk = lax.rsqrt((jnp.sum(kn * kn, axis=-1, keepdims=True) + ss_kpe) * inv_dqk + EPS)
        k_ref[:, h * MLA_SLAB:h * MLA_SLAB + LANES] = (kn * rk * gk[:, :LANES]).astype(BF16)
        k_ref[:, h * MLA_SLAB + LANES:(h + 1) * MLA_SLAB] = (kpe_rot * rk).astype(BF16)

    ones_col = jnp.where(lane == 0, 1.0, 0.0).astype(BF16)
    for h in range(MLA_HEADS):
        v_ref[:, h * MLA_SLAB:h * MLA_SLAB + LANES] = kv_all[:, (MLA_HEADS + h) * LANES:(MLA_HEADS + h + 1) * LANES
                                                             ].astype(BF16)
        v_ref[:, h * MLA_SLAB + LANES:(h + 1) * MLA_SLAB] = ones_col


def _mla_proj(z, tab, qag, kvag, wq, wkv, gq, gk, tm, cols, n_x_tiles, tab_tiles):
    r = z.shape[0]
    cqa, ckva, ckpe = cols
    lora = qag.shape[1]
    hs = MLA_HEADS * MLA_SLAB
    hv = MLA_HEADS * LANES
    return pl.pallas_call(
        _mla_proj_kernel,
        out_shape=(jax.ShapeDtypeStruct((r, hs), BF16), jax.ShapeDtypeStruct((r, hs), BF16),
                   jax.ShapeDtypeStruct((r, hs), BF16)),
        grid=(r // tm,),
        in_specs=[
            pl.BlockSpec((tm, lora), lambda i: (i, cqa // lora)),
            pl.BlockSpec((tm, lora), lambda i: (i, ckva // lora)),
            pl.BlockSpec((tm, LANES), lambda i: (i, ckpe // LANES)),
            pl.BlockSpec((tm, LANES), lambda i: (jnp.where(i < n_x_tiles, i % tab_tiles, tab_tiles), 0)),
            pl.BlockSpec((1, lora), lambda i: (0, 0)),
            pl.BlockSpec((1, lora), lambda i: (0, 0)),
            pl.BlockSpec((lora, hs), lambda i: (0, 0)),
            pl.BlockSpec((lora, 2 * hv), lambda i: (0, 0)),
            pl.BlockSpec((1, MLA_SLAB), lambda i: (0, 0)),
            pl.BlockSpec((1, MLA_SLAB), lambda i: (0, 0)),
        ],
        out_specs=(pl.BlockSpec((tm, hs), lambda i: (i, 0)), pl.BlockSpec((tm, hs), lambda i: (i, 0)),
                   pl.BlockSpec((tm, hs), lambda i: (i, 0))),
        compiler_params=_cparams(("parallel",)),
        name="mla_qkv",
    )(z, z, z, tab, qag, kvag, wq, wkv, gq, gk)


def _attn_kernel(n_kv, tq, q_ref, *refs):
    k_refs = refs[:n_kv]
    v_refs = refs[n_kv:2 * n_kv]
    o_ref, s0_ref, s1_ref, m0_ref, m1_ref = refs[2 * n_kv:]
    slots = ((s0_ref, m0_ref), (s1_ref, m1_ref))
    n_tiles = q_ref.shape[0] // tq
    chunks = []
    col = 0
    for kv, k_ref in enumerate(k_refs):
        n_keys = k_ref.shape[0]
        step = min(ATTN_KEY_CHUNK, n_keys)
        for off in range(0, n_keys, step):
            chunks.append((kv, off, col, step))
            col += step

    def scores(t, slot):
        s_ref, m_ref = slots[slot]
        rows = pl.ds(pl.multiple_of(t * tq, tq), tq)
        q = q_ref[rows, :]
        run = None
        for kv, off, c0, size in chunks:
            s = lax.dot_general(q, k_refs[kv][off:off + size, :], (((1,), (1,)), ((), ())),
                                preferred_element_type=F32)
            s_ref[:, c0:c0 + size] = s
            for lb in range(size // LANES):
                blk = s[:, lb * LANES:(lb + 1) * LANES]
                run = blk if run is None else jnp.maximum(run, blk)
        m_ref[...] = run

    def finish(t, slot):
        s_ref, m_ref = slots[slot]
        rows = pl.ds(pl.multiple_of(t * tq, tq), tq)
        m = jnp.max(m_ref[...], axis=-1, keepdims=True)
        acc = None
        for kv, off, c0, size in chunks:
            p = jnp.exp2(s_ref[:, c0:c0 + size] - m).astype(BF16)
            pv = jnp.dot(p, v_refs[kv][off:off + size, :], preferred_element_type=F32)
            acc = pv if acc is None else acc + pv
        o_ref[rows, :] = (acc[:, :LANES] / acc[:, LANES:LANES + 1]).astype(BF16)

    scores(0, 0)

    def body(k, carry):
        scores(2 * k + 1, 1)
        finish(2 * k, 0)
        scores(jnp.minimum(2 * k + 2, n_tiles - 1), 0)
        finish(2 * k + 1, 1)
        return carry

    lax.fori_loop(0, n_tiles // 2, body, 0)
    if n_tiles % 2:
        finish(n_tiles - 1, 0)


def _attention(qo, ko, vo, b, s, tc, tq, latent):
    n_x = b * s
    ctx0 = n_x // tc
    cspec = pl.BlockSpec((tc, MLA_SLAB), lambda i, h: (ctx0 + i, h))
    xspec = pl.BlockSpec((s, MLA_SLAB), lambda i, h: (i, h))
    if latent:
        n_q, n_keys = s, s + tc
        in_specs = [xspec, cspec, xspec, cspec, xspec]
        args = (qo, ko, ko, vo, vo)
    else:
        n_q, n_keys = tc, tc
        in_specs = [cspec, cspec, cspec]
        args = (qo, ko, vo)
    return pl.pallas_call(
        functools.partial(_attn_kernel, (len(args) - 1) // 2, tq),
        out_shape=jax.ShapeDtypeStruct((b * n_q, MLA_HEADS * LANES), BF16),
        grid=(b, MLA_HEADS),
        in_specs=in_specs,
        out_specs=pl.BlockSpec((n_q, LANES), lambda i, h: (i, h)),
        scratch_shapes=[pltpu.VMEM((tq, n_keys), F32), pltpu.VMEM((tq, n_keys), F32),
                        pltpu.VMEM((tq, LANES), F32), pltpu.VMEM((tq, LANES), F32)],
        compiler_params=_cparams(("parallel", "parallel")),
        name="attn_latent" if latent else "attn_ctx",
    )(*args)


S5_HALF = S5_BLOCK_GROUPS * S5_STATE


def _s5_kernel(n_batch, rows_x, rows_c, dot_rows, u_ref, fac_ref, pw_ref, tap_ref, tab_ref, y_ref,
               u2_ref, v_ref, r_ref, m_ref, ot_ref):
    d = pl.program_id(1)
    n_dot = u2_ref.shape[0] // dot_rows

    for s in range(S5_SUB):
        blk = slice(s * LANES, (s + 1) * LANES)
        r_ref[blk, :] = (fac_ref[0, 0, 0] * pw_ref[0, 0, 0, s:s + 1, :]
                         + fac_ref[0, 0, 1] * pw_ref[0, 0, 1, s:s + 1, :]).astype(BF16)
        ot_ref[blk, :] = (fac_ref[0, 0, 2] * pw_ref[0, 0, 2, s:s + 1, :]
                          + fac_ref[0, 0, 3] * pw_ref[0, 0, 3, s:s + 1, :]).astype(BF16)
        for t in range(S5_SUB):
            m_ref[blk, t * LANES:(t + 1) * LANES] = tap_ref[0, 0, t - s + S5_SUB - 1].astype(BF16)

    def dot_rows_of(i):
        return pl.ds(pl.multiple_of(i * dot_rows, dot_rows), dot_rows)

    def token_rows_of(i, s):
        return pl.ds(i * (dot_rows * S5_SUB) + s, dot_rows, stride=S5_SUB)

    @pl.when(d == 0)
    def _():
        def stage(i, carry):
            rows = pl.ds(pl.multiple_of(i * dot_rows, dot_rows), dot_rows)
            y_ref[rows, :] = u_ref[rows, :].astype(F32)
            return carry

        lax.fori_loop(0, n_dot * S5_SUB, stage, 0)

        def regroup(i, carry):
            for s in range(S5_SUB):
                u2_ref[dot_rows_of(i), s * LANES:(s + 1) * LANES] = y_ref[token_rows_of(i, s), :].astype(BF16)
            return carry

        lax.fori_loop(0, n_dot, regroup, 0)

    def increments(i, carry):
        rows = dot_rows_of(i)
        v_ref[rows, :] = jnp.dot(u2_ref[rows, :], r_ref[...], preferred_element_type=F32)
        return carry

    lax.fori_loop(0, n_dot, increments, 0)

    def cmul_add(ar, ai, cr, ci, xr, xi):
        return ar + cr * xr - ci * xi, ai + cr * xi + ci * xr

    def run(reverse):
        tab = tab_ref.at[0, 0]
        last = 0 if reverse else SUBLANES - 1
        first_row = lax.broadcasted_iota(jnp.int32, (SUBLANES, S5_HALF), 0) == (SUBLANES - 1 - last)

        def segment(bases, n_groups, carry):
            def body(i, carry):
                gi = (n_groups - 1 - i) if reverse else i
                out = []
                for base, (cre, cim) in zip(bases, carry):
                    rows = pl.ds(pl.multiple_of(base + gi * SUBLANES, SUBLANES), SUBLANES)
                    re = v_ref[rows, :S5_HALF]
                    im = v_ref[rows, S5_HALF:]
                    for lvl, shift in enumerate((1, 2, 4)):
                        sh = (SUBLANES - shift) if reverse else shift
                        re, im = cmul_add(re, im, tab[2 * lvl], tab[2 * lvl + 1],
                                          pltpu.roll(re, sh, axis=0), pltpu.roll(im, sh, axis=0))
                    re, im = cmul_add(re, im, tab[6], tab[7], cre, cim)
                    sh1 = (SUBLANES - 1) if reverse else 1
                    v_ref[rows, :S5_HALF] = jnp.where(first_row, cre, pltpu.roll(re, sh1, axis=0))
                    v_ref[rows, S5_HALF:] = jnp.where(first_row, cim, pltpu.roll(im, sh1, axis=0))
                    out.append((jnp.broadcast_to(re[last:last + 1, :], re.shape),
                                jnp.broadcast_to(im[last:last + 1, :], im.shape)))
                return tuple(out)
            return lax.fori_loop(0, n_groups, body, carry)

        zero = jnp.zeros((SUBLANES, S5_HALF), F32)
        carry = tuple((zero, zero) for _ in range(n_batch))
        carry = segment([n_batch * rows_x + bi * rows_c for bi in range(n_batch)], rows_c // SUBLANES, carry)
        segment([bi * rows_x for bi in range(n_batch)], rows_x // SUBLANES, carry)

        def outputs(i, carry):
            rows = dot_rows_of(i)
            y = jnp.dot(u2_ref[rows, :], m_ref[...], preferred_element_type=F32)
            y = y + lax.dot_general(v_ref[rows, :].astype(BF16), ot_ref[...], (((1,), (1,)), ((), ())),
                                    preferred_element_type=F32)
            for s in range(S5_SUB):
                part = y[:, s * LANES:(s + 1) * LANES]
                if reverse:
                    y_ref[token_rows_of(i, s), :] += part
                else:
                    y_ref[token_rows_of(i, s), :] = part
            return carry

        lax.fori_loop(0, n_dot, outputs, 0)

    @pl.when(d == 0)
    def _():
        run(False)

    @pl.when(d == 1)
    def _():
        run(True)


def _s5_scan(z, fac, pw, taps, tabs, layer, b, s, tc, col_u):
    r = z.shape[0]
    rc = r // S5_SUB
    n_blocks = fac.shape[2]
    width = S5_SUB * LANES
    assert width == 2 * S5_HALF
    dot_rows = max(n for n in range(16, 641, 16) if rc % n == 0)
    return pl.pallas_call(
        functools.partial(_s5_kernel, b, s // S5_SUB, tc // S5_SUB, dot_rows),
        out_shape=jax.ShapeDtypeStruct((r, n_blocks * LANES), F32),
        grid=(n_blocks, 2),
        in_specs=[
            pl.BlockSpec((r, LANES), lambda cb, d: (0, col_u // LANES + cb), pipeline_mode=pl.Buffered(1)),
            pl.BlockSpec((None, 1, 1, 4, LANES, width), lambda cb, d: (layer, d, cb, 0, 0, 0),
                         pipeline_mode=pl.Buffered(1)),
            pl.BlockSpec((None, 1, 1, 4, S5_SUB, width), lambda cb, d: (layer, d, cb, 0, 0, 0)),
            pl.BlockSpec((None, 1, 1, 2 * S5_SUB - 1, LANES, LANES), lambda cb, d: (layer, d, cb, 0, 0, 0)),
            pl.BlockSpec((None, 1, 1, 8, SUBLANES, S5_HALF), lambda cb, d: (layer, d, cb, 0, 0, 0)),
        ],
        out_specs=pl.BlockSpec((r, LANES), lambda cb, d: (0, cb)),
        scratch_shapes=[pltpu.VMEM((rc, width), BF16), pltpu.VMEM((rc, width), F32),
                        pltpu.VMEM((width, width), BF16), pltpu.VMEM((width, width), BF16),
                        pltpu.VMEM((width, width), BF16)],
        compiler_params=_cparams(("parallel", "arbitrary")),
        name="s5_scan",
    )(z, fac, pw, taps, tabs)


def _glu_kernel(y_ref, u_ref, d_ref, w_ref, b_ref, o_ref, g_ref):
    tm = y_ref.shape[0]

    def body(r, carry):
        rows = pl.ds(pl.multiple_of(r * ROW_CHUNK, ROW_CHUNK), ROW_CHUNK)
        y = y_ref[rows, :] + d_ref[...] * u_ref[rows, :].astype(F32)
        g_ref[rows, :] = jax.nn.gelu(y).astype(BF16)
        return carry

    lax.fori_loop(0, tm // ROW_CHUNK, body, 0)
    g = g_ref[...]
    gate = jax.nn.sigmoid(jnp.dot(g, w_ref[...], preferred_element_type=F32) + b_ref[...])
    o_ref[...] = (g.astype(F32) * gate).astype(BF16)


def _s5_glu(y, z, d_skip, w_glu, b_glu, tm, col_u, n_row_tiles):
    width = y.shape[1]
    return pl.pallas_call(
        _glu_kernel,
        out_shape=jax.ShapeDtypeStruct((n_row_tiles * tm, width), BF16),
        grid=(n_row_tiles,),
        in_specs=[
            pl.BlockSpec((tm, width), lambda i: (i, 0)),
            pl.BlockSpec((tm, width), lambda i: (i, col_u // width)),
            pl.BlockSpec((1, width), lambda i: (0, 0)),
            pl.BlockSpec((width, width), lambda i: (0, 0)),
            pl.BlockSpec((1, width), lambda i: (0, 0)),
        ],
        out_specs=pl.BlockSpec((tm, width), lambda i: (i, 0)),
        scratch_shapes=[pltpu.VMEM((tm, width), BF16)],
        compiler_params=_cparams(("parallel",)),
        name="s5_glu",
    )(y, z, d_skip, w_glu, b_glu)


def _merge_kernel(n_x_tiles, ax_ref, ac_ref, bx_ref, bc_ref, c_ref, ga_ref, gb_ref, gc_ref, w_ref, o_ref):
    def combine(a_ref, b_ref):
        acc = None
        for r, (br_ref, gate_ref) in enumerate(((a_ref, ga_ref), (b_ref, gb_ref), (c_ref, gc_ref))):
            proj = jnp.dot(br_ref[...], w_ref[r], preferred_element_type=F32)
            term = jax.nn.sigmoid(gate_ref[...].astype(F32)) * proj
            acc = term if acc is None else acc + term
        o_ref[...] = acc.astype(BF16)

    @pl.when(pl.program_id(0) < n_x_tiles)
    def _():
        combine(ax_ref, bx_ref)

    @pl.when(pl.program_id(0) >= n_x_tiles)
    def _():
        combine(ac_ref, bc_ref)


def _merge(a_x, a_c, b_x, b_c, cc, z, w_branch, tm, tn, col_g, n_row_tiles, n_x_tiles):
    width = a_x.shape[1]
    d = w_branch.shape[2]

    def gate_spec(r):
        return pl.BlockSpec((tm, tn), lambda i, j: (i, (col_g + r * d) // tn + j))

    ax_spec, ac_spec = _stream_specs((tm, width), n_x_tiles, 0, lambda j: 0)
    return pl.pallas_call(
        functools.partial(_merge_kernel, n_x_tiles),
        out_shape=jax.ShapeDtypeStruct((n_row_tiles * tm, d), BF16),
        grid=(n_row_tiles, d // tn),
        in_specs=[
            ax_spec, ac_spec, ax_spec, ac_spec,
            pl.BlockSpec((tm, width), lambda i, j: (i, 0)),
            gate_spec(0), gate_spec(1), gate_spec(2),
            pl.BlockSpec((N_BRANCH, width, tn), lambda i, j: (0, 0, j)),
        ],
        out_specs=pl.BlockSpec((tm, tn), lambda i, j: (i, j)),
        compiler_params=_cparams(("parallel", "arbitrary")),
        name="merge",
    )(a_x, a_c, b_x, b_c, cc, z, z, z, w_branch)


def _resid_kernel(m_ref, w_ref, x_ref, al_ref, o_ref):
    o_ref[...] = x_ref[...] + al_ref[0] * jnp.dot(m_ref[...], w_ref[...], preferred_element_type=F32)


def _out_proj_residual(m, w, xs, mod3, tm, tn, n_row_tiles, n_x_tiles, tiles_per_batch, ctx_row, k_alpha):
    kdim, d = w.shape
    nt = d // tn

    def alpha_map(i, j):
        r = jnp.where(i < n_x_tiles, i // tiles_per_batch, ctx_row)
        return (r * N_MOD + k_alpha, 0, j)

    return pl.pallas_call(
        _resid_kernel,
        out_shape=jax.ShapeDtypeStruct((n_row_tiles * tm, d), F32),
        grid=(n_row_tiles, nt),
        in_specs=[
            pl.BlockSpec((tm, kdim), lambda i, j: (i, 0)),
            pl.BlockSpec((kdim, tn), lambda i, j: (0, j)),
            pl.BlockSpec((tm, tn), lambda i, j: (i, j)),
            pl.BlockSpec((1, 1, tn), alpha_map),
        ],
        out_specs=pl.BlockSpec((tm, tn), lambda i, j: (i, j)),
        compiler_params=_cparams(("parallel", "arbitrary")),
        name="out_proj",
    )(m, w, xs, mod3)


def _ff1_kernel(x_ref, g_ref, sh_ref, sc_ref, w_ref, h_ref, xn_ref):
    @pl.when(pl.program_id(1) == 0)
    def _():
        _norm_mod_rows(x_ref, g_ref, sh_ref, sc_ref, xn_ref)

    a = jnp.maximum(jnp.dot(xn_ref[...], w_ref[...], preferred_element_type=F32), 0.0)
    h_ref[...] = (a * a).astype(BF16)


def _ff1(xs, g, mod3, w, tm, tn, n_row_tiles, n_x_tiles, tiles_per_batch, ctx_row):
    d, dff = w.shape
    return pl.pallas_call(
        _ff1_kernel,
        out_shape=jax.ShapeDtypeStruct((n_row_tiles * tm, dff), BF16),
        grid=(n_row_tiles, dff // tn),
        in_specs=[
            pl.BlockSpec((tm, d), lambda i, j: (i, 0)),
            pl.BlockSpec((1, d), lambda i, j: (0, 0)),
            pl.BlockSpec((1, 1, d), _mod_row_map(n_x_tiles, tiles_per_batch, ctx_row, 3)),
            pl.BlockSpec((1, 1, d), _mod_row_map(n_x_tiles, tiles_per_batch, ctx_row, 4)),
            pl.BlockSpec((d, tn), lambda i, j: (0, j)),
        ],
        out_specs=pl.BlockSpec((tm, tn), lambda i, j: (i, j)),
        scratch_shapes=[pltpu.VMEM((tm, d), BF16)],
        compiler_params=_cparams(("parallel", "arbitrary")),
        name="ff1",
    )(xs, g, mod3, mod3, w)


def _ff2_kernel(h_ref, w_ref, x_ref, al_ref, o_ref):
    k = pl.program_id(2)
    part = jnp.dot(h_ref[...], w_ref[...], preferred_element_type=F32)

    @pl.when(k == 0)
    def _():
        o_ref[...] = part

    @pl.when(k > 0)
    def _():
        o_ref[...] += part

    @pl.when(k == pl.num_programs(2) - 1)
    def _():
        o_ref[...] = x_ref[...] + al_ref[0] * o_ref[...]


def _ff2(h, w, xs, mod3, tm, tn, tk, n_row_tiles, n_x_tiles, tiles_per_batch, ctx_row):
    dff, d = w.shape

    def alpha_map(i, j, k):
        r = jnp.where(i < n_x_tiles, i // tiles_per_batch, ctx_row)
        return (r * N_MOD + 5, 0, j)

    return pl.pallas_call(
        _ff2_kernel,
        out_shape=jax.ShapeDtypeStruct((n_row_tiles * tm, d), F32),
        grid=(n_row_tiles, d // tn, dff // tk),
        in_specs=[
            pl.BlockSpec((tm, tk), lambda i, j, k: (i, k)),
            pl.BlockSpec((tk, tn), lambda i, j, k: (k, j)),
            pl.BlockSpec((tm, tn), lambda i, j, k: (i, j)),
            pl.BlockSpec((1, 1, tn), alpha_map),
        ],
        out_specs=pl.BlockSpec((tm, tn), lambda i, j, k: (i, j)),
        compiler_params=_cparams(("parallel", "parallel", "arbitrary")),
        name="ff2",
    )(h, w, xs, mod3)


def _rope_partner():
    j = np.arange(MLA_ROPE)
    quarter = MLA_ROPE // 4
    return np.where((j // quarter) % 2 == 0, j + quarter, j - quarter)


def _rope_table(s, tm):
    pos = jnp.arange(s)
    row = (pos // GRID_W).astype(F32)
    col = (pos % GRID_W).astype(F32)
    n_freq = MLA_ROPE // 4
    inv_freq = ROPE_THETA ** (-jnp.arange(n_freq, dtype=F32) / n_freq)
    ang_r = row[:, None] * inv_freq
    ang_c = col[:, None] * inv_freq
    cos = jnp.concatenate([jnp.cos(ang_r)] * 2 + [jnp.cos(ang_c)] * 2, axis=-1)
    sin = jnp.concatenate([-jnp.sin(ang_r), jnp.sin(ang_r), -jnp.sin(ang_c), jnp.sin(ang_c)], axis=-1)
    ident = jnp.concatenate([jnp.ones((tm, MLA_ROPE), F32), jnp.zeros((tm, MLA_ROPE), F32)], axis=-1)
    return jnp.concatenate([jnp.concatenate([cos, sin], axis=-1), ident], axis=0)


def _pack_w_in(w_in, b_in, gate_b, sizes, n_pad):
    w_t = w_in.T
    d_model = w_in.shape[0]
    bounds = np.cumsum((0,) + sizes)
    seg = [slice(int(bounds[i]), int(bounds[i + 1])) for i in range(len(sizes))]
    partner = _rope_partner()
    order = (0, 1, 2, 3, 5, 6, 8, 9)
    names = ("q", "k", "v", "o", "qa", "kva", "u", "gates")
    offs, pos = {}, 0
    for name, i in zip(names, order):
        offs[name] = pos
        pos += sizes[i]
    offs["kpe"], offs["kpe_sw"] = pos, pos + MLA_ROPE
    n_zero = n_pad - pos - 2 * MLA_ROPE
    kpe_w, kpe_b = w_t[seg[7]], b_in[seg[7]]
    w = jnp.concatenate([w_t[seg[i]] for i in order] + [kpe_w, kpe_w[partner], jnp.zeros((n_zero, d_model), F32)],
                        axis=0).astype(BF16)
    b = jnp.concatenate([b_in[seg[i]] for i in order] + [kpe_b, kpe_b[partner], jnp.zeros((n_zero,), F32)])
    n_g = sizes[4]
    wg = jnp.concatenate([w_t[seg[4]], jnp.zeros((LANES - n_g, d_model), F32)], axis=0).astype(BF16)
    bg = jnp.concatenate([b_in[seg[4]] + gate_b.reshape(-1), jnp.zeros((LANES - n_g,), F32)])[None, :]
    return w, b[None, :], wg, bg, offs


def _pack_mla(w_uq, w_ukv, qn_g, kn_g):
    partner = _rope_partner()
    lora = w_uq.shape[0]
    wq = w_uq.reshape(lora, MLA_HEADS, MLA_DQK)
    wq = jnp.concatenate([wq, wq[:, :, MLA_NOPE + partner]], axis=-1).reshape(lora, MLA_HEADS * MLA_SLAB)
    wkv = w_ukv.reshape(w_ukv.shape[0], MLA_HEADS, -1)
    wkv = jnp.concatenate([wkv[:, :, :MLA_NOPE].reshape(lora, -1), wkv[:, :, MLA_NOPE:].reshape(lora, -1)], axis=-1)

    def gains(g):
        return jnp.concatenate([g, g[MLA_NOPE + partner]])[None, :]

    return wq.astype(BF16), wkv.astype(BF16), gains(qn_g), gains(kn_g)


def _pack_s5(a_re, a_im, log_dt, b_re, b_im, c_re, c_im):
    n_dir, n_groups, n_state = a_re.shape
    gc = b_re.shape[-1]
    nb = n_groups // S5_BLOCK_GROUPS
    lam_re = jnp.minimum(a_re.astype(F32), -1e-4)
    lam_im = a_im.astype(F32)
    dt = jnp.exp(log_dt.astype(F32))[..., None]

    def pole_power(k):
        mag = jnp.exp(k * lam_re * dt)
        return mag * jnp.cos(k * lam_im * dt), mag * jnp.sin(k * lam_im * dt)

    bar_re, bar_im = pole_power(1.0)
    den = lam_re * lam_re + lam_im * lam_im
    f_re = ((bar_re - 1.0) * lam_re + bar_im * lam_im) / den
    f_im = (bar_im * lam_re - (bar_re - 1.0) * lam_im) / den
    bb_re = f_re[..., None] * b_re.astype(F32) - f_im[..., None] * b_im.astype(F32)
    bb_im = f_re[..., None] * b_im.astype(F32) + f_im[..., None] * b_re.astype(F32)
    eye = jnp.eye(S5_BLOCK_GROUPS, dtype=F32)
    sub = S5_SUB

    def per_block(a):
        return a.reshape(a.shape[:-2] + (nb, S5_BLOCK_GROUPS * n_state))

    def block_b(part):
        p = part.reshape(n_dir, nb, S5_BLOCK_GROUPS, n_state, gc)
        m = jnp.einsum('dbgnc,gh->dbgchn', p, eye, precision=HIGHEST)
        return m.reshape(n_dir, nb, LANES, S5_HALF)

    def block_c(part):
        p = part.astype(F32).reshape(n_dir, nb, S5_BLOCK_GROUPS, gc, n_state)
        m = jnp.einsum('dbgcn,gh->dbgnhc', p, eye, precision=HIGHEST)
        return m.reshape(n_dir, nb, S5_HALF, LANES)

    bm_re, bm_im = block_b(bb_re), block_b(bb_im)
    cm_re, cm_im = block_c(c_re), block_c(c_im)

    tau = jnp.arange(sub + 1, dtype=F32)[:, None, None, None]
    p_re, p_im = (per_block(p) for p in pole_power(tau))
    pr, pi = p_re[:sub, :, :, None, :], p_im[:sub, :, :, None, :]
    taps = (jnp.einsum('zdbkn,dbnc->zdbkc', bm_re * pr - bm_im * pi, cm_re, precision=HIGHEST)
            - jnp.einsum('zdbkn,dbnc->zdbkc', bm_re * pi + bm_im * pr, cm_im, precision=HIGHEST))

    ct_re, ct_im = jnp.swapaxes(cm_re, -1, -2), jnp.swapaxes(cm_im, -1, -2)
    cat = functools.partial(jnp.concatenate, axis=-1)
    fac = jnp.stack([cat([bm_re, bm_im]), cat([-bm_im, bm_re]), cat([ct_re, -ct_im]), cat([-ct_im, -ct_re])],
                    axis=2)
    t = np.arange(sub)
    zero_taps = jnp.zeros((sub - 1,) + taps.shape[2:], F32)
    pws, taps2 = [], []
    for d in range(n_dir):
        to_exit = (sub - 1 - t) if d == 0 else t
        age = (t + 1) if d == 0 else (sub - t)
        rows = [p_re[to_exit, d], p_im[to_exit, d], p_re[age, d], p_im[age, d]]
        pws.append(jnp.stack([cat([a, a]).transpose(1, 0, 2) for a in rows], axis=1))
        taps2.append(jnp.concatenate([zero_taps, taps[:, 0]] if d == 0 else [taps[::-1, 1], zero_taps], axis=0))
    pw = jnp.stack(pws)
    taps2 = jnp.stack(taps2).transpose(0, 2, 1, 3, 4)

    rows = jnp.arange(SUBLANES)
    tabs = []
    for d in range(n_dir):
        per_dir = []
        for shift in (1, 2, 4):
            keep = ((rows <= SUBLANES - 1 - shift) if d == 1 else (rows >= shift))[None, :, None]
            s_re, s_im = pole_power(float(shift * sub))
            per_dir += [jnp.where(keep, per_block(s_re[d])[:, None, :], 0.0),
                        jnp.where(keep, per_block(s_im[d])[:, None, :], 0.0)]
        expo = (((SUBLANES - rows) if d == 1 else (rows + 1)) * sub).astype(F32)
        s_re, s_im = pole_power(expo[:, None, None, None])
        per_dir += [jnp.moveaxis(per_block(s_re[:, d]), 0, 1), jnp.moveaxis(per_block(s_im[:, d]), 0, 1)]
        tabs.append(jnp.stack(per_dir, axis=1))
    return fac, pw, taps2, jnp.stack(tabs).astype(F32)


def _tri_matrices():
    t = np.arange(ML_CHUNK)
    lower = t[None, :] <= t[:, None]
    mask = np.where(np.stack([lower, lower.T]), 0.0, NEG_BIG).astype(np.float32)
    return jnp.asarray(lower.astype(np.float32), dtype=BF16), jnp.asarray(mask)


def kernel(x, c, ctx, c_ctx, w_mod, b_mod, norm_g, w_in, b_in, ml_gate_b, ml_norm_g, mla_qa_g, mla_kva_g, mla_w_uq, mla_w_ukv, mla_qn_g, mla_kn_g, s5_a_re, s5_a_im, s5_log_dt, s5_b_re, s5_b_im, s5_c_re, s5_c_im, s5_d, s5_w_glu, s5_b_glu, w_branch, w_out, w_ff1, w_ff2):
    b, s, d = x.shape
    tc = ctx.shape[1]
    depth = w_mod.shape[0]
    dv = ml_norm_g.shape[2]
    dk = dv // 2
    lora = mla_qa_g.shape[1]
    s5_width = s5_d.shape[1]
    branch_w = w_branch.shape[2]
    sizes = (ML_HEADS * dk, ML_HEADS * dk, ML_HEADS * dv, ML_HEADS * dv, 4 * ML_HEADS, lora, lora, MLA_ROPE,
             s5_width, N_BRANCH * d)
    assert sum(sizes) == w_in.shape[2] and b + 1 <= SUBLANES
    assert s % ML_CHUNK == 0 and tc % ML_CHUNK == 0 and branch_w == ML_HEADS * dv == MLA_HEADS * LANES == s5_width

    n_x = b * s
    n_c = b * tc
    tm = _row_tile(s, n_c)
    n_x_tiles = n_x // tm
    n_tiles = n_x_tiles + n_c // tm
    tiles_per_batch = s // tm
    tile_args = (n_x_tiles, tiles_per_batch, b)

    xs = jnp.concatenate([x.reshape(n_x, d), ctx.reshape(n_c, d)], axis=0)
    cc = jnp.concatenate([c, c_ctx[None, :], jnp.zeros((SUBLANES - b - 1, d), F32)], axis=0)
    mod = _modulation(cc, w_mod, b_mod)
    tm_q = min(tm, 512)
    tab = _rope_table(s, tm_q)
    tri, ml_mask = _tri_matrices()
    s5_packed = jax.vmap(_pack_s5)(s5_a_re, s5_a_im, s5_log_dt, s5_b_re, s5_b_im, s5_c_re, s5_c_im)
    n_used = sum(sizes) - sizes[4] + MLA_ROPE
    tn_in = 1280
    n_pad = -(-n_used // tn_in) * tn_in

    for l in range(depth):
        with_ctx_out = l < depth - 1
        mod3 = mod[l].reshape(SUBLANES * N_MOD, 1, d)
        w_p, b_p, wg, bg, offs = _pack_w_in(w_in[l], b_in[l], ml_gate_b[l], sizes, n_pad)
        z, gz = _in_proj(xs, norm_g[l, 0][None, :], mod3, w_p, b_p, wg, bg, tm, tn_in, *tile_args)

        gates, gates_t = _gate_prep(gz, tri)
        a_x, a_c = _mlstm(z, gates, gates_t, ml_mask, ml_norm_g[l], b, s, tc, dk, dv,
                          (offs["q"], offs["k"], offs["v"], offs["o"]), with_ctx_out)

        wq, wkv, gq, gk = _pack_mla(mla_w_uq[l], mla_w_ukv[l], mla_qn_g[l], mla_kn_g[l])
        qo, ko, vo = _mla_proj(z, tab, mla_qa_g[l][None, :], mla_kva_g[l][None, :], wq, wkv, gq, gk, tm_q,
                               (offs["qa"], offs["kva"], offs["kpe"]), n_x // tm_q, s // tm_q)
        tq = min(512, s)
        b_x = _attention(qo, ko, vo, b, s, tc, tq, True)

        fac, pw, taps, tabs = s5_packed
        y = _s5_scan(z, fac, pw, taps, tabs, l, b, s, tc, offs["u"])
        n_out_tiles = n_tiles if with_ctx_out else n_x_tiles
        c_all = _s5_glu(y, z, s5_d[l][None, :], s5_w_glu[l].astype(BF16), s5_b_glu[l][None, :], tm, offs["u"],
                        n_out_tiles)

        if with_ctx_out:
            b_c = _attention(qo, ko, vo, b, s, tc, min(tq, tc), False)
        else:
            a_c, b_c = a_x, b_x
        merged = _merge(a_x, a_c, b_x, b_c, c_all, z, w_branch[l].astype(BF16), tm, 512, offs["gates"],
                        n_out_tiles, n_x_tiles)
        xs1 = _out_proj_residual(merged, w_out[l].astype(BF16), xs, mod3, tm, 1024, n_out_tiles, *tile_args, 2)
        hid = _ff1(xs1, norm_g[l, 1][None, :], mod3, w_ff1[l].astype(BF16), tm, 1024, n_out_tiles, *tile_args)
        sub = 2 if tm % 1024 == 0 else 1
        xs = _ff2(hid, w_ff2[l].astype(BF16), xs1, mod3, tm // sub, 512, w_ff2.shape[1], n_out_tiles * sub,
                  n_x_tiles * sub, tiles_per_batch * sub, b)

    return xs.reshape(b, s, d)
```

```python
import functools
import math

import jax
import jax.numpy as jnp
import numpy as np
from jax import lax
from jax.experimental import pallas as pl
from jax.experimental.pallas import tpu as pltpu

F32 = jnp.float32
BF16 = jnp.bfloat16
HIGHEST = lax.Precision.HIGHEST

N_MOD = 6
N_BRANCH = 3
ML_HEADS = 4
MLA_HEADS = 8
MLA_NOPE = 128
MLA_ROPE = 64
MLA_DQK = MLA_NOPE + MLA_ROPE
MLA_SLAB = 256
GRID_W = 64
ROPE_THETA = 10000.0
S5_GROUP = 16
S5_STATE = 64
S5_BLOCK_GROUPS = 8
EPS = 1e-6
NEG_BIG = -1e30

LANES = 128
SUBLANES = 8
VMEM_LIMIT = 56 * 1024 * 1024

ML_CHUNK = 256
ML_GATE_ROWS = 32
S5_SUB = 8
ATTN_KEY_CHUNK = 512
Q_PRESCALE = MLA_DQK ** -0.5 * math.log2(math.e)
ROW_CHUNK = 128


def _cparams(sem):
    return pltpu.CompilerParams(dimension_semantics=sem, vmem_limit_bytes=VMEM_LIMIT)


def _row_tile(n_x_rows_per_batch, n_ctx_rows):
    tm = 1024
    while n_x_rows_per_batch % tm or n_ctx_rows % tm:
        tm //= 2
    return tm


def _mod_kernel(c_ref, w_ref, b_ref, o_ref):
    s = c_ref[...]
    s = s * jax.nn.sigmoid(s)
    o_ref[0] = jnp.dot(s.astype(BF16), w_ref[0].astype(BF16), preferred_element_type=F32) + b_ref[0]


def _modulation(cc, w_mod, b_mod):
    n_layers, d, n = w_mod.shape
    tn = 1024
    return pl.pallas_call(
        _mod_kernel,
        out_shape=jax.ShapeDtypeStruct((n_layers, SUBLANES, n), F32),
        grid=(n_layers, n // tn),
        in_specs=[
            pl.BlockSpec((SUBLANES, d), lambda l, j: (0, 0)),
            pl.BlockSpec((1, d, tn), lambda l, j: (l, 0, j)),
            pl.BlockSpec((1, 1, tn), lambda l, j: (l, 0, j)),
        ],
        out_specs=pl.BlockSpec((1, SUBLANES, tn), lambda l, j: (l, 0, j)),
        compiler_params=_cparams(("parallel", "parallel")),
        name="adaln_mod",
    )(cc, w_mod, b_mod.reshape(n_layers, 1, n))


def _norm_mod_rows(x_ref, g_ref, sh_ref, sc_ref, xn_ref):
    tm = x_ref.shape[0]
    g = g_ref[...]
    sc = 1.0 + sc_ref[0]
    sh = sh_ref[0]

    def body(r, carry):
        rows = pl.ds(pl.multiple_of(r * ROW_CHUNK, ROW_CHUNK), ROW_CHUNK)
        x = x_ref[rows, :]
        ms = jnp.mean(x * x, axis=-1, keepdims=True)
        y = x * lax.rsqrt(ms + EPS) * g
        xn_ref[rows, :] = (y * sc + sh).astype(BF16)
        return carry

    lax.fori_loop(0, tm // ROW_CHUNK, body, 0)


def _mod_row_map(n_x_tiles, tiles_per_batch, ctx_row, k):
    def index_map(i, j):
        r = jnp.where(i < n_x_tiles, i // tiles_per_batch, ctx_row)
        return (r * N_MOD + k, 0, 0)
    return index_map


def _stream_specs(block, n_x_tiles, ctx_tile0, col_map):
    x_spec = pl.BlockSpec(block, lambda i, *r: (jnp.minimum(i, n_x_tiles - 1), col_map(*r)))
    c_spec = pl.BlockSpec(block, lambda i, *r: (ctx_tile0 + jnp.maximum(i - n_x_tiles, 0), col_map(*r)),
                          pipeline_mode=pl.Buffered(1))
    return x_spec, c_spec


def _dot_nt(a, b_t):
    return lax.dot_general(a, b_t, (((1,), (1,)), ((), ())), preferred_element_type=F32)


def _in_kernel(x_ref, g_ref, sh_ref, sc_ref, w_ref, b_ref, wg_ref, bg_ref, z_ref, gz_ref, xn_ref):
    @pl.when(pl.program_id(1) == 0)
    def _():
        _norm_mod_rows(x_ref, g_ref, sh_ref, sc_ref, xn_ref)
        gz_ref[...] = _dot_nt(xn_ref[...], wg_ref[...]) + bg_ref[...]

    z_ref[...] = (_dot_nt(xn_ref[...], w_ref[...]) + b_ref[...]).astype(BF16)


def _in_proj(xs, g, mod3, w, b, wg, bg, tm, tn, n_x_tiles, tiles_per_batch, ctx_row):
    r, d = xs.shape
    nz = w.shape[0]
    return pl.pallas_call(
        _in_kernel,
        out_shape=(jax.ShapeDtypeStruct((r, nz), BF16), jax.ShapeDtypeStruct((r, LANES), F32)),
        grid=(r // tm, nz // tn),
        in_specs=[
            pl.BlockSpec((tm, d), lambda i, j: (i, 0)),
            pl.BlockSpec((1, d), lambda i, j: (0, 0)),
            pl.BlockSpec((1, 1, d), _mod_row_map(n_x_tiles, tiles_per_batch, ctx_row, 0)),
            pl.BlockSpec((1, 1, d), _mod_row_map(n_x_tiles, tiles_per_batch, ctx_row, 1)),
            pl.BlockSpec((tn, d), lambda i, j: (j, 0)),
            pl.BlockSpec((1, tn), lambda i, j: (0, j)),
            pl.BlockSpec((LANES, d), lambda i, j: (0, 0)),
            pl.BlockSpec((1, LANES), lambda i, j: (0, 0)),
        ],
        out_specs=(
            pl.BlockSpec((tm, tn), lambda i, j: (i, j)),
            pl.BlockSpec((tm, LANES), lambda i, j: (i, 0)),
        ),
        scratch_shapes=[pltpu.VMEM((tm, d), BF16)],
        compiler_params=_cparams(("parallel", "arbitrary")),
        name="in_proj",
    )(xs, g, mod3, mod3, w, b, wg, bg)


def _log_sigmoid(x):
    return jnp.minimum(x, 0.0) - jnp.log1p(jnp.exp(-jnp.abs(x)))


def _split3(a):
    hi = a.astype(BF16)
    r1 = a - hi.astype(F32)
    mid = r1.astype(BF16)
    lo = (r1 - mid.astype(F32)).astype(BF16)
    return hi, mid, lo


def _gate_prep_kernel(g_ref, tri_ref, a_ref, at_ref):
    g = g_ref[...]
    kind = lax.broadcasted_iota(jnp.int32, g.shape, 1) // ML_HEADS
    lg = jnp.where((kind == 1) | (kind == 3), _log_sigmoid(g), g)
    pre = sum(jnp.dot(tri_ref[...], p, preferred_element_type=F32) for p in _split3(lg))
    suf = pre[ML_CHUNK - 1:ML_CHUNK, :] - pre + lg
    a = jnp.where(kind == 4, pltpu.roll(pre, 3 * ML_HEADS, axis=1),
                  jnp.where(kind == 5, pltpu.roll(suf, 2 * ML_HEADS, axis=1), lg))
    a_ref[...] = a
    at_ref[0] = a.T[:ML_GATE_ROWS, :]


def _gate_prep(gz, tri):
    r = gz.shape[0]
    return pl.pallas_call(
        _gate_prep_kernel,
        out_shape=(jax.ShapeDtypeStruct((r, LANES), F32),
                   jax.ShapeDtypeStruct((r // ML_CHUNK, ML_GATE_ROWS, ML_CHUNK), F32)),
        grid=(r // ML_CHUNK,),
        in_specs=[pl.BlockSpec((ML_CHUNK, LANES), lambda i: (i, 0)),
                  pl.BlockSpec((ML_CHUNK, ML_CHUNK), lambda i: (0, 0))],
        out_specs=(pl.BlockSpec((ML_CHUNK, LANES), lambda i: (i, 0)),
                   pl.BlockSpec((1, ML_GATE_ROWS, ML_CHUNK), lambda i: (i, 0, 0))),
        compiler_params=_cparams(("parallel",)),
        name="mlstm_gates",
    )(gz, tri)


def _ml_chunk(q, k_t, v, cum_cb, li_r, lf_r, cum_r, mask_add, carry, inv_scale):
    c_mat, n_mat, m = carry
    length, dv = v.shape
    crow = cum_r - li_r
    total = jnp.sum(lf_r, axis=-1, keepdims=True)
    ones = jnp.ones((length, LANES), BF16)
    n_blk = length // LANES

    def wide(a):
        return jnp.concatenate([a] * (dv // LANES), axis=1)

    log_w = [cum_cb - crow[:, j * LANES:(j + 1) * LANES] + mask_add[:, j * LANES:(j + 1) * LANES]
             for j in range(n_blk)]
    row_max = jnp.max(functools.reduce(jnp.maximum, log_w), axis=-1, keepdims=True)
    log_inter = cum_cb + m
    m_t = jnp.maximum(log_inter, row_max)
    w_inter = jnp.exp(log_inter - m_t)
    qk = jnp.dot(q, k_t, preferred_element_type=F32)
    s = jnp.concatenate([qk[:, j * LANES:(j + 1) * LANES] * jnp.exp(log_w[j] - m_t) for j in range(n_blk)],
                        axis=1).astype(BF16)
    num = wide(w_inter) * jnp.dot(q, c_mat.astype(BF16), preferred_element_type=F32)
    num = num + jnp.dot(s, v, preferred_element_type=F32)
    den = w_inter * jnp.dot(q, n_mat.astype(BF16), preferred_element_type=F32)
    den = den + jnp.dot(s, ones, preferred_element_type=F32)
    h = num * wide(1.0 / jnp.maximum(jnp.abs(den), jnp.exp(-m_t) * inv_scale))

    log_end = total - crow
    m_new = jnp.maximum(total + m, jnp.max(log_end, axis=-1, keepdims=True))
    decay = jnp.exp(total + m - m_new)
    kw_t = (k_t.astype(F32) * jnp.exp(log_end - m_new)).astype(BF16)
    c_new = decay * c_mat + jnp.dot(kw_t, v, preferred_element_type=F32)
    n_new = decay * n_mat + jnp.dot(kw_t, ones, preferred_element_type=F32)
    return h, (c_new, n_new, m_new)


def _mlstm_kernel(with_ctx_out, qx, kx, vx, ox, qc, kc, vc, oc, gx, gc, gtx, gtc, mask_ref, ng_ref, *rest):
    if with_ctx_out:
        ax_ref, ac_ref, cumx, ktx, cumc, ktc, hx, hc = rest
    else:
        ax_ref, cumx, ktx, cumc, ktc, hx, hc = rest
        ac_ref = None
    head = pl.program_id(1)
    dk = qx.shape[1]
    inv_scale = float(dk) ** 0.5
    n_x_chunks = qx.shape[0] // ML_CHUNK
    n_c_chunks = qc.shape[0] // ML_CHUNK

    def prep(g_ref, k_ref, cum_ref, kt_ref):
        def body(i, carry):
            rows = pl.ds(pl.multiple_of(i * ML_CHUNK, ML_CHUNK), ML_CHUNK)
            a = g_ref[rows, :]
            col = lax.broadcasted_iota(jnp.int32, a.shape, 1)
            for d in range(2):
                pick = jnp.sum(jnp.where(col == (4 + d) * ML_HEADS + head, a, 0.0), axis=-1, keepdims=True)
                cum_ref[d, rows, :] = jnp.broadcast_to(pick, a.shape)
            kt_ref[i] = k_ref[rows, :].astype(F32).T.astype(BF16)
            return carry
        n_chunks = g_ref.shape[0] // ML_CHUNK
        lax.fori_loop(0, n_chunks, body, 0, unroll=2 if n_chunks % 2 == 0 else 1)

    prep(gx, kx, cumx, ktx)
    prep(gc, kc, cumc, ktc)

    for d in range(2):
        reverse = d == 1
        mask_add = mask_ref[d]

        def step(q_ref, v_ref, cum_ref, gt_ref, kt_ref, h_ref, ci, carry):
            rows = pl.ds(pl.multiple_of(ci * ML_CHUNK, ML_CHUNK), ML_CHUNK)
            li_r, lf_r, cum_r = (gt_ref[ci, pl.ds(kind * ML_HEADS + head, 1), :] for kind in (2 * d, 2 * d + 1, 4 + d))
            h, carry = _ml_chunk(q_ref[rows, :], kt_ref[ci], v_ref[rows, :], cum_ref[d, rows, :], li_r, lf_r, cum_r,
                                 mask_add, carry, inv_scale)
            if reverse:
                h_ref[rows, :] += h
            else:
                h_ref[rows, :] = h
            return carry

        carry = (jnp.zeros((dk, vx.shape[1]), F32), jnp.zeros((dk, LANES), F32), jnp.zeros((1, 1), F32))

        def ctx_body(i, carry):
            ci = (n_c_chunks - 1 - i) if reverse else i
            return step(qc, vc, cumc, gtc, ktc, hc, ci, carry)

        def x_body(i, carry):
            ci = (n_x_chunks - 1 - i) if reverse else i
            return step(qx, vx, cumx, gtx, ktx, hx, ci, carry)

        carry = lax.fori_loop(0, n_c_chunks, ctx_body, carry)
        lax.fori_loop(0, n_x_chunks, x_body, carry, unroll=2 if n_x_chunks % 2 == 0 else 1)

    ng = ng_ref[0]

    def finish(h_ref, o_ref, a_ref):
        def body(i, carry):
            rows = pl.ds(pl.multiple_of(i * ML_CHUNK, ML_CHUNK), ML_CHUNK)
            h = h_ref[rows, :]
            hn = h * lax.rsqrt(jnp.mean(h * h, axis=-1, keepdims=True) + EPS) * ng
            a_ref[rows, :] = (hn * jax.nn.sigmoid(o_ref[rows, :].astype(F32))).astype(BF16)
            return carry
        lax.fori_loop(0, h_ref.shape[0] // ML_CHUNK, body, 0)

    finish(hx, ox, ax_ref)
    if with_ctx_out:
        finish(hc, oc, ac_ref)


def _mlstm(z, gates, gates_t, mask, ml_norm_g, b, s, tc, dk, dv, cols, with_ctx_out):
    n_x = b * s
    cq, ck, cv, co = cols
    ctx0 = n_x // tc
    xc, cc = s // ML_CHUNK, tc // ML_CHUNK

    def xspec(width, col0):
        return pl.BlockSpec((s, width), lambda i, h: (i, col0 // width + h))

    def cspec(width, col0):
        return pl.BlockSpec((tc, width), lambda i, h: (ctx0 + i, col0 // width + h))

    out_shape = [jax.ShapeDtypeStruct((n_x, ML_HEADS * dv), BF16)]
    out_specs = [pl.BlockSpec((s, dv), lambda i, h: (i, h))]
    if with_ctx_out:
        out_shape.append(jax.ShapeDtypeStruct((b * tc, ML_HEADS * dv), BF16))
        out_specs.append(pl.BlockSpec((tc, dv), lambda i, h: (i, h)))
    res = pl.pallas_call(
        functools.partial(_mlstm_kernel, with_ctx_out),
        out_shape=tuple(out_shape),
        grid=(b, ML_HEADS),
        in_specs=[
            xspec(dk, cq), xspec(dk, ck), xspec(dv, cv), xspec(dv, co),
            cspec(dk, cq), cspec(dk, ck), cspec(dv, cv), cspec(dv, co),
            pl.BlockSpec((s, LANES), lambda i, h: (i, 0)),
            pl.BlockSpec((tc, LANES), lambda i, h: (ctx0 + i, 0)),
            pl.BlockSpec((xc, ML_GATE_ROWS, ML_CHUNK), lambda i, h: (i, 0, 0)),
            pl.BlockSpec((cc, ML_GATE_ROWS, ML_CHUNK), lambda i, h: (b * xc // cc + i, 0, 0)),
            pl.BlockSpec((2, ML_CHUNK, ML_CHUNK), lambda i, h: (0, 0, 0)),
            pl.BlockSpec((1, 1, dv), lambda i, h: (h, 0, 0)),
        ],
        out_specs=tuple(out_specs),
        scratch_shapes=[
            pltpu.VMEM((2, s, LANES), F32), pltpu.VMEM((xc, dk, ML_CHUNK), BF16),
            pltpu.VMEM((2, tc, LANES), F32), pltpu.VMEM((cc, dk, ML_CHUNK), BF16),
            pltpu.VMEM((s, dv), F32), pltpu.VMEM((tc, dv), F32),
        ],
        compiler_params=_cparams(("parallel", "parallel")),
        name="mlstm",
    )(z, z, z, z, z, z, z, z, gates, gates, gates_t, gates_t, mask, ml_norm_g.reshape(ML_HEADS, 1, dv))
    return res if with_ctx_out else (res[0], None)


def _mla_proj_kernel(qa_ref, kva_ref, kpe_ref, tab_ref, qag_ref, kvag_ref, wq_ref, wkv_ref,
                     gq_ref, gk_ref, q_ref, k_ref, v_ref):
    def normed(a_ref, g_ref):
        a = a_ref[...].astype(F32)
        return (a * lax.rsqrt(jnp.mean(a * a, axis=-1, keepdims=True) + EPS) * g_ref[...]).astype(BF16)

    q_all = jnp.dot(normed(qa_ref, qag_ref), wq_ref[...], preferred_element_type=F32)
    kv_all = jnp.dot(normed(kva_ref, kvag_ref), wkv_ref[...], preferred_element_type=F32)
    tab = tab_ref[...]
    lane = lax.broadcasted_iota(jnp.int32, tab.shape, 1)
    first_half = lane < MLA_ROPE
    gq = gq_ref[...]
    gk = gk_ref[...]
    inv_dqk = 1.0 / MLA_DQK

    kpe = kpe_ref[...].astype(F32)
    ss_kpe = jnp.sum(jnp.where(first_half, kpe * kpe, 0.0), axis=-1, keepdims=True)
    kpe_t = kpe * (tab * gk[:, LANES:])
    kpe_rot = jnp.where(first_half, kpe_t + pltpu.roll(kpe_t, MLA_ROPE, axis=1), 0.0)

    for h in range(MLA_HEADS):
        qn = q_all[:, h * MLA_SLAB:h * MLA_SLAB + LANES]
        qp = q_all[:, h * MLA_SLAB + LANES:(h + 1) * MLA_SLAB]
        ss = jnp.sum(qn * qn, axis=-1, keepdims=True) + jnp.sum(jnp.where(first_half, qp * qp, 0.0), axis=-1,
                                                                keepdims=True)
        r = lax.rsqrt(ss * inv_dqk + EPS) * Q_PRESCALE
        qp_t = qp * (tab * gq[:, LANES:])
        qp_rot = qp_t + pltpu.roll(qp_t, MLA_ROPE, axis=1)
        q_ref[:, h * MLA_SLAB:h * MLA_SLAB + LANES] = (qn * r * gq[:, :LANES]).astype(BF16)
        q_ref[:, h * MLA_SLAB + LANES:(h + 1) * MLA_SLAB] = (qp_rot * r).astype(BF16)

        kn = kv_all[:, h * LANES:(h + 1) * LANES]
        rk = lax.rsqrt((jnp.sum(kn * kn, axis=-1, keepdims=True) + ss_kpe) * inv_dqk + EPS)
        k_ref[:, h * MLA_SLAB:h * MLA_SLAB + LANES] = (kn * rk * gk[:, :LANES]).astype(BF16)
        k_ref[:, h * MLA_SLAB + LANES:(h + 1) * MLA_SLAB] = (kpe_rot * rk).astype(BF16)

    ones_col = jnp.where(lane == 0, 1.0, 0.0).astype(BF16)
    for h in range(MLA_HEADS):
        v_ref[:, h * MLA_SLAB:h * MLA_SLAB + LANES] = kv_all[:, (MLA_HEADS + h) * LANES:(MLA_HEADS + h + 1) * LANES
                                                             ].astype(BF16)
        v_ref[:, h * MLA_SLAB + LANES:(h + 1) * MLA_SLAB] = ones_col


def _mla_proj(z, tab, qag, kvag, wq, wkv, gq, gk, tm, cols, n_x_tiles, tab_tiles):
    r = z.shape[0]
    cqa, ckva, ckpe = cols
    lora = qag.shape[1]
    hs = MLA_HEADS * MLA_SLAB
    hv = MLA_HEADS * LANES
    return pl.pallas_call(
        _mla_proj_kernel,
        out_shape=(jax.ShapeDtypeStruct((r, hs), BF16), jax.ShapeDtypeStruct((r, hs), BF16),
                   jax.ShapeDtypeStruct((r, hs), BF16)),
        grid=(r // tm,),
        in_specs=[
            pl.BlockSpec((tm, lora), lambda i: (i, cqa // lora)),
            pl.BlockSpec((tm, lora), lambda i: (i, ckva // lora)),
            pl.BlockSpec((tm, LANES), lambda i: (i, ckpe // LANES)),
            pl.BlockSpec((tm, LANES), lambda i: (jnp.where(i < n_x_tiles, i % tab_tiles, tab_tiles), 0)),
            pl.BlockSpec((1, lora), lambda i: (0, 0)),
            pl.BlockSpec((1, lora), lambda i: (0, 0)),
            pl.BlockSpec((lora, hs), lambda i: (0, 0)),
            pl.BlockSpec((lora, 2 * hv), lambda i: (0, 0)),
            pl.BlockSpec((1, MLA_SLAB), lambda i: (0, 0)),
            pl.BlockSpec((1, MLA_SLAB), lambda i: (0, 0)),
        ],
        out_specs=(pl.BlockSpec((tm, hs), lambda i: (i, 0)), pl.BlockSpec((tm, hs), lambda i: (i, 0)),
                   pl.BlockSpec((tm, hs), lambda i: (i, 0))),
        compiler_params=_cparams(("parallel",)),
        name="mla_qkv",
    )(z, z, z, tab, qag, kvag, wq, wkv, gq, gk)


def _attn_kernel(n_kv, tq, q_ref, *refs):
    k_refs = refs[:n_kv]
    v_refs = refs[n_kv:2 * n_kv]
    o_ref, s0_ref, s1_ref, m0_ref, m1_ref = refs[2 * n_kv:]
    slots = ((s0_ref, m0_ref), (s1_ref, m1_ref))
    n_tiles = q_ref.shape[0] // tq
    chunks = []
    col = 0
    for kv, k_ref in enumerate(k_refs):
        n_keys = k_ref.shape[0]
        step = min(ATTN_KEY_CHUNK, n_keys)
        for off in range(0, n_keys, step):
            chunks.append((kv, off, col, step))
            col += step

    def scores(t, slot):
        s_ref, m_ref = slots[slot]
        rows = pl.ds(pl.multiple_of(t * tq, tq), tq)
        q = q_ref[rows, :]
        run = None
        for kv, off, c0, size in chunks:
            s = lax.dot_general(q, k_refs[kv][off:off + size, :], (((1,), (1,)), ((), ())),
                                preferred_element_type=F32)
            s_ref[:, c0:c0 + size] = s
            for lb in range(size // LANES):
                blk = s[:, lb * LANES:(lb + 1) * LANES]
                run = blk if run is None else jnp.maximum(run, blk)
        m_ref[...] = run

    def finish(t, slot):
        s_ref, m_ref = slots[slot]
        rows = pl.ds(pl.multiple_of(t * tq, tq), tq)
        m = jnp.max(m_ref[...], axis=-1, keepdims=True)
        acc = None
        for kv, off, c0, size in chunks:
            p = jnp.exp2(s_ref[:, c0:c0 + size] - m).astype(BF16)
            pv = jnp.dot(p, v_refs[kv][off:off + size, :], preferred_element_type=F32)
            acc = pv if acc is None else acc + pv
        o_ref[rows, :] = (acc[:, :LANES] / acc[:, LANES:LANES + 1]).astype(BF16)

    scores(0, 0)

    def body(k, carry):
        scores(2 * k + 1, 1)
        finish(2 * k, 0)
        scores(jnp.minimum(2 * k + 2, n_tiles - 1), 0)
        finish(2 * k + 1, 1)
        return carry

    lax.fori_loop(0, n_tiles // 2, body, 0)
    if n_tiles % 2:
        finish(n_tiles - 1, 0)


def _attention(qo, ko, vo, b, s, tc, tq, latent):
    n_x = b * s
    ctx0 = n_x // tc
    cspec = pl.BlockSpec((tc, MLA_SLAB), lambda i, h: (ctx0 + i, h))
    xspec = pl.BlockSpec((s, MLA_SLAB), lambda i, h: (i, h))
    if latent:
        n_q, n_keys = s, s + tc
        in_specs = [xspec, cspec, xspec, cspec, xspec]
        args = (qo, ko, ko, vo, vo)
    else:
        n_q, n_keys = tc, tc
        in_specs = [cspec, cspec, cspec]
        args = (qo, ko, vo)
    return pl.pallas_call(
        functools.partial(_attn_kernel, (len(args) - 1) // 2, tq),
        out_shape=jax.ShapeDtypeStruct((b * n_q, MLA_HEADS * LANES), BF16),
        grid=(b, MLA_HEADS),
        in_specs=in_specs,
        out_specs=pl.BlockSpec((n_q, LANES), lambda i, h: (i, h)),
        scratch_shapes=[pltpu.VMEM((tq, n_keys), F32), pltpu.VMEM((tq, n_keys), F32),
                        pltpu.VMEM((tq, LANES), F32), pltpu.VMEM((tq, LANES), F32)],
        compiler_params=_cparams(("parallel", "parallel")),
        name="attn_latent" if latent else "attn_ctx",
    )(*args)


S5_HALF = S5_BLOCK_GROUPS * S5_STATE


def _s5_kernel(n_batch, rows_x, rows_c, dot_rows, u_ref, fac_ref, pw_ref, tab_ref, y_ref,
               u2_ref, v_ref, r_ref, m_ref, ot_ref):
    d = pl.program_id(1)
    n_dot = u2_ref.shape[0] // dot_rows

    for s in range(S5_SUB):
        blk = slice(s * LANES, (s + 1) * LANES)
        r_ref[blk, :] = (fac_ref[0, 0, 0] * pw_ref[0, 0, 0, s:s + 1, :]
                         + fac_ref[0, 0, 1] * pw_ref[0, 0, 1, s:s + 1, :]).astype(BF16)
        ot_ref[blk, :] = (fac_ref[0, 0, 2] * pw_ref[0, 0, 2, s:s + 1, :]
                          + fac_ref[0, 0, 3] * pw_ref[0, 0, 3, s:s + 1, :]).astype(BF16)

    c_t = fac_ref[0, 0, 2].astype(BF16)

    def toeplitz(reverse):
        taps = {}
        for s in range(S5_SUB):
            z = s if reverse else S5_SUB - 1 - s
            taps[z] = _dot_nt(r_ref[s * LANES:(s + 1) * LANES, :], c_t).astype(BF16)
        zero = jnp.zeros((LANES, LANES), BF16)
        for s in range(S5_SUB):
            for t in range(S5_SUB):
                lag = (s - t) if reverse else (t - s)
                m_ref[s * LANES:(s + 1) * LANES, t * LANES:(t + 1) * LANES] = taps[lag] if lag >= 0 else zero

    @pl.when(d == 0)
    def _():
        toeplitz(False)

    @pl.when(d == 1)
    def _():
        toeplitz(True)

    def dot_rows_of(i):
        return pl.ds(pl.multiple_of(i * dot_rows, dot_rows), dot_rows)

    def token_rows_of(i, s):
        return pl.ds(i * (dot_rows * S5_SUB) + s, dot_rows, stride=S5_SUB)

    @pl.when(d == 0)
    def _():
        def stage(i, carry):
            rows = pl.ds(pl.multiple_of(i * dot_rows, dot_rows), dot_rows)
            y_ref[rows, :] = u_ref[rows, :].astype(F32)
            return carry

        lax.fori_loop(0, n_dot * S5_SUB, stage, 0)

        def regroup(i, carry):
            for s in range(S5_SUB):
                u2_ref[dot_rows_of(i), s * LANES:(s + 1) * LANES] = y_ref[token_rows_of(i, s), :].astype(BF16)
            return carry

        lax.fori_loop(0, n_dot, regroup, 0)

    def increments(i, carry):
        rows = dot_rows_of(i)
        v_ref[rows, :] = jnp.dot(u2_ref[rows, :], r_ref[...], preferred_element_type=F32)
        return carry

    lax.fori_loop(0, n_dot, increments, 0)

    def cmul_add(ar, ai, cr, ci, xr, xi):
        return ar + cr * xr - ci * xi, ai + cr * xi + ci * xr

    def run(reverse):
        tab = tab_ref.at[0, 0]
        last = 0 if reverse else SUBLANES - 1
        first_row = lax.broadcasted_iota(jnp.int32, (SUBLANES, S5_HALF), 0) == (SUBLANES - 1 - last)

        def segment(bases, n_groups, carry):
            def body(i, carry):
                gi = (n_groups - 1 - i) if reverse else i
                out = []
                for base, (cre, cim) in zip(bases, carry):
                    rows = pl.ds(pl.multiple_of(base + gi * SUBLANES, SUBLANES), SUBLANES)
                    re = v_ref[rows, :S5_HALF]
                    im = v_ref[rows, S5_HALF:]
                    for lvl, shift in enumerate((1, 2, 4)):
                        sh = (SUBLANES - shift) if reverse else shift
                        re, im = cmul_add(re, im, tab[2 * lvl], tab[2 * lvl + 1],
                                          pltpu.roll(re, sh, axis=0), pltpu.roll(im, sh, axis=0))
                    re, im = cmul_add(re, im, tab[6], tab[7], cre, cim)
                    sh1 = (SUBLANES - 1) if reverse else 1
                    v_ref[rows, :S5_HALF] = jnp.where(first_row, cre, pltpu.roll(re, sh1, axis=0))
                    v_ref[rows, S5_HALF:] = jnp.where(first_row, cim, pltpu.roll(im, sh1, axis=0))
                    out.append((jnp.broadcast_to(re[last:last + 1, :], re.shape),
                                jnp.broadcast_to(im[last:last + 1, :], im.shape)))
                return tuple(out)
            return lax.fori_loop(0, n_groups, body, carry)

        zero = jnp.zeros((SUBLANES, S5_HALF), F32)
        carry = tuple((zero, zero) for _ in range(n_batch))
        carry = segment([n_batch * rows_x + bi * rows_c for bi in range(n_batch)], rows_c // SUBLANES, carry)
        segment([bi * rows_x for bi in range(n_batch)], rows_x // SUBLANES, carry)

        def outputs(i, carry):
            rows = dot_rows_of(i)
            y = jnp.dot(u2_ref[rows, :], m_ref[...], preferred_element_type=F32)
            y = y + lax.dot_general(v_ref[rows, :].astype(BF16), ot_ref[...], (((1,), (1,)), ((), ())),
                                    preferred_element_type=F32)
            for s in range(S5_SUB):
                part = y[:, s * LANES:(s + 1) * LANES]
                if reverse:
                    y_ref[token_rows_of(i, s), :] += part
                else:
                    y_ref[token_rows_of(i, s), :] = part
            return carry

        lax.fori_loop(0, n_dot, outputs, 0)

    @pl.when(d == 0)
    def _():
        run(False)

    @pl.when(d == 1)
    def _():
        run(True)


def _s5_scan(z, fac, pw, tabs, layer, b, s, tc, col_u):
    r = z.shape[0]
    rc = r // S5_SUB
    n_blocks = fac.shape[2]
    width = S5_SUB * LANES
    assert width == 2 * S5_HALF
    dot_rows = max(n for n in range(16, 641, 16) if rc % n == 0)
    return pl.pallas_call(
        functools.partial(_s5_kernel, b, s // S5_SUB, tc // S5_SUB, dot_rows),
        out_shape=jax.ShapeDtypeStruct((r, n_blocks * LANES), F32),
        grid=(n_blocks, 2),
        in_specs=[
            pl.BlockSpec((r, LANES), lambda cb, d: (0, col_u // LANES + cb), pipeline_mode=pl.Buffered(1)),
            pl.BlockSpec((None, 1, 1, 4, LANES, width), lambda cb, d: (layer, d, cb, 0, 0, 0),
                         pipeline_mode=pl.Buffered(1)),
            pl.BlockSpec((None, 1, 1, 4, S5_SUB, width), lambda cb, d: (layer, d, cb, 0, 0, 0)),
            pl.BlockSpec((None, 1, 1, 8, SUBLANES, S5_HALF), lambda cb, d: (layer, d, cb, 0, 0, 0)),
        ],
        out_specs=pl.BlockSpec((r, LANES), lambda cb, d: (0, cb)),
        scratch_shapes=[pltpu.VMEM((rc, width), BF16), pltpu.VMEM((rc, width), F32),
                        pltpu.VMEM((width, width), BF16), pltpu.VMEM((width, width), BF16),
                        pltpu.VMEM((width, width), BF16)],
        compiler_params=_cparams(("parallel", "arbitrary")),
        name="s5_scan",
    )(z, fac, pw, tabs)


def _glu_kernel(y_ref, u_ref, d_ref, w_ref, b_ref, o_ref, g_ref):
    tm = y_ref.shape[0]

    def body(r, carry):
        rows = pl.ds(pl.multiple_of(r * ROW_CHUNK, ROW_CHUNK), ROW_CHUNK)
        y = y_ref[rows, :] + d_ref[...] * u_ref[rows, :].astype(F32)
        g_ref[rows, :] = jax.nn.gelu(y).astype(BF16)
        return carry

    lax.fori_loop(0, tm // ROW_CHUNK, body, 0)
    g = g_ref[...]
    gate = jax.nn.sigmoid(jnp.dot(g, w_ref[...], preferred_element_type=F32) + b_ref[...])
    o_ref[...] = (g.astype(F32) * gate).astype(BF16)


def _s5_glu(y, z, d_skip, w_glu, b_glu, tm, col_u, n_row_tiles):
    width = y.shape[1]
    return pl.pallas_call(
        _glu_kernel,
        out_shape=jax.ShapeDtypeStruct((n_row_tiles * tm, width), BF16),
        grid=(n_row_tiles,),
        in_specs=[
            pl.BlockSpec((tm, width), lambda i: (i, 0)),
            pl.BlockSpec((tm, width), lambda i: (i, col_u // width)),
            pl.BlockSpec((1, width), lambda i: (0, 0)),
            pl.BlockSpec((width, width), lambda i: (0, 0)),
            pl.BlockSpec((1, width), lambda i: (0, 0)),
        ],
        out_specs=pl.BlockSpec((tm, width), lambda i: (i, 0)),
        scratch_shapes=[pltpu.VMEM((tm, width), BF16)],
        compiler_params=_cparams(("parallel",)),
        name="s5_glu",
    )(y, z, d_skip, w_glu, b_glu)


def _merge_kernel(n_x_tiles, ax_ref, ac_ref, bx_ref, bc_ref, c_ref, ga_ref, gb_ref, gc_ref, w_ref, o_ref):
    def combine(a_ref, b_ref):
        acc = None
        for r, (br_ref, gate_ref) in enumerate(((a_ref, ga_ref), (b_ref, gb_ref), (c_ref, gc_ref))):
            proj = jnp.dot(br_ref[...], w_ref[r], preferred_element_type=F32)
            term = jax.nn.sigmoid(gate_ref[...].astype(F32)) * proj
            acc = term if acc is None else acc + term
        o_ref[...] = acc.astype(BF16)

    @pl.when(pl.program_id(0) < n_x_tiles)
    def _():
        combine(ax_ref, bx_ref)

    @pl.when(pl.program_id(0) >= n_x_tiles)
    def _():
        combine(ac_ref, bc_ref)


def _merge(a_x, a_c, b_x, b_c, cc, z, w_branch, tm, tn, col_g, n_row_tiles, n_x_tiles):
    width = a_x.shape[1]
    d = w_branch.shape[2]

    def gate_spec(r):
        return pl.BlockSpec((tm, tn), lambda i, j: (i, (col_g + r * d) // tn + j))

    ax_spec, ac_spec = _stream_specs((tm, width), n_x_tiles, 0, lambda j: 0)
    return pl.pallas_call(
        functools.partial(_merge_kernel, n_x_tiles),
        out_shape=jax.ShapeDtypeStruct((n_row_tiles * tm, d), BF16),
        grid=(n_row_tiles, d // tn),
        in_specs=[
            ax_spec, ac_spec, ax_spec, ac_spec,
            pl.BlockSpec((tm, width), lambda i, j: (i, 0)),
            gate_spec(0), gate_spec(1), gate_spec(2),
            pl.BlockSpec((N_BRANCH, width, tn), lambda i, j: (0, 0, j)),
        ],
        out_specs=pl.BlockSpec((tm, tn), lambda i, j: (i, j)),
        compiler_params=_cparams(("parallel", "arbitrary")),
        name="merge",
    )(a_x, a_c, b_x, b_c, cc, z, z, z, w_branch)


def _resid_kernel(m_ref, w_ref, x_ref, al_ref, o_ref):
    o_ref[...] = x_ref[...] + al_ref[0] * jnp.dot(m_ref[...], w_ref[...], preferred_element_type=F32)


def _out_proj_residual(m, w, xs, mod3, tm, tn, n_row_tiles, n_x_tiles, tiles_per_batch, ctx_row, k_alpha):
    kdim, d = w.shape
    nt = d // tn

    def alpha_map(i, j):
        r = jnp.where(i < n_x_tiles, i // tiles_per_batch, ctx_row)
        return (r * N_MOD + k_alpha, 0, j)

    return pl.pallas_call(
        _resid_kernel,
        out_shape=jax.ShapeDtypeStruct((n_row_tiles * tm, d), F32),
        grid=(n_row_tiles, nt),
        in_specs=[
            pl.BlockSpec((tm, kdim), lambda i, j: (i, 0)),
            pl.BlockSpec((kdim, tn), lambda i, j: (0, j)),
            pl.BlockSpec((tm, tn), lambda i, j: (i, j)),
            pl.BlockSpec((1, 1, tn), alpha_map),
        ],
        out_specs=pl.BlockSpec((tm, tn), lambda i, j: (i, j)),
        compiler_params=_cparams(("parallel", "arbitrary")),
        name="out_proj",
    )(m, w, xs, mod3)


def _ff1_kernel(x_ref, g_ref, sh_ref, sc_ref, w_ref, h_ref, xn_ref):
    @pl.when(pl.program_id(1) == 0)
    def _():
        _norm_mod_rows(x_ref, g_ref, sh_ref, sc_ref, xn_ref)

    a = jnp.maximum(jnp.dot(xn_ref[...], w_ref[...], preferred_element_type=F32), 0.0)
    h_ref[...] = (a * a).astype(BF16)


def _ff1(xs, g, mod3, w, tm, tn, n_row_tiles, n_x_tiles, tiles_per_batch, ctx_row):
    d, dff = w.shape
    return pl.pallas_call(
        _ff1_kernel,
        out_shape=jax.ShapeDtypeStruct((n_row_tiles * tm, dff), BF16),
        grid=(n_row_tiles, dff // tn),
        in_specs=[
            pl.BlockSpec((tm, d), lambda i, j: (i, 0)),
            pl.BlockSpec((1, d), lambda i, j: (0, 0)),
            pl.BlockSpec((1, 1, d), _mod_row_map(n_x_tiles, tiles_per_batch, ctx_row, 3)),
            pl.BlockSpec((1, 1, d), _mod_row_map(n_x_tiles, tiles_per_batch, ctx_row, 4)),
            pl.BlockSpec((d, tn), lambda i, j: (0, j)),
        ],
        out_specs=pl.BlockSpec((tm, tn), lambda i, j: (i, j)),
        scratch_shapes=[pltpu.VMEM((tm, d), BF16)],
        compiler_params=_cparams(("parallel", "arbitrary")),
        name="ff1",
    )(xs, g, mod3, mod3, w)


def _ff2_kernel(h_ref, w_ref, x_ref, al_ref, o_ref):
    k = pl.program_id(2)
    part = jnp.dot(h_ref[...], w_ref[...], preferred_element_type=F32)

    @pl.when(k == 0)
    def _():
        o_ref[...] = part

    @pl.when(k > 0)
    def _():
        o_ref[...] += part

    @pl.when(k == pl.num_programs(2) - 1)
    def _():
        o_ref[...] = x_ref[...] + al_ref[0] * o_ref[...]


def _ff2(h, w, xs, mod3, tm, tn, tk, n_row_tiles, n_x_tiles, tiles_per_batch, ctx_row):
    dff, d = w.shape

    def alpha_map(i, j, k):
        r = jnp.where(i < n_x_tiles, i // tiles_per_batch, ctx_row)
        return (r * N_MOD + 5, 0, j)

    return pl.pallas_call(
        _ff2_kernel,
        out_shape=jax.ShapeDtypeStruct((n_row_tiles * tm, d), F32),
        grid=(n_row_tiles, d // tn, dff // tk),
        in_specs=[
            pl.BlockSpec((tm, tk), lambda i, j, k: (i, k)),
            pl.BlockSpec((tk, tn), lambda i, j, k: (k, j)),
            pl.BlockSpec((tm, tn), lambda i, j, k: (i, j)),
            pl.BlockSpec((1, 1, tn), alpha_map),
        ],
        out_specs=pl.BlockSpec((tm, tn), lambda i, j, k: (i, j)),
        compiler_params=_cparams(("parallel", "parallel", "arbitrary")),
        name="ff2",
    )(h, w, xs, mod3)


def _rope_partner():
    j = np.arange(MLA_ROPE)
    quarter = MLA_ROPE // 4
    return np.where((j // quarter) % 2 == 0, j + quarter, j - quarter)


def _rope_table(s, tm):
    pos = jnp.arange(s)
    row = (pos // GRID_W).astype(F32)
    col = (pos % GRID_W).astype(F32)
    n_freq = MLA_ROPE // 4
    inv_freq = ROPE_THETA ** (-jnp.arange(n_freq, dtype=F32) / n_freq)
    ang_r = row[:, None] * inv_freq
    ang_c = col[:, None] * inv_freq
    cos = jnp.concatenate([jnp.cos(ang_r)] * 2 + [jnp.cos(ang_c)] * 2, axis=-1)
    sin = jnp.concatenate([-jnp.sin(ang_r), jnp.sin(ang_r), -jnp.sin(ang_c), jnp.sin(ang_c)], axis=-1)
    ident = jnp.concatenate([jnp.ones((tm, MLA_ROPE), F32), jnp.zeros((tm, MLA_ROPE), F32)], axis=-1)
    return jnp.concatenate([jnp.concatenate([cos, sin], axis=-1), ident], axis=0)


def _pack_w_in(w_in, b_in, gate_b, sizes, n_pad):
    w_t = w_in.T
    d_model = w_in.shape[0]
    bounds = np.cumsum((0,) + sizes)
    seg = [slice(int(bounds[i]), int(bounds[i + 1])) for i in range(len(sizes))]
    partner = _rope_partner()
    order = (0, 1, 2, 3, 5, 6, 8, 9)
    names = ("q", "k", "v", "o", "qa", "kva", "u", "gates")
    offs, pos = {}, 0
    for name, i in zip(names, order):
        offs[name] = pos
        pos += sizes[i]
    offs["kpe"], offs["kpe_sw"] = pos, pos + MLA_ROPE
    n_zero = n_pad - pos - 2 * MLA_ROPE
    kpe_w, kpe_b = w_t[seg[7]], b_in[seg[7]]
    w = jnp.concatenate([w_t[seg[i]] for i in order] + [kpe_w, kpe_w[partner], jnp.zeros((n_zero, d_model), F32)],
                        axis=0).astype(BF16)
    b = jnp.concatenate([b_in[seg[i]] for i in order] + [kpe_b, kpe_b[partner], jnp.zeros((n_zero,), F32)])
    n_g = sizes[4]
    wg = jnp.concatenate([w_t[seg[4]], jnp.zeros((LANES - n_g, d_model), F32)], axis=0).astype(BF16)
    bg = jnp.concatenate([b_in[seg[4]] + gate_b.reshape(-1), jnp.zeros((LANES - n_g,), F32)])[None, :]
    return w, b[None, :], wg, bg, offs


def _pack_mla(w_uq, w_ukv, qn_g, kn_g):
    partner = _rope_partner()
    lora = w_uq.shape[0]
    wq = w_uq.reshape(lora, MLA_HEADS, MLA_DQK)
    wq = jnp.concatenate([wq, wq[:, :, MLA_NOPE + partner]], axis=-1).reshape(lora, MLA_HEADS * MLA_SLAB)
    wkv = w_ukv.reshape(w_ukv.shape[0], MLA_HEADS, -1)
    wkv = jnp.concatenate([wkv[:, :, :MLA_NOPE].reshape(lora, -1), wkv[:, :, MLA_NOPE:].reshape(lora, -1)], axis=-1)

    def gains(g):
        return jnp.concatenate([g, g[MLA_NOPE + partner]])[None, :]

    return wq.astype(BF16), wkv.astype(BF16), gains(qn_g), gains(kn_g)


def _pack_s5(a_re, a_im, log_dt, b_re, b_im, c_re, c_im):
    n_dir, n_groups, n_state = a_re.shape
    gc = b_re.shape[-1]
    nb = n_groups // S5_BLOCK_GROUPS
    lam_re = jnp.minimum(a_re.astype(F32), -1e-4)
    lam_im = a_im.astype(F32)
    dt = jnp.exp(log_dt.astype(F32))[..., None]

    def pole_power(k):
        mag = jnp.exp(k * lam_re * dt)
        return mag * jnp.cos(k * lam_im * dt), mag * jnp.sin(k * lam_im * dt)

    bar_re, bar_im = pole_power(1.0)
    den = lam_re * lam_re + lam_im * lam_im
    f_re = ((bar_re - 1.0) * lam_re + bar_im * lam_im) / den
    f_im = (bar_im * lam_re - (bar_re - 1.0) * lam_im) / den
    bb_re = f_re[..., None] * b_re.astype(F32) - f_im[..., None] * b_im.astype(F32)
    bb_im = f_re[..., None] * b_im.astype(F32) + f_im[..., None] * b_re.astype(F32)
    eye = jnp.eye(S5_BLOCK_GROUPS, dtype=F32)
    sub = S5_SUB

    def per_block(a):
        return a.reshape(a.shape[:-2] + (nb, S5_BLOCK_GROUPS * n_state))

    def block_b(part):
        p = part.reshape(n_dir, nb, S5_BLOCK_GROUPS, n_state, gc)
        m = jnp.einsum('dbgnc,gh->dbgchn', p, eye, precision=HIGHEST)
        return m.reshape(n_dir, nb, LANES, S5_HALF)

    def block_c(part):
        p = part.astype(F32).reshape(n_dir, nb, S5_BLOCK_GROUPS, gc, n_state)
        m = jnp.einsum('dbgcn,gh->dbgnhc', p, eye, precision=HIGHEST)
        return m.reshape(n_dir, nb, S5_HALF, LANES)

    bm_re, bm_im = block_b(bb_re), block_b(bb_im)
    cm_re, cm_im = block_c(c_re), block_c(c_im)

    tau = jnp.arange(sub + 1, dtype=F32)[:, None, None, None]
    p_re, p_im = (per_block(p) for p in pole_power(tau))
    ct_re, ct_im = jnp.swapaxes(cm_re, -1, -2), jnp.swapaxes(cm_im, -1, -2)
    cat = functools.partial(jnp.concatenate, axis=-1)
    fac = jnp.stack([cat([bm_re, bm_im]), cat([-bm_im, bm_re]), cat([ct_re, -ct_im]), cat([-ct_im, -ct_re])],
                    axis=2)
    t = np.arange(sub)
    pws = []
    for d in range(n_dir):
        to_exit = (sub - 1 - t) if d == 0 else t
        age = (t + 1) if d == 0 else (sub - t)
        rows = [p_re[to_exit, d], p_im[to_exit, d], p_re[age, d], p_im[age, d]]
        pws.append(jnp.stack([cat([a, a]).transpose(1, 0, 2) for a in rows], axis=1))
    pw = jnp.stack(pws)

    rows = jnp.arange(SUBLANES)
    tabs = []
    for d in range(n_dir):
        per_dir = []
        for shift in (1, 2, 4):
            keep = ((rows <= SUBLANES - 1 - shift) if d == 1 else (rows >= shift))[None, :, None]
            s_re, s_im = pole_power(float(shift * sub))
            per_dir += [jnp.where(keep, per_block(s_re[d])[:, None, :], 0.0),
                        jnp.where(keep, per_block(s_im[d])[:, None, :], 0.0)]
        expo = (((SUBLANES - rows) if d == 1 else (rows + 1)) * sub).astype(F32)
        s_re, s_im = pole_power(expo[:, None, None, None])
        per_dir += [jnp.moveaxis(per_block(s_re[:, d]), 0, 1), jnp.moveaxis(per_block(s_im[:, d]), 0, 1)]
        tabs.append(jnp.stack(per_dir, axis=1))
    return fac, pw, jnp.stack(tabs).astype(F32)


def _tri_matrices():
    t = np.arange(ML_CHUNK)
    lower = t[None, :] <= t[:, None]
    mask = np.where(np.stack([lower, lower.T]), 0.0, NEG_BIG).astype(np.float32)
    return jnp.asarray(lower.astype(np.float32), dtype=BF16), jnp.asarray(mask)


def kernel(x, c, ctx, c_ctx, w_mod, b_mod, norm_g, w_in, b_in, ml_gate_b, ml_norm_g, mla_qa_g, mla_kva_g, mla_w_uq, mla_w_ukv, mla_qn_g, mla_kn_g, s5_a_re, s5_a_im, s5_log_dt, s5_b_re, s5_b_im, s5_c_re, s5_c_im, s5_d, s5_w_glu, s5_b_glu, w_branch, w_out, w_ff1, w_ff2):
    b, s, d = x.shape
    tc = ctx.shape[1]
    depth = w_mod.shape[0]
    dv = ml_norm_g.shape[2]
    dk = dv // 2
    lora = mla_qa_g.shape[1]
    s5_width = s5_d.shape[1]
    branch_w = w_branch.shape[2]
    sizes = (ML_HEADS * dk, ML_HEADS * dk, ML_HEADS * dv, ML_HEADS * dv, 4 * ML_HEADS, lora, lora, MLA_ROPE,
             s5_width, N_BRANCH * d)
    assert sum(sizes) == w_in.shape[2] and b + 1 <= SUBLANES
    assert s % ML_CHUNK == 0 and tc % ML_CHUNK == 0 and branch_w == ML_HEADS * dv == MLA_HEADS * LANES == s5_width

    n_x = b * s
    n_c = b * tc
    tm = _row_tile(s, n_c)
    n_x_tiles = n_x // tm
    n_tiles = n_x_tiles + n_c // tm
    tiles_per_batch = s // tm
    tile_args = (n_x_tiles, tiles_per_batch, b)

    xs = jnp.concatenate([x.reshape(n_x, d), ctx.reshape(n_c, d)], axis=0)
    cc = jnp.concatenate([c, c_ctx[None, :], jnp.zeros((SUBLANES - b - 1, d), F32)], axis=0)
    mod = _modulation(cc, w_mod, b_mod)
    tm_q = min(tm, 512)
    tab = _rope_table(s, tm_q)
    tri, ml_mask = _tri_matrices()
    s5_packed = jax.vmap(_pack_s5)(s5_a_re, s5_a_im, s5_log_dt, s5_b_re, s5_b_im, s5_c_re, s5_c_im)
    n_used = sum(sizes) - sizes[4] + MLA_ROPE
    tn_in = 1280
    n_pad = -(-n_used // tn_in) * tn_in

    for l in range(depth):
        with_ctx_out = l < depth - 1
        mod3 = mod[l].reshape(SUBLANES * N_MOD, 1, d)
        w_p, b_p, wg, bg, offs = _pack_w_in(w_in[l], b_in[l], ml_gate_b[l], sizes, n_pad)
        z, gz = _in_proj(xs, norm_g[l, 0][None, :], mod3, w_p, b_p, wg, bg, tm, tn_in, *tile_args)

        gates, gates_t = _gate_prep(gz, tri)
        a_x, a_c = _mlstm(z, gates, gates_t, ml_mask, ml_norm_g[l], b, s, tc, dk, dv,
                          (offs["q"], offs["k"], offs["v"], offs["o"]), with_ctx_out)

        wq, wkv, gq, gk = _pack_mla(mla_w_uq[l], mla_w_ukv[l], mla_qn_g[l], mla_kn_g[l])
        qo, ko, vo = _mla_proj(z, tab, mla_qa_g[l][None, :], mla_kva_g[l][None, :], wq, wkv, gq, gk, tm_q,
                               (offs["qa"], offs["kva"], offs["kpe"]), n_x // tm_q, s // tm_q)
        tq = min(512, s)
        b_x = _attention(qo, ko, vo, b, s, tc, tq, True)

        y = _s5_scan(z, *s5_packed, l, b, s, tc, offs["u"])
        n_out_tiles = n_tiles if with_ctx_out else n_x_tiles
        c_all = _s5_glu(y, z, s5_d[l][None, :], s5_w_glu[l].astype(BF16), s5_b_glu[l][None, :], tm, offs["u"],
                        n_out_tiles)

        if with_ctx_out:
            b_c = _attention(qo, ko, vo, b, s, tc, min(tq, tc), False)
        else:
            a_c, b_c = a_x, b_x
        merged = _merge(a_x, a_c, b_x, b_c, c_all, z, w_branch[l].astype(BF16), tm, 512, offs["gates"],
                        n_out_tiles, n_x_tiles)
        xs1 = _out_proj_residual(merged, w_out[l].astype(BF16), xs, mod3, tm, 1024, n_out_tiles, *tile_args, 2)
        hid = _ff1(xs1, norm_g[l, 1][None, :], mod3, w_ff1[l].astype(BF16), tm, 1024, n_out_tiles, *tile_args)
        xs = _ff2(hid, w_ff2[l].astype(BF16), xs1, mod3, tm, 1024, 2048, n_out_tiles, *tile_args)

    return xs.reshape(b, s, d)
```

```python
import functools
import math

import jax
import jax.numpy as jnp
import numpy as np
from jax import lax
from jax.experimental import pallas as pl
from jax.experimental.pallas import tpu as pltpu

F32 = jnp.float32
BF16 = jnp.bfloat16
HIGHEST = lax.Precision.HIGHEST

N_MOD = 6
N_BRANCH = 3
ML_HEADS = 4
MLA_HEADS = 8
MLA_NOPE = 128
MLA_ROPE = 64
MLA_DQK = MLA_NOPE + MLA_ROPE
MLA_SLAB = 256
GRID_W = 64
ROPE_THETA = 10000.0
S5_GROUP = 16
S5_STATE = 64
S5_BLOCK_GROUPS = 8
EPS = 1e-6
NEG_BIG = -1e30

LANES = 128
SUBLANES = 8
VMEM_LIMIT = 56 * 1024 * 1024

ML_CHUNK = 256
ML_GATE_ROWS = 32
S5_SUB = 8
ATTN_KEY_CHUNK = 512
Q_PRESCALE = MLA_DQK ** -0.5 * math.log2(math.e)
ROW_CHUNK = 128
CAST_BLOCK_BYTES = 8 * 1024 * 1024


def _cparams(sem):
    return pltpu.CompilerParams(dimension_semantics=sem, vmem_limit_bytes=VMEM_LIMIT)


def _row_tile(n_x_rows_per_batch, n_ctx_rows):
    tm = 1024
    while n_x_rows_per_batch % tm or n_ctx_rows % tm:
        tm //= 2
    return tm


def _mod_kernel(c_ref, w_ref, b_ref, o_ref):
    s = c_ref[...]
    s = s * jax.nn.sigmoid(s)
    o_ref[0] = jnp.dot(s.astype(BF16), w_ref[0].astype(BF16), preferred_element_type=F32) + b_ref[0]


def _modulation(cc, w_mod, b_mod):
    n_layers, d, n = w_mod.shape
    tn = 1024
    return pl.pallas_call(
        _mod_kernel,
        out_shape=jax.ShapeDtypeStruct((n_layers, SUBLANES, n), F32),
        grid=(n_layers, n // tn),
        in_specs=[
            pl.BlockSpec((SUBLANES, d), lambda l, j: (0, 0)),
            pl.BlockSpec((1, d, tn), lambda l, j: (l, 0, j)),
            pl.BlockSpec((1, 1, tn), lambda l, j: (l, 0, j)),
        ],
        out_specs=pl.BlockSpec((1, SUBLANES, tn), lambda l, j: (l, 0, j)),
        compiler_params=_cparams(("parallel", "parallel")),
        name="adaln_mod",
    )(cc, w_mod, b_mod.reshape(n_layers, 1, n))


def _norm_mod_rows(x_ref, g_ref, sh_ref, sc_ref, xn_ref):
    tm = x_ref.shape[0]
    g = g_ref[...]
    sc = 1.0 + sc_ref[0]
    sh = sh_ref[0]

    def body(r, carry):
        rows = pl.ds(pl.multiple_of(r * ROW_CHUNK, ROW_CHUNK), ROW_CHUNK)
        x = x_ref[rows, :]
        ms = jnp.mean(x * x, axis=-1, keepdims=True)
        y = x * lax.rsqrt(ms + EPS) * g
        xn_ref[rows, :] = (y * sc + sh).astype(BF16)
        return carry

    lax.fori_loop(0, tm // ROW_CHUNK, body, 0)


def _mod_row_map(n_x_tiles, tiles_per_batch, ctx_row, k):
    def index_map(i, j):
        r = jnp.where(i < n_x_tiles, i // tiles_per_batch, ctx_row)
        return (r * N_MOD + k, 0, 0)
    return index_map


def _stream_specs(block, n_x_tiles, ctx_tile0, col_map):
    x_spec = pl.BlockSpec(block, lambda i, *r: (jnp.minimum(i, n_x_tiles - 1), col_map(*r)))
    c_spec = pl.BlockSpec(block, lambda i, *r: (ctx_tile0 + jnp.maximum(i - n_x_tiles, 0), col_map(*r)),
                          pipeline_mode=pl.Buffered(1))
    return x_spec, c_spec


def _dot_nt(a, b_t):
    return lax.dot_general(a, b_t, (((1,), (1,)), ((), ())), preferred_element_type=F32)


def _in_kernel(x_ref, g_ref, sh_ref, sc_ref, w_ref, b_ref, wg_ref, bg_ref, z_ref, gz_ref, xn_ref):
    @pl.when(pl.program_id(1) == 0)
    def _():
        _norm_mod_rows(x_ref, g_ref, sh_ref, sc_ref, xn_ref)
        gz_ref[...] = _dot_nt(xn_ref[...], wg_ref[...]) + bg_ref[...]

    z_ref[...] = (_dot_nt(xn_ref[...], w_ref[...]) + b_ref[...]).astype(BF16)


def _in_proj(xs, g, mod3, w, b, wg, bg, tm, tn, n_x_tiles, tiles_per_batch, ctx_row):
    r, d = xs.shape
    nz = w.shape[0]
    return pl.pallas_call(
        _in_kernel,
        out_shape=(jax.ShapeDtypeStruct((r, nz), BF16), jax.ShapeDtypeStruct((r, LANES), F32)),
        grid=(r // tm, nz // tn),
        in_specs=[
            pl.BlockSpec((tm, d), lambda i, j: (i, 0)),
            pl.BlockSpec((1, d), lambda i, j: (0, 0)),
            pl.BlockSpec((1, 1, d), _mod_row_map(n_x_tiles, tiles_per_batch, ctx_row, 0)),
            pl.BlockSpec((1, 1, d), _mod_row_map(n_x_tiles, tiles_per_batch, ctx_row, 1)),
            pl.BlockSpec((tn, d), lambda i, j: (j, 0)),
            pl.BlockSpec((1, tn), lambda i, j: (0, j)),
            pl.BlockSpec((LANES, d), lambda i, j: (0, 0)),
            pl.BlockSpec((1, LANES), lambda i, j: (0, 0)),
        ],
        out_specs=(
            pl.BlockSpec((tm, tn), lambda i, j: (i, j)),
            pl.BlockSpec((tm, LANES), lambda i, j: (i, 0)),
        ),
        scratch_shapes=[pltpu.VMEM((tm, d), BF16)],
        compiler_params=_cparams(("parallel", "arbitrary")),
        name="in_proj",
    )(xs, g, mod3, mod3, w, b, wg, bg)


def _log_sigmoid(x):
    return jnp.minimum(x, 0.0) - jnp.log1p(jnp.exp(-jnp.abs(x)))


def _split3(a):
    hi = a.astype(BF16)
    r1 = a - hi.astype(F32)
    mid = r1.astype(BF16)
    lo = (r1 - mid.astype(F32)).astype(BF16)
    return hi, mid, lo


def _gate_prep_kernel(g_ref, tri_ref, a_ref, at_ref):
    g = g_ref[...]
    kind = lax.broadcasted_iota(jnp.int32, g.shape, 1) // ML_HEADS
    lg = jnp.where((kind == 1) | (kind == 3), _log_sigmoid(g), g)
    pre = sum(jnp.dot(tri_ref[...], p, preferred_element_type=F32) for p in _split3(lg))
    suf = pre[ML_CHUNK - 1:ML_CHUNK, :] - pre + lg
    a = jnp.where(kind == 4, pltpu.roll(pre, 3 * ML_HEADS, axis=1),
                  jnp.where(kind == 5, pltpu.roll(suf, 2 * ML_HEADS, axis=1), lg))
    a_ref[...] = a
    at_ref[0] = a.T[:ML_GATE_ROWS, :]


def _gate_prep(gz, tri):
    r = gz.shape[0]
    return pl.pallas_call(
        _gate_prep_kernel,
        out_shape=(jax.ShapeDtypeStruct((r, LANES), F32),
                   jax.ShapeDtypeStruct((r // ML_CHUNK, ML_GATE_ROWS, ML_CHUNK), F32)),
        grid=(r // ML_CHUNK,),
        in_specs=[pl.BlockSpec((ML_CHUNK, LANES), lambda i: (i, 0)),
                  pl.BlockSpec((ML_CHUNK, ML_CHUNK), lambda i: (0, 0))],
        out_specs=(pl.BlockSpec((ML_CHUNK, LANES), lambda i: (i, 0)),
                   pl.BlockSpec((1, ML_GATE_ROWS, ML_CHUNK), lambda i: (i, 0, 0))),
        compiler_params=_cparams(("parallel",)),
        name="mlstm_gates",
    )(gz, tri)


def _ml_chunk(q, k_t, v, cum_cb, li_r, lf_r, cum_r, mask_add, carry, inv_scale):
    c_mat, n_mat, m = carry
    length, dv = v.shape
    crow = cum_r - li_r
    total = jnp.sum(lf_r, axis=-1, keepdims=True)
    ones = jnp.ones((length, LANES), BF16)
    n_blk = length // LANES

    def wide(a):
        return jnp.concatenate([a] * (dv // LANES), axis=1)

    log_w = [cum_cb - crow[:, j * LANES:(j + 1) * LANES] + mask_add[:, j * LANES:(j + 1) * LANES]
             for j in range(n_blk)]
    row_max = jnp.max(functools.reduce(jnp.maximum, log_w), axis=-1, keepdims=True)
    log_inter = cum_cb + m
    m_t = jnp.maximum(log_inter, row_max)
    w_inter = jnp.exp(log_inter - m_t)
    qk = jnp.dot(q, k_t, preferred_element_type=F32)
    s = jnp.concatenate([qk[:, j * LANES:(j + 1) * LANES] * jnp.exp(log_w[j] - m_t) for j in range(n_blk)],
                        axis=1).astype(BF16)
    num = wide(w_inter) * jnp.dot(q, c_mat.astype(BF16), preferred_element_type=F32)
    num = num + jnp.dot(s, v, preferred_element_type=F32)
    den = w_inter * jnp.dot(q, n_mat.astype(BF16), preferred_element_type=F32)
    den = den + jnp.dot(s, ones, preferred_element_type=F32)
    h = num * wide(1.0 / jnp.maximum(jnp.abs(den), jnp.exp(-m_t) * inv_scale))

    log_end = total - crow
    m_new = jnp.maximum(total + m, jnp.max(log_end, axis=-1, keepdims=True))
    decay = jnp.exp(total + m - m_new)
    kw_t = (k_t.astype(F32) * jnp.exp(log_end - m_new)).astype(BF16)
    c_new = decay * c_mat + jnp.dot(kw_t, v, preferred_element_type=F32)
    n_new = decay * n_mat + jnp.dot(kw_t, ones, preferred_element_type=F32)
    return h, (c_new, n_new, m_new)


def _mlstm_kernel(with_ctx_out, qx, kx, vx, ox, qc, kc, vc, oc, gx, gc, gtx, gtc, mask_ref, ng_ref, *rest):
    if with_ctx_out:
        ax_ref, ac_ref, cumx, ktx, cumc, ktc, hx, hc = rest
    else:
        ax_ref, cumx, ktx, cumc, ktc, hx, hc = rest
        ac_ref = None
    head = pl.program_id(1)
    dk = qx.shape[1]
    inv_scale = float(dk) ** 0.5
    n_x_chunks = qx.shape[0] // ML_CHUNK
    n_c_chunks = qc.shape[0] // ML_CHUNK

    def prep(g_ref, k_ref, cum_ref, kt_ref):
        def body(i, carry):
            rows = pl.ds(pl.multiple_of(i * ML_CHUNK, ML_CHUNK), ML_CHUNK)
            a = g_ref[rows, :]
            col = lax.broadcasted_iota(jnp.int32, a.shape, 1)
            for d in range(2):
                pick = jnp.sum(jnp.where(col == (4 + d) * ML_HEADS + head, a, 0.0), axis=-1, keepdims=True)
                cum_ref[d, rows, :] = jnp.broadcast_to(pick, a.shape)
            kt_ref[i] = k_ref[rows, :].astype(F32).T.astype(BF16)
            return carry
        n_chunks = g_ref.shape[0] // ML_CHUNK
        lax.fori_loop(0, n_chunks, body, 0, unroll=2 if n_chunks % 2 == 0 else 1)

    prep(gx, kx, cumx, ktx)
    prep(gc, kc, cumc, ktc)

    for d in range(2):
        reverse = d == 1
        mask_add = mask_ref[d]

        def step(q_ref, v_ref, cum_ref, gt_ref, kt_ref, h_ref, ci, carry):
            rows = pl.ds(pl.multiple_of(ci * ML_CHUNK, ML_CHUNK), ML_CHUNK)
            li_r, lf_r, cum_r = (gt_ref[ci, pl.ds(kind * ML_HEADS + head, 1), :] for kind in (2 * d, 2 * d + 1, 4 + d))
            h, carry = _ml_chunk(q_ref[rows, :], kt_ref[ci], v_ref[rows, :], cum_ref[d, rows, :], li_r, lf_r, cum_r,
                                 mask_add, carry, inv_scale)
            if reverse:
                h_ref[rows, :] += h
            else:
                h_ref[rows, :] = h
            return carry

        carry = (jnp.zeros((dk, vx.shape[1]), F32), jnp.zeros((dk, LANES), F32), jnp.zeros((1, 1), F32))

        def ctx_body(i, carry):
            ci = (n_c_chunks - 1 - i) if reverse else i
            return step(qc, vc, cumc, gtc, ktc, hc, ci, carry)

        def x_body(i, carry):
            ci = (n_x_chunks - 1 - i) if reverse else i
            return step(qx, vx, cumx, gtx, ktx, hx, ci, carry)

        carry = lax.fori_loop(0, n_c_chunks, ctx_body, carry)
        lax.fori_loop(0, n_x_chunks, x_body, carry, unroll=2 if n_x_chunks % 2 == 0 else 1)

    ng = ng_ref[0]

    def finish(h_ref, o_ref, a_ref):
        def body(i, carry):
            rows = pl.ds(pl.multiple_of(i * ML_CHUNK, ML_CHUNK), ML_CHUNK)
            h = h_ref[rows, :]
            hn = h * lax.rsqrt(jnp.mean(h * h, axis=-1, keepdims=True) + EPS) * ng
            a_ref[rows, :] = (hn * jax.nn.sigmoid(o_ref[rows, :].astype(F32))).astype(BF16)
            return carry
        lax.fori_loop(0, h_ref.shape[0] // ML_CHUNK, body, 0)

    finish(hx, ox, ax_ref)
    if with_ctx_out:
        finish(hc, oc, ac_ref)


def _mlstm(z, gates, gates_t, mask, ml_norm_g, b, s, tc, dk, dv, cols, with_ctx_out):
    n_x = b * s
    cq, ck, cv, co = cols
    ctx0 = n_x // tc
    xc, cc = s // ML_CHUNK, tc // ML_CHUNK

    def xspec(width, col0):
        return pl.BlockSpec((s, width), lambda i, h: (i, col0 // width + h))

    def cspec(width, col0):
        return pl.BlockSpec((tc, width), lambda i, h: (ctx0 + i, col0 // width + h))

    out_shape = [jax.ShapeDtypeStruct((n_x, ML_HEADS * dv), BF16)]
    out_specs = [pl.BlockSpec((s, dv), lambda i, h: (i, h))]
    if with_ctx_out:
        out_shape.append(jax.ShapeDtypeStruct((b * tc, ML_HEADS * dv), BF16))
        out_specs.append(pl.BlockSpec((tc, dv), lambda i, h: (i, h)))
    res = pl.pallas_call(
        functools.partial(_mlstm_kernel, with_ctx_out),
        out_shape=tuple(out_shape),
        grid=(b, ML_HEADS),
        in_specs=[
            xspec(dk, cq), xspec(dk, ck), xspec(dv, cv), xspec(dv, co),
            cspec(dk, cq), cspec(dk, ck), cspec(dv, cv), cspec(dv, co),
            pl.BlockSpec((s, LANES), lambda i, h: (i, 0)),
            pl.BlockSpec((tc, LANES), lambda i, h: (ctx0 + i, 0)),
            pl.BlockSpec((xc, ML_GATE_ROWS, ML_CHUNK), lambda i, h: (i, 0, 0)),
            pl.BlockSpec((cc, ML_GATE_ROWS, ML_CHUNK), lambda i, h: (b * xc // cc + i, 0, 0)),
            pl.BlockSpec((2, ML_CHUNK, ML_CHUNK), lambda i, h: (0, 0, 0)),
            pl.BlockSpec((1, 1, dv), lambda i, h: (h, 0, 0)),
        ],
        out_specs=tuple(out_specs),
        scratch_shapes=[
            pltpu.VMEM((2, s, LANES), F32), pltpu.VMEM((xc, dk, ML_CHUNK), BF16),
            pltpu.VMEM((2, tc, LANES), F32), pltpu.VMEM((cc, dk, ML_CHUNK), BF16),
            pltpu.VMEM((s, dv), F32), pltpu.VMEM((tc, dv), F32),
        ],
        compiler_params=_cparams(("parallel", "parallel")),
        name="mlstm",
    )(z, z, z, z, z, z, z, z, gates, gates, gates_t, gates_t, mask, ml_norm_g.reshape(ML_HEADS, 1, dv))
    return res if with_ctx_out else (res[0], None)


def _mla_proj_kernel(qa_ref, kva_ref, kpe_ref, tab_ref, qag_ref, kvag_ref, wq_ref, wkv_ref,
                     gq_ref, gk_ref, q_ref, k_ref, v_ref):
    def normed(a_ref, g_ref):
        a = a_ref[...].astype(F32)
        return (a * lax.rsqrt(jnp.mean(a * a, axis=-1, keepdims=True) + EPS) * g_ref[...]).astype(BF16)

    q_all = jnp.dot(normed(qa_ref, qag_ref), wq_ref[...], preferred_element_type=F32)
    kv_all = jnp.dot(normed(kva_ref, kvag_ref), wkv_ref[...], preferred_element_type=F32)
    tab = tab_ref[...]
    lane = lax.broadcasted_iota(jnp.int32, tab.shape, 1)
    first_half = lane < MLA_ROPE
    gq = gq_ref[...]
    gk = gk_ref[...]
    inv_dqk = 1.0 / MLA_DQK

    kpe = kpe_ref[...].astype(F32)
    ss_kpe = jnp.sum(jnp.where(first_half, kpe * kpe, 0.0), axis=-1, keepdims=True)
    kpe_t = kpe * (tab * gk[:, LANES:])
    kpe_rot = jnp.where(first_half, kpe_t + pltpu.roll(kpe_t, MLA_ROPE, axis=1), 0.0)

    for h in range(MLA_HEADS):
        qn = q_all[:, h * MLA_SLAB:h * MLA_SLAB + LANES]
        qp = q_all[:, h * MLA_SLAB + LANES:(h + 1) * MLA_SLAB]
        ss = jnp.sum(qn * qn, axis=-1, keepdims=True) + jnp.sum(jnp.where(first_half, qp * qp, 0.0), axis=-1,
                                                                keepdims=True)
        r = lax.rsqrt(ss * inv_dqk + EPS) * Q_PRESCALE
        qp_t = qp * (tab * gq[:, LANES:])
        qp_rot = qp_t + pltpu.roll(qp_t, MLA_ROPE, axis=1)
        q_ref[:, h * MLA_SLAB:h * MLA_SLAB + LANES] = (qn * r * gq[:, :LANES]).astype(BF16)
        q_ref[:, h * MLA_SLAB + LANES:(h + 1) * MLA_SLAB] = (qp_rot * r).astype(BF16)

        kn = kv_all[:, h * LANES:(h + 1) * LANES]
        rk = lax.rsqrt((jnp.sum(kn * kn, axis=-1, keepdims=True) + ss_kpe) * inv_dqk + EPS)
        k_ref[:, h * MLA_SLAB:h * MLA_SLAB + LANES] = (kn * rk * gk[:, :LANES]).astype(BF16)
        k_ref[:, h * MLA_SLAB + LANES:(h + 1) * MLA_SLAB] = (kpe_rot * rk).astype(BF16)

    ones_col = jnp.where(lane == 0, 1.0, 0.0).astype(BF16)
    for h in range(MLA_HEADS):
        v_ref[:, h * MLA_SLAB:h * MLA_SLAB + LANES] = kv_all[:, (MLA_HEADS + h) * LANES:(MLA_HEADS + h + 1) * LANES
                                                             ].astype(BF16)
        v_ref[:, h * MLA_SLAB + LANES:(h + 1) * MLA_SLAB] = ones_col


def _mla_proj(z, tab, qag, kvag, wq, wkv, gq, gk, tm, cols, n_x_tiles, tab_tiles):
    r = z.shape[0]
    cqa, ckva, ckpe = cols
    lora = qag.shape[1]
    hs = MLA_HEADS * MLA_SLAB
    hv = MLA_HEADS * LANES
    return pl.pallas_call(
        _mla_proj_kernel,
        out_shape=(jax.ShapeDtypeStruct((r, hs), BF16), jax.ShapeDtypeStruct((r, hs), BF16),
                   jax.ShapeDtypeStruct((r, hs), BF16)),
        grid=(r // tm,),
        in_specs=[
            pl.BlockSpec((tm, lora), lambda i: (i, cqa // lora)),
            pl.BlockSpec((tm, lora), lambda i: (i, ckva // lora)),
            pl.BlockSpec((tm, LANES), lambda i: (i, ckpe // LANES)),
            pl.BlockSpec((tm, LANES), lambda i: (jnp.where(i < n_x_tiles, i % tab_tiles, tab_tiles), 0)),
            pl.BlockSpec((1, lora), lambda i: (0, 0)),
            pl.BlockSpec((1, lora), lambda i: (0, 0)),
            pl.BlockSpec((lora, hs), lambda i: (0, 0)),
            pl.BlockSpec((lora, 2 * hv), lambda i: (0, 0)),
            pl.BlockSpec((1, MLA_SLAB), lambda i: (0, 0)),
            pl.BlockSpec((1, MLA_SLAB), lambda i: (0, 0)),
        ],
        out_specs=(pl.BlockSpec((tm, hs), lambda i: (i, 0)), pl.BlockSpec((tm, hs), lambda i: (i, 0)),
                   pl.BlockSpec((tm, hs), lambda i: (i, 0))),
        compiler_params=_cparams(("parallel",)),
        name="mla_qkv",
    )(z, z, z, tab, qag, kvag, wq, wkv, gq, gk)


def _attn_kernel(n_kv, tq, q_ref, *refs):
    k_refs = refs[:n_kv]
    v_refs = refs[n_kv:2 * n_kv]
    o_ref, s0_ref, s1_ref, m0_ref, m1_ref = refs[2 * n_kv:]
    slots = ((s0_ref, m0_ref), (s1_ref, m1_ref))
    n_tiles = q_ref.shape[0] // tq
    chunks = []
    col = 0
    for kv, k_ref in enumerate(k_refs):
        n_keys = k_ref.shape[0]
        step = min(ATTN_KEY_CHUNK, n_keys)
        for off in range(0, n_keys, step):
            chunks.append((kv, off, col, step))
            col += step

    def scores(t, slot):
        s_ref, m_ref = slots[slot]
        rows = pl.ds(pl.multiple_of(t * tq, tq), tq)
        q = q_ref[rows, :]
        run = None
        for kv, off, c0, size in chunks:
            s = lax.dot_general(q, k_refs[kv][off:off + size, :], (((1,), (1,)), ((), ())),
                                preferred_element_type=F32)
            s_ref[:, c0:c0 + size] = s
            for lb in range(size // LANES):
                blk = s[:, lb * LANES:(lb + 1) * LANES]
                run = blk if run is None else jnp.maximum(run, blk)
        m_ref[...] = run

    def finish(t, slot):
        s_ref, m_ref = slots[slot]
        rows = pl.ds(pl.multiple_of(t * tq, tq), tq)
        m = jnp.max(m_ref[...], axis=-1, keepdims=True)
        acc = None
        for kv, off, c0, size in chunks:
            p = jnp.exp2(s_ref[:, c0:c0 + size] - m).astype(BF16)
            pv = jnp.dot(p, v_refs[kv][off:off + size, :], preferred_element_type=F32)
            acc = pv if acc is None else acc + pv
        o_ref[rows, :] = (acc[:, :LANES] / acc[:, LANES:LANES + 1]).astype(BF16)

    scores(0, 0)

    def body(k, carry):
        scores(2 * k + 1, 1)
        finish(2 * k, 0)
        scores(jnp.minimum(2 * k + 2, n_tiles - 1), 0)
        finish(2 * k + 1, 1)
        return carry

    lax.fori_loop(0, n_tiles // 2, body, 0)
    if n_tiles % 2:
        finish(n_tiles - 1, 0)


def _attention(qo, ko, vo, b, s, tc, tq, latent):
    n_x = b * s
    ctx0 = n_x // tc
    cspec = pl.BlockSpec((tc, MLA_SLAB), lambda i, h: (ctx0 + i, h))
    xspec = pl.BlockSpec((s, MLA_SLAB), lambda i, h: (i, h))
    if latent:
        n_q, n_keys = s, s + tc
        in_specs = [xspec, cspec, xspec, cspec, xspec]
        args = (qo, ko, ko, vo, vo)
    else:
        n_q, n_keys = tc, tc
        in_specs = [cspec, cspec, cspec]
        args = (qo, ko, vo)
    return pl.pallas_call(
        functools.partial(_attn_kernel, (len(args) - 1) // 2, tq),
        out_shape=jax.ShapeDtypeStruct((b * n_q, MLA_HEADS * LANES), BF16),
        grid=(b, MLA_HEADS),
        in_specs=in_specs,
        out_specs=pl.BlockSpec((n_q, LANES), lambda i, h: (i, h)),
        scratch_shapes=[pltpu.VMEM((tq, n_keys), F32), pltpu.VMEM((tq, n_keys), F32),
                        pltpu.VMEM((tq, LANES), F32), pltpu.VMEM((tq, LANES), F32)],
        compiler_params=_cparams(("parallel", "parallel")),
        name="attn_latent" if latent else "attn_ctx",
    )(*args)


S5_HALF = S5_BLOCK_GROUPS * S5_STATE


def _s5_kernel(n_batch, rows_x, rows_c, dot_rows, u_ref, fac_ref, pw_ref, tab_ref, y_ref,
               u2_ref, v_ref, r_ref, m_ref, ot_ref):
    d = pl.program_id(1)
    n_dot = u2_ref.shape[0] // dot_rows

    for s in range(S5_SUB):
        blk = slice(s * LANES, (s + 1) * LANES)
        r_ref[blk, :] = (fac_ref[0, 0, 0] * pw_ref[0, 0, 0, s:s + 1, :]
                         + fac_ref[0, 0, 1] * pw_ref[0, 0, 1, s:s + 1, :]).astype(BF16)
        ot_ref[blk, :] = (fac_ref[0, 0, 2] * pw_ref[0, 0, 2, s:s + 1, :]
                          + fac_ref[0, 0, 3] * pw_ref[0, 0, 3, s:s + 1, :]).astype(BF16)

    c_t = fac_ref[0, 0, 2].astype(BF16)

    def toeplitz(reverse):
        taps = {}
        for s in range(S5_SUB):
            z = s if reverse else S5_SUB - 1 - s
            taps[z] = _dot_nt(r_ref[s * LANES:(s + 1) * LANES, :], c_t).astype(BF16)
        zero = jnp.zeros((LANES, LANES), BF16)
        for s in range(S5_SUB):
            for t in range(S5_SUB):
                lag = (s - t) if reverse else (t - s)
                m_ref[s * LANES:(s + 1) * LANES, t * LANES:(t + 1) * LANES] = taps[lag] if lag >= 0 else zero

    @pl.when(d == 0)
    def _():
        toeplitz(False)

    @pl.when(d == 1)
    def _():
        toeplitz(True)

    def dot_rows_of(i):
        return pl.ds(pl.multiple_of(i * dot_rows, dot_rows), dot_rows)

    def token_rows_of(i, s):
        return pl.ds(i * (dot_rows * S5_SUB) + s, dot_rows, stride=S5_SUB)

    @pl.when(d == 0)
    def _():
        def stage(i, carry):
            rows = pl.ds(pl.multiple_of(i * dot_rows, dot_rows), dot_rows)
            y_ref[rows, :] = u_ref[rows, :].astype(F32)
            return carry

        lax.fori_loop(0, n_dot * S5_SUB, stage, 0)

        def regroup(i, carry):
            for s in range(S5_SUB):
                u2_ref[dot_rows_of(i), s * LANES:(s + 1) * LANES] = y_ref[token_rows_of(i, s), :].astype(BF16)
            return carry

        lax.fori_loop(0, n_dot, regroup, 0)

    def increments(i, carry):
        rows = dot_rows_of(i)
        v_ref[rows, :] = jnp.dot(u2_ref[rows, :], r_ref[...], preferred_element_type=F32)
        return carry

    lax.fori_loop(0, n_dot, increments, 0)

    def cmul_add(ar, ai, cr, ci, xr, xi):
        return ar + cr * xr - ci * xi, ai + cr * xi + ci * xr

    def run(reverse):
        tab = tab_ref.at[0, 0]
        last = 0 if reverse else SUBLANES - 1
        first_row = lax.broadcasted_iota(jnp.int32, (SUBLANES, S5_HALF), 0) == (SUBLANES - 1 - last)

        def segment(bases, n_groups, carry):
            def body(i, carry):
                gi = (n_groups - 1 - i) if reverse else i
                out = []
                for base, (cre, cim) in zip(bases, carry):
                    rows = pl.ds(pl.multiple_of(base + gi * SUBLANES, SUBLANES), SUBLANES)
                    re = v_ref[rows, :S5_HALF]
                    im = v_ref[rows, S5_HALF:]
                    for lvl, shift in enumerate((1, 2, 4)):
                        sh = (SUBLANES - shift) if reverse else shift
                        re, im = cmul_add(re, im, tab[2 * lvl], tab[2 * lvl + 1],
                                          pltpu.roll(re, sh, axis=0), pltpu.roll(im, sh, axis=0))
                    re, im = cmul_add(re, im, tab[6], tab[7], cre, cim)
                    sh1 = (SUBLANES - 1) if reverse else 1
                    v_ref[rows, :S5_HALF] = jnp.where(first_row, cre, pltpu.roll(re, sh1, axis=0))
                    v_ref[rows, S5_HALF:] = jnp.where(first_row, cim, pltpu.roll(im, sh1, axis=0))
                    out.append((jnp.broadcast_to(re[last:last + 1, :], re.shape),
                                jnp.broadcast_to(im[last:last + 1, :], im.shape)))
                return tuple(out)
            return lax.fori_loop(0, n_groups, body, carry)

        zero = jnp.zeros((SUBLANES, S5_HALF), F32)
        carry = tuple((zero, zero) for _ in range(n_batch))
        carry = segment([n_batch * rows_x + bi * rows_c for bi in range(n_batch)], rows_c // SUBLANES, carry)
        segment([bi * rows_x for bi in range(n_batch)], rows_x // SUBLANES, carry)

        def outputs(i, carry):
            rows = dot_rows_of(i)
            y = jnp.dot(u2_ref[rows, :], m_ref[...], preferred_element_type=F32)
            y = y + lax.dot_general(v_ref[rows, :].astype(BF16), ot_ref[...], (((1,), (1,)), ((), ())),
                                    preferred_element_type=F32)
            for s in range(S5_SUB):
                part = y[:, s * LANES:(s + 1) * LANES]
                if reverse:
                    y_ref[token_rows_of(i, s), :] += part
                else:
                    y_ref[token_rows_of(i, s), :] = part
            return carry

        lax.fori_loop(0, n_dot, outputs, 0)

    @pl.when(d == 0)
    def _():
        run(False)

    @pl.when(d == 1)
    def _():
        run(True)


def _s5_scan(z, fac, pw, tabs, layer, b, s, tc, col_u):
    r = z.shape[0]
    rc = r // S5_SUB
    n_blocks = fac.shape[2]
    width = S5_SUB * LANES
    assert width == 2 * S5_HALF
    dot_rows = max(n for n in range(16, 641, 16) if rc % n == 0)
    return pl.pallas_call(
        functools.partial(_s5_kernel, b, s // S5_SUB, tc // S5_SUB, dot_rows),
        out_shape=jax.ShapeDtypeStruct((r, n_blocks * LANES), F32),
        grid=(n_blocks, 2),
        in_specs=[
            pl.BlockSpec((r, LANES), lambda cb, d: (0, col_u // LANES + cb), pipeline_mode=pl.Buffered(1)),
            pl.BlockSpec((None, 1, 1, 4, LANES, width), lambda cb, d: (layer, d, cb, 0, 0, 0),
                         pipeline_mode=pl.Buffered(1)),
            pl.BlockSpec((None, 1, 1, 4, S5_SUB, width), lambda cb, d: (layer, d, cb, 0, 0, 0)),
            pl.BlockSpec((None, 1, 1, 8, SUBLANES, S5_HALF), lambda cb, d: (layer, d, cb, 0, 0, 0)),
        ],
        out_specs=pl.BlockSpec((r, LANES), lambda cb, d: (0, cb)),
        scratch_shapes=[pltpu.VMEM((rc, width), BF16), pltpu.VMEM((rc, width), F32),
                        pltpu.VMEM((width, width), BF16), pltpu.VMEM((width, width), BF16),
                        pltpu.VMEM((width, width), BF16)],
        compiler_params=_cparams(("parallel", "arbitrary")),
        name="s5_scan",
    )(z, fac, pw, tabs)


def _glu_kernel(y_ref, u_ref, d_ref, w_ref, b_ref, o_ref, g_ref):
    tm = y_ref.shape[0]

    def body(r, carry):
        rows = pl.ds(pl.multiple_of(r * ROW_CHUNK, ROW_CHUNK), ROW_CHUNK)
        y = y_ref[rows, :] + d_ref[...] * u_ref[rows, :].astype(F32)
        g_ref[rows, :] = jax.nn.gelu(y).astype(BF16)
        return carry

    lax.fori_loop(0, tm // ROW_CHUNK, body, 0)
    g = g_ref[...]
    gate = jax.nn.sigmoid(jnp.dot(g, w_ref[...], preferred_element_type=F32) + b_ref[...])
    o_ref[...] = (g.astype(F32) * gate).astype(BF16)


def _s5_glu(y, z, d_skip, w_glu, b_glu, tm, col_u, n_row_tiles):
    width = y.shape[1]
    return pl.pallas_call(
        _glu_kernel,
        out_shape=jax.ShapeDtypeStruct((n_row_tiles * tm, width), BF16),
        grid=(n_row_tiles,),
        in_specs=[
            pl.BlockSpec((tm, width), lambda i: (i, 0)),
            pl.BlockSpec((tm, width), lambda i: (i, col_u // width)),
            pl.BlockSpec((1, width), lambda i: (0, 0)),
            pl.BlockSpec((width, width), lambda i: (0, 0)),
            pl.BlockSpec((1, width), lambda i: (0, 0)),
        ],
        out_specs=pl.BlockSpec((tm, width), lambda i: (i, 0)),
        scratch_shapes=[pltpu.VMEM((tm, width), BF16)],
        compiler_params=_cparams(("parallel",)),
        name="s5_glu",
    )(y, z, d_skip, w_glu, b_glu)


def _merge_kernel(n_x_tiles, ax_ref, ac_ref, bx_ref, bc_ref, c_ref, ga_ref, gb_ref, gc_ref, w_ref, o_ref):
    def combine(a_ref, b_ref):
        acc = None
        for r, (br_ref, gate_ref) in enumerate(((a_ref, ga_ref), (b_ref, gb_ref), (c_ref, gc_ref))):
            proj = jnp.dot(br_ref[...], w_ref[r], preferred_element_type=F32)
            term = jax.nn.sigmoid(gate_ref[...].astype(F32)) * proj
            acc = term if acc is None else acc + term
        o_ref[...] = acc.astype(BF16)

    @pl.when(pl.program_id(0) < n_x_tiles)
    def _():
        combine(ax_ref, bx_ref)

    @pl.when(pl.program_id(0) >= n_x_tiles)
    def _():
        combine(ac_ref, bc_ref)


def _merge(a_x, a_c, b_x, b_c, cc, z, w_branch, tm, tn, col_g, n_row_tiles, n_x_tiles):
    width = a_x.shape[1]
    d = w_branch.shape[2]

    def gate_spec(r):
        return pl.BlockSpec((tm, tn), lambda i, j: (i, (col_g + r * d) // tn + j))

    ax_spec, ac_spec = _stream_specs((tm, width), n_x_tiles, 0, lambda j: 0)
    return pl.pallas_call(
        functools.partial(_merge_kernel, n_x_tiles),
        out_shape=jax.ShapeDtypeStruct((n_row_tiles * tm, d), BF16),
        grid=(n_row_tiles, d // tn),
        in_specs=[
            ax_spec, ac_spec, ax_spec, ac_spec,
            pl.BlockSpec((tm, width), lambda i, j: (i, 0)),
            gate_spec(0), gate_spec(1), gate_spec(2),
            pl.BlockSpec((N_BRANCH, width, tn), lambda i, j: (0, 0, j)),
        ],
        out_specs=pl.BlockSpec((tm, tn), lambda i, j: (i, j)),
        compiler_params=_cparams(("parallel", "arbitrary")),
        name="merge",
    )(a_x, a_c, b_x, b_c, cc, z, z, z, w_branch)


def _resid_kernel(m_ref, w_ref, x_ref, al_ref, o_ref):
    o_ref[...] = x_ref[...] + al_ref[0] * jnp.dot(m_ref[...], w_ref[...], preferred_element_type=F32)


def _out_proj_residual(m, w, xs, mod3, tm, tn, n_row_tiles, n_x_tiles, tiles_per_batch, ctx_row, k_alpha):
    kdim, d = w.shape
    nt = d // tn

    def alpha_map(i, j):
        r = jnp.where(i < n_x_tiles, i // tiles_per_batch, ctx_row)
        return (r * N_MOD + k_alpha, 0, j)

    return pl.pallas_call(
        _resid_kernel,
        out_shape=jax.ShapeDtypeStruct((n_row_tiles * tm, d), F32),
        grid=(n_row_tiles, nt),
        in_specs=[
            pl.BlockSpec((tm, kdim), lambda i, j: (i, 0)),
            pl.BlockSpec((kdim, tn), lambda i, j: (0, j)),
            pl.BlockSpec((tm, tn), lambda i, j: (i, j)),
            pl.BlockSpec((1, 1, tn), alpha_map),
        ],
        out_specs=pl.BlockSpec((tm, tn), lambda i, j: (i, j)),
        compiler_params=_cparams(("parallel", "arbitrary")),
        name="out_proj",
    )(m, w, xs, mod3)


def _ff1_kernel(x_ref, g_ref, sh_ref, sc_ref, w_ref, h_ref, xn_ref):
    @pl.when(pl.program_id(1) == 0)
    def _():
        _norm_mod_rows(x_ref, g_ref, sh_ref, sc_ref, xn_ref)

    a = jnp.maximum(jnp.dot(xn_ref[...], w_ref[...], preferred_element_type=F32), 0.0)
    h_ref[...] = (a * a).astype(BF16)


def _ff1(xs, g, mod3, w, tm, tn, n_row_tiles, n_x_tiles, tiles_per_batch, ctx_row):
    d, dff = w.shape
    return pl.pallas_call(
        _ff1_kernel,
        out_shape=jax.ShapeDtypeStruct((n_row_tiles * tm, dff), BF16),
        grid=(n_row_tiles, dff // tn),
        in_specs=[
            pl.BlockSpec((tm, d), lambda i, j: (i, 0)),
            pl.BlockSpec((1, d), lambda i, j: (0, 0)),
            pl.BlockSpec((1, 1, d), _mod_row_map(n_x_tiles, tiles_per_batch, ctx_row, 3)),
            pl.BlockSpec((1, 1, d), _mod_row_map(n_x_tiles, tiles_per_batch, ctx_row, 4)),
            pl.BlockSpec((d, tn), lambda i, j: (0, j)),
        ],
        out_specs=pl.BlockSpec((tm, tn), lambda i, j: (i, j)),
        scratch_shapes=[pltpu.VMEM((tm, d), BF16)],
        compiler_params=_cparams(("parallel", "arbitrary")),
        name="ff1",
    )(xs, g, mod3, mod3, w)


def _ff2_kernel(h_ref, w_ref, x_ref, al_ref, o_ref):
    k = pl.program_id(2)
    part = jnp.dot(h_ref[...], w_ref[...], preferred_element_type=F32)

    @pl.when(k == 0)
    def _():
        o_ref[...] = part

    @pl.when(k > 0)
    def _():
        o_ref[...] += part

    @pl.when(k == pl.num_programs(2) - 1)
    def _():
        o_ref[...] = x_ref[...] + al_ref[0] * o_ref[...]


def _ff2(h, w, xs, mod3, tm, tn, tk, n_row_tiles, n_x_tiles, tiles_per_batch, ctx_row):
    dff, d = w.shape

    def alpha_map(i, j, k):
        r = jnp.where(i < n_x_tiles, i // tiles_per_batch, ctx_row)
        return (r * N_MOD + 5, 0, j)

    return pl.pallas_call(
        _ff2_kernel,
        out_shape=jax.ShapeDtypeStruct((n_row_tiles * tm, d), F32),
        grid=(n_row_tiles, d // tn, dff // tk),
        in_specs=[
            pl.BlockSpec((tm, tk), lambda i, j, k: (i, k)),
            pl.BlockSpec((tk, tn), lambda i, j, k: (k, j)),
            pl.BlockSpec((tm, tn), lambda i, j, k: (i, j)),
            pl.BlockSpec((1, 1, tn), alpha_map),
        ],
        out_specs=pl.BlockSpec((tm, tn), lambda i, j, k: (i, j)),
        compiler_params=_cparams(("parallel", "parallel", "arbitrary")),
        name="ff2",
    )(h, w, xs, mod3)


def _rope_partner():
    j = np.arange(MLA_ROPE)
    quarter = MLA_ROPE // 4
    return np.where((j // quarter) % 2 == 0, j + quarter, j - quarter)


def _rope_table(s, tm):
    pos = jnp.arange(s)
    row = (pos // GRID_W).astype(F32)
    col = (pos % GRID_W).astype(F32)
    n_freq = MLA_ROPE // 4
    inv_freq = ROPE_THETA ** (-jnp.arange(n_freq, dtype=F32) / n_freq)
    ang_r = row[:, None] * inv_freq
    ang_c = col[:, None] * inv_freq
    cos = jnp.concatenate([jnp.cos(ang_r)] * 2 + [jnp.cos(ang_c)] * 2, axis=-1)
    sin = jnp.concatenate([-jnp.sin(ang_r), jnp.sin(ang_r), -jnp.sin(ang_c), jnp.sin(ang_c)], axis=-1)
    ident = jnp.concatenate([jnp.ones((tm, MLA_ROPE), F32), jnp.zeros((tm, MLA_ROPE), F32)], axis=-1)
    return jnp.concatenate([jnp.concatenate([cos, sin], axis=-1), ident], axis=0)


def _cast_kernel(w_ref, o_ref):
    o_ref[...] = w_ref[...].astype(BF16)


def _layer_bf16(w_all, layer):
    rows, cols = w_all.shape[1:]
    tr = CAST_BLOCK_BYTES // (4 * cols)
    while rows % tr:
        tr //= 2
    return pl.pallas_call(
        _cast_kernel,
        out_shape=jax.ShapeDtypeStruct((rows, cols), BF16),
        grid=(rows // tr,),
        in_specs=[pl.BlockSpec((None, tr, cols), lambda i: (layer, i, 0))],
        out_specs=pl.BlockSpec((tr, cols), lambda i: (i, 0)),
        compiler_params=_cparams(("parallel",)),
        name="cast_bf16",
    )(w_all)


def _pack_w_in(w_in, b_in, gate_b, sizes, n_pad):
    w_t = w_in.T
    d_model = w_in.shape[0]
    bounds = np.cumsum((0,) + sizes)
    seg = [slice(int(bounds[i]), int(bounds[i + 1])) for i in range(len(sizes))]
    partner = _rope_partner()
    order = (0, 1, 2, 3, 5, 6, 8, 9)
    names = ("q", "k", "v", "o", "qa", "kva", "u", "gates")
    offs, pos = {}, 0
    for name, i in zip(names, order):
        offs[name] = pos
        pos += sizes[i]
    offs["kpe"], offs["kpe_sw"] = pos, pos + MLA_ROPE
    n_zero = n_pad - pos - 2 * MLA_ROPE
    kpe_w, kpe_b = w_t[seg[7]], b_in[seg[7]]
    w = jnp.concatenate([w_t[seg[i]] for i in order] + [kpe_w, kpe_w[partner], jnp.zeros((n_zero, d_model), F32)],
                        axis=0).astype(BF16)
    b = jnp.concatenate([b_in[seg[i]] for i in order] + [kpe_b, kpe_b[partner], jnp.zeros((n_zero,), F32)])
    n_g = sizes[4]
    wg = jnp.concatenate([w_t[seg[4]], jnp.zeros((LANES - n_g, d_model), F32)], axis=0).astype(BF16)
    bg = jnp.concatenate([b_in[seg[4]] + gate_b.reshape(-1), jnp.zeros((LANES - n_g,), F32)])[None, :]
    return w, b[None, :], wg, bg, offs


def _pack_mla(w_uq, w_ukv, qn_g, kn_g):
    partner = _rope_partner()
    lora = w_uq.shape[0]
    wq = w_uq.reshape(lora, MLA_HEADS, MLA_DQK)
    wq = jnp.concatenate([wq, wq[:, :, MLA_NOPE + partner]], axis=-1).reshape(lora, MLA_HEADS * MLA_SLAB)
    wkv = w_ukv.reshape(w_ukv.shape[0], MLA_HEADS, -1)
    wkv = jnp.concatenate([wkv[:, :, :MLA_NOPE].reshape(lora, -1), wkv[:, :, MLA_NOPE:].reshape(lora, -1)], axis=-1)

    def gains(g):
        return jnp.concatenate([g, g[MLA_NOPE + partner]])[None, :]

    return wq.astype(BF16), wkv.astype(BF16), gains(qn_g), gains(kn_g)


def _pack_s5(a_re, a_im, log_dt, b_re, b_im, c_re, c_im):
    n_dir, n_groups, n_state = a_re.shape
    gc = b_re.shape[-1]
    nb = n_groups // S5_BLOCK_GROUPS
    lam_re = jnp.minimum(a_re.astype(F32), -1e-4)
    lam_im = a_im.astype(F32)
    dt = jnp.exp(log_dt.astype(F32))[..., None]

    def pole_power(k):
        mag = jnp.exp(k * lam_re * dt)
        return mag * jnp.cos(k * lam_im * dt), mag * jnp.sin(k * lam_im * dt)

    bar_re, bar_im = pole_power(1.0)
    den = lam_re * lam_re + lam_im * lam_im
    f_re = ((bar_re - 1.0) * lam_re + bar_im * lam_im) / den
    f_im = (bar_im * lam_re - (bar_re - 1.0) * lam_im) / den
    bb_re = f_re[..., None] * b_re.astype(F32) - f_im[..., None] * b_im.astype(F32)
    bb_im = f_re[..., None] * b_im.astype(F32) + f_im[..., None] * b_re.astype(F32)
    eye = jnp.eye(S5_BLOCK_GROUPS, dtype=F32)
    sub = S5_SUB

    def per_block(a):
        return a.reshape(a.shape[:-2] + (nb, S5_BLOCK_GROUPS * n_state))

    def block_b(part):
        p = part.reshape(n_dir, nb, S5_BLOCK_GROUPS, n_state, gc)
        m = jnp.einsum('dbgnc,gh->dbgchn', p, eye, precision=HIGHEST)
        return m.reshape(n_dir, nb, LANES, S5_HALF)

    def block_c(part):
        p = part.astype(F32).reshape(n_dir, nb, S5_BLOCK_GROUPS, gc, n_state)
        m = jnp.einsum('dbgcn,gh->dbgnhc', p, eye, precision=HIGHEST)
        return m.reshape(n_dir, nb, S5_HALF, LANES)

    bm_re, bm_im = block_b(bb_re), block_b(bb_im)
    cm_re, cm_im = block_c(c_re), block_c(c_im)

    tau = jnp.arange(sub + 1, dtype=F32)[:, None, None, None]
    p_re, p_im = (per_block(p) for p in pole_power(tau))
    ct_re, ct_im = jnp.swapaxes(cm_re, -1, -2), jnp.swapaxes(cm_im, -1, -2)
    cat = functools.partial(jnp.concatenate, axis=-1)
    fac = jnp.stack([cat([bm_re, bm_im]), cat([-bm_im, bm_re]), cat([ct_re, -ct_im]), cat([-ct_im, -ct_re])],
                    axis=2)
    t = np.arange(sub)
    pws = []
    for d in range(n_dir):
        to_exit = (sub - 1 - t) if d == 0 else t
        age = (t + 1) if d == 0 else (sub - t)
        rows = [p_re[to_exit, d], p_im[to_exit, d], p_re[age, d], p_im[age, d]]
        pws.append(jnp.stack([cat([a, a]).transpose(1, 0, 2) for a in rows], axis=1))
    pw = jnp.stack(pws)

    rows = jnp.arange(SUBLANES)
    tabs = []
    for d in range(n_dir):
        per_dir = []
        for shift in (1, 2, 4):
            keep = ((rows <= SUBLANES - 1 - shift) if d == 1 else (rows >= shift))[None, :, None]
            s_re, s_im = pole_power(float(shift * sub))
            per_dir += [jnp.where(keep, per_block(s_re[d])[:, None, :], 0.0),
                        jnp.where(keep, per_block(s_im[d])[:, None, :], 0.0)]
        expo = (((SUBLANES - rows) if d == 1 else (rows + 1)) * sub).astype(F32)
        s_re, s_im = pole_power(expo[:, None, None, None])
        per_dir += [jnp.moveaxis(per_block(s_re[:, d]), 0, 1), jnp.moveaxis(per_block(s_im[:, d]), 0, 1)]
        tabs.append(jnp.stack(per_dir, axis=1))
    return fac, pw, jnp.stack(tabs).astype(F32)


def _tri_matrices():
    t = np.arange(ML_CHUNK)
    lower = t[None, :] <= t[:, None]
    mask = np.where(np.stack([lower, lower.T]), 0.0, NEG_BIG).astype(np.float32)
    return jnp.asarray(lower.astype(np.float32), dtype=BF16), jnp.asarray(mask)


def kernel(x, c, ctx, c_ctx, w_mod, b_mod, norm_g, w_in, b_in, ml_gate_b, ml_norm_g, mla_qa_g, mla_kva_g, mla_w_uq, mla_w_ukv, mla_qn_g, mla_kn_g, s5_a_re, s5_a_im, s5_log_dt, s5_b_re, s5_b_im, s5_c_re, s5_c_im, s5_d, s5_w_glu, s5_b_glu, w_branch, w_out, w_ff1, w_ff2):
    b, s, d = x.shape
    tc = ctx.shape[1]
    depth = w_mod.shape[0]
    dv = ml_norm_g.shape[2]
    dk = dv // 2
    lora = mla_qa_g.shape[1]
    s5_width = s5_d.shape[1]
    branch_w = w_branch.shape[2]
    sizes = (ML_HEADS * dk, ML_HEADS * dk, ML_HEADS * dv, ML_HEADS * dv, 4 * ML_HEADS, lora, lora, MLA_ROPE,
             s5_width, N_BRANCH * d)
    assert sum(sizes) == w_in.shape[2] and b + 1 <= SUBLANES
    assert s % ML_CHUNK == 0 and tc % ML_CHUNK == 0 and branch_w == ML_HEADS * dv == MLA_HEADS * LANES == s5_width

    n_x = b * s
    n_c = b * tc
    tm = _row_tile(s, n_c)
    n_x_tiles = n_x // tm
    n_tiles = n_x_tiles + n_c // tm
    tiles_per_batch = s // tm
    tile_args = (n_x_tiles, tiles_per_batch, b)

    xs = jnp.concatenate([x.reshape(n_x, d), ctx.reshape(n_c, d)], axis=0)
    cc = jnp.concatenate([c, c_ctx[None, :], jnp.zeros((SUBLANES - b - 1, d), F32)], axis=0)
    mod = _modulation(cc, w_mod, b_mod)
    tm_q = min(tm, 512)
    tab = _rope_table(s, tm_q)
    tri, ml_mask = _tri_matrices()
    s5_packed = jax.vmap(_pack_s5)(s5_a_re, s5_a_im, s5_log_dt, s5_b_re, s5_b_im, s5_c_re, s5_c_im)
    n_used = sum(sizes) - sizes[4] + MLA_ROPE
    tn_in = 1280
    n_pad = -(-n_used // tn_in) * tn_in

    for l in range(depth):
        with_ctx_out = l < depth - 1
        mod3 = mod[l].reshape(SUBLANES * N_MOD, 1, d)
        w_p, b_p, wg, bg, offs = _pack_w_in(w_in[l], b_in[l], ml_gate_b[l], sizes, n_pad)
        z, gz = _in_proj(xs, norm_g[l, 0][None, :], mod3, w_p, b_p, wg, bg, tm, tn_in, *tile_args)

        gates, gates_t = _gate_prep(gz, tri)
        a_x, a_c = _mlstm(z, gates, gates_t, ml_mask, ml_norm_g[l], b, s, tc, dk, dv,
                          (offs["q"], offs["k"], offs["v"], offs["o"]), with_ctx_out)

        wq, wkv, gq, gk = _pack_mla(mla_w_uq[l], mla_w_ukv[l], mla_qn_g[l], mla_kn_g[l])
        qo, ko, vo = _mla_proj(z, tab, mla_qa_g[l][None, :], mla_kva_g[l][None, :], wq, wkv, gq, gk, tm_q,
                               (offs["qa"], offs["kva"], offs["kpe"]), n_x // tm_q, s // tm_q)
        tq = min(512, s)
        b_x = _attention(qo, ko, vo, b, s, tc, tq, True)

        y = _s5_scan(z, *s5_packed, l, b, s, tc, offs["u"])
        n_out_tiles = n_tiles if with_ctx_out else n_x_tiles
        c_all = _s5_glu(y, z, s5_d[l][None, :], s5_w_glu[l].astype(BF16), s5_b_glu[l][None, :], tm, offs["u"],
                        n_out_tiles)

        if with_ctx_out:
            b_c = _attention(qo, ko, vo, b, s, tc, min(tq, tc), False)
        else:
            a_c, b_c = a_x, b_x
        w_br = _layer_bf16(w_branch.reshape(depth, N_BRANCH * branch_w, d), l).reshape(N_BRANCH, branch_w, d)
        merged = _merge(a_x, a_c, b_x, b_c, c_all, z, w_br, tm, 512, offs["gates"],
                        n_out_tiles, n_x_tiles)
        xs1 = _out_proj_residual(merged, _layer_bf16(w_out, l), xs, mod3, tm, 1024, n_out_tiles, *tile_args, 2)
        hid = _ff1(xs1, norm_g[l, 1][None, :], mod3, _layer_bf16(w_ff1, l), tm, 1024, n_out_tiles, *tile_args)
        xs = _ff2(hid, _layer_bf16(w_ff2, l), xs1, mod3, tm, 1024, 2048, n_out_tiles, *tile_args)

    return xs.reshape(b, s, d)
```

```python
import functools
import math

import jax
import jax.numpy as jnp
import numpy as np
from jax import lax
from jax.experimental import pallas as pl
from jax.experimental.pallas import tpu as pltpu

F32 = jnp.float32
BF16 = jnp.bfloat16
HIGHEST = lax.Precision.HIGHEST

N_MOD = 6
N_BRANCH = 3
ML_HEADS = 4
MLA_HEADS = 8
MLA_NOPE = 128
MLA_ROPE = 64
MLA_DQK = MLA_NOPE + MLA_ROPE
MLA_SLAB = 256
GRID_W = 64
ROPE_THETA = 10000.0
S5_GROUP = 16
S5_STATE = 64
S5_BLOCK_GROUPS = 8
EPS = 1e-6
NEG_BIG = -1e30

LANES = 128
SUBLANES = 8
VMEM_LIMIT = 56 * 1024 * 1024

ML_CHUNK = 256
ML_GATE_ROWS = 32
S5_SUB = 8
ATTN_KEY_CHUNK = 512
Q_PRESCALE = MLA_DQK ** -0.5 * math.log2(math.e)
ROW_CHUNK = 128
CAST_BLOCK_BYTES = 8 * 1024 * 1024


def _cparams(sem):
    return pltpu.CompilerParams(dimension_semantics=sem, vmem_limit_bytes=VMEM_LIMIT)


def _row_tile(n_x_rows_per_batch, n_ctx_rows):
    tm = 1024
    while n_x_rows_per_batch % tm or n_ctx_rows % tm:
        tm //= 2
    return tm


def _mod_kernel(c_ref, w_ref, b_ref, o_ref):
    s = c_ref[...]
    s = s * jax.nn.sigmoid(s)
    o_ref[0] = jnp.dot(s.astype(BF16), w_ref[0].astype(BF16), preferred_element_type=F32) + b_ref[0]


def _modulation(cc, w_mod, b_mod):
    n_layers, d, n = w_mod.shape
    tn = 1024
    return pl.pallas_call(
        _mod_kernel,
        out_shape=jax.ShapeDtypeStruct((n_layers, SUBLANES, n), F32),
        grid=(n_layers, n // tn),
        in_specs=[
            pl.BlockSpec((SUBLANES, d), lambda l, j: (0, 0)),
            pl.BlockSpec((1, d, tn), lambda l, j: (l, 0, j)),
            pl.BlockSpec((1, 1, tn), lambda l, j: (l, 0, j)),
        ],
        out_specs=pl.BlockSpec((1, SUBLANES, tn), lambda l, j: (l, 0, j)),
        compiler_params=_cparams(("parallel", "parallel")),
        name="adaln_mod",
    )(cc, w_mod, b_mod.reshape(n_layers, 1, n))


def _norm_mod_rows(x_ref, g_ref, sh_ref, sc_ref, xn_ref):
    tm = x_ref.shape[0]
    g = g_ref[...]
    sc = 1.0 + sc_ref[0]
    sh = sh_ref[0]

    def body(r, carry):
        rows = pl.ds(pl.multiple_of(r * ROW_CHUNK, ROW_CHUNK), ROW_CHUNK)
        x = x_ref[rows, :]
        ms = jnp.mean(x * x, axis=-1, keepdims=True)
        y = x * lax.rsqrt(ms + EPS) * g
        xn_ref[rows, :] = (y * sc + sh).astype(BF16)
        return carry

    lax.fori_loop(0, tm // ROW_CHUNK, body, 0)


def _mod_row_map(n_x_tiles, tiles_per_batch, ctx_row, k):
    def index_map(i, j):
        r = jnp.where(i < n_x_tiles, i // tiles_per_batch, ctx_row)
        return (r * N_MOD + k, 0, 0)
    return index_map


def _stream_specs(block, n_x_tiles, ctx_tile0, col_map):
    x_spec = pl.BlockSpec(block, lambda i, *r: (jnp.minimum(i, n_x_tiles - 1), col_map(*r)))
    c_spec = pl.BlockSpec(block, lambda i, *r: (ctx_tile0 + jnp.maximum(i - n_x_tiles, 0), col_map(*r)),
                          pipeline_mode=pl.Buffered(1))
    return x_spec, c_spec


def _dot_nt(a, b_t):
    return lax.dot_general(a, b_t, (((1,), (1,)), ((), ())), preferred_element_type=F32)


def _in_kernel(n_x_tiles, x_ref, c_ref, g_ref, sh_ref, sc_ref, w_ref, b_ref, wg_ref, bg_ref, z_ref, gz_ref, xn_ref):
    @pl.when(pl.program_id(1) == 0)
    def _():
        @pl.when(pl.program_id(0) < n_x_tiles)
        def _():
            _norm_mod_rows(x_ref, g_ref, sh_ref, sc_ref, xn_ref)

        @pl.when(pl.program_id(0) >= n_x_tiles)
        def _():
            _norm_mod_rows(c_ref, g_ref, sh_ref, sc_ref, xn_ref)

        gz_ref[...] = _dot_nt(xn_ref[...], wg_ref[...]) + bg_ref[...]

    z_ref[...] = (_dot_nt(xn_ref[...], w_ref[...]) + b_ref[...]).astype(BF16)


def _in_proj(x_src, c_src, ctx_tile0, n_tiles, g, mod3, w, b, wg, bg, tm, tn, n_x_tiles, tiles_per_batch, ctx_row):
    d = x_src.shape[1]
    r = n_tiles * tm
    nz = w.shape[0]
    x_spec, c_spec = _stream_specs((tm, d), n_x_tiles, ctx_tile0, lambda j: 0)
    return pl.pallas_call(
        functools.partial(_in_kernel, n_x_tiles),
        out_shape=(jax.ShapeDtypeStruct((r, nz), BF16), jax.ShapeDtypeStruct((r, LANES), F32)),
        grid=(n_tiles, nz // tn),
        in_specs=[
            x_spec, c_spec,
            pl.BlockSpec((1, d), lambda i, j: (0, 0)),
            pl.BlockSpec((1, 1, d), _mod_row_map(n_x_tiles, tiles_per_batch, ctx_row, 0)),
            pl.BlockSpec((1, 1, d), _mod_row_map(n_x_tiles, tiles_per_batch, ctx_row, 1)),
            pl.BlockSpec((tn, d), lambda i, j: (j, 0)),
            pl.BlockSpec((1, tn), lambda i, j: (0, j)),
            pl.BlockSpec((LANES, d), lambda i, j: (0, 0)),
            pl.BlockSpec((1, LANES), lambda i, j: (0, 0)),
        ],
        out_specs=(
            pl.BlockSpec((tm, tn), lambda i, j: (i, j)),
            pl.BlockSpec((tm, LANES), lambda i, j: (i, 0)),
        ),
        scratch_shapes=[pltpu.VMEM((tm, d), BF16)],
        compiler_params=_cparams(("parallel", "arbitrary")),
        name="in_proj",
    )(x_src, c_src, g, mod3, mod3, w, b, wg, bg)


def _log_sigmoid(x):
    return jnp.minimum(x, 0.0) - jnp.log1p(jnp.exp(-jnp.abs(x)))


def _split3(a):
    hi = a.astype(BF16)
    r1 = a - hi.astype(F32)
    mid = r1.astype(BF16)
    lo = (r1 - mid.astype(F32)).astype(BF16)
    return hi, mid, lo


def _gate_prep_kernel(g_ref, tri_ref, a_ref, at_ref):
    g = g_ref[...]
    kind = lax.broadcasted_iota(jnp.int32, g.shape, 1) // ML_HEADS
    lg = jnp.where((kind == 1) | (kind == 3), _log_sigmoid(g), g)
    pre = sum(jnp.dot(tri_ref[...], p, preferred_element_type=F32) for p in _split3(lg))
    suf = pre[ML_CHUNK - 1:ML_CHUNK, :] - pre + lg
    a = jnp.where(kind == 4, pltpu.roll(pre, 3 * ML_HEADS, axis=1),
                  jnp.where(kind == 5, pltpu.roll(suf, 2 * ML_HEADS, axis=1), lg))
    a_ref[...] = a
    at_ref[0] = a.T[:ML_GATE_ROWS, :]


def _gate_prep(gz, tri):
    r = gz.shape[0]
    return pl.pallas_call(
        _gate_prep_kernel,
        out_shape=(jax.ShapeDtypeStruct((r, LANES), F32),
                   jax.ShapeDtypeStruct((r // ML_CHUNK, ML_GATE_ROWS, ML_CHUNK), F32)),
        grid=(r // ML_CHUNK,),
        in_specs=[pl.BlockSpec((ML_CHUNK, LANES), lambda i: (i, 0)),
                  pl.BlockSpec((ML_CHUNK, ML_CHUNK), lambda i: (0, 0))],
        out_specs=(pl.BlockSpec((ML_CHUNK, LANES), lambda i: (i, 0)),
                   pl.BlockSpec((1, ML_GATE_ROWS, ML_CHUNK), lambda i: (i, 0, 0))),
        compiler_params=_cparams(("parallel",)),
        name="mlstm_gates",
    )(gz, tri)


def _ml_chunk(q, k_t, v, cum_cb, li_r, lf_r, cum_r, mask_add, carry, inv_scale):
    c_mat, n_mat, m = carry
    length, dv = v.shape
    crow = cum_r - li_r
    total = jnp.sum(lf_r, axis=-1, keepdims=True)
    ones = jnp.ones((length, LANES), BF16)
    n_blk = length // LANES

    def wide(a):
        return jnp.concatenate([a] * (dv // LANES), axis=1)

    log_w = [cum_cb - crow[:, j * LANES:(j + 1) * LANES] + mask_add[:, j * LANES:(j + 1) * LANES]
             for j in range(n_blk)]
    row_max = jnp.max(functools.reduce(jnp.maximum, log_w), axis=-1, keepdims=True)
    log_inter = cum_cb + m
    m_t = jnp.maximum(log_inter, row_max)
    w_inter = jnp.exp(log_inter - m_t)
    qk = jnp.dot(q, k_t, preferred_element_type=F32)
    s = jnp.concatenate([qk[:, j * LANES:(j + 1) * LANES] * jnp.exp(log_w[j] - m_t) for j in range(n_blk)],
                        axis=1).astype(BF16)
    num = wide(w_inter) * jnp.dot(q, c_mat.astype(BF16), preferred_element_type=F32)
    num = num + jnp.dot(s, v, preferred_element_type=F32)
    den = w_inter * jnp.dot(q, n_mat.astype(BF16), preferred_element_type=F32)
    den = den + jnp.dot(s, ones, preferred_element_type=F32)
    h = num * wide(1.0 / jnp.maximum(jnp.abs(den), jnp.exp(-m_t) * inv_scale))

    log_end = total - crow
    m_new = jnp.maximum(total + m, jnp.max(log_end, axis=-1, keepdims=True))
    decay = jnp.exp(total + m - m_new)
    kw_t = (k_t.astype(F32) * jnp.exp(log_end - m_new)).astype(BF16)
    c_new = decay * c_mat + jnp.dot(kw_t, v, preferred_element_type=F32)
    n_new = decay * n_mat + jnp.dot(kw_t, ones, preferred_element_type=F32)
    return h, (c_new, n_new, m_new)


def _mlstm_kernel(with_ctx_out, qx, kx, vx, ox, qc, kc, vc, oc, gx, gc, gtx, gtc, mask_ref, ng_ref, *rest):
    if with_ctx_out:
        ax_ref, ac_ref, cumx, ktx, cumc, ktc, hx, hc = rest
    else:
        ax_ref, cumx, ktx, cumc, ktc, hx, hc = rest
        ac_ref = None
    head = pl.program_id(1)
    dk = qx.shape[1]
    inv_scale = float(dk) ** 0.5
    n_x_chunks = qx.shape[0] // ML_CHUNK
    n_c_chunks = qc.shape[0] // ML_CHUNK

    def prep(g_ref, k_ref, cum_ref, kt_ref):
        def body(i, carry):
            rows = pl.ds(pl.multiple_of(i * ML_CHUNK, ML_CHUNK), ML_CHUNK)
            a = g_ref[rows, :]
            col = lax.broadcasted_iota(jnp.int32, a.shape, 1)
            for d in range(2):
                pick = jnp.sum(jnp.where(col == (4 + d) * ML_HEADS + head, a, 0.0), axis=-1, keepdims=True)
                cum_ref[d, rows, :] = jnp.broadcast_to(pick, a.shape)
            kt_ref[i] = k_ref[rows, :].astype(F32).T.astype(BF16)
            return carry
        n_chunks = g_ref.shape[0] // ML_CHUNK
        lax.fori_loop(0, n_chunks, body, 0, unroll=2 if n_chunks % 2 == 0 else 1)

    prep(gx, kx, cumx, ktx)
    prep(gc, kc, cumc, ktc)

    for d in range(2):
        reverse = d == 1
        mask_add = mask_ref[d]

        def step(q_ref, v_ref, cum_ref, gt_ref, kt_ref, h_ref, ci, carry):
            rows = pl.ds(pl.multiple_of(ci * ML_CHUNK, ML_CHUNK), ML_CHUNK)
            li_r, lf_r, cum_r = (gt_ref[ci, pl.ds(kind * ML_HEADS + head, 1), :] for kind in (2 * d, 2 * d + 1, 4 + d))
            h, carry = _ml_chunk(q_ref[rows, :], kt_ref[ci], v_ref[rows, :], cum_ref[d, rows, :], li_r, lf_r, cum_r,
                                 mask_add, carry, inv_scale)
            if reverse:
                h_ref[rows, :] += h
            else:
                h_ref[rows, :] = h
            return carry

        carry = (jnp.zeros((dk, vx.shape[1]), F32), jnp.zeros((dk, LANES), F32), jnp.zeros((1, 1), F32))

        def ctx_body(i, carry):
            ci = (n_c_chunks - 1 - i) if reverse else i
            return step(qc, vc, cumc, gtc, ktc, hc, ci, carry)

        def x_body(i, carry):
            ci = (n_x_chunks - 1 - i) if reverse else i
            return step(qx, vx, cumx, gtx, ktx, hx, ci, carry)

        carry = lax.fori_loop(0, n_c_chunks, ctx_body, carry)
        lax.fori_loop(0, n_x_chunks, x_body, carry, unroll=2 if n_x_chunks % 2 == 0 else 1)

    ng = ng_ref[0]

    def finish(h_ref, o_ref, a_ref):
        def body(i, carry):
            rows = pl.ds(pl.multiple_of(i * ML_CHUNK, ML_CHUNK), ML_CHUNK)
            h = h_ref[rows, :]
            hn = h * lax.rsqrt(jnp.mean(h * h, axis=-1, keepdims=True) + EPS) * ng
            a_ref[rows, :] = (hn * jax.nn.sigmoid(o_ref[rows, :].astype(F32))).astype(BF16)
            return carry
        lax.fori_loop(0, h_ref.shape[0] // ML_CHUNK, body, 0)

    finish(hx, ox, ax_ref)
    if with_ctx_out:
        finish(hc, oc, ac_ref)


def _mlstm(z, gates, gates_t, mask, ml_norm_g, b, s, tc, dk, dv, cols, with_ctx_out):
    n_x = b * s
    cq, ck, cv, co = cols
    ctx0 = n_x // tc
    xc, cc = s // ML_CHUNK, tc // ML_CHUNK

    def xspec(width, col0):
        return pl.BlockSpec((s, width), lambda i, h: (i, col0 // width + h))

    def cspec(width, col0):
        return pl.BlockSpec((tc, width), lambda i, h: (ctx0 + i, col0 // width + h))

    out_shape = [jax.ShapeDtypeStruct((n_x, ML_HEADS * dv), BF16)]
    out_specs = [pl.BlockSpec((s, dv), lambda i, h: (i, h))]
    if with_ctx_out:
        out_shape.append(jax.ShapeDtypeStruct((b * tc, ML_HEADS * dv), BF16))
        out_specs.append(pl.BlockSpec((tc, dv), lambda i, h: (i, h)))
    res = pl.pallas_call(
        functools.partial(_mlstm_kernel, with_ctx_out),
        out_shape=tuple(out_shape),
        grid=(b, ML_HEADS),
        in_specs=[
            xspec(dk, cq), xspec(dk, ck), xspec(dv, cv), xspec(dv, co),
            cspec(dk, cq), cspec(dk, ck), cspec(dv, cv), cspec(dv, co),
            pl.BlockSpec((s, LANES), lambda i, h: (i, 0)),
            pl.BlockSpec((tc, LANES), lambda i, h: (ctx0 + i, 0)),
            pl.BlockSpec((xc, ML_GATE_ROWS, ML_CHUNK), lambda i, h: (i, 0, 0)),
            pl.BlockSpec((cc, ML_GATE_ROWS, ML_CHUNK), lambda i, h: (b * xc // cc + i, 0, 0)),
            pl.BlockSpec((2, ML_CHUNK, ML_CHUNK), lambda i, h: (0, 0, 0)),
            pl.BlockSpec((1, 1, dv), lambda i, h: (h, 0, 0)),
        ],
        out_specs=tuple(out_specs),
        scratch_shapes=[
            pltpu.VMEM((2, s, LANES), F32), pltpu.VMEM((xc, dk, ML_CHUNK), BF16),
            pltpu.VMEM((2, tc, LANES), F32), pltpu.VMEM((cc, dk, ML_CHUNK), BF16),
            pltpu.VMEM((s, dv), F32), pltpu.VMEM((tc, dv), F32),
        ],
        compiler_params=_cparams(("parallel", "parallel")),
        name="mlstm",
    )(z, z, z, z, z, z, z, z, gates, gates, gates_t, gates_t, mask, ml_norm_g.reshape(ML_HEADS, 1, dv))
    return res if with_ctx_out else (res[0], None)


def _mla_proj_kernel(qa_ref, kva_ref, kpe_ref, tab_ref, qag_ref, kvag_ref, wq_ref, wkv_ref,
                     gq_ref, gk_ref, q_ref, k_ref, v_ref):
    def normed(a_ref, g_ref):
        a = a_ref[...].astype(F32)
        return (a * lax.rsqrt(jnp.mean(a * a, axis=-1, keepdims=True) + EPS) * g_ref[...]).astype(BF16)

    q_all = jnp.dot(normed(qa_ref, qag_ref), wq_ref[...], preferred_element_type=F32)
    kv_all = jnp.dot(normed(kva_ref, kvag_ref), wkv_ref[...], preferred_element_type=F32)
    tab = tab_ref[...]
    lane = lax.broadcasted_iota(jnp.int32, tab.shape, 1)
    first_half = lane < MLA_ROPE
    gq = gq_ref[...]
    gk = gk_ref[...]
    inv_dqk = 1.0 / MLA_DQK

    kpe = kpe_ref[...].astype(F32)
    ss_kpe = jnp.sum(jnp.where(first_half, kpe * kpe, 0.0), axis=-1, keepdims=True)
    kpe_t = kpe * (tab * gk[:, LANES:])
    kpe_rot = jnp.where(first_half, kpe_t + pltpu.roll(kpe_t, MLA_ROPE, axis=1), 0.0)

    for h in range(MLA_HEADS):
        qn = q_all[:, h * MLA_SLAB:h * MLA_SLAB + LANES]
        qp = q_all[:, h * MLA_SLAB + LANES:(h + 1) * MLA_SLAB]
        ss = jnp.sum(qn * qn, axis=-1, keepdims=True) + jnp.sum(jnp.where(first_half, qp * qp, 0.0), axis=-1,
                                                                keepdims=True)
        r = lax.rsqrt(ss * inv_dqk + EPS) * Q_PRESCALE
        qp_t = qp * (tab * gq[:, LANES:])
        qp_rot = qp_t + pltpu.roll(qp_t, MLA_ROPE, axis=1)
        q_ref[:, h * MLA_SLAB:h * MLA_SLAB + LANES] = (qn * r * gq[:, :LANES]).astype(BF16)
        q_ref[:, h * MLA_SLAB + LANES:(h + 1) * MLA_SLAB] = (qp_rot * r).astype(BF16)

        kn = kv_all[:, h * LANES:(h + 1) * LANES]
        rk = lax.rsqrt((jnp.sum(kn * kn, axis=-1, keepdims=True) + ss_kpe) * inv_dqk + EPS)
        k_ref[:, h * MLA_SLAB:h * MLA_SLAB + LANES] = (kn * rk * gk[:, :LANES]).astype(BF16)
        k_ref[:, h * MLA_SLAB + LANES:(h + 1) * MLA_SLAB] = (kpe_rot * rk).astype(BF16)

    ones_col = jnp.where(lane == 0, 1.0, 0.0).astype(BF16)
    for h in range(MLA_HEADS):
        v_ref[:, h * MLA_SLAB:h * MLA_SLAB + LANES] = kv_all[:, (MLA_HEADS + h) * LANES:(MLA_HEADS + h + 1) * LANES
                                                             ].astype(BF16)
        v_ref[:, h * MLA_SLAB + LANES:(h + 1) * MLA_SLAB] = ones_col


def _mla_proj(z, tab, qag, kvag, wq, wkv, gq, gk, tm, cols, n_x_tiles, tab_tiles):
    r = z.shape[0]
    cqa, ckva, ckpe = cols
    lora = qag.shape[1]
    hs = MLA_HEADS * MLA_SLAB
    hv = MLA_HEADS * LANES
    return pl.pallas_call(
        _mla_proj_kernel,
        out_shape=(jax.ShapeDtypeStruct((r, hs), BF16), jax.ShapeDtypeStruct((r, hs), BF16),
                   jax.ShapeDtypeStruct((r, hs), BF16)),
        grid=(r // tm,),
        in_specs=[
            pl.BlockSpec((tm, lora), lambda i: (i, cqa // lora)),
            pl.BlockSpec((tm, lora), lambda i: (i, ckva // lora)),
            pl.BlockSpec((tm, LANES), lambda i: (i, ckpe // LANES)),
            pl.BlockSpec((tm, LANES), lambda i: (jnp.where(i < n_x_tiles, i % tab_tiles, tab_tiles), 0)),
            pl.BlockSpec((1, lora), lambda i: (0, 0)),
            pl.BlockSpec((1, lora), lambda i: (0, 0)),
            pl.BlockSpec((lora, hs), lambda i: (0, 0)),
            pl.BlockSpec((lora, 2 * hv), lambda i: (0, 0)),
            pl.BlockSpec((1, MLA_SLAB), lambda i: (0, 0)),
            pl.BlockSpec((1, MLA_SLAB), lambda i: (0, 0)),
        ],
        out_specs=(pl.BlockSpec((tm, hs), lambda i: (i, 0)), pl.BlockSpec((tm, hs), lambda i: (i, 0)),
                   pl.BlockSpec((tm, hs), lambda i: (i, 0))),
        compiler_params=_cparams(("parallel",)),
        name="mla_qkv",
    )(z, z, z, tab, qag, kvag, wq, wkv, gq, gk)


def _attn_kernel(n_kv, tq, q_ref, *refs):
    k_refs = refs[:n_kv]
    v_refs = refs[n_kv:2 * n_kv]
    o_ref, s0_ref, s1_ref, m0_ref, m1_ref = refs[2 * n_kv:]
    slots = ((s0_ref, m0_ref), (s1_ref, m1_ref))
    n_tiles = q_ref.shape[0] // tq
    chunks = []
    col = 0
    for kv, k_ref in enumerate(k_refs):
        n_keys = k_ref.shape[0]
        step = min(ATTN_KEY_CHUNK, n_keys)
        for off in range(0, n_keys, step):
            chunks.append((kv, off, col, step))
            col += step

    def scores(t, slot):
        s_ref, m_ref = slots[slot]
        rows = pl.ds(pl.multiple_of(t * tq, tq), tq)
        q = q_ref[rows, :]
        run = None
        for kv, off, c0, size in chunks:
            s = lax.dot_general(q, k_refs[kv][off:off + size, :], (((1,), (1,)), ((), ())),
                                preferred_element_type=F32)
            s_ref[:, c0:c0 + size] = s
            for lb in range(size // LANES):
                blk = s[:, lb * LANES:(lb + 1) * LANES]
                run = blk if run is None else jnp.maximum(run, blk)
        m_ref[...] = run

    def finish(t, slot):
        s_ref, m_ref = slots[slot]
        rows = pl.ds(pl.multiple_of(t * tq, tq), tq)
        m = jnp.max(m_ref[...], axis=-1, keepdims=True)
        acc = None
        for kv, off, c0, size in chunks:
            p = jnp.exp2(s_ref[:, c0:c0 + size] - m).astype(BF16)
            pv = jnp.dot(p, v_refs[kv][off:off + size, :], preferred_element_type=F32)
            acc = pv if acc is None else acc + pv
        o_ref[rows, :] = (acc[:, :LANES] / acc[:, LANES:LANES + 1]).astype(BF16)

    scores(0, 0)

    def body(k, carry):
        scores(2 * k + 1, 1)
        finish(2 * k, 0)
        scores(jnp.minimum(2 * k + 2, n_tiles - 1), 0)
        finish(2 * k + 1, 1)
        return carry

    lax.fori_loop(0, n_tiles // 2, body, 0)
    if n_tiles % 2:
        finish(n_tiles - 1, 0)


def _attention(qo, ko, vo, b, s, tc, tq, latent):
    n_x = b * s
    ctx0 = n_x // tc
    cspec = pl.BlockSpec((tc, MLA_SLAB), lambda i, h: (ctx0 + i, h))
    xspec = pl.BlockSpec((s, MLA_SLAB), lambda i, h: (i, h))
    if latent:
        n_q, n_keys = s, s + tc
        in_specs = [xspec, cspec, xspec, cspec, xspec]
        args = (qo, ko, ko, vo, vo)
    else:
        n_q, n_keys = tc, tc
        in_specs = [cspec, cspec, cspec]
        args = (qo, ko, vo)
    return pl.pallas_call(
        functools.partial(_attn_kernel, (len(args) - 1) // 2, tq),
        out_shape=jax.ShapeDtypeStruct((b * n_q, MLA_HEADS * LANES), BF16),
        grid=(b, MLA_HEADS),
        in_specs=in_specs,
        out_specs=pl.BlockSpec((n_q, LANES), lambda i, h: (i, h)),
        scratch_shapes=[pltpu.VMEM((tq, n_keys), F32), pltpu.VMEM((tq, n_keys), F32),
                        pltpu.VMEM((tq, LANES), F32), pltpu.VMEM((tq, LANES), F32)],
        compiler_params=_cparams(("parallel", "parallel")),
        name="attn_latent" if latent else "attn_ctx",
    )(*args)


S5_HALF = S5_BLOCK_GROUPS * S5_STATE


def _s5_kernel(n_batch, rows_x, rows_c, dot_rows, u_ref, fac_ref, pw_ref, tab_ref, y_ref,
               u2_ref, v_ref, r_ref, m_ref, ot_ref):
    d = pl.program_id(1)
    n_dot = u2_ref.shape[0] // dot_rows

    for s in range(S5_SUB):
        blk = slice(s * LANES, (s + 1) * LANES)
        r_ref[blk, :] = (fac_ref[0, 0, 0] * pw_ref[0, 0, 0, s:s + 1, :]
                         + fac_ref[0, 0, 1] * pw_ref[0, 0, 1, s:s + 1, :]).astype(BF16)
        ot_ref[blk, :] = (fac_ref[0, 0, 2] * pw_ref[0, 0, 2, s:s + 1, :]
                          + fac_ref[0, 0, 3] * pw_ref[0, 0, 3, s:s + 1, :]).astype(BF16)

    c_t = fac_ref[0, 0, 2].astype(BF16)

    def toeplitz(reverse):
        taps = {}
        for s in range(S5_SUB):
            z = s if reverse else S5_SUB - 1 - s
            taps[z] = _dot_nt(r_ref[s * LANES:(s + 1) * LANES, :], c_t).astype(BF16)
        zero = jnp.zeros((LANES, LANES), BF16)
        for s in range(S5_SUB):
            for t in range(S5_SUB):
                lag = (s - t) if reverse else (t - s)
                m_ref[s * LANES:(s + 1) * LANES, t * LANES:(t + 1) * LANES] = taps[lag] if lag >= 0 else zero

    @pl.when(d == 0)
    def _():
        toeplitz(False)

    @pl.when(d == 1)
    def _():
        toeplitz(True)

    def dot_rows_of(i):
        return pl.ds(pl.multiple_of(i * dot_rows, dot_rows), dot_rows)

    def token_rows_of(i, s):
        return pl.ds(i * (dot_rows * S5_SUB) + s, dot_rows, stride=S5_SUB)

    @pl.when(d == 0)
    def _():
        def stage(i, carry):
            rows = pl.ds(pl.multiple_of(i * dot_rows, dot_rows), dot_rows)
            y_ref[rows, :] = u_ref[rows, :].astype(F32)
            return carry

        lax.fori_loop(0, n_dot * S5_SUB, stage, 0)

        def regroup(i, carry):
            for s in range(S5_SUB):
                u2_ref[dot_rows_of(i), s * LANES:(s + 1) * LANES] = y_ref[token_rows_of(i, s), :].astype(BF16)
            return carry

        lax.fori_loop(0, n_dot, regroup, 0)

    def increments(i, carry):
        rows = dot_rows_of(i)
        v_ref[rows, :] = jnp.dot(u2_ref[rows, :], r_ref[...], preferred_element_type=F32)
        return carry

    lax.fori_loop(0, n_dot, increments, 0)

    def cmul_add(ar, ai, cr, ci, xr, xi):
        return ar + cr * xr - ci * xi, ai + cr * xi + ci * xr

    def run(reverse):
        tab = tab_ref.at[0, 0]
        last = 0 if reverse else SUBLANES - 1
        first_row = lax.broadcasted_iota(jnp.int32, (SUBLANES, S5_HALF), 0) == (SUBLANES - 1 - last)

        def segment(bases, n_groups, carry):
            def body(i, carry):
                gi = (n_groups - 1 - i) if reverse else i
                out = []
                for base, (cre, cim) in zip(bases, carry):
                    rows = pl.ds(pl.multiple_of(base + gi * SUBLANES, SUBLANES), SUBLANES)
                    re = v_ref[rows, :S5_HALF]
                    im = v_ref[rows, S5_HALF:]
                    for lvl, shift in enumerate((1, 2, 4)):
                        sh = (SUBLANES - shift) if reverse else shift
                        re, im = cmul_add(re, im, tab[2 * lvl], tab[2 * lvl + 1],
                                          pltpu.roll(re, sh, axis=0), pltpu.roll(im, sh, axis=0))
                    re, im = cmul_add(re, im, tab[6], tab[7], cre, cim)
                    sh1 = (SUBLANES - 1) if reverse else 1
                    v_ref[rows, :S5_HALF] = jnp.where(first_row, cre, pltpu.roll(re, sh1, axis=0))
                    v_ref[rows, S5_HALF:] = jnp.where(first_row, cim, pltpu.roll(im, sh1, axis=0))
                    out.append((jnp.broadcast_to(re[last:last + 1, :], re.shape),
                                jnp.broadcast_to(im[last:last + 1, :], im.shape)))
                return tuple(out)
            return lax.fori_loop(0, n_groups, body, carry)

        zero = jnp.zeros((SUBLANES, S5_HALF), F32)
        carry = tuple((zero, zero) for _ in range(n_batch))
        carry = segment([n_batch * rows_x + bi * rows_c for bi in range(n_batch)], rows_c // SUBLANES, carry)
        segment([bi * rows_x for bi in range(n_batch)], rows_x // SUBLANES, carry)

        def outputs(i, carry):
            rows = dot_rows_of(i)
            y = jnp.dot(u2_ref[rows, :], m_ref[...], preferred_element_type=F32)
            y = y + lax.dot_general(v_ref[rows, :].astype(BF16), ot_ref[...], (((1,), (1,)), ((), ())),
                                    preferred_element_type=F32)
            for s in range(S5_SUB):
                part = y[:, s * LANES:(s + 1) * LANES]
                if reverse:
                    y_ref[token_rows_of(i, s), :] += part
                else:
                    y_ref[token_rows_of(i, s), :] = part
            return carry

        lax.fori_loop(0, n_dot, outputs, 0)

    @pl.when(d == 0)
    def _():
        run(False)

    @pl.when(d == 1)
    def _():
        run(True)


def _s5_scan(z, fac, pw, tabs, layer, b, s, tc, col_u):
    r = z.shape[0]
    rc = r // S5_SUB
    n_blocks = fac.shape[2]
    width = S5_SUB * LANES
    assert width == 2 * S5_HALF
    dot_rows = max(n for n in range(16, 641, 16) if rc % n == 0)
    return pl.pallas_call(
        functools.partial(_s5_kernel, b, s // S5_SUB, tc // S5_SUB, dot_rows),
        out_shape=jax.ShapeDtypeStruct((r, n_blocks * LANES), F32),
        grid=(n_blocks, 2),
        in_specs=[
            pl.BlockSpec((r, LANES), lambda cb, d: (0, col_u // LANES + cb), pipeline_mode=pl.Buffered(1)),
            pl.BlockSpec((None, 1, 1, 4, LANES, width), lambda cb, d: (layer, d, cb, 0, 0, 0),
                         pipeline_mode=pl.Buffered(1)),
            pl.BlockSpec((None, 1, 1, 4, S5_SUB, width), lambda cb, d: (layer, d, cb, 0, 0, 0)),
            pl.BlockSpec((None, 1, 1, 8, SUBLANES, S5_HALF), lambda cb, d: (layer, d, cb, 0, 0, 0)),
        ],
        out_specs=pl.BlockSpec((r, LANES), lambda cb, d: (0, cb)),
        scratch_shapes=[pltpu.VMEM((rc, width), BF16), pltpu.VMEM((rc, width), F32),
                        pltpu.VMEM((width, width), BF16), pltpu.VMEM((width, width), BF16),
                        pltpu.VMEM((width, width), BF16)],
        compiler_params=_cparams(("parallel", "arbitrary")),
        name="s5_scan",
    )(z, fac, pw, tabs)


def _glu_kernel(y_ref, u_ref, d_ref, w_ref, b_ref, o_ref, g_ref):
    tm = y_ref.shape[0]

    def body(r, carry):
        rows = pl.ds(pl.multiple_of(r * ROW_CHUNK, ROW_CHUNK), ROW_CHUNK)
        y = y_ref[rows, :] + d_ref[...] * u_ref[rows, :].astype(F32)
        g_ref[rows, :] = jax.nn.gelu(y).astype(BF16)
        return carry

    lax.fori_loop(0, tm // ROW_CHUNK, body, 0)
    g = g_ref[...]
    gate = jax.nn.sigmoid(jnp.dot(g, w_ref[...], preferred_element_type=F32) + b_ref[...])
    o_ref[...] = (g.astype(F32) * gate).astype(BF16)


def _s5_glu(y, z, d_skip, w_glu, b_glu, tm, col_u, n_row_tiles):
    width = y.shape[1]
    return pl.pallas_call(
        _glu_kernel,
        out_shape=jax.ShapeDtypeStruct((n_row_tiles * tm, width), BF16),
        grid=(n_row_tiles,),
        in_specs=[
            pl.BlockSpec((tm, width), lambda i: (i, 0)),
            pl.BlockSpec((tm, width), lambda i: (i, col_u // width)),
            pl.BlockSpec((1, width), lambda i: (0, 0)),
            pl.BlockSpec((width, width), lambda i: (0, 0)),
            pl.BlockSpec((1, width), lambda i: (0, 0)),
        ],
        out_specs=pl.BlockSpec((tm, width), lambda i: (i, 0)),
        scratch_shapes=[pltpu.VMEM((tm, width), BF16)],
        compiler_params=_cparams(("parallel",)),
        name="s5_glu",
    )(y, z, d_skip, w_glu, b_glu)


def _merge_kernel(n_x_tiles, ax_ref, ac_ref, bx_ref, bc_ref, c_ref, ga_ref, gb_ref, gc_ref, w_ref, o_ref):
    def combine(a_ref, b_ref):
        acc = None
        for r, (br_ref, gate_ref) in enumerate(((a_ref, ga_ref), (b_ref, gb_ref), (c_ref, gc_ref))):
            proj = jnp.dot(br_ref[...], w_ref[r], preferred_element_type=F32)
            term = jax.nn.sigmoid(gate_ref[...].astype(F32)) * proj
            acc = term if acc is None else acc + term
        o_ref[...] = acc.astype(BF16)

    @pl.when(pl.program_id(0) < n_x_tiles)
    def _():
        combine(ax_ref, bx_ref)

    @pl.when(pl.program_id(0) >= n_x_tiles)
    def _():
        combine(ac_ref, bc_ref)


def _merge(a_x, a_c, b_x, b_c, cc, z, w_branch, tm, tn, col_g, n_row_tiles, n_x_tiles):
    width = a_x.shape[1]
    d = w_branch.shape[2]

    def gate_spec(r):
        return pl.BlockSpec((tm, tn), lambda i, j: (i, (col_g + r * d) // tn + j))

    ax_spec, ac_spec = _stream_specs((tm, width), n_x_tiles, 0, lambda j: 0)
    return pl.pallas_call(
        functools.partial(_merge_kernel, n_x_tiles),
        out_shape=jax.ShapeDtypeStruct((n_row_tiles * tm, d), BF16),
        grid=(n_row_tiles, d // tn),
        in_specs=[
            ax_spec, ac_spec, ax_spec, ac_spec,
            pl.BlockSpec((tm, width), lambda i, j: (i, 0)),
            gate_spec(0), gate_spec(1), gate_spec(2),
            pl.BlockSpec((N_BRANCH, width, tn), lambda i, j: (0, 0, j)),
        ],
        out_specs=pl.BlockSpec((tm, tn), lambda i, j: (i, j)),
        compiler_params=_cparams(("parallel", "arbitrary")),
        name="merge",
    )(a_x, a_c, b_x, b_c, cc, z, z, z, w_branch)


def _resid_kernel(n_x_tiles, m_ref, w_ref, x_ref, c_ref, al_ref, o_ref):
    upd = al_ref[0] * jnp.dot(m_ref[...], w_ref[...], preferred_element_type=F32)

    @pl.when(pl.program_id(0) < n_x_tiles)
    def _():
        o_ref[...] = x_ref[...] + upd

    @pl.when(pl.program_id(0) >= n_x_tiles)
    def _():
        o_ref[...] = c_ref[...] + upd


def _out_proj_residual(m, w, x_src, c_src, ctx_tile0, mod3, tm, tn, n_row_tiles, n_x_tiles, tiles_per_batch, ctx_row,
                       k_alpha):
    kdim, d = w.shape
    nt = d // tn

    def alpha_map(i, j):
        r = jnp.where(i < n_x_tiles, i // tiles_per_batch, ctx_row)
        return (r * N_MOD + k_alpha, 0, j)

    x_spec, c_spec = _stream_specs((tm, tn), n_x_tiles, ctx_tile0, lambda j: j)
    return pl.pallas_call(
        functools.partial(_resid_kernel, n_x_tiles),
        out_shape=jax.ShapeDtypeStruct((n_row_tiles * tm, d), F32),
        grid=(n_row_tiles, nt),
        in_specs=[
            pl.BlockSpec((tm, kdim), lambda i, j: (i, 0)),
            pl.BlockSpec((kdim, tn), lambda i, j: (0, j)),
            x_spec, c_spec,
            pl.BlockSpec((1, 1, tn), alpha_map),
        ],
        out_specs=pl.BlockSpec((tm, tn), lambda i, j: (i, j)),
        compiler_params=_cparams(("parallel", "arbitrary")),
        name="out_proj",
    )(m, w, x_src, c_src, mod3)


def _ff1_kernel(x_ref, g_ref, sh_ref, sc_ref, w_ref, h_ref, xn_ref):
    @pl.when(pl.program_id(1) == 0)
    def _():
        _norm_mod_rows(x_ref, g_ref, sh_ref, sc_ref, xn_ref)

    a = jnp.maximum(jnp.dot(xn_ref[...], w_ref[...], preferred_element_type=F32), 0.0)
    h_ref[...] = (a * a).astype(BF16)


def _ff1(xs, g, mod3, w, tm, tn, n_row_tiles, n_x_tiles, tiles_per_batch, ctx_row):
    d, dff = w.shape
    return pl.pallas_call(
        _ff1_kernel,
        out_shape=jax.ShapeDtypeStruct((n_row_tiles * tm, dff), BF16),
        grid=(n_row_tiles, dff // tn),
        in_specs=[
            pl.BlockSpec((tm, d), lambda i, j: (i, 0)),
            pl.BlockSpec((1, d), lambda i, j: (0, 0)),
            pl.BlockSpec((1, 1, d), _mod_row_map(n_x_tiles, tiles_per_batch, ctx_row, 3)),
            pl.BlockSpec((1, 1, d), _mod_row_map(n_x_tiles, tiles_per_batch, ctx_row, 4)),
            pl.BlockSpec((d, tn), lambda i, j: (0, j)),
        ],
        out_specs=pl.BlockSpec((tm, tn), lambda i, j: (i, j)),
        scratch_shapes=[pltpu.VMEM((tm, d), BF16)],
        compiler_params=_cparams(("parallel", "arbitrary")),
        name="ff1",
    )(xs, g, mod3, mod3, w)


def _ff2_kernel(h_ref, w_ref, x_ref, al_ref, o_ref):
    k = pl.program_id(2)
    part = jnp.dot(h_ref[...], w_ref[...], preferred_element_type=F32)

    @pl.when(k == 0)
    def _():
        o_ref[...] = part

    @pl.when(k > 0)
    def _():
        o_ref[...] += part

    @pl.when(k == pl.num_programs(2) - 1)
    def _():
        o_ref[...] = x_ref[...] + al_ref[0] * o_ref[...]


def _ff2(h, w, xs, mod3, tm, tn, tk, n_row_tiles, n_x_tiles, tiles_per_batch, ctx_row):
    dff, d = w.shape

    def alpha_map(i, j, k):
        r = jnp.where(i < n_x_tiles, i // tiles_per_batch, ctx_row)
        return (r * N_MOD + 5, 0, j)

    return pl.pallas_call(
        _ff2_kernel,
        out_shape=jax.ShapeDtypeStruct((n_row_tiles * tm, d), F32),
        grid=(n_row_tiles, d // tn, dff // tk),
        in_specs=[
            pl.BlockSpec((tm, tk), lambda i, j, k: (i, k)),
            pl.BlockSpec((tk, tn), lambda i, j, k: (k, j)),
            pl.BlockSpec((tm, tn), lambda i, j, k: (i, j)),
            pl.BlockSpec((1, 1, tn), alpha_map),
        ],
        out_specs=pl.BlockSpec((tm, tn), lambda i, j, k: (i, j)),
        compiler_params=_cparams(("parallel", "parallel", "arbitrary")),
        name="ff2",
    )(h, w, xs, mod3)


def _rope_partner():
    j = np.arange(MLA_ROPE)
    quarter = MLA_ROPE // 4
    return np.where((j // quarter) % 2 == 0, j + quarter, j - quarter)


def _rope_table(s, tm):
    pos = jnp.arange(s)
    row = (pos // GRID_W).astype(F32)
    col = (pos % GRID_W).astype(F32)
    n_freq = MLA_ROPE // 4
    inv_freq = ROPE_THETA ** (-jnp.arange(n_freq, dtype=F32) / n_freq)
    ang_r = row[:, None] * inv_freq
    ang_c = col[:, None] * inv_freq
    cos = jnp.concatenate([jnp.cos(ang_r)] * 2 + [jnp.cos(ang_c)] * 2, axis=-1)
    sin = jnp.concatenate([-jnp.sin(ang_r), jnp.sin(ang_r), -jnp.sin(ang_c), jnp.sin(ang_c)], axis=-1)
    ident = jnp.concatenate([jnp.ones((tm, MLA_ROPE), F32), jnp.zeros((tm, MLA_ROPE), F32)], axis=-1)
    return jnp.concatenate([jnp.concatenate([cos, sin], axis=-1), ident], axis=0)


def _cast_kernel(w_ref, o_ref):
    o_ref[...] = w_ref[...].astype(BF16)


def _layer_bf16(w_all, layer):
    rows, cols = w_all.shape[1:]
    tr = CAST_BLOCK_BYTES // (4 * cols)
    while rows % tr:
        tr //= 2
    return pl.pallas_call(
        _cast_kernel,
        out_shape=jax.ShapeDtypeStruct((rows, cols), BF16),
        grid=(rows // tr,),
        in_specs=[pl.BlockSpec((None, tr, cols), lambda i: (layer, i, 0))],
        out_specs=pl.BlockSpec((tr, cols), lambda i: (i, 0)),
        compiler_params=_cparams(("parallel",)),
        name="cast_bf16",
    )(w_all)


def _pack_w_in(w_in, b_in, gate_b, sizes, n_pad):
    w_t = w_in.T
    d_model = w_in.shape[0]
    bounds = np.cumsum((0,) + sizes)
    seg = [slice(int(bounds[i]), int(bounds[i + 1])) for i in range(len(sizes))]
    partner = _rope_partner()
    order = (0, 1, 2, 3, 5, 6, 8, 9)
    names = ("q", "k", "v", "o", "qa", "kva", "u", "gates")
    offs, pos = {}, 0
    for name, i in zip(names, order):
        offs[name] = pos
        pos += sizes[i]
    offs["kpe"], offs["kpe_sw"] = pos, pos + MLA_ROPE
    n_zero = n_pad - pos - 2 * MLA_ROPE
    kpe_w, kpe_b = w_t[seg[7]], b_in[seg[7]]
    w = jnp.concatenate([w_t[seg[i]] for i in order] + [kpe_w, kpe_w[partner], jnp.zeros((n_zero, d_model), F32)],
                        axis=0).astype(BF16)
    b = jnp.concatenate([b_in[seg[i]] for i in order] + [kpe_b, kpe_b[partner], jnp.zeros((n_zero,), F32)])
    n_g = sizes[4]
    wg = jnp.concatenate([w_t[seg[4]], jnp.zeros((LANES - n_g, d_model), F32)], axis=0).astype(BF16)
    bg = jnp.concatenate([b_in[seg[4]] + gate_b.reshape(-1), jnp.zeros((LANES - n_g,), F32)])[None, :]
    return w, b[None, :], wg, bg, offs


def _pack_mla(w_uq, w_ukv, qn_g, kn_g):
    partner = _rope_partner()
    lora = w_uq.shape[0]
    wq = w_uq.reshape(lora, MLA_HEADS, MLA_DQK)
    wq = jnp.concatenate([wq, wq[:, :, MLA_NOPE + partner]], axis=-1).reshape(lora, MLA_HEADS * MLA_SLAB)
    wkv = w_ukv.reshape(w_ukv.shape[0], MLA_HEADS, -1)
    wkv = jnp.concatenate([wkv[:, :, :MLA_NOPE].reshape(lora, -1), wkv[:, :, MLA_NOPE:].reshape(lora, -1)], axis=-1)

    def gains(g):
        return jnp.concatenate([g, g[MLA_NOPE + partner]])[None, :]

    return wq.astype(BF16), wkv.astype(BF16), gains(qn_g), gains(kn_g)


def _pack_s5(a_re, a_im, log_dt, b_re, b_im, c_re, c_im):
    n_dir, n_groups, n_state = a_re.shape
    gc = b_re.shape[-1]
    nb = n_groups // S5_BLOCK_GROUPS
    lam_re = jnp.minimum(a_re.astype(F32), -1e-4)
    lam_im = a_im.astype(F32)
    dt = jnp.exp(log_dt.astype(F32))[..., None]

    def pole_power(k):
        mag = jnp.exp(k * lam_re * dt)
        return mag * jnp.cos(k * lam_im * dt), mag * jnp.sin(k * lam_im * dt)

    bar_re, bar_im = pole_power(1.0)
    den = lam_re * lam_re + lam_im * lam_im
    f_re = ((bar_re - 1.0) * lam_re + bar_im * lam_im) / den
    f_im = (bar_im * lam_re - (bar_re - 1.0) * lam_im) / den
    bb_re = f_re[..., None] * b_re.astype(F32) - f_im[..., None] * b_im.astype(F32)
    bb_im = f_re[..., None] * b_im.astype(F32) + f_im[..., None] * b_re.astype(F32)
    eye = jnp.eye(S5_BLOCK_GROUPS, dtype=F32)
    sub = S5_SUB

    def per_block(a):
        return a.reshape(a.shape[:-2] + (nb, S5_BLOCK_GROUPS * n_state))

    def block_b(part):
        p = part.reshape(n_dir, nb, S5_BLOCK_GROUPS, n_state, gc)
        m = jnp.einsum('dbgnc,gh->dbgchn', p, eye, precision=HIGHEST)
        return m.reshape(n_dir, nb, LANES, S5_HALF)

    def block_c(part):
        p = part.astype(F32).reshape(n_dir, nb, S5_BLOCK_GROUPS, gc, n_state)
        m = jnp.einsum('dbgcn,gh->dbgnhc', p, eye, precision=HIGHEST)
        return m.reshape(n_dir, nb, S5_HALF, LANES)

    bm_re, bm_im = block_b(bb_re), block_b(bb_im)
    cm_re, cm_im = block_c(c_re), block_c(c_im)

    tau = jnp.arange(sub + 1, dtype=F32)[:, None, None, None]
    p_re, p_im = (per_block(p) for p in pole_power(tau))
    ct_re, ct_im = jnp.swapaxes(cm_re, -1, -2), jnp.swapaxes(cm_im, -1, -2)
    cat = functools.partial(jnp.concatenate, axis=-1)
    fac = jnp.stack([cat([bm_re, bm_im]), cat([-bm_im, bm_re]), cat([ct_re, -ct_im]), cat([-ct_im, -ct_re])],
                    axis=2)
    t = np.arange(sub)
    pws = []
    for d in range(n_dir):
        to_exit = (sub - 1 - t) if d == 0 else t
        age = (t + 1) if d == 0 else (sub - t)
        rows = [p_re[to_exit, d], p_im[to_exit, d], p_re[age, d], p_im[age, d]]
        pws.append(jnp.stack([cat([a, a]).transpose(1, 0, 2) for a in rows], axis=1))
    pw = jnp.stack(pws)

    rows = jnp.arange(SUBLANES)
    tabs = []
    for d in range(n_dir):
        per_dir = []
        for shift in (1, 2, 4):
            keep = ((rows <= SUBLANES - 1 - shift) if d == 1 else (rows >= shift))[None, :, None]
            s_re, s_im = pole_power(float(shift * sub))
            per_dir += [jnp.where(keep, per_block(s_re[d])[:, None, :], 0.0),
                        jnp.where(keep, per_block(s_im[d])[:, None, :], 0.0)]
        expo = (((SUBLANES - rows) if d == 1 else (rows + 1)) * sub).astype(F32)
        s_re, s_im = pole_power(expo[:, None, None, None])
        per_dir += [jnp.moveaxis(per_block(s_re[:, d]), 0, 1), jnp.moveaxis(per_block(s_im[:, d]), 0, 1)]
        tabs.append(jnp.stack(per_dir, axis=1))
    return fac, pw, jnp.stack(tabs).astype(F32)


def _tri_matrices():
    t = np.arange(ML_CHUNK)
    lower = t[None, :] <= t[:, None]
    mask = np.where(np.stack([lower, lower.T]), 0.0, NEG_BIG).astype(np.float32)
    return jnp.asarray(lower.astype(np.float32), dtype=BF16), jnp.asarray(mask)


def kernel(x, c, ctx, c_ctx, w_mod, b_mod, norm_g, w_in, b_in, ml_gate_b, ml_norm_g, mla_qa_g, mla_kva_g, mla_w_uq, mla_w_ukv, mla_qn_g, mla_kn_g, s5_a_re, s5_a_im, s5_log_dt, s5_b_re, s5_b_im, s5_c_re, s5_c_im, s5_d, s5_w_glu, s5_b_glu, w_branch, w_out, w_ff1, w_ff2):
    b, s, d = x.shape
    tc = ctx.shape[1]
    depth = w_mod.shape[0]
    dv = ml_norm_g.shape[2]
    dk = dv // 2
    lora = mla_qa_g.shape[1]
    s5_width = s5_d.shape[1]
    branch_w = w_branch.shape[2]
    sizes = (ML_HEADS * dk, ML_HEADS * dk, ML_HEADS * dv, ML_HEADS * dv, 4 * ML_HEADS, lora, lora, MLA_ROPE,
             s5_width, N_BRANCH * d)
    assert sum(sizes) == w_in.shape[2] and b + 1 <= SUBLANES
    assert s % ML_CHUNK == 0 and tc % ML_CHUNK == 0 and branch_w == ML_HEADS * dv == MLA_HEADS * LANES == s5_width

    n_x = b * s
    n_c = b * tc
    tm = _row_tile(s, n_c)
    n_x_tiles = n_x // tm
    n_tiles = n_x_tiles + n_c // tm
    tiles_per_batch = s // tm
    tile_args = (n_x_tiles, tiles_per_batch, b)

    x_src, c_src, ctx_tile0 = x.reshape(n_x, d), ctx.reshape(n_c, d), 0
    cc = jnp.concatenate([c, c_ctx[None, :], jnp.zeros((SUBLANES - b - 1, d), F32)], axis=0)
    mod = _modulation(cc, w_mod, b_mod)
    tm_q = min(tm, 512)
    tab = _rope_table(s, tm_q)
    tri, ml_mask = _tri_matrices()
    s5_packed = jax.vmap(_pack_s5)(s5_a_re, s5_a_im, s5_log_dt, s5_b_re, s5_b_im, s5_c_re, s5_c_im)
    n_used = sum(sizes) - sizes[4] + MLA_ROPE
    tn_in = 1280
    n_pad = -(-n_used // tn_in) * tn_in

    for l in range(depth):
        with_ctx_out = l < depth - 1
        mod3 = mod[l].reshape(SUBLANES * N_MOD, 1, d)
        w_p, b_p, wg, bg, offs = _pack_w_in(w_in[l], b_in[l], ml_gate_b[l], sizes, n_pad)
        z, gz = _in_proj(x_src, c_src, ctx_tile0, n_tiles, norm_g[l, 0][None, :], mod3, w_p, b_p, wg, bg, tm, tn_in,
                         *tile_args)

        gates, gates_t = _gate_prep(gz, tri)
        a_x, a_c = _mlstm(z, gates, gates_t, ml_mask, ml_norm_g[l], b, s, tc, dk, dv,
                          (offs["q"], offs["k"], offs["v"], offs["o"]), with_ctx_out)

        wq, wkv, gq, gk = _pack_mla(mla_w_uq[l], mla_w_ukv[l], mla_qn_g[l], mla_kn_g[l])
        qo, ko, vo = _mla_proj(z, tab, mla_qa_g[l][None, :], mla_kva_g[l][None, :], wq, wkv, gq, gk, tm_q,
                               (offs["qa"], offs["kva"], offs["kpe"]), n_x // tm_q, s // tm_q)
        tq = min(512, s)
        b_x = _attention(qo, ko, vo, b, s, tc, tq, True)

        y = _s5_scan(z, *s5_packed, l, b, s, tc, offs["u"])
        n_out_tiles = n_tiles if with_ctx_out else n_x_tiles
        c_all = _s5_glu(y, z, s5_d[l][None, :], s5_w_glu[l].astype(BF16), s5_b_glu[l][None, :], tm, offs["u"],
                        n_out_tiles)

        if with_ctx_out:
            b_c = _attention(qo, ko, vo, b, s, tc, min(tq, tc), False)
        else:
            a_c, b_c = a_x, b_x
        w_br = _layer_bf16(w_branch.reshape(depth, N_BRANCH * branch_w, d), l).reshape(N_BRANCH, branch_w, d)
        merged = _merge(a_x, a_c, b_x, b_c, c_all, z, w_br, tm, 512, offs["gates"],
                        n_out_tiles, n_x_tiles)
        xs1 = _out_proj_residual(merged, _layer_bf16(w_out, l), x_src, c_src, ctx_tile0, mod3, tm, 1024,
                                 n_out_tiles, *tile_args, 2)
        hid = _ff1(xs1, norm_g[l, 1][None, :], mod3, _layer_bf16(w_ff1, l), tm, 1024, n_out_tiles, *tile_args)
        xs = _ff2(hid, _layer_bf16(w_ff2, l), xs1, mod3, tm, 1024, 2048, n_out_tiles, *tile_args)
        x_src, c_src, ctx_tile0 = xs, xs, n_x_tiles

    return xs.reshape(b, s, d)
```

```python
import functools
import math

import jax
import jax.numpy as jnp
import numpy as np
from jax import lax
from jax.experimental import pallas as pl
from jax.experimental.pallas import tpu as pltpu

F32 = jnp.float32
BF16 = jnp.bfloat16
HIGHEST = lax.Precision.HIGHEST

N_MOD = 6
N_BRANCH = 3
ML_HEADS = 4
MLA_HEADS = 8
MLA_NOPE = 128
MLA_ROPE = 64
MLA_DQK = MLA_NOPE + MLA_ROPE
MLA_SLAB = 256
GRID_W = 64
ROPE_THETA = 10000.0
S5_GROUP = 16
S5_STATE = 64
S5_BLOCK_GROUPS = 8
EPS = 1e-6
NEG_BIG = -1e30

LANES = 128
SUBLANES = 8
VMEM_LIMIT = 56 * 1024 * 1024

ML_CHUNK = 256
ML_GATE_ROWS = 32
S5_SUB = 8
ATTN_KEY_CHUNK = 512
Q_PRESCALE = MLA_DQK ** -0.5 * math.log2(math.e)
ROW_CHUNK = 128
CAST_BLOCK_BYTES = 8 * 1024 * 1024


def _cparams(sem):
    return pltpu.CompilerParams(dimension_semantics=sem, vmem_limit_bytes=VMEM_LIMIT)


def _row_tile(n_x_rows_per_batch, n_ctx_rows):
    tm = 1024
    while n_x_rows_per_batch % tm or n_ctx_rows % tm:
        tm //= 2
    return tm


def _mod_kernel(c_ref, w_ref, b_ref, o_ref):
    s = c_ref[...]
    s = s * jax.nn.sigmoid(s)
    o_ref[0] = jnp.dot(s.astype(BF16), w_ref[0].astype(BF16), preferred_element_type=F32) + b_ref[0]


def _modulation(cc, w_mod, b_mod):
    n_layers, d, n = w_mod.shape
    tn = 1024
    return pl.pallas_call(
        _mod_kernel,
        out_shape=jax.ShapeDtypeStruct((n_layers, SUBLANES, n), F32),
        grid=(n_layers, n // tn),
        in_specs=[
            pl.BlockSpec((SUBLANES, d), lambda l, j: (0, 0)),
            pl.BlockSpec((1, d, tn), lambda l, j: (l, 0, j)),
            pl.BlockSpec((1, 1, tn), lambda l, j: (l, 0, j)),
        ],
        out_specs=pl.BlockSpec((1, SUBLANES, tn), lambda l, j: (l, 0, j)),
        compiler_params=_cparams(("parallel", "parallel")),
        name="adaln_mod",
    )(cc, w_mod, b_mod.reshape(n_layers, 1, n))


def _norm_mod_rows(x_ref, g_ref, sh_ref, sc_ref, xn_ref):
    tm = x_ref.shape[0]
    g = g_ref[...]
    sc = 1.0 + sc_ref[0]
    sh = sh_ref[0]

    def body(r, carry):
        rows = pl.ds(pl.multiple_of(r * ROW_CHUNK, ROW_CHUNK), ROW_CHUNK)
        x = x_ref[rows, :]
        ms = jnp.mean(x * x, axis=-1, keepdims=True)
        y = x * lax.rsqrt(ms + EPS) * g
        xn_ref[rows, :] = (y * sc + sh).astype(BF16)
        return carry

    lax.fori_loop(0, tm // ROW_CHUNK, body, 0)


def _mod_row_map(n_x_tiles, tiles_per_batch, ctx_row, k):
    def index_map(i, j):
        r = jnp.where(i < n_x_tiles, i // tiles_per_batch, ctx_row)
        return (r * N_MOD + k, 0, 0)
    return index_map


def _stream_specs(block, n_x_tiles, ctx_tile0, col_map):
    x_spec = pl.BlockSpec(block, lambda i, *r: (jnp.minimum(i, n_x_tiles - 1), col_map(*r)))
    c_spec = pl.BlockSpec(block, lambda i, *r: (ctx_tile0 + jnp.maximum(i - n_x_tiles, 0), col_map(*r)),
                          pipeline_mode=pl.Buffered(1))
    return x_spec, c_spec


def _dot_nt(a, b_t):
    return lax.dot_general(a, b_t, (((1,), (1,)), ((), ())), preferred_element_type=F32)


def _in_kernel(n_x_tiles, x_ref, c_ref, g_ref, sh_ref, sc_ref, w_ref, b_ref, wg_ref, bg_ref, z_ref, gz_ref, xn_ref):
    @pl.when(pl.program_id(1) == 0)
    def _():
        @pl.when(pl.program_id(0) < n_x_tiles)
        def _():
            _norm_mod_rows(x_ref, g_ref, sh_ref, sc_ref, xn_ref)

        @pl.when(pl.program_id(0) >= n_x_tiles)
        def _():
            _norm_mod_rows(c_ref, g_ref, sh_ref, sc_ref, xn_ref)

        gz_ref[...] = _dot_nt(xn_ref[...], wg_ref[...]) + bg_ref[...]

    z_ref[...] = (_dot_nt(xn_ref[...], w_ref[...]) + b_ref[...]).astype(BF16)


def _in_proj(x_src, c_src, ctx_tile0, n_tiles, g, mod3, w, b, wg, bg, tm, tn, n_x_tiles, tiles_per_batch, ctx_row):
    d = x_src.shape[1]
    r = n_tiles * tm
    nz = w.shape[0]
    x_spec, c_spec = _stream_specs((tm, d), n_x_tiles, ctx_tile0, lambda j: 0)
    return pl.pallas_call(
        functools.partial(_in_kernel, n_x_tiles),
        out_shape=(jax.ShapeDtypeStruct((r, nz), BF16), jax.ShapeDtypeStruct((r, LANES), F32)),
        grid=(n_tiles, nz // tn),
        in_specs=[
            x_spec, c_spec,
            pl.BlockSpec((1, d), lambda i, j: (0, 0)),
            pl.BlockSpec((1, 1, d), _mod_row_map(n_x_tiles, tiles_per_batch, ctx_row, 0)),
            pl.BlockSpec((1, 1, d), _mod_row_map(n_x_tiles, tiles_per_batch, ctx_row, 1)),
            pl.BlockSpec((tn, d), lambda i, j: (j, 0)),
            pl.BlockSpec((1, tn), lambda i, j: (0, j)),
            pl.BlockSpec((LANES, d), lambda i, j: (0, 0)),
            pl.BlockSpec((1, LANES), lambda i, j: (0, 0)),
        ],
        out_specs=(
            pl.BlockSpec((tm, tn), lambda i, j: (i, j)),
            pl.BlockSpec((tm, LANES), lambda i, j: (i, 0)),
        ),
        scratch_shapes=[pltpu.VMEM((tm, d), BF16)],
        compiler_params=_cparams(("parallel", "arbitrary")),
        name="in_proj",
    )(x_src, c_src, g, mod3, mod3, w, b, wg, bg)


def _log_sigmoid(x):
    return jnp.minimum(x, 0.0) - jnp.log1p(jnp.exp(-jnp.abs(x)))


def _split3(a):
    hi = a.astype(BF16)
    r1 = a - hi.astype(F32)
    mid = r1.astype(BF16)
    lo = (r1 - mid.astype(F32)).astype(BF16)
    return hi, mid, lo


def _gate_prep_kernel(g_ref, tri_ref, a_ref, at_ref):
    g = g_ref[...]
    kind = lax.broadcasted_iota(jnp.int32, g.shape, 1) // ML_HEADS
    lg = jnp.where((kind == 1) | (kind == 3), _log_sigmoid(g), g)
    pre = sum(jnp.dot(tri_ref[...], p, preferred_element_type=F32) for p in _split3(lg))
    suf = pre[ML_CHUNK - 1:ML_CHUNK, :] - pre + lg
    a = jnp.where(kind == 4, pltpu.roll(pre, 3 * ML_HEADS, axis=1),
                  jnp.where(kind == 5, pltpu.roll(suf, 2 * ML_HEADS, axis=1), lg))
    a_ref[...] = a
    at_ref[0] = a.T[:ML_GATE_ROWS, :]


def _gate_prep(gz, tri):
    r = gz.shape[0]
    return pl.pallas_call(
        _gate_prep_kernel,
        out_shape=(jax.ShapeDtypeStruct((r, LANES), F32),
                   jax.ShapeDtypeStruct((r // ML_CHUNK, ML_GATE_ROWS, ML_CHUNK), F32)),
        grid=(r // ML_CHUNK,),
        in_specs=[pl.BlockSpec((ML_CHUNK, LANES), lambda i: (i, 0)),
                  pl.BlockSpec((ML_CHUNK, ML_CHUNK), lambda i: (0, 0))],
        out_specs=(pl.BlockSpec((ML_CHUNK, LANES), lambda i: (i, 0)),
                   pl.BlockSpec((1, ML_GATE_ROWS, ML_CHUNK), lambda i: (i, 0, 0))),
        compiler_params=_cparams(("parallel",)),
        name="mlstm_gates",
    )(gz, tri)


def _ml_chunk(q, k_t, v, cum_cb, li_r, lf_r, cum_r, mask_add, carry, inv_scale):
    c_mat, n_mat, m = carry
    length, dv = v.shape
    crow = cum_r - li_r
    total = jnp.sum(lf_r, axis=-1, keepdims=True)
    ones = jnp.ones((length, LANES), BF16)
    n_blk = length // LANES

    def wide(a):
        return jnp.concatenate([a] * (dv // LANES), axis=1)

    log_w = [cum_cb - crow[:, j * LANES:(j + 1) * LANES] + mask_add[:, j * LANES:(j + 1) * LANES]
             for j in range(n_blk)]
    row_max = jnp.max(functools.reduce(jnp.maximum, log_w), axis=-1, keepdims=True)
    log_inter = cum_cb + m
    m_t = jnp.maximum(log_inter, row_max)
    w_inter = jnp.exp(log_inter - m_t)
    qk = jnp.dot(q, k_t, preferred_element_type=F32)
    s = jnp.concatenate([qk[:, j * LANES:(j + 1) * LANES] * jnp.exp(log_w[j] - m_t) for j in range(n_blk)],
                        axis=1).astype(BF16)
    num = wide(w_inter) * jnp.dot(q, c_mat.astype(BF16), preferred_element_type=F32)
    num = num + jnp.dot(s, v, preferred_element_type=F32)
    den = w_inter * jnp.dot(q, n_mat.astype(BF16), preferred_element_type=F32)
    den = den + jnp.dot(s, ones, preferred_element_type=F32)
    h = num * wide(1.0 / jnp.maximum(jnp.abs(den), jnp.exp(-m_t) * inv_scale))

    log_end = total - crow
    m_new = jnp.maximum(total + m, jnp.max(log_end, axis=-1, keepdims=True))
    decay = jnp.exp(total + m - m_new)
    kw_t = (k_t.astype(F32) * jnp.exp(log_end - m_new)).astype(BF16)
    c_new = decay * c_mat + jnp.dot(kw_t, v, preferred_element_type=F32)
    n_new = decay * n_mat + jnp.dot(kw_t, ones, preferred_element_type=F32)
    return h, (c_new, n_new, m_new)


def _mlstm_kernel(with_ctx_out, qx, kx, vx, ox, qc, kc, vc, oc, gx, gc, gtx, gtc, mask_ref, ng_ref, *rest):
    if with_ctx_out:
        ax_ref, ac_ref, cumx, ktx, cumc, ktc, hx, hc = rest
    else:
        ax_ref, cumx, ktx, cumc, ktc, hx, hc = rest
        ac_ref = None
    head = pl.program_id(1)
    dk = qx.shape[1]
    inv_scale = float(dk) ** 0.5
    n_x_chunks = qx.shape[0] // ML_CHUNK
    n_c_chunks = qc.shape[0] // ML_CHUNK

    def prep(g_ref, k_ref, cum_ref, kt_ref):
        def body(i, carry):
            rows = pl.ds(pl.multiple_of(i * ML_CHUNK, ML_CHUNK), ML_CHUNK)
            a = g_ref[rows, :]
            col = lax.broadcasted_iota(jnp.int32, a.shape, 1)
            for d in range(2):
                pick = jnp.sum(jnp.where(col == (4 + d) * ML_HEADS + head, a, 0.0), axis=-1, keepdims=True)
                cum_ref[d, rows, :] = jnp.broadcast_to(pick, a.shape)
            kt_ref[i] = k_ref[rows, :].astype(F32).T.astype(BF16)
            return carry
        n_chunks = g_ref.shape[0] // ML_CHUNK
        lax.fori_loop(0, n_chunks, body, 0, unroll=2 if n_chunks % 2 == 0 else 1)

    prep(gx, kx, cumx, ktx)
    prep(gc, kc, cumc, ktc)

    for d in range(2):
        reverse = d == 1
        mask_add = mask_ref[d]

        def step(q_ref, v_ref, cum_ref, gt_ref, kt_ref, h_ref, ci, carry):
            rows = pl.ds(pl.multiple_of(ci * ML_CHUNK, ML_CHUNK), ML_CHUNK)
            li_r, lf_r, cum_r = (gt_ref[ci, pl.ds(kind * ML_HEADS + head, 1), :] for kind in (2 * d, 2 * d + 1, 4 + d))
            h, carry = _ml_chunk(q_ref[rows, :], kt_ref[ci], v_ref[rows, :], cum_ref[d, rows, :], li_r, lf_r, cum_r,
                                 mask_add, carry, inv_scale)
            if reverse:
                h_ref[rows, :] += h
            else:
                h_ref[rows, :] = h
            return carry

        carry = (jnp.zeros((dk, vx.shape[1]), F32), jnp.zeros((dk, LANES), F32), jnp.zeros((1, 1), F32))

        def ctx_body(i, carry):
            ci = (n_c_chunks - 1 - i) if reverse else i
            return step(qc, vc, cumc, gtc, ktc, hc, ci, carry)

        def x_body(i, carry):
            ci = (n_x_chunks - 1 - i) if reverse else i
            return step(qx, vx, cumx, gtx, ktx, hx, ci, carry)

        carry = lax.fori_loop(0, n_c_chunks, ctx_body, carry)
        lax.fori_loop(0, n_x_chunks, x_body, carry, unroll=2 if n_x_chunks % 2 == 0 else 1)

    ng = ng_ref[0]

    def finish(h_ref, o_ref, a_ref):
        def body(i, carry):
            rows = pl.ds(pl.multiple_of(i * ML_CHUNK, ML_CHUNK), ML_CHUNK)
            h = h_ref[rows, :]
            hn = h * lax.rsqrt(jnp.mean(h * h, axis=-1, keepdims=True) + EPS) * ng
            a_ref[rows, :] = (hn * jax.nn.sigmoid(o_ref[rows, :].astype(F32))).astype(BF16)
            return carry
        lax.fori_loop(0, h_ref.shape[0] // ML_CHUNK, body, 0)

    finish(hx, ox, ax_ref)
    if with_ctx_out:
        finish(hc, oc, ac_ref)


def _mlstm(z, gates, gates_t, mask, ml_norm_g, b, s, tc, dk, dv, cols, with_ctx_out):
    n_x = b * s
    cq, ck, cv, co = cols
    ctx0 = n_x // tc
    xc, cc = s // ML_CHUNK, tc // ML_CHUNK

    def xspec(width, col0):
        return pl.BlockSpec((s, width), lambda i, h: (i, col0 // width + h))

    def cspec(width, col0):
        return pl.BlockSpec((tc, width), lambda i, h: (ctx0 + i, col0 // width + h))

    out_shape = [jax.ShapeDtypeStruct((n_x, ML_HEADS * dv), BF16)]
    out_specs = [pl.BlockSpec((s, dv), lambda i, h: (i, h))]
    if with_ctx_out:
        out_shape.append(jax.ShapeDtypeStruct((b * tc, ML_HEADS * dv), BF16))
        out_specs.append(pl.BlockSpec((tc, dv), lambda i, h: (i, h)))
    res = pl.pallas_call(
        functools.partial(_mlstm_kernel, with_ctx_out),
        out_shape=tuple(out_shape),
        grid=(b, ML_HEADS),
        in_specs=[
            xspec(dk, cq), xspec(dk, ck), xspec(dv, cv), xspec(dv, co),
            cspec(dk, cq), cspec(dk, ck), cspec(dv, cv), cspec(dv, co),
            pl.BlockSpec((s, LANES), lambda i, h: (i, 0)),
            pl.BlockSpec((tc, LANES), lambda i, h: (ctx0 + i, 0)),
            pl.BlockSpec((xc, ML_GATE_ROWS, ML_CHUNK), lambda i, h: (i, 0, 0)),
            pl.BlockSpec((cc, ML_GATE_ROWS, ML_CHUNK), lambda i, h: (b * xc // cc + i, 0, 0)),
            pl.BlockSpec((2, ML_CHUNK, ML_CHUNK), lambda i, h: (0, 0, 0)),
            pl.BlockSpec((1, 1, dv), lambda i, h: (h, 0, 0)),
        ],
        out_specs=tuple(out_specs),
        scratch_shapes=[
            pltpu.VMEM((2, s, LANES), F32), pltpu.VMEM((xc, dk, ML_CHUNK), BF16),
            pltpu.VMEM((2, tc, LANES), F32), pltpu.VMEM((cc, dk, ML_CHUNK), BF16),
            pltpu.VMEM((s, dv), F32), pltpu.VMEM((tc, dv), F32),
        ],
        compiler_params=_cparams(("parallel", "parallel")),
        name="mlstm",
    )(z, z, z, z, z, z, z, z, gates, gates, gates_t, gates_t, mask, ml_norm_g.reshape(ML_HEADS, 1, dv))
    return res if with_ctx_out else (res[0], None)


def _mla_proj_kernel(qa_ref, kva_ref, kpe_ref, tab_ref, qag_ref, kvag_ref, wq_ref, wkv_ref,
                     gq_ref, gk_ref, q_ref, k_ref, v_ref):
    def normed(a_ref, g_ref):
        a = a_ref[...].astype(F32)
        return (a * lax.rsqrt(jnp.mean(a * a, axis=-1, keepdims=True) + EPS) * g_ref[...]).astype(BF16)

    q_all = jnp.dot(normed(qa_ref, qag_ref), wq_ref[...], preferred_element_type=F32)
    kv_all = jnp.dot(normed(kva_ref, kvag_ref), wkv_ref[...], preferred_element_type=F32)
    tab = tab_ref[...]
    lane = lax.broadcasted_iota(jnp.int32, tab.shape, 1)
    first_half = lane < MLA_ROPE
    gq = gq_ref[...]
    gk = gk_ref[...]
    inv_dqk = 1.0 / MLA_DQK

    kpe = kpe_ref[...].astype(F32)
    ss_kpe = jnp.sum(jnp.where(first_half, kpe * kpe, 0.0), axis=-1, keepdims=True)
    kpe_t = kpe * (tab * gk[:, LANES:])
    kpe_rot = jnp.where(first_half, kpe_t + pltpu.roll(kpe_t, MLA_ROPE, axis=1), 0.0)

    for h in range(MLA_HEADS):
        qn = q_all[:, h * MLA_SLAB:h * MLA_SLAB + LANES]
        qp = q_all[:, h * MLA_SLAB + LANES:(h + 1) * MLA_SLAB]
        ss = jnp.sum(qn * qn, axis=-1, keepdims=True) + jnp.sum(jnp.where(first_half, qp * qp, 0.0), axis=-1,
                                                                keepdims=True)
        r = lax.rsqrt(ss * inv_dqk + EPS) * Q_PRESCALE
        qp_t = qp * (tab * gq[:, LANES:])
        qp_rot = qp_t + pltpu.roll(qp_t, MLA_ROPE, axis=1)
        q_ref[:, h * MLA_SLAB:h * MLA_SLAB + LANES] = (qn * r * gq[:, :LANES]).astype(BF16)
        q_ref[:, h * MLA_SLAB + LANES:(h + 1) * MLA_SLAB] = (qp_rot * r).astype(BF16)

        kn = kv_all[:, h * LANES:(h + 1) * LANES]
        rk = lax.rsqrt((jnp.sum(kn * kn, axis=-1, keepdims=True) + ss_kpe) * inv_dqk + EPS)
        k_ref[:, h * MLA_SLAB:h * MLA_SLAB + LANES] = (kn * rk * gk[:, :LANES]).astype(BF16)
        k_ref[:, h * MLA_SLAB + LANES:(h + 1) * MLA_SLAB] = (kpe_rot * rk).astype(BF16)

    ones_col = jnp.where(lane == 0, 1.0, 0.0).astype(BF16)
    for h in range(MLA_HEADS):
        v_ref[:, h * MLA_SLAB:h * MLA_SLAB + LANES] = kv_all[:, (MLA_HEADS + h) * LANES:(MLA_HEADS + h + 1) * LANES
                                                             ].astype(BF16)
        v_ref[:, h * MLA_SLAB + LANES:(h + 1) * MLA_SLAB] = ones_col


def _mla_proj(z, tab, qag, kvag, wq, wkv, gq, gk, tm, cols, n_x_tiles, tab_tiles):
    r = z.shape[0]
    cqa, ckva, ckpe = cols
    lora = qag.shape[1]
    hs = MLA_HEADS * MLA_SLAB
    hv = MLA_HEADS * LANES
    return pl.pallas_call(
        _mla_proj_kernel,
        out_shape=(jax.ShapeDtypeStruct((r, hs), BF16), jax.ShapeDtypeStruct((r, hs), BF16),
                   jax.ShapeDtypeStruct((r, hs), BF16)),
        grid=(r // tm,),
        in_specs=[
            pl.BlockSpec((tm, lora), lambda i: (i, cqa // lora)),
            pl.BlockSpec((tm, lora), lambda i: (i, ckva // lora)),
            pl.BlockSpec((tm, LANES), lambda i: (i, ckpe // LANES)),
            pl.BlockSpec((tm, LANES), lambda i: (jnp.where(i < n_x_tiles, i % tab_tiles, tab_tiles), 0)),
            pl.BlockSpec((1, lora), lambda i: (0, 0)),
            pl.BlockSpec((1, lora), lambda i: (0, 0)),
            pl.BlockSpec((lora, hs), lambda i: (0, 0)),
            pl.BlockSpec((lora, 2 * hv), lambda i: (0, 0)),
            pl.BlockSpec((1, MLA_SLAB), lambda i: (0, 0)),
            pl.BlockSpec((1, MLA_SLAB), lambda i: (0, 0)),
        ],
        out_specs=(pl.BlockSpec((tm, hs), lambda i: (i, 0)), pl.BlockSpec((tm, hs), lambda i: (i, 0)),
                   pl.BlockSpec((tm, hs), lambda i: (i, 0))),
        compiler_params=_cparams(("parallel",)),
        name="mla_qkv",
    )(z, z, z, tab, qag, kvag, wq, wkv, gq, gk)


def _attn_kernel(n_kv, tq, q_ref, *refs):
    k_refs = refs[:n_kv]
    v_refs = refs[n_kv:2 * n_kv]
    o_ref, s0_ref, s1_ref, m0_ref, m1_ref = refs[2 * n_kv:]
    slots = ((s0_ref, m0_ref), (s1_ref, m1_ref))
    n_tiles = q_ref.shape[0] // tq
    chunks = []
    col = 0
    for kv, k_ref in enumerate(k_refs):
        n_keys = k_ref.shape[0]
        step = min(ATTN_KEY_CHUNK, n_keys)
        for off in range(0, n_keys, step):
            chunks.append((kv, off, col, step))
            col += step

    def scores(t, slot):
        s_ref, m_ref = slots[slot]
        rows = pl.ds(pl.multiple_of(t * tq, tq), tq)
        q = q_ref[rows, :]
        run = None
        for kv, off, c0, size in chunks:
            s = lax.dot_general(q, k_refs[kv][off:off + size, :], (((1,), (1,)), ((), ())),
                                preferred_element_type=F32)
            s_ref[:, c0:c0 + size] = s
            for lb in range(size // LANES):
                blk = s[:, lb * LANES:(lb + 1) * LANES]
                run = blk if run is None else jnp.maximum(run, blk)
        m_ref[...] = run

    def finish(t, slot):
        s_ref, m_ref = slots[slot]
        rows = pl.ds(pl.multiple_of(t * tq, tq), tq)
        m = jnp.max(m_ref[...], axis=-1, keepdims=True)
        acc = None
        for kv, off, c0, size in chunks:
            p = jnp.exp2(s_ref[:, c0:c0 + size] - m).astype(BF16)
            pv = jnp.dot(p, v_refs[kv][off:off + size, :], preferred_element_type=F32)
            acc = pv if acc is None else acc + pv
        o_ref[rows, :] = (acc[:, :LANES] / acc[:, LANES:LANES + 1]).astype(BF16)

    scores(0, 0)

    def body(k, carry):
        scores(2 * k + 1, 1)
        finish(2 * k, 0)
        scores(jnp.minimum(2 * k + 2, n_tiles - 1), 0)
        finish(2 * k + 1, 1)
        return carry

    lax.fori_loop(0, n_tiles // 2, body, 0)
    if n_tiles % 2:
        finish(n_tiles - 1, 0)


def _attention(qo, ko, vo, b, s, tc, tq, latent):
    n_x = b * s
    ctx0 = n_x // tc
    cspec = pl.BlockSpec((tc, MLA_SLAB), lambda i, h: (ctx0 + i, h))
    xspec = pl.BlockSpec((s, MLA_SLAB), lambda i, h: (i, h))
    if latent:
        n_q, n_keys = s, s + tc
        in_specs = [xspec, cspec, xspec, cspec, xspec]
        args = (qo, ko, ko, vo, vo)
    else:
        n_q, n_keys = tc, tc
        in_specs = [cspec, cspec, cspec]
        args = (qo, ko, vo)
    return pl.pallas_call(
        functools.partial(_attn_kernel, (len(args) - 1) // 2, tq),
        out_shape=jax.ShapeDtypeStruct((b * n_q, MLA_HEADS * LANES), BF16),
        grid=(b, MLA_HEADS),
        in_specs=in_specs,
        out_specs=pl.BlockSpec((n_q, LANES), lambda i, h: (i, h)),
        scratch_shapes=[pltpu.VMEM((tq, n_keys), F32), pltpu.VMEM((tq, n_keys), F32),
                        pltpu.VMEM((tq, LANES), F32), pltpu.VMEM((tq, LANES), F32)],
        compiler_params=_cparams(("parallel", "parallel")),
        name="attn_latent" if latent else "attn_ctx",
    )(*args)


S5_HALF = S5_BLOCK_GROUPS * S5_STATE


def _s5_kernel(n_batch, rows_x, rows_c, dot_rows, u_ref, fac_ref, pw_ref, tab_ref, y_ref,
               u2_ref, v_ref, r_ref, m_ref, ot_ref):
    d = pl.program_id(1)
    n_dot = u2_ref.shape[0] // dot_rows

    for s in range(S5_SUB):
        blk = slice(s * LANES, (s + 1) * LANES)
        r_ref[blk, :] = (fac_ref[0, 0, 0] * pw_ref[0, 0, 0, s:s + 1, :]
                         + fac_ref[0, 0, 1] * pw_ref[0, 0, 1, s:s + 1, :]).astype(BF16)
        ot_ref[blk, :] = (fac_ref[0, 0, 2] * pw_ref[0, 0, 2, s:s + 1, :]
                          + fac_ref[0, 0, 3] * pw_ref[0, 0, 3, s:s + 1, :]).astype(BF16)

    c_t = fac_ref[0, 0, 2].astype(BF16)

    def toeplitz(reverse):
        taps = {}
        for s in range(S5_SUB):
            z = s if reverse else S5_SUB - 1 - s
            taps[z] = _dot_nt(r_ref[s * LANES:(s + 1) * LANES, :], c_t).astype(BF16)
        zero = jnp.zeros((LANES, LANES), BF16)
        for s in range(S5_SUB):
            for t in range(S5_SUB):
                lag = (s - t) if reverse else (t - s)
                m_ref[s * LANES:(s + 1) * LANES, t * LANES:(t + 1) * LANES] = taps[lag] if lag >= 0 else zero

    @pl.when(d == 0)
    def _():
        toeplitz(False)

    @pl.when(d == 1)
    def _():
        toeplitz(True)

    def dot_rows_of(i):
        return pl.ds(pl.multiple_of(i * dot_rows, dot_rows), dot_rows)

    def token_rows_of(i, s):
        return pl.ds(i * (dot_rows * S5_SUB) + s, dot_rows, stride=S5_SUB)

    @pl.when(d == 0)
    def _():
        def stage(i, carry):
            rows = pl.ds(pl.multiple_of(i * dot_rows, dot_rows), dot_rows)
            y_ref[rows, :] = u_ref[rows, :].astype(F32)
            return carry

        lax.fori_loop(0, n_dot * S5_SUB, stage, 0)

        def regroup(i, carry):
            for s in range(S5_SUB):
                u2_ref[dot_rows_of(i), s * LANES:(s + 1) * LANES] = y_ref[token_rows_of(i, s), :].astype(BF16)
            return carry

        lax.fori_loop(0, n_dot, regroup, 0)

    def increments(i, carry):
        rows = dot_rows_of(i)
        v_ref[rows, :] = jnp.dot(u2_ref[rows, :], r_ref[...], preferred_element_type=F32)
        return carry

    lax.fori_loop(0, n_dot, increments, 0)

    def cmul_add(ar, ai, cr, ci, xr, xi):
        return ar + cr * xr - ci * xi, ai + cr * xi + ci * xr

    def run(reverse):
        tab = tab_ref.at[0, 0]
        last = 0 if reverse else SUBLANES - 1
        first_row = lax.broadcasted_iota(jnp.int32, (SUBLANES, S5_HALF), 0) == (SUBLANES - 1 - last)

        def segment(bases, n_groups, carry):
            def body(i, carry):
                gi = (n_groups - 1 - i) if reverse else i
                out = []
                for base, (cre, cim) in zip(bases, carry):
                    rows = pl.ds(pl.multiple_of(base + gi * SUBLANES, SUBLANES), SUBLANES)
                    re = v_ref[rows, :S5_HALF]
                    im = v_ref[rows, S5_HALF:]
                    for lvl, shift in enumerate((1, 2, 4)):
                        sh = (SUBLANES - shift) if reverse else shift
                        re, im = cmul_add(re, im, tab[2 * lvl], tab[2 * lvl + 1],
                                          pltpu.roll(re, sh, axis=0), pltpu.roll(im, sh, axis=0))
                    re, im = cmul_add(re, im, tab[6], tab[7], cre, cim)
                    sh1 = (SUBLANES - 1) if reverse else 1
                    v_ref[rows, :S5_HALF] = jnp.where(first_row, cre, pltpu.roll(re, sh1, axis=0))
                    v_ref[rows, S5_HALF:] = jnp.where(first_row, cim, pltpu.roll(im, sh1, axis=0))
                    out.append((jnp.broadcast_to(re[last:last + 1, :], re.shape),
                                jnp.broadcast_to(im[last:last + 1, :], im.shape)))
                return tuple(out)
            return lax.fori_loop(0, n_groups, body, carry)

        zero = jnp.zeros((SUBLANES, S5_HALF), F32)
        carry = tuple((zero, zero) for _ in range(n_batch))
        carry = segment([n_batch * rows_x + bi * rows_c for bi in range(n_batch)], rows_c // SUBLANES, carry)
        segment([bi * rows_x for bi in range(n_batch)], rows_x // SUBLANES, carry)

        def outputs(i, carry):
            rows = dot_rows_of(i)
            y = jnp.dot(u2_ref[rows, :], m_ref[...], preferred_element_type=F32)
            y = y + lax.dot_general(v_ref[rows, :].astype(BF16), ot_ref[...], (((1,), (1,)), ((), ())),
                                    preferred_element_type=F32)
            for s in range(S5_SUB):
                part = y[:, s * LANES:(s + 1) * LANES]
                if reverse:
                    y_ref[token_rows_of(i, s), :] += part
                else:
                    y_ref[token_rows_of(i, s), :] = part
            return carry

        lax.fori_loop(0, n_dot, outputs, 0)

    @pl.when(d == 0)
    def _():
        run(False)

    @pl.when(d == 1)
    def _():
        run(True)


def _s5_scan(z, fac, pw, tabs, layer, b, s, tc, col_u):
    r = z.shape[0]
    rc = r // S5_SUB
    n_blocks = fac.shape[2]
    width = S5_SUB * LANES
    assert width == 2 * S5_HALF
    dot_rows = max(n for n in range(16, 641, 16) if rc % n == 0)
    return pl.pallas_call(
        functools.partial(_s5_kernel, b, s // S5_SUB, tc // S5_SUB, dot_rows),
        out_shape=jax.ShapeDtypeStruct((r, n_blocks * LANES), F32),
        grid=(n_blocks, 2),
        in_specs=[
            pl.BlockSpec((r, LANES), lambda cb, d: (0, col_u // LANES + cb), pipeline_mode=pl.Buffered(1)),
            pl.BlockSpec((None, 1, 1, 4, LANES, width), lambda cb, d: (layer, d, cb, 0, 0, 0),
                         pipeline_mode=pl.Buffered(1)),
            pl.BlockSpec((None, 1, 1, 4, S5_SUB, width), lambda cb, d: (layer, d, cb, 0, 0, 0)),
            pl.BlockSpec((None, 1, 1, 8, SUBLANES, S5_HALF), lambda cb, d: (layer, d, cb, 0, 0, 0)),
        ],
        out_specs=pl.BlockSpec((r, LANES), lambda cb, d: (0, cb)),
        scratch_shapes=[pltpu.VMEM((rc, width), BF16), pltpu.VMEM((rc, width), F32),
                        pltpu.VMEM((width, width), BF16), pltpu.VMEM((width, width), BF16),
                        pltpu.VMEM((width, width), BF16)],
        compiler_params=_cparams(("parallel", "arbitrary")),
        name="s5_scan",
    )(z, fac, pw, tabs)


def _glu_kernel(y_ref, u_ref, d_ref, w_ref, b_ref, o_ref, g_ref):
    tm = y_ref.shape[0]

    def body(r, carry):
        rows = pl.ds(pl.multiple_of(r * ROW_CHUNK, ROW_CHUNK), ROW_CHUNK)
        y = y_ref[rows, :] + d_ref[...] * u_ref[rows, :].astype(F32)
        g_ref[rows, :] = jax.nn.gelu(y).astype(BF16)
        return carry

    lax.fori_loop(0, tm // ROW_CHUNK, body, 0)
    g = g_ref[...]
    gate = jax.nn.sigmoid(jnp.dot(g, w_ref[...], preferred_element_type=F32) + b_ref[...])
    o_ref[...] = (g.astype(F32) * gate).astype(BF16)


def _s5_glu(y, z, d_skip, w_glu, b_glu, tm, col_u, n_row_tiles):
    width = y.shape[1]
    return pl.pallas_call(
        _glu_kernel,
        out_shape=jax.ShapeDtypeStruct((n_row_tiles * tm, width), BF16),
        grid=(n_row_tiles,),
        in_specs=[
            pl.BlockSpec((tm, width), lambda i: (i, 0)),
            pl.BlockSpec((tm, width), lambda i: (i, col_u // width)),
            pl.BlockSpec((1, width), lambda i: (0, 0)),
            pl.BlockSpec((width, width), lambda i: (0, 0)),
            pl.BlockSpec((1, width), lambda i: (0, 0)),
        ],
        out_specs=pl.BlockSpec((tm, width), lambda i: (i, 0)),
        scratch_shapes=[pltpu.VMEM((tm, width), BF16)],
        compiler_params=_cparams(("parallel",)),
        name="s5_glu",
    )(y, z, d_skip, w_glu, b_glu)


def _merge_kernel(n_x_tiles, ax_ref, ac_ref, bx_ref, bc_ref, c_ref, ga_ref, gb_ref, gc_ref, w_ref, o_ref):
    def combine(a_ref, b_ref):
        acc = None
        for r, (br_ref, gate_ref) in enumerate(((a_ref, ga_ref), (b_ref, gb_ref), (c_ref, gc_ref))):
            proj = jnp.dot(br_ref[...], w_ref[r], preferred_element_type=F32)
            term = jax.nn.sigmoid(gate_ref[...].astype(F32)) * proj
            acc = term if acc is None else acc + term
        o_ref[...] = acc.astype(BF16)

    @pl.when(pl.program_id(0) < n_x_tiles)
    def _():
        combine(ax_ref, bx_ref)

    @pl.when(pl.program_id(0) >= n_x_tiles)
    def _():
        combine(ac_ref, bc_ref)


def _merge(a_x, a_c, b_x, b_c, cc, z, w_branch, tm, tn, col_g, n_row_tiles, n_x_tiles):
    width = a_x.shape[1]
    d = w_branch.shape[2]

    def gate_spec(r):
        return pl.BlockSpec((tm, tn), lambda i, j: (i, (col_g + r * d) // tn + j))

    ax_spec, ac_spec = _stream_specs((tm, width), n_x_tiles, 0, lambda j: 0)
    return pl.pallas_call(
        functools.partial(_merge_kernel, n_x_tiles),
        out_shape=jax.ShapeDtypeStruct((n_row_tiles * tm, d), BF16),
        grid=(n_row_tiles, d // tn),
        in_specs=[
            ax_spec, ac_spec, ax_spec, ac_spec,
            pl.BlockSpec((tm, width), lambda i, j: (i, 0)),
            gate_spec(0), gate_spec(1), gate_spec(2),
            pl.BlockSpec((N_BRANCH, width, tn), lambda i, j: (0, 0, j)),
        ],
        out_specs=pl.BlockSpec((tm, tn), lambda i, j: (i, j)),
        compiler_params=_cparams(("parallel", "arbitrary")),
        name="merge",
    )(a_x, a_c, b_x, b_c, cc, z, z, z, w_branch)


def _resid_kernel(n_x_tiles, m_ref, w_ref, x_ref, c_ref, al_ref, o_ref):
    resid = jnp.where(pl.program_id(0) < n_x_tiles, x_ref[...], c_ref[...])
    o_ref[...] = resid + al_ref[0] * jnp.dot(m_ref[...], w_ref[...], preferred_element_type=F32)


def _out_proj_residual(m, w, x_src, c_src, ctx_tile0, mod3, tm, tn, n_row_tiles, n_x_tiles, tiles_per_batch, ctx_row,
                       k_alpha):
    kdim, d = w.shape
    nt = d // tn

    def alpha_map(i, j):
        r = jnp.where(i < n_x_tiles, i // tiles_per_batch, ctx_row)
        return (r * N_MOD + k_alpha, 0, j)

    x_spec, c_spec = _stream_specs((tm, tn), n_x_tiles, ctx_tile0, lambda j: j)
    return pl.pallas_call(
        functools.partial(_resid_kernel, n_x_tiles),
        out_shape=jax.ShapeDtypeStruct((n_row_tiles * tm, d), F32),
        grid=(n_row_tiles, nt),
        in_specs=[
            pl.BlockSpec((tm, kdim), lambda i, j: (i, 0)),
            pl.BlockSpec((kdim, tn), lambda i, j: (0, j)),
            x_spec, c_spec,
            pl.BlockSpec((1, 1, tn), alpha_map),
        ],
        out_specs=pl.BlockSpec((tm, tn), lambda i, j: (i, j)),
        compiler_params=_cparams(("parallel", "arbitrary")),
        name="out_proj",
    )(m, w, x_src, c_src, mod3)


def _ff1_kernel(x_ref, g_ref, sh_ref, sc_ref, w_ref, h_ref, xn_ref):
    @pl.when(pl.program_id(1) == 0)
    def _():
        _norm_mod_rows(x_ref, g_ref, sh_ref, sc_ref, xn_ref)

    a = jnp.maximum(jnp.dot(xn_ref[...], w_ref[...], preferred_element_type=F32), 0.0)
    h_ref[...] = (a * a).astype(BF16)


def _ff1(xs, g, mod3, w, tm, tn, n_row_tiles, n_x_tiles, tiles_per_batch, ctx_row):
    d, dff = w.shape
    return pl.pallas_call(
        _ff1_kernel,
        out_shape=jax.ShapeDtypeStruct((n_row_tiles * tm, dff), BF16),
        grid=(n_row_tiles, dff // tn),
        in_specs=[
            pl.BlockSpec((tm, d), lambda i, j: (i, 0)),
            pl.BlockSpec((1, d), lambda i, j: (0, 0)),
            pl.BlockSpec((1, 1, d), _mod_row_map(n_x_tiles, tiles_per_batch, ctx_row, 3)),
            pl.BlockSpec((1, 1, d), _mod_row_map(n_x_tiles, tiles_per_batch, ctx_row, 4)),
            pl.BlockSpec((d, tn), lambda i, j: (0, j)),
        ],
        out_specs=pl.BlockSpec((tm, tn), lambda i, j: (i, j)),
        scratch_shapes=[pltpu.VMEM((tm, d), BF16)],
        compiler_params=_cparams(("parallel", "arbitrary")),
        name="ff1",
    )(xs, g, mod3, mod3, w)


def _ff2_kernel(h_ref, w_ref, x_ref, al_ref, o_ref):
    k = pl.program_id(2)
    part = jnp.dot(h_ref[...], w_ref[...], preferred_element_type=F32)

    @pl.when(k == 0)
    def _():
        o_ref[...] = part

    @pl.when(k > 0)
    def _():
        o_ref[...] += part

    @pl.when(k == pl.num_programs(2) - 1)
    def _():
        o_ref[...] = x_ref[...] + al_ref[0] * o_ref[...]


def _ff2(h, w, xs, mod3, tm, tn, tk, n_row_tiles, n_x_tiles, tiles_per_batch, ctx_row):
    dff, d = w.shape

    def alpha_map(i, j, k):
        r = jnp.where(i < n_x_tiles, i // tiles_per_batch, ctx_row)
        return (r * N_MOD + 5, 0, j)

    return pl.pallas_call(
        _ff2_kernel,
        out_shape=jax.ShapeDtypeStruct((n_row_tiles * tm, d), F32),
        grid=(n_row_tiles, d // tn, dff // tk),
        in_specs=[
            pl.BlockSpec((tm, tk), lambda i, j, k: (i, k)),
            pl.BlockSpec((tk, tn), lambda i, j, k: (k, j)),
            pl.BlockSpec((tm, tn), lambda i, j, k: (i, j)),
            pl.BlockSpec((1, 1, tn), alpha_map),
        ],
        out_specs=pl.BlockSpec((tm, tn), lambda i, j, k: (i, j)),
        compiler_params=_cparams(("parallel", "parallel", "arbitrary")),
        name="ff2",
    )(h, w, xs, mod3)


def _rope_partner():
    j = np.arange(MLA_ROPE)
    quarter = MLA_ROPE // 4
    return np.where((j // quarter) % 2 == 0, j + quarter, j - quarter)


def _rope_table(s, tm):
    pos = jnp.arange(s)
    row = (pos // GRID_W).astype(F32)
    col = (pos % GRID_W).astype(F32)
    n_freq = MLA_ROPE // 4
    inv_freq = ROPE_THETA ** (-jnp.arange(n_freq, dtype=F32) / n_freq)
    ang_r = row[:, None] * inv_freq
    ang_c = col[:, None] * inv_freq
    cos = jnp.concatenate([jnp.cos(ang_r)] * 2 + [jnp.cos(ang_c)] * 2, axis=-1)
    sin = jnp.concatenate([-jnp.sin(ang_r), jnp.sin(ang_r), -jnp.sin(ang_c), jnp.sin(ang_c)], axis=-1)
    ident = jnp.concatenate([jnp.ones((tm, MLA_ROPE), F32), jnp.zeros((tm, MLA_ROPE), F32)], axis=-1)
    return jnp.concatenate([jnp.concatenate([cos, sin], axis=-1), ident], axis=0)


def _cast_kernel(w_ref, o_ref):
    o_ref[...] = w_ref[...].astype(BF16)


def _layer_bf16(w_all, layer):
    rows, cols = w_all.shape[1:]
    tr = CAST_BLOCK_BYTES // (4 * cols)
    while rows % tr:
        tr //= 2
    return pl.pallas_call(
        _cast_kernel,
        out_shape=jax.ShapeDtypeStruct((rows, cols), BF16),
        grid=(rows // tr,),
        in_specs=[pl.BlockSpec((None, tr, cols), lambda i: (layer, i, 0))],
        out_specs=pl.BlockSpec((tr, cols), lambda i: (i, 0)),
        compiler_params=_cparams(("parallel",)),
        name="cast_bf16",
    )(w_all)


def _pack_w_in(w_in, b_in, gate_b, sizes, n_pad):
    w_t = w_in.T
    d_model = w_in.shape[0]
    bounds = np.cumsum((0,) + sizes)
    seg = [slice(int(bounds[i]), int(bounds[i + 1])) for i in range(len(sizes))]
    partner = _rope_partner()
    order = (0, 1, 2, 3, 5, 6, 8, 9)
    names = ("q", "k", "v", "o", "qa", "kva", "u", "gates")
    offs, pos = {}, 0
    for name, i in zip(names, order):
        offs[name] = pos
        pos += sizes[i]
    offs["kpe"], offs["kpe_sw"] = pos, pos + MLA_ROPE
    n_zero = n_pad - pos - 2 * MLA_ROPE
    kpe_w, kpe_b = w_t[seg[7]], b_in[seg[7]]
    w = jnp.concatenate([w_t[seg[i]] for i in order] + [kpe_w, kpe_w[partner], jnp.zeros((n_zero, d_model), F32)],
                        axis=0).astype(BF16)
    b = jnp.concatenate([b_in[seg[i]] for i in order] + [kpe_b, kpe_b[partner], jnp.zeros((n_zero,), F32)])
    n_g = sizes[4]
    wg = jnp.concatenate([w_t[seg[4]], jnp.zeros((LANES - n_g, d_model), F32)], axis=0).astype(BF16)
    bg = jnp.concatenate([b_in[seg[4]] + gate_b.reshape(-1), jnp.zeros((LANES - n_g,), F32)])[None, :]
    return w, b[None, :], wg, bg, offs


def _pack_mla(w_uq, w_ukv, qn_g, kn_g):
    partner = _rope_partner()
    lora = w_uq.shape[0]
    wq = w_uq.reshape(lora, MLA_HEADS, MLA_DQK)
    wq = jnp.concatenate([wq, wq[:, :, MLA_NOPE + partner]], axis=-1).reshape(lora, MLA_HEADS * MLA_SLAB)
    wkv = w_ukv.reshape(w_ukv.shape[0], MLA_HEADS, -1)
    wkv = jnp.concatenate([wkv[:, :, :MLA_NOPE].reshape(lora, -1), wkv[:, :, MLA_NOPE:].reshape(lora, -1)], axis=-1)

    def gains(g):
        return jnp.concatenate([g, g[MLA_NOPE + partner]])[None, :]

    return wq.astype(BF16), wkv.astype(BF16), gains(qn_g), gains(kn_g)


def _pack_s5(a_re, a_im, log_dt, b_re, b_im, c_re, c_im):
    n_dir, n_groups, n_state = a_re.shape
    gc = b_re.shape[-1]
    nb = n_groups // S5_BLOCK_GROUPS
    lam_re = jnp.minimum(a_re.astype(F32), -1e-4)
    lam_im = a_im.astype(F32)
    dt = jnp.exp(log_dt.astype(F32))[..., None]

    def pole_power(k):
        mag = jnp.exp(k * lam_re * dt)
        return mag * jnp.cos(k * lam_im * dt), mag * jnp.sin(k * lam_im * dt)

    bar_re, bar_im = pole_power(1.0)
    den = lam_re * lam_re + lam_im * lam_im
    f_re = ((bar_re - 1.0) * lam_re + bar_im * lam_im) / den
    f_im = (bar_im * lam_re - (bar_re - 1.0) * lam_im) / den
    bb_re = f_re[..., None] * b_re.astype(F32) - f_im[..., None] * b_im.astype(F32)
    bb_im = f_re[..., None] * b_im.astype(F32) + f_im[..., None] * b_re.astype(F32)
    eye = jnp.eye(S5_BLOCK_GROUPS, dtype=F32)
    sub = S5_SUB

    def per_block(a):
        return a.reshape(a.shape[:-2] + (nb, S5_BLOCK_GROUPS * n_state))

    def block_b(part):
        p = part.reshape(n_dir, nb, S5_BLOCK_GROUPS, n_state, gc)
        m = jnp.einsum('dbgnc,gh->dbgchn', p, eye, precision=HIGHEST)
        return m.reshape(n_dir, nb, LANES, S5_HALF)

    def block_c(part):
        p = part.astype(F32).reshape(n_dir, nb, S5_BLOCK_GROUPS, gc, n_state)
        m = jnp.einsum('dbgcn,gh->dbgnhc', p, eye, precision=HIGHEST)
        return m.reshape(n_dir, nb, S5_HALF, LANES)

    bm_re, bm_im = block_b(bb_re), block_b(bb_im)
    cm_re, cm_im = block_c(c_re), block_c(c_im)

    tau = jnp.arange(sub + 1, dtype=F32)[:, None, None, None]
    p_re, p_im = (per_block(p) for p in pole_power(tau))
    ct_re, ct_im = jnp.swapaxes(cm_re, -1, -2), jnp.swapaxes(cm_im, -1, -2)
    cat = functools.partial(jnp.concatenate, axis=-1)
    fac = jnp.stack([cat([bm_re, bm_im]), cat([-bm_im, bm_re]), cat([ct_re, -ct_im]), cat([-ct_im, -ct_re])],
                    axis=2)
    t = np.arange(sub)
    pws = []
    for d in range(n_dir):
        to_exit = (sub - 1 - t) if d == 0 else t
        age = (t + 1) if d == 0 else (sub - t)
        rows = [p_re[to_exit, d], p_im[to_exit, d], p_re[age, d], p_im[age, d]]
        pws.append(jnp.stack([cat([a, a]).transpose(1, 0, 2) for a in rows], axis=1))
    pw = jnp.stack(pws)

    rows = jnp.arange(SUBLANES)
    tabs = []
    for d in range(n_dir):
        per_dir = []
        for shift in (1, 2, 4):
            keep = ((rows <= SUBLANES - 1 - shift) if d == 1 else (rows >= shift))[None, :, None]
            s_re, s_im = pole_power(float(shift * sub))
            per_dir += [jnp.where(keep, per_block(s_re[d])[:, None, :], 0.0),
                        jnp.where(keep, per_block(s_im[d])[:, None, :], 0.0)]
        expo = (((SUBLANES - rows) if d == 1 else (rows + 1)) * sub).astype(F32)
        s_re, s_im = pole_power(expo[:, None, None, None])
        per_dir += [jnp.moveaxis(per_block(s_re[:, d]), 0, 1), jnp.moveaxis(per_block(s_im[:, d]), 0, 1)]
        tabs.append(jnp.stack(per_dir, axis=1))
    return fac, pw, jnp.stack(tabs).astype(F32)


def _tri_matrices():
    t = np.arange(ML_CHUNK)
    lower = t[None, :] <= t[:, None]
    mask = np.where(np.stack([lower, lower.T]), 0.0, NEG_BIG).astype(np.float32)
    return jnp.asarray(lower.astype(np.float32), dtype=BF16), jnp.asarray(mask)


def kernel(x, c, ctx, c_ctx, w_mod, b_mod, norm_g, w_in, b_in, ml_gate_b, ml_norm_g, mla_qa_g, mla_kva_g, mla_w_uq, mla_w_ukv, mla_qn_g, mla_kn_g, s5_a_re, s5_a_im, s5_log_dt, s5_b_re, s5_b_im, s5_c_re, s5_c_im, s5_d, s5_w_glu, s5_b_glu, w_branch, w_out, w_ff1, w_ff2):
    b, s, d = x.shape
    tc = ctx.shape[1]
    depth = w_mod.shape[0]
    dv = ml_norm_g.shape[2]
    dk = dv // 2
    lora = mla_qa_g.shape[1]
    s5_width = s5_d.shape[1]
    branch_w = w_branch.shape[2]
    sizes = (ML_HEADS * dk, ML_HEADS * dk, ML_HEADS * dv, ML_HEADS * dv, 4 * ML_HEADS, lora, lora, MLA_ROPE,
             s5_width, N_BRANCH * d)
    assert sum(sizes) == w_in.shape[2] and b + 1 <= SUBLANES
    assert s % ML_CHUNK == 0 and tc % ML_CHUNK == 0 and branch_w == ML_HEADS * dv == MLA_HEADS * LANES == s5_width

    n_x = b * s
    n_c = b * tc
    tm = _row_tile(s, n_c)
    n_x_tiles = n_x // tm
    n_tiles = n_x_tiles + n_c // tm
    tiles_per_batch = s // tm
    tile_args = (n_x_tiles, tiles_per_batch, b)

    x_src, c_src, ctx_tile0 = x.reshape(n_x, d), ctx.reshape(n_c, d), 0
    cc = jnp.concatenate([c, c_ctx[None, :], jnp.zeros((SUBLANES - b - 1, d), F32)], axis=0)
    mod = _modulation(cc, w_mod, b_mod)
    tm_q = min(tm, 512)
    tab = _rope_table(s, tm_q)
    tri, ml_mask = _tri_matrices()
    s5_packed = jax.vmap(_pack_s5)(s5_a_re, s5_a_im, s5_log_dt, s5_b_re, s5_b_im, s5_c_re, s5_c_im)
    n_used = sum(sizes) - sizes[4] + MLA_ROPE
    tn_in = 1280
    n_pad = -(-n_used // tn_in) * tn_in

    for l in range(depth):
        with_ctx_out = l < depth - 1
        mod3 = mod[l].reshape(SUBLANES * N_MOD, 1, d)
        w_p, b_p, wg, bg, offs = _pack_w_in(w_in[l], b_in[l], ml_gate_b[l], sizes, n_pad)
        z, gz = _in_proj(x_src, c_src, ctx_tile0, n_tiles, norm_g[l, 0][None, :], mod3, w_p, b_p, wg, bg, tm, tn_in,
                         *tile_args)

        gates, gates_t = _gate_prep(gz, tri)
        a_x, a_c = _mlstm(z, gates, gates_t, ml_mask, ml_norm_g[l], b, s, tc, dk, dv,
                          (offs["q"], offs["k"], offs["v"], offs["o"]), with_ctx_out)

        wq, wkv, gq, gk = _pack_mla(mla_w_uq[l], mla_w_ukv[l], mla_qn_g[l], mla_kn_g[l])
        qo, ko, vo = _mla_proj(z, tab, mla_qa_g[l][None, :], mla_kva_g[l][None, :], wq, wkv, gq, gk, tm_q,
                               (offs["qa"], offs["kva"], offs["kpe"]), n_x // tm_q, s // tm_q)
        tq = min(512, s)
        b_x = _attention(qo, ko, vo, b, s, tc, tq, True)

        y = _s5_scan(z, *s5_packed, l, b, s, tc, offs["u"])
        n_out_tiles = n_tiles if with_ctx_out else n_x_tiles
        c_all = _s5_glu(y, z, s5_d[l][None, :], s5_w_glu[l].astype(BF16), s5_b_glu[l][None, :], tm, offs["u"],
                        n_out_tiles)

        if with_ctx_out:
            b_c = _attention(qo, ko, vo, b, s, tc, min(tq, tc), False)
        else:
            a_c, b_c = a_x, b_x
        w_br = _layer_bf16(w_branch.reshape(depth, N_BRANCH * branch_w, d), l).reshape(N_BRANCH, branch_w, d)
        merged = _merge(a_x, a_c, b_x, b_c, c_all, z, w_br, tm, 512, offs["gates"],
                        n_out_tiles, n_x_tiles)
        xs1 = _out_proj_residual(merged, _layer_bf16(w_out, l), x_src, c_src, ctx_tile0, mod3, tm, 1024,
                                 n_out_tiles, *tile_args, 2)
        hid = _ff1(xs1, norm_g[l, 1][None, :], mod3, _layer_bf16(w_ff1, l), tm, 1024, n_out_tiles, *tile_args)
        xs = _ff2(hid, _layer_bf16(w_ff2, l), xs1, mod3, tm, 1024, 2048, n_out_tiles, *tile_args)
        x_src, c_src, ctx_tile0 = xs, xs, n_x_tiles

    return xs.reshape(b, s, d)
```

```python
import functools
import math

import jax
import jax.numpy as jnp
import numpy as np
from jax import lax
from jax.experimental import pallas as pl
from jax.experimental.pallas import tpu as pltpu

F32 = jnp.float32
BF16 = jnp.bfloat16
HIGHEST = lax.Precision.HIGHEST

N_MOD = 6
N_BRANCH = 3
ML_HEADS = 4
MLA_HEADS = 8
MLA_NOPE = 128
MLA_ROPE = 64
MLA_DQK = MLA_NOPE + MLA_ROPE
MLA_SLAB = 256
GRID_W = 64
ROPE_THETA = 10000.0
S5_GROUP = 16
S5_STATE = 64
S5_BLOCK_GROUPS = 8
EPS = 1e-6
NEG_BIG = -1e30

LANES = 128
SUBLANES = 8
VMEM_LIMIT = 56 * 1024 * 1024

ML_CHUNK = 256
ML_GATE_ROWS = 32
S5_SUB = 8
ATTN_KEY_CHUNK = 512
Q_PRESCALE = MLA_DQK ** -0.5 * math.log2(math.e)
ROW_CHUNK = 128
CAST_BLOCK_BYTES = 8 * 1024 * 1024


def _cparams(sem):
    return pltpu.CompilerParams(dimension_semantics=sem, vmem_limit_bytes=VMEM_LIMIT)


def _row_tile(n_x_rows_per_batch, n_ctx_rows):
    tm = 1024
    while n_x_rows_per_batch % tm or n_ctx_rows % tm:
        tm //= 2
    return tm


def _mod_kernel(c_ref, w_ref, b_ref, o_ref):
    s = c_ref[...]
    s = s * jax.nn.sigmoid(s)
    o_ref[0] = jnp.dot(s.astype(BF16), w_ref[0].astype(BF16), preferred_element_type=F32) + b_ref[0]


def _modulation(cc, w_mod, b_mod):
    n_layers, d, n = w_mod.shape
    tn = 1024
    return pl.pallas_call(
        _mod_kernel,
        out_shape=jax.ShapeDtypeStruct((n_layers, SUBLANES, n), F32),
        grid=(n_layers, n // tn),
        in_specs=[
            pl.BlockSpec((SUBLANES, d), lambda l, j: (0, 0)),
            pl.BlockSpec((1, d, tn), lambda l, j: (l, 0, j)),
            pl.BlockSpec((1, 1, tn), lambda l, j: (l, 0, j)),
        ],
        out_specs=pl.BlockSpec((1, SUBLANES, tn), lambda l, j: (l, 0, j)),
        compiler_params=_cparams(("parallel", "parallel")),
        name="adaln_mod",
    )(cc, w_mod, b_mod.reshape(n_layers, 1, n))


def _norm_mod_rows(x_ref, g_ref, sh_ref, sc_ref, xn_ref):
    tm = x_ref.shape[0]
    g = g_ref[...]
    sc = 1.0 + sc_ref[0]
    sh = sh_ref[0]

    def body(r, carry):
        rows = pl.ds(pl.multiple_of(r * ROW_CHUNK, ROW_CHUNK), ROW_CHUNK)
        x = x_ref[rows, :]
        ms = jnp.mean(x * x, axis=-1, keepdims=True)
        y = x * lax.rsqrt(ms + EPS) * g
        xn_ref[rows, :] = (y * sc + sh).astype(BF16)
        return carry

    lax.fori_loop(0, tm // ROW_CHUNK, body, 0)


def _mod_row_map(n_x_tiles, tiles_per_batch, ctx_row, k):
    def index_map(i, j):
        r = jnp.where(i < n_x_tiles, i // tiles_per_batch, ctx_row)
        return (r * N_MOD + k, 0, 0)
    return index_map


def _stream_specs(block, n_x_tiles, ctx_tile0, col_map):
    x_spec = pl.BlockSpec(block, lambda i, *r: (jnp.minimum(i, n_x_tiles - 1), col_map(*r)))
    c_spec = pl.BlockSpec(block, lambda i, *r: (ctx_tile0 + jnp.maximum(i - n_x_tiles, 0), col_map(*r)),
                          pipeline_mode=pl.Buffered(1))
    return x_spec, c_spec


def _dot_nt(a, b_t):
    return lax.dot_general(a, b_t, (((1,), (1,)), ((), ())), preferred_element_type=F32)


def _in_kernel(n_x_tiles, x_ref, c_ref, g_ref, sh_ref, sc_ref, w_ref, b_ref, wg_ref, bg_ref, z_ref, gz_ref, xn_ref):
    @pl.when(pl.program_id(1) == 0)
    def _():
        @pl.when(pl.program_id(0) < n_x_tiles)
        def _():
            _norm_mod_rows(x_ref, g_ref, sh_ref, sc_ref, xn_ref)

        @pl.when(pl.program_id(0) >= n_x_tiles)
        def _():
            _norm_mod_rows(c_ref, g_ref, sh_ref, sc_ref, xn_ref)

        gz_ref[...] = _dot_nt(xn_ref[...], wg_ref[...]) + bg_ref[...]

    z_ref[...] = (_dot_nt(xn_ref[...], w_ref[...]) + b_ref[...]).astype(BF16)


def _in_proj(x_src, c_src, ctx_tile0, n_tiles, g, mod3, w, b, wg, bg, tm, tn, n_x_tiles, tiles_per_batch, ctx_row):
    d = x_src.shape[1]
    r = n_tiles * tm
    nz = w.shape[0]
    x_spec, c_spec = _stream_specs((tm, d), n_x_tiles, ctx_tile0, lambda j: 0)
    return pl.pallas_call(
        functools.partial(_in_kernel, n_x_tiles),
        out_shape=(jax.ShapeDtypeStruct((r, nz), BF16), jax.ShapeDtypeStruct((r, LANES), F32)),
        grid=(n_tiles, nz // tn),
        in_specs=[
            x_spec, c_spec,
            pl.BlockSpec((1, d), lambda i, j: (0, 0)),
            pl.BlockSpec((1, 1, d), _mod_row_map(n_x_tiles, tiles_per_batch, ctx_row, 0)),
            pl.BlockSpec((1, 1, d), _mod_row_map(n_x_tiles, tiles_per_batch, ctx_row, 1)),
            pl.BlockSpec((tn, d), lambda i, j: (j, 0)),
            pl.BlockSpec((1, tn), lambda i, j: (0, j)),
            pl.BlockSpec((LANES, d), lambda i, j: (0, 0)),
            pl.BlockSpec((1, LANES), lambda i, j: (0, 0)),
        ],
        out_specs=(
            pl.BlockSpec((tm, tn), lambda i, j: (i, j)),
            pl.BlockSpec((tm, LANES), lambda i, j: (i, 0)),
        ),
        scratch_shapes=[pltpu.VMEM((tm, d), BF16)],
        compiler_params=_cparams(("parallel", "arbitrary")),
        name="in_proj",
    )(x_src, c_src, g, mod3, mod3, w, b, wg, bg)


def _log_sigmoid(x):
    return jnp.minimum(x, 0.0) - jnp.log1p(jnp.exp(-jnp.abs(x)))


def _split3(a):
    hi = a.astype(BF16)
    r1 = a - hi.astype(F32)
    mid = r1.astype(BF16)
    lo = (r1 - mid.astype(F32)).astype(BF16)
    return hi, mid, lo


def _gate_prep_kernel(g_ref, tri_ref, a_ref, at_ref):
    g = g_ref[...]
    kind = lax.broadcasted_iota(jnp.int32, g.shape, 1) // ML_HEADS
    lg = jnp.where((kind == 1) | (kind == 3), _log_sigmoid(g), g)
    pre = sum(jnp.dot(tri_ref[...], p, preferred_element_type=F32) for p in _split3(lg))
    suf = pre[ML_CHUNK - 1:ML_CHUNK, :] - pre + lg
    a = jnp.where(kind == 4, pltpu.roll(pre, 3 * ML_HEADS, axis=1),
                  jnp.where(kind == 5, pltpu.roll(suf, 2 * ML_HEADS, axis=1), lg))
    a_ref[...] = a
    at_ref[0] = a.T[:ML_GATE_ROWS, :]


def _gate_prep(gz, tri):
    r = gz.shape[0]
    return pl.pallas_call(
        _gate_prep_kernel,
        out_shape=(jax.ShapeDtypeStruct((r, LANES), F32),
                   jax.ShapeDtypeStruct((r // ML_CHUNK, ML_GATE_ROWS, ML_CHUNK), F32)),
        grid=(r // ML_CHUNK,),
        in_specs=[pl.BlockSpec((ML_CHUNK, LANES), lambda i: (i, 0)),
                  pl.BlockSpec((ML_CHUNK, ML_CHUNK), lambda i: (0, 0))],
        out_specs=(pl.BlockSpec((ML_CHUNK, LANES), lambda i: (i, 0)),
                   pl.BlockSpec((1, ML_GATE_ROWS, ML_CHUNK), lambda i: (i, 0, 0))),
        compiler_params=_cparams(("parallel",)),
        name="mlstm_gates",
    )(gz, tri)


def _ml_chunk(q, k_t, v, cum_cb, li_r, lf_r, cum_r, mask_add, carry, inv_scale):
    c_mat, n_mat, m = carry
    length, dv = v.shape
    crow = cum_r - li_r
    total = jnp.sum(lf_r, axis=-1, keepdims=True)
    ones = jnp.ones((length, LANES), BF16)
    n_blk = length // LANES

    def wide(a):
        return jnp.concatenate([a] * (dv // LANES), axis=1)

    log_w = [cum_cb - crow[:, j * LANES:(j + 1) * LANES] + mask_add[:, j * LANES:(j + 1) * LANES]
             for j in range(n_blk)]
    row_max = jnp.max(functools.reduce(jnp.maximum, log_w), axis=-1, keepdims=True)
    log_inter = cum_cb + m
    m_t = jnp.maximum(log_inter, row_max)
    w_inter = jnp.exp(log_inter - m_t)
    qk = jnp.dot(q, k_t, preferred_element_type=F32)
    s = jnp.concatenate([qk[:, j * LANES:(j + 1) * LANES] * jnp.exp(log_w[j] - m_t) for j in range(n_blk)],
                        axis=1).astype(BF16)
    num = wide(w_inter) * jnp.dot(q, c_mat.astype(BF16), preferred_element_type=F32)
    num = num + jnp.dot(s, v, preferred_element_type=F32)
    den = w_inter * jnp.dot(q, n_mat.astype(BF16), preferred_element_type=F32)
    den = den + jnp.dot(s, ones, preferred_element_type=F32)
    h = num * wide(1.0 / jnp.maximum(jnp.abs(den), jnp.exp(-m_t) * inv_scale))

    log_end = total - crow
    m_new = jnp.maximum(total + m, jnp.max(log_end, axis=-1, keepdims=True))
    decay = jnp.exp(total + m - m_new)
    kw_t = (k_t.astype(F32) * jnp.exp(log_end - m_new)).astype(BF16)
    c_new = decay * c_mat + jnp.dot(kw_t, v, preferred_element_type=F32)
    n_new = decay * n_mat + jnp.dot(kw_t, ones, preferred_element_type=F32)
    return h, (c_new, n_new, m_new)


def _mlstm_kernel(with_ctx_out, qx, kx, vx, ox, qc, kc, vc, oc, gx, gc, gtx, gtc, mask_ref, ng_ref, *rest):
    if with_ctx_out:
        ax_ref, ac_ref, cumx, ktx, cumc, ktc, hx, hc = rest
    else:
        ax_ref, cumx, ktx, cumc, ktc, hx, hc = rest
        ac_ref = None
    head = pl.program_id(1)
    dk = qx.shape[1]
    inv_scale = float(dk) ** 0.5
    n_x_chunks = qx.shape[0] // ML_CHUNK
    n_c_chunks = qc.shape[0] // ML_CHUNK

    def prep(g_ref, k_ref, cum_ref, kt_ref):
        def body(i, carry):
            rows = pl.ds(pl.multiple_of(i * ML_CHUNK, ML_CHUNK), ML_CHUNK)
            a = g_ref[rows, :]
            col = lax.broadcasted_iota(jnp.int32, a.shape, 1)
            for d in range(2):
                pick = jnp.sum(jnp.where(col == (4 + d) * ML_HEADS + head, a, 0.0), axis=-1, keepdims=True)
                cum_ref[d, rows, :] = jnp.broadcast_to(pick, a.shape)
            kt_ref[i] = k_ref[rows, :].astype(F32).T.astype(BF16)
            return carry
        n_chunks = g_ref.shape[0] // ML_CHUNK
        lax.fori_loop(0, n_chunks, body, 0, unroll=2 if n_chunks % 2 == 0 else 1)

    prep(gx, kx, cumx, ktx)
    prep(gc, kc, cumc, ktc)

    for d in range(2):
        reverse = d == 1
        mask_add = mask_ref[d]

        def step(q_ref, v_ref, cum_ref, gt_ref, kt_ref, h_ref, ci, carry):
            rows = pl.ds(pl.multiple_of(ci * ML_CHUNK, ML_CHUNK), ML_CHUNK)
            li_r, lf_r, cum_r = (gt_ref[ci, pl.ds(kind * ML_HEADS + head, 1), :] for kind in (2 * d, 2 * d + 1, 4 + d))
            h, carry = _ml_chunk(q_ref[rows, :], kt_ref[ci], v_ref[rows, :], cum_ref[d, rows, :], li_r, lf_r, cum_r,
                                 mask_add, carry, inv_scale)
            if reverse:
                h_ref[rows, :] += h
            else:
                h_ref[rows, :] = h
            return carry

        carry = (jnp.zeros((dk, vx.shape[1]), F32), jnp.zeros((dk, LANES), F32), jnp.zeros((1, 1), F32))

        def ctx_body(i, carry):
            ci = (n_c_chunks - 1 - i) if reverse else i
            return step(qc, vc, cumc, gtc, ktc, hc, ci, carry)

        def x_body(i, carry):
            ci = (n_x_chunks - 1 - i) if reverse else i
            return step(qx, vx, cumx, gtx, ktx, hx, ci, carry)

        carry = lax.fori_loop(0, n_c_chunks, ctx_body, carry)
        lax.fori_loop(0, n_x_chunks, x_body, carry, unroll=2 if n_x_chunks % 2 == 0 else 1)

    ng = ng_ref[0]

    def finish(h_ref, o_ref, a_ref):
        def body(i, carry):
            rows = pl.ds(pl.multiple_of(i * ML_CHUNK, ML_CHUNK), ML_CHUNK)
            h = h_ref[rows, :]
            hn = h * lax.rsqrt(jnp.mean(h * h, axis=-1, keepdims=True) + EPS) * ng
            a_ref[rows, :] = (hn * jax.nn.sigmoid(o_ref[rows, :].astype(F32))).astype(BF16)
            return carry
        lax.fori_loop(0, h_ref.shape[0] // ML_CHUNK, body, 0)

    finish(hx, ox, ax_ref)
    if with_ctx_out:
        finish(hc, oc, ac_ref)


def _mlstm(z, gates, gates_t, mask, ml_norm_g, b, s, tc, dk, dv, cols, with_ctx_out):
    n_x = b * s
    cq, ck, cv, co = cols
    ctx0 = n_x // tc
    xc, cc = s // ML_CHUNK, tc // ML_CHUNK

    def xspec(width, col0):
        return pl.BlockSpec((s, width), lambda i, h: (i, col0 // width + h))

    def cspec(width, col0):
        return pl.BlockSpec((tc, width), lambda i, h: (ctx0 + i, col0 // width + h))

    out_shape = [jax.ShapeDtypeStruct((n_x, ML_HEADS * dv), BF16)]
    out_specs = [pl.BlockSpec((s, dv), lambda i, h: (i, h))]
    if with_ctx_out:
        out_shape.append(jax.ShapeDtypeStruct((b * tc, ML_HEADS * dv), BF16))
        out_specs.append(pl.BlockSpec((tc, dv), lambda i, h: (i, h)))
    res = pl.pallas_call(
        functools.partial(_mlstm_kernel, with_ctx_out),
        out_shape=tuple(out_shape),
        grid=(b, ML_HEADS),
        in_specs=[
            xspec(dk, cq), xspec(dk, ck), xspec(dv, cv), xspec(dv, co),
            cspec(dk, cq), cspec(dk, ck), cspec(dv, cv), cspec(dv, co),
            pl.BlockSpec((s, LANES), lambda i, h: (i, 0)),
            pl.BlockSpec((tc, LANES), lambda i, h: (ctx0 + i, 0)),
            pl.BlockSpec((xc, ML_GATE_ROWS, ML_CHUNK), lambda i, h: (i, 0, 0)),
            pl.BlockSpec((cc, ML_GATE_ROWS, ML_CHUNK), lambda i, h: (b * xc // cc + i, 0, 0)),
            pl.BlockSpec((2, ML_CHUNK, ML_CHUNK), lambda i, h: (0, 0, 0)),
            pl.BlockSpec((1, 1, dv), lambda i, h: (h, 0, 0)),
        ],
        out_specs=tuple(out_specs),
        scratch_shapes=[
            pltpu.VMEM((2, s, LANES), F32), pltpu.VMEM((xc, dk, ML_CHUNK), BF16),
            pltpu.VMEM((2, tc, LANES), F32), pltpu.VMEM((cc, dk, ML_CHUNK), BF16),
            pltpu.VMEM((s, dv), F32), pltpu.VMEM((tc, dv), F32),
        ],
        compiler_params=_cparams(("parallel", "parallel")),
        name="mlstm",
    )(z, z, z, z, z, z, z, z, gates, gates, gates_t, gates_t, mask, ml_norm_g.reshape(ML_HEADS, 1, dv))
    return res if with_ctx_out else (res[0], None)


def _mla_proj_kernel(qa_ref, kva_ref, kpe_ref, tab_ref, qag_ref, kvag_ref, wq_ref, wkv_ref,
                     gq_ref, gk_ref, q_ref, k_ref, v_ref):
    def normed(a_ref, g_ref):
        a = a_ref[...].astype(F32)
        return (a * lax.rsqrt(jnp.mean(a * a, axis=-1, keepdims=True) + EPS) * g_ref[...]).astype(BF16)

    q_all = jnp.dot(normed(qa_ref, qag_ref), wq_ref[...], preferred_element_type=F32)
    kv_all = jnp.dot(normed(kva_ref, kvag_ref), wkv_ref[...], preferred_element_type=F32)
    tab = tab_ref[...]
    lane = lax.broadcasted_iota(jnp.int32, tab.shape, 1)
    first_half = lane < MLA_ROPE
    gq = gq_ref[...]
    gk = gk_ref[...]
    inv_dqk = 1.0 / MLA_DQK

    kpe = kpe_ref[...].astype(F32)
    ss_kpe = jnp.sum(jnp.where(first_half, kpe * kpe, 0.0), axis=-1, keepdims=True)
    kpe_t = kpe * (tab * gk[:, LANES:])
    kpe_rot = jnp.where(first_half, kpe_t + pltpu.roll(kpe_t, MLA_ROPE, axis=1), 0.0)

    for h in range(MLA_HEADS):
        qn = q_all[:, h * MLA_SLAB:h * MLA_SLAB + LANES]
        qp = q_all[:, h * MLA_SLAB + LANES:(h + 1) * MLA_SLAB]
        ss = jnp.sum(qn * qn, axis=-1, keepdims=True) + jnp.sum(jnp.where(first_half, qp * qp, 0.0), axis=-1,
                                                                keepdims=True)
        r = lax.rsqrt(ss * inv_dqk + EPS) * Q_PRESCALE
        qp_t = qp * (tab * gq[:, LANES:])
        qp_rot = qp_t + pltpu.roll(qp_t, MLA_ROPE, axis=1)
        q_ref[:, h * MLA_SLAB:h * MLA_SLAB + LANES] = (qn * r * gq[:, :LANES]).astype(BF16)
        q_ref[:, h * MLA_SLAB + LANES:(h + 1) * MLA_SLAB] = (qp_rot * r).astype(BF16)

        kn = kv_all[:, h * LANES:(h + 1) * LANES]
        rk = lax.rsqrt((jnp.sum(kn * kn, axis=-1, keepdims=True) + ss_kpe) * inv_dqk + EPS)
        k_ref[:, h * MLA_SLAB:h * MLA_SLAB + LANES] = (kn * rk * gk[:, :LANES]).astype(BF16)
        k_ref[:, h * MLA_SLAB + LANES:(h + 1) * MLA_SLAB] = (kpe_rot * rk).astype(BF16)

    ones_col = jnp.where(lane == 0, 1.0, 0.0).astype(BF16)
    for h in range(MLA_HEADS):
        v_ref[:, h * MLA_SLAB:h * MLA_SLAB + LANES] = kv_all[:, (MLA_HEADS + h) * LANES:(MLA_HEADS + h + 1) * LANES
                                                             ].astype(BF16)
        v_ref[:, h * MLA_SLAB + LANES:(h + 1) * MLA_SLAB] = ones_col


def _mla_proj(z, tab, qag, kvag, wq, wkv, gq, gk, tm, cols, n_x_tiles, tab_tiles):
    r = z.shape[0]
    cqa, ckva, ckpe = cols
    lora = qag.shape[1]
    hs = MLA_HEADS * MLA_SLAB
    hv = MLA_HEADS * LANES
    return pl.pallas_call(
        _mla_proj_kernel,
        out_shape=(jax.ShapeDtypeStruct((r, hs), BF16), jax.ShapeDtypeStruct((r, hs), BF16),
                   jax.ShapeDtypeStruct((r, hs), BF16)),
        grid=(r // tm,),
        in_specs=[
            pl.BlockSpec((tm, lora), lambda i: (i, cqa // lora)),
            pl.BlockSpec((tm, lora), lambda i: (i, ckva // lora)),
            pl.BlockSpec((tm, LANES), lambda i: (i, ckpe // LANES)),
            pl.BlockSpec((tm, LANES), lambda i: (jnp.where(i < n_x_tiles, i % tab_tiles, tab_tiles), 0)),
            pl.BlockSpec((1, lora), lambda i: (0, 0)),
            pl.BlockSpec((1, lora), lambda i: (0, 0)),
            pl.BlockSpec((lora, hs), lambda i: (0, 0)),
            pl.BlockSpec((lora, 2 * hv), lambda i: (0, 0)),
            pl.BlockSpec((1, MLA_SLAB), lambda i: (0, 0)),
            pl.BlockSpec((1, MLA_SLAB), lambda i: (0, 0)),
        ],
        out_specs=(pl.BlockSpec((tm, hs), lambda i: (i, 0)), pl.BlockSpec((tm, hs), lambda i: (i, 0)),
                   pl.BlockSpec((tm, hs), lambda i: (i, 0))),
        compiler_params=_cparams(("parallel",)),
        name="mla_qkv",
    )(z, z, z, tab, qag, kvag, wq, wkv, gq, gk)


def _attn_kernel(n_kv, tq, q_ref, *refs):
    k_refs = refs[:n_kv]
    v_refs = refs[n_kv:2 * n_kv]
    o_ref, s0_ref, s1_ref, m0_ref, m1_ref = refs[2 * n_kv:]
    slots = ((s0_ref, m0_ref), (s1_ref, m1_ref))
    n_tiles = q_ref.shape[0] // tq
    chunks = []
    col = 0
    for kv, k_ref in enumerate(k_refs):
        n_keys = k_ref.shape[0]
        step = min(ATTN_KEY_CHUNK, n_keys)
        for off in range(0, n_keys, step):
            chunks.append((kv, off, col, step))
            col += step

    def scores(t, slot):
        s_ref, m_ref = slots[slot]
        rows = pl.ds(pl.multiple_of(t * tq, tq), tq)
        q = q_ref[rows, :]
        run = None
        for kv, off, c0, size in chunks:
            s = lax.dot_general(q, k_refs[kv][off:off + size, :], (((1,), (1,)), ((), ())),
                                preferred_element_type=F32)
            s_ref[:, c0:c0 + size] = s
            for lb in range(size // LANES):
                blk = s[:, lb * LANES:(lb + 1) * LANES]
                run = blk if run is None else jnp.maximum(run, blk)
        m_ref[...] = run

    def finish(t, slot):
        s_ref, m_ref = slots[slot]
        rows = pl.ds(pl.multiple_of(t * tq, tq), tq)
        m = jnp.max(m_ref[...], axis=-1, keepdims=True)
        acc = None
        for kv, off, c0, size in chunks:
            p = jnp.exp2(s_ref[:, c0:c0 + size] - m).astype(BF16)
            pv = jnp.dot(p, v_refs[kv][off:off + size, :], preferred_element_type=F32)
            acc = pv if acc is None else acc + pv
        o_ref[rows, :] = (acc[:, :LANES] / acc[:, LANES:LANES + 1]).astype(BF16)

    scores(0, 0)

    def body(k, carry):
        scores(2 * k + 1, 1)
        finish(2 * k, 0)
        scores(jnp.minimum(2 * k + 2, n_tiles - 1), 0)
        finish(2 * k + 1, 1)
        return carry

    lax.fori_loop(0, n_tiles // 2, body, 0)
    if n_tiles % 2:
        finish(n_tiles - 1, 0)


def _attention(qo, ko, vo, b, s, tc, tq, latent):
    n_x = b * s
    ctx0 = n_x // tc
    cspec = pl.BlockSpec((tc, MLA_SLAB), lambda i, h: (ctx0 + i, h))
    xspec = pl.BlockSpec((s, MLA_SLAB), lambda i, h: (i, h))
    if latent:
        n_q, n_keys = s, s + tc
        in_specs = [xspec, cspec, xspec, cspec, xspec]
        args = (qo, ko, ko, vo, vo)
    else:
        n_q, n_keys = tc, tc
        in_specs = [cspec, cspec, cspec]
        args = (qo, ko, vo)
    return pl.pallas_call(
        functools.partial(_attn_kernel, (len(args) - 1) // 2, tq),
        out_shape=jax.ShapeDtypeStruct((b * n_q, MLA_HEADS * LANES), BF16),
        grid=(b, MLA_HEADS),
        in_specs=in_specs,
        out_specs=pl.BlockSpec((n_q, LANES), lambda i, h: (i, h)),
        scratch_shapes=[pltpu.VMEM((tq, n_keys), F32), pltpu.VMEM((tq, n_keys), F32),
                        pltpu.VMEM((tq, LANES), F32), pltpu.VMEM((tq, LANES), F32)],
        compiler_params=_cparams(("parallel", "parallel")),
        name="attn_latent" if latent else "attn_ctx",
    )(*args)


S5_HALF = S5_BLOCK_GROUPS * S5_STATE


def _s5_kernel(n_batch, rows_x, rows_c, dot_rows, u_ref, fac_ref, pw_ref, tab_ref, y_ref,
               u2_ref, v_ref, r_ref, m_ref, ot_ref):
    d = pl.program_id(1)
    n_dot = u2_ref.shape[0] // dot_rows

    for s in range(S5_SUB):
        blk = slice(s * LANES, (s + 1) * LANES)
        r_ref[blk, :] = (fac_ref[0, 0, 0] * pw_ref[0, 0, 0, s:s + 1, :]
                         + fac_ref[0, 0, 1] * pw_ref[0, 0, 1, s:s + 1, :]).astype(BF16)
        ot_ref[blk, :] = (fac_ref[0, 0, 2] * pw_ref[0, 0, 2, s:s + 1, :]
                          + fac_ref[0, 0, 3] * pw_ref[0, 0, 3, s:s + 1, :]).astype(BF16)

    c_t = fac_ref[0, 0, 2].astype(BF16)

    def toeplitz(reverse):
        taps = {}
        for s in range(S5_SUB):
            z = s if reverse else S5_SUB - 1 - s
            taps[z] = _dot_nt(r_ref[s * LANES:(s + 1) * LANES, :], c_t).astype(BF16)
        zero = jnp.zeros((LANES, LANES), BF16)
        for s in range(S5_SUB):
            for t in range(S5_SUB):
                lag = (s - t) if reverse else (t - s)
                m_ref[s * LANES:(s + 1) * LANES, t * LANES:(t + 1) * LANES] = taps[lag] if lag >= 0 else zero

    @pl.when(d == 0)
    def _():
        toeplitz(False)

    @pl.when(d == 1)
    def _():
        toeplitz(True)

    def dot_rows_of(i):
        return pl.ds(pl.multiple_of(i * dot_rows, dot_rows), dot_rows)

    def token_rows_of(i, s):
        return pl.ds(i * (dot_rows * S5_SUB) + s, dot_rows, stride=S5_SUB)

    @pl.when(d == 0)
    def _():
        def stage(i, carry):
            rows = pl.ds(pl.multiple_of(i * dot_rows, dot_rows), dot_rows)
            y_ref[rows, :] = u_ref[rows, :].astype(F32)
            return carry

        lax.fori_loop(0, n_dot * S5_SUB, stage, 0)

        def regroup(i, carry):
            for s in range(S5_SUB):
                u2_ref[dot_rows_of(i), s * LANES:(s + 1) * LANES] = y_ref[token_rows_of(i, s), :].astype(BF16)
            return carry

        lax.fori_loop(0, n_dot, regroup, 0)

    def increments(i, carry):
        rows = dot_rows_of(i)
        v_ref[rows, :] = jnp.dot(u2_ref[rows, :], r_ref[...], preferred_element_type=F32)
        return carry

    lax.fori_loop(0, n_dot, increments, 0)

    def cmul_add(ar, ai, cr, ci, xr, xi):
        return ar + cr * xr - ci * xi, ai + cr * xi + ci * xr

    def run(reverse):
        tab = tab_ref.at[0, 0]
        last = 0 if reverse else SUBLANES - 1
        first_row = lax.broadcasted_iota(jnp.int32, (SUBLANES, S5_HALF), 0) == (SUBLANES - 1 - last)

        def segment(bases, n_groups, carry):
            def body(i, carry):
                gi = (n_groups - 1 - i) if reverse else i
                out = []
                for base, (cre, cim) in zip(bases, carry):
                    rows = pl.ds(pl.multiple_of(base + gi * SUBLANES, SUBLANES), SUBLANES)
                    re = v_ref[rows, :S5_HALF]
                    im = v_ref[rows, S5_HALF:]
                    for lvl, shift in enumerate((1, 2, 4)):
                        sh = (SUBLANES - shift) if reverse else shift
                        re, im = cmul_add(re, im, tab[2 * lvl], tab[2 * lvl + 1],
                                          pltpu.roll(re, sh, axis=0), pltpu.roll(im, sh, axis=0))
                    re, im = cmul_add(re, im, tab[6], tab[7], cre, cim)
                    sh1 = (SUBLANES - 1) if reverse else 1
                    v_ref[rows, :S5_HALF] = jnp.where(first_row, cre, pltpu.roll(re, sh1, axis=0))
                    v_ref[rows, S5_HALF:] = jnp.where(first_row, cim, pltpu.roll(im, sh1, axis=0))
                    out.append((jnp.broadcast_to(re[last:last + 1, :], re.shape),
                                jnp.broadcast_to(im[last:last + 1, :], im.shape)))
                return tuple(out)
            return lax.fori_loop(0, n_groups, body, carry)

        zero = jnp.zeros((SUBLANES, S5_HALF), F32)
        carry = tuple((zero, zero) for _ in range(n_batch))
        carry = segment([n_batch * rows_x + bi * rows_c for bi in range(n_batch)], rows_c // SUBLANES, carry)
        segment([bi * rows_x for bi in range(n_batch)], rows_x // SUBLANES, carry)

        def outputs(i, carry):
            rows = dot_rows_of(i)
            y = jnp.dot(u2_ref[rows, :], m_ref[...], preferred_element_type=F32)
            y = y + lax.dot_general(v_ref[rows, :].astype(BF16), ot_ref[...], (((1,), (1,)), ((), ())),
                                    preferred_element_type=F32)
            for s in range(S5_SUB):
                part = y[:, s * LANES:(s + 1) * LANES]
                if reverse:
                    y_ref[token_rows_of(i, s), :] += part
                else:
                    y_ref[token_rows_of(i, s), :] = part
            return carry

        lax.fori_loop(0, n_dot, outputs, 0)

    @pl.when(d == 0)
    def _():
        run(False)

    @pl.when(d == 1)
    def _():
        run(True)


def _s5_scan(z, fac, pw, tabs, layer, b, s, tc, col_u):
    r = z.shape[0]
    rc = r // S5_SUB
    n_blocks = fac.shape[2]
    width = S5_SUB * LANES
    assert width == 2 * S5_HALF
    dot_rows = max(n for n in range(16, 641, 16) if rc % n == 0)
    return pl.pallas_call(
        functools.partial(_s5_kernel, b, s // S5_SUB, tc // S5_SUB, dot_rows),
        out_shape=jax.ShapeDtypeStruct((r, n_blocks * LANES), F32),
        grid=(n_blocks, 2),
        in_specs=[
            pl.BlockSpec((r, LANES), lambda cb, d: (0, col_u // LANES + cb), pipeline_mode=pl.Buffered(1)),
            pl.BlockSpec((None, 1, 1, 4, LANES, width), lambda cb, d: (layer, d, cb, 0, 0, 0),
                         pipeline_mode=pl.Buffered(1)),
            pl.BlockSpec((None, 1, 1, 4, S5_SUB, width), lambda cb, d: (layer, d, cb, 0, 0, 0)),
            pl.BlockSpec((None, 1, 1, 8, SUBLANES, S5_HALF), lambda cb, d: (layer, d, cb, 0, 0, 0)),
        ],
        out_specs=pl.BlockSpec((r, LANES), lambda cb, d: (0, cb)),
        scratch_shapes=[pltpu.VMEM((rc, width), BF16), pltpu.VMEM((rc, width), F32),
                        pltpu.VMEM((width, width), BF16), pltpu.VMEM((width, width), BF16),
                        pltpu.VMEM((width, width), BF16)],
        compiler_params=_cparams(("parallel", "arbitrary")),
        name="s5_scan",
    )(z, fac, pw, tabs)


def _glu_kernel(y_ref, u_ref, d_ref, w_ref, b_ref, o_ref, g_ref):
    tm = y_ref.shape[0]

    def body(r, carry):
        rows = pl.ds(pl.multiple_of(r * ROW_CHUNK, ROW_CHUNK), ROW_CHUNK)
        y = y_ref[rows, :] + d_ref[...] * u_ref[rows, :].astype(F32)
        g_ref[rows, :] = jax.nn.gelu(y).astype(BF16)
        return carry

    lax.fori_loop(0, tm // ROW_CHUNK, body, 0)
    g = g_ref[...]
    gate = jax.nn.sigmoid(jnp.dot(g, w_ref[...], preferred_element_type=F32) + b_ref[...])
    o_ref[...] = (g.astype(F32) * gate).astype(BF16)


def _s5_glu(y, z, d_skip, w_glu, b_glu, tm, col_u, n_row_tiles):
    width = y.shape[1]
    return pl.pallas_call(
        _glu_kernel,
        out_shape=jax.ShapeDtypeStruct((n_row_tiles * tm, width), BF16),
        grid=(n_row_tiles,),
        in_specs=[
            pl.BlockSpec((tm, width), lambda i: (i, 0)),
            pl.BlockSpec((tm, width), lambda i: (i, col_u // width)),
            pl.BlockSpec((1, width), lambda i: (0, 0)),
            pl.BlockSpec((width, width), lambda i: (0, 0)),
            pl.BlockSpec((1, width), lambda i: (0, 0)),
        ],
        out_specs=pl.BlockSpec((tm, width), lambda i: (i, 0)),
        scratch_shapes=[pltpu.VMEM((tm, width), BF16)],
        compiler_params=_cparams(("parallel",)),
        name="s5_glu",
    )(y, z, d_skip, w_glu, b_glu)


def _merge_kernel(n_x_tiles, ax_ref, ac_ref, bx_ref, bc_ref, c_ref, ga_ref, gb_ref, gc_ref, w_ref, o_ref):
    def combine(a_ref, b_ref):
        acc = None
        for r, (br_ref, gate_ref) in enumerate(((a_ref, ga_ref), (b_ref, gb_ref), (c_ref, gc_ref))):
            proj = jnp.dot(br_ref[...], w_ref[r], preferred_element_type=F32)
            term = jax.nn.sigmoid(gate_ref[...].astype(F32)) * proj
            acc = term if acc is None else acc + term
        o_ref[...] = acc.astype(BF16)

    @pl.when(pl.program_id(0) < n_x_tiles)
    def _():
        combine(ax_ref, bx_ref)

    @pl.when(pl.program_id(0) >= n_x_tiles)
    def _():
        combine(ac_ref, bc_ref)


def _merge(a_x, a_c, b_x, b_c, cc, z, w_branch, tm, tn, col_g, n_row_tiles, n_x_tiles):
    width = a_x.shape[1]
    d = w_branch.shape[2]

    def gate_spec(r):
        return pl.BlockSpec((tm, tn), lambda i, j: (i, (col_g + r * d) // tn + j))

    ax_spec, ac_spec = _stream_specs((tm, width), n_x_tiles, 0, lambda j: 0)
    return pl.pallas_call(
        functools.partial(_merge_kernel, n_x_tiles),
        out_shape=jax.ShapeDtypeStruct((n_row_tiles * tm, d), BF16),
        grid=(n_row_tiles, d // tn),
        in_specs=[
            ax_spec, ac_spec, ax_spec, ac_spec,
            pl.BlockSpec((tm, width), lambda i, j: (i, 0)),
            gate_spec(0), gate_spec(1), gate_spec(2),
            pl.BlockSpec((N_BRANCH, width, tn), lambda i, j: (0, 0, j)),
        ],
        out_specs=pl.BlockSpec((tm, tn), lambda i, j: (i, j)),
        compiler_params=_cparams(("parallel", "arbitrary")),
        name="merge",
    )(a_x, a_c, b_x, b_c, cc, z, z, z, w_branch)


def _resid_kernel(n_x_tiles, m_ref, w_ref, x_ref, c_ref, al_ref, o_ref):
    resid = jnp.where(pl.program_id(0) < n_x_tiles, x_ref[...], c_ref[...])
    o_ref[...] = resid + al_ref[0] * jnp.dot(m_ref[...], w_ref[...], preferred_element_type=F32)


def _out_proj_residual(m, w, x_src, c_src, ctx_tile0, mod3, tm, tn, n_row_tiles, n_x_tiles, tiles_per_batch, ctx_row,
                       k_alpha):
    kdim, d = w.shape
    nt = d // tn

    def alpha_map(i, j):
        r = jnp.where(i < n_x_tiles, i // tiles_per_batch, ctx_row)
        return (r * N_MOD + k_alpha, 0, j)

    x_spec = pl.BlockSpec((tm, tn), lambda i, j: (jnp.minimum(i, n_x_tiles - 1), jnp.where(i < n_x_tiles, j, nt - 1)))
    c_spec = pl.BlockSpec((tm, tn), lambda i, j: (ctx_tile0 + jnp.maximum(i - n_x_tiles, 0),
                                                  jnp.where(i < n_x_tiles, 0, j)))
    return pl.pallas_call(
        functools.partial(_resid_kernel, n_x_tiles),
        out_shape=jax.ShapeDtypeStruct((n_row_tiles * tm, d), F32),
        grid=(n_row_tiles, nt),
        in_specs=[
            pl.BlockSpec((tm, kdim), lambda i, j: (i, 0)),
            pl.BlockSpec((kdim, tn), lambda i, j: (0, j)),
            x_spec, c_spec,
            pl.BlockSpec((1, 1, tn), alpha_map),
        ],
        out_specs=pl.BlockSpec((tm, tn), lambda i, j: (i, j)),
        compiler_params=_cparams(("parallel", "arbitrary")),
        name="out_proj",
    )(m, w, x_src, c_src, mod3)


def _ff1_kernel(x_ref, g_ref, sh_ref, sc_ref, w_ref, h_ref, xn_ref):
    @pl.when(pl.program_id(1) == 0)
    def _():
        _norm_mod_rows(x_ref, g_ref, sh_ref, sc_ref, xn_ref)

    a = jnp.maximum(jnp.dot(xn_ref[...], w_ref[...], preferred_element_type=F32), 0.0)
    h_ref[...] = (a * a).astype(BF16)


def _ff1(xs, g, mod3, w, tm, tn, n_row_tiles, n_x_tiles, tiles_per_batch, ctx_row):
    d, dff = w.shape
    return pl.pallas_call(
        _ff1_kernel,
        out_shape=jax.ShapeDtypeStruct((n_row_tiles * tm, dff), BF16),
        grid=(n_row_tiles, dff // tn),
        in_specs=[
            pl.BlockSpec((tm, d), lambda i, j: (i, 0)),
            pl.BlockSpec((1, d), lambda i, j: (0, 0)),
            pl.BlockSpec((1, 1, d), _mod_row_map(n_x_tiles, tiles_per_batch, ctx_row, 3)),
            pl.BlockSpec((1, 1, d), _mod_row_map(n_x_tiles, tiles_per_batch, ctx_row, 4)),
            pl.BlockSpec((d, tn), lambda i, j: (0, j)),
        ],
        out_specs=pl.BlockSpec((tm, tn), lambda i, j: (i, j)),
        scratch_shapes=[pltpu.VMEM((tm, d), BF16)],
        compiler_params=_cparams(("parallel", "arbitrary")),
        name="ff1",
    )(xs, g, mod3, mod3, w)


def _ff2_kernel(h_ref, w_ref, x_ref, al_ref, o_ref):
    k = pl.program_id(2)
    part = jnp.dot(h_ref[...], w_ref[...], preferred_element_type=F32)

    @pl.when(k == 0)
    def _():
        o_ref[...] = part

    @pl.when(k > 0)
    def _():
        o_ref[...] += part

    @pl.when(k == pl.num_programs(2) - 1)
    def _():
        o_ref[...] = x_ref[...] + al_ref[0] * o_ref[...]


def _ff2(h, w, xs, mod3, tm, tn, tk, n_row_tiles, n_x_tiles, tiles_per_batch, ctx_row):
    dff, d = w.shape

    def alpha_map(i, j, k):
        r = jnp.where(i < n_x_tiles, i // tiles_per_batch, ctx_row)
        return (r * N_MOD + 5, 0, j)

    return pl.pallas_call(
        _ff2_kernel,
        out_shape=jax.ShapeDtypeStruct((n_row_tiles * tm, d), F32),
        grid=(n_row_tiles, d // tn, dff // tk),
        in_specs=[
            pl.BlockSpec((tm, tk), lambda i, j, k: (i, k)),
            pl.BlockSpec((tk, tn), lambda i, j, k: (k, j)),
            pl.BlockSpec((tm, tn), lambda i, j, k: (i, j)),
            pl.BlockSpec((1, 1, tn), alpha_map),
        ],
        out_specs=pl.BlockSpec((tm, tn), lambda i, j, k: (i, j)),
        compiler_params=_cparams(("parallel", "parallel", "arbitrary")),
        name="ff2",
    )(h, w, xs, mod3)


def _rope_partner():
    j = np.arange(MLA_ROPE)
    quarter = MLA_ROPE // 4
    return np.where((j // quarter) % 2 == 0, j + quarter, j - quarter)


def _rope_table(s, tm):
    pos = jnp.arange(s)
    row = (pos // GRID_W).astype(F32)
    col = (pos % GRID_W).astype(F32)
    n_freq = MLA_ROPE // 4
    inv_freq = ROPE_THETA ** (-jnp.arange(n_freq, dtype=F32) / n_freq)
    ang_r = row[:, None] * inv_freq
    ang_c = col[:, None] * inv_freq
    cos = jnp.concatenate([jnp.cos(ang_r)] * 2 + [jnp.cos(ang_c)] * 2, axis=-1)
    sin = jnp.concatenate([-jnp.sin(ang_r), jnp.sin(ang_r), -jnp.sin(ang_c), jnp.sin(ang_c)], axis=-1)
    ident = jnp.concatenate([jnp.ones((tm, MLA_ROPE), F32), jnp.zeros((tm, MLA_ROPE), F32)], axis=-1)
    return jnp.concatenate([jnp.concatenate([cos, sin], axis=-1), ident], axis=0)


def _cast_kernel(w_ref, o_ref):
    o_ref[...] = w_ref[...].astype(BF16)


def _layer_bf16(w_all, layer):
    rows, cols = w_all.shape[1:]
    tr = CAST_BLOCK_BYTES // (4 * cols)
    while rows % tr:
        tr //= 2
    return pl.pallas_call(
        _cast_kernel,
        out_shape=jax.ShapeDtypeStruct((rows, cols), BF16),
        grid=(rows // tr,),
        in_specs=[pl.BlockSpec((None, tr, cols), lambda i: (layer, i, 0))],
        out_specs=pl.BlockSpec((tr, cols), lambda i: (i, 0)),
        compiler_params=_cparams(("parallel",)),
        name="cast_bf16",
    )(w_all)


def _pack_w_in(w_in, b_in, gate_b, sizes, n_pad):
    w_t = w_in.T
    d_model = w_in.shape[0]
    bounds = np.cumsum((0,) + sizes)
    seg = [slice(int(bounds[i]), int(bounds[i + 1])) for i in range(len(sizes))]
    partner = _rope_partner()
    order = (0, 1, 2, 3, 5, 6, 8, 9)
    names = ("q", "k", "v", "o", "qa", "kva", "u", "gates")
    offs, pos = {}, 0
    for name, i in zip(names, order):
        offs[name] = pos
        pos += sizes[i]
    offs["kpe"], offs["kpe_sw"] = pos, pos + MLA_ROPE
    n_zero = n_pad - pos - 2 * MLA_ROPE
    kpe_w, kpe_b = w_t[seg[7]], b_in[seg[7]]
    w = jnp.concatenate([w_t[seg[i]] for i in order] + [kpe_w, kpe_w[partner], jnp.zeros((n_zero, d_model), F32)],
                        axis=0).astype(BF16)
    b = jnp.concatenate([b_in[seg[i]] for i in order] + [kpe_b, kpe_b[partner], jnp.zeros((n_zero,), F32)])
    n_g = sizes[4]
    wg = jnp.concatenate([w_t[seg[4]], jnp.zeros((LANES - n_g, d_model), F32)], axis=0).astype(BF16)
    bg = jnp.concatenate([b_in[seg[4]] + gate_b.reshape(-1), jnp.zeros((LANES - n_g,), F32)])[None, :]
    return w, b[None, :], wg, bg, offs


def _pack_mla(w_uq, w_ukv, qn_g, kn_g):
    partner = _rope_partner()
    lora = w_uq.shape[0]
    wq = w_uq.reshape(lora, MLA_HEADS, MLA_DQK)
    wq = jnp.concatenate([wq, wq[:, :, MLA_NOPE + partner]], axis=-1).reshape(lora, MLA_HEADS * MLA_SLAB)
    wkv = w_ukv.reshape(w_ukv.shape[0], MLA_HEADS, -1)
    wkv = jnp.concatenate([wkv[:, :, :MLA_NOPE].reshape(lora, -1), wkv[:, :, MLA_NOPE:].reshape(lora, -1)], axis=-1)

    def gains(g):
        return jnp.concatenate([g, g[MLA_NOPE + partner]])[None, :]

    return wq.astype(BF16), wkv.astype(BF16), gains(qn_g), gains(kn_g)


def _pack_s5(a_re, a_im, log_dt, b_re, b_im, c_re, c_im):
    n_dir, n_groups, n_state = a_re.shape
    gc = b_re.shape[-1]
    nb = n_groups // S5_BLOCK_GROUPS
    lam_re = jnp.minimum(a_re.astype(F32), -1e-4)
    lam_im = a_im.astype(F32)
    dt = jnp.exp(log_dt.astype(F32))[..., None]

    def pole_power(k):
        mag = jnp.exp(k * lam_re * dt)
        return mag * jnp.cos(k * lam_im * dt), mag * jnp.sin(k * lam_im * dt)

    bar_re, bar_im = pole_power(1.0)
    den = lam_re * lam_re + lam_im * lam_im
    f_re = ((bar_re - 1.0) * lam_re + bar_im * lam_im) / den
    f_im = (bar_im * lam_re - (bar_re - 1.0) * lam_im) / den
    bb_re = f_re[..., None] * b_re.astype(F32) - f_im[..., None] * b_im.astype(F32)
    bb_im = f_re[..., None] * b_im.astype(F32) + f_im[..., None] * b_re.astype(F32)
    eye = jnp.eye(S5_BLOCK_GROUPS, dtype=F32)
    sub = S5_SUB

    def per_block(a):
        return a.reshape(a.shape[:-2] + (nb, S5_BLOCK_GROUPS * n_state))

    def block_b(part):
        p = part.reshape(n_dir, nb, S5_BLOCK_GROUPS, n_state, gc)
        m = jnp.einsum('dbgnc,gh->dbgchn', p, eye, precision=HIGHEST)
        return m.reshape(n_dir, nb, LANES, S5_HALF)

    def block_c(part):
        p = part.astype(F32).reshape(n_dir, nb, S5_BLOCK_GROUPS, gc, n_state)
        m = jnp.einsum('dbgcn,gh->dbgnhc', p, eye, precision=HIGHEST)
        return m.reshape(n_dir, nb, S5_HALF, LANES)

    bm_re, bm_im = block_b(bb_re), block_b(bb_im)
    cm_re, cm_im = block_c(c_re), block_c(c_im)

    tau = jnp.arange(sub + 1, dtype=F32)[:, None, None, None]
    p_re, p_im = (per_block(p) for p in pole_power(tau))
    ct_re, ct_im = jnp.swapaxes(cm_re, -1, -2), jnp.swapaxes(cm_im, -1, -2)
    cat = functools.partial(jnp.concatenate, axis=-1)
    fac = jnp.stack([cat([bm_re, bm_im]), cat([-bm_im, bm_re]), cat([ct_re, -ct_im]), cat([-ct_im, -ct_re])],
                    axis=2)
    t = np.arange(sub)
    pws = []
    for d in range(n_dir):
        to_exit = (sub - 1 - t) if d == 0 else t
        age = (t + 1) if d == 0 else (sub - t)
        rows = [p_re[to_exit, d], p_im[to_exit, d], p_re[age, d], p_im[age, d]]
        pws.append(jnp.stack([cat([a, a]).transpose(1, 0, 2) for a in rows], axis=1))
    pw = jnp.stack(pws)

    rows = jnp.arange(SUBLANES)
    tabs = []
    for d in range(n_dir):
        per_dir = []
        for shift in (1, 2, 4):
            keep = ((rows <= SUBLANES - 1 - shift) if d == 1 else (rows >= shift))[None, :, None]
            s_re, s_im = pole_power(float(shift * sub))
            per_dir += [jnp.where(keep, per_block(s_re[d])[:, None, :], 0.0),
                        jnp.where(keep, per_block(s_im[d])[:, None, :], 0.0)]
        expo = (((SUBLANES - rows) if d == 1 else (rows + 1)) * sub).astype(F32)
        s_re, s_im = pole_power(expo[:, None, None, None])
        per_dir += [jnp.moveaxis(per_block(s_re[:, d]), 0, 1), jnp.moveaxis(per_block(s_im[:, d]), 0, 1)]
        tabs.append(jnp.stack(per_dir, axis=1))
    return fac, pw, jnp.stack(tabs).astype(F32)


def _tri_matrices():
    t = np.arange(ML_CHUNK)
    lower = t[None, :] <= t[:, None]
    mask = np.where(np.stack([lower, lower.T]), 0.0, NEG_BIG).astype(np.float32)
    return jnp.asarray(lower.astype(np.float32), dtype=BF16), jnp.asarray(mask)


def kernel(x, c, ctx, c_ctx, w_mod, b_mod, norm_g, w_in, b_in, ml_gate_b, ml_norm_g, mla_qa_g, mla_kva_g, mla_w_uq, mla_w_ukv, mla_qn_g, mla_kn_g, s5_a_re, s5_a_im, s5_log_dt, s5_b_re, s5_b_im, s5_c_re, s5_c_im, s5_d, s5_w_glu, s5_b_glu, w_branch, w_out, w_ff1, w_ff2):
    b, s, d = x.shape
    tc = ctx.shape[1]
    depth = w_mod.shape[0]
    dv = ml_norm_g.shape[2]
    dk = dv // 2
    lora = mla_qa_g.shape[1]
    s5_width = s5_d.shape[1]
    branch_w = w_branch.shape[2]
    sizes = (ML_HEADS * dk, ML_HEADS * dk, ML_HEADS * dv, ML_HEADS * dv, 4 * ML_HEADS, lora, lora, MLA_ROPE,
             s5_width, N_BRANCH * d)
    assert sum(sizes) == w_in.shape[2] and b + 1 <= SUBLANES
    assert s % ML_CHUNK == 0 and tc % ML_CHUNK == 0 and branch_w == ML_HEADS * dv == MLA_HEADS * LANES == s5_width

    n_x = b * s
    n_c = b * tc
    tm = _row_tile(s, n_c)
    n_x_tiles = n_x // tm
    n_tiles = n_x_tiles + n_c // tm
    tiles_per_batch = s // tm
    tile_args = (n_x_tiles, tiles_per_batch, b)

    x_src, c_src, ctx_tile0 = x.reshape(n_x, d), ctx.reshape(n_c, d), 0
    cc = jnp.concatenate([c, c_ctx[None, :], jnp.zeros((SUBLANES - b - 1, d), F32)], axis=0)
    mod = _modulation(cc, w_mod, b_mod)
    tm_q = min(tm, 512)
    tab = _rope_table(s, tm_q)
    tri, ml_mask = _tri_matrices()
    s5_packed = jax.vmap(_pack_s5)(s5_a_re, s5_a_im, s5_log_dt, s5_b_re, s5_b_im, s5_c_re, s5_c_im)
    n_used = sum(sizes) - sizes[4] + MLA_ROPE
    tn_in = 1280
    n_pad = -(-n_used // tn_in) * tn_in

    for l in range(depth):
        with_ctx_out = l < depth - 1
        mod3 = mod[l].reshape(SUBLANES * N_MOD, 1, d)
        w_p, b_p, wg, bg, offs = _pack_w_in(w_in[l], b_in[l], ml_gate_b[l], sizes, n_pad)
        z, gz = _in_proj(x_src, c_src, ctx_tile0, n_tiles, norm_g[l, 0][None, :], mod3, w_p, b_p, wg, bg, tm, tn_in,
                         *tile_args)

        gates, gates_t = _gate_prep(gz, tri)
        a_x, a_c = _mlstm(z, gates, gates_t, ml_mask, ml_norm_g[l], b, s, tc, dk, dv,
                          (offs["q"], offs["k"], offs["v"], offs["o"]), with_ctx_out)

        wq, wkv, gq, gk = _pack_mla(mla_w_uq[l], mla_w_ukv[l], mla_qn_g[l], mla_kn_g[l])
        qo, ko, vo = _mla_proj(z, tab, mla_qa_g[l][None, :], mla_kva_g[l][None, :], wq, wkv, gq, gk, tm_q,
                               (offs["qa"], offs["kva"], offs["kpe"]), n_x // tm_q, s // tm_q)
        tq = min(512, s)
        b_x = _attention(qo, ko, vo, b, s, tc, tq, True)

        y = _s5_scan(z, *s5_packed, l, b, s, tc, offs["u"])
        n_out_tiles = n_tiles if with_ctx_out else n_x_tiles
        c_all = _s5_glu(y, z, s5_d[l][None, :], s5_w_glu[l].astype(BF16), s5_b_glu[l][None, :], tm, offs["u"],
                        n_out_tiles)

        if with_ctx_out:
            b_c = _attention(qo, ko, vo, b, s, tc, min(tq, tc), False)
        else:
            a_c, b_c = a_x, b_x
        w_br = _layer_bf16(w_branch.reshape(depth, N_BRANCH * branch_w, d), l).reshape(N_BRANCH, branch_w, d)
        merged = _merge(a_x, a_c, b_x, b_c, c_all, z, w_br, tm, 512, offs["gates"],
                        n_out_tiles, n_x_tiles)
        xs1 = _out_proj_residual(merged, _layer_bf16(w_out, l), x_src, c_src, ctx_tile0, mod3, tm, 1024,
                                 n_out_tiles, *tile_args, 2)
        hid = _ff1(xs1, norm_g[l, 1][None, :], mod3, _layer_bf16(w_ff1, l), tm, 1024, n_out_tiles, *tile_args)
        xs = _ff2(hid, _layer_bf16(w_ff2, l), xs1, mod3, tm, 1024, 2048, n_out_tiles, *tile_args)
        x_src, c_src, ctx_tile0 = xs, xs, n_x_tiles

    return xs.reshape(b, s, d)
```

```python
import functools
import math

import jax
import jax.numpy as jnp
import numpy as np
from jax import lax
from jax.experimental import pallas as pl
from jax.experimental.pallas import tpu as pltpu

F32 = jnp.float32
BF16 = jnp.bfloat16
HIGHEST = lax.Precision.HIGHEST

N_MOD = 6
N_BRANCH = 3
ML_HEADS = 4
MLA_HEADS = 8
MLA_NOPE = 128
MLA_ROPE = 64
MLA_DQK = MLA_NOPE + MLA_ROPE
MLA_SLAB = 256
GRID_W = 64
ROPE_THETA = 10000.0
S5_GROUP = 16
S5_STATE = 64
S5_BLOCK_GROUPS = 8
EPS = 1e-6
NEG_BIG = -1e30

LANES = 128
SUBLANES = 8
VMEM_LIMIT = 56 * 1024 * 1024

ML_CHUNK = 256
ML_GATE_ROWS = 32
S5_SUB = 8
ATTN_KEY_CHUNK = 512
Q_PRESCALE = MLA_DQK ** -0.5 * math.log2(math.e)
ROW_CHUNK = 128
CAST_BLOCK_BYTES = 8 * 1024 * 1024


def _cparams(sem):
    return pltpu.CompilerParams(dimension_semantics=sem, vmem_limit_bytes=VMEM_LIMIT)


def _row_tile(n_x_rows_per_batch, n_ctx_rows):
    tm = 1024
    while n_x_rows_per_batch % tm or n_ctx_rows % tm:
        tm //= 2
    return tm


def _mod_kernel(c_ref, w_ref, b_ref, o_ref):
    s = c_ref[...]
    s = s * jax.nn.sigmoid(s)
    o_ref[0] = jnp.dot(s.astype(BF16), w_ref[0].astype(BF16), preferred_element_type=F32) + b_ref[0]


def _modulation(cc, w_mod, b_mod):
    n_layers, d, n = w_mod.shape
    tn = 1024
    return pl.pallas_call(
        _mod_kernel,
        out_shape=jax.ShapeDtypeStruct((n_layers, SUBLANES, n), F32),
        grid=(n_layers, n // tn),
        in_specs=[
            pl.BlockSpec((SUBLANES, d), lambda l, j: (0, 0)),
            pl.BlockSpec((1, d, tn), lambda l, j: (l, 0, j)),
            pl.BlockSpec((1, 1, tn), lambda l, j: (l, 0, j)),
        ],
        out_specs=pl.BlockSpec((1, SUBLANES, tn), lambda l, j: (l, 0, j)),
        compiler_params=_cparams(("parallel", "parallel")),
        name="adaln_mod",
    )(cc, w_mod, b_mod.reshape(n_layers, 1, n))


def _norm_mod_rows(x_ref, g_ref, sh_ref, sc_ref, xn_ref):
    tm = x_ref.shape[0]
    g = g_ref[...]
    sc = 1.0 + sc_ref[0]
    sh = sh_ref[0]

    def body(r, carry):
        rows = pl.ds(pl.multiple_of(r * ROW_CHUNK, ROW_CHUNK), ROW_CHUNK)
        x = x_ref[rows, :]
        ms = jnp.mean(x * x, axis=-1, keepdims=True)
        y = x * lax.rsqrt(ms + EPS) * g
        xn_ref[rows, :] = (y * sc + sh).astype(BF16)
        return carry

    lax.fori_loop(0, tm // ROW_CHUNK, body, 0)


def _mod_row_map(n_x_tiles, tiles_per_batch, ctx_row, k):
    def index_map(i, j):
        r = jnp.where(i < n_x_tiles, i // tiles_per_batch, ctx_row)
        return (r * N_MOD + k, 0, 0)
    return index_map


def _stream_specs(block, n_x_tiles, ctx_tile0, col_map):
    x_spec = pl.BlockSpec(block, lambda i, *r: (jnp.minimum(i, n_x_tiles - 1), col_map(*r)))
    c_spec = pl.BlockSpec(block, lambda i, *r: (ctx_tile0 + jnp.maximum(i - n_x_tiles, 0), col_map(*r)),
                          pipeline_mode=pl.Buffered(1))
    return x_spec, c_spec


def _dot_nt(a, b_t):
    return lax.dot_general(a, b_t, (((1,), (1,)), ((), ())), preferred_element_type=F32)


def _in_kernel(n_x_tiles, x_ref, c_ref, g_ref, sh_ref, sc_ref, w_ref, b_ref, wg_ref, bg_ref, z_ref, gz_ref, xn_ref):
    @pl.when(pl.program_id(1) == 0)
    def _():
        @pl.when(pl.program_id(0) < n_x_tiles)
        def _():
            _norm_mod_rows(x_ref, g_ref, sh_ref, sc_ref, xn_ref)

        @pl.when(pl.program_id(0) >= n_x_tiles)
        def _():
            _norm_mod_rows(c_ref, g_ref, sh_ref, sc_ref, xn_ref)

        gz_ref[...] = _dot_nt(xn_ref[...], wg_ref[...]) + bg_ref[...]

    z_ref[...] = (_dot_nt(xn_ref[...], w_ref[...]) + b_ref[...]).astype(BF16)


def _in_proj(x_src, c_src, ctx_tile0, n_tiles, g, mod3, w, b, wg, bg, tm, tn, n_x_tiles, tiles_per_batch, ctx_row):
    d = x_src.shape[1]
    r = n_tiles * tm
    nz = w.shape[0]
    x_spec, c_spec = _stream_specs((tm, d), n_x_tiles, ctx_tile0, lambda j: 0)
    return pl.pallas_call(
        functools.partial(_in_kernel, n_x_tiles),
        out_shape=(jax.ShapeDtypeStruct((r, nz), BF16), jax.ShapeDtypeStruct((r, LANES), F32)),
        grid=(n_tiles, nz // tn),
        in_specs=[
            x_spec, c_spec,
            pl.BlockSpec((1, d), lambda i, j: (0, 0)),
            pl.BlockSpec((1, 1, d), _mod_row_map(n_x_tiles, tiles_per_batch, ctx_row, 0)),
            pl.BlockSpec((1, 1, d), _mod_row_map(n_x_tiles, tiles_per_batch, ctx_row, 1)),
            pl.BlockSpec((tn, d), lambda i, j: (j, 0)),
            pl.BlockSpec((1, tn), lambda i, j: (0, j)),
            pl.BlockSpec((LANES, d), lambda i, j: (0, 0)),
            pl.BlockSpec((1, LANES), lambda i, j: (0, 0)),
        ],
        out_specs=(
            pl.BlockSpec((tm, tn), lambda i, j: (i, j)),
            pl.BlockSpec((tm, LANES), lambda i, j: (i, 0)),
        ),
        scratch_shapes=[pltpu.VMEM((tm, d), BF16)],
        compiler_params=_cparams(("parallel", "arbitrary")),
        name="in_proj",
    )(x_src, c_src, g, mod3, mod3, w, b, wg, bg)


def _log_sigmoid(x):
    return jnp.minimum(x, 0.0) - jnp.log1p(jnp.exp(-jnp.abs(x)))


def _split3(a):
    hi = a.astype(BF16)
    r1 = a - hi.astype(F32)
    mid = r1.astype(BF16)
    lo = (r1 - mid.astype(F32)).astype(BF16)
    return hi, mid, lo


def _gate_prep_kernel(g_ref, tri_ref, a_ref, at_ref):
    g = g_ref[...]
    kind = lax.broadcasted_iota(jnp.int32, g.shape, 1) // ML_HEADS
    lg = jnp.where((kind == 1) | (kind == 3), _log_sigmoid(g), g)
    pre = sum(jnp.dot(tri_ref[...], p, preferred_element_type=F32) for p in _split3(lg))
    suf = pre[ML_CHUNK - 1:ML_CHUNK, :] - pre + lg
    a = jnp.where(kind == 4, pltpu.roll(pre, 3 * ML_HEADS, axis=1),
                  jnp.where(kind == 5, pltpu.roll(suf, 2 * ML_HEADS, axis=1), lg))
    a_ref[...] = a
    at_ref[0] = a.T[:ML_GATE_ROWS, :]


def _gate_prep(gz, tri):
    r = gz.shape[0]
    return pl.pallas_call(
        _gate_prep_kernel,
        out_shape=(jax.ShapeDtypeStruct((r, LANES), F32),
                   jax.ShapeDtypeStruct((r // ML_CHUNK, ML_GATE_ROWS, ML_CHUNK), F32)),
        grid=(r // ML_CHUNK,),
        in_specs=[pl.BlockSpec((ML_CHUNK, LANES), lambda i: (i, 0)),
                  pl.BlockSpec((ML_CHUNK, ML_CHUNK), lambda i: (0, 0))],
        out_specs=(pl.BlockSpec((ML_CHUNK, LANES), lambda i: (i, 0)),
                   pl.BlockSpec((1, ML_GATE_ROWS, ML_CHUNK), lambda i: (i, 0, 0))),
        compiler_params=_cparams(("parallel",)),
        name="mlstm_gates",
    )(gz, tri)


def _ml_chunk(q, k_t, v, cum_cb, li_r, lf_r, cum_r, mask_add, carry, inv_scale):
    c_mat, n_mat, m = carry
    length, dv = v.shape
    crow = cum_r - li_r
    total = jnp.sum(lf_r, axis=-1, keepdims=True)
    ones = jnp.ones((length, LANES), BF16)
    n_blk = length // LANES

    def wide(a):
        return jnp.concatenate([a] * (dv // LANES), axis=1)

    log_w = [cum_cb - crow[:, j * LANES:(j + 1) * LANES] + mask_add[:, j * LANES:(j + 1) * LANES]
             for j in range(n_blk)]
    row_max = jnp.max(functools.reduce(jnp.maximum, log_w), axis=-1, keepdims=True)
    log_inter = cum_cb + m
    m_t = jnp.maximum(log_inter, row_max)
    w_inter = jnp.exp(log_inter - m_t)
    qk = jnp.dot(q, k_t, preferred_element_type=F32)
    s = jnp.concatenate([qk[:, j * LANES:(j + 1) * LANES] * jnp.exp(log_w[j] - m_t) for j in range(n_blk)],
                        axis=1).astype(BF16)
    num = wide(w_inter) * jnp.dot(q, c_mat.astype(BF16), preferred_element_type=F32)
    num = num + jnp.dot(s, v, preferred_element_type=F32)
    den = w_inter * jnp.dot(q, n_mat.astype(BF16), preferred_element_type=F32)
    den = den + jnp.dot(s, ones, preferred_element_type=F32)
    h = num * wide(1.0 / jnp.maximum(jnp.abs(den), jnp.exp(-m_t) * inv_scale))

    log_end = total - crow
    m_new = jnp.maximum(total + m, jnp.max(log_end, axis=-1, keepdims=True))
    decay = jnp.exp(total + m - m_new)
    kw_t = (k_t.astype(F32) * jnp.exp(log_end - m_new)).astype(BF16)
    c_new = decay * c_mat + jnp.dot(kw_t, v, preferred_element_type=F32)
    n_new = decay * n_mat + jnp.dot(kw_t, ones, preferred_element_type=F32)
    return h, (c_new, n_new, m_new)


def _mlstm_kernel(with_ctx_out, qx, kx, vx, ox, qc, kc, vc, oc, gx, gc, gtx, gtc, mask_ref, ng_ref, *rest):
    if with_ctx_out:
        ax_ref, ac_ref, cumx, ktx, cumc, ktc, hx, hc = rest
    else:
        ax_ref, cumx, ktx, cumc, ktc, hx, hc = rest
        ac_ref = None
    head = pl.program_id(1)
    dk = qx.shape[1]
    inv_scale = float(dk) ** 0.5
    n_x_chunks = qx.shape[0] // ML_CHUNK
    n_c_chunks = qc.shape[0] // ML_CHUNK

    def prep(g_ref, k_ref, cum_ref, kt_ref):
        def body(i, carry):
            rows = pl.ds(pl.multiple_of(i * ML_CHUNK, ML_CHUNK), ML_CHUNK)
            a = g_ref[rows, :]
            col = lax.broadcasted_iota(jnp.int32, a.shape, 1)
            for d in range(2):
                pick = jnp.sum(jnp.where(col == (4 + d) * ML_HEADS + head, a, 0.0), axis=-1, keepdims=True)
                cum_ref[d, rows, :] = jnp.broadcast_to(pick, a.shape)
            kt_ref[i] = k_ref[rows, :].astype(F32).T.astype(BF16)
            return carry
        n_chunks = g_ref.shape[0] // ML_CHUNK
        lax.fori_loop(0, n_chunks, body, 0, unroll=2 if n_chunks % 2 == 0 else 1)

    prep(gx, kx, cumx, ktx)
    prep(gc, kc, cumc, ktc)

    for d in range(2):
        reverse = d == 1
        mask_add = mask_ref[d]

        def step(q_ref, v_ref, cum_ref, gt_ref, kt_ref, h_ref, ci, carry):
            rows = pl.ds(pl.multiple_of(ci * ML_CHUNK, ML_CHUNK), ML_CHUNK)
            li_r, lf_r, cum_r = (gt_ref[ci, pl.ds(kind * ML_HEADS + head, 1), :] for kind in (2 * d, 2 * d + 1, 4 + d))
            h, carry = _ml_chunk(q_ref[rows, :], kt_ref[ci], v_ref[rows, :], cum_ref[d, rows, :], li_r, lf_r, cum_r,
                                 mask_add, carry, inv_scale)
            if reverse:
                h_ref[rows, :] += h
            else:
                h_ref[rows, :] = h
            return carry

        carry = (jnp.zeros((dk, vx.shape[1]), F32), jnp.zeros((dk, LANES), F32), jnp.zeros((1, 1), F32))

        def ctx_body(i, carry):
            ci = (n_c_chunks - 1 - i) if reverse else i
            return step(qc, vc, cumc, gtc, ktc, hc, ci, carry)

        def x_body(i, carry):
            ci = (n_x_chunks - 1 - i) if reverse else i
            return step(qx, vx, cumx, gtx, ktx, hx, ci, carry)

        carry = lax.fori_loop(0, n_c_chunks, ctx_body, carry)
        lax.fori_loop(0, n_x_chunks, x_body, carry, unroll=2 if n_x_chunks % 2 == 0 else 1)

    ng = ng_ref[0]

    def finish(h_ref, o_ref, a_ref):
        def body(i, carry):
            rows = pl.ds(pl.multiple_of(i * ML_CHUNK, ML_CHUNK), ML_CHUNK)
            h = h_ref[rows, :]
            hn = h * lax.rsqrt(jnp.mean(h * h, axis=-1, keepdims=True) + EPS) * ng
            a_ref[rows, :] = (hn * jax.nn.sigmoid(o_ref[rows, :].astype(F32))).astype(BF16)
            return carry
        lax.fori_loop(0, h_ref.shape[0] // ML_CHUNK, body, 0)

    finish(hx, ox, ax_ref)
    if with_ctx_out:
        finish(hc, oc, ac_ref)


def _mlstm(z, gates, gates_t, mask, ml_norm_g, b, s, tc, dk, dv, cols, with_ctx_out):
    n_x = b * s
    cq, ck, cv, co = cols
    ctx0 = n_x // tc
    xc, cc = s // ML_CHUNK, tc // ML_CHUNK

    def xspec(width, col0):
        return pl.BlockSpec((s, width), lambda i, h: (i, col0 // width + h))

    def cspec(width, col0):
        return pl.BlockSpec((tc, width), lambda i, h: (ctx0 + i, col0 // width + h))

    out_shape = [jax.ShapeDtypeStruct((n_x, ML_HEADS * dv), BF16)]
    out_specs = [pl.BlockSpec((s, dv), lambda i, h: (i, h))]
    if with_ctx_out:
        out_shape.append(jax.ShapeDtypeStruct((b * tc, ML_HEADS * dv), BF16))
        out_specs.append(pl.BlockSpec((tc, dv), lambda i, h: (i, h)))
    res = pl.pallas_call(
        functools.partial(_mlstm_kernel, with_ctx_out),
        out_shape=tuple(out_shape),
        grid=(b, ML_HEADS),
        in_specs=[
            xspec(dk, cq), xspec(dk, ck), xspec(dv, cv), xspec(dv, co),
            cspec(dk, cq), cspec(dk, ck), cspec(dv, cv), cspec(dv, co),
            pl.BlockSpec((s, LANES), lambda i, h: (i, 0)),
            pl.BlockSpec((tc, LANES), lambda i, h: (ctx0 + i, 0)),
            pl.BlockSpec((xc, ML_GATE_ROWS, ML_CHUNK), lambda i, h: (i, 0, 0)),
            pl.BlockSpec((cc, ML_GATE_ROWS, ML_CHUNK), lambda i, h: (b * xc // cc + i, 0, 0)),
            pl.BlockSpec((2, ML_CHUNK, ML_CHUNK), lambda i, h: (0, 0, 0)),
            pl.BlockSpec((1, 1, dv), lambda i, h: (h, 0, 0)),
        ],
        out_specs=tuple(out_specs),
        scratch_shapes=[
            pltpu.VMEM((2, s, LANES), F32), pltpu.VMEM((xc, dk, ML_CHUNK), BF16),
            pltpu.VMEM((2, tc, LANES), F32), pltpu.VMEM((cc, dk, ML_CHUNK), BF16),
            pltpu.VMEM((s, dv), F32), pltpu.VMEM((tc, dv), F32),
        ],
        compiler_params=_cparams(("parallel", "parallel")),
        name="mlstm",
    )(z, z, z, z, z, z, z, z, gates, gates, gates_t, gates_t, mask, ml_norm_g.reshape(ML_HEADS, 1, dv))
    return res if with_ctx_out else (res[0], None)


def _mla_proj_kernel(qa_ref, kva_ref, kpe_ref, tab_ref, qag_ref, kvag_ref, wq_ref, wkv_ref,
                     gq_ref, gk_ref, q_ref, k_ref, v_ref):
    def normed(a_ref, g_ref):
        a = a_ref[...].astype(F32)
        return (a * lax.rsqrt(jnp.mean(a * a, axis=-1, keepdims=True) + EPS) * g_ref[...]).astype(BF16)

    q_all = jnp.dot(normed(qa_ref, qag_ref), wq_ref[...], preferred_element_type=F32)
    kv_all = jnp.dot(normed(kva_ref, kvag_ref), wkv_ref[...], preferred_element_type=F32)
    tab = tab_ref[...]
    lane = lax.broadcasted_iota(jnp.int32, tab.shape, 1)
    first_half = lane < MLA_ROPE
    gq = gq_ref[...]
    gk = gk_ref[...]
    inv_dqk = 1.0 / MLA_DQK

    kpe = kpe_ref[...].astype(F32)
    ss_kpe = jnp.sum(jnp.where(first_half, kpe * kpe, 0.0), axis=-1, keepdims=True)
    kpe_t = kpe * (tab * gk[:, LANES:])
    kpe_rot = jnp.where(first_half, kpe_t + pltpu.roll(kpe_t, MLA_ROPE, axis=1), 0.0)

    for h in range(MLA_HEADS):
        qn = q_all[:, h * MLA_SLAB:h * MLA_SLAB + LANES]
        qp = q_all[:, h * MLA_SLAB + LANES:(h + 1) * MLA_SLAB]
        ss = jnp.sum(qn * qn, axis=-1, keepdims=True) + jnp.sum(jnp.where(first_half, qp * qp, 0.0), axis=-1,
                                                                keepdims=True)
        r = lax.rsqrt(ss * inv_dqk + EPS) * Q_PRESCALE
        qp_t = qp * (tab * gq[:, LANES:])
        qp_rot = qp_t + pltpu.roll(qp_t, MLA_ROPE, axis=1)
        q_ref[:, h * MLA_SLAB:h * MLA_SLAB + LANES] = (qn * r * gq[:, :LANES]).astype(BF16)
        q_ref[:, h * MLA_SLAB + LANES:(h + 1) * MLA_SLAB] = (qp_rot * r).astype(BF16)

        kn = kv_all[:, h * LANES:(h + 1) * LANES]
        rk = lax.rsqrt((jnp.sum(kn * kn, axis=-1, keepdims=True) + ss_kpe) * inv_dqk + EPS)
        k_ref[:, h * MLA_SLAB:h * MLA_SLAB + LANES] = (kn * rk * gk[:, :LANES]).astype(BF16)
        k_ref[:, h * MLA_SLAB + LANES:(h + 1) * MLA_SLAB] = (kpe_rot * rk).astype(BF16)

    ones_col = jnp.where(lane == 0, 1.0, 0.0).astype(BF16)
    for h in range(MLA_HEADS):
        v_ref[:, h * MLA_SLAB:h * MLA_SLAB + LANES] = kv_all[:, (MLA_HEADS + h) * LANES:(MLA_HEADS + h + 1) * LANES
                                                             ].astype(BF16)
        v_ref[:, h * MLA_SLAB + LANES:(h + 1) * MLA_SLAB] = ones_col


def _mla_proj(z, tab, qag, kvag, wq, wkv, gq, gk, tm, cols, n_x_tiles, tab_tiles):
    r = z.shape[0]
    cqa, ckva, ckpe = cols
    lora = qag.shape[1]
    hs = MLA_HEADS * MLA_SLAB
    hv = MLA_HEADS * LANES
    return pl.pallas_call(
        _mla_proj_kernel,
        out_shape=(jax.ShapeDtypeStruct((r, hs), BF16), jax.ShapeDtypeStruct((r, hs), BF16),
                   jax.ShapeDtypeStruct((r, hs), BF16)),
        grid=(r // tm,),
        in_specs=[
            pl.BlockSpec((tm, lora), lambda i: (i, cqa // lora)),
            pl.BlockSpec((tm, lora), lambda i: (i, ckva // lora)),
            pl.BlockSpec((tm, LANES), lambda i: (i, ckpe // LANES)),
            pl.BlockSpec((tm, LANES), lambda i: (jnp.where(i < n_x_tiles, i % tab_tiles, tab_tiles), 0)),
            pl.BlockSpec((1, lora), lambda i: (0, 0)),
            pl.BlockSpec((1, lora), lambda i: (0, 0)),
            pl.BlockSpec((lora, hs), lambda i: (0, 0)),
            pl.BlockSpec((lora, 2 * hv), lambda i: (0, 0)),
            pl.BlockSpec((1, MLA_SLAB), lambda i: (0, 0)),
            pl.BlockSpec((1, MLA_SLAB), lambda i: (0, 0)),
        ],
        out_specs=(pl.BlockSpec((tm, hs), lambda i: (i, 0)), pl.BlockSpec((tm, hs), lambda i: (i, 0)),
                   pl.BlockSpec((tm, hs), lambda i: (i, 0))),
        compiler_params=_cparams(("parallel",)),
        name="mla_qkv",
    )(z, z, z, tab, qag, kvag, wq, wkv, gq, gk)


def _attn_kernel(n_kv, tq, q_ref, *refs):
    k_refs = refs[:n_kv]
    v_refs = refs[n_kv:2 * n_kv]
    o_ref, s0_ref, s1_ref, m0_ref, m1_ref = refs[2 * n_kv:]
    slots = ((s0_ref, m0_ref), (s1_ref, m1_ref))
    n_tiles = q_ref.shape[0] // tq
    chunks = []
    col = 0
    for kv, k_ref in enumerate(k_refs):
        n_keys = k_ref.shape[0]
        step = min(ATTN_KEY_CHUNK, n_keys)
        for off in range(0, n_keys, step):
            chunks.append((kv, off, col, step))
            col += step

    def scores(t, slot):
        s_ref, m_ref = slots[slot]
        rows = pl.ds(pl.multiple_of(t * tq, tq), tq)
        q = q_ref[rows, :]
        run = None
        for kv, off, c0, size in chunks:
            s = lax.dot_general(q, k_refs[kv][off:off + size, :], (((1,), (1,)), ((), ())),
                                preferred_element_type=F32)
            s_ref[:, c0:c0 + size] = s
            for lb in range(size // LANES):
                blk = s[:, lb * LANES:(lb + 1) * LANES]
                run = blk if run is None else jnp.maximum(run, blk)
        m_ref[...] = run

    def finish(t, slot):
        s_ref, m_ref = slots[slot]
        rows = pl.ds(pl.multiple_of(t * tq, tq), tq)
        m = jnp.max(m_ref[...], axis=-1, keepdims=True)
        acc = None
        for kv, off, c0, size in chunks:
            p = jnp.exp2(s_ref[:, c0:c0 + size] - m).astype(BF16)
            pv = jnp.dot(p, v_refs[kv][off:off + size, :], preferred_element_type=F32)
            acc = pv if acc is None else acc + pv
        o_ref[rows, :] = (acc[:, :LANES] / acc[:, LANES:LANES + 1]).astype(BF16)

    scores(0, 0)

    def body(k, carry):
        scores(2 * k + 1, 1)
        finish(2 * k, 0)
        scores(jnp.minimum(2 * k + 2, n_tiles - 1), 0)
        finish(2 * k + 1, 1)
        return carry

    lax.fori_loop(0, n_tiles // 2, body, 0)
    if n_tiles % 2:
        finish(n_tiles - 1, 0)


def _attention(qo, ko, vo, b, s, tc, tq, latent):
    n_x = b * s
    ctx0 = n_x // tc
    cspec = pl.BlockSpec((tc, MLA_SLAB), lambda i, h: (ctx0 + i, h))
    xspec = pl.BlockSpec((s, MLA_SLAB), lambda i, h: (i, h))
    if latent:
        n_q, n_keys = s, s + tc
        in_specs = [xspec, cspec, xspec, cspec, xspec]
        args = (qo, ko, ko, vo, vo)
    else:
        n_q, n_keys = tc, tc
        in_specs = [cspec, cspec, cspec]
        args = (qo, ko, vo)
    return pl.pallas_call(
        functools.partial(_attn_kernel, (len(args) - 1) // 2, tq),
        out_shape=jax.ShapeDtypeStruct((b * n_q, MLA_HEADS * LANES), BF16),
        grid=(b, MLA_HEADS),
        in_specs=in_specs,
        out_specs=pl.BlockSpec((n_q, LANES), lambda i, h: (i, h)),
        scratch_shapes=[pltpu.VMEM((tq, n_keys), F32), pltpu.VMEM((tq, n_keys), F32),
                        pltpu.VMEM((tq, LANES), F32), pltpu.VMEM((tq, LANES), F32)],
        compiler_params=_cparams(("parallel", "parallel")),
        name="attn_latent" if latent else "attn_ctx",
    )(*args)


S5_HALF = S5_BLOCK_GROUPS * S5_STATE


def _s5_kernel(n_batch, rows_x, rows_c, dot_rows, u_ref, fac_ref, pw_ref, tab_ref, y_ref,
               u2_ref, v_ref, r_ref, m_ref, ot_ref):
    d = pl.program_id(1)
    n_dot = u2_ref.shape[0] // dot_rows

    b_f = fac_ref[0, 0, 0]
    c_f = fac_ref[0, 0, 1]
    b_g = jnp.concatenate([-b_f[:, S5_HALF:], b_f[:, :S5_HALF]], axis=1)
    c_g = jnp.concatenate([c_f[:, S5_HALF:], -c_f[:, :S5_HALF]], axis=1)
    for s in range(S5_SUB):
        blk = slice(s * LANES, (s + 1) * LANES)
        r_ref[blk, :] = (b_f * pw_ref[0, 0, 0, s:s + 1, :] + b_g * pw_ref[0, 0, 1, s:s + 1, :]).astype(BF16)
        ot_ref[blk, :] = (c_f * pw_ref[0, 0, 2, s:s + 1, :] + c_g * pw_ref[0, 0, 3, s:s + 1, :]).astype(BF16)

    c_t = c_f.astype(BF16)

    def toeplitz(reverse):
        taps = {}
        for s in range(S5_SUB):
            z = s if reverse else S5_SUB - 1 - s
            taps[z] = _dot_nt(r_ref[s * LANES:(s + 1) * LANES, :], c_t).astype(BF16)
        zero = jnp.zeros((LANES, LANES), BF16)
        for s in range(S5_SUB):
            for t in range(S5_SUB):
                lag = (s - t) if reverse else (t - s)
                m_ref[s * LANES:(s + 1) * LANES, t * LANES:(t + 1) * LANES] = taps[lag] if lag >= 0 else zero

    @pl.when(d == 0)
    def _():
        toeplitz(False)

    @pl.when(d == 1)
    def _():
        toeplitz(True)

    def dot_rows_of(i):
        return pl.ds(pl.multiple_of(i * dot_rows, dot_rows), dot_rows)

    def token_rows_of(i, s):
        return pl.ds(i * (dot_rows * S5_SUB) + s, dot_rows, stride=S5_SUB)

    @pl.when(d == 0)
    def _():
        def stage(i, carry):
            rows = pl.ds(pl.multiple_of(i * dot_rows, dot_rows), dot_rows)
            y_ref[rows, :] = u_ref[rows, :].astype(F32)
            return carry

        lax.fori_loop(0, n_dot * S5_SUB, stage, 0)

        def regroup(i, carry):
            for s in range(S5_SUB):
                u2_ref[dot_rows_of(i), s * LANES:(s + 1) * LANES] = y_ref[token_rows_of(i, s), :].astype(BF16)
            return carry

        lax.fori_loop(0, n_dot, regroup, 0)

    def increments(i, carry):
        rows = dot_rows_of(i)
        v_ref[rows, :] = jnp.dot(u2_ref[rows, :], r_ref[...], preferred_element_type=F32)
        return carry

    lax.fori_loop(0, n_dot, increments, 0)

    def cmul_add(ar, ai, cr, ci, xr, xi):
        return ar + cr * xr - ci * xi, ai + cr * xi + ci * xr

    def run(reverse):
        tab = tab_ref.at[0, 0]
        last = 0 if reverse else SUBLANES - 1
        first_row = lax.broadcasted_iota(jnp.int32, (SUBLANES, S5_HALF), 0) == (SUBLANES - 1 - last)

        def segment(bases, n_groups, carry):
            def body(i, carry):
                gi = (n_groups - 1 - i) if reverse else i
                out = []
                for base, (cre, cim) in zip(bases, carry):
                    rows = pl.ds(pl.multiple_of(base + gi * SUBLANES, SUBLANES), SUBLANES)
                    re = v_ref[rows, :S5_HALF]
                    im = v_ref[rows, S5_HALF:]
                    for lvl, shift in enumerate((1, 2, 4)):
                        sh = (SUBLANES - shift) if reverse else shift
                        re, im = cmul_add(re, im, tab[2 * lvl], tab[2 * lvl + 1],
                                          pltpu.roll(re, sh, axis=0), pltpu.roll(im, sh, axis=0))
                    re, im = cmul_add(re, im, tab[6], tab[7], cre, cim)
                    sh1 = (SUBLANES - 1) if reverse else 1
                    v_ref[rows, :S5_HALF] = jnp.where(first_row, cre, pltpu.roll(re, sh1, axis=0))
                    v_ref[rows, S5_HALF:] = jnp.where(first_row, cim, pltpu.roll(im, sh1, axis=0))
                    out.append((jnp.broadcast_to(re[last:last + 1, :], re.shape),
                                jnp.broadcast_to(im[last:last + 1, :], im.shape)))
                return tuple(out)
            return lax.fori_loop(0, n_groups, body, carry)

        zero = jnp.zeros((SUBLANES, S5_HALF), F32)
        carry = tuple((zero, zero) for _ in range(n_batch))
        carry = segment([n_batch * rows_x + bi * rows_c for bi in range(n_batch)], rows_c // SUBLANES, carry)
        segment([bi * rows_x for bi in range(n_batch)], rows_x // SUBLANES, carry)

        def outputs(i, carry):
            rows = dot_rows_of(i)
            y = jnp.dot(u2_ref[rows, :], m_ref[...], preferred_element_type=F32)
            y = y + lax.dot_general(v_ref[rows, :].astype(BF16), ot_ref[...], (((1,), (1,)), ((), ())),
                                    preferred_element_type=F32)
            for s in range(S5_SUB):
                part = y[:, s * LANES:(s + 1) * LANES]
                if reverse:
                    y_ref[token_rows_of(i, s), :] += part
                else:
                    y_ref[token_rows_of(i, s), :] = part
            return carry

        lax.fori_loop(0, n_dot, outputs, 0)

    @pl.when(d == 0)
    def _():
        run(False)

    @pl.when(d == 1)
    def _():
        run(True)


def _s5_scan(z, fac, pw, tabs, layer, b, s, tc, col_u):
    r = z.shape[0]
    rc = r // S5_SUB
    n_blocks = fac.shape[2]
    width = S5_SUB * LANES
    assert width == 2 * S5_HALF
    dot_rows = max(n for n in range(16, 641, 16) if rc % n == 0)
    return pl.pallas_call(
        functools.partial(_s5_kernel, b, s // S5_SUB, tc // S5_SUB, dot_rows),
        out_shape=jax.ShapeDtypeStruct((r, n_blocks * LANES), F32),
        grid=(n_blocks, 2),
        in_specs=[
            pl.BlockSpec((r, LANES), lambda cb, d: (0, col_u // LANES + cb), pipeline_mode=pl.Buffered(1)),
            pl.BlockSpec((None, 1, 1, 2, LANES, width), lambda cb, d: (layer, d, cb, 0, 0, 0),
                         pipeline_mode=pl.Buffered(1)),
            pl.BlockSpec((None, 1, 1, 4, S5_SUB, width), lambda cb, d: (layer, d, cb, 0, 0, 0)),
            pl.BlockSpec((None, 1, 1, 8, SUBLANES, S5_HALF), lambda cb, d: (layer, d, cb, 0, 0, 0)),
        ],
        out_specs=pl.BlockSpec((r, LANES), lambda cb, d: (0, cb)),
        scratch_shapes=[pltpu.VMEM((rc, width), BF16), pltpu.VMEM((rc, width), F32),
                        pltpu.VMEM((width, width), BF16), pltpu.VMEM((width, width), BF16),
                        pltpu.VMEM((width, width), BF16)],
        compiler_params=_cparams(("parallel", "arbitrary")),
        name="s5_scan",
    )(z, fac, pw, tabs)


def _glu_kernel(y_ref, u_ref, d_ref, w_ref, b_ref, o_ref, g_ref):
    tm = y_ref.shape[0]

    def body(r, carry):
        rows = pl.ds(pl.multiple_of(r * ROW_CHUNK, ROW_CHUNK), ROW_CHUNK)
        y = y_ref[rows, :] + d_ref[...] * u_ref[rows, :].astype(F32)
        g_ref[rows, :] = jax.nn.gelu(y).astype(BF16)
        return carry

    lax.fori_loop(0, tm // ROW_CHUNK, body, 0)
    g = g_ref[...]
    gate = jax.nn.sigmoid(jnp.dot(g, w_ref[...], preferred_element_type=F32) + b_ref[...])
    o_ref[...] = (g.astype(F32) * gate).astype(BF16)


def _s5_glu(y, z, d_skip, w_glu, b_glu, tm, col_u, n_row_tiles):
    width = y.shape[1]
    return pl.pallas_call(
        _glu_kernel,
        out_shape=jax.ShapeDtypeStruct((n_row_tiles * tm, width), BF16),
        grid=(n_row_tiles,),
        in_specs=[
            pl.BlockSpec((tm, width), lambda i: (i, 0)),
            pl.BlockSpec((tm, width), lambda i: (i, col_u // width)),
            pl.BlockSpec((1, width), lambda i: (0, 0)),
            pl.BlockSpec((width, width), lambda i: (0, 0)),
            pl.BlockSpec((1, width), lambda i: (0, 0)),
        ],
        out_specs=pl.BlockSpec((tm, width), lambda i: (i, 0)),
        scratch_shapes=[pltpu.VMEM((tm, width), BF16)],
        compiler_params=_cparams(("parallel",)),
        name="s5_glu",
    )(y, z, d_skip, w_glu, b_glu)


def _merge_kernel(n_x_tiles, ax_ref, ac_ref, bx_ref, bc_ref, c_ref, ga_ref, gb_ref, gc_ref, w_ref, o_ref):
    def combine(a_ref, b_ref):
        acc = None
        for r, (br_ref, gate_ref) in enumerate(((a_ref, ga_ref), (b_ref, gb_ref), (c_ref, gc_ref))):
            proj = jnp.dot(br_ref[...], w_ref[r], preferred_element_type=F32)
            term = jax.nn.sigmoid(gate_ref[...].astype(F32)) * proj
            acc = term if acc is None else acc + term
        o_ref[...] = acc.astype(BF16)

    @pl.when(pl.program_id(0) < n_x_tiles)
    def _():
        combine(ax_ref, bx_ref)

    @pl.when(pl.program_id(0) >= n_x_tiles)
    def _():
        combine(ac_ref, bc_ref)


def _merge(a_x, a_c, b_x, b_c, cc, z, w_branch, tm, tn, col_g, n_row_tiles, n_x_tiles):
    width = a_x.shape[1]
    d = w_branch.shape[2]

    def gate_spec(r):
        return pl.BlockSpec((tm, tn), lambda i, j: (i, (col_g + r * d) // tn + j))

    ax_spec, ac_spec = _stream_specs((tm, width), n_x_tiles, 0, lambda j: 0)
    return pl.pallas_call(
        functools.partial(_merge_kernel, n_x_tiles),
        out_shape=jax.ShapeDtypeStruct((n_row_tiles * tm, d), BF16),
        grid=(n_row_tiles, d // tn),
        in_specs=[
            ax_spec, ac_spec, ax_spec, ac_spec,
            pl.BlockSpec((tm, width), lambda i, j: (i, 0)),
            gate_spec(0), gate_spec(1), gate_spec(2),
            pl.BlockSpec((N_BRANCH, width, tn), lambda i, j: (0, 0, j)),
        ],
        out_specs=pl.BlockSpec((tm, tn), lambda i, j: (i, j)),
        compiler_params=_cparams(("parallel", "arbitrary")),
        name="merge",
    )(a_x, a_c, b_x, b_c, cc, z, z, z, w_branch)


def _resid_kernel(n_x_tiles, m_ref, w_ref, x_ref, c_ref, al_ref, o_ref):
    resid = jnp.where(pl.program_id(0) < n_x_tiles, x_ref[...], c_ref[...])
    o_ref[...] = resid + al_ref[0] * jnp.dot(m_ref[...], w_ref[...], preferred_element_type=F32)


def _out_proj_residual(m, w, x_src, c_src, ctx_tile0, mod3, tm, tn, n_row_tiles, n_x_tiles, tiles_per_batch, ctx_row,
                       k_alpha):
    kdim, d = w.shape
    nt = d // tn

    def alpha_map(i, j):
        r = jnp.where(i < n_x_tiles, i // tiles_per_batch, ctx_row)
        return (r * N_MOD + k_alpha, 0, j)

    x_spec = pl.BlockSpec((tm, tn), lambda i, j: (jnp.minimum(i, n_x_tiles - 1), jnp.where(i < n_x_tiles, j, nt - 1)))
    c_spec = pl.BlockSpec((tm, tn), lambda i, j: (ctx_tile0 + jnp.maximum(i - n_x_tiles, 0),
                                                  jnp.where(i < n_x_tiles, 0, j)))
    return pl.pallas_call(
        functools.partial(_resid_kernel, n_x_tiles),
        out_shape=jax.ShapeDtypeStruct((n_row_tiles * tm, d), F32),
        grid=(n_row_tiles, nt),
        in_specs=[
            pl.BlockSpec((tm, kdim), lambda i, j: (i, 0)),
            pl.BlockSpec((kdim, tn), lambda i, j: (0, j)),
            x_spec, c_spec,
            pl.BlockSpec((1, 1, tn), alpha_map),
        ],
        out_specs=pl.BlockSpec((tm, tn), lambda i, j: (i, j)),
        compiler_params=_cparams(("parallel", "arbitrary")),
        name="out_proj",
    )(m, w, x_src, c_src, mod3)


def _ff1_kernel(x_ref, g_ref, sh_ref, sc_ref, w_ref, h_ref, xn_ref):
    @pl.when(pl.program_id(1) == 0)
    def _():
        _norm_mod_rows(x_ref, g_ref, sh_ref, sc_ref, xn_ref)

    a = jnp.maximum(jnp.dot(xn_ref[...], w_ref[...], preferred_element_type=F32), 0.0)
    h_ref[...] = (a * a).astype(BF16)


def _ff1(xs, g, mod3, w, tm, tn, n_row_tiles, n_x_tiles, tiles_per_batch, ctx_row):
    d, dff = w.shape
    return pl.pallas_call(
        _ff1_kernel,
        out_shape=jax.ShapeDtypeStruct((n_row_tiles * tm, dff), BF16),
        grid=(n_row_tiles, dff // tn),
        in_specs=[
            pl.BlockSpec((tm, d), lambda i, j: (i, 0)),
            pl.BlockSpec((1, d), lambda i, j: (0, 0)),
            pl.BlockSpec((1, 1, d), _mod_row_map(n_x_tiles, tiles_per_batch, ctx_row, 3)),
            pl.BlockSpec((1, 1, d), _mod_row_map(n_x_tiles, tiles_per_batch, ctx_row, 4)),
            pl.BlockSpec((d, tn), lambda i, j: (0, j)),
        ],
        out_specs=pl.BlockSpec((tm, tn), lambda i, j: (i, j)),
        scratch_shapes=[pltpu.VMEM((tm, d), BF16)],
        compiler_params=_cparams(("parallel", "arbitrary")),
        name="ff1",
    )(xs, g, mod3, mod3, w)


def _ff2_kernel(h_ref, w_ref, x_ref, al_ref, o_ref):
    k = pl.program_id(2)
    part = jnp.dot(h_ref[...], w_ref[...], preferred_element_type=F32)

    @pl.when(k == 0)
    def _():
        o_ref[...] = part

    @pl.when(k > 0)
    def _():
        o_ref[...] += part

    @pl.when(k == pl.num_programs(2) - 1)
    def _():
        o_ref[...] = x_ref[...] + al_ref[0] * o_ref[...]


def _ff2(h, w, xs, mod3, tm, tn, tk, n_row_tiles, n_x_tiles, tiles_per_batch, ctx_row):
    dff, d = w.shape

    def alpha_map(i, j, k):
        r = jnp.where(i < n_x_tiles, i // tiles_per_batch, ctx_row)
        return (r * N_MOD + 5, 0, j)

    return pl.pallas_call(
        _ff2_kernel,
        out_shape=jax.ShapeDtypeStruct((n_row_tiles * tm, d), F32),
        grid=(n_row_tiles, d // tn, dff // tk),
        in_specs=[
            pl.BlockSpec((tm, tk), lambda i, j, k: (i, k)),
            pl.BlockSpec((tk, tn), lambda i, j, k: (k, j)),
            pl.BlockSpec((tm, tn), lambda i, j, k: (i, j)),
            pl.BlockSpec((1, 1, tn), alpha_map),
        ],
        out_specs=pl.BlockSpec((tm, tn), lambda i, j, k: (i, j)),
        compiler_params=_cparams(("parallel", "parallel", "arbitrary")),
        name="ff2",
    )(h, w, xs, mod3)


def _rope_partner():
    j = np.arange(MLA_ROPE)
    quarter = MLA_ROPE // 4
    return np.where((j // quarter) % 2 == 0, j + quarter, j - quarter)


def _rope_table(s, tm):
    pos = jnp.arange(s)
    row = (pos // GRID_W).astype(F32)
    col = (pos % GRID_W).astype(F32)
    n_freq = MLA_ROPE // 4
    inv_freq = ROPE_THETA ** (-jnp.arange(n_freq, dtype=F32) / n_freq)
    ang_r = row[:, None] * inv_freq
    ang_c = col[:, None] * inv_freq
    cos = jnp.concatenate([jnp.cos(ang_r)] * 2 + [jnp.cos(ang_c)] * 2, axis=-1)
    sin = jnp.concatenate([-jnp.sin(ang_r), jnp.sin(ang_r), -jnp.sin(ang_c), jnp.sin(ang_c)], axis=-1)
    ident = jnp.concatenate([jnp.ones((tm, MLA_ROPE), F32), jnp.zeros((tm, MLA_ROPE), F32)], axis=-1)
    return jnp.concatenate([jnp.concatenate([cos, sin], axis=-1), ident], axis=0)


def _cast_kernel(w_ref, o_ref):
    o_ref[...] = w_ref[...].astype(BF16)


def _layer_bf16(w_all, layer):
    rows, cols = w_all.shape[1:]
    tr = CAST_BLOCK_BYTES // (4 * cols)
    while rows % tr:
        tr //= 2
    return pl.pallas_call(
        _cast_kernel,
        out_shape=jax.ShapeDtypeStruct((rows, cols), BF16),
        grid=(rows // tr,),
        in_specs=[pl.BlockSpec((None, tr, cols), lambda i: (layer, i, 0))],
        out_specs=pl.BlockSpec((tr, cols), lambda i: (i, 0)),
        compiler_params=_cparams(("parallel",)),
        name="cast_bf16",
    )(w_all)


def _pack_w_in(w_in, b_in, gate_b, sizes, n_pad):
    w_t = w_in.T
    d_model = w_in.shape[0]
    bounds = np.cumsum((0,) + sizes)
    seg = [slice(int(bounds[i]), int(bounds[i + 1])) for i in range(len(sizes))]
    partner = _rope_partner()
    order = (0, 1, 2, 3, 5, 6, 8, 9)
    names = ("q", "k", "v", "o", "qa", "kva", "u", "gates")
    offs, pos = {}, 0
    for name, i in zip(names, order):
        offs[name] = pos
        pos += sizes[i]
    offs["kpe"], offs["kpe_sw"] = pos, pos + MLA_ROPE
    n_zero = n_pad - pos - 2 * MLA_ROPE
    kpe_w, kpe_b = w_t[seg[7]], b_in[seg[7]]
    w = jnp.concatenate([w_t[seg[i]] for i in order] + [kpe_w, kpe_w[partner], jnp.zeros((n_zero, d_model), F32)],
                        axis=0).astype(BF16)
    b = jnp.concatenate([b_in[seg[i]] for i in order] + [kpe_b, kpe_b[partner], jnp.zeros((n_zero,), F32)])
    n_g = sizes[4]
    wg = jnp.concatenate([w_t[seg[4]], jnp.zeros((LANES - n_g, d_model), F32)], axis=0).astype(BF16)
    bg = jnp.concatenate([b_in[seg[4]] + gate_b.reshape(-1), jnp.zeros((LANES - n_g,), F32)])[None, :]
    return w, b[None, :], wg, bg, offs


def _pack_mla(w_uq, w_ukv, qn_g, kn_g):
    partner = _rope_partner()
    lora = w_uq.shape[0]
    wq = w_uq.reshape(lora, MLA_HEADS, MLA_DQK)
    wq = jnp.concatenate([wq, wq[:, :, MLA_NOPE + partner]], axis=-1).reshape(lora, MLA_HEADS * MLA_SLAB)
    wkv = w_ukv.reshape(w_ukv.shape[0], MLA_HEADS, -1)
    wkv = jnp.concatenate([wkv[:, :, :MLA_NOPE].reshape(lora, -1), wkv[:, :, MLA_NOPE:].reshape(lora, -1)], axis=-1)

    def gains(g):
        return jnp.concatenate([g, g[MLA_NOPE + partner]])[None, :]

    return wq.astype(BF16), wkv.astype(BF16), gains(qn_g), gains(kn_g)


def _pack_s5(a_re, a_im, log_dt, b_re, b_im, c_re, c_im):
    n_dir, n_groups, n_state = a_re.shape
    gc = b_re.shape[-1]
    nb = n_groups // S5_BLOCK_GROUPS
    lam_re = jnp.minimum(a_re.astype(F32), -1e-4)
    lam_im = a_im.astype(F32)
    dt = jnp.exp(log_dt.astype(F32))[..., None]

    def pole_power(k):
        mag = jnp.exp(k * lam_re * dt)
        return mag * jnp.cos(k * lam_im * dt), mag * jnp.sin(k * lam_im * dt)

    bar_re, bar_im = pole_power(1.0)
    den = lam_re * lam_re + lam_im * lam_im
    f_re = ((bar_re - 1.0) * lam_re + bar_im * lam_im) / den
    f_im = (bar_im * lam_re - (bar_re - 1.0) * lam_im) / den
    bb_re = f_re[..., None] * b_re.astype(F32) - f_im[..., None] * b_im.astype(F32)
    bb_im = f_re[..., None] * b_im.astype(F32) + f_im[..., None] * b_re.astype(F32)
    eye = jnp.eye(S5_BLOCK_GROUPS, dtype=F32)
    sub = S5_SUB

    def per_block(a):
        return a.reshape(a.shape[:-2] + (nb, S5_BLOCK_GROUPS * n_state))

    def block_b(part):
        p = part.reshape(n_dir, nb, S5_BLOCK_GROUPS, n_state, gc)
        m = jnp.einsum('dbgnc,gh->dbgchn', p, eye, precision=HIGHEST)
        return m.reshape(n_dir, nb, LANES, S5_HALF)

    def block_c(part):
        p = part.astype(F32).reshape(n_dir, nb, S5_BLOCK_GROUPS, gc, n_state)
        m = jnp.einsum('dbgcn,gh->dbgnhc', p, eye, precision=HIGHEST)
        return m.reshape(n_dir, nb, S5_HALF, LANES)

    bm_re, bm_im = block_b(bb_re), block_b(bb_im)
    cm_re, cm_im = block_c(c_re), block_c(c_im)

    tau = jnp.arange(sub + 1, dtype=F32)[:, None, None, None]
    p_re, p_im = (per_block(p) for p in pole_power(tau))
    ct_re, ct_im = jnp.swapaxes(cm_re, -1, -2), jnp.swapaxes(cm_im, -1, -2)
    cat = functools.partial(jnp.concatenate, axis=-1)
    fac = jnp.stack([cat([bm_re, bm_im]), cat([ct_re, -ct_im])], axis=2)
    t = np.arange(sub)
    pws = []
    for d in range(n_dir):
        to_exit = (sub - 1 - t) if d == 0 else t
        age = (t + 1) if d == 0 else (sub - t)
        rows = [p_re[to_exit, d], p_im[to_exit, d], p_re[age, d], p_im[age, d]]
        pws.append(jnp.stack([cat([a, a]).transpose(1, 0, 2) for a in rows], axis=1))
    pw = jnp.stack(pws)

    rows = jnp.arange(SUBLANES)
    tabs = []
    for d in range(n_dir):
        per_dir = []
        for shift in (1, 2, 4):
            keep = ((rows <= SUBLANES - 1 - shift) if d == 1 else (rows >= shift))[None, :, None]
            s_re, s_im = pole_power(float(shift * sub))
            per_dir += [jnp.where(keep, per_block(s_re[d])[:, None, :], 0.0),
                        jnp.where(keep, per_block(s_im[d])[:, None, :], 0.0)]
        expo = (((SUBLANES - rows) if d == 1 else (rows + 1)) * sub).astype(F32)
        s_re, s_im = pole_power(expo[:, None, None, None])
        per_dir += [jnp.moveaxis(per_block(s_re[:, d]), 0, 1), jnp.moveaxis(per_block(s_im[:, d]), 0, 1)]
        tabs.append(jnp.stack(per_dir, axis=1))
    return fac, pw, jnp.stack(tabs).astype(F32)


def _tri_matrices():
    t = np.arange(ML_CHUNK)
    lower = t[None, :] <= t[:, None]
    mask = np.where(np.stack([lower, lower.T]), 0.0, NEG_BIG).astype(np.float32)
    return jnp.asarray(lower.astype(np.float32), dtype=BF16), jnp.asarray(mask)


def kernel(x, c, ctx, c_ctx, w_mod, b_mod, norm_g, w_in, b_in, ml_gate_b, ml_norm_g, mla_qa_g, mla_kva_g, mla_w_uq, mla_w_ukv, mla_qn_g, mla_kn_g, s5_a_re, s5_a_im, s5_log_dt, s5_b_re, s5_b_im, s5_c_re, s5_c_im, s5_d, s5_w_glu, s5_b_glu, w_branch, w_out, w_ff1, w_ff2):
    b, s, d = x.shape
    tc = ctx.shape[1]
    depth = w_mod.shape[0]
    dv = ml_norm_g.shape[2]
    dk = dv // 2
    lora = mla_qa_g.shape[1]
    s5_width = s5_d.shape[1]
    branch_w = w_branch.shape[2]
    sizes = (ML_HEADS * dk, ML_HEADS * dk, ML_HEADS * dv, ML_HEADS * dv, 4 * ML_HEADS, lora, lora, MLA_ROPE,
             s5_width, N_BRANCH * d)
    assert sum(sizes) == w_in.shape[2] and b + 1 <= SUBLANES
    assert s % ML_CHUNK == 0 and tc % ML_CHUNK == 0 and branch_w == ML_HEADS * dv == MLA_HEADS * LANES == s5_width

    n_x = b * s
    n_c = b * tc
    tm = _row_tile(s, n_c)
    n_x_tiles = n_x // tm
    n_tiles = n_x_tiles + n_c // tm
    tiles_per_batch = s // tm
    tile_args = (n_x_tiles, tiles_per_batch, b)

    x_src, c_src, ctx_tile0 = x.reshape(n_x, d), ctx.reshape(n_c, d), 0
    cc = jnp.concatenate([c, c_ctx[None, :], jnp.zeros((SUBLANES - b - 1, d), F32)], axis=0)
    mod = _modulation(cc, w_mod, b_mod)
    tm_q = min(tm, 512)
    tab = _rope_table(s, tm_q)
    tri, ml_mask = _tri_matrices()
    s5_packed = jax.vmap(_pack_s5)(s5_a_re, s5_a_im, s5_log_dt, s5_b_re, s5_b_im, s5_c_re, s5_c_im)
    n_used = sum(sizes) - sizes[4] + MLA_ROPE
    tn_in = 1280
    n_pad = -(-n_used // tn_in) * tn_in

    for l in range(depth):
        with_ctx_out = l < depth - 1
        mod3 = mod[l].reshape(SUBLANES * N_MOD, 1, d)
        w_p, b_p, wg, bg, offs = _pack_w_in(w_in[l], b_in[l], ml_gate_b[l], sizes, n_pad)
        z, gz = _in_proj(x_src, c_src, ctx_tile0, n_tiles, norm_g[l, 0][None, :], mod3, w_p, b_p, wg, bg, tm, tn_in,
                         *tile_args)

        gates, gates_t = _gate_prep(gz, tri)
        a_x, a_c = _mlstm(z, gates, gates_t, ml_mask, ml_norm_g[l], b, s, tc, dk, dv,
                          (offs["q"], offs["k"], offs["v"], offs["o"]), with_ctx_out)

        wq, wkv, gq, gk = _pack_mla(mla_w_uq[l], mla_w_ukv[l], mla_qn_g[l], mla_kn_g[l])
        qo, ko, vo = _mla_proj(z, tab, mla_qa_g[l][None, :], mla_kva_g[l][None, :], wq, wkv, gq, gk, tm_q,
                               (offs["qa"], offs["kva"], offs["kpe"]), n_x // tm_q, s // tm_q)
        tq = min(512, s)
        b_x = _attention(qo, ko, vo, b, s, tc, tq, True)

        y = _s5_scan(z, *s5_packed, l, b, s, tc, offs["u"])
        n_out_tiles = n_tiles if with_ctx_out else n_x_tiles
        c_all = _s5_glu(y, z, s5_d[l][None, :], s5_w_glu[l].astype(BF16), s5_b_glu[l][None, :], tm, offs["u"],
                        n_out_tiles)

        if with_ctx_out:
            b_c = _attention(qo, ko, vo, b, s, tc, min(tq, tc), False)
        else:
            a_c, b_c = a_x, b_x
        w_br = _layer_bf16(w_branch.reshape(depth, N_BRANCH * branch_w, d), l).reshape(N_BRANCH, branch_w, d)
        merged = _merge(a_x, a_c, b_x, b_c, c_all, z, w_br, tm, 512, offs["gates"],
                        n_out_tiles, n_x_tiles)
        xs1 = _out_proj_residual(merged, _layer_bf16(w_out, l), x_src, c_src, ctx_tile0, mod3, tm, 1024,
                                 n_out_tiles, *tile_args, 2)
        hid = _ff1(xs1, norm_g[l, 1][None, :], mod3, _layer_bf16(w_ff1, l), tm, 1024, n_out_tiles, *tile_args)
        xs = _ff2(hid, _layer_bf16(w_ff2, l), xs1, mod3, tm, 1024, 2048, n_out_tiles, *tile_args)
        x_src, c_src, ctx_tile0 = xs, xs, n_x_tiles

    return xs.reshape(b, s, d)
```

```python
import functools
import math

import jax
import jax.numpy as jnp
import numpy as np
from jax import lax
from jax.experimental import pallas as pl
from jax.experimental.pallas import tpu as pltpu

F32 = jnp.float32
BF16 = jnp.bfloat16
HIGHEST = lax.Precision.HIGHEST

N_MOD = 6
N_BRANCH = 3
ML_HEADS = 4
MLA_HEADS = 8
MLA_NOPE = 128
MLA_ROPE = 64
MLA_DQK = MLA_NOPE + MLA_ROPE
MLA_SLAB = 256
GRID_W = 64
ROPE_THETA = 10000.0
S5_GROUP = 16
S5_STATE = 64
S5_BLOCK_GROUPS = 8
EPS = 1e-6
NEG_BIG = -1e30

LANES = 128
SUBLANES = 8
VMEM_LIMIT = 56 * 1024 * 1024

ML_CHUNK = 256
ML_GATE_ROWS = 32
S5_SUB = 8
ATTN_KEY_CHUNK = 512
Q_PRESCALE = MLA_DQK ** -0.5 * math.log2(math.e)
ROW_CHUNK = 128
CAST_BLOCK_BYTES = 8 * 1024 * 1024


def _cparams(sem):
    return pltpu.CompilerParams(dimension_semantics=sem, vmem_limit_bytes=VMEM_LIMIT)


def _row_tile(n_x_rows_per_batch, n_ctx_rows):
    tm = 1024
    while n_x_rows_per_batch % tm or n_ctx_rows % tm:
        tm //= 2
    return tm


def _mod_kernel(c_ref, w_ref, b_ref, o_ref):
    s = c_ref[...]
    s = s * jax.nn.sigmoid(s)
    o_ref[0] = jnp.dot(s.astype(BF16), w_ref[0].astype(BF16), preferred_element_type=F32) + b_ref[0]


def _modulation(cc, w_mod, b_mod):
    n_layers, d, n = w_mod.shape
    tn = 1024
    return pl.pallas_call(
        _mod_kernel,
        out_shape=jax.ShapeDtypeStruct((n_layers, SUBLANES, n), F32),
        grid=(n_layers, n // tn),
        in_specs=[
            pl.BlockSpec((SUBLANES, d), lambda l, j: (0, 0)),
            pl.BlockSpec((1, d, tn), lambda l, j: (l, 0, j)),
            pl.BlockSpec((1, 1, tn), lambda l, j: (l, 0, j)),
        ],
        out_specs=pl.BlockSpec((1, SUBLANES, tn), lambda l, j: (l, 0, j)),
        compiler_params=_cparams(("parallel", "parallel")),
        name="adaln_mod",
    )(cc, w_mod, b_mod.reshape(n_layers, 1, n))


def _norm_mod_rows(x_ref, g_ref, sh_ref, sc_ref, xn_ref):
    tm = x_ref.shape[0]
    g = g_ref[...]
    sc = 1.0 + sc_ref[0]
    sh = sh_ref[0]

    def body(r, carry):
        rows = pl.ds(pl.multiple_of(r * ROW_CHUNK, ROW_CHUNK), ROW_CHUNK)
        x = x_ref[rows, :]
        ms = jnp.mean(x * x, axis=-1, keepdims=True)
        y = x * lax.rsqrt(ms + EPS) * g
        xn_ref[rows, :] = (y * sc + sh).astype(BF16)
        return carry

    lax.fori_loop(0, tm // ROW_CHUNK, body, 0)


def _mod_row_map(n_x_tiles, tiles_per_batch, ctx_row, k):
    def index_map(i, j):
        r = jnp.where(i < n_x_tiles, i // tiles_per_batch, ctx_row)
        return (r * N_MOD + k, 0, 0)
    return index_map


def _stream_specs(block, n_x_tiles, ctx_tile0, col_map):
    x_spec = pl.BlockSpec(block, lambda i, *r: (jnp.minimum(i, n_x_tiles - 1), col_map(*r)))
    c_spec = pl.BlockSpec(block, lambda i, *r: (ctx_tile0 + jnp.maximum(i - n_x_tiles, 0), col_map(*r)),
                          pipeline_mode=pl.Buffered(1))
    return x_spec, c_spec


def _dot_nt(a, b_t):
    return lax.dot_general(a, b_t, (((1,), (1,)), ((), ())), preferred_element_type=F32)


def _in_kernel(n_x_tiles, x_ref, c_ref, g_ref, sh_ref, sc_ref, w_ref, b_ref, wg_ref, bg_ref, z_ref, gz_ref, xn_ref):
    @pl.when(pl.program_id(1) == 0)
    def _():
        @pl.when(pl.program_id(0) < n_x_tiles)
        def _():
            _norm_mod_rows(x_ref, g_ref, sh_ref, sc_ref, xn_ref)

        @pl.when(pl.program_id(0) >= n_x_tiles)
        def _():
            _norm_mod_rows(c_ref, g_ref, sh_ref, sc_ref, xn_ref)

        gz_ref[...] = _dot_nt(xn_ref[...], wg_ref[...]) + bg_ref[...]

    z_ref[...] = (_dot_nt(xn_ref[...], w_ref[...]) + b_ref[...]).astype(BF16)


def _in_proj(x_src, c_src, ctx_tile0, n_tiles, g, mod3, w, b, wg, bg, tm, tn, n_x_tiles, tiles_per_batch, ctx_row):
    d = x_src.shape[1]
    r = n_tiles * tm
    nz = w.shape[0]
    x_spec, c_spec = _stream_specs((tm, d), n_x_tiles, ctx_tile0, lambda j: 0)
    return pl.pallas_call(
        functools.partial(_in_kernel, n_x_tiles),
        out_shape=(jax.ShapeDtypeStruct((r, nz), BF16), jax.ShapeDtypeStruct((r, LANES), F32)),
        grid=(n_tiles, nz // tn),
        in_specs=[
            x_spec, c_spec,
            pl.BlockSpec((1, d), lambda i, j: (0, 0)),
            pl.BlockSpec((1, 1, d), _mod_row_map(n_x_tiles, tiles_per_batch, ctx_row, 0)),
            pl.BlockSpec((1, 1, d), _mod_row_map(n_x_tiles, tiles_per_batch, ctx_row, 1)),
            pl.BlockSpec((tn, d), lambda i, j: (j, 0)),
            pl.BlockSpec((1, tn), lambda i, j: (0, j)),
            pl.BlockSpec((LANES, d), lambda i, j: (0, 0)),
            pl.BlockSpec((1, LANES), lambda i, j: (0, 0)),
        ],
        out_specs=(
            pl.BlockSpec((tm, tn), lambda i, j: (i, j)),
            pl.BlockSpec((tm, LANES), lambda i, j: (i, 0)),
        ),
        scratch_shapes=[pltpu.VMEM((tm, d), BF16)],
        compiler_params=_cparams(("parallel", "arbitrary")),
        name="in_proj",
    )(x_src, c_src, g, mod3, mod3, w, b, wg, bg)


def _log_sigmoid(x):
    return jnp.minimum(x, 0.0) - jnp.log1p(jnp.exp(-jnp.abs(x)))


def _split3(a):
    hi = a.astype(BF16)
    r1 = a - hi.astype(F32)
    mid = r1.astype(BF16)
    lo = (r1 - mid.astype(F32)).astype(BF16)
    return hi, mid, lo


def _gate_prep_kernel(g_ref, tri_ref, a_ref, at_ref):
    for c in range(at_ref.shape[0]):
        rows = slice(c * ML_CHUNK, (c + 1) * ML_CHUNK)
        g = g_ref[rows, :]
        kind = lax.broadcasted_iota(jnp.int32, g.shape, 1) // ML_HEADS
        lg = jnp.where((kind == 1) | (kind == 3), _log_sigmoid(g), g)
        pre = sum(jnp.dot(tri_ref[...], p, preferred_element_type=F32) for p in _split3(lg))
        suf = pre[ML_CHUNK - 1:ML_CHUNK, :] - pre + lg
        a = jnp.where(kind == 4, pltpu.roll(pre, 3 * ML_HEADS, axis=1),
                      jnp.where(kind == 5, pltpu.roll(suf, 2 * ML_HEADS, axis=1), lg))
        a_ref[rows, :] = a
        at_ref[c] = a.T[:ML_GATE_ROWS, :]


def _gate_prep(gz, tri):
    r = gz.shape[0]
    n_chunks = r // ML_CHUNK
    per_step = 4 if n_chunks % 4 == 0 else 1
    return pl.pallas_call(
        _gate_prep_kernel,
        out_shape=(jax.ShapeDtypeStruct((r, LANES), F32),
                   jax.ShapeDtypeStruct((n_chunks, ML_GATE_ROWS, ML_CHUNK), F32)),
        grid=(n_chunks // per_step,),
        in_specs=[pl.BlockSpec((per_step * ML_CHUNK, LANES), lambda i: (i, 0)),
                  pl.BlockSpec((ML_CHUNK, ML_CHUNK), lambda i: (0, 0))],
        out_specs=(pl.BlockSpec((per_step * ML_CHUNK, LANES), lambda i: (i, 0)),
                   pl.BlockSpec((per_step, ML_GATE_ROWS, ML_CHUNK), lambda i: (i, 0, 0))),
        compiler_params=_cparams(("parallel",)),
        name="mlstm_gates",
    )(gz, tri)


def _ml_chunk(q, k_t, v, cum_cb, li_r, lf_r, cum_r, mask_add, carry, inv_scale):
    c_mat, n_mat, m = carry
    length, dv = v.shape
    crow = cum_r - li_r
    total = jnp.sum(lf_r, axis=-1, keepdims=True)
    ones = jnp.ones((length, LANES), BF16)
    n_blk = length // LANES

    def wide(a):
        return jnp.concatenate([a] * (dv // LANES), axis=1)

    log_w = [cum_cb - crow[:, j * LANES:(j + 1) * LANES] + mask_add[:, j * LANES:(j + 1) * LANES]
             for j in range(n_blk)]
    row_max = jnp.max(functools.reduce(jnp.maximum, log_w), axis=-1, keepdims=True)
    log_inter = cum_cb + m
    m_t = jnp.maximum(log_inter, row_max)
    w_inter = jnp.exp(log_inter - m_t)
    qk = jnp.dot(q, k_t, preferred_element_type=F32)
    s = jnp.concatenate([qk[:, j * LANES:(j + 1) * LANES] * jnp.exp(log_w[j] - m_t) for j in range(n_blk)],
                        axis=1).astype(BF16)
    num = wide(w_inter) * jnp.dot(q, c_mat.astype(BF16), preferred_element_type=F32)
    num = num + jnp.dot(s, v, preferred_element_type=F32)
    den = w_inter * jnp.dot(q, n_mat.astype(BF16), preferred_element_type=F32)
    den = den + jnp.dot(s, ones, preferred_element_type=F32)
    h = num * wide(1.0 / jnp.maximum(jnp.abs(den), jnp.exp(-m_t) * inv_scale))

    log_end = total - crow
    m_new = jnp.maximum(total + m, jnp.max(log_end, axis=-1, keepdims=True))
    decay = jnp.exp(total + m - m_new)
    kw_t = (k_t.astype(F32) * jnp.exp(log_end - m_new)).astype(BF16)
    c_new = decay * c_mat + jnp.dot(kw_t, v, preferred_element_type=F32)
    n_new = decay * n_mat + jnp.dot(kw_t, ones, preferred_element_type=F32)
    return h, (c_new, n_new, m_new)


def _mlstm_kernel(with_ctx_out, qx, kx, vx, ox, qc, kc, vc, oc, gx, gc, gtx, gtc, mask_ref, ng_ref, *rest):
    if with_ctx_out:
        ax_ref, ac_ref, cumx, ktx, cumc, ktc, hx, hc = rest
    else:
        ax_ref, cumx, ktx, cumc, ktc, hx, hc = rest
        ac_ref = None
    head = pl.program_id(1)
    dk = qx.shape[1]
    inv_scale = float(dk) ** 0.5
    n_x_chunks = qx.shape[0] // ML_CHUNK
    n_c_chunks = qc.shape[0] // ML_CHUNK

    def prep(g_ref, k_ref, cum_ref, kt_ref):
        def body(i, carry):
            rows = pl.ds(pl.multiple_of(i * ML_CHUNK, ML_CHUNK), ML_CHUNK)
            a = g_ref[rows, :]
            col = lax.broadcasted_iota(jnp.int32, a.shape, 1)
            for d in range(2):
                pick = jnp.sum(jnp.where(col == (4 + d) * ML_HEADS + head, a, 0.0), axis=-1, keepdims=True)
                cum_ref[d, rows, :] = jnp.broadcast_to(pick, a.shape)
            kt_ref[i] = k_ref[rows, :].astype(F32).T.astype(BF16)
            return carry
        n_chunks = g_ref.shape[0] // ML_CHUNK
        lax.fori_loop(0, n_chunks, body, 0, unroll=2 if n_chunks % 2 == 0 else 1)

    prep(gx, kx, cumx, ktx)
    prep(gc, kc, cumc, ktc)

    for d in range(2):
        reverse = d == 1
        mask_add = mask_ref[d]

        def step(q_ref, v_ref, cum_ref, gt_ref, kt_ref, h_ref, ci, carry):
            rows = pl.ds(pl.multiple_of(ci * ML_CHUNK, ML_CHUNK), ML_CHUNK)
            li_r, lf_r, cum_r = (gt_ref[ci, pl.ds(kind * ML_HEADS + head, 1), :] for kind in (2 * d, 2 * d + 1, 4 + d))
            h, carry = _ml_chunk(q_ref[rows, :], kt_ref[ci], v_ref[rows, :], cum_ref[d, rows, :], li_r, lf_r, cum_r,
                                 mask_add, carry, inv_scale)
            if reverse:
                h_ref[rows, :] += h
            else:
                h_ref[rows, :] = h
            return carry

        carry = (jnp.zeros((dk, vx.shape[1]), F32), jnp.zeros((dk, LANES), F32), jnp.zeros((1, 1), F32))

        def ctx_body(i, carry):
            ci = (n_c_chunks - 1 - i) if reverse else i
            return step(qc, vc, cumc, gtc, ktc, hc, ci, carry)

        def x_body(i, carry):
            ci = (n_x_chunks - 1 - i) if reverse else i
            return step(qx, vx, cumx, gtx, ktx, hx, ci, carry)

        carry = lax.fori_loop(0, n_c_chunks, ctx_body, carry)
        lax.fori_loop(0, n_x_chunks, x_body, carry, unroll=2 if n_x_chunks % 2 == 0 else 1)

    ng = ng_ref[0]

    def finish(h_ref, o_ref, a_ref):
        def body(i, carry):
            rows = pl.ds(pl.multiple_of(i * ML_CHUNK, ML_CHUNK), ML_CHUNK)
            h = h_ref[rows, :]
            hn = h * lax.rsqrt(jnp.mean(h * h, axis=-1, keepdims=True) + EPS) * ng
            a_ref[rows, :] = (hn * jax.nn.sigmoid(o_ref[rows, :].astype(F32))).astype(BF16)
            return carry
        lax.fori_loop(0, h_ref.shape[0] // ML_CHUNK, body, 0)

    finish(hx, ox, ax_ref)
    if with_ctx_out:
        finish(hc, oc, ac_ref)


def _mlstm(z, gates, gates_t, mask, ml_norm_g, b, s, tc, dk, dv, cols, with_ctx_out):
    n_x = b * s
    cq, ck, cv, co = cols
    ctx0 = n_x // tc
    xc, cc = s // ML_CHUNK, tc // ML_CHUNK

    def xspec(width, col0):
        return pl.BlockSpec((s, width), lambda i, h: (i, col0 // width + h))

    def cspec(width, col0):
        return pl.BlockSpec((tc, width), lambda i, h: (ctx0 + i, col0 // width + h))

    out_shape = [jax.ShapeDtypeStruct((n_x, ML_HEADS * dv), BF16)]
    out_specs = [pl.BlockSpec((s, dv), lambda i, h: (i, h))]
    if with_ctx_out:
        out_shape.append(jax.ShapeDtypeStruct((b * tc, ML_HEADS * dv), BF16))
        out_specs.append(pl.BlockSpec((tc, dv), lambda i, h: (i, h)))
    res = pl.pallas_call(
        functools.partial(_mlstm_kernel, with_ctx_out),
        out_shape=tuple(out_shape),
        grid=(b, ML_HEADS),
        in_specs=[
            xspec(dk, cq), xspec(dk, ck), xspec(dv, cv), xspec(dv, co),
            cspec(dk, cq), cspec(dk, ck), cspec(dv, cv), cspec(dv, co),
            pl.BlockSpec((s, LANES), lambda i, h: (i, 0)),
            pl.BlockSpec((tc, LANES), lambda i, h: (ctx0 + i, 0)),
            pl.BlockSpec((xc, ML_GATE_ROWS, ML_CHUNK), lambda i, h: (i, 0, 0)),
            pl.BlockSpec((cc, ML_GATE_ROWS, ML_CHUNK), lambda i, h: (b * xc // cc + i, 0, 0)),
            pl.BlockSpec((2, ML_CHUNK, ML_CHUNK), lambda i, h: (0, 0, 0)),
            pl.BlockSpec((1, 1, dv), lambda i, h: (h, 0, 0)),
        ],
        out_specs=tuple(out_specs),
        scratch_shapes=[
            pltpu.VMEM((2, s, LANES), F32), pltpu.VMEM((xc, dk, ML_CHUNK), BF16),
            pltpu.VMEM((2, tc, LANES), F32), pltpu.VMEM((cc, dk, ML_CHUNK), BF16),
            pltpu.VMEM((s, dv), F32), pltpu.VMEM((tc, dv), F32),
        ],
        compiler_params=_cparams(("parallel", "parallel")),
        name="mlstm",
    )(z, z, z, z, z, z, z, z, gates, gates, gates_t, gates_t, mask, ml_norm_g.reshape(ML_HEADS, 1, dv))
    return res if with_ctx_out else (res[0], None)


def _mla_proj_kernel(qa_ref, kva_ref, kpe_ref, tab_ref, qag_ref, kvag_ref, wq_ref, wkv_ref,
                     gq_ref, gk_ref, q_ref, k_ref, v_ref):
    def normed(a_ref, g_ref):
        a = a_ref[...].astype(F32)
        return (a * lax.rsqrt(jnp.mean(a * a, axis=-1, keepdims=True) + EPS) * g_ref[...]).astype(BF16)

    q_all = jnp.dot(normed(qa_ref, qag_ref), wq_ref[...], preferred_element_type=F32)
    kv_all = jnp.dot(normed(kva_ref, kvag_ref), wkv_ref[...], preferred_element_type=F32)
    tab = tab_ref[...]
    lane = lax.broadcasted_iota(jnp.int32, tab.shape, 1)
    first_half = lane < MLA_ROPE
    gq = gq_ref[...]
    gk = gk_ref[...]
    inv_dqk = 1.0 / MLA_DQK

    kpe = kpe_ref[...].astype(F32)
    ss_kpe = jnp.sum(jnp.where(first_half, kpe * kpe, 0.0), axis=-1, keepdims=True)
    kpe_t = kpe * (tab * gk[:, LANES:])
    kpe_rot = jnp.where(first_half, kpe_t + pltpu.roll(kpe_t, MLA_ROPE, axis=1), 0.0)

    for h in range(MLA_HEADS):
        qn = q_all[:, h * MLA_SLAB:h * MLA_SLAB + LANES]
        qp = q_all[:, h * MLA_SLAB + LANES:(h + 1) * MLA_SLAB]
        ss = jnp.sum(qn * qn, axis=-1, keepdims=True) + jnp.sum(jnp.where(first_half, qp * qp, 0.0), axis=-1,
                                                                keepdims=True)
        r = lax.rsqrt(ss * inv_dqk + EPS) * Q_PRESCALE
        qp_t = qp * (tab * gq[:, LANES:])
        qp_rot = qp_t + pltpu.roll(qp_t, MLA_ROPE, axis=1)
        q_ref[:, h * MLA_SLAB:h * MLA_SLAB + LANES] = (qn * r * gq[:, :LANES]).astype(BF16)
        q_ref[:, h * MLA_SLAB + LANES:(h + 1) * MLA_SLAB] = (qp_rot * r).astype(BF16)

        kn = kv_all[:, h * LANES:(h + 1) * LANES]
        rk = lax.rsqrt((jnp.sum(kn * kn, axis=-1, keepdims=True) + ss_kpe) * inv_dqk + EPS)
        k_ref[:, h * MLA_SLAB:h * MLA_SLAB + LANES] = (kn * rk * gk[:, :LANES]).astype(BF16)
        k_ref[:, h * MLA_SLAB + LANES:(h + 1) * MLA_SLAB] = (kpe_rot * rk).astype(BF16)

    ones_col = jnp.where(lane == 0, 1.0, 0.0).astype(BF16)
    for h in range(MLA_HEADS):
        v_ref[:, h * MLA_SLAB:h * MLA_SLAB + LANES] = kv_all[:, (MLA_HEADS + h) * LANES:(MLA_HEADS + h + 1) * LANES
                                                             ].astype(BF16)
        v_ref[:, h * MLA_SLAB + LANES:(h + 1) * MLA_SLAB] = ones_col


def _mla_proj(z, tab, qag, kvag, wq, wkv, gq, gk, tm, cols, n_x_tiles, tab_tiles):
    r = z.shape[0]
    cqa, ckva, ckpe = cols
    lora = qag.shape[1]
    hs = MLA_HEADS * MLA_SLAB
    hv = MLA_HEADS * LANES
    return pl.pallas_call(
        _mla_proj_kernel,
        out_shape=(jax.ShapeDtypeStruct((r, hs), BF16), jax.ShapeDtypeStruct((r, hs), BF16),
                   jax.ShapeDtypeStruct((r, hs), BF16)),
        grid=(r // tm,),
        in_specs=[
            pl.BlockSpec((tm, lora), lambda i: (i, cqa // lora)),
            pl.BlockSpec((tm, lora), lambda i: (i, ckva // lora)),
            pl.BlockSpec((tm, LANES), lambda i: (i, ckpe // LANES)),
            pl.BlockSpec((tm, LANES), lambda i: (jnp.where(i < n_x_tiles, i % tab_tiles, tab_tiles), 0)),
            pl.BlockSpec((1, lora), lambda i: (0, 0)),
            pl.BlockSpec((1, lora), lambda i: (0, 0)),
            pl.BlockSpec((lora, hs), lambda i: (0, 0)),
            pl.BlockSpec((lora, 2 * hv), lambda i: (0, 0)),
            pl.BlockSpec((1, MLA_SLAB), lambda i: (0, 0)),
            pl.BlockSpec((1, MLA_SLAB), lambda i: (0, 0)),
        ],
        out_specs=(pl.BlockSpec((tm, hs), lambda i: (i, 0)), pl.BlockSpec((tm, hs), lambda i: (i, 0)),
                   pl.BlockSpec((tm, hs), lambda i: (i, 0))),
        compiler_params=_cparams(("parallel",)),
        name="mla_qkv",
    )(z, z, z, tab, qag, kvag, wq, wkv, gq, gk)


def _attn_kernel(n_kv, tq, q_ref, *refs):
    k_refs = refs[:n_kv]
    v_refs = refs[n_kv:2 * n_kv]
    o_ref, s0_ref, s1_ref, m0_ref, m1_ref = refs[2 * n_kv:]
    slots = ((s0_ref, m0_ref), (s1_ref, m1_ref))
    n_tiles = q_ref.shape[0] // tq
    chunks = []
    col = 0
    for kv, k_ref in enumerate(k_refs):
        n_keys = k_ref.shape[0]
        step = min(ATTN_KEY_CHUNK, n_keys)
        for off in range(0, n_keys, step):
            chunks.append((kv, off, col, step))
            col += step

    def scores(t, slot):
        s_ref, m_ref = slots[slot]
        rows = pl.ds(pl.multiple_of(t * tq, tq), tq)
        q = q_ref[rows, :]
        run = None
        for kv, off, c0, size in chunks:
            s = lax.dot_general(q, k_refs[kv][off:off + size, :], (((1,), (1,)), ((), ())),
                                preferred_element_type=F32)
            s_ref[:, c0:c0 + size] = s
            for lb in range(size // LANES):
                blk = s[:, lb * LANES:(lb + 1) * LANES]
                run = blk if run is None else jnp.maximum(run, blk)
        m_ref[...] = run

    def finish(t, slot):
        s_ref, m_ref = slots[slot]
        rows = pl.ds(pl.multiple_of(t * tq, tq), tq)
        m = jnp.max(m_ref[...], axis=-1, keepdims=True)
        acc = None
        for kv, off, c0, size in chunks:
            p = jnp.exp2(s_ref[:, c0:c0 + size] - m).astype(BF16)
            pv = jnp.dot(p, v_refs[kv][off:off + size, :], preferred_element_type=F32)
            acc = pv if acc is None else acc + pv
        o_ref[rows, :] = (acc[:, :LANES] / acc[:, LANES:LANES + 1]).astype(BF16)

    scores(0, 0)

    def body(k, carry):
        scores(2 * k + 1, 1)
        finish(2 * k, 0)
        scores(jnp.minimum(2 * k + 2, n_tiles - 1), 0)
        finish(2 * k + 1, 1)
        return carry

    lax.fori_loop(0, n_tiles // 2, body, 0)
    if n_tiles % 2:
        finish(n_tiles - 1, 0)


def _attention(qo, ko, vo, b, s, tc, tq, latent):
    n_x = b * s
    ctx0 = n_x // tc
    cspec = pl.BlockSpec((tc, MLA_SLAB), lambda i, h: (ctx0 + i, h))
    xspec = pl.BlockSpec((s, MLA_SLAB), lambda i, h: (i, h))
    if latent:
        n_q, n_keys = s, s + tc
        in_specs = [xspec, cspec, xspec, cspec, xspec]
        args = (qo, ko, ko, vo, vo)
    else:
        n_q, n_keys = tc, tc
        in_specs = [cspec, cspec, cspec]
        args = (qo, ko, vo)
    return pl.pallas_call(
        functools.partial(_attn_kernel, (len(args) - 1) // 2, tq),
        out_shape=jax.ShapeDtypeStruct((b * n_q, MLA_HEADS * LANES), BF16),
        grid=(b, MLA_HEADS),
        in_specs=in_specs,
        out_specs=pl.BlockSpec((n_q, LANES), lambda i, h: (i, h)),
        scratch_shapes=[pltpu.VMEM((tq, n_keys), F32), pltpu.VMEM((tq, n_keys), F32),
                        pltpu.VMEM((tq, LANES), F32), pltpu.VMEM((tq, LANES), F32)],
        compiler_params=_cparams(("parallel", "parallel")),
        name="attn_latent" if latent else "attn_ctx",
    )(*args)


S5_HALF = S5_BLOCK_GROUPS * S5_STATE


def _s5_kernel(n_batch, rows_x, rows_c, dot_rows, u_ref, fac_ref, pw_ref, tab_ref, y_ref,
               u2_ref, v_ref, r_ref, m_ref, ot_ref):
    d = pl.program_id(1)
    n_dot = u2_ref.shape[0] // dot_rows

    b_f = fac_ref[0, 0, 0]
    c_f = fac_ref[0, 0, 1]
    b_g = jnp.concatenate([-b_f[:, S5_HALF:], b_f[:, :S5_HALF]], axis=1)
    c_g = jnp.concatenate([c_f[:, S5_HALF:], -c_f[:, :S5_HALF]], axis=1)
    for s in range(S5_SUB):
        blk = slice(s * LANES, (s + 1) * LANES)
        r_ref[blk, :] = (b_f * pw_ref[0, 0, 0, s:s + 1, :] + b_g * pw_ref[0, 0, 1, s:s + 1, :]).astype(BF16)
        ot_ref[blk, :] = (c_f * pw_ref[0, 0, 2, s:s + 1, :] + c_g * pw_ref[0, 0, 3, s:s + 1, :]).astype(BF16)

    c_t = c_f.astype(BF16)

    def toeplitz(reverse):
        taps = {}
        for s in range(S5_SUB):
            z = s if reverse else S5_SUB - 1 - s
            taps[z] = _dot_nt(r_ref[s * LANES:(s + 1) * LANES, :], c_t).astype(BF16)
        zero = jnp.zeros((LANES, LANES), BF16)
        for s in range(S5_SUB):
            for t in range(S5_SUB):
                lag = (s - t) if reverse else (t - s)
                m_ref[s * LANES:(s + 1) * LANES, t * LANES:(t + 1) * LANES] = taps[lag] if lag >= 0 else zero

    @pl.when(d == 0)
    def _():
        toeplitz(False)

    @pl.when(d == 1)
    def _():
        toeplitz(True)

    def dot_rows_of(i):
        return pl.ds(pl.multiple_of(i * dot_rows, dot_rows), dot_rows)

    def token_rows_of(i, s):
        return pl.ds(i * (dot_rows * S5_SUB) + s, dot_rows, stride=S5_SUB)

    @pl.when(d == 0)
    def _():
        def stage(i, carry):
            rows = pl.ds(pl.multiple_of(i * dot_rows, dot_rows), dot_rows)
            y_ref[rows, :] = u_ref[rows, :].astype(F32)
            return carry

        lax.fori_loop(0, n_dot * S5_SUB, stage, 0)

        def regroup(i, carry):
            for s in range(S5_SUB):
                u2_ref[dot_rows_of(i), s * LANES:(s + 1) * LANES] = y_ref[token_rows_of(i, s), :].astype(BF16)
            return carry

        lax.fori_loop(0, n_dot, regroup, 0)

    def increments(i, carry):
        rows = dot_rows_of(i)
        v_ref[rows, :] = jnp.dot(u2_ref[rows, :], r_ref[...], preferred_element_type=F32)
        return carry

    lax.fori_loop(0, n_dot, increments, 0)

    def cmul_add(ar, ai, cr, ci, xr, xi):
        return ar + cr * xr - ci * xi, ai + cr * xi + ci * xr

    def run(reverse):
        tab = tab_ref.at[0, 0]
        last = 0 if reverse else SUBLANES - 1
        first_row = lax.broadcasted_iota(jnp.int32, (SUBLANES, S5_HALF), 0) == (SUBLANES - 1 - last)

        def segment(bases, n_groups, carry):
            def body(i, carry):
                gi = (n_groups - 1 - i) if reverse else i
                out = []
                for base, (cre, cim) in zip(bases, carry):
                    rows = pl.ds(pl.multiple_of(base + gi * SUBLANES, SUBLANES), SUBLANES)
                    re = v_ref[rows, :S5_HALF]
                    im = v_ref[rows, S5_HALF:]
                    for lvl, shift in enumerate((1, 2, 4)):
                        sh = (SUBLANES - shift) if reverse else shift
                        re, im = cmul_add(re, im, tab[2 * lvl], tab[2 * lvl + 1],
                                          pltpu.roll(re, sh, axis=0), pltpu.roll(im, sh, axis=0))
                    re, im = cmul_add(re, im, tab[6], tab[7], cre, cim)
                    sh1 = (SUBLANES - 1) if reverse else 1
                    v_ref[rows, :S5_HALF] = jnp.where(first_row, cre, pltpu.roll(re, sh1, axis=0))
                    v_ref[rows, S5_HALF:] = jnp.where(first_row, cim, pltpu.roll(im, sh1, axis=0))
                    out.append((jnp.broadcast_to(re[last:last + 1, :], re.shape),
                                jnp.broadcast_to(im[last:last + 1, :], im.shape)))
                return tuple(out)
            return lax.fori_loop(0, n_groups, body, carry)

        zero = jnp.zeros((SUBLANES, S5_HALF), F32)
        carry = tuple((zero, zero) for _ in range(n_batch))
        carry = segment([n_batch * rows_x + bi * rows_c for bi in range(n_batch)], rows_c // SUBLANES, carry)
        segment([bi * rows_x for bi in range(n_batch)], rows_x // SUBLANES, carry)

        def outputs(i, carry):
            rows = dot_rows_of(i)
            y = jnp.dot(u2_ref[rows, :], m_ref[...], preferred_element_type=F32)
            y = y + lax.dot_general(v_ref[rows, :].astype(BF16), ot_ref[...], (((1,), (1,)), ((), ())),
                                    preferred_element_type=F32)
            for s in range(S5_SUB):
                part = y[:, s * LANES:(s + 1) * LANES]
                if reverse:
                    y_ref[token_rows_of(i, s), :] += part
                else:
                    y_ref[token_rows_of(i, s), :] = part
            return carry

        lax.fori_loop(0, n_dot, outputs, 0)

    @pl.when(d == 0)
    def _():
        run(False)

    @pl.when(d == 1)
    def _():
        run(True)


def _s5_scan(z, fac, pw, tabs, layer, b, s, tc, col_u):
    r = z.shape[0]
    rc = r // S5_SUB
    n_blocks = fac.shape[2]
    width = S5_SUB * LANES
    assert width == 2 * S5_HALF
    dot_rows = max(n for n in range(16, 641, 16) if rc % n == 0)
    return pl.pallas_call(
        functools.partial(_s5_kernel, b, s // S5_SUB, tc // S5_SUB, dot_rows),
        out_shape=jax.ShapeDtypeStruct((r, n_blocks * LANES), F32),
        grid=(n_blocks, 2),
        in_specs=[
            pl.BlockSpec((r, LANES), lambda cb, d: (0, col_u // LANES + cb), pipeline_mode=pl.Buffered(1)),
            pl.BlockSpec((None, 1, 1, 2, LANES, width), lambda cb, d: (layer, d, cb, 0, 0, 0),
                         pipeline_mode=pl.Buffered(1)),
            pl.BlockSpec((None, 1, 1, 4, S5_SUB, width), lambda cb, d: (layer, d, cb, 0, 0, 0)),
            pl.BlockSpec((None, 1, 1, 8, SUBLANES, S5_HALF), lambda cb, d: (layer, d, cb, 0, 0, 0)),
        ],
        out_specs=pl.BlockSpec((r, LANES), lambda cb, d: (0, cb)),
        scratch_shapes=[pltpu.VMEM((rc, width), BF16), pltpu.VMEM((rc, width), F32),
                        pltpu.VMEM((width, width), BF16), pltpu.VMEM((width, width), BF16),
                        pltpu.VMEM((width, width), BF16)],
        compiler_params=_cparams(("parallel", "arbitrary")),
        name="s5_scan",
    )(z, fac, pw, tabs)


def _glu_kernel(y_ref, u_ref, d_ref, w_ref, b_ref, o_ref, g_ref):
    tm = y_ref.shape[0]

    def body(r, carry):
        rows = pl.ds(pl.multiple_of(r * ROW_CHUNK, ROW_CHUNK), ROW_CHUNK)
        y = y_ref[rows, :] + d_ref[...] * u_ref[rows, :].astype(F32)
        g_ref[rows, :] = jax.nn.gelu(y).astype(BF16)
        return carry

    lax.fori_loop(0, tm // ROW_CHUNK, body, 0)
    g = g_ref[...]
    gate = jax.nn.sigmoid(jnp.dot(g, w_ref[...], preferred_element_type=F32) + b_ref[...])
    o_ref[...] = (g.astype(F32) * gate).astype(BF16)


def _s5_glu(y, z, d_skip, w_glu, b_glu, tm, col_u, n_row_tiles):
    width = y.shape[1]
    return pl.pallas_call(
        _glu_kernel,
        out_shape=jax.ShapeDtypeStruct((n_row_tiles * tm, width), BF16),
        grid=(n_row_tiles,),
        in_specs=[
            pl.BlockSpec((tm, width), lambda i: (i, 0)),
            pl.BlockSpec((tm, width), lambda i: (i, col_u // width)),
            pl.BlockSpec((1, width), lambda i: (0, 0)),
            pl.BlockSpec((width, width), lambda i: (0, 0)),
            pl.BlockSpec((1, width), lambda i: (0, 0)),
        ],
        out_specs=pl.BlockSpec((tm, width), lambda i: (i, 0)),
        scratch_shapes=[pltpu.VMEM((tm, width), BF16)],
        compiler_params=_cparams(("parallel",)),
        name="s5_glu",
    )(y, z, d_skip, w_glu, b_glu)


def _merge_kernel(n_x_tiles, ax_ref, ac_ref, bx_ref, bc_ref, c_ref, ga_ref, gb_ref, gc_ref, w_ref, o_ref):
    def combine(a_ref, b_ref):
        acc = None
        for r, (br_ref, gate_ref) in enumerate(((a_ref, ga_ref), (b_ref, gb_ref), (c_ref, gc_ref))):
            proj = jnp.dot(br_ref[...], w_ref[r], preferred_element_type=F32)
            term = jax.nn.sigmoid(gate_ref[...].astype(F32)) * proj
            acc = term if acc is None else acc + term
        o_ref[...] = acc.astype(BF16)

    @pl.when(pl.program_id(0) < n_x_tiles)
    def _():
        combine(ax_ref, bx_ref)

    @pl.when(pl.program_id(0) >= n_x_tiles)
    def _():
        combine(ac_ref, bc_ref)


def _merge(a_x, a_c, b_x, b_c, cc, z, w_branch, tm, tn, col_g, n_row_tiles, n_x_tiles):
    width = a_x.shape[1]
    d = w_branch.shape[2]

    def gate_spec(r):
        return pl.BlockSpec((tm, tn), lambda i, j: (i, (col_g + r * d) // tn + j))

    ax_spec, ac_spec = _stream_specs((tm, width), n_x_tiles, 0, lambda j: 0)
    return pl.pallas_call(
        functools.partial(_merge_kernel, n_x_tiles),
        out_shape=jax.ShapeDtypeStruct((n_row_tiles * tm, d), BF16),
        grid=(n_row_tiles, d // tn),
        in_specs=[
            ax_spec, ac_spec, ax_spec, ac_spec,
            pl.BlockSpec((tm, width), lambda i, j: (i, 0)),
            gate_spec(0), gate_spec(1), gate_spec(2),
            pl.BlockSpec((N_BRANCH, width, tn), lambda i, j: (0, 0, j)),
        ],
        out_specs=pl.BlockSpec((tm, tn), lambda i, j: (i, j)),
        compiler_params=_cparams(("parallel", "arbitrary")),
        name="merge",
    )(a_x, a_c, b_x, b_c, cc, z, z, z, w_branch)


def _resid_kernel(n_x_tiles, m_ref, w_ref, x_ref, c_ref, al_ref, o_ref):
    resid = jnp.where(pl.program_id(0) < n_x_tiles, x_ref[...], c_ref[...])
    o_ref[...] = resid + al_ref[0] * jnp.dot(m_ref[...], w_ref[...], preferred_element_type=F32)


def _out_proj_residual(m, w, x_src, c_src, ctx_tile0, mod3, tm, tn, n_row_tiles, n_x_tiles, tiles_per_batch, ctx_row,
                       k_alpha):
    kdim, d = w.shape
    nt = d // tn

    def alpha_map(i, j):
        r = jnp.where(i < n_x_tiles, i // tiles_per_batch, ctx_row)
        return (r * N_MOD + k_alpha, 0, j)

    x_spec = pl.BlockSpec((tm, tn), lambda i, j: (jnp.minimum(i, n_x_tiles - 1), jnp.where(i < n_x_tiles, j, nt - 1)))
    c_spec = pl.BlockSpec((tm, tn), lambda i, j: (ctx_tile0 + jnp.maximum(i - n_x_tiles, 0),
                                                  jnp.where(i < n_x_tiles, 0, j)))
    return pl.pallas_call(
        functools.partial(_resid_kernel, n_x_tiles),
        out_shape=jax.ShapeDtypeStruct((n_row_tiles * tm, d), F32),
        grid=(n_row_tiles, nt),
        in_specs=[
            pl.BlockSpec((tm, kdim), lambda i, j: (i, 0)),
            pl.BlockSpec((kdim, tn), lambda i, j: (0, j)),
            x_spec, c_spec,
            pl.BlockSpec((1, 1, tn), alpha_map),
        ],
        out_specs=pl.BlockSpec((tm, tn), lambda i, j: (i, j)),
        compiler_params=_cparams(("parallel", "arbitrary")),
        name="out_proj",
    )(m, w, x_src, c_src, mod3)


def _ff1_kernel(x_ref, g_ref, sh_ref, sc_ref, w_ref, h_ref, xn_ref):
    @pl.when(pl.program_id(1) == 0)
    def _():
        _norm_mod_rows(x_ref, g_ref, sh_ref, sc_ref, xn_ref)

    a = jnp.maximum(jnp.dot(xn_ref[...], w_ref[...], preferred_element_type=F32), 0.0)
    h_ref[...] = (a * a).astype(BF16)


def _ff1(xs, g, mod3, w, tm, tn, n_row_tiles, n_x_tiles, tiles_per_batch, ctx_row):
    d, dff = w.shape
    return pl.pallas_call(
        _ff1_kernel,
        out_shape=jax.ShapeDtypeStruct((n_row_tiles * tm, dff), BF16),
        grid=(n_row_tiles, dff // tn),
        in_specs=[
            pl.BlockSpec((tm, d), lambda i, j: (i, 0)),
            pl.BlockSpec((1, d), lambda i, j: (0, 0)),
            pl.BlockSpec((1, 1, d), _mod_row_map(n_x_tiles, tiles_per_batch, ctx_row, 3)),
            pl.BlockSpec((1, 1, d), _mod_row_map(n_x_tiles, tiles_per_batch, ctx_row, 4)),
            pl.BlockSpec((d, tn), lambda i, j: (0, j)),
        ],
        out_specs=pl.BlockSpec((tm, tn), lambda i, j: (i, j)),
        scratch_shapes=[pltpu.VMEM((tm, d), BF16)],
        compiler_params=_cparams(("parallel", "arbitrary")),
        name="ff1",
    )(xs, g, mod3, mod3, w)


def _ff2_kernel(h_ref, w_ref, x_ref, al_ref, o_ref):
    k = pl.program_id(2)
    part = jnp.dot(h_ref[...], w_ref[...], preferred_element_type=F32)

    @pl.when(k == 0)
    def _():
        o_ref[...] = part

    @pl.when(k > 0)
    def _():
        o_ref[...] += part

    @pl.when(k == pl.num_programs(2) - 1)
    def _():
        o_ref[...] = x_ref[...] + al_ref[0] * o_ref[...]


def _ff2(h, w, xs, mod3, tm, tn, tk, n_row_tiles, n_x_tiles, tiles_per_batch, ctx_row):
    dff, d = w.shape

    def alpha_map(i, j, k):
        r = jnp.where(i < n_x_tiles, i // tiles_per_batch, ctx_row)
        return (r * N_MOD + 5, 0, j)

    return pl.pallas_call(
        _ff2_kernel,
        out_shape=jax.ShapeDtypeStruct((n_row_tiles * tm, d), F32),
        grid=(n_row_tiles, d // tn, dff // tk),
        in_specs=[
            pl.BlockSpec((tm, tk), lambda i, j, k: (i, k)),
            pl.BlockSpec((tk, tn), lambda i, j, k: (k, j)),
            pl.BlockSpec((tm, tn), lambda i, j, k: (i, j)),
            pl.BlockSpec((1, 1, tn), alpha_map),
        ],
        out_specs=pl.BlockSpec((tm, tn), lambda i, j, k: (i, j)),
        compiler_params=_cparams(("parallel", "parallel", "arbitrary")),
        name="ff2",
    )(h, w, xs, mod3)


def _rope_partner():
    j = np.arange(MLA_ROPE)
    quarter = MLA_ROPE // 4
    return np.where((j // quarter) % 2 == 0, j + quarter, j - quarter)


def _rope_table(s, tm):
    pos = jnp.arange(s)
    row = (pos // GRID_W).astype(F32)
    col = (pos % GRID_W).astype(F32)
    n_freq = MLA_ROPE // 4
    inv_freq = ROPE_THETA ** (-jnp.arange(n_freq, dtype=F32) / n_freq)
    ang_r = row[:, None] * inv_freq
    ang_c = col[:, None] * inv_freq
    cos = jnp.concatenate([jnp.cos(ang_r)] * 2 + [jnp.cos(ang_c)] * 2, axis=-1)
    sin = jnp.concatenate([-jnp.sin(ang_r), jnp.sin(ang_r), -jnp.sin(ang_c), jnp.sin(ang_c)], axis=-1)
    ident = jnp.concatenate([jnp.ones((tm, MLA_ROPE), F32), jnp.zeros((tm, MLA_ROPE), F32)], axis=-1)
    return jnp.concatenate([jnp.concatenate([cos, sin], axis=-1), ident], axis=0)


def _cast_kernel(w_ref, o_ref):
    o_ref[...] = w_ref[...].astype(BF16)


def _layer_bf16(w_all, layer):
    rows, cols = w_all.shape[1:]
    tr = CAST_BLOCK_BYTES // (4 * cols)
    while rows % tr:
        tr //= 2
    return pl.pallas_call(
        _cast_kernel,
        out_shape=jax.ShapeDtypeStruct((rows, cols), BF16),
        grid=(rows // tr,),
        in_specs=[pl.BlockSpec((None, tr, cols), lambda i: (layer, i, 0))],
        out_specs=pl.BlockSpec((tr, cols), lambda i: (i, 0)),
        compiler_params=_cparams(("parallel",)),
        name="cast_bf16",
    )(w_all)


def _pack_w_in(w_in, b_in, gate_b, sizes, n_pad):
    w_t = w_in.T
    d_model = w_in.shape[0]
    bounds = np.cumsum((0,) + sizes)
    seg = [slice(int(bounds[i]), int(bounds[i + 1])) for i in range(len(sizes))]
    partner = _rope_partner()
    order = (0, 1, 2, 3, 5, 6, 8, 9)
    names = ("q", "k", "v", "o", "qa", "kva", "u", "gates")
    offs, pos = {}, 0
    for name, i in zip(names, order):
        offs[name] = pos
        pos += sizes[i]
    offs["kpe"], offs["kpe_sw"] = pos, pos + MLA_ROPE
    n_zero = n_pad - pos - 2 * MLA_ROPE
    kpe_w, kpe_b = w_t[seg[7]], b_in[seg[7]]
    w = jnp.concatenate([w_t[seg[i]] for i in order] + [kpe_w, kpe_w[partner], jnp.zeros((n_zero, d_model), F32)],
                        axis=0).astype(BF16)
    b = jnp.concatenate([b_in[seg[i]] for i in order] + [kpe_b, kpe_b[partner], jnp.zeros((n_zero,), F32)])
    n_g = sizes[4]
    wg = jnp.concatenate([w_t[seg[4]], jnp.zeros((LANES - n_g, d_model), F32)], axis=0).astype(BF16)
    bg = jnp.concatenate([b_in[seg[4]] + gate_b.reshape(-1), jnp.zeros((LANES - n_g,), F32)])[None, :]
    return w, b[None, :], wg, bg, offs


def _pack_mla(w_uq, w_ukv, qn_g, kn_g):
    partner = _rope_partner()
    lora = w_uq.shape[0]
    wq = w_uq.reshape(lora, MLA_HEADS, MLA_DQK)
    wq = jnp.concatenate([wq, wq[:, :, MLA_NOPE + partner]], axis=-1).reshape(lora, MLA_HEADS * MLA_SLAB)
    wkv = w_ukv.reshape(w_ukv.shape[0], MLA_HEADS, -1)
    wkv = jnp.concatenate([wkv[:, :, :MLA_NOPE].reshape(lora, -1), wkv[:, :, MLA_NOPE:].reshape(lora, -1)], axis=-1)

    def gains(g):
        return jnp.concatenate([g, g[MLA_NOPE + partner]])[None, :]

    return wq.astype(BF16), wkv.astype(BF16), gains(qn_g), gains(kn_g)


def _pack_s5(a_re, a_im, log_dt, b_re, b_im, c_re, c_im):
    n_dir, n_groups, n_state = a_re.shape
    gc = b_re.shape[-1]
    nb = n_groups // S5_BLOCK_GROUPS
    lam_re = jnp.minimum(a_re.astype(F32), -1e-4)
    lam_im = a_im.astype(F32)
    dt = jnp.exp(log_dt.astype(F32))[..., None]

    def pole_power(k):
        mag = jnp.exp(k * lam_re * dt)
        return mag * jnp.cos(k * lam_im * dt), mag * jnp.sin(k * lam_im * dt)

    bar_re, bar_im = pole_power(1.0)
    den = lam_re * lam_re + lam_im * lam_im
    f_re = ((bar_re - 1.0) * lam_re + bar_im * lam_im) / den
    f_im = (bar_im * lam_re - (bar_re - 1.0) * lam_im) / den
    bb_re = f_re[..., None] * b_re.astype(F32) - f_im[..., None] * b_im.astype(F32)
    bb_im = f_re[..., None] * b_im.astype(F32) + f_im[..., None] * b_re.astype(F32)
    eye = jnp.eye(S5_BLOCK_GROUPS, dtype=F32)
    sub = S5_SUB

    def per_block(a):
        return a.reshape(a.shape[:-2] + (nb, S5_BLOCK_GROUPS * n_state))

    def block_b(part):
        p = part.reshape(n_dir, nb, S5_BLOCK_GROUPS, n_state, gc)
        m = jnp.einsum('dbgnc,gh->dbgchn', p, eye, precision=HIGHEST)
        return m.reshape(n_dir, nb, LANES, S5_HALF)

    def block_c(part):
        p = part.astype(F32).reshape(n_dir, nb, S5_BLOCK_GROUPS, gc, n_state)
        m = jnp.einsum('dbgcn,gh->dbgnhc', p, eye, precision=HIGHEST)
        return m.reshape(n_dir, nb, S5_HALF, LANES)

    bm_re, bm_im = block_b(bb_re), block_b(bb_im)
    cm_re, cm_im = block_c(c_re), block_c(c_im)

    tau = jnp.arange(sub + 1, dtype=F32)[:, None, None, None]
    p_re, p_im = (per_block(p) for p in pole_power(tau))
    ct_re, ct_im = jnp.swapaxes(cm_re, -1, -2), jnp.swapaxes(cm_im, -1, -2)
    cat = functools.partial(jnp.concatenate, axis=-1)
    fac = jnp.stack([cat([bm_re, bm_im]), cat([ct_re, -ct_im])], axis=2)
    t = np.arange(sub)
    pws = []
    for d in range(n_dir):
        to_exit = (sub - 1 - t) if d == 0 else t
        age = (t + 1) if d == 0 else (sub - t)
        rows = [p_re[to_exit, d], p_im[to_exit, d], p_re[age, d], p_im[age, d]]
        pws.append(jnp.stack([cat([a, a]).transpose(1, 0, 2) for a in rows], axis=1))
    pw = jnp.stack(pws)

    rows = jnp.arange(SUBLANES)
    tabs = []
    for d in range(n_dir):
        per_dir = []
        for shift in (1, 2, 4):
            keep = ((rows <= SUBLANES - 1 - shift) if d == 1 else (rows >= shift))[None, :, None]
            s_re, s_im = pole_power(float(shift * sub))
            per_dir += [jnp.where(keep, per_block(s_re[d])[:, None, :], 0.0),
                        jnp.where(keep, per_block(s_im[d])[:, None, :], 0.0)]
        expo = (((SUBLANES - rows) if d == 1 else (rows + 1)) * sub).astype(F32)
        s_re, s_im = pole_power(expo[:, None, None, None])
        per_dir += [jnp.moveaxis(per_block(s_re[:, d]), 0, 1), jnp.moveaxis(per_block(s_im[:, d]), 0, 1)]
        tabs.append(jnp.stack(per_dir, axis=1))
    return fac, pw, jnp.stack(tabs).astype(F32)


def _tri_matrices():
    t = np.arange(ML_CHUNK)
    lower = t[None, :] <= t[:, None]
    mask = np.where(np.stack([lower, lower.T]), 0.0, NEG_BIG).astype(np.float32)
    return jnp.asarray(lower.astype(np.float32), dtype=BF16), jnp.asarray(mask)


def kernel(x, c, ctx, c_ctx, w_mod, b_mod, norm_g, w_in, b_in, ml_gate_b, ml_norm_g, mla_qa_g, mla_kva_g, mla_w_uq, mla_w_ukv, mla_qn_g, mla_kn_g, s5_a_re, s5_a_im, s5_log_dt, s5_b_re, s5_b_im, s5_c_re, s5_c_im, s5_d, s5_w_glu, s5_b_glu, w_branch, w_out, w_ff1, w_ff2):
    b, s, d = x.shape
    tc = ctx.shape[1]
    depth = w_mod.shape[0]
    dv = ml_norm_g.shape[2]
    dk = dv // 2
    lora = mla_qa_g.shape[1]
    s5_width = s5_d.shape[1]
    branch_w = w_branch.shape[2]
    sizes = (ML_HEADS * dk, ML_HEADS * dk, ML_HEADS * dv, ML_HEADS * dv, 4 * ML_HEADS, lora, lora, MLA_ROPE,
             s5_width, N_BRANCH * d)
    assert sum(sizes) == w_in.shape[2] and b + 1 <= SUBLANES
    assert s % ML_CHUNK == 0 and tc % ML_CHUNK == 0 and branch_w == ML_HEADS * dv == MLA_HEADS * LANES == s5_width

    n_x = b * s
    n_c = b * tc
    tm = _row_tile(s, n_c)
    n_x_tiles = n_x // tm
    n_tiles = n_x_tiles + n_c // tm
    tiles_per_batch = s // tm
    tile_args = (n_x_tiles, tiles_per_batch, b)

    x_src, c_src, ctx_tile0 = x.reshape(n_x, d), ctx.reshape(n_c, d), 0
    cc = jnp.concatenate([c, c_ctx[None, :], jnp.zeros((SUBLANES - b - 1, d), F32)], axis=0)
    mod = _modulation(cc, w_mod, b_mod)
    tm_q = min(tm, 512)
    tab = _rope_table(s, tm_q)
    tri, ml_mask = _tri_matrices()
    s5_packed = jax.vmap(_pack_s5)(s5_a_re, s5_a_im, s5_log_dt, s5_b_re, s5_b_im, s5_c_re, s5_c_im)
    n_used = sum(sizes) - sizes[4] + MLA_ROPE
    tn_in = 1280
    n_pad = -(-n_used // tn_in) * tn_in

    for l in range(depth):
        with_ctx_out = l < depth - 1
        mod3 = mod[l].reshape(SUBLANES * N_MOD, 1, d)
        w_p, b_p, wg, bg, offs = _pack_w_in(w_in[l], b_in[l], ml_gate_b[l], sizes, n_pad)
        z, gz = _in_proj(x_src, c_src, ctx_tile0, n_tiles, norm_g[l, 0][None, :], mod3, w_p, b_p, wg, bg, tm, tn_in,
                         *tile_args)

        gates, gates_t = _gate_prep(gz, tri)
        a_x, a_c = _mlstm(z, gates, gates_t, ml_mask, ml_norm_g[l], b, s, tc, dk, dv,
                          (offs["q"], offs["k"], offs["v"], offs["o"]), with_ctx_out)

        wq, wkv, gq, gk = _pack_mla(mla_w_uq[l], mla_w_ukv[l], mla_qn_g[l], mla_kn_g[l])
        qo, ko, vo = _mla_proj(z, tab, mla_qa_g[l][None, :], mla_kva_g[l][None, :], wq, wkv, gq, gk, tm_q,
                               (offs["qa"], offs["kva"], offs["kpe"]), n_x // tm_q, s // tm_q)
        tq = min(512, s)
        b_x = _attention(qo, ko, vo, b, s, tc, tq, True)

        y = _s5_scan(z, *s5_packed, l, b, s, tc, offs["u"])
        n_out_tiles = n_tiles if with_ctx_out else n_x_tiles
        c_all = _s5_glu(y, z, s5_d[l][None, :], s5_w_glu[l].astype(BF16), s5_b_glu[l][None, :], tm, offs["u"],
                        n_out_tiles)

        if with_ctx_out:
            b_c = _attention(qo, ko, vo, b, s, tc, min(tq, tc), False)
        else:
            a_c, b_c = a_x, b_x
        w_br = _layer_bf16(w_branch.reshape(depth, N_BRANCH * branch_w, d), l).reshape(N_BRANCH, branch_w, d)
        merged = _merge(a_x, a_c, b_x, b_c, c_all, z, w_br, tm, 512, offs["gates"],
                        n_out_tiles, n_x_tiles)
        xs1 = _out_proj_residual(merged, _layer_bf16(w_out, l), x_src, c_src, ctx_tile0, mod3, tm, 1024,
                                 n_out_tiles, *tile_args, 2)
        hid = _ff1(xs1, norm_g[l, 1][None, :], mod3, _layer_bf16(w_ff1, l), tm, 1024, n_out_tiles, *tile_args)
        xs = _ff2(hid, _layer_bf16(w_ff2, l), xs1, mod3, tm, 1024, 2048, n_out_tiles, *tile_args)
        x_src, c_src, ctx_tile0 = xs, xs, n_x_tiles

    return xs.reshape(b, s, d)
```
